```python
import jax
import jax.numpy as jnp
from jax import lax
import numpy as np

D_MODEL = 1024
BATCH = 8
SEQ = 4096
DEPTH = 1

NSA_HEADS = 8
NSA_KV_GROUPS = 2
NSA_HEAD_DIM = 64
NSA_CMP_STRIDE = 16
NSA_CMP_LEN = 2 * NSA_CMP_STRIDE
NSA_SLC_LEN = 64
NSA_TOPN = 16
NSA_WINDOW = 512
NSA_CMP_HIDDEN = 2 * NSA_HEAD_DIM
NSA_Q_BLOCK = 64
NSA_FORCE_BONUS = 1e4

MLA_HEADS = 8
MLA_Q_RANK = 384
MLA_KV_RANK = 256
MLA_NOPE_DIM = 64
MLA_ROPE_DIM = 32
MLA_V_DIM = 64
MLA_Q_BLOCK = 128

MIX_WIDTH = NSA_HEADS * NSA_HEAD_DIM + MLA_HEADS * MLA_V_DIM

MEM_TOKENS = 256
MEM_HEADS = 4
MEM_HEAD_DIM = D_MODEL // MEM_HEADS

D_FF = 2816
CONV_WIDTH = 3

ROPE_THETA = 10000.0
LN_EPS = 1e-5
RMS_EPS = 1e-6
NEG_INF = -1e30
DEEPNORM_ALPHA = (2.0 * DEPTH) ** 0.25
DEEPNORM_BETA = (8.0 * DEPTH) ** -0.25

NSA_Q_COLS = NSA_HEADS * NSA_HEAD_DIM
NSA_KV_COLS = 3 * 2 * NSA_KV_GROUPS * NSA_HEAD_DIM
NSA_GATE_COLS = 3 * NSA_HEADS
_C1 = NSA_Q_COLS
_C2 = _C1 + NSA_KV_COLS
_C3 = _C2 + NSA_GATE_COLS
_C4 = _C3 + MLA_Q_RANK
_C5 = _C4 + MLA_KV_RANK
IN_COLS = _C5 + MLA_ROPE_DIM
IN_SPLITS = (_C1, _C2, _C3, _C4, _C5)

kernel_name = "hymba_nsa_mla_deepnorm_convffn"


def layer_norm(x, g, b):
    xf = x.astype(jnp.float32)
    mu = jnp.mean(xf, -1, keepdims=True)
    var = jnp.mean(jnp.square(xf - mu), -1, keepdims=True)
    return ((xf - mu) * lax.rsqrt(var + LN_EPS) * g.astype(jnp.float32) + b.astype(jnp.float32)).astype(x.dtype)


def rms_norm(x, g):
    xf = x.astype(jnp.float32)
    return (xf * lax.rsqrt(jnp.mean(xf * xf, -1, keepdims=True) + RMS_EPS) * g.astype(jnp.float32)).astype(x.dtype)


def rope(x, pos):
    d = x.shape[-1]
    inv = ROPE_THETA ** (-jnp.arange(0, d, 2, dtype=jnp.float32) / d)
    ang = pos.astype(jnp.float32)[:, :, None, None] * inv
    cos, sin = jnp.cos(ang), jnp.sin(ang)
    x1, x2 = jnp.split(x.astype(jnp.float32), 2, axis=-1)
    return jnp.concatenate([x1 * cos - x2 * sin, x2 * cos + x1 * sin], -1).astype(x.dtype)


def masked_softmax(s, mask, axis=-1):
    s = jnp.where(mask, s.astype(jnp.float32), NEG_INF)
    return jax.nn.softmax(s, axis=axis) * mask


def compress(kv, pos_emb, w1, b1, w2, b2):
    B, S, G, dh = kv.shape
    ch = kv.reshape(B, S // NSA_CMP_STRIDE, NSA_CMP_STRIDE, G, dh)
    blocks = jnp.concatenate([ch[:, :-1], ch[:, 1:]], axis=2)
    blocks = blocks + pos_emb[None, None, :, None, :]
    nc = blocks.shape[1]
    flat = blocks.transpose(0, 1, 3, 2, 4).reshape(B, nc, G, NSA_CMP_LEN * dh)
    return jax.nn.gelu(flat @ w1 + b1) @ w2 + b2


def nsa_attention(q, k_cmp, v_cmp, k_slc, v_slc, k_win, v_win, gates):
    B, S, H, dh = q.shape
    G = NSA_KV_GROUPS
    R = H // G
    NC = k_cmp.shape[1]
    NS = S // NSA_SLC_LEN
    TOPN = min(NSA_TOPN, NS)
    QB = NSA_Q_BLOCK
    scale = dh ** -0.5
    qg = q.reshape(B, S, G, R, dh)
    cmp_start = jnp.arange(NC)[:, None] * NSA_CMP_STRIDE
    cmp_end = jnp.arange(NC) * NSA_CMP_STRIDE + NSA_CMP_LEN - 1
    slc_start = jnp.arange(NS)[None, :] * NSA_SLC_LEN
    cover = jnp.clip(jnp.minimum(cmp_start + NSA_CMP_LEN, slc_start + NSA_SLC_LEN)
                     - jnp.maximum(cmp_start, slc_start), 0, None).astype(jnp.float32) / NSA_CMP_LEN
    kb = k_slc.reshape(B, NS, NSA_SLC_LEN, G, dh).transpose(0, 3, 1, 2, 4)
    vb = v_slc.reshape(B, NS, NSA_SLC_LEN, G, dh).transpose(0, 3, 1, 2, 4)
    pad = ((0, 0), (NSA_WINDOW, 0), (0, 0), (0, 0))
    kw = jnp.pad(k_win, pad)
    vw = jnp.pad(v_win, pad)
    bi = jnp.arange(B)[:, None, None, None]
    gi = jnp.arange(G)[None, :, None, None]
    blk = jnp.arange(NS)
    in_blk = jnp.arange(NSA_SLC_LEN)
    span = jnp.arange(NSA_WINDOW + QB)

    def query_block(c):
        t0 = c * QB
        t = t0 + jnp.arange(QB)
        qc = lax.dynamic_slice_in_dim(qg, t0, QB, axis=1)
        s_c = jnp.einsum('bqgrd,bngd->bgrqn', qc, k_cmp) * scale
        p_c = masked_softmax(s_c, cmp_end[None, :] <= t[:, None])
        o_c = jnp.einsum('bgrqn,bngd->bqgrd', p_c.astype(v_cmp.dtype), v_cmp)
        imp = jnp.einsum('bgrqn,nj->bgqj', p_c, cover)
        cur = (t // NSA_SLC_LEN)[:, None]
        forced = (blk == 0) | (blk == cur) | (blk == cur - 1)
        score = jnp.where(blk <= cur, jnp.where(forced, NSA_FORCE_BONUS, imp), NEG_INF)
        _, idx = lax.top_k(score, TOPN)
        ks = kb[bi, gi, idx]
        vs = vb[bi, gi, idx]
        kpos = idx[..., None] * NSA_SLC_LEN + in_blk
        m_s = (kpos <= t[None, None, :, None, None])[:, :, None]
        s_s = jnp.einsum('bqgrd,bgqnld->bgrqnl', qc, ks) * scale
        p_s = masked_softmax(s_s, m_s, axis=(-2, -1))
        o_s = jnp.einsum('bgrqnl,bgqnld->bqgrd', p_s.astype(vs.dtype), vs)
        kwc = lax.dynamic_slice_in_dim(kw, t0, NSA_WINDOW + QB, axis=1)
        vwc = lax.dynamic_slice_in_dim(vw, t0, NSA_WINDOW + QB, axis=1)
        spos = t0 - NSA_WINDOW + span
        diff = t[:, None] - spos[None, :]
        m_w = (spos[None, :] >= 0) & (diff >= 0) & (diff < NSA_WINDOW)
        s_w = jnp.einsum('bqgrd,bkgd->bgrqk', qc, kwc) * scale
        p_w = masked_softmax(s_w, m_w)
        o_w = jnp.einsum('bgrqk,bkgd->bqgrd', p_w.astype(vwc.dtype), vwc)
        gc = lax.dynamic_slice_in_dim(gates, t0, QB, axis=1)
        return gc[..., 0:1] * o_c + gc[..., 1:2] * o_s + gc[..., 2:3] * o_w

    out = lax.map(query_block, jnp.arange(S // QB))
    return out.transpose(1, 0, 2, 3, 4, 5).reshape(B, S, H * dh)


def causal_attention_blocks(q, k, v):
    B, S, H, _ = q.shape
    dv = v.shape[-1]
    scale = q.shape[-1] ** -0.5
    QB = MLA_Q_BLOCK
    kpos = jnp.arange(S)

    def query_block(c):
        t0 = c * QB
        qc = lax.dynamic_slice_in_dim(q, t0, QB, axis=1)
        s = jnp.einsum('bqhd,bkhd->bhqk', qc, k) * scale
        p = masked_softmax(s, kpos[None, :] <= (t0 + jnp.arange(QB))[:, None])
        return jnp.einsum('bhqk,bkhd->bqhd', p.astype(v.dtype), v)

    out = lax.map(query_block, jnp.arange(S // QB))
    return out.transpose(1, 0, 2, 3, 4).reshape(B, S, H * dv)


def memory_cross_attention(x, mem, wq, wk, wv, wo):
    B, S, _ = x.shape
    M = mem.shape[1]
    q = (x @ wq).reshape(B, S, MEM_HEADS, MEM_HEAD_DIM)
    k = (mem @ wk).reshape(B, M, MEM_HEADS, MEM_HEAD_DIM)
    v = (mem @ wv).reshape(B, M, MEM_HEADS, MEM_HEAD_DIM)
    s = jnp.einsum('bqhd,bkhd->bhqk', q, k).astype(jnp.float32) * (MEM_HEAD_DIM ** -0.5)
    p = jax.nn.softmax(s, axis=-1)
    o = jnp.einsum('bhqk,bkhd->bqhd', p.astype(v.dtype), v).reshape(B, S, D_MODEL)
    return o @ wo


def conv_ffn(x, w_up, conv_w, conv_b, w_down):
    gate, up = jnp.split(x @ w_up, 2, axis=-1)
    gate = lax.conv_general_dilated(
        gate, conv_w[:, None, :], window_strides=(1,), padding=[(CONV_WIDTH - 1, 0)],
        dimension_numbers=('NWC', 'WIO', 'NWC'), feature_group_count=D_FF) + conv_b
    return (jax.nn.silu(gate) * up) @ w_down


def _w(k, shape, fan_in, scale=1.0):
    return jax.random.normal(k, shape, jnp.float32) * (scale * fan_in ** -0.5)


def _gain(k, shape):
    return 1.0 + 0.01 * jax.random.normal(k, shape, jnp.float32)


def _small(k, shape):
    return 0.01 * jax.random.normal(k, shape, jnp.float32)


def setup_inputs(seed: int = 0) -> dict:
    key = jax.random.key(seed)
    ks = jax.random.split(key, 40)
    L = DEPTH
    dh = NSA_HEAD_DIM
    flat = NSA_CMP_LEN * dh
    offset = jax.random.randint(ks[2], (BATCH, 1), 0, 1024, dtype=jnp.int32)
    positions = (offset + jnp.arange(SEQ, dtype=jnp.int32)[None, :]).astype(jnp.int32)
    return {
        "x": jax.random.normal(ks[0], (BATCH, SEQ, D_MODEL), jnp.float32),
        "mem": jax.random.normal(ks[1], (BATCH, MEM_TOKENS, D_MODEL), jnp.float32),
        "positions": positions,
        "w_in": _w(ks[3], (L, D_MODEL, IN_COLS), D_MODEL),
        "nsa_k_pos": 0.02 * jax.random.normal(ks[4], (L, NSA_CMP_LEN, dh), jnp.float32),
        "nsa_ck_w1": _w(ks[5], (L, flat, NSA_CMP_HIDDEN), flat),
        "nsa_ck_b1": _small(ks[6], (L, NSA_CMP_HIDDEN)),
        "nsa_ck_w2": _w(ks[7], (L, NSA_CMP_HIDDEN, dh), NSA_CMP_HIDDEN),
        "nsa_ck_b2": _small(ks[8], (L, dh)),
        "nsa_v_pos": 0.02 * jax.random.normal(ks[9], (L, NSA_CMP_LEN, dh), jnp.float32),
        "nsa_cv_w1": _w(ks[10], (L, flat, NSA_CMP_HIDDEN), flat),
        "nsa_cv_b1": _small(ks[11], (L, NSA_CMP_HIDDEN)),
        "nsa_cv_w2": _w(ks[12], (L, NSA_CMP_HIDDEN, dh), NSA_CMP_HIDDEN),
        "nsa_cv_b2": _small(ks[13], (L, dh)),
        "mla_q_norm": _gain(ks[14], (L, MLA_Q_RANK)),
        "mla_w_uq": _w(ks[15], (L, MLA_Q_RANK, MLA_HEADS * (MLA_NOPE_DIM + MLA_ROPE_DIM)), MLA_Q_RANK),
        "mla_kv_norm": _gain(ks[16], (L, MLA_KV_RANK)),
        "mla_w_ukv": _w(ks[17], (L, MLA_KV_RANK, MLA_HEADS * (MLA_NOPE_DIM + MLA_V_DIM)), MLA_KV_RANK),
        "w_o": _w(ks[18], (L, MIX_WIDTH, D_MODEL), MIX_WIDTH, DEEPNORM_BETA),
        "ln1_g": _gain(ks[19], (L, D_MODEL)),
        "ln1_b": _small(ks[20], (L, D_MODEL)),
        "mem_wq": _w(ks[21], (L, D_MODEL, D_MODEL), D_MODEL),
        "mem_wk": _w(ks[22], (L, D_MODEL, D_MODEL), D_MODEL),
        "mem_wv": _w(ks[23], (L, D_MODEL, D_MODEL), D_MODEL),
        "mem_wo": _w(ks[24], (L, D_MODEL, D_MODEL), D_MODEL, DEEPNORM_BETA),
        "ln2_g": _gain(ks[25], (L, D_MODEL)),
        "ln2_b": _small(ks[26], (L, D_MODEL)),
        "ffn_w_up": _w(ks[27], (L, D_MODEL, 2 * D_FF), D_MODEL),
        "ffn_conv_w": _w(ks[28], (L, CONV_WIDTH, D_FF), CONV_WIDTH),
        "ffn_conv_b": _small(ks[29], (L, D_FF)),
        "ffn_w_down": _w(ks[30], (L, D_FF, D_MODEL), D_FF, DEEPNORM_BETA),
        "ln3_g": _gain(ks[31], (L, D_MODEL)),
        "ln3_b": _small(ks[32], (L, D_MODEL)),
    }


def reference(x, mem, positions, w_in, nsa_k_pos, nsa_ck_w1, nsa_ck_b1, nsa_ck_w2, nsa_ck_b2,
              nsa_v_pos, nsa_cv_w1, nsa_cv_b1, nsa_cv_w2, nsa_cv_b2,
              mla_q_norm, mla_w_uq, mla_kv_norm, mla_w_ukv, w_o, ln1_g, ln1_b,
              mem_wq, mem_wk, mem_wv, mem_wo, ln2_g, ln2_b,
              ffn_w_up, ffn_conv_w, ffn_conv_b, ffn_w_down, ln3_g, ln3_b):
    B, S, _ = x.shape
    G = NSA_KV_GROUPS
    R = NSA_HEADS // NSA_KV_GROUPS
    dh = NSA_HEAD_DIM
    pos_cmp = positions[:, NSA_CMP_LEN - 1::NSA_CMP_STRIDE]
    for l in range(DEPTH):
        h = x @ w_in[l]
        nq, nkv, ng, mq, mkv, mkr = jnp.split(h, IN_SPLITS, axis=-1)
        q_n = rope(nq.reshape(B, S, NSA_HEADS, dh), positions)
        kv = nkv.reshape(B, S, 3, 2, G, dh)
        k_cmp = rope(compress(kv[:, :, 0, 0], nsa_k_pos[l], nsa_ck_w1[l], nsa_ck_b1[l], nsa_ck_w2[l], nsa_ck_b2[l]), pos_cmp)
        v_cmp = compress(kv[:, :, 0, 1], nsa_v_pos[l], nsa_cv_w1[l], nsa_cv_b1[l], nsa_cv_w2[l], nsa_cv_b2[l])
        k_slc = rope(kv[:, :, 1, 0], positions)
        k_win = rope(kv[:, :, 2, 0], positions)
        gates = jax.nn.sigmoid(ng.astype(jnp.float32)).astype(x.dtype).reshape(B, S, G, R, 3)
        o_nsa = nsa_attention(q_n, k_cmp, v_cmp, k_slc, kv[:, :, 1, 1], k_win, kv[:, :, 2, 1], gates)
        q_m = (rms_norm(mq, mla_q_norm[l]) @ mla_w_uq[l]).reshape(B, S, MLA_HEADS, MLA_NOPE_DIM + MLA_ROPE_DIM)
        q_nope, q_pe = jnp.split(q_m, [MLA_NOPE_DIM], axis=-1)
        q_m = jnp.concatenate([q_nope, rope(q_pe, positions)], axis=-1)
        kv_m = (rms_norm(mkv, mla_kv_norm[l]) @ mla_w_ukv[l]).reshape(B, S, MLA_HEADS, MLA_NOPE_DIM + MLA_V_DIM)
        k_nope, v_m = jnp.split(kv_m, [MLA_NOPE_DIM], axis=-1)
        k_pe = jnp.broadcast_to(rope(mkr[:, :, None, :], positions), (B, S, MLA_HEADS, MLA_ROPE_DIM))
        k_m = jnp.concatenate([k_nope, k_pe], axis=-1)
        o_mla = causal_attention_blocks(q_m, k_m, v_m)
        mix = jnp.concatenate([o_nsa, o_mla], axis=-1) @ w_o[l]
        x = layer_norm(DEEPNORM_ALPHA * x + mix, ln1_g[l], ln1_b[l])
        x = layer_norm(DEEPNORM_ALPHA * x + memory_cross_attention(x, mem, mem_wq[l], mem_wk[l], mem_wv[l], mem_wo[l]),
                       ln2_g[l], ln2_b[l])
        x = layer_norm(DEEPNORM_ALPHA * x + conv_ffn(x, ffn_w_up[l], ffn_conv_w[l], ffn_conv_b[l], ffn_w_down[l]),
                       ln3_g[l], ln3_b[l])
    return x
```

```python
import functools
import math

import numpy as np
import jax
import jax.numpy as jnp
from jax import lax
from jax.experimental import pallas as pl
from jax.experimental.pallas import tpu as pltpu

F32 = jnp.float32
BF16 = jnp.bfloat16

NSA_HEADS = 8
NSA_GROUPS = 2
NSA_REP = NSA_HEADS // NSA_GROUPS
NSA_DH = 64
CMP_STRIDE = 16
CMP_LEN = 32
SLC_LEN = 64
TOPN = 16
WINDOW = 512
CMP_HIDDEN = 128
FORCE_BONUS = 1e4
MLA_HEADS = 8
MLA_Q_RANK = 384
MLA_KV_RANK = 256
MLA_NOPE = 64
MLA_ROPE = 32
MLA_V = 64
MEM_HEADS = 4
CONV_WIDTH = 3
ROPE_THETA = 10000.0
LN_EPS = 1e-5
RMS_EPS = 1e-6
NEG_INF = -1e30
LOG2E = math.log2(math.e)

LANES = 128
VMEM_LIMIT = 56 * 1024 * 1024

C_Q = 0
C_KVC = 512
C_KVN = 768
C_LAT = 1280
C_MISC = 1920
IN_COLS_PAD = 2048
GATE_LANE0 = MLA_ROPE


def _dot(a, b):
    return jnp.dot(a, b, preferred_element_type=F32)


def _dot_nt(a, b):
    return lax.dot_general(a, b, (((1,), (1,)), ((), ())), preferred_element_type=F32)


def _layer_norm(y, g, b):
    mu = jnp.mean(y, axis=-1, keepdims=True)
    d = y - mu
    var = jnp.mean(d * d, axis=-1, keepdims=True)
    return d * lax.rsqrt(var + LN_EPS) * g + b


def _params(*sem):
    return pltpu.CompilerParams(dimension_semantics=sem, vmem_limit_bytes=VMEM_LIMIT)


def _rope_tables(pos_col, inv_row, half):
    ang = pos_col * inv_row
    cos = jnp.cos(ang)
    sin = jnp.sin(ang)
    lane = lax.broadcasted_iota(jnp.int32, (1, LANES), 1)
    upper = (lane & (2 * half - 1)) >= half
    rot = inv_row != 0.0
    sin_hi = jnp.where(upper & rot, sin, 0.0)
    sin_lo = jnp.where(upper | (~rot), 0.0, -sin)
    return cos, sin_hi, sin_lo


def _apply_rope(v, tabs, half):
    cos, sin_hi, sin_lo = tabs
    return v * cos + pltpu.roll(v, half, 1) * sin_hi + pltpu.roll(v, LANES - half, 1) * sin_lo


def _inproj_body(pos_ref, inv_ref, x_ref, w_ref, qn_ref, kvn_ref, kvc_ref, lat_ref, misc_ref):
    xb = x_ref[...].astype(BF16)
    tabs = _rope_tables(pos_ref[...].astype(F32), inv_ref[...], NSA_DH // 2)
    lane = lax.broadcasted_iota(jnp.int32, (1, LANES), 1)
    low = lane < NSA_DH
    qscale = NSA_DH ** -0.5 * LOG2E

    def proj(c0, n):
        return _dot(xb, w_ref[:, c0:c0 + n])

    def split_store(v, ref, idx_lo, idx_hi):
        ref[0, idx_lo] = jnp.where(low, v, 0.0).astype(BF16)
        ref[0, idx_hi] = jnp.where(low, pltpu.roll(v, NSA_DH, 1), 0.0).astype(BF16)

    for slab in range(2):
        h = proj(C_Q + 256 * slab, 256)
        for j in range(2):
            r = _apply_rope(h[:, LANES * j:LANES * (j + 1)], tabs, NSA_DH // 2) * qscale
            split_store(r, qn_ref, 4 * slab + 2 * j, 4 * slab + 2 * j + 1)

    h = proj(C_KVC, 256)
    for j in range(2):
        v = h[:, LANES * j:LANES * (j + 1)]
        kvc_ref[2 * j] = v[:, :NSA_DH]
        kvc_ref[2 * j + 1] = pltpu.roll(v, NSA_DH, 1)[:, :NSA_DH]

    for slab in range(2):
        h = proj(C_KVN + 256 * slab, 256)
        k = _apply_rope(h[:, :LANES], tabs, NSA_DH // 2)
        split_store(k, kvn_ref, 4 * slab, 4 * slab + 1)
        split_store(h[:, LANES:], kvn_ref, 4 * slab + 2, 4 * slab + 3)

    for c0 in range(C_LAT, C_MISC, 128):
        lat_ref[:, c0 - C_LAT:c0 - C_LAT + 128] = proj(c0, 128)
    misc_ref[...] = proj(C_MISC, 128)


def _inproj(pos, inv_nsa, xf, w_in_p, B, S, tm):
    T = B * S
    nst = S // tm
    tok = lambda i: (i, 0)
    const = lambda i: (0, 0)
    head_blk = lambda i: (i // nst, 0, i % nst, 0)
    return pl.pallas_call(
        _inproj_body,
        grid=(T // tm,),
        in_specs=[
            pl.BlockSpec((tm, 1), tok),
            pl.BlockSpec((1, LANES), const),
            pl.BlockSpec((tm, xf.shape[1]), tok),
            pl.BlockSpec(w_in_p.shape, const),
        ],
        out_specs=[
            pl.BlockSpec((1, 8, tm, LANES), head_blk),
            pl.BlockSpec((1, 8, tm, LANES), head_blk),
            pl.BlockSpec((4, tm, NSA_DH), lambda i: (0, i, 0)),
            pl.BlockSpec((tm, C_MISC - C_LAT), tok),
            pl.BlockSpec((tm, LANES), tok),
        ],
        out_shape=[
            jax.ShapeDtypeStruct((B, 8, S, LANES), BF16),
            jax.ShapeDtypeStruct((B, 8, S, LANES), BF16),
            jax.ShapeDtypeStruct((4, T, NSA_DH), F32),
            jax.ShapeDtypeStruct((T, C_MISC - C_LAT), F32),
            jax.ShapeDtypeStruct((T, LANES), F32),
        ],
        compiler_params=_params("arbitrary"),
        name="inproj",
    )(pos, inv_nsa, xf, w_in_p)


def _compress_body(pos_ref, inv_ref, x_ref, pe_ref, w1_ref, b1_ref, w2_ref, b2_ref, o_ref):
    is_k = pl.program_id(0) < NSA_GROUPS
    x = x_ref[0, 0]
    half = CMP_STRIDE * NSA_DH
    a1 = _dot((x + pe_ref[0, :, :half]).astype(BF16), w1_ref[0, :half, :])
    a2 = _dot((x + pe_ref[0, :, half:]).astype(BF16), w1_ref[0, half:, :])
    nch = x.shape[0]
    pre = a1 + pltpu.roll(a2, nch - 1, 0) + b1_ref[0]
    hid = jax.nn.gelu(pre, approximate=True)
    out = _dot(hid.astype(BF16), w2_ref[0]) + b2_ref[0]
    tabs = _rope_tables(pos_ref[0].astype(F32), inv_ref[...], NSA_DH // 2)
    roped = _apply_rope(out, tabs, NSA_DH // 2)
    out = jnp.where(is_k, roped, out)
    row = lax.broadcasted_iota(jnp.int32, (nch, 1), 0)
    o_ref[0, 0] = jnp.where(row < nch - 1, out, 0.0).astype(BF16)


def _compress(pos_cmp, inv_nsa, kvc, pe, w1, b1, w2, b2, B, S):
    nch = S // CMP_STRIDE
    flat = CMP_STRIDE * NSA_DH
    x = kvc.reshape(4, B, nch, flat)
    kv = lambda j, b: (j // NSA_GROUPS, 0, 0)
    return pl.pallas_call(
        _compress_body,
        grid=(4, B),
        in_specs=[
            pl.BlockSpec((1, nch, 1), lambda j, b: (b, 0, 0)),
            pl.BlockSpec((1, LANES), lambda j, b: (0, 0)),
            pl.BlockSpec((1, 1, nch, flat), lambda j, b: (j, b, 0, 0)),
            pl.BlockSpec((1, 1, 2 * flat), kv),
            pl.BlockSpec((1, 2 * flat, CMP_HIDDEN), kv),
            pl.BlockSpec((1, 1, CMP_HIDDEN), kv),
            pl.BlockSpec((1, CMP_HIDDEN, LANES), kv),
            pl.BlockSpec((1, 1, LANES), kv),
        ],
        out_specs=pl.BlockSpec((1, 1, nch, LANES), lambda j, b: (j, b, 0, 0)),
        out_shape=jax.ShapeDtypeStruct((4, B, nch, LANES), BF16),
        compiler_params=_params("arbitrary", "arbitrary"),
        name="compress",
    )(pos_cmp, inv_nsa, x, pe, w1, b1, w2, b2)


def _rms_norm(v, g):
    return v * lax.rsqrt(jnp.mean(v * v, axis=-1, keepdims=True) + RMS_EPS) * g


def _mla_up_body(pos_ref, inv_ref, lat_ref, misc_ref, gq_ref, gkv_ref, wq_ref, wkv_ref,
                 q_ref, k_ref, v_ref):
    tabs = _rope_tables(pos_ref[...].astype(F32), inv_ref[...], MLA_ROPE // 2)
    qscale = (MLA_NOPE + MLA_ROPE) ** -0.5 * LOG2E
    lane = lax.broadcasted_iota(jnp.int32, (1, LANES), 1)
    nope = lane < MLA_NOPE
    pe_lanes = (lane >= MLA_NOPE) & (lane < MLA_NOPE + MLA_ROPE)

    qn = _rms_norm(lat_ref[:, :MLA_Q_RANK], gq_ref[...]).astype(BF16)
    kvn = _rms_norm(lat_ref[:, MLA_Q_RANK:], gkv_ref[...]).astype(BF16)
    kpe = pltpu.roll(misc_ref[...], MLA_NOPE, 1)
    kpe = jnp.where(pe_lanes, _apply_rope(kpe, tabs, MLA_ROPE // 2), 0.0)

    for slab in range(MLA_HEADS // 2):
        hq = _dot(qn, wq_ref[:, 256 * slab:256 * (slab + 1)])
        hk = _dot(kvn, wkv_ref[:, 128 * slab:128 * (slab + 1)])
        v_ref[0, slab] = _dot(kvn, wkv_ref[:, 512 + 128 * slab:512 + 128 * (slab + 1)]).astype(BF16)
        for j in range(2):
            q = _apply_rope(hq[:, LANES * j:LANES * (j + 1)], tabs, MLA_ROPE // 2) * qscale
            q_ref[0, 2 * slab + j] = q.astype(BF16)
            kn = hk if j == 0 else pltpu.roll(hk, MLA_NOPE, 1)
            k_ref[0, 2 * slab + j] = jnp.where(nope, kn, kpe).astype(BF16)


def _mla_up(pos, inv_mla, lat, misc, gq, gkv, wq_p, wkv_p, B, S, tm):
    T = B * S
    nst = S // tm
    tok = lambda i: (i, 0)
    const = lambda i: (0, 0)
    head_blk = lambda i: (i // nst, 0, i % nst, 0)
    return pl.pallas_call(
        _mla_up_body,
        grid=(T // tm,),
        in_specs=[
            pl.BlockSpec((tm, 1), tok),
            pl.BlockSpec((1, LANES), const),
            pl.BlockSpec((tm, lat.shape[1]), tok),
            pl.BlockSpec((tm, LANES), tok),
            pl.BlockSpec(gq.shape, const),
            pl.BlockSpec(gkv.shape, const),
            pl.BlockSpec(wq_p.shape, const),
            pl.BlockSpec(wkv_p.shape, const),
        ],
        out_specs=[
            pl.BlockSpec((1, MLA_HEADS, tm, LANES), head_blk),
            pl.BlockSpec((1, MLA_HEADS, tm, LANES), head_blk),
            pl.BlockSpec((1, MLA_HEADS // 2, tm, LANES), head_blk),
        ],
        out_shape=[
            jax.ShapeDtypeStruct((B, MLA_HEADS, S, LANES), BF16),
            jax.ShapeDtypeStruct((B, MLA_HEADS, S, LANES), BF16),
            jax.ShapeDtypeStruct((B, MLA_HEADS // 2, S, LANES), BF16),
        ],
        compiler_params=_params("arbitrary"),
        name="mla_up",
    )(pos, inv_mla, lat, misc, gq, gkv, wq_p, wkv_p)


def _online_step(s, v, carry):
    m, l, acc = carry
    m_new = jnp.maximum(m, jnp.max(s, axis=1, keepdims=True))
    alpha = jnp.exp2(m - m_new)
    p = jnp.exp2(s - m_new)
    l = alpha * l + jnp.sum(p, axis=1, keepdims=True)
    acc = alpha * acc + _dot(p.astype(BF16), v)
    return m_new, l, acc


def _flash_init(rows):
    return (jnp.full((rows, 1), NEG_INF, F32), jnp.zeros((rows, 1), F32),
            jnp.zeros((rows, LANES), F32))


SLC_CHUNK = 512
WIN_SPAN = WINDOW + SLC_LEN


def _nsa_body(q_ref, kvn_ref, kvc_ref, misc_ref, cov_ref, eye_ref, o_ref,
              kaug_ref, score_ref, bias_ref):
    c = pl.program_id(1)
    rows = NSA_REP * SLC_LEN
    t_row = c * SLC_LEN + (lax.broadcasted_iota(jnp.int32, (rows, 1), 0) & (SLC_LEN - 1))
    ncmp = kvc_ref.shape[2]
    nblk = kaug_ref.shape[1] // SLC_LEN

    @pl.when(c == 0)
    def _():
        kaug_ref[0] = kvn_ref[0, 0]
        kaug_ref[1] = kvn_ref[0, 1]

    qs = [q_ref[0, NSA_REP * g:NSA_REP * (g + 1)].reshape(rows, LANES) for g in range(NSA_GROUPS)]

    o_cmp = []
    psums = []
    cmp_valid = (CMP_STRIDE * lax.broadcasted_iota(jnp.int32, (1, ncmp), 1) + CMP_LEN - 1) <= t_row
    for g in range(NSA_GROUPS):
        s = jnp.where(cmp_valid, _dot_nt(qs[g], kvc_ref[g, 0]), NEG_INF)
        m = jnp.max(s, axis=1, keepdims=True)
        p = jnp.where(cmp_valid, jnp.exp2(s - m), 0.0)
        l = jnp.sum(p, axis=1, keepdims=True)
        p = p * jnp.where(l > 0.0, 1.0 / l, 0.0)
        o_cmp.append(_dot(p.astype(BF16), kvc_ref[NSA_GROUPS + g, 0]))
        psums.append(p[0:SLC_LEN] + p[SLC_LEN:2 * SLC_LEN]
                     + p[2 * SLC_LEN:3 * SLC_LEN] + p[3 * SLC_LEN:4 * SLC_LEN])

    @pl.when(c >= TOPN)
    def _():
        ps = jnp.concatenate(psums, axis=0)
        hi = ps.astype(BF16)
        lo = (ps - hi.astype(F32)).astype(BF16)
        imp = _dot_nt(cov_ref[...], hi) + _dot_nt(cov_ref[...], lo)
        jidx = lax.broadcasted_iota(jnp.int32, (nblk, LANES), 0)
        forced = (jidx == 0) | (jidx == c) | (jidx == c - 1)
        score = jnp.where(jidx <= c, jnp.where(forced, FORCE_BONUS, imp), NEG_INF)
        score_ref[...] = score
        sub = 8
        cnt = [jnp.zeros((sub, LANES), F32) for _ in range(nblk // sub)]
        tiles = [score[sub * v:sub * (v + 1)] for v in range(nblk // sub)]
        sidx = lax.broadcasted_iota(jnp.int32, (sub, LANES), 0)
        for jp in range(nblk):
            rowv = jnp.broadcast_to(score_ref[jp:jp + 1, :], (sub, LANES))
            for v in range(nblk // sub):
                if sub * v > jp:
                    cnt[v] = jnp.where(rowv >= tiles[v], cnt[v] + 1.0, cnt[v])
                elif sub * v + sub - 1 <= jp:
                    cnt[v] = jnp.where(rowv > tiles[v], cnt[v] + 1.0, cnt[v])
                else:
                    ge = jnp.where(rowv >= tiles[v], cnt[v] + 1.0, cnt[v])
                    gt = jnp.where(rowv > tiles[v], cnt[v] + 1.0, cnt[v])
                    cnt[v] = jnp.where(sidx + sub * v > jp, ge, gt)
        rank = jnp.concatenate(cnt, axis=0)
        bias = jnp.where(rank < float(TOPN), 0.0, NEG_INF)
        bias_ref[0] = pltpu.roll(bias, SLC_LEN, 1)
        bias_ref[1] = bias

        def write_bias(j, carry):
            r0 = pl.multiple_of(j * SLC_LEN, SLC_LEN)
            for g in range(NSA_GROUPS):
                blk = jnp.broadcast_to(bias_ref[g, pl.ds(j, 1), SLC_LEN:], (SLC_LEN, SLC_LEN))
                kaug_ref[g, pl.ds(r0, SLC_LEN), SLC_LEN:] = blk.astype(BF16)
            return carry
        lax.fori_loop(0, c + 1, write_bias, 0)

    sig = 1.0 / (1.0 + jnp.exp(-misc_ref[...]))
    n_full = c // (SLC_CHUNK // SLC_LEN)
    win_start = pl.multiple_of(jnp.maximum(c - WINDOW // SLC_LEN, 0) * SLC_LEN, SLC_LEN)
    outs = []
    for g in range(NSA_GROUPS):
        q = qs[g]
        qa = q + eye_ref[...]

        def slc_step(kc, carry, g=g, qa=qa):
            k0 = pl.multiple_of(kc * SLC_CHUNK, SLC_CHUNK)
            s = _dot_nt(qa, kaug_ref[g, pl.ds(k0, SLC_CHUNK), :])
            return _online_step(s, kvn_ref[0, 2 + g, pl.ds(k0, SLC_CHUNK), :], carry)

        carry = lax.fori_loop(0, n_full, slc_step, _flash_init(rows))
        k0 = pl.multiple_of(n_full * SLC_CHUNK, SLC_CHUNK)
        key = k0 + lax.broadcasted_iota(jnp.int32, (1, SLC_CHUNK), 1)
        s = jnp.where(key <= t_row, _dot_nt(qa, kaug_ref[g, pl.ds(k0, SLC_CHUNK), :]), NEG_INF)
        _, l, acc = _online_step(s, kvn_ref[0, 2 + g, pl.ds(k0, SLC_CHUNK), :], carry)
        o_slc = acc * (1.0 / l)

        key = win_start + lax.broadcasted_iota(jnp.int32, (1, WIN_SPAN), 1)
        diff = t_row - key
        s = jnp.where((diff >= 0) & (diff < WINDOW),
                      _dot_nt(q, kvn_ref[0, 4 + g, pl.ds(win_start, WIN_SPAN), :]), NEG_INF)
        _, l, acc = _online_step(s, kvn_ref[0, 6 + g, pl.ds(win_start, WIN_SPAN), :],
                                 _flash_init(rows))
        o_win = acc * (1.0 / l)

        for r in range(NSA_REP):
            lane0 = GATE_LANE0 + 3 * (NSA_REP * g + r)
            rs = slice(SLC_LEN * r, SLC_LEN * (r + 1))
            o = (sig[:, lane0:lane0 + 1] * o_cmp[g][rs] + sig[:, lane0 + 1:lane0 + 2] * o_slc[rs]
                 + sig[:, lane0 + 2:lane0 + 3] * o_win[rs])
            outs.append(o[:, :NSA_DH])
    o_ref[0] = jnp.concatenate(outs, axis=1).astype(BF16)


def _nsa(qn, kvn, kvcmp, misc, cov_t, eye, B, S):
    nblk = S // SLC_LEN
    ncmp = S // CMP_STRIDE
    return pl.pallas_call(
        _nsa_body,
        grid=(B, nblk),
        in_specs=[
            pl.BlockSpec((1, NSA_HEADS, SLC_LEN, LANES), lambda b, c: (b, 0, c, 0)),
            pl.BlockSpec((1, 8, S, LANES), lambda b, c: (b, 0, 0, 0)),
            pl.BlockSpec((4, 1, ncmp, LANES), lambda b, c: (0, b, 0, 0)),
            pl.BlockSpec((SLC_LEN, LANES), lambda b, c: (b * nblk + c, 0)),
            pl.BlockSpec(cov_t.shape, lambda b, c: (0, 0)),
            pl.BlockSpec(eye.shape, lambda b, c: (0, 0)),
        ],
        out_specs=pl.BlockSpec((1, SLC_LEN, NSA_HEADS * NSA_DH), lambda b, c: (b, c, 0)),
        out_shape=jax.ShapeDtypeStruct((B, S, NSA_HEADS * NSA_DH), BF16),
        scratch_shapes=[
            pltpu.VMEM((NSA_GROUPS, S, LANES), BF16),
            pltpu.VMEM((nblk, LANES), F32),
            pltpu.VMEM((NSA_GROUPS, nblk, LANES), F32),
        ],
        compiler_params=_params("arbitrary", "arbitrary"),
        name="nsa",
    )(qn, kvn, kvcmp, misc, cov_t, eye)


MLA_TQ = 256
MLA_CHUNK = 512


def _mla_body(q_ref, k_ref, v_ref, o_ref):
    qi = pl.program_id(2)
    n_full = (qi * MLA_TQ) // MLA_CHUNK
    t_row = qi * MLA_TQ + lax.broadcasted_iota(jnp.int32, (MLA_TQ, 1), 0)
    k0_last = pl.multiple_of(n_full * MLA_CHUNK, MLA_CHUNK)
    key_last = k0_last + lax.broadcasted_iota(jnp.int32, (1, MLA_CHUNK), 1)
    outs = []
    for j in range(2):
        q = q_ref[0, j]

        def step(kc, carry, j=j, q=q):
            k0 = pl.multiple_of(kc * MLA_CHUNK, MLA_CHUNK)
            s = _dot_nt(q, k_ref[0, j, pl.ds(k0, MLA_CHUNK), :])
            return _online_step(s, v_ref[0, 0, pl.ds(k0, MLA_CHUNK), :], carry)

        carry = lax.fori_loop(0, n_full, step, _flash_init(MLA_TQ))
        s = jnp.where(key_last <= t_row,
                      _dot_nt(q, k_ref[0, j, pl.ds(k0_last, MLA_CHUNK), :]), NEG_INF)
        _, l, acc = _online_step(s, v_ref[0, 0, pl.ds(k0_last, MLA_CHUNK), :], carry)
        outs.append(acc * (1.0 / l))
    lane = lax.broadcasted_iota(jnp.int32, (1, LANES), 1)
    o_ref[0] = jnp.where(lane < MLA_V, outs[0], outs[1]).astype(BF16)


def _mla(q, k, v, B, S):
    return pl.pallas_call(
        _mla_body,
        grid=(B, MLA_HEADS // 2, S // MLA_TQ),
        in_specs=[
            pl.BlockSpec((1, 2, MLA_TQ, LANES), lambda b, h, i: (b, h, i, 0)),
            pl.BlockSpec((1, 2, S, LANES), lambda b, h, i: (b, h, 0, 0)),
            pl.BlockSpec((1, 1, S, LANES), lambda b, h, i: (b, h, 0, 0)),
        ],
        out_specs=pl.BlockSpec((1, MLA_TQ, LANES), lambda b, h, i: (b, i, h)),
        out_shape=jax.ShapeDtypeStruct((B, S, MLA_HEADS * MLA_V), BF16),
        compiler_params=_params("arbitrary", "arbitrary", "arbitrary"),
        name="mla",
    )(q, k, v)


def _mix_ln_body(x_ref, on_ref, om_ref, w_ref, g_ref, b_ref, o_ref, *, alpha):
    half = on_ref.shape[1]
    mix = _dot(on_ref[...], w_ref[:half, :]) + _dot(om_ref[...], w_ref[half:, :])
    o_ref[...] = _layer_norm(alpha * x_ref[...] + mix, g_ref[...], b_ref[...])


def _mix_ln(xf, o_nsa, o_mla, w_o, g, b, alpha, tm):
    T, D = xf.shape
    tok = lambda i: (i, 0)
    const = lambda i: (0, 0)
    return pl.pallas_call(
        functools.partial(_mix_ln_body, alpha=alpha),
        grid=(T // tm,),
        in_specs=[
            pl.BlockSpec((tm, D), tok),
            pl.BlockSpec((tm, o_nsa.shape[1]), tok),
            pl.BlockSpec((tm, o_mla.shape[1]), tok),
            pl.BlockSpec(w_o.shape, const),
            pl.BlockSpec((1, D), const),
            pl.BlockSpec((1, D), const),
        ],
        out_specs=pl.BlockSpec((tm, D), tok),
        out_shape=jax.ShapeDtypeStruct((T, D), F32),
        compiler_params=_params("arbitrary"),
        name="mix_ln",
    )(xf, o_nsa, o_mla, w_o, g, b)


def _mem_kv_body(m_ref, wk_ref, wv_ref, k_ref, v_ref):
    mb = m_ref[...].astype(BF16)
    k_ref[...] = _dot(mb, wk_ref[...]).astype(BF16)
    v_ref[...] = _dot(mb, wv_ref[...]).astype(BF16)


def _mem_kv(memf, wk, wv, tm):
    R, D = memf.shape
    tok = lambda i: (i, 0)
    const = lambda i: (0, 0)
    return pl.pallas_call(
        _mem_kv_body,
        grid=(R // tm,),
        in_specs=[pl.BlockSpec((tm, D), tok), pl.BlockSpec(wk.shape, const),
                  pl.BlockSpec(wv.shape, const)],
        out_specs=[pl.BlockSpec((tm, D), tok), pl.BlockSpec((tm, D), tok)],
        out_shape=[jax.ShapeDtypeStruct((R, D), BF16), jax.ShapeDtypeStruct((R, D), BF16)],
        compiler_params=_params("arbitrary"),
        name="mem_kv",
    )(memf, wk, wv)


def _mem_attn_body(x_ref, k_ref, v_ref, wq_ref, wo_ref, g_ref, b_ref, o_ref, *, alpha):
    x = x_ref[...]
    D = x.shape[1]
    dh = D // MEM_HEADS
    q = (_dot(x.astype(BF16), wq_ref[...]) * (dh ** -0.5 * LOG2E)).astype(BF16)
    outs = []
    for h in range(MEM_HEADS):
        cs = slice(dh * h, dh * (h + 1))
        s = _dot_nt(q[:, cs], k_ref[0, :, cs])
        p = jnp.exp2(s - jnp.max(s, axis=1, keepdims=True))
        l = jnp.sum(p, axis=1, keepdims=True)
        outs.append((_dot(p.astype(BF16), v_ref[0, :, cs]) * (1.0 / l)).astype(BF16))
    o = jnp.concatenate(outs, axis=1)
    y = _dot(o, wo_ref[...])
    o_ref[...] = _layer_norm(alpha * x + y, g_ref[...], b_ref[...])


def _mem_attn(xf, k_mem, v_mem, wq, wo, g, b, alpha, S, tm):
    T, D = xf.shape
    nst = S // tm
    M = k_mem.shape[1]
    tok = lambda i: (i, 0)
    const = lambda i: (0, 0)
    memb = lambda i: (i // nst, 0, 0)
    return pl.pallas_call(
        functools.partial(_mem_attn_body, alpha=alpha),
        grid=(T // tm,),
        in_specs=[
            pl.BlockSpec((tm, D), tok),
            pl.BlockSpec((1, M, D), memb),
            pl.BlockSpec((1, M, D), memb),
            pl.BlockSpec(wq.shape, const),
            pl.BlockSpec(wo.shape, const),
            pl.BlockSpec((1, D), const),
            pl.BlockSpec((1, D), const),
        ],
        out_specs=pl.BlockSpec((tm, D), tok),
        out_shape=jax.ShapeDtypeStruct((T, D), F32),
        compiler_params=_params("arbitrary"),
        name="mem_attn",
    )(xf, k_mem, v_mem, wq, wo, g, b)


HALO = 8


def _ffn_body(x_ref, xh_ref, wg_ref, wu_ref, cw_ref, cb_ref, wd_ref, g_ref, b_ref, o_ref,
              acc_ref, *, alpha, seq_tiles):
    i = pl.program_id(0)
    f = pl.program_id(1)
    x = x_ref[...]
    xb = x.astype(BF16)
    tm = x.shape[0]
    gate = _dot(xb, wg_ref[...])
    up = _dot(xb, wu_ref[...])
    halo = _dot(xh_ref[...].astype(BF16), wg_ref[...])
    halo = jnp.where(i % seq_tiles == 0, 0.0, halo)
    row = lax.broadcasted_iota(jnp.int32, (tm, 1), 0)
    g1 = jnp.where(row == 0, halo[HALO - 1:HALO], pltpu.roll(gate, 1, 0))
    g2 = pltpu.roll(gate, 2, 0)
    g2 = jnp.where(row == 0, halo[HALO - 2:HALO - 1], jnp.where(row == 1, halo[HALO - 1:HALO], g2))
    conv = cw_ref[0:1, :] * g2 + cw_ref[1:2, :] * g1 + cw_ref[2:3, :] * gate + cb_ref[...]
    act = (conv * (1.0 / (1.0 + jnp.exp(-conv))) * up).astype(BF16)
    part = _dot(act, wd_ref[...])

    @pl.when(f == 0)
    def _():
        acc_ref[...] = part

    @pl.when(f > 0)
    def _():
        acc_ref[...] += part

    @pl.when(f == pl.num_programs(1) - 1)
    def _():
        o_ref[...] = _layer_norm(alpha * x + acc_ref[...], g_ref[...], b_ref[...])


def _ffn(xf, wg, wu, cw, cb, wd, g, b, alpha, S, tm, fc):
    T, D = xf.shape
    dff = wg.shape[1]
    tok = lambda i, f: (i, 0)
    const = lambda i, f: (0, 0)
    return pl.pallas_call(
        functools.partial(_ffn_body, alpha=alpha, seq_tiles=S // tm),
        grid=(T // tm, dff // fc),
        in_specs=[
            pl.BlockSpec((tm, D), tok),
            pl.BlockSpec((HALO, D), lambda i, f: (jnp.maximum(i * (tm // HALO) - 1, 0), 0)),
            pl.BlockSpec((D, fc), lambda i, f: (0, f)),
            pl.BlockSpec((D, fc), lambda i, f: (0, f)),
            pl.BlockSpec((CONV_WIDTH, fc), lambda i, f: (0, f)),
            pl.BlockSpec((1, fc), lambda i, f: (0, f)),
            pl.BlockSpec((fc, D), lambda i, f: (f, 0)),
            pl.BlockSpec((1, D), const),
            pl.BlockSpec((1, D), const),
        ],
        out_specs=pl.BlockSpec((tm, D), tok),
        out_shape=jax.ShapeDtypeStruct((T, D), F32),
        scratch_shapes=[pltpu.VMEM((tm, D), F32)],
        compiler_params=_params("arbitrary", "arbitrary"),
        name="ffn",
    )(xf, xf, wg, wu, cw, cb, wd, g, b)


def _inv_freq_row(dim, lane_lo, lane_hi, period):
    inv = ROPE_THETA ** (-np.arange(0, dim, 2, dtype=np.float64) / dim)
    row = np.zeros((1, LANES), np.float32)
    for lane in range(lane_lo, lane_hi):
        row[0, lane] = inv[(lane % period) % (dim // 2)]
    return jnp.asarray(row)


def _cover_t(S):
    nc = S // CMP_STRIDE
    ns = S // SLC_LEN
    cs = np.arange(nc)[:, None] * CMP_STRIDE
    ss = np.arange(ns)[None, :] * SLC_LEN
    cover = np.clip(np.minimum(cs + CMP_LEN, ss + SLC_LEN) - np.maximum(cs, ss), 0, None) / CMP_LEN
    cover[nc - 1:] = 0.0
    return jnp.asarray(cover.T, dtype=BF16)


def _eye_aug():
    eye = np.zeros((NSA_REP * SLC_LEN, LANES), np.float32)
    r = np.arange(NSA_REP * SLC_LEN)
    eye[r, SLC_LEN + (r % SLC_LEN)] = 1.0
    return jnp.asarray(eye, dtype=BF16)


def _permute_w_in(w):
    D = w.shape[0]
    c1 = NSA_HEADS * NSA_DH
    c2 = c1 + 3 * 2 * NSA_GROUPS * NSA_DH
    c3 = c2 + 3 * NSA_HEADS
    c4 = c3 + MLA_Q_RANK
    c5 = c4 + MLA_KV_RANK
    c6 = c5 + MLA_ROPE
    pad = jnp.zeros((D, IN_COLS_PAD - C_MISC - MLA_ROPE - 3 * NSA_HEADS), w.dtype)
    return jnp.concatenate(
        [w[:, :c2], w[:, c3:c5], w[:, c5:c6], w[:, c2:c3], pad], axis=1).astype(BF16)


def _permute_w_uq(w):
    r = w.shape[0]
    w3 = w.reshape(r, MLA_HEADS, MLA_NOPE + MLA_ROPE)
    pad = jnp.zeros((r, MLA_HEADS, LANES - MLA_NOPE - MLA_ROPE), w.dtype)
    return jnp.concatenate([w3, pad], axis=2).reshape(r, MLA_HEADS * LANES).astype(BF16)


def _permute_w_ukv(w):
    r = w.shape[0]
    w3 = w.reshape(r, MLA_HEADS, MLA_NOPE + MLA_V)
    return jnp.concatenate([w3[:, :, :MLA_NOPE].reshape(r, -1),
                            w3[:, :, MLA_NOPE:].reshape(r, -1)], axis=1).astype(BF16)


def _pad_lanes(a):
    return jnp.concatenate([a, jnp.zeros(a.shape[:-1] + (LANES - a.shape[-1],), a.dtype)], axis=-1)


def kernel(x, mem, positions, w_in, nsa_k_pos, nsa_ck_w1, nsa_ck_b1, nsa_ck_w2, nsa_ck_b2,
           nsa_v_pos, nsa_cv_w1, nsa_cv_b1, nsa_cv_w2, nsa_cv_b2,
           mla_q_norm, mla_w_uq, mla_kv_norm, mla_w_ukv, w_o, ln1_g, ln1_b,
           mem_wq, mem_wk, mem_wv, mem_wo, ln2_g, ln2_b,
           ffn_w_up, ffn_conv_w, ffn_conv_b, ffn_w_down, ln3_g, ln3_b):
    B, S, D = x.shape
    T = B * S
    depth = w_in.shape[0]
    alpha = (2.0 * depth) ** 0.25
    d_ff = ffn_w_down.shape[1]
    tm = min(512, S)
    assert S % MLA_CHUNK == 0 and S >= WIN_SPAN and S % tm == 0
    assert (B * mem.shape[1]) % 256 == 0

    pos = positions.reshape(T, 1)
    pos_cmp = positions[:, CMP_LEN - 1::CMP_STRIDE]
    pos_cmp = jnp.concatenate([pos_cmp, pos_cmp[:, -1:]], axis=1)[:, :, None]
    inv_nsa = _inv_freq_row(NSA_DH, 0, LANES, NSA_DH)
    inv_cmp = _inv_freq_row(NSA_DH, 0, NSA_DH, NSA_DH)
    inv_mla = _inv_freq_row(MLA_ROPE, MLA_NOPE, MLA_NOPE + MLA_ROPE, MLA_ROPE)
    cov_t = _cover_t(S)
    eye = _eye_aug()
    memf = mem.reshape(B * mem.shape[1], D)

    xf = x.reshape(T, D)
    for l in range(depth):
        qn, kvn, kvc, lat, misc = _inproj(pos, inv_nsa, xf, _permute_w_in(w_in[l]), B, S, tm)
        kvcmp = _compress(
            pos_cmp, inv_cmp, kvc,
            jnp.stack([nsa_k_pos[l].reshape(1, -1), nsa_v_pos[l].reshape(1, -1)]),
            jnp.stack([nsa_ck_w1[l], nsa_cv_w1[l]]).astype(BF16),
            jnp.stack([nsa_ck_b1[l], nsa_cv_b1[l]])[:, None, :],
            _pad_lanes(jnp.stack([nsa_ck_w2[l], nsa_cv_w2[l]])).astype(BF16),
            _pad_lanes(jnp.stack([nsa_ck_b2[l], nsa_cv_b2[l]]))[:, None, :],
            B, S)
        q_m, k_m, v_m = _mla_up(pos, inv_mla, lat, misc, mla_q_norm[l][None, :],
                                mla_kv_norm[l][None, :], _permute_w_uq(mla_w_uq[l]),
                                _permute_w_ukv(mla_w_ukv[l]), B, S, tm)
        o_nsa = _nsa(qn, kvn, kvcmp, misc, cov_t, eye, B, S)
        o_mla = _mla(q_m, k_m, v_m, B, S)
        xf = _mix_ln(xf, o_nsa.reshape(T, -1), o_mla.reshape(T, -1), w_o[l].astype(BF16),
                     ln1_g[l][None, :], ln1_b[l][None, :], alpha, tm)
        k_mem, v_mem = _mem_kv(memf, mem_wk[l].astype(BF16), mem_wv[l].astype(BF16), 256)
        xf = _mem_attn(xf, k_mem.reshape(B, -1, D), v_mem.reshape(B, -1, D),
                       mem_wq[l].astype(BF16), mem_wo[l].astype(BF16),
                       ln2_g[l][None, :], ln2_b[l][None, :], alpha, S, tm)
        xf = _ffn(xf, ffn_w_up[l][:, :d_ff].astype(BF16), ffn_w_up[l][:, d_ff:].astype(BF16),
                  ffn_conv_w[l], ffn_conv_b[l][None, :], ffn_w_down[l].astype(BF16),
                  ln3_g[l][None, :], ln3_b[l][None, :], alpha, S, tm, d_ff // 2)
    return xf.reshape(B, S, D)
```

```python
import functools
import math

import numpy as np
import jax
import jax.numpy as jnp
from jax import lax
from jax.experimental import pallas as pl
from jax.experimental.pallas import tpu as pltpu

F32 = jnp.float32
BF16 = jnp.bfloat16

NSA_HEADS = 8
NSA_GROUPS = 2
NSA_REP = NSA_HEADS // NSA_GROUPS
NSA_DH = 64
CMP_STRIDE = 16
CMP_LEN = 32
SLC_LEN = 64
TOPN = 16
WINDOW = 512
CMP_HIDDEN = 128
FORCE_BONUS = 1e4
MLA_HEADS = 8
MLA_Q_RANK = 384
MLA_KV_RANK = 256
MLA_NOPE = 64
MLA_ROPE = 32
MLA_V = 64
MEM_HEADS = 4
CONV_WIDTH = 3
ROPE_THETA = 10000.0
LN_EPS = 1e-5
RMS_EPS = 1e-6
NEG_INF = -1e30
LOG2E = math.log2(math.e)

LANES = 128
VMEM_LIMIT = 56 * 1024 * 1024

C_Q = 0
C_KVC = 512
C_KVN = 768
C_LAT = 1280
C_MISC = 1920
IN_COLS_PAD = 2048
GATE_LANE0 = MLA_ROPE
ONES_LANE = 64


def _dot(a, b):
    return jnp.dot(a, b, preferred_element_type=F32)


def _dot_nt(a, b):
    return lax.dot_general(a, b, (((1,), (1,)), ((), ())), preferred_element_type=F32)


def _layer_norm(y, g, b):
    mu = jnp.mean(y, axis=-1, keepdims=True)
    d = y - mu
    var = jnp.mean(d * d, axis=-1, keepdims=True)
    return d * lax.rsqrt(var + LN_EPS) * g + b


def _params(*sem):
    return pltpu.CompilerParams(dimension_semantics=sem, vmem_limit_bytes=VMEM_LIMIT)


def _rope_tables(pos_col, inv_row, half):
    ang = pos_col * inv_row
    cos = jnp.cos(ang)
    sin = jnp.sin(ang)
    lane = lax.broadcasted_iota(jnp.int32, (1, LANES), 1)
    upper = (lane & (2 * half - 1)) >= half
    rot = inv_row != 0.0
    sin_hi = jnp.where(upper & rot, sin, 0.0)
    sin_lo = jnp.where(upper | (~rot), 0.0, -sin)
    return cos, sin_hi, sin_lo


def _apply_rope(v, tabs, half):
    cos, sin_hi, sin_lo = tabs
    return v * cos + pltpu.roll(v, half, 1) * sin_hi + pltpu.roll(v, LANES - half, 1) * sin_lo


def _inproj_body(pos_ref, inv_ref, x_ref, w_ref, qn_ref, kvn_ref, kvc_ref, lat_ref, misc_ref):
    xb = x_ref[...].astype(BF16)
    tabs = _rope_tables(pos_ref[...].astype(F32), inv_ref[...], NSA_DH // 2)
    lane = lax.broadcasted_iota(jnp.int32, (1, LANES), 1)
    low = lane < NSA_DH
    qscale = NSA_DH ** -0.5 * LOG2E

    def proj(c0, n):
        return _dot(xb, w_ref[:, c0:c0 + n])

    def split_store(v, ref, idx_lo, idx_hi, pad=0.0):
        ref[0, idx_lo] = jnp.where(low, v, pad).astype(BF16)
        ref[0, idx_hi] = jnp.where(low, pltpu.roll(v, NSA_DH, 1), pad).astype(BF16)

    ones_pad = jnp.where(lane == ONES_LANE, 1.0, 0.0)

    for slab in range(2):
        h = proj(C_Q + 256 * slab, 256)
        for j in range(2):
            r = _apply_rope(h[:, LANES * j:LANES * (j + 1)], tabs, NSA_DH // 2) * qscale
            split_store(r, qn_ref, 4 * slab + 2 * j, 4 * slab + 2 * j + 1)

    h = proj(C_KVC, 256)
    for j in range(2):
        v = h[:, LANES * j:LANES * (j + 1)]
        kvc_ref[2 * j] = v[:, :NSA_DH]
        kvc_ref[2 * j + 1] = pltpu.roll(v, NSA_DH, 1)[:, :NSA_DH]

    for slab in range(2):
        h = proj(C_KVN + 256 * slab, 256)
        k = _apply_rope(h[:, :LANES], tabs, NSA_DH // 2)
        split_store(k, kvn_ref, 4 * slab, 4 * slab + 1)
        split_store(h[:, LANES:], kvn_ref, 4 * slab + 2, 4 * slab + 3, ones_pad)

    for c0 in range(C_LAT, C_MISC, 128):
        lat_ref[:, c0 - C_LAT:c0 - C_LAT + 128] = proj(c0, 128)
    misc_ref[...] = proj(C_MISC, 128)


def _inproj(pos, inv_nsa, xf, w_in_p, B, S, tm):
    T = B * S
    nst = S // tm
    tok = lambda i: (i, 0)
    const = lambda i: (0, 0)
    head_blk = lambda i: (i // nst, 0, i % nst, 0)
    return pl.pallas_call(
        _inproj_body,
        grid=(T // tm,),
        in_specs=[
            pl.BlockSpec((tm, 1), tok),
            pl.BlockSpec((1, LANES), const),
            pl.BlockSpec((tm, xf.shape[1]), tok),
            pl.BlockSpec(w_in_p.shape, const),
        ],
        out_specs=[
            pl.BlockSpec((1, 8, tm, LANES), head_blk),
            pl.BlockSpec((1, 8, tm, LANES), head_blk),
            pl.BlockSpec((4, tm, NSA_DH), lambda i: (0, i, 0)),
            pl.BlockSpec((tm, C_MISC - C_LAT), tok),
            pl.BlockSpec((tm, LANES), tok),
        ],
        out_shape=[
            jax.ShapeDtypeStruct((B, 8, S, LANES), BF16),
            jax.ShapeDtypeStruct((B, 8, S, LANES), BF16),
            jax.ShapeDtypeStruct((4, T, NSA_DH), F32),
            jax.ShapeDtypeStruct((T, C_MISC - C_LAT), F32),
            jax.ShapeDtypeStruct((T, LANES), F32),
        ],
        compiler_params=_params("arbitrary"),
        name="inproj",
    )(pos, inv_nsa, xf, w_in_p)


def _compress_body(pos_ref, inv_ref, x_ref, pe_ref, w1_ref, b1_ref, w2_ref, b2_ref, o_ref):
    is_k = pl.program_id(0) < NSA_GROUPS
    x = x_ref[0, 0]
    half = CMP_STRIDE * NSA_DH
    a1 = _dot((x + pe_ref[0, :, :half]).astype(BF16), w1_ref[0, :half, :])
    a2 = _dot((x + pe_ref[0, :, half:]).astype(BF16), w1_ref[0, half:, :])
    nch = x.shape[0]
    pre = a1 + pltpu.roll(a2, nch - 1, 0) + b1_ref[0]
    hid = jax.nn.gelu(pre, approximate=True)
    out = _dot(hid.astype(BF16), w2_ref[0]) + b2_ref[0]
    tabs = _rope_tables(pos_ref[0].astype(F32), inv_ref[...], NSA_DH // 2)
    roped = _apply_rope(out, tabs, NSA_DH // 2)
    out = jnp.where(is_k, roped, out)
    row = lax.broadcasted_iota(jnp.int32, (nch, 1), 0)
    o_ref[0, 0] = jnp.where(row < nch - 1, out, 0.0).astype(BF16)


def _compress(pos_cmp, inv_nsa, kvc, pe, w1, b1, w2, b2, B, S):
    nch = S // CMP_STRIDE
    flat = CMP_STRIDE * NSA_DH
    x = kvc.reshape(4, B, nch, flat)
    kv = lambda j, b: (j // NSA_GROUPS, 0, 0)
    return pl.pallas_call(
        _compress_body,
        grid=(4, B),
        in_specs=[
            pl.BlockSpec((1, nch, 1), lambda j, b: (b, 0, 0)),
            pl.BlockSpec((1, LANES), lambda j, b: (0, 0)),
            pl.BlockSpec((1, 1, nch, flat), lambda j, b: (j, b, 0, 0)),
            pl.BlockSpec((1, 1, 2 * flat), kv),
            pl.BlockSpec((1, 2 * flat, CMP_HIDDEN), kv),
            pl.BlockSpec((1, 1, CMP_HIDDEN), kv),
            pl.BlockSpec((1, CMP_HIDDEN, LANES), kv),
            pl.BlockSpec((1, 1, LANES), kv),
        ],
        out_specs=pl.BlockSpec((1, 1, nch, LANES), lambda j, b: (j, b, 0, 0)),
        out_shape=jax.ShapeDtypeStruct((4, B, nch, LANES), BF16),
        compiler_params=_params("arbitrary", "arbitrary"),
        name="compress",
    )(pos_cmp, inv_nsa, x, pe, w1, b1, w2, b2)


def _rms_norm(v, g):
    return v * lax.rsqrt(jnp.mean(v * v, axis=-1, keepdims=True) + RMS_EPS) * g


def _mla_up_body(pos_ref, inv_ref, lat_ref, misc_ref, gq_ref, gkv_ref, wq_ref, wkv_ref,
                 q_ref, k_ref, v_ref):
    tabs = _rope_tables(pos_ref[...].astype(F32), inv_ref[...], MLA_ROPE // 2)
    qscale = (MLA_NOPE + MLA_ROPE) ** -0.5 * LOG2E
    lane = lax.broadcasted_iota(jnp.int32, (1, LANES), 1)
    nope = lane < MLA_NOPE
    pe_lanes = (lane >= MLA_NOPE) & (lane < MLA_NOPE + MLA_ROPE)
    ones_pad = jnp.where(lane == ONES_LANE, 1.0, 0.0)

    qn = _rms_norm(lat_ref[:, :MLA_Q_RANK], gq_ref[...]).astype(BF16)
    kvn = _rms_norm(lat_ref[:, MLA_Q_RANK:], gkv_ref[...]).astype(BF16)
    kpe = pltpu.roll(misc_ref[...], MLA_NOPE, 1)
    kpe = jnp.where(pe_lanes, _apply_rope(kpe, tabs, MLA_ROPE // 2), 0.0)

    for slab in range(MLA_HEADS // 2):
        hq = _dot(qn, wq_ref[:, 256 * slab:256 * (slab + 1)])
        hk = _dot(kvn, wkv_ref[:, 128 * slab:128 * (slab + 1)])
        hv = _dot(kvn, wkv_ref[:, 512 + 128 * slab:512 + 128 * (slab + 1)])
        for j in range(2):
            q = _apply_rope(hq[:, LANES * j:LANES * (j + 1)], tabs, MLA_ROPE // 2) * qscale
            q_ref[0, 2 * slab + j] = q.astype(BF16)
            kn = hk if j == 0 else pltpu.roll(hk, MLA_NOPE, 1)
            k_ref[0, 2 * slab + j] = jnp.where(nope, kn, kpe).astype(BF16)
            vn = hv if j == 0 else pltpu.roll(hv, MLA_V, 1)
            v_ref[0, 2 * slab + j] = jnp.where(nope, vn, ones_pad).astype(BF16)


def _mla_up(pos, inv_mla, lat, misc, gq, gkv, wq_p, wkv_p, B, S, tm):
    T = B * S
    nst = S // tm
    tok = lambda i: (i, 0)
    const = lambda i: (0, 0)
    head_blk = lambda i: (i // nst, 0, i % nst, 0)
    return pl.pallas_call(
        _mla_up_body,
        grid=(T // tm,),
        in_specs=[
            pl.BlockSpec((tm, 1), tok),
            pl.BlockSpec((1, LANES), const),
            pl.BlockSpec((tm, lat.shape[1]), tok),
            pl.BlockSpec((tm, LANES), tok),
            pl.BlockSpec(gq.shape, const),
            pl.BlockSpec(gkv.shape, const),
            pl.BlockSpec(wq_p.shape, const),
            pl.BlockSpec(wkv_p.shape, const),
        ],
        out_specs=[
            pl.BlockSpec((1, MLA_HEADS, tm, LANES), head_blk),
            pl.BlockSpec((1, MLA_HEADS, tm, LANES), head_blk),
            pl.BlockSpec((1, MLA_HEADS, tm, LANES), head_blk),
        ],
        out_shape=[
            jax.ShapeDtypeStruct((B, MLA_HEADS, S, LANES), BF16),
            jax.ShapeDtypeStruct((B, MLA_HEADS, S, LANES), BF16),
            jax.ShapeDtypeStruct((B, MLA_HEADS, S, LANES), BF16),
        ],
        compiler_params=_params("arbitrary"),
        name="mla_up",
    )(pos, inv_mla, lat, misc, gq, gkv, wq_p, wkv_p)


STRIP = 64


def _lane_tile(col, n):
    reps = [col] * (n // LANES)
    if n % LANES:
        reps.append(col[:, :n % LANES])
    return reps[0] if len(reps) == 1 else jnp.concatenate(reps, axis=1)


def _flash_reset(m_ref, acc_ref):
    m_ref[...] = jnp.full(m_ref.shape, NEG_INF, F32)
    acc_ref[...] = jnp.zeros(acc_ref.shape, F32)


def _flash_update(s_ref, v, m_ref, acc_ref, p_ref, mask=None):
    rows, n = s_ref.shape
    for r in range(rows // STRIP):
        rs = slice(STRIP * r, STRIP * (r + 1))
        s = s_ref[rs, :]
        if mask is not None:
            s = jnp.where(mask[rs], s, NEG_INF)
        m_old = m_ref[rs, :]
        m_new = jnp.maximum(m_old, jnp.max(s, axis=1, keepdims=True))
        p_ref[rs, :] = jnp.exp2(s - _lane_tile(m_new, n)).astype(BF16)
        acc_ref[rs, :] = jnp.exp2(m_old - m_new) * acc_ref[rs, :]
        m_ref[rs, :] = m_new
    acc_ref[...] += _dot(p_ref[...], v)


def _flash_finish(acc):
    return acc * (1.0 / acc[:, ONES_LANE:ONES_LANE + 1])


SLC_CHUNK = 256
WIN_SPAN = WINDOW + SLC_LEN


def _nsa_body(q_ref, kvn_ref, kvc_ref, misc_ref, cov_ref, eye_ref, o_ref,
              kaug_ref, score_ref, bias_ref, sa_ref, sb_ref, pa_ref, pb_ref, ms_ref, accs_ref,
              sw_ref, pw_ref, mw_ref, accw_ref):
    c = pl.program_id(1)
    rows = NSA_REP * SLC_LEN
    t_row = c * SLC_LEN + (lax.broadcasted_iota(jnp.int32, (rows, 1), 0) & (SLC_LEN - 1))
    ncmp = kvc_ref.shape[2]
    nblk = kaug_ref.shape[1] // SLC_LEN

    @pl.when(c == 0)
    def _():
        kaug_ref[0] = kvn_ref[0, 0]
        kaug_ref[1] = kvn_ref[0, 1]

    qs = [q_ref[0, NSA_REP * g:NSA_REP * (g + 1)].reshape(rows, LANES) for g in range(NSA_GROUPS)]

    o_cmp = []
    psums = []
    cmp_valid = (CMP_STRIDE * lax.broadcasted_iota(jnp.int32, (1, ncmp), 1) + CMP_LEN - 1) <= t_row
    for g in range(NSA_GROUPS):
        s = jnp.where(cmp_valid, _dot_nt(qs[g], kvc_ref[g, 0]), NEG_INF)
        m = jnp.max(s, axis=1, keepdims=True)
        p = jnp.where(cmp_valid, jnp.exp2(s - m), 0.0)
        l = jnp.sum(p, axis=1, keepdims=True)
        p = p * jnp.where(l > 0.0, 1.0 / l, 0.0)
        o_cmp.append(_dot(p.astype(BF16), kvc_ref[NSA_GROUPS + g, 0]))
        psums.append(p[0:SLC_LEN] + p[SLC_LEN:2 * SLC_LEN]
                     + p[2 * SLC_LEN:3 * SLC_LEN] + p[3 * SLC_LEN:4 * SLC_LEN])

    _flash_reset(mw_ref, accw_ref)
    win_start = pl.multiple_of(jnp.maximum(c - WINDOW // SLC_LEN, 0) * SLC_LEN, SLC_LEN)
    diff = t_row - (win_start + lax.broadcasted_iota(jnp.int32, (1, WIN_SPAN), 1))
    win_valid = (diff >= 0) & (diff < WINDOW)
    for g in range(NSA_GROUPS):
        sw_ref[g] = _dot_nt(qs[g], kvn_ref[0, 4 + g, pl.ds(win_start, WIN_SPAN), :])
    for g in range(NSA_GROUPS):
        _flash_update(sw_ref.at[g], kvn_ref[0, 6 + g, pl.ds(win_start, WIN_SPAN), :],
                      mw_ref.at[g], accw_ref.at[g], pw_ref.at[g], win_valid)

    @pl.when(c >= TOPN)
    def _():
        ps = jnp.concatenate(psums, axis=0)
        hi = ps.astype(BF16)
        lo = (ps - hi.astype(F32)).astype(BF16)
        imp = _dot_nt(cov_ref[...], hi) + _dot_nt(cov_ref[...], lo)
        jidx = lax.broadcasted_iota(jnp.int32, (nblk, LANES), 0)
        forced = (jidx == 0) | (jidx == c) | (jidx == c - 1)
        score = jnp.where(jidx <= c, jnp.where(forced, FORCE_BONUS, imp), NEG_INF)
        score_ref[...] = score
        sub = 8
        cnt = [jnp.zeros((sub, LANES), F32) for _ in range(nblk // sub)]
        tiles = [score[sub * v:sub * (v + 1)] for v in range(nblk // sub)]
        sidx = lax.broadcasted_iota(jnp.int32, (sub, LANES), 0)
        for jp in range(nblk):
            rowv = jnp.broadcast_to(score_ref[jp:jp + 1, :], (sub, LANES))
            for v in range(nblk // sub):
                if sub * v > jp:
                    cnt[v] = jnp.where(rowv >= tiles[v], cnt[v] + 1.0, cnt[v])
                elif sub * v + sub - 1 <= jp:
                    cnt[v] = jnp.where(rowv > tiles[v], cnt[v] + 1.0, cnt[v])
                else:
                    ge = jnp.where(rowv >= tiles[v], cnt[v] + 1.0, cnt[v])
                    gt = jnp.where(rowv > tiles[v], cnt[v] + 1.0, cnt[v])
                    cnt[v] = jnp.where(sidx + sub * v > jp, ge, gt)
        rank = jnp.concatenate(cnt, axis=0)
        bias = jnp.where(rank < float(TOPN), 0.0, NEG_INF)
        bias_ref[0] = pltpu.roll(bias, SLC_LEN, 1)
        bias_ref[1] = bias

        def write_bias(j, carry):
            r0 = pl.multiple_of(j * SLC_LEN, SLC_LEN)
            for g in range(NSA_GROUPS):
                blk = jnp.broadcast_to(bias_ref[g, pl.ds(j, 1), SLC_LEN:], (SLC_LEN, SLC_LEN))
                kaug_ref[g, pl.ds(r0, SLC_LEN), SLC_LEN:] = blk.astype(BF16)
            return carry
        lax.fori_loop(0, c + 1, write_bias, 0)

    _flash_reset(ms_ref, accs_ref)
    qas = [q + eye_ref[...] for q in qs]
    nchunk = kaug_ref.shape[1] // SLC_CHUNK
    last = c // (SLC_CHUNK // SLC_LEN)

    def slc_scores(kc, dst):
        k0 = pl.multiple_of(jnp.minimum(kc, nchunk - 1) * SLC_CHUNK, SLC_CHUNK)
        for g in range(NSA_GROUPS):
            dst[g] = _dot_nt(qas[g], kaug_ref[g, pl.ds(k0, SLC_CHUNK), :])

    def slc_update(src, p_ref, kc, causal):
        k0 = pl.multiple_of(jnp.minimum(kc, nchunk - 1) * SLC_CHUNK, SLC_CHUNK)
        mask = None
        if causal:
            mask = (kc * SLC_CHUNK + lax.broadcasted_iota(jnp.int32, (1, SLC_CHUNK), 1)) <= t_row
        for g in range(NSA_GROUPS):
            _flash_update(src.at[g], kvn_ref[0, 2 + g, pl.ds(k0, SLC_CHUNK), :],
                          ms_ref.at[g], accs_ref.at[g], p_ref.at[g], mask)

    slc_scores(0, sa_ref)

    def slc_pair(i, carry):
        slc_scores(2 * i + 1, sb_ref)
        slc_update(sa_ref, pa_ref, 2 * i, False)
        slc_scores(2 * i + 2, sa_ref)
        slc_update(sb_ref, pb_ref, 2 * i + 1, False)
        return carry
    lax.fori_loop(0, last // 2, slc_pair, 0)
    tail = 2 * (last // 2)
    slc_scores(tail + 1, sb_ref)
    slc_update(sa_ref, pa_ref, tail, True)
    slc_update(sb_ref, pb_ref, tail + 1, True)

    sig = 1.0 / (1.0 + jnp.exp(-misc_ref[...]))
    outs = []
    for g in range(NSA_GROUPS):
        o_slc = _flash_finish(accs_ref[g])
        o_win = _flash_finish(accw_ref[g])
        for r in range(NSA_REP):
            lane0 = GATE_LANE0 + 3 * (NSA_REP * g + r)
            rs = slice(SLC_LEN * r, SLC_LEN * (r + 1))
            o = (sig[:, lane0:lane0 + 1] * o_cmp[g][rs] + sig[:, lane0 + 1:lane0 + 2] * o_slc[rs]
                 + sig[:, lane0 + 2:lane0 + 3] * o_win[rs])
            outs.append(o[:, :NSA_DH])
    o_ref[0] = jnp.concatenate(outs, axis=1).astype(BF16)


def _nsa(qn, kvn, kvcmp, misc, cov_t, eye, B, S):
    nblk = S // SLC_LEN
    ncmp = S // CMP_STRIDE
    rows = NSA_REP * SLC_LEN
    return pl.pallas_call(
        _nsa_body,
        grid=(B, nblk),
        in_specs=[
            pl.BlockSpec((1, NSA_HEADS, SLC_LEN, LANES), lambda b, c: (b, 0, c, 0)),
            pl.BlockSpec((1, 8, S, LANES), lambda b, c: (b, 0, 0, 0)),
            pl.BlockSpec((4, 1, ncmp, LANES), lambda b, c: (0, b, 0, 0)),
            pl.BlockSpec((SLC_LEN, LANES), lambda b, c: (b * nblk + c, 0)),
            pl.BlockSpec(cov_t.shape, lambda b, c: (0, 0)),
            pl.BlockSpec(eye.shape, lambda b, c: (0, 0)),
        ],
        out_specs=pl.BlockSpec((1, SLC_LEN, NSA_HEADS * NSA_DH), lambda b, c: (b, c, 0)),
        out_shape=jax.ShapeDtypeStruct((B, S, NSA_HEADS * NSA_DH), BF16),
        scratch_shapes=[
            pltpu.VMEM((NSA_GROUPS, S, LANES), BF16),
            pltpu.VMEM((nblk, LANES), F32),
            pltpu.VMEM((NSA_GROUPS, nblk, LANES), F32),
            pltpu.VMEM((NSA_GROUPS, rows, SLC_CHUNK), F32),
            pltpu.VMEM((NSA_GROUPS, rows, SLC_CHUNK), F32),
            pltpu.VMEM((NSA_GROUPS, rows, SLC_CHUNK), BF16),
            pltpu.VMEM((NSA_GROUPS, rows, SLC_CHUNK), BF16),
            pltpu.VMEM((NSA_GROUPS, rows, LANES), F32),
            pltpu.VMEM((NSA_GROUPS, rows, LANES), F32),
            pltpu.VMEM((NSA_GROUPS, rows, WIN_SPAN), F32),
            pltpu.VMEM((NSA_GROUPS, rows, WIN_SPAN), BF16),
            pltpu.VMEM((NSA_GROUPS, rows, LANES), F32),
            pltpu.VMEM((NSA_GROUPS, rows, LANES), F32),
        ],
        compiler_params=_params("arbitrary", "arbitrary"),
        name="nsa",
    )(qn, kvn, kvcmp, misc, cov_t, eye)


MLA_TQ = 256
MLA_HPB = 4


def _mla_body(q_ref, k_ref, v_ref, o_ref, sa_ref, sb_ref, pa_ref, pb_ref, m_ref, acc_ref):
    qi = pl.program_id(2)
    nchunk = k_ref.shape[2] // MLA_TQ
    t_row = qi * MLA_TQ + lax.broadcasted_iota(jnp.int32, (MLA_TQ, 1), 0)
    _flash_reset(m_ref, acc_ref)

    def scores(kc, dst):
        k0 = pl.multiple_of(jnp.minimum(kc, nchunk - 1) * MLA_TQ, MLA_TQ)
        for j in range(MLA_HPB):
            dst[j] = _dot_nt(q_ref[0, j], k_ref[0, j, pl.ds(k0, MLA_TQ), :])

    def update(src, p_ref, kc, causal):
        k0 = pl.multiple_of(jnp.minimum(kc, nchunk - 1) * MLA_TQ, MLA_TQ)
        mask = None
        if causal:
            mask = (kc * MLA_TQ + lax.broadcasted_iota(jnp.int32, (1, MLA_TQ), 1)) <= t_row
        for j in range(MLA_HPB):
            _flash_update(src.at[j], v_ref[0, j, pl.ds(k0, MLA_TQ), :],
                          m_ref.at[j], acc_ref.at[j], p_ref.at[j], mask)

    scores(0, sa_ref)

    def pair(i, carry):
        scores(2 * i + 1, sb_ref)
        update(sa_ref, pa_ref, 2 * i, False)
        scores(2 * i + 2, sa_ref)
        update(sb_ref, pb_ref, 2 * i + 1, False)
        return carry
    lax.fori_loop(0, qi // 2, pair, 0)
    tail = 2 * (qi // 2)
    scores(tail + 1, sb_ref)
    update(sa_ref, pa_ref, tail, True)
    update(sb_ref, pb_ref, tail + 1, True)
    o_ref[0] = jnp.concatenate([_flash_finish(acc_ref[j])[:, :MLA_V] for j in range(MLA_HPB)],
                               axis=1).astype(BF16)


def _mla(q, k, v, B, S):
    return pl.pallas_call(
        _mla_body,
        grid=(B, MLA_HEADS // MLA_HPB, S // MLA_TQ),
        in_specs=[
            pl.BlockSpec((1, MLA_HPB, MLA_TQ, LANES), lambda b, h, i: (b, h, i, 0)),
            pl.BlockSpec((1, MLA_HPB, S, LANES), lambda b, h, i: (b, h, 0, 0)),
            pl.BlockSpec((1, MLA_HPB, S, LANES), lambda b, h, i: (b, h, 0, 0)),
        ],
        out_specs=pl.BlockSpec((1, MLA_TQ, MLA_HPB * MLA_V), lambda b, h, i: (b, i, h)),
        out_shape=jax.ShapeDtypeStruct((B, S, MLA_HEADS * MLA_V), BF16),
        scratch_shapes=[
            pltpu.VMEM((MLA_HPB, MLA_TQ, MLA_TQ), F32),
            pltpu.VMEM((MLA_HPB, MLA_TQ, MLA_TQ), F32),
            pltpu.VMEM((MLA_HPB, MLA_TQ, MLA_TQ), BF16),
            pltpu.VMEM((MLA_HPB, MLA_TQ, MLA_TQ), BF16),
            pltpu.VMEM((MLA_HPB, MLA_TQ, LANES), F32),
            pltpu.VMEM((MLA_HPB, MLA_TQ, LANES), F32),
        ],
        compiler_params=_params("arbitrary", "arbitrary", "arbitrary"),
        name="mla",
    )(q, k, v)


def _mix_ln_body(x_ref, on_ref, om_ref, w_ref, g_ref, b_ref, o_ref, *, alpha):
    half = on_ref.shape[1]
    mix = _dot(on_ref[...], w_ref[:half, :]) + _dot(om_ref[...], w_ref[half:, :])
    o_ref[...] = _layer_norm(alpha * x_ref[...] + mix, g_ref[...], b_ref[...])


def _mix_ln(xf, o_nsa, o_mla, w_o, g, b, alpha, tm):
    T, D = xf.shape
    tok = lambda i: (i, 0)
    const = lambda i: (0, 0)
    return pl.pallas_call(
        functools.partial(_mix_ln_body, alpha=alpha),
        grid=(T // tm,),
        in_specs=[
            pl.BlockSpec((tm, D), tok),
            pl.BlockSpec((tm, o_nsa.shape[1]), tok),
            pl.BlockSpec((tm, o_mla.shape[1]), tok),
            pl.BlockSpec(w_o.shape, const),
            pl.BlockSpec((1, D), const),
            pl.BlockSpec((1, D), const),
        ],
        out_specs=pl.BlockSpec((tm, D), tok),
        out_shape=jax.ShapeDtypeStruct((T, D), F32),
        compiler_params=_params("arbitrary"),
        name="mix_ln",
    )(xf, o_nsa, o_mla, w_o, g, b)


def _mem_kv_body(m_ref, wk_ref, wv_ref, k_ref, v_ref):
    mb = m_ref[...].astype(BF16)
    k_ref[...] = _dot(mb, wk_ref[...]).astype(BF16)
    v_ref[...] = _dot(mb, wv_ref[...]).astype(BF16)


def _mem_kv(memf, wk, wv, tm):
    R, D = memf.shape
    tok = lambda i: (i, 0)
    const = lambda i: (0, 0)
    return pl.pallas_call(
        _mem_kv_body,
        grid=(R // tm,),
        in_specs=[pl.BlockSpec((tm, D), tok), pl.BlockSpec(wk.shape, const),
                  pl.BlockSpec(wv.shape, const)],
        out_specs=[pl.BlockSpec((tm, D), tok), pl.BlockSpec((tm, D), tok)],
        out_shape=[jax.ShapeDtypeStruct((R, D), BF16), jax.ShapeDtypeStruct((R, D), BF16)],
        compiler_params=_params("arbitrary"),
        name="mem_kv",
    )(memf, wk, wv)


def _mem_attn_body(x_ref, k_ref, v_ref, wq_ref, wo_ref, g_ref, b_ref, o_ref, *, alpha):
    x = x_ref[...]
    D = x.shape[1]
    dh = D // MEM_HEADS
    q = (_dot(x.astype(BF16), wq_ref[...]) * (dh ** -0.5 * LOG2E)).astype(BF16)
    outs = []
    for h in range(MEM_HEADS):
        cs = slice(dh * h, dh * (h + 1))
        s = _dot_nt(q[:, cs], k_ref[0, :, cs])
        p = jnp.exp2(s - jnp.max(s, axis=1, keepdims=True))
        l = jnp.sum(p, axis=1, keepdims=True)
        outs.append((_dot(p.astype(BF16), v_ref[0, :, cs]) * (1.0 / l)).astype(BF16))
    o = jnp.concatenate(outs, axis=1)
    y = _dot(o, wo_ref[...])
    o_ref[...] = _layer_norm(alpha * x + y, g_ref[...], b_ref[...])


def _mem_attn(xf, k_mem, v_mem, wq, wo, g, b, alpha, S, tm):
    T, D = xf.shape
    nst = S // tm
    M = k_mem.shape[1]
    tok = lambda i: (i, 0)
    const = lambda i: (0, 0)
    memb = lambda i: (i // nst, 0, 0)
    return pl.pallas_call(
        functools.partial(_mem_attn_body, alpha=alpha),
        grid=(T // tm,),
        in_specs=[
            pl.BlockSpec((tm, D), tok),
            pl.BlockSpec((1, M, D), memb),
            pl.BlockSpec((1, M, D), memb),
            pl.BlockSpec(wq.shape, const),
            pl.BlockSpec(wo.shape, const),
            pl.BlockSpec((1, D), const),
            pl.BlockSpec((1, D), const),
        ],
        out_specs=pl.BlockSpec((tm, D), tok),
        out_shape=jax.ShapeDtypeStruct((T, D), F32),
        compiler_params=_params("arbitrary"),
        name="mem_attn",
    )(xf, k_mem, v_mem, wq, wo, g, b)


HALO = 8


def _ffn_body(x_ref, xh_ref, wg_ref, wu_ref, cw_ref, cb_ref, wd_ref, g_ref, b_ref, o_ref,
              acc_ref, *, alpha, seq_tiles):
    i = pl.program_id(0)
    f = pl.program_id(1)
    x = x_ref[...]
    xb = x.astype(BF16)
    tm = x.shape[0]
    gate = _dot(xb, wg_ref[...])
    up = _dot(xb, wu_ref[...])
    halo = _dot(xh_ref[...].astype(BF16), wg_ref[...])
    halo = jnp.where(i % seq_tiles == 0, 0.0, halo)
    row = lax.broadcasted_iota(jnp.int32, (tm, 1), 0)
    g1 = jnp.where(row == 0, halo[HALO - 1:HALO], pltpu.roll(gate, 1, 0))
    g2 = pltpu.roll(gate, 2, 0)
    g2 = jnp.where(row == 0, halo[HALO - 2:HALO - 1], jnp.where(row == 1, halo[HALO - 1:HALO], g2))
    conv = cw_ref[0:1, :] * g2 + cw_ref[1:2, :] * g1 + cw_ref[2:3, :] * gate + cb_ref[...]
    act = (conv * (1.0 / (1.0 + jnp.exp(-conv))) * up).astype(BF16)
    part = _dot(act, wd_ref[...])

    @pl.when(f == 0)
    def _():
        acc_ref[...] = part

    @pl.when(f > 0)
    def _():
        acc_ref[...] += part

    @pl.when(f == pl.num_programs(1) - 1)
    def _():
        o_ref[...] = _layer_norm(alpha * x + acc_ref[...], g_ref[...], b_ref[...])


def _ffn(xf, wg, wu, cw, cb, wd, g, b, alpha, S, tm, fc):
    T, D = xf.shape
    dff = wg.shape[1]
    tok = lambda i, f: (i, 0)
    const = lambda i, f: (0, 0)
    return pl.pallas_call(
        functools.partial(_ffn_body, alpha=alpha, seq_tiles=S // tm),
        grid=(T // tm, dff // fc),
        in_specs=[
            pl.BlockSpec((tm, D), tok),
            pl.BlockSpec((HALO, D), lambda i, f: (jnp.maximum(i * (tm // HALO) - 1, 0), 0)),
            pl.BlockSpec((D, fc), lambda i, f: (0, f)),
            pl.BlockSpec((D, fc), lambda i, f: (0, f)),
            pl.BlockSpec((CONV_WIDTH, fc), lambda i, f: (0, f)),
            pl.BlockSpec((1, fc), lambda i, f: (0, f)),
            pl.BlockSpec((fc, D), lambda i, f: (f, 0)),
            pl.BlockSpec((1, D), const),
            pl.BlockSpec((1, D), const),
        ],
        out_specs=pl.BlockSpec((tm, D), tok),
        out_shape=jax.ShapeDtypeStruct((T, D), F32),
        scratch_shapes=[pltpu.VMEM((tm, D), F32)],
        compiler_params=_params("arbitrary", "arbitrary"),
        name="ffn",
    )(xf, xf, wg, wu, cw, cb, wd, g, b)


def _inv_freq_row(dim, lane_lo, lane_hi, period):
    inv = ROPE_THETA ** (-np.arange(0, dim, 2, dtype=np.float64) / dim)
    row = np.zeros((1, LANES), np.float32)
    for lane in range(lane_lo, lane_hi):
        row[0, lane] = inv[(lane % period) % (dim // 2)]
    return jnp.asarray(row)


def _cover_t(S):
    nc = S // CMP_STRIDE
    ns = S // SLC_LEN
    cs = np.arange(nc)[:, None] * CMP_STRIDE
    ss = np.arange(ns)[None, :] * SLC_LEN
    cover = np.clip(np.minimum(cs + CMP_LEN, ss + SLC_LEN) - np.maximum(cs, ss), 0, None) / CMP_LEN
    cover[nc - 1:] = 0.0
    return jnp.asarray(cover.T, dtype=BF16)


def _eye_aug():
    eye = np.zeros((NSA_REP * SLC_LEN, LANES), np.float32)
    r = np.arange(NSA_REP * SLC_LEN)
    eye[r, SLC_LEN + (r % SLC_LEN)] = 1.0
    return jnp.asarray(eye, dtype=BF16)


def _permute_w_in(w):
    D = w.shape[0]
    c1 = NSA_HEADS * NSA_DH
    c2 = c1 + 3 * 2 * NSA_GROUPS * NSA_DH
    c3 = c2 + 3 * NSA_HEADS
    c4 = c3 + MLA_Q_RANK
    c5 = c4 + MLA_KV_RANK
    c6 = c5 + MLA_ROPE
    pad = jnp.zeros((D, IN_COLS_PAD - C_MISC - MLA_ROPE - 3 * NSA_HEADS), w.dtype)
    return jnp.concatenate(
        [w[:, :c2], w[:, c3:c5], w[:, c5:c6], w[:, c2:c3], pad], axis=1).astype(BF16)


def _permute_w_uq(w):
    r = w.shape[0]
    w3 = w.reshape(r, MLA_HEADS, MLA_NOPE + MLA_ROPE)
    pad = jnp.zeros((r, MLA_HEADS, LANES - MLA_NOPE - MLA_ROPE), w.dtype)
    return jnp.concatenate([w3, pad], axis=2).reshape(r, MLA_HEADS * LANES).astype(BF16)


def _permute_w_ukv(w):
    r = w.shape[0]
    w3 = w.reshape(r, MLA_HEADS, MLA_NOPE + MLA_V)
    return jnp.concatenate([w3[:, :, :MLA_NOPE].reshape(r, -1),
                            w3[:, :, MLA_NOPE:].reshape(r, -1)], axis=1).astype(BF16)


def _pad_lanes(a):
    return jnp.concatenate([a, jnp.zeros(a.shape[:-1] + (LANES - a.shape[-1],), a.dtype)], axis=-1)


def kernel(x, mem, positions, w_in, nsa_k_pos, nsa_ck_w1, nsa_ck_b1, nsa_ck_w2, nsa_ck_b2,
           nsa_v_pos, nsa_cv_w1, nsa_cv_b1, nsa_cv_w2, nsa_cv_b2,
           mla_q_norm, mla_w_uq, mla_kv_norm, mla_w_ukv, w_o, ln1_g, ln1_b,
           mem_wq, mem_wk, mem_wv, mem_wo, ln2_g, ln2_b,
           ffn_w_up, ffn_conv_w, ffn_conv_b, ffn_w_down, ln3_g, ln3_b):
    B, S, D = x.shape
    T = B * S
    depth = w_in.shape[0]
    alpha = (2.0 * depth) ** 0.25
    d_ff = ffn_w_down.shape[1]
    tm = min(512, S)
    assert S % SLC_CHUNK == 0 and S >= WIN_SPAN and S % tm == 0
    assert (B * mem.shape[1]) % 256 == 0

    pos = positions.reshape(T, 1)
    pos_cmp = positions[:, CMP_LEN - 1::CMP_STRIDE]
    pos_cmp = jnp.concatenate([pos_cmp, pos_cmp[:, -1:]], axis=1)[:, :, None]
    inv_nsa = _inv_freq_row(NSA_DH, 0, LANES, NSA_DH)
    inv_cmp = _inv_freq_row(NSA_DH, 0, NSA_DH, NSA_DH)
    inv_mla = _inv_freq_row(MLA_ROPE, MLA_NOPE, MLA_NOPE + MLA_ROPE, MLA_ROPE)
    cov_t = _cover_t(S)
    eye = _eye_aug()
    memf = mem.reshape(B * mem.shape[1], D)

    xf = x.reshape(T, D)
    for l in range(depth):
        qn, kvn, kvc, lat, misc = _inproj(pos, inv_nsa, xf, _permute_w_in(w_in[l]), B, S, tm)
        kvcmp = _compress(
            pos_cmp, inv_cmp, kvc,
            jnp.stack([nsa_k_pos[l].reshape(1, -1), nsa_v_pos[l].reshape(1, -1)]),
            jnp.stack([nsa_ck_w1[l], nsa_cv_w1[l]]).astype(BF16),
            jnp.stack([nsa_ck_b1[l], nsa_cv_b1[l]])[:, None, :],
            _pad_lanes(jnp.stack([nsa_ck_w2[l], nsa_cv_w2[l]])).astype(BF16),
            _pad_lanes(jnp.stack([nsa_ck_b2[l], nsa_cv_b2[l]]))[:, None, :],
            B, S)
        q_m, k_m, v_m = _mla_up(pos, inv_mla, lat, misc, mla_q_norm[l][None, :],
                                mla_kv_norm[l][None, :], _permute_w_uq(mla_w_uq[l]),
                                _permute_w_ukv(mla_w_ukv[l]), B, S, tm)
        o_nsa = _nsa(qn, kvn, kvcmp, misc, cov_t, eye, B, S)
        o_mla = _mla(q_m, k_m, v_m, B, S)
        xf = _mix_ln(xf, o_nsa.reshape(T, -1), o_mla.reshape(T, -1), w_o[l].astype(BF16),
                     ln1_g[l][None, :], ln1_b[l][None, :], alpha, tm)
        k_mem, v_mem = _mem_kv(memf, mem_wk[l].astype(BF16), mem_wv[l].astype(BF16), 256)
        xf = _mem_attn(xf, k_mem.reshape(B, -1, D), v_mem.reshape(B, -1, D),
                       mem_wq[l].astype(BF16), mem_wo[l].astype(BF16),
                       ln2_g[l][None, :], ln2_b[l][None, :], alpha, S, tm)
        xf = _ffn(xf, ffn_w_up[l][:, :d_ff].astype(BF16), ffn_w_up[l][:, d_ff:].astype(BF16),
                  ffn_conv_w[l], ffn_conv_b[l][None, :], ffn_w_down[l].astype(BF16),
                  ln3_g[l][None, :], ln3_b[l][None, :], alpha, S, tm, d_ff // 2)
    return xf.reshape(B, S, D)
```

```python
import functools
import math

import numpy as np
import jax
import jax.numpy as jnp
from jax import lax
from jax.experimental import pallas as pl
from jax.experimental.pallas import tpu as pltpu

F32 = jnp.float32
BF16 = jnp.bfloat16

NSA_HEADS = 8
NSA_GROUPS = 2
NSA_REP = NSA_HEADS // NSA_GROUPS
NSA_DH = 64
CMP_STRIDE = 16
CMP_LEN = 32
SLC_LEN = 64
TOPN = 16
WINDOW = 512
CMP_HIDDEN = 128
FORCE_BONUS = 1e4
MLA_HEADS = 8
MLA_Q_RANK = 384
MLA_KV_RANK = 256
MLA_NOPE = 64
MLA_ROPE = 32
MLA_V = 64
MEM_HEADS = 4
CONV_WIDTH = 3
ROPE_THETA = 10000.0
LN_EPS = 1e-5
RMS_EPS = 1e-6
NEG_INF = -1e30
LOG2E = math.log2(math.e)

LANES = 128
VMEM_LIMIT = 56 * 1024 * 1024

C_Q = 0
C_KVC = 512
C_KVN = 768
C_LAT = 1280
C_MISC = 1920
IN_COLS_PAD = 2048
GATE_LANE0 = MLA_ROPE
ONES_LANE = 64


def _dot(a, b):
    return jnp.dot(a, b, preferred_element_type=F32)


def _dot_nt(a, b):
    return lax.dot_general(a, b, (((1,), (1,)), ((), ())), preferred_element_type=F32)


def _layer_norm(y, g, b):
    mu = jnp.mean(y, axis=-1, keepdims=True)
    d = y - mu
    var = jnp.mean(d * d, axis=-1, keepdims=True)
    return d * lax.rsqrt(var + LN_EPS) * g + b


def _params(*sem):
    return pltpu.CompilerParams(dimension_semantics=sem, vmem_limit_bytes=VMEM_LIMIT)


def _rope_tables(pos_col, inv_row, half):
    ang = pos_col * inv_row
    cos = jnp.cos(ang)
    sin = jnp.sin(ang)
    lane = lax.broadcasted_iota(jnp.int32, (1, LANES), 1)
    upper = (lane & (2 * half - 1)) >= half
    rot = inv_row != 0.0
    sin_hi = jnp.where(upper & rot, sin, 0.0)
    sin_lo = jnp.where(upper | (~rot), 0.0, -sin)
    return cos, sin_hi, sin_lo


def _apply_rope(v, tabs, half):
    cos, sin_hi, sin_lo = tabs
    return v * cos + pltpu.roll(v, half, 1) * sin_hi + pltpu.roll(v, LANES - half, 1) * sin_lo


def _inproj_body(pos_ref, inv_ref, x_ref, w_ref, qn_ref, kvn_ref, kvc_ref, lat_ref, misc_ref):
    xb = x_ref[...].astype(BF16)
    tabs = _rope_tables(pos_ref[...].astype(F32), inv_ref[...], NSA_DH // 2)
    lane = lax.broadcasted_iota(jnp.int32, (1, LANES), 1)
    low = lane < NSA_DH
    qscale = NSA_DH ** -0.5 * LOG2E

    def proj(c0, n):
        return _dot(xb, w_ref[:, c0:c0 + n])

    def split_store(v, ref, idx_lo, idx_hi, pad=0.0):
        ref[0, idx_lo] = jnp.where(low, v, pad).astype(BF16)
        ref[0, idx_hi] = jnp.where(low, pltpu.roll(v, NSA_DH, 1), pad).astype(BF16)

    ones_pad = jnp.where(lane == ONES_LANE, 1.0, 0.0)

    for slab in range(2):
        h = proj(C_Q + 256 * slab, 256)
        for j in range(2):
            r = _apply_rope(h[:, LANES * j:LANES * (j + 1)], tabs, NSA_DH // 2) * qscale
            split_store(r, qn_ref, 4 * slab + 2 * j, 4 * slab + 2 * j + 1)

    h = proj(C_KVC, 256)
    for j in range(2):
        v = h[:, LANES * j:LANES * (j + 1)]
        kvc_ref[2 * j] = v[:, :NSA_DH]
        kvc_ref[2 * j + 1] = pltpu.roll(v, NSA_DH, 1)[:, :NSA_DH]

    for slab in range(2):
        h = proj(C_KVN + 256 * slab, 256)
        k = _apply_rope(h[:, :LANES], tabs, NSA_DH // 2)
        split_store(k, kvn_ref, 4 * slab, 4 * slab + 1)
        split_store(h[:, LANES:], kvn_ref, 4 * slab + 2, 4 * slab + 3, ones_pad)

    for c0 in range(C_LAT, C_MISC, 128):
        lat_ref[:, c0 - C_LAT:c0 - C_LAT + 128] = proj(c0, 128)
    misc_ref[...] = proj(C_MISC, 128)


def _inproj(pos, inv_nsa, xf, w_in_p, B, S, tm):
    T = B * S
    nst = S // tm
    tok = lambda i: (i, 0)
    const = lambda i: (0, 0)
    head_blk = lambda i: (i // nst, 0, i % nst, 0)
    return pl.pallas_call(
        _inproj_body,
        grid=(T // tm,),
        in_specs=[
            pl.BlockSpec((tm, 1), tok),
            pl.BlockSpec((1, LANES), const),
            pl.BlockSpec((tm, xf.shape[1]), tok),
            pl.BlockSpec(w_in_p.shape, const),
        ],
        out_specs=[
            pl.BlockSpec((1, 8, tm, LANES), head_blk),
            pl.BlockSpec((1, 8, tm, LANES), head_blk),
            pl.BlockSpec((4, tm, NSA_DH), lambda i: (0, i, 0)),
            pl.BlockSpec((tm, C_MISC - C_LAT), tok),
            pl.BlockSpec((tm, LANES), tok),
        ],
        out_shape=[
            jax.ShapeDtypeStruct((B, 8, S, LANES), BF16),
            jax.ShapeDtypeStruct((B, 8, S, LANES), BF16),
            jax.ShapeDtypeStruct((4, T, NSA_DH), F32),
            jax.ShapeDtypeStruct((T, C_MISC - C_LAT), F32),
            jax.ShapeDtypeStruct((T, LANES), F32),
        ],
        compiler_params=_params("arbitrary"),
        name="inproj",
    )(pos, inv_nsa, xf, w_in_p)


def _compress_body(pos_ref, inv_ref, x_ref, pe_ref, w1_ref, b1_ref, w2_ref, b2_ref, o_ref):
    is_k = pl.program_id(0) < NSA_GROUPS
    x = x_ref[0, 0]
    half = CMP_STRIDE * NSA_DH
    a1 = _dot((x + pe_ref[0, :, :half]).astype(BF16), w1_ref[0, :half, :])
    a2 = _dot((x + pe_ref[0, :, half:]).astype(BF16), w1_ref[0, half:, :])
    nch = x.shape[0]
    pre = a1 + pltpu.roll(a2, nch - 1, 0) + b1_ref[0]
    hid = jax.nn.gelu(pre, approximate=True)
    out = _dot(hid.astype(BF16), w2_ref[0]) + b2_ref[0]
    tabs = _rope_tables(pos_ref[0].astype(F32), inv_ref[...], NSA_DH // 2)
    roped = _apply_rope(out, tabs, NSA_DH // 2)
    out = jnp.where(is_k, roped, out)
    row = lax.broadcasted_iota(jnp.int32, (nch, 1), 0)
    o_ref[0, 0] = jnp.where(row < nch - 1, out, 0.0).astype(BF16)


def _compress(pos_cmp, inv_nsa, kvc, pe, w1, b1, w2, b2, B, S):
    nch = S // CMP_STRIDE
    flat = CMP_STRIDE * NSA_DH
    x = kvc.reshape(4, B, nch, flat)
    kv = lambda j, b: (j // NSA_GROUPS, 0, 0)
    return pl.pallas_call(
        _compress_body,
        grid=(4, B),
        in_specs=[
            pl.BlockSpec((1, nch, 1), lambda j, b: (b, 0, 0)),
            pl.BlockSpec((1, LANES), lambda j, b: (0, 0)),
            pl.BlockSpec((1, 1, nch, flat), lambda j, b: (j, b, 0, 0)),
            pl.BlockSpec((1, 1, 2 * flat), kv),
            pl.BlockSpec((1, 2 * flat, CMP_HIDDEN), kv),
            pl.BlockSpec((1, 1, CMP_HIDDEN), kv),
            pl.BlockSpec((1, CMP_HIDDEN, LANES), kv),
            pl.BlockSpec((1, 1, LANES), kv),
        ],
        out_specs=pl.BlockSpec((1, 1, nch, LANES), lambda j, b: (j, b, 0, 0)),
        out_shape=jax.ShapeDtypeStruct((4, B, nch, LANES), BF16),
        compiler_params=_params("arbitrary", "arbitrary"),
        name="compress",
    )(pos_cmp, inv_nsa, x, pe, w1, b1, w2, b2)


def _rms_norm(v, g):
    return v * lax.rsqrt(jnp.mean(v * v, axis=-1, keepdims=True) + RMS_EPS) * g


def _mla_up_body(pos_ref, inv_ref, lat_ref, misc_ref, gq_ref, gkv_ref, wq_ref, wkv_ref,
                 q_ref, k_ref, v_ref):
    tabs = _rope_tables(pos_ref[...].astype(F32), inv_ref[...], MLA_ROPE // 2)
    qscale = (MLA_NOPE + MLA_ROPE) ** -0.5 * LOG2E
    lane = lax.broadcasted_iota(jnp.int32, (1, LANES), 1)
    nope = lane < MLA_NOPE
    pe_lanes = (lane >= MLA_NOPE) & (lane < MLA_NOPE + MLA_ROPE)
    ones_pad = jnp.where(lane == ONES_LANE, 1.0, 0.0)

    qn = _rms_norm(lat_ref[:, :MLA_Q_RANK], gq_ref[...]).astype(BF16)
    kvn = _rms_norm(lat_ref[:, MLA_Q_RANK:], gkv_ref[...]).astype(BF16)
    kpe = pltpu.roll(misc_ref[...], MLA_NOPE, 1)
    kpe = jnp.where(pe_lanes, _apply_rope(kpe, tabs, MLA_ROPE // 2), 0.0)

    for slab in range(MLA_HEADS // 2):
        hq = _dot(qn, wq_ref[:, 256 * slab:256 * (slab + 1)])
        hk = _dot(kvn, wkv_ref[:, 128 * slab:128 * (slab + 1)])
        hv = _dot(kvn, wkv_ref[:, 512 + 128 * slab:512 + 128 * (slab + 1)])
        for j in range(2):
            q = _apply_rope(hq[:, LANES * j:LANES * (j + 1)], tabs, MLA_ROPE // 2) * qscale
            q_ref[0, 2 * slab + j] = q.astype(BF16)
            kn = hk if j == 0 else pltpu.roll(hk, MLA_NOPE, 1)
            k_ref[0, 2 * slab + j] = jnp.where(nope, kn, kpe).astype(BF16)
            vn = hv if j == 0 else pltpu.roll(hv, MLA_V, 1)
            v_ref[0, 2 * slab + j] = jnp.where(nope, vn, ones_pad).astype(BF16)


def _mla_up(pos, inv_mla, lat, misc, gq, gkv, wq_p, wkv_p, B, S, tm):
    T = B * S
    nst = S // tm
    tok = lambda i: (i, 0)
    const = lambda i: (0, 0)
    head_blk = lambda i: (i // nst, 0, i % nst, 0)
    return pl.pallas_call(
        _mla_up_body,
        grid=(T // tm,),
        in_specs=[
            pl.BlockSpec((tm, 1), tok),
            pl.BlockSpec((1, LANES), const),
            pl.BlockSpec((tm, lat.shape[1]), tok),
            pl.BlockSpec((tm, LANES), tok),
            pl.BlockSpec(gq.shape, const),
            pl.BlockSpec(gkv.shape, const),
            pl.BlockSpec(wq_p.shape, const),
            pl.BlockSpec(wkv_p.shape, const),
        ],
        out_specs=[
            pl.BlockSpec((1, MLA_HEADS, tm, LANES), head_blk),
            pl.BlockSpec((1, MLA_HEADS, tm, LANES), head_blk),
            pl.BlockSpec((1, MLA_HEADS, tm, LANES), head_blk),
        ],
        out_shape=[
            jax.ShapeDtypeStruct((B, MLA_HEADS, S, LANES), BF16),
            jax.ShapeDtypeStruct((B, MLA_HEADS, S, LANES), BF16),
            jax.ShapeDtypeStruct((B, MLA_HEADS, S, LANES), BF16),
        ],
        compiler_params=_params("arbitrary"),
        name="mla_up",
    )(pos, inv_mla, lat, misc, gq, gkv, wq_p, wkv_p)


STRIP = 64


def _lane_tile(col, n):
    reps = [col] * (n // LANES)
    if n % LANES:
        reps.append(col[:, :n % LANES])
    return reps[0] if len(reps) == 1 else jnp.concatenate(reps, axis=1)


def _flash_reset(m_ref, acc_ref):
    m_ref[...] = jnp.full(m_ref.shape, NEG_INF, F32)
    acc_ref[...] = jnp.zeros(acc_ref.shape, F32)


def _flash_update(s_ref, v, m_ref, acc_ref, p_ref, mask=None):
    rows, n = s_ref.shape
    for r in range(rows // STRIP):
        rs = slice(STRIP * r, STRIP * (r + 1))
        s = s_ref[rs, :]
        if mask is not None:
            s = jnp.where(mask[rs], s, NEG_INF)
        m_old = m_ref[rs, :]
        m_new = jnp.maximum(m_old, jnp.max(s, axis=1, keepdims=True))
        p_ref[rs, :] = jnp.exp2(s - _lane_tile(m_new, n)).astype(BF16)
        acc_ref[rs, :] = jnp.exp2(m_old - m_new) * acc_ref[rs, :]
        m_ref[rs, :] = m_new
    acc_ref[...] += _dot(p_ref[...], v)


def _flash_finish(acc):
    return acc * (1.0 / acc[:, ONES_LANE:ONES_LANE + 1])


NSA_TQ = 128
NSA_ROWS = NSA_REP * NSA_TQ
SLC_CHUNK = 256
WIN_SPAN = WINDOW + NSA_TQ
BIAS_LANE0 = LANES


def _nsa_body(q_ref, kvn_ref, kvc_ref, misc_ref, cov_ref, eye_ref, o_ref,
              kaug_ref, score_ref, bias_ref, sa_ref, sb_ref, pa_ref, pb_ref, ms_ref, accs_ref,
              sw_ref, pw_ref, mw_ref, accw_ref, sc_ref, pn_ref, pc_ref, oc_ref):
    c = pl.program_id(1)
    rows = NSA_ROWS
    t_row = c * NSA_TQ + (lax.broadcasted_iota(jnp.int32, (rows, 1), 0) & (NSA_TQ - 1))
    ncmp = kvc_ref.shape[2]
    nblk = kaug_ref.shape[1] // SLC_LEN

    @pl.when(c == 0)
    def _():
        for g in range(NSA_GROUPS):
            kaug_ref[g, :, :BIAS_LANE0] = kvn_ref[0, g]
            kaug_ref[g, :, BIAS_LANE0:] = jnp.zeros((kaug_ref.shape[1], NSA_TQ), BF16)

    qs = [q_ref[0, NSA_REP * g:NSA_REP * (g + 1)].reshape(rows, LANES) for g in range(NSA_GROUPS)]

    cmp_valid = (CMP_STRIDE * lax.broadcasted_iota(jnp.int32, (1, ncmp), 1) + CMP_LEN - 1) <= t_row
    for g in range(NSA_GROUPS):
        sc_ref[g] = _dot_nt(qs[g], kvc_ref[g, 0])
    for g in range(NSA_GROUPS):
        for r in range(rows // STRIP):
            rs = slice(STRIP * r, STRIP * (r + 1))
            s = jnp.where(cmp_valid[rs], sc_ref[g, rs, :], NEG_INF)
            p = jnp.where(cmp_valid[rs], jnp.exp2(s - jnp.max(s, axis=1, keepdims=True)), 0.0)
            l = jnp.sum(p, axis=1, keepdims=True)
            p = p * jnp.where(l > 0.0, 1.0 / l, 0.0)
            pn_ref[g, rs, :] = p
            pc_ref[g, rs, :] = p.astype(BF16)
        oc_ref[g] = _dot(pc_ref[g], kvc_ref[NSA_GROUPS + g, 0])

    _flash_reset(mw_ref, accw_ref)
    win_start = pl.multiple_of(jnp.maximum(c * NSA_TQ - WINDOW, 0), NSA_TQ)
    diff = t_row - (win_start + lax.broadcasted_iota(jnp.int32, (1, WIN_SPAN), 1))
    win_valid = (diff >= 0) & (diff < WINDOW)
    for g in range(NSA_GROUPS):
        sw_ref[g] = _dot_nt(qs[g], kvn_ref[0, 4 + g, pl.ds(win_start, WIN_SPAN), :])
    for g in range(NSA_GROUPS):
        _flash_update(sw_ref.at[g], kvn_ref[0, 6 + g, pl.ds(win_start, WIN_SPAN), :],
                      mw_ref.at[g], accw_ref.at[g], pw_ref.at[g], win_valid)

    @pl.when((c + 1) * (NSA_TQ // SLC_LEN) > TOPN)
    def _():
        width = NSA_GROUPS * NSA_TQ
        ps = jnp.concatenate(
            [sum(pn_ref[g, NSA_TQ * r:NSA_TQ * (r + 1), :] for r in range(NSA_REP))
             for g in range(NSA_GROUPS)], axis=0)
        hi = ps.astype(BF16)
        lo = (ps - hi.astype(F32)).astype(BF16)
        imp = _dot_nt(cov_ref[...], hi) + _dot_nt(cov_ref[...], lo)
        jidx = lax.broadcasted_iota(jnp.int32, (nblk, width), 0)
        lane_q = lax.broadcasted_iota(jnp.int32, (1, width), 1) & (NSA_TQ - 1)
        cur = c * (NSA_TQ // SLC_LEN) + lane_q // SLC_LEN
        forced = (jidx == 0) | (jidx == cur) | (jidx == cur - 1)
        score = jnp.where(jidx <= cur, jnp.where(forced, FORCE_BONUS, imp), NEG_INF)
        score_ref[...] = score
        sub = 8
        cnt = [jnp.zeros((sub, width), F32) for _ in range(nblk // sub)]
        tiles = [score[sub * v:sub * (v + 1)] for v in range(nblk // sub)]
        sidx = lax.broadcasted_iota(jnp.int32, (sub, width), 0)
        for jp in range(nblk):
            rowv = jnp.broadcast_to(score_ref[jp:jp + 1, :], (sub, width))
            for v in range(nblk // sub):
                if sub * v > jp:
                    cnt[v] = jnp.where(rowv >= tiles[v], cnt[v] + 1.0, cnt[v])
                elif sub * v + sub - 1 <= jp:
                    cnt[v] = jnp.where(rowv > tiles[v], cnt[v] + 1.0, cnt[v])
                else:
                    ge = jnp.where(rowv >= tiles[v], cnt[v] + 1.0, cnt[v])
                    gt = jnp.where(rowv > tiles[v], cnt[v] + 1.0, cnt[v])
                    cnt[v] = jnp.where(sidx + sub * v > jp, ge, gt)
        rank = jnp.concatenate(cnt, axis=0)
        bias = jnp.where(rank < float(TOPN), 0.0, NEG_INF)
        for g in range(NSA_GROUPS):
            bias_ref[g] = bias[:, NSA_TQ * g:NSA_TQ * (g + 1)]

        def write_bias(j, carry):
            r0 = pl.multiple_of(j * SLC_LEN, SLC_LEN)
            for g in range(NSA_GROUPS):
                blk = jnp.broadcast_to(bias_ref[g, pl.ds(j, 1), :], (SLC_LEN, NSA_TQ))
                kaug_ref[g, pl.ds(r0, SLC_LEN), BIAS_LANE0:] = blk.astype(BF16)
            return carry
        lax.fori_loop(0, (c + 1) * (NSA_TQ // SLC_LEN), write_bias, 0)

    _flash_reset(ms_ref, accs_ref)
    qas = [jnp.concatenate([q, eye_ref[...]], axis=1) for q in qs]
    nchunk = kaug_ref.shape[1] // SLC_CHUNK
    last = c // (SLC_CHUNK // NSA_TQ)

    def slc_scores(kc, dst):
        k0 = pl.multiple_of(jnp.minimum(kc, nchunk - 1) * SLC_CHUNK, SLC_CHUNK)
        for g in range(NSA_GROUPS):
            dst[g] = _dot_nt(qas[g], kaug_ref[g, pl.ds(k0, SLC_CHUNK), :])

    def slc_update(src, p_ref, kc, causal):
        k0 = pl.multiple_of(jnp.minimum(kc, nchunk - 1) * SLC_CHUNK, SLC_CHUNK)
        mask = None
        if causal:
            mask = (kc * SLC_CHUNK + lax.broadcasted_iota(jnp.int32, (1, SLC_CHUNK), 1)) <= t_row
        for g in range(NSA_GROUPS):
            _flash_update(src.at[g], kvn_ref[0, 2 + g, pl.ds(k0, SLC_CHUNK), :],
                          ms_ref.at[g], accs_ref.at[g], p_ref.at[g], mask)

    slc_scores(0, sa_ref)

    def slc_pair(i, carry):
        slc_scores(2 * i + 1, sb_ref)
        slc_update(sa_ref, pa_ref, 2 * i, False)
        slc_scores(2 * i + 2, sa_ref)
        slc_update(sb_ref, pb_ref, 2 * i + 1, False)
        return carry
    lax.fori_loop(0, last // 2, slc_pair, 0)
    tail = 2 * (last // 2)
    slc_scores(tail + 1, sb_ref)
    slc_update(sa_ref, pa_ref, tail, True)
    slc_update(sb_ref, pb_ref, tail + 1, True)

    sig = 1.0 / (1.0 + jnp.exp(-misc_ref[...]))
    outs = []
    for g in range(NSA_GROUPS):
        for r in range(NSA_REP):
            lane0 = GATE_LANE0 + 3 * (NSA_REP * g + r)
            rs = slice(NSA_TQ * r, NSA_TQ * (r + 1))
            o = (sig[:, lane0:lane0 + 1] * oc_ref[g, rs, :]
                 + sig[:, lane0 + 1:lane0 + 2] * _flash_finish(accs_ref[g, rs, :])
                 + sig[:, lane0 + 2:lane0 + 3] * _flash_finish(accw_ref[g, rs, :]))
            outs.append(o[:, :NSA_DH])
    o_ref[0] = jnp.concatenate(outs, axis=1).astype(BF16)


def _nsa(qn, kvn, kvcmp, misc, cov_t, eye, B, S):
    nblk = S // SLC_LEN
    ncmp = S // CMP_STRIDE
    nq = S // NSA_TQ
    rows = NSA_ROWS
    return pl.pallas_call(
        _nsa_body,
        grid=(B, nq),
        in_specs=[
            pl.BlockSpec((1, NSA_HEADS, NSA_TQ, LANES), lambda b, c: (b, 0, c, 0)),
            pl.BlockSpec((1, 8, S, LANES), lambda b, c: (b, 0, 0, 0)),
            pl.BlockSpec((4, 1, ncmp, LANES), lambda b, c: (0, b, 0, 0)),
            pl.BlockSpec((NSA_TQ, LANES), lambda b, c: (b * nq + c, 0)),
            pl.BlockSpec(cov_t.shape, lambda b, c: (0, 0)),
            pl.BlockSpec(eye.shape, lambda b, c: (0, 0)),
        ],
        out_specs=pl.BlockSpec((1, NSA_TQ, NSA_HEADS * NSA_DH), lambda b, c: (b, c, 0)),
        out_shape=jax.ShapeDtypeStruct((B, S, NSA_HEADS * NSA_DH), BF16),
        scratch_shapes=[
            pltpu.VMEM((NSA_GROUPS, S, BIAS_LANE0 + NSA_TQ), BF16),
            pltpu.VMEM((nblk, NSA_GROUPS * NSA_TQ), F32),
            pltpu.VMEM((NSA_GROUPS, nblk, NSA_TQ), F32),
            pltpu.VMEM((NSA_GROUPS, rows, SLC_CHUNK), F32),
            pltpu.VMEM((NSA_GROUPS, rows, SLC_CHUNK), F32),
            pltpu.VMEM((NSA_GROUPS, rows, SLC_CHUNK), BF16),
            pltpu.VMEM((NSA_GROUPS, rows, SLC_CHUNK), BF16),
            pltpu.VMEM((NSA_GROUPS, rows, LANES), F32),
            pltpu.VMEM((NSA_GROUPS, rows, LANES), F32),
            pltpu.VMEM((NSA_GROUPS, rows, WIN_SPAN), F32),
            pltpu.VMEM((NSA_GROUPS, rows, WIN_SPAN), BF16),
            pltpu.VMEM((NSA_GROUPS, rows, LANES), F32),
            pltpu.VMEM((NSA_GROUPS, rows, LANES), F32),
            pltpu.VMEM((NSA_GROUPS, rows, ncmp), F32),
            pltpu.VMEM((NSA_GROUPS, rows, ncmp), F32),
            pltpu.VMEM((NSA_GROUPS, rows, ncmp), BF16),
            pltpu.VMEM((NSA_GROUPS, rows, LANES), F32),
        ],
        compiler_params=_params("arbitrary", "arbitrary"),
        name="nsa",
    )(qn, kvn, kvcmp, misc, cov_t, eye)


MLA_TQ = 256
MLA_HPB = 4


def _mla_body(q_ref, k_ref, v_ref, o_ref, sa_ref, sb_ref, pa_ref, pb_ref, m_ref, acc_ref):
    qi = pl.program_id(2)
    nchunk = k_ref.shape[2] // MLA_TQ
    t_row = qi * MLA_TQ + lax.broadcasted_iota(jnp.int32, (MLA_TQ, 1), 0)
    _flash_reset(m_ref, acc_ref)

    def scores(kc, dst):
        k0 = pl.multiple_of(jnp.minimum(kc, nchunk - 1) * MLA_TQ, MLA_TQ)
        for j in range(MLA_HPB):
            dst[j] = _dot_nt(q_ref[0, j], k_ref[0, j, pl.ds(k0, MLA_TQ), :])

    def update(src, p_ref, kc, causal):
        k0 = pl.multiple_of(jnp.minimum(kc, nchunk - 1) * MLA_TQ, MLA_TQ)
        mask = None
        if causal:
            mask = (kc * MLA_TQ + lax.broadcasted_iota(jnp.int32, (1, MLA_TQ), 1)) <= t_row
        for j in range(MLA_HPB):
            _flash_update(src.at[j], v_ref[0, j, pl.ds(k0, MLA_TQ), :],
                          m_ref.at[j], acc_ref.at[j], p_ref.at[j], mask)

    scores(0, sa_ref)

    def pair(i, carry):
        scores(2 * i + 1, sb_ref)
        update(sa_ref, pa_ref, 2 * i, False)
        scores(2 * i + 2, sa_ref)
        update(sb_ref, pb_ref, 2 * i + 1, False)
        return carry
    lax.fori_loop(0, qi // 2, pair, 0)
    tail = 2 * (qi // 2)
    scores(tail + 1, sb_ref)
    update(sa_ref, pa_ref, tail, True)
    update(sb_ref, pb_ref, tail + 1, True)
    o_ref[0] = jnp.concatenate([_flash_finish(acc_ref[j])[:, :MLA_V] for j in range(MLA_HPB)],
                               axis=1).astype(BF16)


def _mla(q, k, v, B, S):
    return pl.pallas_call(
        _mla_body,
        grid=(B, MLA_HEADS // MLA_HPB, S // MLA_TQ),
        in_specs=[
            pl.BlockSpec((1, MLA_HPB, MLA_TQ, LANES), lambda b, h, i: (b, h, i, 0)),
            pl.BlockSpec((1, MLA_HPB, S, LANES), lambda b, h, i: (b, h, 0, 0)),
            pl.BlockSpec((1, MLA_HPB, S, LANES), lambda b, h, i: (b, h, 0, 0)),
        ],
        out_specs=pl.BlockSpec((1, MLA_TQ, MLA_HPB * MLA_V), lambda b, h, i: (b, i, h)),
        out_shape=jax.ShapeDtypeStruct((B, S, MLA_HEADS * MLA_V), BF16),
        scratch_shapes=[
            pltpu.VMEM((MLA_HPB, MLA_TQ, MLA_TQ), F32),
            pltpu.VMEM((MLA_HPB, MLA_TQ, MLA_TQ), F32),
            pltpu.VMEM((MLA_HPB, MLA_TQ, MLA_TQ), BF16),
            pltpu.VMEM((MLA_HPB, MLA_TQ, MLA_TQ), BF16),
            pltpu.VMEM((MLA_HPB, MLA_TQ, LANES), F32),
            pltpu.VMEM((MLA_HPB, MLA_TQ, LANES), F32),
        ],
        compiler_params=_params("arbitrary", "arbitrary", "arbitrary"),
        name="mla",
    )(q, k, v)


def _mix_ln_body(x_ref, on_ref, om_ref, w_ref, g_ref, b_ref, o_ref, *, alpha):
    half = on_ref.shape[1]
    mix = _dot(on_ref[...], w_ref[:half, :]) + _dot(om_ref[...], w_ref[half:, :])
    o_ref[...] = _layer_norm(alpha * x_ref[...] + mix, g_ref[...], b_ref[...])


def _mix_ln(xf, o_nsa, o_mla, w_o, g, b, alpha, tm):
    T, D = xf.shape
    tok = lambda i: (i, 0)
    const = lambda i: (0, 0)
    return pl.pallas_call(
        functools.partial(_mix_ln_body, alpha=alpha),
        grid=(T // tm,),
        in_specs=[
            pl.BlockSpec((tm, D), tok),
            pl.BlockSpec((tm, o_nsa.shape[1]), tok),
            pl.BlockSpec((tm, o_mla.shape[1]), tok),
            pl.BlockSpec(w_o.shape, const),
            pl.BlockSpec((1, D), const),
            pl.BlockSpec((1, D), const),
        ],
        out_specs=pl.BlockSpec((tm, D), tok),
        out_shape=jax.ShapeDtypeStruct((T, D), F32),
        compiler_params=_params("arbitrary"),
        name="mix_ln",
    )(xf, o_nsa, o_mla, w_o, g, b)


def _mem_kv_body(m_ref, wk_ref, wv_ref, k_ref, v_ref):
    mb = m_ref[...].astype(BF16)
    k_ref[...] = _dot(mb, wk_ref[...]).astype(BF16)
    v_ref[...] = _dot(mb, wv_ref[...]).astype(BF16)


def _mem_kv(memf, wk, wv, tm):
    R, D = memf.shape
    tok = lambda i: (i, 0)
    const = lambda i: (0, 0)
    return pl.pallas_call(
        _mem_kv_body,
        grid=(R // tm,),
        in_specs=[pl.BlockSpec((tm, D), tok), pl.BlockSpec(wk.shape, const),
                  pl.BlockSpec(wv.shape, const)],
        out_specs=[pl.BlockSpec((tm, D), tok), pl.BlockSpec((tm, D), tok)],
        out_shape=[jax.ShapeDtypeStruct((R, D), BF16), jax.ShapeDtypeStruct((R, D), BF16)],
        compiler_params=_params("arbitrary"),
        name="mem_kv",
    )(memf, wk, wv)


def _mem_attn_body(x_ref, k_ref, v_ref, wq_ref, wo_ref, g_ref, b_ref, o_ref, *, alpha):
    x = x_ref[...]
    D = x.shape[1]
    dh = D // MEM_HEADS
    q = (_dot(x.astype(BF16), wq_ref[...]) * (dh ** -0.5 * LOG2E)).astype(BF16)
    outs = []
    for h in range(MEM_HEADS):
        cs = slice(dh * h, dh * (h + 1))
        s = _dot_nt(q[:, cs], k_ref[0, :, cs])
        p = jnp.exp2(s - jnp.max(s, axis=1, keepdims=True))
        l = jnp.sum(p, axis=1, keepdims=True)
        outs.append((_dot(p.astype(BF16), v_ref[0, :, cs]) * (1.0 / l)).astype(BF16))
    o = jnp.concatenate(outs, axis=1)
    y = _dot(o, wo_ref[...])
    o_ref[...] = _layer_norm(alpha * x + y, g_ref[...], b_ref[...])


def _mem_attn(xf, k_mem, v_mem, wq, wo, g, b, alpha, S, tm):
    T, D = xf.shape
    nst = S // tm
    M = k_mem.shape[1]
    tok = lambda i: (i, 0)
    const = lambda i: (0, 0)
    memb = lambda i: (i // nst, 0, 0)
    return pl.pallas_call(
        functools.partial(_mem_attn_body, alpha=alpha),
        grid=(T // tm,),
        in_specs=[
            pl.BlockSpec((tm, D), tok),
            pl.BlockSpec((1, M, D), memb),
            pl.BlockSpec((1, M, D), memb),
            pl.BlockSpec(wq.shape, const),
            pl.BlockSpec(wo.shape, const),
            pl.BlockSpec((1, D), const),
            pl.BlockSpec((1, D), const),
        ],
        out_specs=pl.BlockSpec((tm, D), tok),
        out_shape=jax.ShapeDtypeStruct((T, D), F32),
        compiler_params=_params("arbitrary"),
        name="mem_attn",
    )(xf, k_mem, v_mem, wq, wo, g, b)


HALO = 8


def _ffn_body(x_ref, xh_ref, wg_ref, wu_ref, cw_ref, cb_ref, wd_ref, g_ref, b_ref, o_ref,
              acc_ref, *, alpha, seq_tiles):
    i = pl.program_id(0)
    f = pl.program_id(1)
    x = x_ref[...]
    xb = x.astype(BF16)
    tm = x.shape[0]
    gate = _dot(xb, wg_ref[...])
    up = _dot(xb, wu_ref[...])
    halo = _dot(xh_ref[...].astype(BF16), wg_ref[...])
    halo = jnp.where(i % seq_tiles == 0, 0.0, halo)
    row = lax.broadcasted_iota(jnp.int32, (tm, 1), 0)
    g1 = jnp.where(row == 0, halo[HALO - 1:HALO], pltpu.roll(gate, 1, 0))
    g2 = pltpu.roll(gate, 2, 0)
    g2 = jnp.where(row == 0, halo[HALO - 2:HALO - 1], jnp.where(row == 1, halo[HALO - 1:HALO], g2))
    conv = cw_ref[0:1, :] * g2 + cw_ref[1:2, :] * g1 + cw_ref[2:3, :] * gate + cb_ref[...]
    act = (conv * (1.0 / (1.0 + jnp.exp(-conv))) * up).astype(BF16)
    part = _dot(act, wd_ref[...])

    @pl.when(f == 0)
    def _():
        acc_ref[...] = part

    @pl.when(f > 0)
    def _():
        acc_ref[...] += part

    @pl.when(f == pl.num_programs(1) - 1)
    def _():
        o_ref[...] = _layer_norm(alpha * x + acc_ref[...], g_ref[...], b_ref[...])


def _ffn(xf, wg, wu, cw, cb, wd, g, b, alpha, S, tm, fc):
    T, D = xf.shape
    dff = wg.shape[1]
    tok = lambda i, f: (i, 0)
    const = lambda i, f: (0, 0)
    return pl.pallas_call(
        functools.partial(_ffn_body, alpha=alpha, seq_tiles=S // tm),
        grid=(T // tm, dff // fc),
        in_specs=[
            pl.BlockSpec((tm, D), tok),
            pl.BlockSpec((HALO, D), lambda i, f: (jnp.maximum(i * (tm // HALO) - 1, 0), 0)),
            pl.BlockSpec((D, fc), lambda i, f: (0, f)),
            pl.BlockSpec((D, fc), lambda i, f: (0, f)),
            pl.BlockSpec((CONV_WIDTH, fc), lambda i, f: (0, f)),
            pl.BlockSpec((1, fc), lambda i, f: (0, f)),
            pl.BlockSpec((fc, D), lambda i, f: (f, 0)),
            pl.BlockSpec((1, D), const),
            pl.BlockSpec((1, D), const),
        ],
        out_specs=pl.BlockSpec((tm, D), tok),
        out_shape=jax.ShapeDtypeStruct((T, D), F32),
        scratch_shapes=[pltpu.VMEM((tm, D), F32)],
        compiler_params=_params("arbitrary", "arbitrary"),
        name="ffn",
    )(xf, xf, wg, wu, cw, cb, wd, g, b)


def _inv_freq_row(dim, lane_lo, lane_hi, period):
    inv = ROPE_THETA ** (-np.arange(0, dim, 2, dtype=np.float64) / dim)
    row = np.zeros((1, LANES), np.float32)
    for lane in range(lane_lo, lane_hi):
        row[0, lane] = inv[(lane % period) % (dim // 2)]
    return jnp.asarray(row)


def _cover_t(S):
    nc = S // CMP_STRIDE
    ns = S // SLC_LEN
    cs = np.arange(nc)[:, None] * CMP_STRIDE
    ss = np.arange(ns)[None, :] * SLC_LEN
    cover = np.clip(np.minimum(cs + CMP_LEN, ss + SLC_LEN) - np.maximum(cs, ss), 0, None) / CMP_LEN
    cover[nc - 1:] = 0.0
    return jnp.asarray(cover.T, dtype=BF16)


def _eye_aug():
    eye = np.zeros((NSA_ROWS, NSA_TQ), np.float32)
    r = np.arange(NSA_ROWS)
    eye[r, r % NSA_TQ] = 1.0
    return jnp.asarray(eye, dtype=BF16)


def _permute_w_in(w):
    D = w.shape[0]
    c1 = NSA_HEADS * NSA_DH
    c2 = c1 + 3 * 2 * NSA_GROUPS * NSA_DH
    c3 = c2 + 3 * NSA_HEADS
    c4 = c3 + MLA_Q_RANK
    c5 = c4 + MLA_KV_RANK
    c6 = c5 + MLA_ROPE
    pad = jnp.zeros((D, IN_COLS_PAD - C_MISC - MLA_ROPE - 3 * NSA_HEADS), w.dtype)
    return jnp.concatenate(
        [w[:, :c2], w[:, c3:c5], w[:, c5:c6], w[:, c2:c3], pad], axis=1).astype(BF16)


def _permute_w_uq(w):
    r = w.shape[0]
    w3 = w.reshape(r, MLA_HEADS, MLA_NOPE + MLA_ROPE)
    pad = jnp.zeros((r, MLA_HEADS, LANES - MLA_NOPE - MLA_ROPE), w.dtype)
    return jnp.concatenate([w3, pad], axis=2).reshape(r, MLA_HEADS * LANES).astype(BF16)


def _permute_w_ukv(w):
    r = w.shape[0]
    w3 = w.reshape(r, MLA_HEADS, MLA_NOPE + MLA_V)
    return jnp.concatenate([w3[:, :, :MLA_NOPE].reshape(r, -1),
                            w3[:, :, MLA_NOPE:].reshape(r, -1)], axis=1).astype(BF16)


def _pad_lanes(a):
    return jnp.concatenate([a, jnp.zeros(a.shape[:-1] + (LANES - a.shape[-1],), a.dtype)], axis=-1)


def kernel(x, mem, positions, w_in, nsa_k_pos, nsa_ck_w1, nsa_ck_b1, nsa_ck_w2, nsa_ck_b2,
           nsa_v_pos, nsa_cv_w1, nsa_cv_b1, nsa_cv_w2, nsa_cv_b2,
           mla_q_norm, mla_w_uq, mla_kv_norm, mla_w_ukv, w_o, ln1_g, ln1_b,
           mem_wq, mem_wk, mem_wv, mem_wo, ln2_g, ln2_b,
           ffn_w_up, ffn_conv_w, ffn_conv_b, ffn_w_down, ln3_g, ln3_b):
    B, S, D = x.shape
    T = B * S
    depth = w_in.shape[0]
    alpha = (2.0 * depth) ** 0.25
    d_ff = ffn_w_down.shape[1]
    tm = min(512, S)
    assert S % SLC_CHUNK == 0 and S >= WIN_SPAN and S % tm == 0
    assert (B * mem.shape[1]) % 256 == 0

    pos = positions.reshape(T, 1)
    pos_cmp = positions[:, CMP_LEN - 1::CMP_STRIDE]
    pos_cmp = jnp.concatenate([pos_cmp, pos_cmp[:, -1:]], axis=1)[:, :, None]
    inv_nsa = _inv_freq_row(NSA_DH, 0, LANES, NSA_DH)
    inv_cmp = _inv_freq_row(NSA_DH, 0, NSA_DH, NSA_DH)
    inv_mla = _inv_freq_row(MLA_ROPE, MLA_NOPE, MLA_NOPE + MLA_ROPE, MLA_ROPE)
    cov_t = _cover_t(S)
    eye = _eye_aug()
    memf = mem.reshape(B * mem.shape[1], D)

    xf = x.reshape(T, D)
    for l in range(depth):
        qn, kvn, kvc, lat, misc = _inproj(pos, inv_nsa, xf, _permute_w_in(w_in[l]), B, S, tm)
        kvcmp = _compress(
            pos_cmp, inv_cmp, kvc,
            jnp.stack([nsa_k_pos[l].reshape(1, -1), nsa_v_pos[l].reshape(1, -1)]),
            jnp.stack([nsa_ck_w1[l], nsa_cv_w1[l]]).astype(BF16),
            jnp.stack([nsa_ck_b1[l], nsa_cv_b1[l]])[:, None, :],
            _pad_lanes(jnp.stack([nsa_ck_w2[l], nsa_cv_w2[l]])).astype(BF16),
            _pad_lanes(jnp.stack([nsa_ck_b2[l], nsa_cv_b2[l]]))[:, None, :],
            B, S)
        q_m, k_m, v_m = _mla_up(pos, inv_mla, lat, misc, mla_q_norm[l][None, :],
                                mla_kv_norm[l][None, :], _permute_w_uq(mla_w_uq[l]),
                                _permute_w_ukv(mla_w_ukv[l]), B, S, tm)
        o_nsa = _nsa(qn, kvn, kvcmp, misc, cov_t, eye, B, S)
        o_mla = _mla(q_m, k_m, v_m, B, S)
        xf = _mix_ln(xf, o_nsa.reshape(T, -1), o_mla.reshape(T, -1), w_o[l].astype(BF16),
                     ln1_g[l][None, :], ln1_b[l][None, :], alpha, tm)
        k_mem, v_mem = _mem_kv(memf, mem_wk[l].astype(BF16), mem_wv[l].astype(BF16), 256)
        xf = _mem_attn(xf, k_mem.reshape(B, -1, D), v_mem.reshape(B, -1, D),
                       mem_wq[l].astype(BF16), mem_wo[l].astype(BF16),
                       ln2_g[l][None, :], ln2_b[l][None, :], alpha, S, tm)
        xf = _ffn(xf, ffn_w_up[l][:, :d_ff].astype(BF16), ffn_w_up[l][:, d_ff:].astype(BF16),
                  ffn_conv_w[l], ffn_conv_b[l][None, :], ffn_w_down[l].astype(BF16),
                  ln3_g[l][None, :], ln3_b[l][None, :], alpha, S, tm, d_ff // 2)
    return xf.reshape(B, S, D)
```

```python
import functools
import math

import numpy as np
import jax
import jax.numpy as jnp
from jax import lax
from jax.experimental import pallas as pl
from jax.experimental.pallas import tpu as pltpu

F32 = jnp.float32
BF16 = jnp.bfloat16

NSA_HEADS = 8
NSA_GROUPS = 2
NSA_REP = NSA_HEADS // NSA_GROUPS
NSA_DH = 64
CMP_STRIDE = 16
CMP_LEN = 32
SLC_LEN = 64
TOPN = 16
WINDOW = 512
CMP_HIDDEN = 128
FORCE_BONUS = 1e4
MLA_HEADS = 8
MLA_Q_RANK = 384
MLA_KV_RANK = 256
MLA_NOPE = 64
MLA_ROPE = 32
MLA_V = 64
MEM_HEADS = 4
CONV_WIDTH = 3
ROPE_THETA = 10000.0
LN_EPS = 1e-5
RMS_EPS = 1e-6
NEG_INF = -1e30
LOG2E = math.log2(math.e)

LANES = 128
VMEM_LIMIT = 56 * 1024 * 1024

C_Q = 0
C_KVC = 512
C_KVN = 768
C_LAT = 1280
C_MISC = 1920
IN_COLS_PAD = 2048
GATE_LANE0 = MLA_ROPE
ONES_LANE = 64


def _dot(a, b):
    return jnp.dot(a, b, preferred_element_type=F32)


def _dot_nt(a, b):
    return lax.dot_general(a, b, (((1,), (1,)), ((), ())), preferred_element_type=F32)


def _layer_norm(y, g, b):
    mu = jnp.mean(y, axis=-1, keepdims=True)
    d = y - mu
    var = jnp.mean(d * d, axis=-1, keepdims=True)
    return d * lax.rsqrt(var + LN_EPS) * g + b


def _params(*sem):
    return pltpu.CompilerParams(dimension_semantics=sem, vmem_limit_bytes=VMEM_LIMIT)


def _rope_tables(pos_col, inv_row, half):
    ang = pos_col * inv_row
    cos = jnp.cos(ang)
    sin = jnp.sin(ang)
    lane = lax.broadcasted_iota(jnp.int32, (1, LANES), 1)
    upper = (lane & (2 * half - 1)) >= half
    rot = inv_row != 0.0
    sin_hi = jnp.where(upper & rot, sin, 0.0)
    sin_lo = jnp.where(upper | (~rot), 0.0, -sin)
    return cos, sin_hi, sin_lo


def _apply_rope(v, tabs, half):
    cos, sin_hi, sin_lo = tabs
    return v * cos + pltpu.roll(v, half, 1) * sin_hi + pltpu.roll(v, LANES - half, 1) * sin_lo


def _rms_norm(v, g):
    return v * lax.rsqrt(jnp.mean(v * v, axis=-1, keepdims=True) + RMS_EPS) * g


def _inproj_body(pos_ref, inv_ref, x_ref, w_ref, gq_ref, gkv_ref, wq_ref, wkv_ref,
                 qn_ref, kvn_ref, kvc_ref, misc_ref, qm_ref, km_ref, vm_ref):
    xb = x_ref[...].astype(BF16)
    lane = lax.broadcasted_iota(jnp.int32, (1, LANES), 1)
    low = lane < NSA_DH
    qscale = NSA_DH ** -0.5 * LOG2E

    ang = pos_ref[...].astype(F32) * inv_ref[...]
    cos, sin = jnp.cos(ang), jnp.sin(ang)
    nf, mf = NSA_DH // 2, MLA_ROPE // 2

    def tile_nsa(t):
        t = jnp.where(lane < nf, t, 0.0)
        t = t + pltpu.roll(t, nf, 1)
        return t + pltpu.roll(t, 2 * nf, 1)

    def place_mla(t):
        t = jnp.where((lane >= nf) & (lane < nf + mf), t, 0.0)
        return pltpu.roll(t, MLA_NOPE - nf, 1) + pltpu.roll(t, MLA_NOPE - nf + mf, 1)

    upper = (lane & (NSA_DH - 1)) >= nf
    sin_n = tile_nsa(sin)
    tabs = (tile_nsa(cos), jnp.where(upper, sin_n, 0.0), jnp.where(upper, 0.0, -sin_n))
    pe_lanes = (lane >= MLA_NOPE) & (lane < MLA_NOPE + MLA_ROPE)
    pe_upper = lane >= MLA_NOPE + mf
    sin_m = place_mla(sin)
    tabs_m = (jnp.where(pe_lanes, place_mla(cos), 1.0),
              jnp.where(pe_lanes & pe_upper, sin_m, 0.0),
              jnp.where(pe_lanes & (~pe_upper), -sin_m, 0.0))

    def proj(c0, n):
        return _dot(xb, w_ref[:, c0:c0 + n])

    def split_store(v, ref, idx_lo, idx_hi, pad=0.0):
        ref[0, idx_lo] = jnp.where(low, v, pad).astype(BF16)
        ref[0, idx_hi] = jnp.where(low, pltpu.roll(v, NSA_DH, 1), pad).astype(BF16)

    ones_pad = jnp.where(lane == ONES_LANE, 1.0, 0.0)

    for slab in range(2):
        h = proj(C_Q + 256 * slab, 256)
        for j in range(2):
            r = _apply_rope(h[:, LANES * j:LANES * (j + 1)], tabs, NSA_DH // 2) * qscale
            split_store(r, qn_ref, 4 * slab + 2 * j, 4 * slab + 2 * j + 1)

    h = proj(C_KVC, 256)
    for j in range(2):
        v = h[:, LANES * j:LANES * (j + 1)]
        kvc_ref[2 * j] = v[:, :NSA_DH]
        kvc_ref[2 * j + 1] = pltpu.roll(v, NSA_DH, 1)[:, :NSA_DH]

    for slab in range(2):
        h = proj(C_KVN + 256 * slab, 256)
        k = _apply_rope(h[:, :LANES], tabs, NSA_DH // 2)
        split_store(k, kvn_ref, 4 * slab, 4 * slab + 1)
        split_store(h[:, LANES:], kvn_ref, 4 * slab + 2, 4 * slab + 3, ones_pad)

    lat = [proj(C_LAT + 256 * i, 256) for i in range((IN_COLS_PAD - C_LAT) // 256)]
    misc = lat[2][:, LANES:]
    misc_ref[...] = misc
    mq = jnp.concatenate([lat[0], lat[1][:, :LANES]], axis=1)
    mkv = jnp.concatenate([lat[1][:, LANES:], lat[2][:, :LANES]], axis=1)
    mscale = (MLA_NOPE + MLA_ROPE) ** -0.5 * LOG2E
    nope = lane < MLA_NOPE
    qn = _rms_norm(mq, gq_ref[...]).astype(BF16)
    kvn = _rms_norm(mkv, gkv_ref[...]).astype(BF16)
    kpe = jnp.where(pe_lanes, _apply_rope(pltpu.roll(misc, MLA_NOPE, 1), tabs_m, mf), 0.0)
    for slab in range(MLA_HEADS // 2):
        hq = _dot(qn, wq_ref[:, 256 * slab:256 * (slab + 1)])
        for j in range(2):
            q = _apply_rope(hq[:, LANES * j:LANES * (j + 1)], tabs_m, mf) * mscale
            qm_ref[0, 2 * slab + j] = q.astype(BF16)
    for slab in range(MLA_HEADS // 4):
        hk = _dot(kvn, wkv_ref[:, 256 * slab:256 * (slab + 1)])
        hv = _dot(kvn, wkv_ref[:, 512 + 256 * slab:512 + 256 * (slab + 1)])
        for j in range(4):
            kn = hk[:, LANES * (j // 2):LANES * (j // 2 + 1)]
            vn = hv[:, LANES * (j // 2):LANES * (j // 2 + 1)]
            if j % 2:
                kn, vn = pltpu.roll(kn, MLA_NOPE, 1), pltpu.roll(vn, MLA_V, 1)
            km_ref[0, 4 * slab + j] = jnp.where(nope, kn, kpe).astype(BF16)
            vm_ref[0, 4 * slab + j] = jnp.where(nope, vn, ones_pad).astype(BF16)


def _inproj(pos, inv_row, xf, w_in_p, gq, gkv, wq_p, wkv_p, B, S, tm):
    T = B * S
    nst = S // tm
    tok = lambda i: (i, 0)
    const = lambda i: (0, 0)
    head_blk = lambda i: (i // nst, 0, i % nst, 0)
    heads = jax.ShapeDtypeStruct((B, 8, S, LANES), BF16)
    return pl.pallas_call(
        _inproj_body,
        grid=(T // tm,),
        in_specs=[
            pl.BlockSpec((tm, 1), tok),
            pl.BlockSpec((1, LANES), const),
            pl.BlockSpec((tm, xf.shape[1]), tok),
            pl.BlockSpec(w_in_p.shape, const),
            pl.BlockSpec(gq.shape, const),
            pl.BlockSpec(gkv.shape, const),
            pl.BlockSpec(wq_p.shape, const),
            pl.BlockSpec(wkv_p.shape, const),
        ],
        out_specs=[
            pl.BlockSpec((1, 8, tm, LANES), head_blk),
            pl.BlockSpec((1, 8, tm, LANES), head_blk),
            pl.BlockSpec((4, tm, NSA_DH), lambda i: (0, i, 0)),
            pl.BlockSpec((tm, LANES), tok),
            pl.BlockSpec((1, 8, tm, LANES), head_blk),
            pl.BlockSpec((1, 8, tm, LANES), head_blk),
            pl.BlockSpec((1, 8, tm, LANES), head_blk),
        ],
        out_shape=[
            heads,
            heads,
            jax.ShapeDtypeStruct((4, T, NSA_DH), F32),
            jax.ShapeDtypeStruct((T, LANES), F32),
            heads,
            heads,
            heads,
        ],
        compiler_params=_params("arbitrary"),
        name="inproj",
    )(pos, inv_row, xf, w_in_p, gq, gkv, wq_p, wkv_p)


def _compress_body(pos_ref, inv_ref, x_ref, pe_ref, w1_ref, b1_ref, w2_ref, b2_ref, o_ref):
    is_k = pl.program_id(0) < NSA_GROUPS
    x = x_ref[0, 0]
    half = CMP_STRIDE * NSA_DH
    a1 = _dot((x + pe_ref[0, :, :half]).astype(BF16), w1_ref[0, :half, :])
    a2 = _dot((x + pe_ref[0, :, half:]).astype(BF16), w1_ref[0, half:, :])
    nch = x.shape[0]
    pre = a1 + pltpu.roll(a2, nch - 1, 0) + b1_ref[0]
    hid = jax.nn.gelu(pre, approximate=True)
    out = _dot(hid.astype(BF16), w2_ref[0]) + b2_ref[0]
    tabs = _rope_tables(pos_ref[0].astype(F32), inv_ref[...], NSA_DH // 2)
    roped = _apply_rope(out, tabs, NSA_DH // 2)
    out = jnp.where(is_k, roped, out)
    row = lax.broadcasted_iota(jnp.int32, (nch, 1), 0)
    o_ref[0, 0] = jnp.where(row < nch - 1, out, 0.0).astype(BF16)


def _compress(pos_cmp, inv_nsa, kvc, pe, w1, b1, w2, b2, B, S):
    nch = S // CMP_STRIDE
    flat = CMP_STRIDE * NSA_DH
    x = kvc.reshape(4, B, nch, flat)
    kv = lambda j, b: (j // NSA_GROUPS, 0, 0)
    return pl.pallas_call(
        _compress_body,
        grid=(4, B),
        in_specs=[
            pl.BlockSpec((1, nch, 1), lambda j, b: (b, 0, 0)),
            pl.BlockSpec((1, LANES), lambda j, b: (0, 0)),
            pl.BlockSpec((1, 1, nch, flat), lambda j, b: (j, b, 0, 0)),
            pl.BlockSpec((1, 1, 2 * flat), kv),
            pl.BlockSpec((1, 2 * flat, CMP_HIDDEN), kv),
            pl.BlockSpec((1, 1, CMP_HIDDEN), kv),
            pl.BlockSpec((1, CMP_HIDDEN, LANES), kv),
            pl.BlockSpec((1, 1, LANES), kv),
        ],
        out_specs=pl.BlockSpec((1, 1, nch, LANES), lambda j, b: (j, b, 0, 0)),
        out_shape=jax.ShapeDtypeStruct((4, B, nch, LANES), BF16),
        compiler_params=_params("arbitrary", "arbitrary"),
        name="compress",
    )(pos_cmp, inv_nsa, x, pe, w1, b1, w2, b2)


STRIP = 64


def _lane_tile(col, n):
    reps = [col] * (n // LANES)
    if n % LANES:
        reps.append(col[:, :n % LANES])
    return reps[0] if len(reps) == 1 else jnp.concatenate(reps, axis=1)


def _flash_reset(m_ref, acc_ref):
    m_ref[...] = jnp.full(m_ref.shape, NEG_INF, F32)
    acc_ref[...] = jnp.zeros(acc_ref.shape, F32)


def _flash_update(s_ref, v, m_ref, acc_ref, p_ref, mask=None):
    rows, n = s_ref.shape
    for r in range(rows // STRIP):
        rs = slice(STRIP * r, STRIP * (r + 1))
        s = s_ref[rs, :]
        if mask is not None:
            s = jnp.where(mask[rs], s, NEG_INF)
        m_old = m_ref[rs, :]
        m_new = jnp.maximum(m_old, jnp.max(s, axis=1, keepdims=True))
        p_ref[rs, :] = jnp.exp2(s - _lane_tile(m_new, n)).astype(BF16)
        acc_ref[rs, :] = jnp.exp2(m_old - m_new) * acc_ref[rs, :]
        m_ref[rs, :] = m_new
    acc_ref[...] += _dot(p_ref[...], v)


def _flash_finish(acc):
    return acc * (1.0 / acc[:, ONES_LANE:ONES_LANE + 1])


NSA_TQ = 128
NSA_ROWS = NSA_REP * NSA_TQ
SLC_CHUNK = 256
WIN_SPAN = WINDOW + NSA_TQ
BIAS_LANE0 = LANES


def _nsa_body(q_ref, kvn_ref, kvc_ref, misc_ref, cov_ref, eye_ref, o_ref,
              kaug_ref, score_ref, bias_ref, sa_ref, sb_ref, pa_ref, pb_ref, ms_ref, accs_ref,
              sw_ref, pw_ref, mw_ref, accw_ref, sc_ref, pn_ref, pc_ref, oc_ref):
    c = pl.program_id(1)
    rows = NSA_ROWS
    t_row = c * NSA_TQ + (lax.broadcasted_iota(jnp.int32, (rows, 1), 0) & (NSA_TQ - 1))
    ncmp = kvc_ref.shape[2]
    nblk = kaug_ref.shape[1] // SLC_LEN

    @pl.when(c == 0)
    def _():
        for g in range(NSA_GROUPS):
            kaug_ref[g, :, :BIAS_LANE0] = kvn_ref[0, g]
            kaug_ref[g, :, BIAS_LANE0:] = jnp.zeros((kaug_ref.shape[1], NSA_TQ), BF16)

    qs = [q_ref[0, NSA_REP * g:NSA_REP * (g + 1)].reshape(rows, LANES) for g in range(NSA_GROUPS)]

    cmp_valid = (CMP_STRIDE * lax.broadcasted_iota(jnp.int32, (1, ncmp), 1) + CMP_LEN - 1) <= t_row
    for g in range(NSA_GROUPS):
        sc_ref[g] = _dot_nt(qs[g], kvc_ref[g, 0])
    for g in range(NSA_GROUPS):
        for r in range(rows // STRIP):
            rs = slice(STRIP * r, STRIP * (r + 1))
            s = jnp.where(cmp_valid[rs], sc_ref[g, rs, :], NEG_INF)
            p = jnp.where(cmp_valid[rs], jnp.exp2(s - jnp.max(s, axis=1, keepdims=True)), 0.0)
            l = jnp.sum(p, axis=1, keepdims=True)
            p = p * jnp.where(l > 0.0, 1.0 / l, 0.0)
            pn_ref[g, rs, :] = p
            pc_ref[g, rs, :] = p.astype(BF16)
        oc_ref[g] = _dot(pc_ref[g], kvc_ref[NSA_GROUPS + g, 0])

    _flash_reset(mw_ref, accw_ref)
    win_start = pl.multiple_of(jnp.maximum(c * NSA_TQ - WINDOW, 0), NSA_TQ)
    diff = t_row - (win_start + lax.broadcasted_iota(jnp.int32, (1, WIN_SPAN), 1))
    win_valid = (diff >= 0) & (diff < WINDOW)
    for g in range(NSA_GROUPS):
        sw_ref[g] = _dot_nt(qs[g], kvn_ref[0, 4 + g, pl.ds(win_start, WIN_SPAN), :])
    for g in range(NSA_GROUPS):
        _flash_update(sw_ref.at[g], kvn_ref[0, 6 + g, pl.ds(win_start, WIN_SPAN), :],
                      mw_ref.at[g], accw_ref.at[g], pw_ref.at[g], win_valid)

    @pl.when((c + 1) * (NSA_TQ // SLC_LEN) > TOPN)
    def _():
        width = NSA_GROUPS * NSA_TQ
        ps = jnp.concatenate(
            [sum(pn_ref[g, NSA_TQ * r:NSA_TQ * (r + 1), :] for r in range(NSA_REP))
             for g in range(NSA_GROUPS)], axis=0)
        hi = ps.astype(BF16)
        lo = (ps - hi.astype(F32)).astype(BF16)
        imp = _dot_nt(cov_ref[...], hi) + _dot_nt(cov_ref[...], lo)
        jidx = lax.broadcasted_iota(jnp.int32, (nblk, width), 0)
        lane_q = lax.broadcasted_iota(jnp.int32, (1, width), 1) & (NSA_TQ - 1)
        cur = c * (NSA_TQ // SLC_LEN) + lane_q // SLC_LEN
        forced = (jidx == 0) | (jidx == cur) | (jidx == cur - 1)
        score = jnp.where(jidx <= cur, jnp.where(forced, FORCE_BONUS, imp), NEG_INF)
        score_ref[...] = score
        sub = 8
        cnt = [jnp.zeros((sub, width), F32) for _ in range(nblk // sub)]
        tiles = [score[sub * v:sub * (v + 1)] for v in range(nblk // sub)]
        sidx = lax.broadcasted_iota(jnp.int32, (sub, width), 0)
        for jp in range(nblk):
            rowv = jnp.broadcast_to(score_ref[jp:jp + 1, :], (sub, width))
            for v in range(nblk // sub):
                if sub * v > jp:
                    cnt[v] = jnp.where(rowv >= tiles[v], cnt[v] + 1.0, cnt[v])
                elif sub * v + sub - 1 <= jp:
                    cnt[v] = jnp.where(rowv > tiles[v], cnt[v] + 1.0, cnt[v])
                else:
                    ge = jnp.where(rowv >= tiles[v], cnt[v] + 1.0, cnt[v])
                    gt = jnp.where(rowv > tiles[v], cnt[v] + 1.0, cnt[v])
                    cnt[v] = jnp.where(sidx + sub * v > jp, ge, gt)
        rank = jnp.concatenate(cnt, axis=0)
        bias = jnp.where(rank < float(TOPN), 0.0, NEG_INF)
        for g in range(NSA_GROUPS):
            bias_ref[g] = bias[:, NSA_TQ * g:NSA_TQ * (g + 1)]

        def write_bias(j, carry):
            r0 = pl.multiple_of(j * SLC_LEN, SLC_LEN)
            for g in range(NSA_GROUPS):
                blk = jnp.broadcast_to(bias_ref[g, pl.ds(j, 1), :], (SLC_LEN, NSA_TQ))
                kaug_ref[g, pl.ds(r0, SLC_LEN), BIAS_LANE0:] = blk.astype(BF16)
            return carry
        lax.fori_loop(0, (c + 1) * (NSA_TQ // SLC_LEN), write_bias, 0)

    _flash_reset(ms_ref, accs_ref)
    qas = [jnp.concatenate([q, eye_ref[...]], axis=1) for q in qs]
    nchunk = kaug_ref.shape[1] // SLC_CHUNK
    last = c // (SLC_CHUNK // NSA_TQ)

    def slc_scores(kc, dst):
        k0 = pl.multiple_of(jnp.minimum(kc, nchunk - 1) * SLC_CHUNK, SLC_CHUNK)
        for g in range(NSA_GROUPS):
            dst[g] = _dot_nt(qas[g], kaug_ref[g, pl.ds(k0, SLC_CHUNK), :])

    def slc_update(src, p_ref, kc, causal):
        k0 = pl.multiple_of(jnp.minimum(kc, nchunk - 1) * SLC_CHUNK, SLC_CHUNK)
        mask = None
        if causal:
            mask = (kc * SLC_CHUNK + lax.broadcasted_iota(jnp.int32, (1, SLC_CHUNK), 1)) <= t_row
        for g in range(NSA_GROUPS):
            _flash_update(src.at[g], kvn_ref[0, 2 + g, pl.ds(k0, SLC_CHUNK), :],
                          ms_ref.at[g], accs_ref.at[g], p_ref.at[g], mask)

    slc_scores(0, sa_ref)

    def slc_pair(i, carry):
        slc_scores(2 * i + 1, sb_ref)
        slc_update(sa_ref, pa_ref, 2 * i, False)
        slc_scores(2 * i + 2, sa_ref)
        slc_update(sb_ref, pb_ref, 2 * i + 1, False)
        return carry
    lax.fori_loop(0, last // 2, slc_pair, 0)
    tail = 2 * (last // 2)
    slc_scores(tail + 1, sb_ref)
    slc_update(sa_ref, pa_ref, tail, True)
    slc_update(sb_ref, pb_ref, tail + 1, True)

    sig = 1.0 / (1.0 + jnp.exp(-misc_ref[...]))
    outs = []
    for g in range(NSA_GROUPS):
        for r in range(NSA_REP):
            lane0 = GATE_LANE0 + 3 * (NSA_REP * g + r)
            rs = slice(NSA_TQ * r, NSA_TQ * (r + 1))
            o = (sig[:, lane0:lane0 + 1] * oc_ref[g, rs, :]
                 + sig[:, lane0 + 1:lane0 + 2] * _flash_finish(accs_ref[g, rs, :])
                 + sig[:, lane0 + 2:lane0 + 3] * _flash_finish(accw_ref[g, rs, :]))
            outs.append(o[:, :NSA_DH])
    o_ref[0] = jnp.concatenate(outs, axis=1).astype(BF16)


def _nsa(qn, kvn, kvcmp, misc, cov_t, eye, B, S):
    nblk = S // SLC_LEN
    ncmp = S // CMP_STRIDE
    nq = S // NSA_TQ
    rows = NSA_ROWS
    return pl.pallas_call(
        _nsa_body,
        grid=(B, nq),
        in_specs=[
            pl.BlockSpec((1, NSA_HEADS, NSA_TQ, LANES), lambda b, c: (b, 0, c, 0)),
            pl.BlockSpec((1, 8, S, LANES), lambda b, c: (b, 0, 0, 0)),
            pl.BlockSpec((4, 1, ncmp, LANES), lambda b, c: (0, b, 0, 0)),
            pl.BlockSpec((NSA_TQ, LANES), lambda b, c: (b * nq + c, 0)),
            pl.BlockSpec(cov_t.shape, lambda b, c: (0, 0)),
            pl.BlockSpec(eye.shape, lambda b, c: (0, 0)),
        ],
        out_specs=pl.BlockSpec((1, NSA_TQ, NSA_HEADS * NSA_DH), lambda b, c: (b, c, 0)),
        out_shape=jax.ShapeDtypeStruct((B, S, NSA_HEADS * NSA_DH), BF16),
        scratch_shapes=[
            pltpu.VMEM((NSA_GROUPS, S, BIAS_LANE0 + NSA_TQ), BF16),
            pltpu.VMEM((nblk, NSA_GROUPS * NSA_TQ), F32),
            pltpu.VMEM((NSA_GROUPS, nblk, NSA_TQ), F32),
            pltpu.VMEM((NSA_GROUPS, rows, SLC_CHUNK), F32),
            pltpu.VMEM((NSA_GROUPS, rows, SLC_CHUNK), F32),
            pltpu.VMEM((NSA_GROUPS, rows, SLC_CHUNK), BF16),
            pltpu.VMEM((NSA_GROUPS, rows, SLC_CHUNK), BF16),
            pltpu.VMEM((NSA_GROUPS, rows, LANES), F32),
            pltpu.VMEM((NSA_GROUPS, rows, LANES), F32),
            pltpu.VMEM((NSA_GROUPS, rows, WIN_SPAN), F32),
            pltpu.VMEM((NSA_GROUPS, rows, WIN_SPAN), BF16),
            pltpu.VMEM((NSA_GROUPS, rows, LANES), F32),
            pltpu.VMEM((NSA_GROUPS, rows, LANES), F32),
            pltpu.VMEM((NSA_GROUPS, rows, ncmp), F32),
            pltpu.VMEM((NSA_GROUPS, rows, ncmp), F32),
            pltpu.VMEM((NSA_GROUPS, rows, ncmp), BF16),
            pltpu.VMEM((NSA_GROUPS, rows, LANES), F32),
        ],
        compiler_params=_params("arbitrary", "arbitrary"),
        name="nsa",
    )(qn, kvn, kvcmp, misc, cov_t, eye)


MLA_TQ = 512
MLA_CHUNK = 256
MLA_HPB = 4


def _mla_body(q_ref, k_ref, v_ref, o_ref, sa_ref, sb_ref, pa_ref, pb_ref, m_ref, acc_ref):
    qi = pl.program_id(2)
    t_row = qi * MLA_TQ + lax.broadcasted_iota(jnp.int32, (MLA_TQ, 1), 0)
    _flash_reset(m_ref, acc_ref)

    def scores(kc, dst):
        k0 = pl.multiple_of(kc * MLA_CHUNK, MLA_CHUNK)
        for j in range(MLA_HPB):
            dst[j] = _dot_nt(q_ref[0, j], k_ref[0, j, pl.ds(k0, MLA_CHUNK), :])

    def update(src, p_ref, kc, causal):
        k0 = pl.multiple_of(kc * MLA_CHUNK, MLA_CHUNK)
        mask = None
        if causal:
            mask = (k0 + lax.broadcasted_iota(jnp.int32, (1, MLA_CHUNK), 1)) <= t_row
        for j in range(MLA_HPB):
            _flash_update(src.at[j], v_ref[0, j, pl.ds(k0, MLA_CHUNK), :],
                          m_ref.at[j], acc_ref.at[j], p_ref.at[j], mask)

    scores(0, sa_ref)

    def pair(i, carry):
        scores(2 * i + 1, sb_ref)
        update(sa_ref, pa_ref, 2 * i, False)
        scores(2 * i + 2, sa_ref)
        update(sb_ref, pb_ref, 2 * i + 1, False)
        return carry
    lax.fori_loop(0, qi, pair, 0)
    tail = (MLA_TQ // MLA_CHUNK) * qi
    scores(tail + 1, sb_ref)
    update(sa_ref, pa_ref, tail, True)
    update(sb_ref, pb_ref, tail + 1, True)
    o_ref[0] = jnp.concatenate([_flash_finish(acc_ref[j])[:, :MLA_V] for j in range(MLA_HPB)],
                               axis=1).astype(BF16)


def _mla(q, k, v, B, S):
    return pl.pallas_call(
        _mla_body,
        grid=(B, MLA_HEADS // MLA_HPB, S // MLA_TQ),
        in_specs=[
            pl.BlockSpec((1, MLA_HPB, MLA_TQ, LANES), lambda b, h, i: (b, h, i, 0)),
            pl.BlockSpec((1, MLA_HPB, S, LANES), lambda b, h, i: (b, h, 0, 0)),
            pl.BlockSpec((1, MLA_HPB, S, LANES), lambda b, h, i: (b, h, 0, 0)),
        ],
        out_specs=pl.BlockSpec((1, MLA_TQ, MLA_HPB * MLA_V), lambda b, h, i: (b, i, h)),
        out_shape=jax.ShapeDtypeStruct((B, S, MLA_HEADS * MLA_V), BF16),
        scratch_shapes=[
            pltpu.VMEM((MLA_HPB, MLA_TQ, MLA_CHUNK), F32),
            pltpu.VMEM((MLA_HPB, MLA_TQ, MLA_CHUNK), F32),
            pltpu.VMEM((MLA_HPB, MLA_TQ, MLA_CHUNK), BF16),
            pltpu.VMEM((MLA_HPB, MLA_TQ, MLA_CHUNK), BF16),
            pltpu.VMEM((MLA_HPB, MLA_TQ, LANES), F32),
            pltpu.VMEM((MLA_HPB, MLA_TQ, LANES), F32),
        ],
        compiler_params=_params("arbitrary", "arbitrary", "arbitrary"),
        name="mla",
    )(q, k, v)


def _mix_ln_body(x_ref, on_ref, om_ref, w_ref, g_ref, b_ref, o_ref, *, alpha):
    half = on_ref.shape[1]
    mix = _dot(on_ref[...], w_ref[:half, :]) + _dot(om_ref[...], w_ref[half:, :])
    o_ref[...] = _layer_norm(alpha * x_ref[...] + mix, g_ref[...], b_ref[...])


def _mix_ln(xf, o_nsa, o_mla, w_o, g, b, alpha, tm):
    T, D = xf.shape
    tok = lambda i: (i, 0)
    const = lambda i: (0, 0)
    return pl.pallas_call(
        functools.partial(_mix_ln_body, alpha=alpha),
        grid=(T // tm,),
        in_specs=[
            pl.BlockSpec((tm, D), tok),
            pl.BlockSpec((tm, o_nsa.shape[1]), tok),
            pl.BlockSpec((tm, o_mla.shape[1]), tok),
            pl.BlockSpec(w_o.shape, const),
            pl.BlockSpec((1, D), const),
            pl.BlockSpec((1, D), const),
        ],
        out_specs=pl.BlockSpec((tm, D), tok),
        out_shape=jax.ShapeDtypeStruct((T, D), F32),
        compiler_params=_params("arbitrary"),
        name="mix_ln",
    )(xf, o_nsa, o_mla, w_o, g, b)


def _mem_kv_body(m_ref, wk_ref, wv_ref, k_ref, v_ref):
    mb = m_ref[...].astype(BF16)
    k_ref[...] = _dot(mb, wk_ref[...]).astype(BF16)
    v_ref[...] = _dot(mb, wv_ref[...]).astype(BF16)


def _mem_kv(memf, wk, wv, tm):
    R, D = memf.shape
    tok = lambda i: (i, 0)
    const = lambda i: (0, 0)
    return pl.pallas_call(
        _mem_kv_body,
        grid=(R // tm,),
        in_specs=[pl.BlockSpec((tm, D), tok), pl.BlockSpec(wk.shape, const),
                  pl.BlockSpec(wv.shape, const)],
        out_specs=[pl.BlockSpec((tm, D), tok), pl.BlockSpec((tm, D), tok)],
        out_shape=[jax.ShapeDtypeStruct((R, D), BF16), jax.ShapeDtypeStruct((R, D), BF16)],
        compiler_params=_params("arbitrary"),
        name="mem_kv",
    )(memf, wk, wv)


def _mem_attn_body(x_ref, k_ref, v_ref, wq_ref, wo_ref, g_ref, b_ref, o_ref, *, alpha):
    x = x_ref[...]
    D = x.shape[1]
    dh = D // MEM_HEADS
    q = (_dot(x.astype(BF16), wq_ref[...]) * (dh ** -0.5 * LOG2E)).astype(BF16)
    outs = []
    for h in range(MEM_HEADS):
        cs = slice(dh * h, dh * (h + 1))
        s = _dot_nt(q[:, cs], k_ref[0, :, cs])
        p = jnp.exp2(s - jnp.max(s, axis=1, keepdims=True))
        l = jnp.sum(p, axis=1, keepdims=True)
        outs.append((_dot(p.astype(BF16), v_ref[0, :, cs]) * (1.0 / l)).astype(BF16))
    o = jnp.concatenate(outs, axis=1)
    y = _dot(o, wo_ref[...])
    o_ref[...] = _layer_norm(alpha * x + y, g_ref[...], b_ref[...])


def _mem_attn(xf, k_mem, v_mem, wq, wo, g, b, alpha, S, tm):
    T, D = xf.shape
    nst = S // tm
    M = k_mem.shape[1]
    tok = lambda i: (i, 0)
    const = lambda i: (0, 0)
    memb = lambda i: (i // nst, 0, 0)
    return pl.pallas_call(
        functools.partial(_mem_attn_body, alpha=alpha),
        grid=(T // tm,),
        in_specs=[
            pl.BlockSpec((tm, D), tok),
            pl.BlockSpec((1, M, D), memb),
            pl.BlockSpec((1, M, D), memb),
            pl.BlockSpec(wq.shape, const),
            pl.BlockSpec(wo.shape, const),
            pl.BlockSpec((1, D), const),
            pl.BlockSpec((1, D), const),
        ],
        out_specs=pl.BlockSpec((tm, D), tok),
        out_shape=jax.ShapeDtypeStruct((T, D), F32),
        compiler_params=_params("arbitrary"),
        name="mem_attn",
    )(xf, k_mem, v_mem, wq, wo, g, b)


HALO = 8


FFN_SLAB = 256
FFN_TM = 256


def _ffn_body(x_ref, xh_ref, wg_ref, wu_ref, cw_ref, cb_ref, wd_ref, g_ref, b_ref, o_ref,
              act_ref, *, alpha, seq_tiles):
    i = pl.program_id(0)
    x = x_ref[...]
    xb = x.astype(BF16)
    xhb = xh_ref[...].astype(BF16)
    tm = x.shape[0]
    row = lax.broadcasted_iota(jnp.int32, (tm, 1), 0)
    seq_start = i % seq_tiles == 0
    for c0 in range(0, wg_ref.shape[1], FFN_SLAB):
        cs = slice(c0, c0 + FFN_SLAB)
        gate = _dot(xb, wg_ref[:, cs])
        up = _dot(xb, wu_ref[:, cs])
        halo = jnp.where(seq_start, 0.0, _dot(xhb, wg_ref[:, cs]))
        g1 = jnp.where(row == 0, halo[HALO - 1:HALO], pltpu.roll(gate, 1, 0))
        g2 = jnp.where(row == 0, halo[HALO - 2:HALO - 1],
                       jnp.where(row == 1, halo[HALO - 1:HALO], pltpu.roll(gate, 2, 0)))
        conv = cw_ref[0:1, cs] * g2 + cw_ref[1:2, cs] * g1 + cw_ref[2:3, cs] * gate + cb_ref[:, cs]
        act_ref[:, cs] = (conv * (1.0 / (1.0 + jnp.exp(-conv))) * up).astype(BF16)
    y = _dot(act_ref[...], wd_ref[...])
    o_ref[...] = _layer_norm(alpha * x + y, g_ref[...], b_ref[...])


def _ffn(xf, wg, wu, cw, cb, wd, g, b, alpha, S, tm):
    T, D = xf.shape
    dff = wg.shape[1]
    assert dff % FFN_SLAB == 0
    tok = lambda i: (i, 0)
    const = lambda i: (0, 0)
    return pl.pallas_call(
        functools.partial(_ffn_body, alpha=alpha, seq_tiles=S // tm),
        grid=(T // tm,),
        in_specs=[
            pl.BlockSpec((tm, D), tok),
            pl.BlockSpec((HALO, D), lambda i: (jnp.maximum(i * (tm // HALO) - 1, 0), 0)),
            pl.BlockSpec((D, dff), const),
            pl.BlockSpec((D, dff), const),
            pl.BlockSpec((CONV_WIDTH, dff), const),
            pl.BlockSpec((1, dff), const),
            pl.BlockSpec((dff, D), const),
            pl.BlockSpec((1, D), const),
            pl.BlockSpec((1, D), const),
        ],
        out_specs=pl.BlockSpec((tm, D), tok),
        out_shape=jax.ShapeDtypeStruct((T, D), F32),
        scratch_shapes=[pltpu.VMEM((tm, dff), BF16)],
        compiler_params=_params("arbitrary"),
        name="ffn",
    )(xf, xf, wg, wu, cw, cb, wd, g, b)


def _inv_freq_row(dim, lane_lo, lane_hi, period):
    inv = ROPE_THETA ** (-np.arange(0, dim, 2, dtype=np.float64) / dim)
    row = np.zeros((1, LANES), np.float32)
    for lane in range(lane_lo, lane_hi):
        row[0, lane] = inv[(lane % period) % (dim // 2)]
    return jnp.asarray(row)


def _cover_t(S):
    nc = S // CMP_STRIDE
    ns = S // SLC_LEN
    cs = np.arange(nc)[:, None] * CMP_STRIDE
    ss = np.arange(ns)[None, :] * SLC_LEN
    cover = np.clip(np.minimum(cs + CMP_LEN, ss + SLC_LEN) - np.maximum(cs, ss), 0, None) / CMP_LEN
    cover[nc - 1:] = 0.0
    return jnp.asarray(cover.T, dtype=BF16)


def _eye_aug():
    eye = np.zeros((NSA_ROWS, NSA_TQ), np.float32)
    r = np.arange(NSA_ROWS)
    eye[r, r % NSA_TQ] = 1.0
    return jnp.asarray(eye, dtype=BF16)


def _permute_w_in(w):
    D = w.shape[0]
    c1 = NSA_HEADS * NSA_DH
    c2 = c1 + 3 * 2 * NSA_GROUPS * NSA_DH
    c3 = c2 + 3 * NSA_HEADS
    c4 = c3 + MLA_Q_RANK
    c5 = c4 + MLA_KV_RANK
    c6 = c5 + MLA_ROPE
    pad = jnp.zeros((D, IN_COLS_PAD - C_MISC - MLA_ROPE - 3 * NSA_HEADS), w.dtype)
    return jnp.concatenate(
        [w[:, :c2], w[:, c3:c5], w[:, c5:c6], w[:, c2:c3], pad], axis=1).astype(BF16)


def _permute_w_uq(w):
    r = w.shape[0]
    w3 = w.reshape(r, MLA_HEADS, MLA_NOPE + MLA_ROPE)
    pad = jnp.zeros((r, MLA_HEADS, LANES - MLA_NOPE - MLA_ROPE), w.dtype)
    return jnp.concatenate([w3, pad], axis=2).reshape(r, MLA_HEADS * LANES).astype(BF16)


def _permute_w_ukv(w):
    r = w.shape[0]
    w3 = w.reshape(r, MLA_HEADS, MLA_NOPE + MLA_V)
    return jnp.concatenate([w3[:, :, :MLA_NOPE].reshape(r, -1),
                            w3[:, :, MLA_NOPE:].reshape(r, -1)], axis=1).astype(BF16)


def _pad_lanes(a):
    return jnp.concatenate([a, jnp.zeros(a.shape[:-1] + (LANES - a.shape[-1],), a.dtype)], axis=-1)


def kernel(x, mem, positions, w_in, nsa_k_pos, nsa_ck_w1, nsa_ck_b1, nsa_ck_w2, nsa_ck_b2,
           nsa_v_pos, nsa_cv_w1, nsa_cv_b1, nsa_cv_w2, nsa_cv_b2,
           mla_q_norm, mla_w_uq, mla_kv_norm, mla_w_ukv, w_o, ln1_g, ln1_b,
           mem_wq, mem_wk, mem_wv, mem_wo, ln2_g, ln2_b,
           ffn_w_up, ffn_conv_w, ffn_conv_b, ffn_w_down, ln3_g, ln3_b):
    B, S, D = x.shape
    T = B * S
    depth = w_in.shape[0]
    alpha = (2.0 * depth) ** 0.25
    d_ff = ffn_w_down.shape[1]
    tm = min(512, S)
    assert S % MLA_TQ == 0 and S >= WIN_SPAN and S % tm == 0
    assert (B * mem.shape[1]) % 256 == 0

    pos = positions.reshape(T, 1)
    pos_cmp = positions[:, CMP_LEN - 1::CMP_STRIDE]
    pos_cmp = jnp.concatenate([pos_cmp, pos_cmp[:, -1:]], axis=1)[:, :, None]
    inv_cmp = _inv_freq_row(NSA_DH, 0, NSA_DH, NSA_DH)
    inv_tok = (_inv_freq_row(NSA_DH, 0, NSA_DH // 2, NSA_DH)
               + _inv_freq_row(MLA_ROPE, NSA_DH // 2, NSA_DH // 2 + MLA_ROPE // 2, MLA_ROPE // 2))
    cov_t = _cover_t(S)
    eye = _eye_aug()
    memf = mem.reshape(B * mem.shape[1], D)

    xf = x.reshape(T, D)
    for l in range(depth):
        qn, kvn, kvc, misc, q_m, k_m, v_m = _inproj(
            pos, inv_tok, xf, _permute_w_in(w_in[l]), mla_q_norm[l][None, :],
            mla_kv_norm[l][None, :], _permute_w_uq(mla_w_uq[l]), _permute_w_ukv(mla_w_ukv[l]),
            B, S, tm)
        kvcmp = _compress(
            pos_cmp, inv_cmp, kvc,
            jnp.stack([nsa_k_pos[l].reshape(1, -1), nsa_v_pos[l].reshape(1, -1)]),
            jnp.stack([nsa_ck_w1[l], nsa_cv_w1[l]]).astype(BF16),
            jnp.stack([nsa_ck_b1[l], nsa_cv_b1[l]])[:, None, :],
            _pad_lanes(jnp.stack([nsa_ck_w2[l], nsa_cv_w2[l]])).astype(BF16),
            _pad_lanes(jnp.stack([nsa_ck_b2[l], nsa_cv_b2[l]]))[:, None, :],
            B, S)
        o_nsa = _nsa(qn, kvn, kvcmp, misc, cov_t, eye, B, S)
        o_mla = _mla(q_m, k_m, v_m, B, S)
        xf = _mix_ln(xf, o_nsa.reshape(T, -1), o_mla.reshape(T, -1), w_o[l].astype(BF16),
                     ln1_g[l][None, :], ln1_b[l][None, :], alpha, tm)
        k_mem, v_mem = _mem_kv(memf, mem_wk[l].astype(BF16), mem_wv[l].astype(BF16), 256)
        xf = _mem_attn(xf, k_mem.reshape(B, -1, D), v_mem.reshape(B, -1, D),
                       mem_wq[l].astype(BF16), mem_wo[l].astype(BF16),
                       ln2_g[l][None, :], ln2_b[l][None, :], alpha, S, tm)
        xf = _ffn(xf, ffn_w_up[l][:, :d_ff].astype(BF16), ffn_w_up[l][:, d_ff:].astype(BF16),
                  ffn_conv_w[l], ffn_conv_b[l][None, :], ffn_w_down[l].astype(BF16),
                  ln3_g[l][None, :], ln3_b[l][None, :], alpha, S, FFN_TM)
    return xf.reshape(B, S, D)
```

```python
import functools
import math

import numpy as np
import jax
import jax.numpy as jnp
from jax import lax
from jax.experimental import pallas as pl
from jax.experimental.pallas import tpu as pltpu

F32 = jnp.float32
BF16 = jnp.bfloat16

NSA_HEADS = 8
NSA_GROUPS = 2
NSA_REP = NSA_HEADS // NSA_GROUPS
NSA_DH = 64
CMP_STRIDE = 16
CMP_LEN = 32
SLC_LEN = 64
TOPN = 16
WINDOW = 512
CMP_HIDDEN = 128
FORCE_BONUS = 1e4
MLA_HEADS = 8
MLA_Q_RANK = 384
MLA_KV_RANK = 256
MLA_NOPE = 64
MLA_ROPE = 32
MLA_V = 64
MEM_HEADS = 4
CONV_WIDTH = 3
ROPE_THETA = 10000.0
LN_EPS = 1e-5
RMS_EPS = 1e-6
NEG_INF = -1e30
LOG2E = math.log2(math.e)

LANES = 128
VMEM_LIMIT = 56 * 1024 * 1024

C_Q = 0
C_KVC = 512
C_KVN = 768
C_LAT = 1280
C_MISC = 1920
IN_COLS_PAD = 2048
GATE_LANE0 = MLA_ROPE
ONES_LANE = 64


def _dot(a, b):
    return jnp.dot(a, b, preferred_element_type=F32)


def _dot_nt(a, b):
    return lax.dot_general(a, b, (((1,), (1,)), ((), ())), preferred_element_type=F32)


def _layer_norm(y, g, b):
    mu = jnp.mean(y, axis=-1, keepdims=True)
    d = y - mu
    var = jnp.mean(d * d, axis=-1, keepdims=True)
    return d * lax.rsqrt(var + LN_EPS) * g + b


def _params(*sem):
    return pltpu.CompilerParams(dimension_semantics=sem, vmem_limit_bytes=VMEM_LIMIT)


def _rope_tables(pos_col, inv_row, half):
    ang = pos_col * inv_row
    cos = jnp.cos(ang)
    sin = jnp.sin(ang)
    lane = lax.broadcasted_iota(jnp.int32, (1, LANES), 1)
    upper = (lane & (2 * half - 1)) >= half
    rot = inv_row != 0.0
    sin_hi = jnp.where(upper & rot, sin, 0.0)
    sin_lo = jnp.where(upper | (~rot), 0.0, -sin)
    return cos, sin_hi, sin_lo


def _apply_rope(v, tabs, half):
    cos, sin_hi, sin_lo = tabs
    return v * cos + pltpu.roll(v, half, 1) * sin_hi + pltpu.roll(v, LANES - half, 1) * sin_lo


def _rms_norm(v, g):
    return v * lax.rsqrt(jnp.mean(v * v, axis=-1, keepdims=True) + RMS_EPS) * g


def _inproj_body(pos_ref, inv_ref, x_ref, w_ref, gq_ref, gkv_ref, wq_ref, wkv_ref,
                 qn_ref, kvn_ref, kvc_ref, misc_ref, qm_ref, km_ref, vm_ref):
    xb = x_ref[...].astype(BF16)
    lane = lax.broadcasted_iota(jnp.int32, (1, LANES), 1)
    low = lane < NSA_DH
    qscale = NSA_DH ** -0.5 * LOG2E

    ang = pos_ref[...].astype(F32) * inv_ref[...]
    cos, sin = jnp.cos(ang), jnp.sin(ang)
    nf, mf = NSA_DH // 2, MLA_ROPE // 2

    def tile_nsa(t):
        t = jnp.where(lane < nf, t, 0.0)
        t = t + pltpu.roll(t, nf, 1)
        return t + pltpu.roll(t, 2 * nf, 1)

    def place_mla(t):
        t = jnp.where((lane >= nf) & (lane < nf + mf), t, 0.0)
        return pltpu.roll(t, MLA_NOPE - nf, 1) + pltpu.roll(t, MLA_NOPE - nf + mf, 1)

    upper = (lane & (NSA_DH - 1)) >= nf
    sin_n = tile_nsa(sin)
    tabs = (tile_nsa(cos), jnp.where(upper, sin_n, 0.0), jnp.where(upper, 0.0, -sin_n))
    pe_lanes = (lane >= MLA_NOPE) & (lane < MLA_NOPE + MLA_ROPE)
    pe_upper = lane >= MLA_NOPE + mf
    sin_m = place_mla(sin)
    tabs_m = (jnp.where(pe_lanes, place_mla(cos), 1.0),
              jnp.where(pe_lanes & pe_upper, sin_m, 0.0),
              jnp.where(pe_lanes & (~pe_upper), -sin_m, 0.0))

    def proj(c0, n):
        return _dot(xb, w_ref[:, c0:c0 + n])

    def split_store(v, ref, idx_lo, idx_hi, pad=0.0):
        ref[0, idx_lo] = jnp.where(low, v, pad).astype(BF16)
        ref[0, idx_hi] = jnp.where(low, pltpu.roll(v, NSA_DH, 1), pad).astype(BF16)

    ones_pad = jnp.where(lane == ONES_LANE, 1.0, 0.0)

    for slab in range(2):
        h = proj(C_Q + 256 * slab, 256)
        for j in range(2):
            r = _apply_rope(h[:, LANES * j:LANES * (j + 1)], tabs, NSA_DH // 2) * qscale
            split_store(r, qn_ref, 4 * slab + 2 * j, 4 * slab + 2 * j + 1)

    h = proj(C_KVC, 256)
    kvc_ref[0] = h[:, :LANES]
    kvc_ref[1] = h[:, LANES:]

    for slab in range(2):
        h = proj(C_KVN + 256 * slab, 256)
        k = _apply_rope(h[:, :LANES], tabs, NSA_DH // 2)
        split_store(k, kvn_ref, 4 * slab, 4 * slab + 1)
        split_store(h[:, LANES:], kvn_ref, 4 * slab + 2, 4 * slab + 3, ones_pad)

    lat = [proj(C_LAT + 256 * i, 256) for i in range((IN_COLS_PAD - C_LAT) // 256)]
    misc = lat[2][:, LANES:]
    misc_ref[...] = misc
    mq = jnp.concatenate([lat[0], lat[1][:, :LANES]], axis=1)
    mkv = jnp.concatenate([lat[1][:, LANES:], lat[2][:, :LANES]], axis=1)
    mscale = (MLA_NOPE + MLA_ROPE) ** -0.5 * LOG2E
    nope = lane < MLA_NOPE
    qn = _rms_norm(mq, gq_ref[...]).astype(BF16)
    kvn = _rms_norm(mkv, gkv_ref[...]).astype(BF16)
    kpe = jnp.where(pe_lanes, _apply_rope(pltpu.roll(misc, MLA_NOPE, 1), tabs_m, mf), 0.0)
    for slab in range(MLA_HEADS // 2):
        hq = _dot(qn, wq_ref[:, 256 * slab:256 * (slab + 1)])
        for j in range(2):
            q = _apply_rope(hq[:, LANES * j:LANES * (j + 1)], tabs_m, mf) * mscale
            qm_ref[0, 2 * slab + j] = q.astype(BF16)
    for slab in range(MLA_HEADS // 4):
        hk = _dot(kvn, wkv_ref[:, 256 * slab:256 * (slab + 1)])
        hv = _dot(kvn, wkv_ref[:, 512 + 256 * slab:512 + 256 * (slab + 1)])
        for j in range(4):
            kn = hk[:, LANES * (j // 2):LANES * (j // 2 + 1)]
            vn = hv[:, LANES * (j // 2):LANES * (j // 2 + 1)]
            if j % 2:
                kn, vn = pltpu.roll(kn, MLA_NOPE, 1), pltpu.roll(vn, MLA_V, 1)
            km_ref[0, 4 * slab + j] = jnp.where(nope, kn, kpe).astype(BF16)
            vm_ref[0, 4 * slab + j] = jnp.where(nope, vn, ones_pad).astype(BF16)


def _inproj(pos, inv_row, xf, w_in_p, gq, gkv, wq_p, wkv_p, B, S, tm):
    T = B * S
    nst = S // tm
    tok = lambda i: (i, 0)
    const = lambda i: (0, 0)
    head_blk = lambda i: (i // nst, 0, i % nst, 0)
    heads = jax.ShapeDtypeStruct((B, 8, S, LANES), BF16)
    return pl.pallas_call(
        _inproj_body,
        grid=(T // tm,),
        in_specs=[
            pl.BlockSpec((tm, 1), tok),
            pl.BlockSpec((1, LANES), const),
            pl.BlockSpec((tm, xf.shape[1]), tok),
            pl.BlockSpec(w_in_p.shape, const),
            pl.BlockSpec(gq.shape, const),
            pl.BlockSpec(gkv.shape, const),
            pl.BlockSpec(wq_p.shape, const),
            pl.BlockSpec(wkv_p.shape, const),
        ],
        out_specs=[
            pl.BlockSpec((1, 8, tm, LANES), head_blk),
            pl.BlockSpec((1, 8, tm, LANES), head_blk),
            pl.BlockSpec((2, tm, LANES), lambda i: (0, i, 0)),
            pl.BlockSpec((tm, LANES), tok),
            pl.BlockSpec((1, 8, tm, LANES), head_blk),
            pl.BlockSpec((1, 8, tm, LANES), head_blk),
            pl.BlockSpec((1, 8, tm, LANES), head_blk),
        ],
        out_shape=[
            heads,
            heads,
            jax.ShapeDtypeStruct((2, T, LANES), F32),
            jax.ShapeDtypeStruct((T, LANES), F32),
            heads,
            heads,
            heads,
        ],
        compiler_params=_params("arbitrary"),
        name="inproj",
    )(pos, inv_row, xf, w_in_p, gq, gkv, wq_p, wkv_p)


def _compress_body(pos_ref, inv_ref, x_ref, pe_ref, w1_ref, b1_ref, w2_ref, b2_ref, o_ref):
    is_k = pl.program_id(0) == 0
    nch = o_ref.shape[2]
    a1 = jnp.zeros((nch, NSA_GROUPS * CMP_HIDDEN), F32)
    a2 = jnp.zeros((nch, NSA_GROUPS * CMP_HIDDEN), F32)
    for l in range(CMP_STRIDE):
        xl = x_ref.at[0, 0][pl.ds(l, nch, stride=CMP_STRIDE), :]
        a1 = a1 + _dot((xl + pe_ref[0, l:l + 1, :]).astype(BF16), w1_ref[0, l])
        a2 = a2 + _dot((xl + pe_ref[0, CMP_STRIDE + l:CMP_STRIDE + l + 1, :]).astype(BF16),
                       w1_ref[0, CMP_STRIDE + l])
    pre = a1 + pltpu.roll(a2, nch - 1, 0) + b1_ref[0]
    hid = jax.nn.gelu(pre, approximate=True)
    out = _dot(hid.astype(BF16), w2_ref[0]) + b2_ref[0]
    tabs = _rope_tables(pos_ref[0].astype(F32), inv_ref[...], NSA_DH // 2)
    row = lax.broadcasted_iota(jnp.int32, (nch, 1), 0)
    for g in range(NSA_GROUPS):
        og = out[:, LANES * g:LANES * (g + 1)]
        og = jnp.where(is_k, _apply_rope(og, tabs, NSA_DH // 2), og)
        o_ref[g, 0] = jnp.where(row < nch - 1, og, 0.0).astype(BF16)


def _compress(pos_cmp, inv_nsa, kvc, pe, w1, b1, w2, b2, B, S):
    nch = S // CMP_STRIDE
    x = kvc.reshape(2, B, S, LANES)
    kv = lambda j, b: (j, 0, 0)
    kv4 = lambda j, b: (j, 0, 0, 0)
    return pl.pallas_call(
        _compress_body,
        grid=(2, B),
        in_specs=[
            pl.BlockSpec((1, nch, 1), lambda j, b: (b, 0, 0)),
            pl.BlockSpec((1, LANES), lambda j, b: (0, 0)),
            pl.BlockSpec((1, 1, S, LANES), lambda j, b: (j, b, 0, 0)),
            pl.BlockSpec((1,) + pe.shape[1:], kv),
            pl.BlockSpec((1,) + w1.shape[1:], kv4),
            pl.BlockSpec((1,) + b1.shape[1:], kv),
            pl.BlockSpec((1,) + w2.shape[1:], kv),
            pl.BlockSpec((1,) + b2.shape[1:], kv),
        ],
        out_specs=pl.BlockSpec((NSA_GROUPS, 1, nch, LANES), lambda j, b: (j, b, 0, 0)),
        out_shape=jax.ShapeDtypeStruct((2 * NSA_GROUPS, B, nch, LANES), BF16),
        compiler_params=_params("arbitrary", "arbitrary"),
        name="compress",
    )(pos_cmp, inv_nsa, x, pe, w1, b1, w2, b2)


def _compress_weights(k_pos, k_w1, k_b1, k_w2, k_b2, v_pos, v_w1, v_b1, v_w2, v_b2):
    def one(pos, w1, b1, w2, b2):
        w1l = w1.reshape(CMP_LEN, NSA_DH, CMP_HIDDEN)
        z1 = jnp.zeros_like(w1l)
        w1bd = jnp.concatenate([jnp.concatenate([w1l, z1], axis=2),
                                jnp.concatenate([z1, w1l], axis=2)], axis=1)
        w2p = _pad_lanes(w2)
        z2 = jnp.zeros_like(w2p)
        w2bd = jnp.concatenate([jnp.concatenate([w2p, z2], axis=1),
                                jnp.concatenate([z2, w2p], axis=1)], axis=0)
        return (jnp.tile(pos, (1, NSA_GROUPS)), w1bd.astype(BF16), jnp.tile(b1, NSA_GROUPS)[None, :],
                w2bd.astype(BF16), jnp.tile(_pad_lanes(b2), NSA_GROUPS)[None, :])
    k = one(k_pos, k_w1, k_b1, k_w2, k_b2)
    v = one(v_pos, v_w1, v_b1, v_w2, v_b2)
    return tuple(jnp.stack([a, b]) for a, b in zip(k, v))


STRIP = 64


def _lane_tile(col, n):
    reps = [col] * (n // LANES)
    if n % LANES:
        reps.append(col[:, :n % LANES])
    return reps[0] if len(reps) == 1 else jnp.concatenate(reps, axis=1)


def _flash_reset(m_ref, acc_ref):
    m_ref[...] = jnp.full(m_ref.shape, NEG_INF, F32)
    acc_ref[...] = jnp.zeros(acc_ref.shape, F32)


def _flash_update(s_ref, v, m_ref, acc_ref, p_ref, mask=None):
    rows, n = s_ref.shape
    for r in range(rows // STRIP):
        rs = slice(STRIP * r, STRIP * (r + 1))
        s = s_ref[rs, :]
        if mask is not None:
            s = jnp.where(mask[rs], s, NEG_INF)
        m_old = m_ref[rs, :]
        m_new = jnp.maximum(m_old, jnp.max(s, axis=1, keepdims=True))
        p_ref[rs, :] = jnp.exp2(s - _lane_tile(m_new, n)).astype(BF16)
        acc_ref[rs, :] = jnp.exp2(m_old - m_new) * acc_ref[rs, :]
        m_ref[rs, :] = m_new
    acc_ref[...] += _dot(p_ref[...], v)


def _flash_finish(acc):
    return acc * (1.0 / acc[:, ONES_LANE:ONES_LANE + 1])


NSA_TQ = 128
NSA_ROWS = NSA_REP * NSA_TQ
SLC_CHUNK = 256
WIN_SPAN = WINDOW + NSA_TQ
BIAS_LANE0 = LANES


def _nsa_body(q_ref, kvn_ref, kvc_ref, misc_ref, cov_ref, eye_ref, o_ref,
              kaug_ref, score_ref, bias_ref, sa_ref, sb_ref, pa_ref, pb_ref, ms_ref, accs_ref,
              sw_ref, pw_ref, mw_ref, accw_ref, sc_ref, pn_ref, pc_ref, oc_ref):
    c = pl.program_id(1)
    rows = NSA_ROWS
    t_row = c * NSA_TQ + (lax.broadcasted_iota(jnp.int32, (rows, 1), 0) & (NSA_TQ - 1))
    ncmp = kvc_ref.shape[2]
    nblk = kaug_ref.shape[1] // SLC_LEN

    @pl.when(c == 0)
    def _():
        for g in range(NSA_GROUPS):
            kaug_ref[g, :, :BIAS_LANE0] = kvn_ref[0, g]
            kaug_ref[g, :, BIAS_LANE0:] = jnp.zeros((kaug_ref.shape[1], NSA_TQ), BF16)

    qs = [q_ref[0, NSA_REP * g:NSA_REP * (g + 1)].reshape(rows, LANES) for g in range(NSA_GROUPS)]

    cmp_valid = (CMP_STRIDE * lax.broadcasted_iota(jnp.int32, (1, ncmp), 1) + CMP_LEN - 1) <= t_row
    for g in range(NSA_GROUPS):
        sc_ref[g] = _dot_nt(qs[g], kvc_ref[g, 0])
    for g in range(NSA_GROUPS):
        for r in range(rows // STRIP):
            rs = slice(STRIP * r, STRIP * (r + 1))
            s = jnp.where(cmp_valid[rs], sc_ref[g, rs, :], NEG_INF)
            p = jnp.where(cmp_valid[rs], jnp.exp2(s - jnp.max(s, axis=1, keepdims=True)), 0.0)
            l = jnp.sum(p, axis=1, keepdims=True)
            p = p * jnp.where(l > 0.0, 1.0 / l, 0.0)
            pn_ref[g, rs, :] = p
            pc_ref[g, rs, :] = p.astype(BF16)
        oc_ref[g] = _dot(pc_ref[g], kvc_ref[NSA_GROUPS + g, 0])

    _flash_reset(mw_ref, accw_ref)
    win_start = pl.multiple_of(jnp.maximum(c * NSA_TQ - WINDOW, 0), NSA_TQ)
    diff = t_row - (win_start + lax.broadcasted_iota(jnp.int32, (1, WIN_SPAN), 1))
    win_valid = (diff >= 0) & (diff < WINDOW)
    for g in range(NSA_GROUPS):
        sw_ref[g] = _dot_nt(qs[g], kvn_ref[0, 4 + g, pl.ds(win_start, WIN_SPAN), :])
    for g in range(NSA_GROUPS):
        _flash_update(sw_ref.at[g], kvn_ref[0, 6 + g, pl.ds(win_start, WIN_SPAN), :],
                      mw_ref.at[g], accw_ref.at[g], pw_ref.at[g], win_valid)

    @pl.when((c + 1) * (NSA_TQ // SLC_LEN) > TOPN)
    def _():
        width = NSA_GROUPS * NSA_TQ
        ps = jnp.concatenate(
            [sum(pn_ref[g, NSA_TQ * r:NSA_TQ * (r + 1), :] for r in range(NSA_REP))
             for g in range(NSA_GROUPS)], axis=0)
        hi = ps.astype(BF16)
        lo = (ps - hi.astype(F32)).astype(BF16)
        imp = _dot_nt(cov_ref[...], hi) + _dot_nt(cov_ref[...], lo)
        jidx = lax.broadcasted_iota(jnp.int32, (nblk, width), 0)
        lane_q = lax.broadcasted_iota(jnp.int32, (1, width), 1) & (NSA_TQ - 1)
        cur = c * (NSA_TQ // SLC_LEN) + lane_q // SLC_LEN
        forced = (jidx == 0) | (jidx == cur) | (jidx == cur - 1)
        score = jnp.where(jidx <= cur, jnp.where(forced, FORCE_BONUS, imp), NEG_INF)
        score_ref[...] = score
        sub = 8
        cnt = [jnp.zeros((sub, width), F32) for _ in range(nblk // sub)]
        tiles = [score[sub * v:sub * (v + 1)] for v in range(nblk // sub)]
        sidx = lax.broadcasted_iota(jnp.int32, (sub, width), 0)
        for jp in range(nblk):
            rowv = jnp.broadcast_to(score_ref[jp:jp + 1, :], (sub, width))
            for v in range(nblk // sub):
                if sub * v > jp:
                    cnt[v] = jnp.where(rowv >= tiles[v], cnt[v] + 1.0, cnt[v])
                elif sub * v + sub - 1 <= jp:
                    cnt[v] = jnp.where(rowv > tiles[v], cnt[v] + 1.0, cnt[v])
                else:
                    ge = jnp.where(rowv >= tiles[v], cnt[v] + 1.0, cnt[v])
                    gt = jnp.where(rowv > tiles[v], cnt[v] + 1.0, cnt[v])
                    cnt[v] = jnp.where(sidx + sub * v > jp, ge, gt)
        rank = jnp.concatenate(cnt, axis=0)
        bias = jnp.where(rank < float(TOPN), 0.0, NEG_INF)
        for g in range(NSA_GROUPS):
            bias_ref[g] = bias[:, NSA_TQ * g:NSA_TQ * (g + 1)]

        def write_bias(j, carry):
            r0 = pl.multiple_of(j * SLC_LEN, SLC_LEN)
            for g in range(NSA_GROUPS):
                blk = jnp.broadcast_to(bias_ref[g, pl.ds(j, 1), :], (SLC_LEN, NSA_TQ))
                kaug_ref[g, pl.ds(r0, SLC_LEN), BIAS_LANE0:] = blk.astype(BF16)
            return carry
        lax.fori_loop(0, (c + 1) * (NSA_TQ // SLC_LEN), write_bias, 0)

    _flash_reset(ms_ref, accs_ref)
    qas = [jnp.concatenate([q, eye_ref[...]], axis=1) for q in qs]
    last = c // (SLC_CHUNK // NSA_TQ)

    def slc_scores(kc, dst):
        k0 = pl.multiple_of(kc * SLC_CHUNK, SLC_CHUNK)
        for g in range(NSA_GROUPS):
            dst[g] = _dot_nt(qas[g], kaug_ref[g, pl.ds(k0, SLC_CHUNK), :])

    def slc_update(src, p_ref, kc, causal):
        k0 = pl.multiple_of(kc * SLC_CHUNK, SLC_CHUNK)
        mask = None
        if causal:
            mask = (k0 + lax.broadcasted_iota(jnp.int32, (1, SLC_CHUNK), 1)) <= t_row
        for g in range(NSA_GROUPS):
            _flash_update(src.at[g], kvn_ref[0, 2 + g, pl.ds(k0, SLC_CHUNK), :],
                          ms_ref.at[g], accs_ref.at[g], p_ref.at[g], mask)

    slc_scores(0, sa_ref)

    def slc_pair(i, carry):
        slc_scores(2 * i + 1, sb_ref)
        slc_update(sa_ref, pa_ref, 2 * i, False)
        slc_scores(2 * i + 2, sa_ref)
        slc_update(sb_ref, pb_ref, 2 * i + 1, False)
        return carry
    lax.fori_loop(0, last // 2, slc_pair, 0)
    tail = 2 * (last // 2)

    @pl.when(last > tail)
    def _():
        slc_scores(tail + 1, sb_ref)
        slc_update(sa_ref, pa_ref, tail, True)
        slc_update(sb_ref, pb_ref, tail + 1, True)

    @pl.when(last == tail)
    def _():
        slc_update(sa_ref, pa_ref, tail, True)

    sig = 1.0 / (1.0 + jnp.exp(-misc_ref[...]))
    outs = []
    for g in range(NSA_GROUPS):
        for r in range(NSA_REP):
            lane0 = GATE_LANE0 + 3 * (NSA_REP * g + r)
            rs = slice(NSA_TQ * r, NSA_TQ * (r + 1))
            o = (sig[:, lane0:lane0 + 1] * oc_ref[g, rs, :]
                 + sig[:, lane0 + 1:lane0 + 2] * _flash_finish(accs_ref[g, rs, :])
                 + sig[:, lane0 + 2:lane0 + 3] * _flash_finish(accw_ref[g, rs, :]))
            outs.append(o[:, :NSA_DH])
    o_ref[0] = jnp.concatenate(outs, axis=1).astype(BF16)


def _nsa(qn, kvn, kvcmp, misc, cov_t, eye, B, S):
    nblk = S // SLC_LEN
    ncmp = S // CMP_STRIDE
    nq = S // NSA_TQ
    rows = NSA_ROWS
    return pl.pallas_call(
        _nsa_body,
        grid=(B, nq),
        in_specs=[
            pl.BlockSpec((1, NSA_HEADS, NSA_TQ, LANES), lambda b, c: (b, 0, c, 0)),
            pl.BlockSpec((1, 8, S, LANES), lambda b, c: (b, 0, 0, 0)),
            pl.BlockSpec((4, 1, ncmp, LANES), lambda b, c: (0, b, 0, 0)),
            pl.BlockSpec((NSA_TQ, LANES), lambda b, c: (b * nq + c, 0)),
            pl.BlockSpec(cov_t.shape, lambda b, c: (0, 0)),
            pl.BlockSpec(eye.shape, lambda b, c: (0, 0)),
        ],
        out_specs=pl.BlockSpec((1, NSA_TQ, NSA_HEADS * NSA_DH), lambda b, c: (b, c, 0)),
        out_shape=jax.ShapeDtypeStruct((B, S, NSA_HEADS * NSA_DH), BF16),
        scratch_shapes=[
            pltpu.VMEM((NSA_GROUPS, S, BIAS_LANE0 + NSA_TQ), BF16),
            pltpu.VMEM((nblk, NSA_GROUPS * NSA_TQ), F32),
            pltpu.VMEM((NSA_GROUPS, nblk, NSA_TQ), F32),
            pltpu.VMEM((NSA_GROUPS, rows, SLC_CHUNK), F32),
            pltpu.VMEM((NSA_GROUPS, rows, SLC_CHUNK), F32),
            pltpu.VMEM((NSA_GROUPS, rows, SLC_CHUNK), BF16),
            pltpu.VMEM((NSA_GROUPS, rows, SLC_CHUNK), BF16),
            pltpu.VMEM((NSA_GROUPS, rows, LANES), F32),
            pltpu.VMEM((NSA_GROUPS, rows, LANES), F32),
            pltpu.VMEM((NSA_GROUPS, rows, WIN_SPAN), F32),
            pltpu.VMEM((NSA_GROUPS, rows, WIN_SPAN), BF16),
            pltpu.VMEM((NSA_GROUPS, rows, LANES), F32),
            pltpu.VMEM((NSA_GROUPS, rows, LANES), F32),
            pltpu.VMEM((NSA_GROUPS, rows, ncmp), F32),
            pltpu.VMEM((NSA_GROUPS, rows, ncmp), F32),
            pltpu.VMEM((NSA_GROUPS, rows, ncmp), BF16),
            pltpu.VMEM((NSA_GROUPS, rows, LANES), F32),
        ],
        compiler_params=_params("arbitrary", "arbitrary"),
        name="nsa",
    )(qn, kvn, kvcmp, misc, cov_t, eye)


MLA_TQ = 512
MLA_CHUNK = 256
MLA_HPB = 4


def _mla_body(q_ref, k_ref, v_ref, o_ref, sa_ref, sb_ref, pa_ref, pb_ref, m_ref, acc_ref):
    qi = pl.program_id(2)
    t_row = qi * MLA_TQ + lax.broadcasted_iota(jnp.int32, (MLA_TQ, 1), 0)
    _flash_reset(m_ref, acc_ref)

    def scores(kc, dst):
        k0 = pl.multiple_of(kc * MLA_CHUNK, MLA_CHUNK)
        for j in range(MLA_HPB):
            dst[j] = _dot_nt(q_ref[0, j], k_ref[0, j, pl.ds(k0, MLA_CHUNK), :])

    def update(src, p_ref, kc, causal):
        k0 = pl.multiple_of(kc * MLA_CHUNK, MLA_CHUNK)
        mask = None
        if causal:
            mask = (k0 + lax.broadcasted_iota(jnp.int32, (1, MLA_CHUNK), 1)) <= t_row
        for j in range(MLA_HPB):
            _flash_update(src.at[j], v_ref[0, j, pl.ds(k0, MLA_CHUNK), :],
                          m_ref.at[j], acc_ref.at[j], p_ref.at[j], mask)

    scores(0, sa_ref)

    def pair(i, carry):
        scores(2 * i + 1, sb_ref)
        update(sa_ref, pa_ref, 2 * i, False)
        scores(2 * i + 2, sa_ref)
        update(sb_ref, pb_ref, 2 * i + 1, False)
        return carry
    lax.fori_loop(0, qi, pair, 0)
    tail = (MLA_TQ // MLA_CHUNK) * qi
    scores(tail + 1, sb_ref)
    update(sa_ref, pa_ref, tail, True)
    update(sb_ref, pb_ref, tail + 1, True)
    o_ref[0] = jnp.concatenate([_flash_finish(acc_ref[j])[:, :MLA_V] for j in range(MLA_HPB)],
                               axis=1).astype(BF16)


def _mla(q, k, v, B, S):
    return pl.pallas_call(
        _mla_body,
        grid=(B, MLA_HEADS // MLA_HPB, S // MLA_TQ),
        in_specs=[
            pl.BlockSpec((1, MLA_HPB, MLA_TQ, LANES), lambda b, h, i: (b, h, i, 0)),
            pl.BlockSpec((1, MLA_HPB, S, LANES), lambda b, h, i: (b, h, 0, 0)),
            pl.BlockSpec((1, MLA_HPB, S, LANES), lambda b, h, i: (b, h, 0, 0)),
        ],
        out_specs=pl.BlockSpec((1, MLA_TQ, MLA_HPB * MLA_V), lambda b, h, i: (b, i, h)),
        out_shape=jax.ShapeDtypeStruct((B, S, MLA_HEADS * MLA_V), BF16),
        scratch_shapes=[
            pltpu.VMEM((MLA_HPB, MLA_TQ, MLA_CHUNK), F32),
            pltpu.VMEM((MLA_HPB, MLA_TQ, MLA_CHUNK), F32),
            pltpu.VMEM((MLA_HPB, MLA_TQ, MLA_CHUNK), BF16),
            pltpu.VMEM((MLA_HPB, MLA_TQ, MLA_CHUNK), BF16),
            pltpu.VMEM((MLA_HPB, MLA_TQ, LANES), F32),
            pltpu.VMEM((MLA_HPB, MLA_TQ, LANES), F32),
        ],
        compiler_params=_params("arbitrary", "arbitrary", "arbitrary"),
        name="mla",
    )(q, k, v)


def _mem_kv_body(m_ref, wk_ref, wv_ref, k_ref, v_ref):
    mb = m_ref[...].astype(BF16)
    k_ref[...] = _dot(mb, wk_ref[...]).astype(BF16)
    v_ref[...] = _dot(mb, wv_ref[...]).astype(BF16)


def _mem_kv(memf, wk, wv, tm):
    R, D = memf.shape
    tok = lambda i: (i, 0)
    const = lambda i: (0, 0)
    return pl.pallas_call(
        _mem_kv_body,
        grid=(R // tm,),
        in_specs=[pl.BlockSpec((tm, D), tok), pl.BlockSpec(wk.shape, const),
                  pl.BlockSpec(wv.shape, const)],
        out_specs=[pl.BlockSpec((tm, D), tok), pl.BlockSpec((tm, D), tok)],
        out_shape=[jax.ShapeDtypeStruct((R, D), BF16), jax.ShapeDtypeStruct((R, D), BF16)],
        compiler_params=_params("arbitrary"),
        name="mem_kv",
    )(memf, wk, wv)


def _mem_attn_body(x0_ref, on_ref, om_ref, wmix_ref, g1_ref, b1_ref,
                   k_ref, v_ref, wq_ref, wo_ref, g_ref, b_ref, o_ref, *, alpha):
    half = on_ref.shape[1]
    mix = _dot(on_ref[...], wmix_ref[:half, :]) + _dot(om_ref[...], wmix_ref[half:, :])
    x = _layer_norm(alpha * x0_ref[...] + mix, g1_ref[...], b1_ref[...])
    D = x.shape[1]
    dh = D // MEM_HEADS
    q = (_dot(x.astype(BF16), wq_ref[...]) * (dh ** -0.5 * LOG2E)).astype(BF16)
    outs = []
    for h in range(MEM_HEADS):
        cs = slice(dh * h, dh * (h + 1))
        s = _dot_nt(q[:, cs], k_ref[0, :, cs])
        p = jnp.exp2(s - jnp.max(s, axis=1, keepdims=True))
        l = jnp.sum(p, axis=1, keepdims=True)
        outs.append((_dot(p.astype(BF16), v_ref[0, :, cs]) * (1.0 / l)).astype(BF16))
    o = jnp.concatenate(outs, axis=1)
    y = _dot(o, wo_ref[...])
    o_ref[...] = _layer_norm(alpha * x + y, g_ref[...], b_ref[...])


def _mem_attn(xf, o_nsa, o_mla, w_o, g1, b1, k_mem, v_mem, wq, wo, g, b, alpha, S, tm):
    T, D = xf.shape
    nst = S // tm
    M = k_mem.shape[1]
    tok = lambda i: (i, 0)
    const = lambda i: (0, 0)
    memb = lambda i: (i // nst, 0, 0)
    return pl.pallas_call(
        functools.partial(_mem_attn_body, alpha=alpha),
        grid=(T // tm,),
        in_specs=[
            pl.BlockSpec((tm, D), tok),
            pl.BlockSpec((tm, o_nsa.shape[1]), tok),
            pl.BlockSpec((tm, o_mla.shape[1]), tok),
            pl.BlockSpec(w_o.shape, const),
            pl.BlockSpec((1, D), const),
            pl.BlockSpec((1, D), const),
            pl.BlockSpec((1, M, D), memb),
            pl.BlockSpec((1, M, D), memb),
            pl.BlockSpec(wq.shape, const),
            pl.BlockSpec(wo.shape, const),
            pl.BlockSpec((1, D), const),
            pl.BlockSpec((1, D), const),
        ],
        out_specs=pl.BlockSpec((tm, D), tok),
        out_shape=jax.ShapeDtypeStruct((T, D), F32),
        compiler_params=_params("arbitrary"),
        name="mem_attn",
    )(xf, o_nsa, o_mla, w_o, g1, b1, k_mem, v_mem, wq, wo, g, b)


HALO = 8


FFN_SLAB = 256
FFN_TM = 512


def _ffn_body(x_ref, xh_ref, wg_ref, wu_ref, cw_ref, cb_ref, wd_ref, g_ref, b_ref, o_ref,
              act_ref, *, alpha, seq_tiles):
    i = pl.program_id(0)
    x = x_ref[...]
    xb = x.astype(BF16)
    xhb = xh_ref[...].astype(BF16)
    tm = x.shape[0]
    row = lax.broadcasted_iota(jnp.int32, (tm, 1), 0)
    seq_start = i % seq_tiles == 0
    for c0 in range(0, wg_ref.shape[1], FFN_SLAB):
        cs = slice(c0, c0 + FFN_SLAB)
        gate = _dot(xb, wg_ref[:, cs])
        up = _dot(xb, wu_ref[:, cs])
        halo = jnp.where(seq_start, 0.0, _dot(xhb, wg_ref[:, cs]))
        g1 = jnp.where(row == 0, halo[HALO - 1:HALO], pltpu.roll(gate, 1, 0))
        g2 = jnp.where(row == 0, halo[HALO - 2:HALO - 1],
                       jnp.where(row == 1, halo[HALO - 1:HALO], pltpu.roll(gate, 2, 0)))
        conv = cw_ref[0:1, cs] * g2 + cw_ref[1:2, cs] * g1 + cw_ref[2:3, cs] * gate + cb_ref[:, cs]
        act_ref[:, cs] = (conv * (1.0 / (1.0 + jnp.exp(-conv))) * up).astype(BF16)
    y = _dot(act_ref[...], wd_ref[...])
    o_ref[...] = _layer_norm(alpha * x + y, g_ref[...], b_ref[...])


def _ffn(xf, wg, wu, cw, cb, wd, g, b, alpha, S, tm):
    T, D = xf.shape
    dff = wg.shape[1]
    assert dff % FFN_SLAB == 0
    tok = lambda i: (i, 0)
    const = lambda i: (0, 0)
    return pl.pallas_call(
        functools.partial(_ffn_body, alpha=alpha, seq_tiles=S // tm),
        grid=(T // tm,),
        in_specs=[
            pl.BlockSpec((tm, D), tok),
            pl.BlockSpec((HALO, D), lambda i: (jnp.maximum(i * (tm // HALO) - 1, 0), 0)),
            pl.BlockSpec((D, dff), const),
            pl.BlockSpec((D, dff), const),
            pl.BlockSpec((CONV_WIDTH, dff), const),
            pl.BlockSpec((1, dff), const),
            pl.BlockSpec((dff, D), const),
            pl.BlockSpec((1, D), const),
            pl.BlockSpec((1, D), const),
        ],
        out_specs=pl.BlockSpec((tm, D), tok),
        out_shape=jax.ShapeDtypeStruct((T, D), F32),
        scratch_shapes=[pltpu.VMEM((tm, dff), BF16)],
        compiler_params=_params("arbitrary"),
        name="ffn",
    )(xf, xf, wg, wu, cw, cb, wd, g, b)


def _inv_freq_row(dim, lane_lo, lane_hi, period):
    inv = ROPE_THETA ** (-np.arange(0, dim, 2, dtype=np.float64) / dim)
    row = np.zeros((1, LANES), np.float32)
    for lane in range(lane_lo, lane_hi):
        row[0, lane] = inv[(lane % period) % (dim // 2)]
    return jnp.asarray(row)


def _cover_t(S):
    nc = S // CMP_STRIDE
    ns = S // SLC_LEN
    cs = np.arange(nc)[:, None] * CMP_STRIDE
    ss = np.arange(ns)[None, :] * SLC_LEN
    cover = np.clip(np.minimum(cs + CMP_LEN, ss + SLC_LEN) - np.maximum(cs, ss), 0, None) / CMP_LEN
    cover[nc - 1:] = 0.0
    return jnp.asarray(cover.T, dtype=BF16)


def _eye_aug():
    eye = np.zeros((NSA_ROWS, NSA_TQ), np.float32)
    r = np.arange(NSA_ROWS)
    eye[r, r % NSA_TQ] = 1.0
    return jnp.asarray(eye, dtype=BF16)


def _permute_w_in(w):
    D = w.shape[0]
    c1 = NSA_HEADS * NSA_DH
    c2 = c1 + 3 * 2 * NSA_GROUPS * NSA_DH
    c3 = c2 + 3 * NSA_HEADS
    c4 = c3 + MLA_Q_RANK
    c5 = c4 + MLA_KV_RANK
    c6 = c5 + MLA_ROPE
    pad = jnp.zeros((D, IN_COLS_PAD - C_MISC - MLA_ROPE - 3 * NSA_HEADS), w.dtype)
    return jnp.concatenate(
        [w[:, :c2], w[:, c3:c5], w[:, c5:c6], w[:, c2:c3], pad], axis=1).astype(BF16)


def _permute_w_uq(w):
    r = w.shape[0]
    w3 = w.reshape(r, MLA_HEADS, MLA_NOPE + MLA_ROPE)
    pad = jnp.zeros((r, MLA_HEADS, LANES - MLA_NOPE - MLA_ROPE), w.dtype)
    return jnp.concatenate([w3, pad], axis=2).reshape(r, MLA_HEADS * LANES).astype(BF16)


def _permute_w_ukv(w):
    r = w.shape[0]
    w3 = w.reshape(r, MLA_HEADS, MLA_NOPE + MLA_V)
    return jnp.concatenate([w3[:, :, :MLA_NOPE].reshape(r, -1),
                            w3[:, :, MLA_NOPE:].reshape(r, -1)], axis=1).astype(BF16)


def _pad_lanes(a):
    return jnp.concatenate([a, jnp.zeros(a.shape[:-1] + (LANES - a.shape[-1],), a.dtype)], axis=-1)


def kernel(x, mem, positions, w_in, nsa_k_pos, nsa_ck_w1, nsa_ck_b1, nsa_ck_w2, nsa_ck_b2,
           nsa_v_pos, nsa_cv_w1, nsa_cv_b1, nsa_cv_w2, nsa_cv_b2,
           mla_q_norm, mla_w_uq, mla_kv_norm, mla_w_ukv, w_o, ln1_g, ln1_b,
           mem_wq, mem_wk, mem_wv, mem_wo, ln2_g, ln2_b,
           ffn_w_up, ffn_conv_w, ffn_conv_b, ffn_w_down, ln3_g, ln3_b):
    B, S, D = x.shape
    T = B * S
    depth = w_in.shape[0]
    alpha = (2.0 * depth) ** 0.25
    d_ff = ffn_w_down.shape[1]
    tm = min(512, S)
    assert S % MLA_TQ == 0 and S >= WIN_SPAN and S % tm == 0
    assert (B * mem.shape[1]) % 256 == 0

    pos = positions.reshape(T, 1)
    pos_cmp = positions[:, CMP_LEN - 1::CMP_STRIDE]
    pos_cmp = jnp.concatenate([pos_cmp, pos_cmp[:, -1:]], axis=1)[:, :, None]
    inv_cmp = _inv_freq_row(NSA_DH, 0, NSA_DH, NSA_DH)
    inv_tok = (_inv_freq_row(NSA_DH, 0, NSA_DH // 2, NSA_DH)
               + _inv_freq_row(MLA_ROPE, NSA_DH // 2, NSA_DH // 2 + MLA_ROPE // 2, MLA_ROPE // 2))
    cov_t = _cover_t(S)
    eye = _eye_aug()
    memf = mem.reshape(B * mem.shape[1], D)

    xf = x.reshape(T, D)
    for l in range(depth):
        qn, kvn, kvc, misc, q_m, k_m, v_m = _inproj(
            pos, inv_tok, xf, _permute_w_in(w_in[l]), mla_q_norm[l][None, :],
            mla_kv_norm[l][None, :], _permute_w_uq(mla_w_uq[l]), _permute_w_ukv(mla_w_ukv[l]),
            B, S, tm)
        kvcmp = _compress(
            pos_cmp, inv_cmp, kvc,
            *_compress_weights(nsa_k_pos[l], nsa_ck_w1[l], nsa_ck_b1[l], nsa_ck_w2[l], nsa_ck_b2[l],
                               nsa_v_pos[l], nsa_cv_w1[l], nsa_cv_b1[l], nsa_cv_w2[l], nsa_cv_b2[l]),
            B, S)
        o_nsa = _nsa(qn, kvn, kvcmp, misc, cov_t, eye, B, S)
        o_mla = _mla(q_m, k_m, v_m, B, S)
        k_mem, v_mem = _mem_kv(memf, mem_wk[l].astype(BF16), mem_wv[l].astype(BF16), 256)
        xf = _mem_attn(xf, o_nsa.reshape(T, -1), o_mla.reshape(T, -1), w_o[l].astype(BF16),
                       ln1_g[l][None, :], ln1_b[l][None, :],
                       k_mem.reshape(B, -1, D), v_mem.reshape(B, -1, D),
                       mem_wq[l].astype(BF16), mem_wo[l].astype(BF16),
                       ln2_g[l][None, :], ln2_b[l][None, :], alpha, S, tm)
        xf = _ffn(xf, ffn_w_up[l][:, :d_ff].astype(BF16), ffn_w_up[l][:, d_ff:].astype(BF16),
                  ffn_conv_w[l], ffn_conv_b[l][None, :], ffn_w_down[l].astype(BF16),
                  ln3_g[l][None, :], ln3_b[l][None, :], alpha, S, FFN_TM)
    return xf.reshape(B, S, D)
```

```python
import functools
import math

import numpy as np
import jax
import jax.numpy as jnp
from jax import lax
from jax.experimental import pallas as pl
from jax.experimental.pallas import tpu as pltpu

F32 = jnp.float32
BF16 = jnp.bfloat16

NSA_HEADS = 8
NSA_GROUPS = 2
NSA_REP = NSA_HEADS // NSA_GROUPS
NSA_DH = 64
CMP_STRIDE = 16
CMP_LEN = 32
SLC_LEN = 64
TOPN = 16
WINDOW = 512
CMP_HIDDEN = 128
FORCE_BONUS = 1e4
MLA_HEADS = 8
MLA_Q_RANK = 384
MLA_KV_RANK = 256
MLA_NOPE = 64
MLA_ROPE = 32
MLA_V = 64
MEM_HEADS = 4
CONV_WIDTH = 3
ROPE_THETA = 10000.0
LN_EPS = 1e-5
RMS_EPS = 1e-6
NEG_INF = -1e30
LOG2E = math.log2(math.e)

LANES = 128
VMEM_LIMIT = 56 * 1024 * 1024

C_Q = 0
C_KVC = 512
C_KVN = 768
C_LAT = 1280
C_MISC = 1920
IN_COLS_PAD = 2048
GATE_LANE0 = MLA_ROPE
MLA_PE2 = 0
MLA_PE1 = 64
ONES_LANE = 64


def _dot(a, b):
    return jnp.dot(a, b, preferred_element_type=F32)


def _dot_nt(a, b):
    return lax.dot_general(a, b, (((1,), (1,)), ((), ())), preferred_element_type=F32)


def _layer_norm(y, g, b):
    mu = jnp.mean(y, axis=-1, keepdims=True)
    d = y - mu
    var = jnp.mean(d * d, axis=-1, keepdims=True)
    return d * lax.rsqrt(var + LN_EPS) * g + b


def _params(*sem):
    return pltpu.CompilerParams(dimension_semantics=sem, vmem_limit_bytes=VMEM_LIMIT)


def _rope_tables(pos_col, inv_row, half):
    ang = pos_col * inv_row
    cos = jnp.cos(ang)
    sin = jnp.sin(ang)
    lane = lax.broadcasted_iota(jnp.int32, (1, LANES), 1)
    upper = (lane & (2 * half - 1)) >= half
    rot = inv_row != 0.0
    sin_hi = jnp.where(upper & rot, sin, 0.0)
    sin_lo = jnp.where(upper | (~rot), 0.0, -sin)
    return cos, sin_hi, sin_lo


def _apply_rope(v, tabs, half):
    cos, sin_hi, sin_lo = tabs
    return v * cos + pltpu.roll(v, half, 1) * sin_hi + pltpu.roll(v, LANES - half, 1) * sin_lo


def _rms_norm(v, g):
    return v * lax.rsqrt(jnp.mean(v * v, axis=-1, keepdims=True) + RMS_EPS) * g


def _inproj_body(pos_ref, inv_ref, x_ref, w_ref, gq_ref, gkv_ref, wq_ref, wkv_ref,
                 qn_ref, kvn_ref, kvc_ref, misc_ref, qm_ref, km_ref, vm_ref):
    xb = x_ref[...].astype(BF16)
    lane = lax.broadcasted_iota(jnp.int32, (1, LANES), 1)
    low = lane < NSA_DH
    qscale = NSA_DH ** -0.5 * LOG2E

    ang = pos_ref[...].astype(F32) * inv_ref[...]
    cos, sin = jnp.cos(ang), jnp.sin(ang)
    nf, mf = NSA_DH // 2, MLA_ROPE // 2

    def tile_nsa(t):
        t = jnp.where(lane < nf, t, 0.0)
        t = t + pltpu.roll(t, nf, 1)
        return t + pltpu.roll(t, 2 * nf, 1)

    def place_mla(t):
        t = jnp.where((lane >= nf) & (lane < nf + mf), t, 0.0)
        return pltpu.roll(t, MLA_PE1 - nf, 1) + pltpu.roll(t, LANES + MLA_PE2 - nf, 1)

    upper = (lane & (NSA_DH - 1)) >= nf
    sin_n = tile_nsa(sin)
    tabs = (tile_nsa(cos), jnp.where(upper, sin_n, 0.0), jnp.where(upper, 0.0, -sin_n))
    pe1 = (lane >= MLA_PE1) & (lane < MLA_PE1 + mf)
    pe2 = (lane >= MLA_PE2) & (lane < MLA_PE2 + mf)
    pe_lanes = pe1 | pe2
    sin_m = place_mla(sin)
    cos_m = jnp.where(pe_lanes, place_mla(cos), 1.0)
    sin_m = jnp.where(pe1, -sin_m, jnp.where(pe2, sin_m, 0.0))

    def rope_mla(v):
        return v * cos_m + pltpu.roll(v, LANES // 2, 1) * sin_m

    def proj(c0, n):
        return _dot(xb, w_ref[:, c0:c0 + n])

    def split_store(v, ref, idx_lo, idx_hi, pad=0.0):
        ref[0, idx_lo] = jnp.where(low, v, pad).astype(BF16)
        ref[0, idx_hi] = jnp.where(low, pltpu.roll(v, NSA_DH, 1), pad).astype(BF16)

    ones_pad = jnp.where(lane == ONES_LANE, 1.0, 0.0)

    for slab in range(2):
        h = proj(C_Q + 256 * slab, 256)
        for j in range(2):
            r = _apply_rope(h[:, LANES * j:LANES * (j + 1)], tabs, NSA_DH // 2) * qscale
            split_store(r, qn_ref, 4 * slab + 2 * j, 4 * slab + 2 * j + 1)

    h = proj(C_KVC, 256)
    kvc_ref[0] = h[:, :LANES]
    kvc_ref[1] = h[:, LANES:]

    for slab in range(2):
        h = proj(C_KVN + 256 * slab, 256)
        k = _apply_rope(h[:, :LANES], tabs, NSA_DH // 2)
        split_store(k, kvn_ref, 4 * slab, 4 * slab + 1)
        split_store(h[:, LANES:], kvn_ref, 4 * slab + 2, 4 * slab + 3, ones_pad)

    lat = [proj(C_LAT + 256 * i, 256) for i in range((IN_COLS_PAD - C_LAT) // 256)]
    misc = lat[2][:, LANES:]
    misc_ref[...] = misc
    mq = jnp.concatenate([lat[0], lat[1][:, :LANES]], axis=1)
    mkv = jnp.concatenate([lat[1][:, LANES:], lat[2][:, :LANES]], axis=1)
    mscale = (MLA_NOPE + MLA_ROPE) ** -0.5 * LOG2E
    qn = _rms_norm(mq, gq_ref[...]).astype(BF16)
    kvn = _rms_norm(mkv, gkv_ref[...]).astype(BF16)
    kpe = jnp.where(pe_lanes, rope_mla(misc), 0.0)
    kcols = MLA_HEADS * LANES
    for slab in range(MLA_HEADS // 2):
        cs = slice(256 * slab, 256 * (slab + 1))
        hq = _dot(qn, wq_ref[:, cs])
        hk = _dot(kvn, wkv_ref[:, cs])
        hv = _dot(kvn, wkv_ref[:, kcols + 256 * slab:kcols + 256 * (slab + 1)])
        for j in range(2):
            ls = slice(LANES * j, LANES * (j + 1))
            qm_ref[0, 2 * slab + j] = (rope_mla(hq[:, ls]) * mscale).astype(BF16)
            km_ref[0, 2 * slab + j] = (hk[:, ls] + kpe).astype(BF16)
            vm_ref[0, 2 * slab + j] = (hv[:, ls] + ones_pad).astype(BF16)


def _inproj(pos, inv_row, xf, w_in_p, gq, gkv, wq_p, wkv_p, B, S, tm):
    T = B * S
    nst = S // tm
    tok = lambda i: (i, 0)
    const = lambda i: (0, 0)
    head_blk = lambda i: (i // nst, 0, i % nst, 0)
    heads = jax.ShapeDtypeStruct((B, 8, S, LANES), BF16)
    return pl.pallas_call(
        _inproj_body,
        grid=(T // tm,),
        in_specs=[
            pl.BlockSpec((tm, 1), tok),
            pl.BlockSpec((1, LANES), const),
            pl.BlockSpec((tm, xf.shape[1]), tok),
            pl.BlockSpec(w_in_p.shape, const),
            pl.BlockSpec(gq.shape, const),
            pl.BlockSpec(gkv.shape, const),
            pl.BlockSpec(wq_p.shape, const),
            pl.BlockSpec(wkv_p.shape, const),
        ],
        out_specs=[
            pl.BlockSpec((1, 8, tm, LANES), head_blk),
            pl.BlockSpec((1, 8, tm, LANES), head_blk),
            pl.BlockSpec((2, tm, LANES), lambda i: (0, i, 0)),
            pl.BlockSpec((tm, LANES), tok),
            pl.BlockSpec((1, 8, tm, LANES), head_blk),
            pl.BlockSpec((1, 8, tm, LANES), head_blk),
            pl.BlockSpec((1, 8, tm, LANES), head_blk),
        ],
        out_shape=[
            heads,
            heads,
            jax.ShapeDtypeStruct((2, T, LANES), F32),
            jax.ShapeDtypeStruct((T, LANES), F32),
            heads,
            heads,
            heads,
        ],
        compiler_params=_params("arbitrary"),
        name="inproj",
    )(pos, inv_row, xf, w_in_p, gq, gkv, wq_p, wkv_p)


def _compress_body(pos_ref, inv_ref, x_ref, pe_ref, w1_ref, b1_ref, w2_ref, b2_ref, o_ref):
    is_k = pl.program_id(0) == 0
    nch = o_ref.shape[2]
    a1 = jnp.zeros((nch, NSA_GROUPS * CMP_HIDDEN), F32)
    a2 = jnp.zeros((nch, NSA_GROUPS * CMP_HIDDEN), F32)
    for l in range(CMP_STRIDE):
        xl = x_ref.at[0, 0][pl.ds(l, nch, stride=CMP_STRIDE), :]
        a1 = a1 + _dot((xl + pe_ref[0, l:l + 1, :]).astype(BF16), w1_ref[0, l])
        a2 = a2 + _dot((xl + pe_ref[0, CMP_STRIDE + l:CMP_STRIDE + l + 1, :]).astype(BF16),
                       w1_ref[0, CMP_STRIDE + l])
    pre = a1 + pltpu.roll(a2, nch - 1, 0) + b1_ref[0]
    hid = jax.nn.gelu(pre, approximate=True)
    out = _dot(hid.astype(BF16), w2_ref[0]) + b2_ref[0]
    tabs = _rope_tables(pos_ref[0].astype(F32), inv_ref[...], NSA_DH // 2)
    row = lax.broadcasted_iota(jnp.int32, (nch, 1), 0)
    for g in range(NSA_GROUPS):
        og = out[:, LANES * g:LANES * (g + 1)]
        og = jnp.where(is_k, _apply_rope(og, tabs, NSA_DH // 2), og)
        o_ref[g, 0] = jnp.where(row < nch - 1, og, 0.0).astype(BF16)


def _compress(pos_cmp, inv_nsa, kvc, pe, w1, b1, w2, b2, B, S):
    nch = S // CMP_STRIDE
    x = kvc.reshape(2, B, S, LANES)
    kv = lambda j, b: (j, 0, 0)
    kv4 = lambda j, b: (j, 0, 0, 0)
    return pl.pallas_call(
        _compress_body,
        grid=(2, B),
        in_specs=[
            pl.BlockSpec((1, nch, 1), lambda j, b: (b, 0, 0)),
            pl.BlockSpec((1, LANES), lambda j, b: (0, 0)),
            pl.BlockSpec((1, 1, S, LANES), lambda j, b: (j, b, 0, 0)),
            pl.BlockSpec((1,) + pe.shape[1:], kv),
            pl.BlockSpec((1,) + w1.shape[1:], kv4),
            pl.BlockSpec((1,) + b1.shape[1:], kv),
            pl.BlockSpec((1,) + w2.shape[1:], kv),
            pl.BlockSpec((1,) + b2.shape[1:], kv),
        ],
        out_specs=pl.BlockSpec((NSA_GROUPS, 1, nch, LANES), lambda j, b: (j, b, 0, 0)),
        out_shape=jax.ShapeDtypeStruct((2 * NSA_GROUPS, B, nch, LANES), BF16),
        compiler_params=_params("arbitrary", "arbitrary"),
        name="compress",
    )(pos_cmp, inv_nsa, x, pe, w1, b1, w2, b2)


def _compress_weights(k_pos, k_w1, k_b1, k_w2, k_b2, v_pos, v_w1, v_b1, v_w2, v_b2):
    def one(pos, w1, b1, w2, b2):
        w1l = w1.reshape(CMP_LEN, NSA_DH, CMP_HIDDEN)
        z1 = jnp.zeros_like(w1l)
        w1bd = jnp.concatenate([jnp.concatenate([w1l, z1], axis=2),
                                jnp.concatenate([z1, w1l], axis=2)], axis=1)
        w2p = _pad_lanes(w2)
        z2 = jnp.zeros_like(w2p)
        w2bd = jnp.concatenate([jnp.concatenate([w2p, z2], axis=1),
                                jnp.concatenate([z2, w2p], axis=1)], axis=0)
        return (jnp.tile(pos, (1, NSA_GROUPS)), w1bd.astype(BF16), jnp.tile(b1, NSA_GROUPS)[None, :],
                w2bd.astype(BF16), jnp.tile(_pad_lanes(b2), NSA_GROUPS)[None, :])
    k = one(k_pos, k_w1, k_b1, k_w2, k_b2)
    v = one(v_pos, v_w1, v_b1, v_w2, v_b2)
    return tuple(jnp.stack([a, b]) for a, b in zip(k, v))


STRIP = 64


def _lane_tile(col, n):
    reps = [col] * (n // LANES)
    if n % LANES:
        reps.append(col[:, :n % LANES])
    return reps[0] if len(reps) == 1 else jnp.concatenate(reps, axis=1)


def _flash_reset(m_ref, acc_ref):
    m_ref[...] = jnp.full(m_ref.shape, NEG_INF, F32)
    acc_ref[...] = jnp.zeros(acc_ref.shape, F32)


def _flash_update(s_ref, v, m_ref, acc_ref, p_ref, mask=None):
    rows, n = s_ref.shape
    for r in range(rows // STRIP):
        rs = slice(STRIP * r, STRIP * (r + 1))
        s = s_ref[rs, :]
        if mask is not None:
            s = jnp.where(mask[rs], s, NEG_INF)
        if m_ref is None:
            p_ref[rs, :] = jnp.exp2(s - jnp.max(s, axis=1, keepdims=True)).astype(BF16)
            continue
        m_old = m_ref[rs, :]
        m_new = jnp.maximum(m_old, jnp.max(s, axis=1, keepdims=True))
        p_ref[rs, :] = jnp.exp2(s - _lane_tile(m_new, n)).astype(BF16)
        acc_ref[rs, :] = jnp.exp2(m_old - m_new) * acc_ref[rs, :]
        m_ref[rs, :] = m_new
    if m_ref is None:
        acc_ref[...] = _dot(p_ref[...], v)
    else:
        acc_ref[...] += _dot(p_ref[...], v)


def _flash_finish(acc):
    return acc * (1.0 / acc[:, ONES_LANE:ONES_LANE + 1])


NSA_TQ = 128
NSA_ROWS = NSA_REP * NSA_TQ
SLC_CHUNK = 512
WIN_SPAN = WINDOW + NSA_TQ
BIAS_LANE0 = LANES


def _nsa_body(q_ref, kvn_ref, kvc_ref, misc_ref, cov_ref, eye_ref, o_ref,
              kaug_ref, score_ref, bias_ref, sa_ref, sb_ref, pa_ref, pb_ref, ms_ref, accs_ref,
              sw_ref, pw_ref, accw_ref, sc_ref, pn_ref, pc_ref, oc_ref):
    c = pl.program_id(1)
    rows = NSA_ROWS
    t_row = c * NSA_TQ + (lax.broadcasted_iota(jnp.int32, (rows, 1), 0) & (NSA_TQ - 1))
    ncmp = kvc_ref.shape[2]
    nblk = kaug_ref.shape[1] // SLC_LEN

    @pl.when(c == 0)
    def _():
        for g in range(NSA_GROUPS):
            kaug_ref[g, :, :BIAS_LANE0] = kvn_ref[0, g]
            kaug_ref[g, :, BIAS_LANE0:] = jnp.zeros((kaug_ref.shape[1], NSA_TQ), BF16)

    qs = [q_ref[0, NSA_REP * g:NSA_REP * (g + 1)].reshape(rows, LANES) for g in range(NSA_GROUPS)]

    cmp_valid = (CMP_STRIDE * lax.broadcasted_iota(jnp.int32, (1, ncmp), 1) + CMP_LEN - 1) <= t_row
    for g in range(NSA_GROUPS):
        sc_ref[g] = _dot_nt(qs[g], kvc_ref[g, 0])
    for g in range(NSA_GROUPS):
        for r in range(rows // STRIP):
            rs = slice(STRIP * r, STRIP * (r + 1))
            s = jnp.where(cmp_valid[rs], sc_ref[g, rs, :], NEG_INF)
            p = jnp.where(cmp_valid[rs], jnp.exp2(s - jnp.max(s, axis=1, keepdims=True)), 0.0)
            l = jnp.sum(p, axis=1, keepdims=True)
            p = p * jnp.where(l > 0.0, 1.0 / l, 0.0)
            pn_ref[g, rs, :] = p
            pc_ref[g, rs, :] = p.astype(BF16)
        oc_ref[g] = _dot(pc_ref[g], kvc_ref[NSA_GROUPS + g, 0])

    win_start = pl.multiple_of(jnp.maximum(c * NSA_TQ - WINDOW, 0), NSA_TQ)
    diff = t_row - (win_start + lax.broadcasted_iota(jnp.int32, (1, WIN_SPAN), 1))
    win_valid = (diff >= 0) & (diff < WINDOW)
    for g in range(NSA_GROUPS):
        sw_ref[g] = _dot_nt(qs[g], kvn_ref[0, 4 + g, pl.ds(win_start, WIN_SPAN), :])
    for g in range(NSA_GROUPS):
        _flash_update(sw_ref.at[g], kvn_ref[0, 6 + g, pl.ds(win_start, WIN_SPAN), :],
                      None, accw_ref.at[g], pw_ref.at[g], win_valid)

    @pl.when((c + 1) * (NSA_TQ // SLC_LEN) > TOPN)
    def _():
        width = NSA_GROUPS * NSA_TQ
        ps = jnp.concatenate(
            [sum(pn_ref[g, NSA_TQ * r:NSA_TQ * (r + 1), :] for r in range(NSA_REP))
             for g in range(NSA_GROUPS)], axis=0)
        hi = ps.astype(BF16)
        lo = (ps - hi.astype(F32)).astype(BF16)
        imp = _dot_nt(cov_ref[...], hi) + _dot_nt(cov_ref[...], lo)
        jidx = lax.broadcasted_iota(jnp.int32, (nblk, width), 0)
        lane_q = lax.broadcasted_iota(jnp.int32, (1, width), 1) & (NSA_TQ - 1)
        cur = c * (NSA_TQ // SLC_LEN) + lane_q // SLC_LEN
        forced = (jidx == 0) | (jidx == cur) | (jidx == cur - 1)
        score = jnp.where(jidx <= cur, jnp.where(forced, FORCE_BONUS, imp), NEG_INF)
        score_ref[...] = score
        sub = 8
        cnt = [jnp.zeros((sub, width), F32) for _ in range(nblk // sub)]
        tiles = [score[sub * v:sub * (v + 1)] for v in range(nblk // sub)]
        sidx = lax.broadcasted_iota(jnp.int32, (sub, width), 0)
        for jp in range(nblk):
            rowv = jnp.broadcast_to(score_ref[jp:jp + 1, :], (sub, width))
            for v in range(nblk // sub):
                if sub * v > jp:
                    cnt[v] = jnp.where(rowv >= tiles[v], cnt[v] + 1.0, cnt[v])
                elif sub * v + sub - 1 <= jp:
                    cnt[v] = jnp.where(rowv > tiles[v], cnt[v] + 1.0, cnt[v])
                else:
                    ge = jnp.where(rowv >= tiles[v], cnt[v] + 1.0, cnt[v])
                    gt = jnp.where(rowv > tiles[v], cnt[v] + 1.0, cnt[v])
                    cnt[v] = jnp.where(sidx + sub * v > jp, ge, gt)
        rank = jnp.concatenate(cnt, axis=0)
        bias = jnp.where(rank < float(TOPN), 0.0, NEG_INF)
        for g in range(NSA_GROUPS):
            bias_ref[g] = bias[:, NSA_TQ * g:NSA_TQ * (g + 1)]

        def write_bias(j, carry):
            r0 = pl.multiple_of(j * SLC_LEN, SLC_LEN)
            for g in range(NSA_GROUPS):
                blk = jnp.broadcast_to(bias_ref[g, pl.ds(j, 1), :], (SLC_LEN, NSA_TQ))
                kaug_ref[g, pl.ds(r0, SLC_LEN), BIAS_LANE0:] = blk.astype(BF16)
            return carry
        lax.fori_loop(0, (c + 1) * (NSA_TQ // SLC_LEN), write_bias, 0)

    _flash_reset(ms_ref, accs_ref)
    qas = [jnp.concatenate([q, eye_ref[...]], axis=1) for q in qs]
    last = c // (SLC_CHUNK // NSA_TQ)

    def slc_scores(kc, dst):
        k0 = pl.multiple_of(kc * SLC_CHUNK, SLC_CHUNK)
        for g in range(NSA_GROUPS):
            dst[g] = _dot_nt(qas[g], kaug_ref[g, pl.ds(k0, SLC_CHUNK), :])

    def slc_update(src, p_ref, kc, causal):
        k0 = pl.multiple_of(kc * SLC_CHUNK, SLC_CHUNK)
        mask = None
        if causal:
            mask = (k0 + lax.broadcasted_iota(jnp.int32, (1, SLC_CHUNK), 1)) <= t_row
        for g in range(NSA_GROUPS):
            _flash_update(src.at[g], kvn_ref[0, 2 + g, pl.ds(k0, SLC_CHUNK), :],
                          ms_ref.at[g], accs_ref.at[g], p_ref.at[g], mask)

    slc_scores(0, sa_ref)

    def slc_pair(i, carry):
        slc_scores(2 * i + 1, sb_ref)
        slc_update(sa_ref, pa_ref, 2 * i, False)
        slc_scores(2 * i + 2, sa_ref)
        slc_update(sb_ref, pb_ref, 2 * i + 1, False)
        return carry
    lax.fori_loop(0, last // 2, slc_pair, 0)
    tail = 2 * (last // 2)

    @pl.when(last > tail)
    def _():
        slc_scores(tail + 1, sb_ref)
        slc_update(sa_ref, pa_ref, tail, False)
        slc_update(sb_ref, pb_ref, tail + 1, True)

    @pl.when(last == tail)
    def _():
        slc_update(sa_ref, pa_ref, tail, True)

    sig = 1.0 / (1.0 + jnp.exp(-misc_ref[...]))
    outs = []
    for g in range(NSA_GROUPS):
        for r in range(NSA_REP):
            lane0 = GATE_LANE0 + 3 * (NSA_REP * g + r)
            rs = slice(NSA_TQ * r, NSA_TQ * (r + 1))
            o = (sig[:, lane0:lane0 + 1] * oc_ref[g, rs, :]
                 + sig[:, lane0 + 1:lane0 + 2] * _flash_finish(accs_ref[g, rs, :])
                 + sig[:, lane0 + 2:lane0 + 3] * _flash_finish(accw_ref[g, rs, :]))
            outs.append(o[:, :NSA_DH])
    o_ref[0] = jnp.concatenate(outs, axis=1).astype(BF16)


def _nsa(qn, kvn, kvcmp, misc, cov_t, eye, B, S):
    nblk = S // SLC_LEN
    ncmp = S // CMP_STRIDE
    nq = S // NSA_TQ
    rows = NSA_ROWS
    return pl.pallas_call(
        _nsa_body,
        grid=(B, nq),
        in_specs=[
            pl.BlockSpec((1, NSA_HEADS, NSA_TQ, LANES), lambda b, c: (b, 0, c, 0)),
            pl.BlockSpec((1, 8, S, LANES), lambda b, c: (b, 0, 0, 0)),
            pl.BlockSpec((4, 1, ncmp, LANES), lambda b, c: (0, b, 0, 0)),
            pl.BlockSpec((NSA_TQ, LANES), lambda b, c: (b * nq + c, 0)),
            pl.BlockSpec(cov_t.shape, lambda b, c: (0, 0)),
            pl.BlockSpec(eye.shape, lambda b, c: (0, 0)),
        ],
        out_specs=pl.BlockSpec((1, NSA_TQ, NSA_HEADS * NSA_DH), lambda b, c: (b, c, 0)),
        out_shape=jax.ShapeDtypeStruct((B, S, NSA_HEADS * NSA_DH), BF16),
        scratch_shapes=[
            pltpu.VMEM((NSA_GROUPS, S, BIAS_LANE0 + NSA_TQ), BF16),
            pltpu.VMEM((nblk, NSA_GROUPS * NSA_TQ), F32),
            pltpu.VMEM((NSA_GROUPS, nblk, NSA_TQ), F32),
            pltpu.VMEM((NSA_GROUPS, rows, SLC_CHUNK), F32),
            pltpu.VMEM((NSA_GROUPS, rows, SLC_CHUNK), F32),
            pltpu.VMEM((NSA_GROUPS, rows, SLC_CHUNK), BF16),
            pltpu.VMEM((NSA_GROUPS, rows, SLC_CHUNK), BF16),
            pltpu.VMEM((NSA_GROUPS, rows, LANES), F32),
            pltpu.VMEM((NSA_GROUPS, rows, LANES), F32),
            pltpu.VMEM((NSA_GROUPS, rows, WIN_SPAN), F32),
            pltpu.VMEM((NSA_GROUPS, rows, WIN_SPAN), BF16),
            pltpu.VMEM((NSA_GROUPS, rows, LANES), F32),
            pltpu.VMEM((NSA_GROUPS, rows, ncmp), F32),
            pltpu.VMEM((NSA_GROUPS, rows, ncmp), F32),
            pltpu.VMEM((NSA_GROUPS, rows, ncmp), BF16),
            pltpu.VMEM((NSA_GROUPS, rows, LANES), F32),
        ],
        compiler_params=_params("arbitrary", "arbitrary"),
        name="nsa",
    )(qn, kvn, kvcmp, misc, cov_t, eye)


MLA_TQ = 512
MLA_CHUNK = 512
MLA_HPB = 4


def _mla_body(q_ref, k_ref, v_ref, o_ref, sa_ref, sb_ref, pa_ref, pb_ref, m_ref, acc_ref):
    qi = pl.program_id(2)
    t_row = qi * MLA_TQ + lax.broadcasted_iota(jnp.int32, (MLA_TQ, 1), 0)
    _flash_reset(m_ref, acc_ref)

    def scores(kc, dst):
        k0 = pl.multiple_of(kc * MLA_CHUNK, MLA_CHUNK)
        for j in range(MLA_HPB):
            dst[j] = _dot_nt(q_ref[0, j], k_ref[0, j, pl.ds(k0, MLA_CHUNK), :])

    def update(src, p_ref, kc, causal):
        k0 = pl.multiple_of(kc * MLA_CHUNK, MLA_CHUNK)
        mask = None
        if causal:
            mask = (k0 + lax.broadcasted_iota(jnp.int32, (1, MLA_CHUNK), 1)) <= t_row
        for j in range(MLA_HPB):
            _flash_update(src.at[j], v_ref[0, j, pl.ds(k0, MLA_CHUNK), :],
                          m_ref.at[j], acc_ref.at[j], p_ref.at[j], mask)

    scores(0, sa_ref)

    def pair(i, carry):
        scores(2 * i + 1, sb_ref)
        update(sa_ref, pa_ref, 2 * i, False)
        scores(2 * i + 2, sa_ref)
        update(sb_ref, pb_ref, 2 * i + 1, False)
        return carry
    lax.fori_loop(0, qi // 2, pair, 0)
    tail = 2 * (qi // 2)

    @pl.when(qi > tail)
    def _():
        scores(tail + 1, sb_ref)
        update(sa_ref, pa_ref, tail, False)
        update(sb_ref, pb_ref, tail + 1, True)

    @pl.when(qi == tail)
    def _():
        update(sa_ref, pa_ref, tail, True)
    o_ref[0] = jnp.concatenate([_flash_finish(acc_ref[j])[:, :MLA_V] for j in range(MLA_HPB)],
                               axis=1).astype(BF16)


def _mla(q, k, v, B, S):
    return pl.pallas_call(
        _mla_body,
        grid=(B, MLA_HEADS // MLA_HPB, S // MLA_TQ),
        in_specs=[
            pl.BlockSpec((1, MLA_HPB, MLA_TQ, LANES), lambda b, h, i: (b, h, i, 0)),
            pl.BlockSpec((1, MLA_HPB, S, LANES), lambda b, h, i: (b, h, 0, 0)),
            pl.BlockSpec((1, MLA_HPB, S, LANES), lambda b, h, i: (b, h, 0, 0)),
        ],
        out_specs=pl.BlockSpec((1, MLA_TQ, MLA_HPB * MLA_V), lambda b, h, i: (b, i, h)),
        out_shape=jax.ShapeDtypeStruct((B, S, MLA_HEADS * MLA_V), BF16),
        scratch_shapes=[
            pltpu.VMEM((MLA_HPB, MLA_TQ, MLA_CHUNK), F32),
            pltpu.VMEM((MLA_HPB, MLA_TQ, MLA_CHUNK), F32),
            pltpu.VMEM((MLA_HPB, MLA_TQ, MLA_CHUNK), BF16),
            pltpu.VMEM((MLA_HPB, MLA_TQ, MLA_CHUNK), BF16),
            pltpu.VMEM((MLA_HPB, MLA_TQ, LANES), F32),
            pltpu.VMEM((MLA_HPB, MLA_TQ, LANES), F32),
        ],
        compiler_params=_params("arbitrary", "arbitrary", "arbitrary"),
        name="mla",
    )(q, k, v)


def _mem_kv_body(m_ref, wk_ref, wv_ref, k_ref, v_ref):
    mb = m_ref[...].astype(BF16)
    k_ref[...] = _dot(mb, wk_ref[...]).astype(BF16)
    v_ref[...] = _dot(mb, wv_ref[...]).astype(BF16)


def _mem_kv(memf, wk, wv, tm):
    R, D = memf.shape
    tok = lambda i: (i, 0)
    const = lambda i: (0, 0)
    return pl.pallas_call(
        _mem_kv_body,
        grid=(R // tm,),
        in_specs=[pl.BlockSpec((tm, D), tok), pl.BlockSpec(wk.shape, const),
                  pl.BlockSpec(wv.shape, const)],
        out_specs=[pl.BlockSpec((tm, D), tok), pl.BlockSpec((tm, D), tok)],
        out_shape=[jax.ShapeDtypeStruct((R, D), BF16), jax.ShapeDtypeStruct((R, D), BF16)],
        compiler_params=_params("arbitrary"),
        name="mem_kv",
    )(memf, wk, wv)


def _mem_attn_body(x0_ref, on_ref, om_ref, wmix_ref, g1_ref, b1_ref,
                   k_ref, v_ref, wq_ref, wo_ref, g_ref, b_ref, o_ref, *, alpha):
    half = on_ref.shape[1]
    mix = _dot(on_ref[...], wmix_ref[:half, :]) + _dot(om_ref[...], wmix_ref[half:, :])
    x = _layer_norm(alpha * x0_ref[...] + mix, g1_ref[...], b1_ref[...])
    D = x.shape[1]
    dh = D // MEM_HEADS
    q = (_dot(x.astype(BF16), wq_ref[...]) * (dh ** -0.5 * LOG2E)).astype(BF16)
    outs = []
    for h in range(MEM_HEADS):
        cs = slice(dh * h, dh * (h + 1))
        s = _dot_nt(q[:, cs], k_ref[0, :, cs])
        p = jnp.exp2(s - jnp.max(s, axis=1, keepdims=True))
        l = jnp.sum(p, axis=1, keepdims=True)
        outs.append((_dot(p.astype(BF16), v_ref[0, :, cs]) * (1.0 / l)).astype(BF16))
    o = jnp.concatenate(outs, axis=1)
    y = _dot(o, wo_ref[...])
    o_ref[...] = _layer_norm(alpha * x + y, g_ref[...], b_ref[...])


def _mem_attn(xf, o_nsa, o_mla, w_o, g1, b1, k_mem, v_mem, wq, wo, g, b, alpha, S, tm):
    T, D = xf.shape
    nst = S // tm
    M = k_mem.shape[1]
    tok = lambda i: (i, 0)
    const = lambda i: (0, 0)
    memb = lambda i: (i // nst, 0, 0)
    return pl.pallas_call(
        functools.partial(_mem_attn_body, alpha=alpha),
        grid=(T // tm,),
        in_specs=[
            pl.BlockSpec((tm, D), tok),
            pl.BlockSpec((tm, o_nsa.shape[1]), tok),
            pl.BlockSpec((tm, o_mla.shape[1]), tok),
            pl.BlockSpec(w_o.shape, const),
            pl.BlockSpec((1, D), const),
            pl.BlockSpec((1, D), const),
            pl.BlockSpec((1, M, D), memb),
            pl.BlockSpec((1, M, D), memb),
            pl.BlockSpec(wq.shape, const),
            pl.BlockSpec(wo.shape, const),
            pl.BlockSpec((1, D), const),
            pl.BlockSpec((1, D), const),
        ],
        out_specs=pl.BlockSpec((tm, D), tok),
        out_shape=jax.ShapeDtypeStruct((T, D), F32),
        compiler_params=_params("arbitrary"),
        name="mem_attn",
    )(xf, o_nsa, o_mla, w_o, g1, b1, k_mem, v_mem, wq, wo, g, b)


HALO = 8


FFN_SLAB = 256
FFN_TM = 512


def _ffn_body(x_ref, xh_ref, wg_ref, wu_ref, cw_ref, cb_ref, wd_ref, g_ref, b_ref, o_ref,
              act_ref, *, alpha, seq_tiles):
    i = pl.program_id(0)
    x = x_ref[...]
    xb = x.astype(BF16)
    xhb = xh_ref[...].astype(BF16)
    tm = x.shape[0]
    row = lax.broadcasted_iota(jnp.int32, (tm, 1), 0)
    seq_start = i % seq_tiles == 0
    for c0 in range(0, wg_ref.shape[1], FFN_SLAB):
        cs = slice(c0, c0 + FFN_SLAB)
        gate = _dot(xb, wg_ref[:, cs])
        up = _dot(xb, wu_ref[:, cs])
        halo = jnp.where(seq_start, 0.0, _dot(xhb, wg_ref[:, cs]))
        g1 = jnp.where(row == 0, halo[HALO - 1:HALO], pltpu.roll(gate, 1, 0))
        g2 = jnp.where(row == 0, halo[HALO - 2:HALO - 1],
                       jnp.where(row == 1, halo[HALO - 1:HALO], pltpu.roll(gate, 2, 0)))
        conv = cw_ref[0:1, cs] * g2 + cw_ref[1:2, cs] * g1 + cw_ref[2:3, cs] * gate + cb_ref[:, cs]
        act_ref[:, cs] = (conv * (1.0 / (1.0 + jnp.exp(-conv))) * up).astype(BF16)
    y = _dot(act_ref[...], wd_ref[...])
    o_ref[...] = _layer_norm(alpha * x + y, g_ref[...], b_ref[...])


def _ffn(xf, wg, wu, cw, cb, wd, g, b, alpha, S, tm):
    T, D = xf.shape
    dff = wg.shape[1]
    assert dff % FFN_SLAB == 0
    tok = lambda i: (i, 0)
    const = lambda i: (0, 0)
    return pl.pallas_call(
        functools.partial(_ffn_body, alpha=alpha, seq_tiles=S // tm),
        grid=(T // tm,),
        in_specs=[
            pl.BlockSpec((tm, D), tok),
            pl.BlockSpec((HALO, D), lambda i: (jnp.maximum(i * (tm // HALO) - 1, 0), 0)),
            pl.BlockSpec((D, dff), const),
            pl.BlockSpec((D, dff), const),
            pl.BlockSpec((CONV_WIDTH, dff), const),
            pl.BlockSpec((1, dff), const),
            pl.BlockSpec((dff, D), const),
            pl.BlockSpec((1, D), const),
            pl.BlockSpec((1, D), const),
        ],
        out_specs=pl.BlockSpec((tm, D), tok),
        out_shape=jax.ShapeDtypeStruct((T, D), F32),
        scratch_shapes=[pltpu.VMEM((tm, dff), BF16)],
        compiler_params=_params("arbitrary"),
        name="ffn",
    )(xf, xf, wg, wu, cw, cb, wd, g, b)


def _inv_freq_row(dim, lane_lo, lane_hi, period):
    inv = ROPE_THETA ** (-np.arange(0, dim, 2, dtype=np.float64) / dim)
    row = np.zeros((1, LANES), np.float32)
    for lane in range(lane_lo, lane_hi):
        row[0, lane] = inv[(lane % period) % (dim // 2)]
    return jnp.asarray(row)


def _cover_t(S):
    nc = S // CMP_STRIDE
    ns = S // SLC_LEN
    cs = np.arange(nc)[:, None] * CMP_STRIDE
    ss = np.arange(ns)[None, :] * SLC_LEN
    cover = np.clip(np.minimum(cs + CMP_LEN, ss + SLC_LEN) - np.maximum(cs, ss), 0, None) / CMP_LEN
    cover[nc - 1:] = 0.0
    return jnp.asarray(cover.T, dtype=BF16)


def _eye_aug():
    eye = np.zeros((NSA_ROWS, NSA_TQ), np.float32)
    r = np.arange(NSA_ROWS)
    eye[r, r % NSA_TQ] = 1.0
    return jnp.asarray(eye, dtype=BF16)


def _permute_w_in(w):
    D = w.shape[0]
    c1 = NSA_HEADS * NSA_DH
    c2 = c1 + 3 * 2 * NSA_GROUPS * NSA_DH
    c3 = c2 + 3 * NSA_HEADS
    c4 = c3 + MLA_Q_RANK
    c5 = c4 + MLA_KV_RANK
    c6 = c5 + MLA_ROPE
    half = MLA_ROPE // 2
    z = lambda n: jnp.zeros((D, n), w.dtype)
    misc = jnp.concatenate(
        [w[:, c5 + half:c6], z(GATE_LANE0 - half), w[:, c2:c3], z(MLA_PE1 - GATE_LANE0 - (c3 - c2)),
         w[:, c5:c5 + half], z(LANES - MLA_PE1 - half)], axis=1)
    return jnp.concatenate([w[:, :c2], w[:, c3:c5], misc], axis=1).astype(BF16)


def _mla_head_lanes(nope, pe):
    r, H, _ = nope.shape
    half = MLA_ROPE // 2
    split = MLA_PE1 - half
    pad = jnp.zeros((r, H, LANES - MLA_NOPE - MLA_ROPE), nope.dtype)
    return jnp.concatenate([pe[..., half:], nope[..., :split], pe[..., :half], nope[..., split:], pad],
                           axis=2).reshape(r, H * LANES)


def _permute_w_uq(w):
    r = w.shape[0]
    w3 = w.reshape(r, MLA_HEADS, MLA_NOPE + MLA_ROPE)
    return _mla_head_lanes(w3[..., :MLA_NOPE], w3[..., MLA_NOPE:]).astype(BF16)


def _permute_w_ukv(w):
    r = w.shape[0]
    w3 = w.reshape(r, MLA_HEADS, MLA_NOPE + MLA_V)
    k = _mla_head_lanes(w3[..., :MLA_NOPE], jnp.zeros((r, MLA_HEADS, MLA_ROPE), w.dtype))
    v = _pad_lanes(w3[..., MLA_NOPE:]).reshape(r, MLA_HEADS * LANES)
    return jnp.concatenate([k, v], axis=1).astype(BF16)


def _pad_lanes(a):
    return jnp.concatenate([a, jnp.zeros(a.shape[:-1] + (LANES - a.shape[-1],), a.dtype)], axis=-1)


def kernel(x, mem, positions, w_in, nsa_k_pos, nsa_ck_w1, nsa_ck_b1, nsa_ck_w2, nsa_ck_b2,
           nsa_v_pos, nsa_cv_w1, nsa_cv_b1, nsa_cv_w2, nsa_cv_b2,
           mla_q_norm, mla_w_uq, mla_kv_norm, mla_w_ukv, w_o, ln1_g, ln1_b,
           mem_wq, mem_wk, mem_wv, mem_wo, ln2_g, ln2_b,
           ffn_w_up, ffn_conv_w, ffn_conv_b, ffn_w_down, ln3_g, ln3_b):
    B, S, D = x.shape
    T = B * S
    depth = w_in.shape[0]
    alpha = (2.0 * depth) ** 0.25
    d_ff = ffn_w_down.shape[1]
    tm = min(512, S)
    assert S % MLA_TQ == 0 and S >= WIN_SPAN and S % tm == 0
    assert (B * mem.shape[1]) % 256 == 0

    pos = positions.reshape(T, 1)
    pos_cmp = positions[:, CMP_LEN - 1::CMP_STRIDE]
    pos_cmp = jnp.concatenate([pos_cmp, pos_cmp[:, -1:]], axis=1)[:, :, None]
    inv_cmp = _inv_freq_row(NSA_DH, 0, NSA_DH, NSA_DH)
    inv_tok = (_inv_freq_row(NSA_DH, 0, NSA_DH // 2, NSA_DH)
               + _inv_freq_row(MLA_ROPE, NSA_DH // 2, NSA_DH // 2 + MLA_ROPE // 2, MLA_ROPE // 2))
    cov_t = _cover_t(S)
    eye = _eye_aug()
    memf = mem.reshape(B * mem.shape[1], D)

    xf = x.reshape(T, D)
    for l in range(depth):
        qn, kvn, kvc, misc, q_m, k_m, v_m = _inproj(
            pos, inv_tok, xf, _permute_w_in(w_in[l]), mla_q_norm[l][None, :],
            mla_kv_norm[l][None, :], _permute_w_uq(mla_w_uq[l]), _permute_w_ukv(mla_w_ukv[l]),
            B, S, tm)
        kvcmp = _compress(
            pos_cmp, inv_cmp, kvc,
            *_compress_weights(nsa_k_pos[l], nsa_ck_w1[l], nsa_ck_b1[l], nsa_ck_w2[l], nsa_ck_b2[l],
                               nsa_v_pos[l], nsa_cv_w1[l], nsa_cv_b1[l], nsa_cv_w2[l], nsa_cv_b2[l]),
            B, S)
        o_nsa = _nsa(qn, kvn, kvcmp, misc, cov_t, eye, B, S)
        o_mla = _mla(q_m, k_m, v_m, B, S)
        k_mem, v_mem = _mem_kv(memf, mem_wk[l].astype(BF16), mem_wv[l].astype(BF16), 256)
        xf = _mem_attn(xf, o_nsa.reshape(T, -1), o_mla.reshape(T, -1), w_o[l].astype(BF16),
                       ln1_g[l][None, :], ln1_b[l][None, :],
                       k_mem.reshape(B, -1, D), v_mem.reshape(B, -1, D),
                       mem_wq[l].astype(BF16), mem_wo[l].astype(BF16),
                       ln2_g[l][None, :], ln2_b[l][None, :], alpha, S, tm)
        xf = _ffn(xf, ffn_w_up[l][:, :d_ff].astype(BF16), ffn_w_up[l][:, d_ff:].astype(BF16),
                  ffn_conv_w[l], ffn_conv_b[l][None, :], ffn_w_down[l].astype(BF16),
                  ln3_g[l][None, :], ln3_b[l][None, :], alpha, S, FFN_TM)
    return xf.reshape(B, S, D)
```

```python
import functools
import math

import numpy as np
import jax
import jax.numpy as jnp
from jax import lax
from jax.experimental import pallas as pl
from jax.experimental.pallas import tpu as pltpu

F32 = jnp.float32
BF16 = jnp.bfloat16

NSA_HEADS = 8
NSA_GROUPS = 2
NSA_REP = NSA_HEADS // NSA_GROUPS
NSA_DH = 64
CMP_STRIDE = 16
CMP_LEN = 32
SLC_LEN = 64
TOPN = 16
WINDOW = 512
CMP_HIDDEN = 128
FORCE_BONUS = 1e4
MLA_HEADS = 8
MLA_Q_RANK = 384
MLA_KV_RANK = 256
MLA_NOPE = 64
MLA_ROPE = 32
MLA_V = 64
MEM_HEADS = 4
CONV_WIDTH = 3
ROPE_THETA = 10000.0
LN_EPS = 1e-5
RMS_EPS = 1e-6
NEG_INF = -1e30
LOG2E = math.log2(math.e)

LANES = 128
VMEM_LIMIT = 56 * 1024 * 1024

C_Q = 0
C_KVC = 512
C_KVN = 768
C_LAT = 1280
C_MISC = 1920
IN_COLS_PAD = 2048
GATE_LANE0 = MLA_ROPE
MLA_PE2 = 0
MLA_PE1 = 64
ONES_LANE = 64


def _dot(a, b):
    return jnp.dot(a, b, preferred_element_type=F32)


def _dot_nt(a, b):
    return lax.dot_general(a, b, (((1,), (1,)), ((), ())), preferred_element_type=F32)


def _layer_norm(y, g, b):
    mu = jnp.mean(y, axis=-1, keepdims=True)
    d = y - mu
    var = jnp.mean(d * d, axis=-1, keepdims=True)
    return d * lax.rsqrt(var + LN_EPS) * g + b


def _params(*sem):
    return pltpu.CompilerParams(dimension_semantics=sem, vmem_limit_bytes=VMEM_LIMIT)


def _rope_tables(pos_col, inv_row, half):
    ang = pos_col * inv_row
    cos = jnp.cos(ang)
    sin = jnp.sin(ang)
    lane = lax.broadcasted_iota(jnp.int32, (1, LANES), 1)
    upper = (lane & (2 * half - 1)) >= half
    rot = inv_row != 0.0
    sin_hi = jnp.where(upper & rot, sin, 0.0)
    sin_lo = jnp.where(upper | (~rot), 0.0, -sin)
    return cos, sin_hi, sin_lo


def _apply_rope(v, tabs, half):
    cos, sin_hi, sin_lo = tabs
    return v * cos + pltpu.roll(v, half, 1) * sin_hi + pltpu.roll(v, LANES - half, 1) * sin_lo


def _rms_norm(v, g):
    return v * lax.rsqrt(jnp.mean(v * v, axis=-1, keepdims=True) + RMS_EPS) * g


def _inproj_body(pos_ref, inv_ref, x_ref, w_ref, gq_ref, gkv_ref, wq_ref, wkv_ref,
                 qn_ref, kvn_ref, kvc_ref, misc_ref, qm_ref, km_ref, vm_ref):
    xb = x_ref[...].astype(BF16)
    lane = lax.broadcasted_iota(jnp.int32, (1, LANES), 1)
    low = lane < NSA_DH
    qscale = NSA_DH ** -0.5 * LOG2E

    ang = pos_ref[...].astype(F32) * inv_ref[...]
    cos, sin = jnp.cos(ang), jnp.sin(ang)
    nf, mf = NSA_DH // 2, MLA_ROPE // 2

    def tile_nsa(t):
        t = jnp.where(lane < nf, t, 0.0)
        t = t + pltpu.roll(t, nf, 1)
        return t + pltpu.roll(t, 2 * nf, 1)

    def place_mla(t):
        t = jnp.where((lane >= nf) & (lane < nf + mf), t, 0.0)
        return pltpu.roll(t, MLA_PE1 - nf, 1) + pltpu.roll(t, LANES + MLA_PE2 - nf, 1)

    upper = (lane & (NSA_DH - 1)) >= nf
    sin_n = tile_nsa(sin)
    tabs = (tile_nsa(cos), jnp.where(upper, sin_n, 0.0), jnp.where(upper, 0.0, -sin_n))
    pe1 = (lane >= MLA_PE1) & (lane < MLA_PE1 + mf)
    pe2 = (lane >= MLA_PE2) & (lane < MLA_PE2 + mf)
    pe_lanes = pe1 | pe2
    sin_m = place_mla(sin)
    cos_m = jnp.where(pe_lanes, place_mla(cos), 1.0)
    sin_m = jnp.where(pe1, -sin_m, jnp.where(pe2, sin_m, 0.0))

    def rope_mla(v):
        return v * cos_m + pltpu.roll(v, LANES // 2, 1) * sin_m

    def proj(c0, n):
        return _dot(xb, w_ref[:, c0:c0 + n])

    def split_store(v, ref, idx_lo, idx_hi, pad=0.0):
        ref[0, idx_lo] = jnp.where(low, v, pad).astype(BF16)
        ref[0, idx_hi] = jnp.where(low, pltpu.roll(v, NSA_DH, 1), pad).astype(BF16)

    ones_pad = jnp.where(lane == ONES_LANE, 1.0, 0.0)

    for slab in range(2):
        h = proj(C_Q + 256 * slab, 256)
        for j in range(2):
            r = _apply_rope(h[:, LANES * j:LANES * (j + 1)], tabs, NSA_DH // 2) * qscale
            split_store(r, qn_ref, 4 * slab + 2 * j, 4 * slab + 2 * j + 1)

    h = proj(C_KVC, 256)
    kvc_ref[0] = h[:, :LANES]
    kvc_ref[1] = h[:, LANES:]

    for slab in range(2):
        h = proj(C_KVN + 256 * slab, 256)
        k = _apply_rope(h[:, :LANES], tabs, NSA_DH // 2)
        split_store(k, kvn_ref, 4 * slab, 4 * slab + 1)
        split_store(h[:, LANES:], kvn_ref, 4 * slab + 2, 4 * slab + 3, ones_pad)

    lat = [proj(C_LAT + 256 * i, 256) for i in range((IN_COLS_PAD - C_LAT) // 256)]
    misc = lat[2][:, LANES:]
    misc_ref[...] = misc
    mq = jnp.concatenate([lat[0], lat[1][:, :LANES]], axis=1)
    mkv = jnp.concatenate([lat[1][:, LANES:], lat[2][:, :LANES]], axis=1)
    mscale = (MLA_NOPE + MLA_ROPE) ** -0.5 * LOG2E
    qn = _rms_norm(mq, gq_ref[...]).astype(BF16)
    kvn = _rms_norm(mkv, gkv_ref[...]).astype(BF16)
    kpe = jnp.where(pe_lanes, rope_mla(misc), 0.0)
    kcols = MLA_HEADS * LANES
    for slab in range(MLA_HEADS // 2):
        cs = slice(256 * slab, 256 * (slab + 1))
        hq = _dot(qn, wq_ref[:, cs])
        hk = _dot(kvn, wkv_ref[:, cs])
        hv = _dot(kvn, wkv_ref[:, kcols + 256 * slab:kcols + 256 * (slab + 1)])
        for j in range(2):
            ls = slice(LANES * j, LANES * (j + 1))
            qm_ref[0, 2 * slab + j] = (rope_mla(hq[:, ls]) * mscale).astype(BF16)
            km_ref[0, 2 * slab + j] = (hk[:, ls] + kpe).astype(BF16)
            vm_ref[0, 2 * slab + j] = (hv[:, ls] + ones_pad).astype(BF16)


def _inproj(pos, inv_row, xf, w_in_p, gq, gkv, wq_p, wkv_p, B, S, tm):
    T = B * S
    nst = S // tm
    tok = lambda i: (i, 0)
    const = lambda i: (0, 0)
    head_blk = lambda i: (i // nst, 0, i % nst, 0)
    heads = jax.ShapeDtypeStruct((B, 8, S, LANES), BF16)
    return pl.pallas_call(
        _inproj_body,
        grid=(T // tm,),
        in_specs=[
            pl.BlockSpec((tm, 1), tok),
            pl.BlockSpec((1, LANES), const),
            pl.BlockSpec((tm, xf.shape[1]), tok),
            pl.BlockSpec(w_in_p.shape, const),
            pl.BlockSpec(gq.shape, const),
            pl.BlockSpec(gkv.shape, const),
            pl.BlockSpec(wq_p.shape, const),
            pl.BlockSpec(wkv_p.shape, const),
        ],
        out_specs=[
            pl.BlockSpec((1, 8, tm, LANES), head_blk),
            pl.BlockSpec((1, 8, tm, LANES), head_blk),
            pl.BlockSpec((2, tm, LANES), lambda i: (0, i, 0)),
            pl.BlockSpec((tm, LANES), tok),
            pl.BlockSpec((1, 8, tm, LANES), head_blk),
            pl.BlockSpec((1, 8, tm, LANES), head_blk),
            pl.BlockSpec((1, 8, tm, LANES), head_blk),
        ],
        out_shape=[
            heads,
            heads,
            jax.ShapeDtypeStruct((2, T, LANES), F32),
            jax.ShapeDtypeStruct((T, LANES), F32),
            heads,
            heads,
            heads,
        ],
        compiler_params=_params("arbitrary"),
        name="inproj",
    )(pos, inv_row, xf, w_in_p, gq, gkv, wq_p, wkv_p)


def _compress_body(pos_ref, inv_ref, x_ref, pe_ref, w1_ref, b1_ref, w2_ref, b2_ref, o_ref):
    is_k = pl.program_id(0) == 0
    nch = o_ref.shape[2]
    a1 = jnp.zeros((nch, NSA_GROUPS * CMP_HIDDEN), F32)
    a2 = jnp.zeros((nch, NSA_GROUPS * CMP_HIDDEN), F32)
    for l in range(CMP_STRIDE):
        xl = x_ref.at[0, 0][pl.ds(l, nch, stride=CMP_STRIDE), :]
        a1 = a1 + _dot((xl + pe_ref[0, l:l + 1, :]).astype(BF16), w1_ref[0, l])
        a2 = a2 + _dot((xl + pe_ref[0, CMP_STRIDE + l:CMP_STRIDE + l + 1, :]).astype(BF16),
                       w1_ref[0, CMP_STRIDE + l])
    pre = a1 + pltpu.roll(a2, nch - 1, 0) + b1_ref[0]
    hid = jax.nn.gelu(pre, approximate=True)
    out = _dot(hid.astype(BF16), w2_ref[0]) + b2_ref[0]
    tabs = _rope_tables(pos_ref[0].astype(F32), inv_ref[...], NSA_DH // 2)
    row = lax.broadcasted_iota(jnp.int32, (nch, 1), 0)
    for g in range(NSA_GROUPS):
        og = out[:, LANES * g:LANES * (g + 1)]
        og = jnp.where(is_k, _apply_rope(og, tabs, NSA_DH // 2), og)
        o_ref[g, 0] = jnp.where(row < nch - 1, og, 0.0).astype(BF16)


def _compress(pos_cmp, inv_nsa, kvc, pe, w1, b1, w2, b2, B, S):
    nch = S // CMP_STRIDE
    x = kvc.reshape(2, B, S, LANES)
    kv = lambda j, b: (j, 0, 0)
    kv4 = lambda j, b: (j, 0, 0, 0)
    return pl.pallas_call(
        _compress_body,
        grid=(2, B),
        in_specs=[
            pl.BlockSpec((1, nch, 1), lambda j, b: (b, 0, 0)),
            pl.BlockSpec((1, LANES), lambda j, b: (0, 0)),
            pl.BlockSpec((1, 1, S, LANES), lambda j, b: (j, b, 0, 0)),
            pl.BlockSpec((1,) + pe.shape[1:], kv),
            pl.BlockSpec((1,) + w1.shape[1:], kv4),
            pl.BlockSpec((1,) + b1.shape[1:], kv),
            pl.BlockSpec((1,) + w2.shape[1:], kv),
            pl.BlockSpec((1,) + b2.shape[1:], kv),
        ],
        out_specs=pl.BlockSpec((NSA_GROUPS, 1, nch, LANES), lambda j, b: (j, b, 0, 0)),
        out_shape=jax.ShapeDtypeStruct((2 * NSA_GROUPS, B, nch, LANES), BF16),
        compiler_params=_params("arbitrary", "arbitrary"),
        name="compress",
    )(pos_cmp, inv_nsa, x, pe, w1, b1, w2, b2)


def _compress_weights(k_pos, k_w1, k_b1, k_w2, k_b2, v_pos, v_w1, v_b1, v_w2, v_b2):
    def one(pos, w1, b1, w2, b2):
        w1l = w1.reshape(CMP_LEN, NSA_DH, CMP_HIDDEN)
        z1 = jnp.zeros_like(w1l)
        w1bd = jnp.concatenate([jnp.concatenate([w1l, z1], axis=2),
                                jnp.concatenate([z1, w1l], axis=2)], axis=1)
        w2p = _pad_lanes(w2)
        z2 = jnp.zeros_like(w2p)
        w2bd = jnp.concatenate([jnp.concatenate([w2p, z2], axis=1),
                                jnp.concatenate([z2, w2p], axis=1)], axis=0)
        return (jnp.tile(pos, (1, NSA_GROUPS)), w1bd.astype(BF16), jnp.tile(b1, NSA_GROUPS)[None, :],
                w2bd.astype(BF16), jnp.tile(_pad_lanes(b2), NSA_GROUPS)[None, :])
    k = one(k_pos, k_w1, k_b1, k_w2, k_b2)
    v = one(v_pos, v_w1, v_b1, v_w2, v_b2)
    return tuple(jnp.stack([a, b]) for a, b in zip(k, v))


STRIP = 64


def _lane_tile(col, n):
    reps = [col] * (n // LANES)
    if n % LANES:
        reps.append(col[:, :n % LANES])
    return reps[0] if len(reps) == 1 else jnp.concatenate(reps, axis=1)


def _flash_reset(m_ref, acc_ref):
    m_ref[...] = jnp.full(m_ref.shape, NEG_INF, F32)
    acc_ref[...] = jnp.zeros(acc_ref.shape, F32)


def _flash_update(s_ref, v, m_ref, acc_ref, p_ref, mask=None, between=None):
    rows, n = s_ref.shape
    for r in range(rows // STRIP):
        rs = slice(STRIP * r, STRIP * (r + 1))
        s = s_ref[rs, :]
        if mask is not None:
            s = jnp.where(mask[rs], s, NEG_INF)
        if m_ref is None:
            p_ref[rs, :] = jnp.exp2(s - jnp.max(s, axis=1, keepdims=True)).astype(BF16)
            if between is not None:
                between()
            continue
        m_old = m_ref[rs, :]
        m_new = jnp.maximum(m_old, jnp.max(s, axis=1, keepdims=True))
        p_ref[rs, :] = jnp.exp2(s - _lane_tile(m_new, n)).astype(BF16)
        acc_ref[rs, :] = jnp.exp2(m_old - m_new) * acc_ref[rs, :]
        m_ref[rs, :] = m_new
    if m_ref is None:
        acc_ref[...] = _dot(p_ref[...], v)
    else:
        acc_ref[...] += _dot(p_ref[...], v)


def _flash_finish(acc):
    return acc * (1.0 / acc[:, ONES_LANE:ONES_LANE + 1])


NSA_TQ = 128
NSA_ROWS = NSA_REP * NSA_TQ
SLC_CHUNK = 512
WIN_SPAN = WINDOW + NSA_TQ
BIAS_LANE0 = LANES


def _nsa_body(q_ref, kvn_ref, kvc_ref, misc_ref, cov_ref, eye_ref, o_ref,
              kaug_ref, score_ref, bias_ref, sa_ref, sb_ref, pa_ref, pb_ref, ms_ref, accs_ref,
              sw_ref, pw_ref, accw_ref, sc_ref, pn_ref, pc_ref, oc_ref):
    c = pl.program_id(1)
    rows = NSA_ROWS
    t_row = c * NSA_TQ + (lax.broadcasted_iota(jnp.int32, (rows, 1), 0) & (NSA_TQ - 1))
    ncmp = kvc_ref.shape[2]
    nblk = kaug_ref.shape[1] // SLC_LEN

    @pl.when(c == 0)
    def _():
        for g in range(NSA_GROUPS):
            kaug_ref[g, :, :BIAS_LANE0] = kvn_ref[0, g]
            kaug_ref[g, :, BIAS_LANE0:] = jnp.zeros((kaug_ref.shape[1], NSA_TQ), BF16)

    qs = [q_ref[0, NSA_REP * g:NSA_REP * (g + 1)].reshape(rows, LANES) for g in range(NSA_GROUPS)]

    cmp_valid = (CMP_STRIDE * lax.broadcasted_iota(jnp.int32, (1, ncmp), 1) + CMP_LEN - 1) <= t_row
    for g in range(NSA_GROUPS):
        sc_ref[g] = _dot_nt(qs[g], kvc_ref[g, 0])
    for g in range(NSA_GROUPS):
        for r in range(rows // STRIP):
            rs = slice(STRIP * r, STRIP * (r + 1))
            s = jnp.where(cmp_valid[rs], sc_ref[g, rs, :], NEG_INF)
            p = jnp.where(cmp_valid[rs], jnp.exp2(s - jnp.max(s, axis=1, keepdims=True)), 0.0)
            l = jnp.sum(p, axis=1, keepdims=True)
            p = p * jnp.where(l > 0.0, 1.0 / l, 0.0)
            pn_ref[g, rs, :] = p
            pc_ref[g, rs, :] = p.astype(BF16)
        oc_ref[g] = _dot(pc_ref[g], kvc_ref[NSA_GROUPS + g, 0])

    win_start = pl.multiple_of(jnp.maximum(c * NSA_TQ - WINDOW, 0), NSA_TQ)
    diff = t_row - (win_start + lax.broadcasted_iota(jnp.int32, (1, WIN_SPAN), 1))
    win_valid = (diff >= 0) & (diff < WINDOW)

    def window(between=None):
        for g in range(NSA_GROUPS):
            sw_ref[g] = _dot_nt(qs[g], kvn_ref[0, 4 + g, pl.ds(win_start, WIN_SPAN), :])
        for g in range(NSA_GROUPS):
            _flash_update(sw_ref.at[g], kvn_ref[0, 6 + g, pl.ds(win_start, WIN_SPAN), :],
                          None, accw_ref.at[g], pw_ref.at[g], win_valid, between)

    needs_selection = (c + 1) * (NSA_TQ // SLC_LEN) > TOPN

    @pl.when(jnp.logical_not(needs_selection))
    def _():
        window()

    @pl.when(needs_selection)
    def _():
        width = NSA_GROUPS * NSA_TQ
        ps = jnp.concatenate(
            [sum(pn_ref[g, NSA_TQ * r:NSA_TQ * (r + 1), :] for r in range(NSA_REP))
             for g in range(NSA_GROUPS)], axis=0)
        hi = ps.astype(BF16)
        lo = (ps - hi.astype(F32)).astype(BF16)
        imp = _dot_nt(cov_ref[...], hi) + _dot_nt(cov_ref[...], lo)
        jidx = lax.broadcasted_iota(jnp.int32, (nblk, width), 0)
        lane_q = lax.broadcasted_iota(jnp.int32, (1, width), 1) & (NSA_TQ - 1)
        cur = c * (NSA_TQ // SLC_LEN) + lane_q // SLC_LEN
        forced = (jidx == 0) | (jidx == cur) | (jidx == cur - 1)
        score = jnp.where(jidx <= cur, jnp.where(forced, FORCE_BONUS, imp), NEG_INF)
        score_ref[...] = score
        sub = 8
        cnt = [jnp.zeros((sub, width), F32) for _ in range(nblk // sub)]
        tiles = [score[sub * v:sub * (v + 1)] for v in range(nblk // sub)]
        sidx = lax.broadcasted_iota(jnp.int32, (sub, width), 0)
        todo = list(range(nblk))

        def count(n):
            for jp in todo[:n]:
                rowv = jnp.broadcast_to(score_ref[jp:jp + 1, :], (sub, width))
                for v in range(nblk // sub):
                    if sub * v > jp:
                        cnt[v] = jnp.where(rowv >= tiles[v], cnt[v] + 1.0, cnt[v])
                    elif sub * v + sub - 1 <= jp:
                        cnt[v] = jnp.where(rowv > tiles[v], cnt[v] + 1.0, cnt[v])
                    else:
                        ge = jnp.where(rowv >= tiles[v], cnt[v] + 1.0, cnt[v])
                        gt = jnp.where(rowv > tiles[v], cnt[v] + 1.0, cnt[v])
                        cnt[v] = jnp.where(sidx + sub * v > jp, ge, gt)
            del todo[:n]

        strips = NSA_GROUPS * rows // STRIP
        window(between=lambda: count(-(-nblk // strips)))
        count(len(todo))
        rank = jnp.concatenate(cnt, axis=0)
        bias = jnp.where(rank < float(TOPN), 0.0, NEG_INF)
        for g in range(NSA_GROUPS):
            bias_ref[g] = bias[:, NSA_TQ * g:NSA_TQ * (g + 1)]

        def write_bias(j, carry):
            r0 = pl.multiple_of(j * SLC_LEN, SLC_LEN)
            for g in range(NSA_GROUPS):
                blk = jnp.broadcast_to(bias_ref[g, pl.ds(j, 1), :], (SLC_LEN, NSA_TQ))
                kaug_ref[g, pl.ds(r0, SLC_LEN), BIAS_LANE0:] = blk.astype(BF16)
            return carry
        lax.fori_loop(0, (c + 1) * (NSA_TQ // SLC_LEN), write_bias, 0)

    _flash_reset(ms_ref, accs_ref)
    qas = [jnp.concatenate([q, eye_ref[...]], axis=1) for q in qs]
    last = c // (SLC_CHUNK // NSA_TQ)

    def slc_scores(kc, dst):
        k0 = pl.multiple_of(kc * SLC_CHUNK, SLC_CHUNK)
        for g in range(NSA_GROUPS):
            dst[g] = _dot_nt(qas[g], kaug_ref[g, pl.ds(k0, SLC_CHUNK), :])

    def slc_update(src, p_ref, kc, causal):
        k0 = pl.multiple_of(kc * SLC_CHUNK, SLC_CHUNK)
        mask = None
        if causal:
            mask = (k0 + lax.broadcasted_iota(jnp.int32, (1, SLC_CHUNK), 1)) <= t_row
        for g in range(NSA_GROUPS):
            _flash_update(src.at[g], kvn_ref[0, 2 + g, pl.ds(k0, SLC_CHUNK), :],
                          ms_ref.at[g], accs_ref.at[g], p_ref.at[g], mask)

    slc_scores(0, sa_ref)

    def slc_pair(i, carry):
        slc_scores(2 * i + 1, sb_ref)
        slc_update(sa_ref, pa_ref, 2 * i, False)
        slc_scores(2 * i + 2, sa_ref)
        slc_update(sb_ref, pb_ref, 2 * i + 1, False)
        return carry
    lax.fori_loop(0, last // 2, slc_pair, 0)
    tail = 2 * (last // 2)

    @pl.when(last > tail)
    def _():
        slc_scores(tail + 1, sb_ref)
        slc_update(sa_ref, pa_ref, tail, False)
        slc_update(sb_ref, pb_ref, tail + 1, True)

    @pl.when(last == tail)
    def _():
        slc_update(sa_ref, pa_ref, tail, True)

    sig = 1.0 / (1.0 + jnp.exp(-misc_ref[...]))
    outs = []
    for g in range(NSA_GROUPS):
        for r in range(NSA_REP):
            lane0 = GATE_LANE0 + 3 * (NSA_REP * g + r)
            rs = slice(NSA_TQ * r, NSA_TQ * (r + 1))
            o = (sig[:, lane0:lane0 + 1] * oc_ref[g, rs, :]
                 + sig[:, lane0 + 1:lane0 + 2] * _flash_finish(accs_ref[g, rs, :])
                 + sig[:, lane0 + 2:lane0 + 3] * _flash_finish(accw_ref[g, rs, :]))
            outs.append(o[:, :NSA_DH])
    o_ref[0] = jnp.concatenate(outs, axis=1).astype(BF16)


def _nsa(qn, kvn, kvcmp, misc, cov_t, eye, B, S):
    nblk = S // SLC_LEN
    ncmp = S // CMP_STRIDE
    nq = S // NSA_TQ
    rows = NSA_ROWS
    return pl.pallas_call(
        _nsa_body,
        grid=(B, nq),
        in_specs=[
            pl.BlockSpec((1, NSA_HEADS, NSA_TQ, LANES), lambda b, c: (b, 0, c, 0)),
            pl.BlockSpec((1, 8, S, LANES), lambda b, c: (b, 0, 0, 0)),
            pl.BlockSpec((4, 1, ncmp, LANES), lambda b, c: (0, b, 0, 0)),
            pl.BlockSpec((NSA_TQ, LANES), lambda b, c: (b * nq + c, 0)),
            pl.BlockSpec(cov_t.shape, lambda b, c: (0, 0)),
            pl.BlockSpec(eye.shape, lambda b, c: (0, 0)),
        ],
        out_specs=pl.BlockSpec((1, NSA_TQ, NSA_HEADS * NSA_DH), lambda b, c: (b, c, 0)),
        out_shape=jax.ShapeDtypeStruct((B, S, NSA_HEADS * NSA_DH), BF16),
        scratch_shapes=[
            pltpu.VMEM((NSA_GROUPS, S, BIAS_LANE0 + NSA_TQ), BF16),
            pltpu.VMEM((nblk, NSA_GROUPS * NSA_TQ), F32),
            pltpu.VMEM((NSA_GROUPS, nblk, NSA_TQ), F32),
            pltpu.VMEM((NSA_GROUPS, rows, SLC_CHUNK), F32),
            pltpu.VMEM((NSA_GROUPS, rows, SLC_CHUNK), F32),
            pltpu.VMEM((NSA_GROUPS, rows, SLC_CHUNK), BF16),
            pltpu.VMEM((NSA_GROUPS, rows, SLC_CHUNK), BF16),
            pltpu.VMEM((NSA_GROUPS, rows, LANES), F32),
            pltpu.VMEM((NSA_GROUPS, rows, LANES), F32),
            pltpu.VMEM((NSA_GROUPS, rows, WIN_SPAN), F32),
            pltpu.VMEM((NSA_GROUPS, rows, WIN_SPAN), BF16),
            pltpu.VMEM((NSA_GROUPS, rows, LANES), F32),
            pltpu.VMEM((NSA_GROUPS, rows, ncmp), F32),
            pltpu.VMEM((NSA_GROUPS, rows, ncmp), F32),
            pltpu.VMEM((NSA_GROUPS, rows, ncmp), BF16),
            pltpu.VMEM((NSA_GROUPS, rows, LANES), F32),
        ],
        compiler_params=_params("arbitrary", "arbitrary"),
        name="nsa",
    )(qn, kvn, kvcmp, misc, cov_t, eye)


MLA_TQ = 512
MLA_CHUNK = 512
MLA_HPB = 4


def _mla_body(q_ref, k_ref, v_ref, o_ref, sa_ref, sb_ref, pa_ref, pb_ref, m_ref, acc_ref):
    qi = pl.program_id(2)
    t_row = qi * MLA_TQ + lax.broadcasted_iota(jnp.int32, (MLA_TQ, 1), 0)
    _flash_reset(m_ref, acc_ref)

    def scores(kc, dst):
        k0 = pl.multiple_of(kc * MLA_CHUNK, MLA_CHUNK)
        for j in range(MLA_HPB):
            dst[j] = _dot_nt(q_ref[0, j], k_ref[0, j, pl.ds(k0, MLA_CHUNK), :])

    def update(src, p_ref, kc, causal):
        k0 = pl.multiple_of(kc * MLA_CHUNK, MLA_CHUNK)
        mask = None
        if causal:
            mask = (k0 + lax.broadcasted_iota(jnp.int32, (1, MLA_CHUNK), 1)) <= t_row
        for j in range(MLA_HPB):
            _flash_update(src.at[j], v_ref[0, j, pl.ds(k0, MLA_CHUNK), :],
                          m_ref.at[j], acc_ref.at[j], p_ref.at[j], mask)

    scores(0, sa_ref)

    def pair(i, carry):
        scores(2 * i + 1, sb_ref)
        update(sa_ref, pa_ref, 2 * i, False)
        scores(2 * i + 2, sa_ref)
        update(sb_ref, pb_ref, 2 * i + 1, False)
        return carry
    lax.fori_loop(0, qi // 2, pair, 0)
    tail = 2 * (qi // 2)

    @pl.when(qi > tail)
    def _():
        scores(tail + 1, sb_ref)
        update(sa_ref, pa_ref, tail, False)
        update(sb_ref, pb_ref, tail + 1, True)

    @pl.when(qi == tail)
    def _():
        update(sa_ref, pa_ref, tail, True)
    o_ref[0] = jnp.concatenate([_flash_finish(acc_ref[j])[:, :MLA_V] for j in range(MLA_HPB)],
                               axis=1).astype(BF16)


def _mla(q, k, v, B, S):
    return pl.pallas_call(
        _mla_body,
        grid=(B, MLA_HEADS // MLA_HPB, S // MLA_TQ),
        in_specs=[
            pl.BlockSpec((1, MLA_HPB, MLA_TQ, LANES), lambda b, h, i: (b, h, i, 0)),
            pl.BlockSpec((1, MLA_HPB, S, LANES), lambda b, h, i: (b, h, 0, 0)),
            pl.BlockSpec((1, MLA_HPB, S, LANES), lambda b, h, i: (b, h, 0, 0)),
        ],
        out_specs=pl.BlockSpec((1, MLA_TQ, MLA_HPB * MLA_V), lambda b, h, i: (b, i, h)),
        out_shape=jax.ShapeDtypeStruct((B, S, MLA_HEADS * MLA_V), BF16),
        scratch_shapes=[
            pltpu.VMEM((MLA_HPB, MLA_TQ, MLA_CHUNK), F32),
            pltpu.VMEM((MLA_HPB, MLA_TQ, MLA_CHUNK), F32),
            pltpu.VMEM((MLA_HPB, MLA_TQ, MLA_CHUNK), BF16),
            pltpu.VMEM((MLA_HPB, MLA_TQ, MLA_CHUNK), BF16),
            pltpu.VMEM((MLA_HPB, MLA_TQ, LANES), F32),
            pltpu.VMEM((MLA_HPB, MLA_TQ, LANES), F32),
        ],
        compiler_params=_params("arbitrary", "arbitrary", "arbitrary"),
        name="mla",
    )(q, k, v)


def _mem_kv_body(m_ref, wk_ref, wv_ref, k_ref, v_ref):
    mb = m_ref[...].astype(BF16)
    k_ref[...] = _dot(mb, wk_ref[...]).astype(BF16)
    v_ref[...] = _dot(mb, wv_ref[...]).astype(BF16)


def _mem_kv(memf, wk, wv, tm):
    R, D = memf.shape
    tok = lambda i: (i, 0)
    const = lambda i: (0, 0)
    return pl.pallas_call(
        _mem_kv_body,
        grid=(R // tm,),
        in_specs=[pl.BlockSpec((tm, D), tok), pl.BlockSpec(wk.shape, const),
                  pl.BlockSpec(wv.shape, const)],
        out_specs=[pl.BlockSpec((tm, D), tok), pl.BlockSpec((tm, D), tok)],
        out_shape=[jax.ShapeDtypeStruct((R, D), BF16), jax.ShapeDtypeStruct((R, D), BF16)],
        compiler_params=_params("arbitrary"),
        name="mem_kv",
    )(memf, wk, wv)


def _mem_attn_body(x0_ref, on_ref, om_ref, wmix_ref, g1_ref, b1_ref,
                   k_ref, v_ref, wq_ref, wo_ref, g_ref, b_ref, o_ref, *, alpha):
    half = on_ref.shape[1]
    mix = _dot(on_ref[...], wmix_ref[:half, :]) + _dot(om_ref[...], wmix_ref[half:, :])
    x = _layer_norm(alpha * x0_ref[...] + mix, g1_ref[...], b1_ref[...])
    D = x.shape[1]
    dh = D // MEM_HEADS
    q = (_dot(x.astype(BF16), wq_ref[...]) * (dh ** -0.5 * LOG2E)).astype(BF16)
    outs = []
    for h in range(MEM_HEADS):
        cs = slice(dh * h, dh * (h + 1))
        s = _dot_nt(q[:, cs], k_ref[0, :, cs])
        p = jnp.exp2(s - jnp.max(s, axis=1, keepdims=True))
        l = jnp.sum(p, axis=1, keepdims=True)
        outs.append((_dot(p.astype(BF16), v_ref[0, :, cs]) * (1.0 / l)).astype(BF16))
    o = jnp.concatenate(outs, axis=1)
    y = _dot(o, wo_ref[...])
    o_ref[...] = _layer_norm(alpha * x + y, g_ref[...], b_ref[...])


def _mem_attn(xf, o_nsa, o_mla, w_o, g1, b1, k_mem, v_mem, wq, wo, g, b, alpha, S, tm):
    T, D = xf.shape
    nst = S // tm
    M = k_mem.shape[1]
    tok = lambda i: (i, 0)
    const = lambda i: (0, 0)
    memb = lambda i: (i // nst, 0, 0)
    return pl.pallas_call(
        functools.partial(_mem_attn_body, alpha=alpha),
        grid=(T // tm,),
        in_specs=[
            pl.BlockSpec((tm, D), tok),
            pl.BlockSpec((tm, o_nsa.shape[1]), tok),
            pl.BlockSpec((tm, o_mla.shape[1]), tok),
            pl.BlockSpec(w_o.shape, const),
            pl.BlockSpec((1, D), const),
            pl.BlockSpec((1, D), const),
            pl.BlockSpec((1, M, D), memb),
            pl.BlockSpec((1, M, D), memb),
            pl.BlockSpec(wq.shape, const),
            pl.BlockSpec(wo.shape, const),
            pl.BlockSpec((1, D), const),
            pl.BlockSpec((1, D), const),
        ],
        out_specs=pl.BlockSpec((tm, D), tok),
        out_shape=jax.ShapeDtypeStruct((T, D), F32),
        compiler_params=_params("arbitrary"),
        name="mem_attn",
    )(xf, o_nsa, o_mla, w_o, g1, b1, k_mem, v_mem, wq, wo, g, b)


HALO = 8


FFN_SLAB = 256
FFN_TM = 512


def _ffn_body(x_ref, xh_ref, wg_ref, wu_ref, cw_ref, cb_ref, wd_ref, g_ref, b_ref, o_ref,
              act_ref, *, alpha, seq_tiles):
    i = pl.program_id(0)
    x = x_ref[...]
    xb = x.astype(BF16)
    xhb = xh_ref[...].astype(BF16)
    tm = x.shape[0]
    row = lax.broadcasted_iota(jnp.int32, (tm, 1), 0)
    seq_start = i % seq_tiles == 0
    for c0 in range(0, wg_ref.shape[1], FFN_SLAB):
        cs = slice(c0, c0 + FFN_SLAB)
        gate = _dot(xb, wg_ref[:, cs])
        up = _dot(xb, wu_ref[:, cs])
        halo = jnp.where(seq_start, 0.0, _dot(xhb, wg_ref[:, cs]))
        g1 = jnp.where(row == 0, halo[HALO - 1:HALO], pltpu.roll(gate, 1, 0))
        g2 = jnp.where(row == 0, halo[HALO - 2:HALO - 1],
                       jnp.where(row == 1, halo[HALO - 1:HALO], pltpu.roll(gate, 2, 0)))
        conv = cw_ref[0:1, cs] * g2 + cw_ref[1:2, cs] * g1 + cw_ref[2:3, cs] * gate + cb_ref[:, cs]
        act_ref[:, cs] = (conv * (1.0 / (1.0 + jnp.exp(-conv))) * up).astype(BF16)
    y = _dot(act_ref[...], wd_ref[...])
    o_ref[...] = _layer_norm(alpha * x + y, g_ref[...], b_ref[...])


def _ffn(xf, wg, wu, cw, cb, wd, g, b, alpha, S, tm):
    T, D = xf.shape
    dff = wg.shape[1]
    assert dff % FFN_SLAB == 0
    tok = lambda i: (i, 0)
    const = lambda i: (0, 0)
    return pl.pallas_call(
        functools.partial(_ffn_body, alpha=alpha, seq_tiles=S // tm),
        grid=(T // tm,),
        in_specs=[
            pl.BlockSpec((tm, D), tok),
            pl.BlockSpec((HALO, D), lambda i: (jnp.maximum(i * (tm // HALO) - 1, 0), 0)),
            pl.BlockSpec((D, dff), const),
            pl.BlockSpec((D, dff), const),
            pl.BlockSpec((CONV_WIDTH, dff), const),
            pl.BlockSpec((1, dff), const),
            pl.BlockSpec((dff, D), const),
            pl.BlockSpec((1, D), const),
            pl.BlockSpec((1, D), const),
        ],
        out_specs=pl.BlockSpec((tm, D), tok),
        out_shape=jax.ShapeDtypeStruct((T, D), F32),
        scratch_shapes=[pltpu.VMEM((tm, dff), BF16)],
        compiler_params=_params("arbitrary"),
        name="ffn",
    )(xf, xf, wg, wu, cw, cb, wd, g, b)


def _inv_freq_row(dim, lane_lo, lane_hi, period):
    inv = ROPE_THETA ** (-np.arange(0, dim, 2, dtype=np.float64) / dim)
    row = np.zeros((1, LANES), np.float32)
    for lane in range(lane_lo, lane_hi):
        row[0, lane] = inv[(lane % period) % (dim // 2)]
    return jnp.asarray(row)


def _cover_t(S):
    nc = S // CMP_STRIDE
    ns = S // SLC_LEN
    cs = np.arange(nc)[:, None] * CMP_STRIDE
    ss = np.arange(ns)[None, :] * SLC_LEN
    cover = np.clip(np.minimum(cs + CMP_LEN, ss + SLC_LEN) - np.maximum(cs, ss), 0, None) / CMP_LEN
    cover[nc - 1:] = 0.0
    return jnp.asarray(cover.T, dtype=BF16)


def _eye_aug():
    eye = np.zeros((NSA_ROWS, NSA_TQ), np.float32)
    r = np.arange(NSA_ROWS)
    eye[r, r % NSA_TQ] = 1.0
    return jnp.asarray(eye, dtype=BF16)


def _permute_w_in(w):
    D = w.shape[0]
    c1 = NSA_HEADS * NSA_DH
    c2 = c1 + 3 * 2 * NSA_GROUPS * NSA_DH
    c3 = c2 + 3 * NSA_HEADS
    c4 = c3 + MLA_Q_RANK
    c5 = c4 + MLA_KV_RANK
    c6 = c5 + MLA_ROPE
    half = MLA_ROPE // 2
    z = lambda n: jnp.zeros((D, n), w.dtype)
    misc = jnp.concatenate(
        [w[:, c5 + half:c6], z(GATE_LANE0 - half), w[:, c2:c3], z(MLA_PE1 - GATE_LANE0 - (c3 - c2)),
         w[:, c5:c5 + half], z(LANES - MLA_PE1 - half)], axis=1)
    return jnp.concatenate([w[:, :c2], w[:, c3:c5], misc], axis=1).astype(BF16)


def _mla_head_lanes(nope, pe):
    r, H, _ = nope.shape
    half = MLA_ROPE // 2
    split = MLA_PE1 - half
    pad = jnp.zeros((r, H, LANES - MLA_NOPE - MLA_ROPE), nope.dtype)
    return jnp.concatenate([pe[..., half:], nope[..., :split], pe[..., :half], nope[..., split:], pad],
                           axis=2).reshape(r, H * LANES)


def _permute_w_uq(w):
    r = w.shape[0]
    w3 = w.reshape(r, MLA_HEADS, MLA_NOPE + MLA_ROPE)
    return _mla_head_lanes(w3[..., :MLA_NOPE], w3[..., MLA_NOPE:]).astype(BF16)


def _permute_w_ukv(w):
    r = w.shape[0]
    w3 = w.reshape(r, MLA_HEADS, MLA_NOPE + MLA_V)
    k = _mla_head_lanes(w3[..., :MLA_NOPE], jnp.zeros((r, MLA_HEADS, MLA_ROPE), w.dtype))
    v = _pad_lanes(w3[..., MLA_NOPE:]).reshape(r, MLA_HEADS * LANES)
    return jnp.concatenate([k, v], axis=1).astype(BF16)


def _pad_lanes(a):
    return jnp.concatenate([a, jnp.zeros(a.shape[:-1] + (LANES - a.shape[-1],), a.dtype)], axis=-1)


def kernel(x, mem, positions, w_in, nsa_k_pos, nsa_ck_w1, nsa_ck_b1, nsa_ck_w2, nsa_ck_b2,
           nsa_v_pos, nsa_cv_w1, nsa_cv_b1, nsa_cv_w2, nsa_cv_b2,
           mla_q_norm, mla_w_uq, mla_kv_norm, mla_w_ukv, w_o, ln1_g, ln1_b,
           mem_wq, mem_wk, mem_wv, mem_wo, ln2_g, ln2_b,
           ffn_w_up, ffn_conv_w, ffn_conv_b, ffn_w_down, ln3_g, ln3_b):
    B, S, D = x.shape
    T = B * S
    depth = w_in.shape[0]
    alpha = (2.0 * depth) ** 0.25
    d_ff = ffn_w_down.shape[1]
    tm = min(512, S)
    assert S % MLA_TQ == 0 and S >= WIN_SPAN and S % tm == 0
    assert (B * mem.shape[1]) % 256 == 0

    pos = positions.reshape(T, 1)
    pos_cmp = positions[:, CMP_LEN - 1::CMP_STRIDE]
    pos_cmp = jnp.concatenate([pos_cmp, pos_cmp[:, -1:]], axis=1)[:, :, None]
    inv_cmp = _inv_freq_row(NSA_DH, 0, NSA_DH, NSA_DH)
    inv_tok = (_inv_freq_row(NSA_DH, 0, NSA_DH // 2, NSA_DH)
               + _inv_freq_row(MLA_ROPE, NSA_DH // 2, NSA_DH // 2 + MLA_ROPE // 2, MLA_ROPE // 2))
    cov_t = _cover_t(S)
    eye = _eye_aug()
    memf = mem.reshape(B * mem.shape[1], D)

    xf = x.reshape(T, D)
    for l in range(depth):
        qn, kvn, kvc, misc, q_m, k_m, v_m = _inproj(
            pos, inv_tok, xf, _permute_w_in(w_in[l]), mla_q_norm[l][None, :],
            mla_kv_norm[l][None, :], _permute_w_uq(mla_w_uq[l]), _permute_w_ukv(mla_w_ukv[l]),
            B, S, tm)
        kvcmp = _compress(
            pos_cmp, inv_cmp, kvc,
            *_compress_weights(nsa_k_pos[l], nsa_ck_w1[l], nsa_ck_b1[l], nsa_ck_w2[l], nsa_ck_b2[l],
                               nsa_v_pos[l], nsa_cv_w1[l], nsa_cv_b1[l], nsa_cv_w2[l], nsa_cv_b2[l]),
            B, S)
        o_nsa = _nsa(qn, kvn, kvcmp, misc, cov_t, eye, B, S)
        o_mla = _mla(q_m, k_m, v_m, B, S)
        k_mem, v_mem = _mem_kv(memf, mem_wk[l].astype(BF16), mem_wv[l].astype(BF16), 256)
        xf = _mem_attn(xf, o_nsa.reshape(T, -1), o_mla.reshape(T, -1), w_o[l].astype(BF16),
                       ln1_g[l][None, :], ln1_b[l][None, :],
                       k_mem.reshape(B, -1, D), v_mem.reshape(B, -1, D),
                       mem_wq[l].astype(BF16), mem_wo[l].astype(BF16),
                       ln2_g[l][None, :], ln2_b[l][None, :], alpha, S, tm)
        xf = _ffn(xf, ffn_w_up[l][:, :d_ff].astype(BF16), ffn_w_up[l][:, d_ff:].astype(BF16),
                  ffn_conv_w[l], ffn_conv_b[l][None, :], ffn_w_down[l].astype(BF16),
                  ln3_g[l][None, :], ln3_b[l][None, :], alpha, S, FFN_TM)
    return xf.reshape(B, S, D)
```

```python
import functools
import math

import numpy as np
import jax
import jax.numpy as jnp
from jax import lax
from jax.experimental import pallas as pl
from jax.experimental.pallas import tpu as pltpu

F32 = jnp.float32
BF16 = jnp.bfloat16

NSA_HEADS = 8
NSA_GROUPS = 2
NSA_REP = NSA_HEADS // NSA_GROUPS
NSA_DH = 64
CMP_STRIDE = 16
CMP_LEN = 32
SLC_LEN = 64
TOPN = 16
WINDOW = 512
CMP_HIDDEN = 128
FORCE_BONUS = 1e4
MLA_HEADS = 8
MLA_Q_RANK = 384
MLA_KV_RANK = 256
MLA_NOPE = 64
MLA_ROPE = 32
MLA_V = 64
MEM_HEADS = 4
CONV_WIDTH = 3
ROPE_THETA = 10000.0
LN_EPS = 1e-5
RMS_EPS = 1e-6
NEG_INF = -1e30
LOG2E = math.log2(math.e)

LANES = 128
VMEM_LIMIT = 56 * 1024 * 1024

C_Q = 0
C_KVC = 512
C_KVN = 768
C_LAT = 1280
C_MISC = 1920
IN_COLS_PAD = 2048
GATE_LANE0 = MLA_ROPE
MLA_PE2 = 0
MLA_PE1 = 64
ONES_LANE = 64


def _dot(a, b):
    return jnp.dot(a, b, preferred_element_type=F32)


def _dot_nt(a, b):
    return lax.dot_general(a, b, (((1,), (1,)), ((), ())), preferred_element_type=F32)


def _layer_norm(y, g, b):
    mu = jnp.mean(y, axis=-1, keepdims=True)
    d = y - mu
    var = jnp.mean(d * d, axis=-1, keepdims=True)
    return d * lax.rsqrt(var + LN_EPS) * g + b


def _params(*sem):
    return pltpu.CompilerParams(dimension_semantics=sem, vmem_limit_bytes=VMEM_LIMIT)


def _rope_tables(pos_col, inv_row, half):
    ang = pos_col * inv_row
    cos = jnp.cos(ang)
    sin = jnp.sin(ang)
    lane = lax.broadcasted_iota(jnp.int32, (1, LANES), 1)
    upper = (lane & (2 * half - 1)) >= half
    rot = inv_row != 0.0
    sin_hi = jnp.where(upper & rot, sin, 0.0)
    sin_lo = jnp.where(upper | (~rot), 0.0, -sin)
    return cos, sin_hi, sin_lo


def _apply_rope(v, tabs, half):
    cos, sin_hi, sin_lo = tabs
    return v * cos + pltpu.roll(v, half, 1) * sin_hi + pltpu.roll(v, LANES - half, 1) * sin_lo


def _rms_norm(v, g):
    return v * lax.rsqrt(jnp.mean(v * v, axis=-1, keepdims=True) + RMS_EPS) * g


def _inproj_body(pos_ref, inv_ref, x_ref, w_ref, gq_ref, gkv_ref, wq_ref, wkv_ref,
                 qn_ref, kvn_ref, kvc_ref, misc_ref, qm_ref, km_ref, vm_ref):
    xb = x_ref[...].astype(BF16)
    lane = lax.broadcasted_iota(jnp.int32, (1, LANES), 1)
    low = lane < NSA_DH
    qscale = NSA_DH ** -0.5 * LOG2E

    ang = pos_ref[...].astype(F32) * inv_ref[...]
    cos, sin = jnp.cos(ang), jnp.sin(ang)
    nf, mf = NSA_DH // 2, MLA_ROPE // 2

    def tile_nsa(t):
        t = jnp.where(lane < nf, t, 0.0)
        t = t + pltpu.roll(t, nf, 1)
        return t + pltpu.roll(t, 2 * nf, 1)

    def place_mla(t):
        t = jnp.where((lane >= nf) & (lane < nf + mf), t, 0.0)
        return pltpu.roll(t, MLA_PE1 - nf, 1) + pltpu.roll(t, LANES + MLA_PE2 - nf, 1)

    upper = (lane & (NSA_DH - 1)) >= nf
    sin_n = tile_nsa(sin)
    tabs = (tile_nsa(cos), jnp.where(upper, sin_n, 0.0), jnp.where(upper, 0.0, -sin_n))
    pe1 = (lane >= MLA_PE1) & (lane < MLA_PE1 + mf)
    pe2 = (lane >= MLA_PE2) & (lane < MLA_PE2 + mf)
    pe_lanes = pe1 | pe2
    sin_m = place_mla(sin)
    cos_m = jnp.where(pe_lanes, place_mla(cos), 1.0)
    sin_m = jnp.where(pe1, -sin_m, jnp.where(pe2, sin_m, 0.0))

    def rope_mla(v):
        return v * cos_m + pltpu.roll(v, LANES // 2, 1) * sin_m

    def proj(c0, n):
        return _dot(xb, w_ref[:, c0:c0 + n])

    def split_store(v, ref, idx_lo, idx_hi, pad=0.0):
        ref[0, idx_lo] = jnp.where(low, v, pad).astype(BF16)
        ref[0, idx_hi] = jnp.where(low, pltpu.roll(v, NSA_DH, 1), pad).astype(BF16)

    ones_pad = jnp.where(lane == ONES_LANE, 1.0, 0.0)

    for slab in range(2):
        h = proj(C_Q + 256 * slab, 256)
        for j in range(2):
            r = _apply_rope(h[:, LANES * j:LANES * (j + 1)], tabs, NSA_DH // 2) * qscale
            split_store(r, qn_ref, 4 * slab + 2 * j, 4 * slab + 2 * j + 1)

    h = proj(C_KVC, 256)
    kvc_ref[0] = h[:, :LANES]
    kvc_ref[1] = h[:, LANES:]

    for slab in range(2):
        h = proj(C_KVN + 256 * slab, 256)
        k = _apply_rope(h[:, :LANES], tabs, NSA_DH // 2)
        split_store(k, kvn_ref, 4 * slab, 4 * slab + 1)
        split_store(h[:, LANES:], kvn_ref, 4 * slab + 2, 4 * slab + 3, ones_pad)

    lat = [proj(C_LAT + 256 * i, 256) for i in range((IN_COLS_PAD - C_LAT) // 256)]
    misc = lat[2][:, LANES:]
    misc_ref[...] = misc
    mq = jnp.concatenate([lat[0], lat[1][:, :LANES]], axis=1)
    mkv = jnp.concatenate([lat[1][:, LANES:], lat[2][:, :LANES]], axis=1)
    mscale = (MLA_NOPE + MLA_ROPE) ** -0.5 * LOG2E
    qn = _rms_norm(mq, gq_ref[...]).astype(BF16)
    kvn = _rms_norm(mkv, gkv_ref[...]).astype(BF16)
    kpe = jnp.where(pe_lanes, rope_mla(misc), 0.0)
    kcols = MLA_HEADS * LANES
    for slab in range(MLA_HEADS // 2):
        cs = slice(256 * slab, 256 * (slab + 1))
        hq = _dot(qn, wq_ref[:, cs])
        hk = _dot(kvn, wkv_ref[:, cs])
        hv = _dot(kvn, wkv_ref[:, kcols + 256 * slab:kcols + 256 * (slab + 1)])
        for j in range(2):
            ls = slice(LANES * j, LANES * (j + 1))
            qm_ref[0, 2 * slab + j] = (rope_mla(hq[:, ls]) * mscale).astype(BF16)
            km_ref[0, 2 * slab + j] = (hk[:, ls] + kpe).astype(BF16)
            vm_ref[0, 2 * slab + j] = (hv[:, ls] + ones_pad).astype(BF16)


def _inproj(pos, inv_row, xf, w_in_p, gq, gkv, wq_p, wkv_p, B, S, tm):
    T = B * S
    nst = S // tm
    tok = lambda i: (i, 0)
    const = lambda i: (0, 0)
    head_blk = lambda i: (i // nst, 0, i % nst, 0)
    heads = jax.ShapeDtypeStruct((B, 8, S, LANES), BF16)
    return pl.pallas_call(
        _inproj_body,
        grid=(T // tm,),
        in_specs=[
            pl.BlockSpec((tm, 1), tok),
            pl.BlockSpec((1, LANES), const),
            pl.BlockSpec((tm, xf.shape[1]), tok),
            pl.BlockSpec(w_in_p.shape, const),
            pl.BlockSpec(gq.shape, const),
            pl.BlockSpec(gkv.shape, const),
            pl.BlockSpec(wq_p.shape, const),
            pl.BlockSpec(wkv_p.shape, const),
        ],
        out_specs=[
            pl.BlockSpec((1, 8, tm, LANES), head_blk),
            pl.BlockSpec((1, 8, tm, LANES), head_blk),
            pl.BlockSpec((2, tm, LANES), lambda i: (0, i, 0)),
            pl.BlockSpec((tm, LANES), tok),
            pl.BlockSpec((1, 8, tm, LANES), head_blk),
            pl.BlockSpec((1, 8, tm, LANES), head_blk),
            pl.BlockSpec((1, 8, tm, LANES), head_blk),
        ],
        out_shape=[
            heads,
            heads,
            jax.ShapeDtypeStruct((2, T, LANES), F32),
            jax.ShapeDtypeStruct((T, LANES), F32),
            heads,
            heads,
            heads,
        ],
        compiler_params=_params("arbitrary"),
        name="inproj",
    )(pos, inv_row, xf, w_in_p, gq, gkv, wq_p, wkv_p)


def _compress_body(pos_ref, inv_ref, x_ref, pe_ref, w1_ref, b1_ref, w2_ref, b2_ref, o_ref):
    is_k = pl.program_id(0) == 0
    nch = o_ref.shape[2]
    a1 = jnp.zeros((nch, NSA_GROUPS * CMP_HIDDEN), F32)
    a2 = jnp.zeros((nch, NSA_GROUPS * CMP_HIDDEN), F32)
    for l in range(CMP_STRIDE):
        xl = x_ref.at[0, 0][pl.ds(l, nch, stride=CMP_STRIDE), :]
        a1 = a1 + _dot((xl + pe_ref[0, l:l + 1, :]).astype(BF16), w1_ref[0, l])
        a2 = a2 + _dot((xl + pe_ref[0, CMP_STRIDE + l:CMP_STRIDE + l + 1, :]).astype(BF16),
                       w1_ref[0, CMP_STRIDE + l])
    pre = a1 + pltpu.roll(a2, nch - 1, 0) + b1_ref[0]
    hid = jax.nn.gelu(pre, approximate=True)
    out = _dot(hid.astype(BF16), w2_ref[0]) + b2_ref[0]
    tabs = _rope_tables(pos_ref[0].astype(F32), inv_ref[...], NSA_DH // 2)
    row = lax.broadcasted_iota(jnp.int32, (nch, 1), 0)
    for g in range(NSA_GROUPS):
        og = out[:, LANES * g:LANES * (g + 1)]
        og = jnp.where(is_k, _apply_rope(og, tabs, NSA_DH // 2), og)
        o_ref[g, 0] = jnp.where(row < nch - 1, og, 0.0).astype(BF16)


def _compress(pos_cmp, inv_nsa, kvc, pe, w1, b1, w2, b2, B, S):
    nch = S // CMP_STRIDE
    x = kvc.reshape(2, B, S, LANES)
    kv = lambda j, b: (j, 0, 0)
    kv4 = lambda j, b: (j, 0, 0, 0)
    return pl.pallas_call(
        _compress_body,
        grid=(2, B),
        in_specs=[
            pl.BlockSpec((1, nch, 1), lambda j, b: (b, 0, 0)),
            pl.BlockSpec((1, LANES), lambda j, b: (0, 0)),
            pl.BlockSpec((1, 1, S, LANES), lambda j, b: (j, b, 0, 0)),
            pl.BlockSpec((1,) + pe.shape[1:], kv),
            pl.BlockSpec((1,) + w1.shape[1:], kv4),
            pl.BlockSpec((1,) + b1.shape[1:], kv),
            pl.BlockSpec((1,) + w2.shape[1:], kv),
            pl.BlockSpec((1,) + b2.shape[1:], kv),
        ],
        out_specs=pl.BlockSpec((NSA_GROUPS, 1, nch, LANES), lambda j, b: (j, b, 0, 0)),
        out_shape=jax.ShapeDtypeStruct((2 * NSA_GROUPS, B, nch, LANES), BF16),
        compiler_params=_params("arbitrary", "arbitrary"),
        name="compress",
    )(pos_cmp, inv_nsa, x, pe, w1, b1, w2, b2)


def _compress_weights(k_pos, k_w1, k_b1, k_w2, k_b2, v_pos, v_w1, v_b1, v_w2, v_b2):
    def one(pos, w1, b1, w2, b2):
        w1l = w1.reshape(CMP_LEN, NSA_DH, CMP_HIDDEN)
        z1 = jnp.zeros_like(w1l)
        w1bd = jnp.concatenate([jnp.concatenate([w1l, z1], axis=2),
                                jnp.concatenate([z1, w1l], axis=2)], axis=1)
        w2p = _pad_lanes(w2)
        z2 = jnp.zeros_like(w2p)
        w2bd = jnp.concatenate([jnp.concatenate([w2p, z2], axis=1),
                                jnp.concatenate([z2, w2p], axis=1)], axis=0)
        return (jnp.tile(pos, (1, NSA_GROUPS)), w1bd.astype(BF16), jnp.tile(b1, NSA_GROUPS)[None, :],
                w2bd.astype(BF16), jnp.tile(_pad_lanes(b2), NSA_GROUPS)[None, :])
    k = one(k_pos, k_w1, k_b1, k_w2, k_b2)
    v = one(v_pos, v_w1, v_b1, v_w2, v_b2)
    return tuple(jnp.stack([a, b]) for a, b in zip(k, v))


STRIP = 64


def _lane_tile(col, n):
    reps = [col] * (n // LANES)
    if n % LANES:
        reps.append(col[:, :n % LANES])
    return reps[0] if len(reps) == 1 else jnp.concatenate(reps, axis=1)


def _flash_reset(m_ref, acc_ref):
    m_ref[...] = jnp.full(m_ref.shape, NEG_INF, F32)
    acc_ref[...] = jnp.zeros(acc_ref.shape, F32)


def _flash_update(s_ref, v, m_ref, acc_ref, p_ref, mask=None):
    rows, n = s_ref.shape
    for r in range(rows // STRIP):
        rs = slice(STRIP * r, STRIP * (r + 1))
        s = s_ref[rs, :]
        if mask is not None:
            s = jnp.where(mask[rs], s, NEG_INF)
        if m_ref is None:
            p_ref[rs, :] = jnp.exp2(s - jnp.max(s, axis=1, keepdims=True)).astype(BF16)
            continue
        m_old = m_ref[rs, :]
        m_new = jnp.maximum(m_old, jnp.max(s, axis=1, keepdims=True))
        p_ref[rs, :] = jnp.exp2(s - _lane_tile(m_new, n)).astype(BF16)
        acc_ref[rs, :] = jnp.exp2(m_old - m_new) * acc_ref[rs, :]
        m_ref[rs, :] = m_new
    if m_ref is None:
        acc_ref[...] = _dot(p_ref[...], v)
    else:
        acc_ref[...] += _dot(p_ref[...], v)


def _flash_finish(acc):
    return acc * (1.0 / acc[:, ONES_LANE:ONES_LANE + 1])


NSA_TQ = 128
NSA_ROWS = NSA_REP * NSA_TQ
SLC_CHUNK = 512
WIN_SPAN = WINDOW + NSA_TQ
BIAS_LANE0 = LANES


def _nsa_body(q_ref, kvn_ref, kvc_ref, misc_ref, cov_ref, eye_ref, o_ref,
              kaug_ref, score_ref, bias_ref, sa_ref, sb_ref, pa_ref, pb_ref, ms_ref, accs_ref,
              sw_ref, pw_ref, accw_ref, sc_ref, pn_ref, pc_ref, oc_ref):
    c = pl.program_id(1)
    rows = NSA_ROWS
    t_row = c * NSA_TQ + (lax.broadcasted_iota(jnp.int32, (rows, 1), 0) & (NSA_TQ - 1))
    ncmp = kvc_ref.shape[2]
    nblk = kaug_ref.shape[1] // SLC_LEN

    @pl.when(c == 0)
    def _():
        for g in range(NSA_GROUPS):
            kaug_ref[g, :, :BIAS_LANE0] = kvn_ref[0, g]
            kaug_ref[g, :, BIAS_LANE0:] = jnp.zeros((kaug_ref.shape[1], NSA_TQ), BF16)

    qs = [q_ref[0, NSA_REP * g:NSA_REP * (g + 1)].reshape(rows, LANES) for g in range(NSA_GROUPS)]

    cmp_valid = (CMP_STRIDE * lax.broadcasted_iota(jnp.int32, (1, ncmp), 1) + CMP_LEN - 1) <= t_row
    for g in range(NSA_GROUPS):
        sc_ref[g] = _dot_nt(qs[g], kvc_ref[g, 0])
    for g in range(NSA_GROUPS):
        for r in range(rows // STRIP):
            rs = slice(STRIP * r, STRIP * (r + 1))
            s = jnp.where(cmp_valid[rs], sc_ref[g, rs, :], NEG_INF)
            p = jnp.where(cmp_valid[rs], jnp.exp2(s - jnp.max(s, axis=1, keepdims=True)), 0.0)
            l = jnp.sum(p, axis=1, keepdims=True)
            p = p * jnp.where(l > 0.0, 1.0 / l, 0.0)
            pn_ref[g, rs, :] = p
            pc_ref[g, rs, :] = p.astype(BF16)
        oc_ref[g] = _dot(pc_ref[g], kvc_ref[NSA_GROUPS + g, 0])

    win_start = pl.multiple_of(jnp.maximum(c * NSA_TQ - WINDOW, 0), NSA_TQ)
    diff = t_row - (win_start + lax.broadcasted_iota(jnp.int32, (1, WIN_SPAN), 1))
    win_valid = (diff >= 0) & (diff < WINDOW)
    for g in range(NSA_GROUPS):
        sw_ref[g] = _dot_nt(qs[g], kvn_ref[0, 4 + g, pl.ds(win_start, WIN_SPAN), :])
    for g in range(NSA_GROUPS):
        _flash_update(sw_ref.at[g], kvn_ref[0, 6 + g, pl.ds(win_start, WIN_SPAN), :],
                      None, accw_ref.at[g], pw_ref.at[g], win_valid)

    @pl.when((c + 1) * (NSA_TQ // SLC_LEN) > TOPN)
    def _():
        width = NSA_GROUPS * NSA_TQ
        ps = jnp.concatenate(
            [sum(pn_ref[g, NSA_TQ * r:NSA_TQ * (r + 1), :] for r in range(NSA_REP))
             for g in range(NSA_GROUPS)], axis=0)
        hi = ps.astype(BF16)
        lo = (ps - hi.astype(F32)).astype(BF16)
        imp = _dot_nt(cov_ref[...], hi) + _dot_nt(cov_ref[...], lo)
        jidx = lax.broadcasted_iota(jnp.int32, (nblk, width), 0)
        lane_q = lax.broadcasted_iota(jnp.int32, (1, width), 1) & (NSA_TQ - 1)
        cur = c * (NSA_TQ // SLC_LEN) + lane_q // SLC_LEN
        forced = (jidx == 0) | (jidx == cur) | (jidx == cur - 1)
        score = jnp.where(jidx <= cur, jnp.where(forced, FORCE_BONUS, imp), NEG_INF)
        score_ref[...] = score
        sub = 8
        cnt = [jnp.zeros((sub, width), F32) for _ in range(nblk // sub)]
        tiles = [score[sub * v:sub * (v + 1)] for v in range(nblk // sub)]
        sidx = lax.broadcasted_iota(jnp.int32, (sub, width), 0)
        for jp in range(nblk):
            rowv = jnp.broadcast_to(score_ref[jp:jp + 1, :], (sub, width))
            for v in range(nblk // sub):
                if sub * v > jp:
                    cnt[v] = jnp.where(rowv >= tiles[v], cnt[v] + 1.0, cnt[v])
                elif sub * v + sub - 1 <= jp:
                    cnt[v] = jnp.where(rowv > tiles[v], cnt[v] + 1.0, cnt[v])
                else:
                    ge = jnp.where(rowv >= tiles[v], cnt[v] + 1.0, cnt[v])
                    gt = jnp.where(rowv > tiles[v], cnt[v] + 1.0, cnt[v])
                    cnt[v] = jnp.where(sidx + sub * v > jp, ge, gt)
        rank = jnp.concatenate(cnt, axis=0)
        bias = jnp.where(rank < float(TOPN), 0.0, NEG_INF)
        for g in range(NSA_GROUPS):
            bias_ref[g] = bias[:, NSA_TQ * g:NSA_TQ * (g + 1)]

        def write_bias(j, carry):
            r0 = pl.multiple_of(j * SLC_LEN, SLC_LEN)
            for g in range(NSA_GROUPS):
                blk = jnp.broadcast_to(bias_ref[g, pl.ds(j, 1), :], (SLC_LEN, NSA_TQ))
                kaug_ref[g, pl.ds(r0, SLC_LEN), BIAS_LANE0:] = blk.astype(BF16)
            return carry
        lax.fori_loop(0, (c + 1) * (NSA_TQ // SLC_LEN), write_bias, 0)

    _flash_reset(ms_ref, accs_ref)
    qas = [jnp.concatenate([q, eye_ref[...]], axis=1) for q in qs]
    last = c // (SLC_CHUNK // NSA_TQ)

    def slc_scores(kc, dst):
        k0 = pl.multiple_of(kc * SLC_CHUNK, SLC_CHUNK)
        for g in range(NSA_GROUPS):
            dst[g] = _dot_nt(qas[g], kaug_ref[g, pl.ds(k0, SLC_CHUNK), :])

    def slc_update(src, p_ref, kc, causal):
        k0 = pl.multiple_of(kc * SLC_CHUNK, SLC_CHUNK)
        mask = None
        if causal:
            mask = (k0 + lax.broadcasted_iota(jnp.int32, (1, SLC_CHUNK), 1)) <= t_row
        for g in range(NSA_GROUPS):
            _flash_update(src.at[g], kvn_ref[0, 2 + g, pl.ds(k0, SLC_CHUNK), :],
                          ms_ref.at[g], accs_ref.at[g], p_ref.at[g], mask)

    slc_scores(0, sa_ref)

    def slc_pair(i, carry):
        slc_scores(2 * i + 1, sb_ref)
        slc_update(sa_ref, pa_ref, 2 * i, False)
        slc_scores(2 * i + 2, sa_ref)
        slc_update(sb_ref, pb_ref, 2 * i + 1, False)
        return carry
    lax.fori_loop(0, last // 2, slc_pair, 0)
    tail = 2 * (last // 2)

    @pl.when(last > tail)
    def _():
        slc_scores(tail + 1, sb_ref)
        slc_update(sa_ref, pa_ref, tail, False)
        slc_update(sb_ref, pb_ref, tail + 1, True)

    @pl.when(last == tail)
    def _():
        slc_update(sa_ref, pa_ref, tail, True)

    sig = 1.0 / (1.0 + jnp.exp(-misc_ref[...]))
    outs = []
    for g in range(NSA_GROUPS):
        for r in range(NSA_REP):
            lane0 = GATE_LANE0 + 3 * (NSA_REP * g + r)
            rs = slice(NSA_TQ * r, NSA_TQ * (r + 1))
            o = (sig[:, lane0:lane0 + 1] * oc_ref[g, rs, :]
                 + sig[:, lane0 + 1:lane0 + 2] * _flash_finish(accs_ref[g, rs, :])
                 + sig[:, lane0 + 2:lane0 + 3] * _flash_finish(accw_ref[g, rs, :]))
            outs.append(o[:, :NSA_DH])
    o_ref[0] = jnp.concatenate(outs, axis=1).astype(BF16)


def _nsa(qn, kvn, kvcmp, misc, cov_t, eye, B, S):
    nblk = S // SLC_LEN
    ncmp = S // CMP_STRIDE
    nq = S // NSA_TQ
    rows = NSA_ROWS
    return pl.pallas_call(
        _nsa_body,
        grid=(B, nq),
        in_specs=[
            pl.BlockSpec((1, NSA_HEADS, NSA_TQ, LANES), lambda b, c: (b, 0, c, 0)),
            pl.BlockSpec((1, 8, S, LANES), lambda b, c: (b, 0, 0, 0)),
            pl.BlockSpec((4, 1, ncmp, LANES), lambda b, c: (0, b, 0, 0)),
            pl.BlockSpec((NSA_TQ, LANES), lambda b, c: (b * nq + c, 0)),
            pl.BlockSpec(cov_t.shape, lambda b, c: (0, 0)),
            pl.BlockSpec(eye.shape, lambda b, c: (0, 0)),
        ],
        out_specs=pl.BlockSpec((1, NSA_TQ, NSA_HEADS * NSA_DH), lambda b, c: (b, c, 0)),
        out_shape=jax.ShapeDtypeStruct((B, S, NSA_HEADS * NSA_DH), BF16),
        scratch_shapes=[
            pltpu.VMEM((NSA_GROUPS, S, BIAS_LANE0 + NSA_TQ), BF16),
            pltpu.VMEM((nblk, NSA_GROUPS * NSA_TQ), F32),
            pltpu.VMEM((NSA_GROUPS, nblk, NSA_TQ), F32),
            pltpu.VMEM((NSA_GROUPS, rows, SLC_CHUNK), F32),
            pltpu.VMEM((NSA_GROUPS, rows, SLC_CHUNK), F32),
            pltpu.VMEM((NSA_GROUPS, rows, SLC_CHUNK), BF16),
            pltpu.VMEM((NSA_GROUPS, rows, SLC_CHUNK), BF16),
            pltpu.VMEM((NSA_GROUPS, rows, LANES), F32),
            pltpu.VMEM((NSA_GROUPS, rows, LANES), F32),
            pltpu.VMEM((NSA_GROUPS, rows, WIN_SPAN), F32),
            pltpu.VMEM((NSA_GROUPS, rows, WIN_SPAN), BF16),
            pltpu.VMEM((NSA_GROUPS, rows, LANES), F32),
            pltpu.VMEM((NSA_GROUPS, rows, ncmp), F32),
            pltpu.VMEM((NSA_GROUPS, rows, ncmp), F32),
            pltpu.VMEM((NSA_GROUPS, rows, ncmp), BF16),
            pltpu.VMEM((NSA_GROUPS, rows, LANES), F32),
        ],
        compiler_params=_params("arbitrary", "arbitrary"),
        name="nsa",
    )(qn, kvn, kvcmp, misc, cov_t, eye)


MLA_TQ = 512
MLA_CHUNK = 512
MLA_HPB = 2


def _mla_body(q_ref, k_ref, v_ref, o_ref, sa_ref, sb_ref, pa_ref, pb_ref, m_ref, acc_ref):
    qi = pl.program_id(2)
    t_row = qi * MLA_TQ + lax.broadcasted_iota(jnp.int32, (MLA_TQ, 1), 0)
    _flash_reset(m_ref, acc_ref)

    def scores(kc, dst):
        k0 = pl.multiple_of(kc * MLA_CHUNK, MLA_CHUNK)
        for j in range(MLA_HPB):
            dst[j] = _dot_nt(q_ref[0, j], k_ref[0, j, pl.ds(k0, MLA_CHUNK), :])

    def update(src, p_ref, kc, causal):
        k0 = pl.multiple_of(kc * MLA_CHUNK, MLA_CHUNK)
        mask = None
        if causal:
            mask = (k0 + lax.broadcasted_iota(jnp.int32, (1, MLA_CHUNK), 1)) <= t_row
        for j in range(MLA_HPB):
            _flash_update(src.at[j], v_ref[0, j, pl.ds(k0, MLA_CHUNK), :],
                          m_ref.at[j], acc_ref.at[j], p_ref.at[j], mask)

    scores(0, sa_ref)

    def pair(i, carry):
        scores(2 * i + 1, sb_ref)
        update(sa_ref, pa_ref, 2 * i, False)
        scores(2 * i + 2, sa_ref)
        update(sb_ref, pb_ref, 2 * i + 1, False)
        return carry
    lax.fori_loop(0, qi // 2, pair, 0)
    tail = 2 * (qi // 2)

    @pl.when(qi > tail)
    def _():
        scores(tail + 1, sb_ref)
        update(sa_ref, pa_ref, tail, False)
        update(sb_ref, pb_ref, tail + 1, True)

    @pl.when(qi == tail)
    def _():
        update(sa_ref, pa_ref, tail, True)
    o_ref[0] = jnp.concatenate([_flash_finish(acc_ref[j])[:, :MLA_V] for j in range(MLA_HPB)],
                               axis=1).astype(BF16)


def _mla(q, k, v, B, S):
    return pl.pallas_call(
        _mla_body,
        grid=(B, MLA_HEADS // MLA_HPB, S // MLA_TQ),
        in_specs=[
            pl.BlockSpec((1, MLA_HPB, MLA_TQ, LANES), lambda b, h, i: (b, h, i, 0)),
            pl.BlockSpec((1, MLA_HPB, S, LANES), lambda b, h, i: (b, h, 0, 0)),
            pl.BlockSpec((1, MLA_HPB, S, LANES), lambda b, h, i: (b, h, 0, 0)),
        ],
        out_specs=pl.BlockSpec((1, MLA_TQ, MLA_HPB * MLA_V), lambda b, h, i: (b, i, h)),
        out_shape=jax.ShapeDtypeStruct((B, S, MLA_HEADS * MLA_V), BF16),
        scratch_shapes=[
            pltpu.VMEM((MLA_HPB, MLA_TQ, MLA_CHUNK), F32),
            pltpu.VMEM((MLA_HPB, MLA_TQ, MLA_CHUNK), F32),
            pltpu.VMEM((MLA_HPB, MLA_TQ, MLA_CHUNK), BF16),
            pltpu.VMEM((MLA_HPB, MLA_TQ, MLA_CHUNK), BF16),
            pltpu.VMEM((MLA_HPB, MLA_TQ, LANES), F32),
            pltpu.VMEM((MLA_HPB, MLA_TQ, LANES), F32),
        ],
        compiler_params=_params("arbitrary", "arbitrary", "arbitrary"),
        name="mla",
    )(q, k, v)


def _mem_kv_body(m_ref, wk_ref, wv_ref, k_ref, v_ref):
    mb = m_ref[...].astype(BF16)
    k_ref[...] = _dot(mb, wk_ref[...]).astype(BF16)
    v_ref[...] = _dot(mb, wv_ref[...]).astype(BF16)


def _mem_kv(memf, wk, wv, tm):
    R, D = memf.shape
    tok = lambda i: (i, 0)
    const = lambda i: (0, 0)
    return pl.pallas_call(
        _mem_kv_body,
        grid=(R // tm,),
        in_specs=[pl.BlockSpec((tm, D), tok), pl.BlockSpec(wk.shape, const),
                  pl.BlockSpec(wv.shape, const)],
        out_specs=[pl.BlockSpec((tm, D), tok), pl.BlockSpec((tm, D), tok)],
        out_shape=[jax.ShapeDtypeStruct((R, D), BF16), jax.ShapeDtypeStruct((R, D), BF16)],
        compiler_params=_params("arbitrary"),
        name="mem_kv",
    )(memf, wk, wv)


def _mem_attn_body(x0_ref, on_ref, om_ref, wmix_ref, g1_ref, b1_ref,
                   k_ref, v_ref, wq_ref, wo_ref, g_ref, b_ref, o_ref, *, alpha):
    half = on_ref.shape[1]
    mix = _dot(on_ref[...], wmix_ref[:half, :]) + _dot(om_ref[...], wmix_ref[half:, :])
    x = _layer_norm(alpha * x0_ref[...] + mix, g1_ref[...], b1_ref[...])
    D = x.shape[1]
    dh = D // MEM_HEADS
    q = (_dot(x.astype(BF16), wq_ref[...]) * (dh ** -0.5 * LOG2E)).astype(BF16)
    outs = []
    for h in range(MEM_HEADS):
        cs = slice(dh * h, dh * (h + 1))
        s = _dot_nt(q[:, cs], k_ref[0, :, cs])
        p = jnp.exp2(s - jnp.max(s, axis=1, keepdims=True))
        l = jnp.sum(p, axis=1, keepdims=True)
        outs.append((_dot(p.astype(BF16), v_ref[0, :, cs]) * (1.0 / l)).astype(BF16))
    o = jnp.concatenate(outs, axis=1)
    y = _dot(o, wo_ref[...])
    o_ref[...] = _layer_norm(alpha * x + y, g_ref[...], b_ref[...])


def _mem_attn(xf, o_nsa, o_mla, w_o, g1, b1, k_mem, v_mem, wq, wo, g, b, alpha, S, tm):
    T, D = xf.shape
    nst = S // tm
    M = k_mem.shape[1]
    tok = lambda i: (i, 0)
    const = lambda i: (0, 0)
    memb = lambda i: (i // nst, 0, 0)
    return pl.pallas_call(
        functools.partial(_mem_attn_body, alpha=alpha),
        grid=(T // tm,),
        in_specs=[
            pl.BlockSpec((tm, D), tok),
            pl.BlockSpec((tm, o_nsa.shape[1]), tok),
            pl.BlockSpec((tm, o_mla.shape[1]), tok),
            pl.BlockSpec(w_o.shape, const),
            pl.BlockSpec((1, D), const),
            pl.BlockSpec((1, D), const),
            pl.BlockSpec((1, M, D), memb),
            pl.BlockSpec((1, M, D), memb),
            pl.BlockSpec(wq.shape, const),
            pl.BlockSpec(wo.shape, const),
            pl.BlockSpec((1, D), const),
            pl.BlockSpec((1, D), const),
        ],
        out_specs=pl.BlockSpec((tm, D), tok),
        out_shape=jax.ShapeDtypeStruct((T, D), F32),
        compiler_params=_params("arbitrary"),
        name="mem_attn",
    )(xf, o_nsa, o_mla, w_o, g1, b1, k_mem, v_mem, wq, wo, g, b)


HALO = 8


FFN_SLAB = 256
FFN_TM = 512


def _ffn_body(x_ref, xh_ref, wg_ref, wu_ref, cw_ref, cb_ref, wd_ref, g_ref, b_ref, o_ref,
              act_ref, *, alpha, seq_tiles):
    i = pl.program_id(0)
    x = x_ref[...]
    xb = x.astype(BF16)
    xhb = xh_ref[...].astype(BF16)
    tm = x.shape[0]
    row = lax.broadcasted_iota(jnp.int32, (tm, 1), 0)
    seq_start = i % seq_tiles == 0
    for c0 in range(0, wg_ref.shape[1], FFN_SLAB):
        cs = slice(c0, c0 + FFN_SLAB)
        gate = _dot(xb, wg_ref[:, cs])
        up = _dot(xb, wu_ref[:, cs])
        halo = jnp.where(seq_start, 0.0, _dot(xhb, wg_ref[:, cs]))
        g1 = jnp.where(row == 0, halo[HALO - 1:HALO], pltpu.roll(gate, 1, 0))
        g2 = jnp.where(row == 0, halo[HALO - 2:HALO - 1],
                       jnp.where(row == 1, halo[HALO - 1:HALO], pltpu.roll(gate, 2, 0)))
        conv = cw_ref[0:1, cs] * g2 + cw_ref[1:2, cs] * g1 + cw_ref[2:3, cs] * gate + cb_ref[:, cs]
        act_ref[:, cs] = (conv * (1.0 / (1.0 + jnp.exp(-conv))) * up).astype(BF16)
    y = _dot(act_ref[...], wd_ref[...])
    o_ref[...] = _layer_norm(alpha * x + y, g_ref[...], b_ref[...])


def _ffn(xf, wg, wu, cw, cb, wd, g, b, alpha, S, tm):
    T, D = xf.shape
    dff = wg.shape[1]
    assert dff % FFN_SLAB == 0
    tok = lambda i: (i, 0)
    const = lambda i: (0, 0)
    return pl.pallas_call(
        functools.partial(_ffn_body, alpha=alpha, seq_tiles=S // tm),
        grid=(T // tm,),
        in_specs=[
            pl.BlockSpec((tm, D), tok),
            pl.BlockSpec((HALO, D), lambda i: (jnp.maximum(i * (tm // HALO) - 1, 0), 0)),
            pl.BlockSpec((D, dff), const),
            pl.BlockSpec((D, dff), const),
            pl.BlockSpec((CONV_WIDTH, dff), const),
            pl.BlockSpec((1, dff), const),
            pl.BlockSpec((dff, D), const),
            pl.BlockSpec((1, D), const),
            pl.BlockSpec((1, D), const),
        ],
        out_specs=pl.BlockSpec((tm, D), tok),
        out_shape=jax.ShapeDtypeStruct((T, D), F32),
        scratch_shapes=[pltpu.VMEM((tm, dff), BF16)],
        compiler_params=_params("arbitrary"),
        name="ffn",
    )(xf, xf, wg, wu, cw, cb, wd, g, b)


def _inv_freq_row(dim, lane_lo, lane_hi, period):
    inv = ROPE_THETA ** (-np.arange(0, dim, 2, dtype=np.float64) / dim)
    row = np.zeros((1, LANES), np.float32)
    for lane in range(lane_lo, lane_hi):
        row[0, lane] = inv[(lane % period) % (dim // 2)]
    return jnp.asarray(row)


def _cover_t(S):
    nc = S // CMP_STRIDE
    ns = S // SLC_LEN
    cs = np.arange(nc)[:, None] * CMP_STRIDE
    ss = np.arange(ns)[None, :] * SLC_LEN
    cover = np.clip(np.minimum(cs + CMP_LEN, ss + SLC_LEN) - np.maximum(cs, ss), 0, None) / CMP_LEN
    cover[nc - 1:] = 0.0
    return jnp.asarray(cover.T, dtype=BF16)


def _eye_aug():
    eye = np.zeros((NSA_ROWS, NSA_TQ), np.float32)
    r = np.arange(NSA_ROWS)
    eye[r, r % NSA_TQ] = 1.0
    return jnp.asarray(eye, dtype=BF16)


def _permute_w_in(w):
    D = w.shape[0]
    c1 = NSA_HEADS * NSA_DH
    c2 = c1 + 3 * 2 * NSA_GROUPS * NSA_DH
    c3 = c2 + 3 * NSA_HEADS
    c4 = c3 + MLA_Q_RANK
    c5 = c4 + MLA_KV_RANK
    c6 = c5 + MLA_ROPE
    half = MLA_ROPE // 2
    z = lambda n: jnp.zeros((D, n), w.dtype)
    misc = jnp.concatenate(
        [w[:, c5 + half:c6], z(GATE_LANE0 - half), w[:, c2:c3], z(MLA_PE1 - GATE_LANE0 - (c3 - c2)),
         w[:, c5:c5 + half], z(LANES - MLA_PE1 - half)], axis=1)
    return jnp.concatenate([w[:, :c2], w[:, c3:c5], misc], axis=1).astype(BF16)


def _mla_head_lanes(nope, pe):
    r, H, _ = nope.shape
    half = MLA_ROPE // 2
    split = MLA_PE1 - half
    pad = jnp.zeros((r, H, LANES - MLA_NOPE - MLA_ROPE), nope.dtype)
    return jnp.concatenate([pe[..., half:], nope[..., :split], pe[..., :half], nope[..., split:], pad],
                           axis=2).reshape(r, H * LANES)


def _permute_w_uq(w):
    r = w.shape[0]
    w3 = w.reshape(r, MLA_HEADS, MLA_NOPE + MLA_ROPE)
    return _mla_head_lanes(w3[..., :MLA_NOPE], w3[..., MLA_NOPE:]).astype(BF16)


def _permute_w_ukv(w):
    r = w.shape[0]
    w3 = w.reshape(r, MLA_HEADS, MLA_NOPE + MLA_V)
    k = _mla_head_lanes(w3[..., :MLA_NOPE], jnp.zeros((r, MLA_HEADS, MLA_ROPE), w.dtype))
    v = _pad_lanes(w3[..., MLA_NOPE:]).reshape(r, MLA_HEADS * LANES)
    return jnp.concatenate([k, v], axis=1).astype(BF16)


def _pad_lanes(a):
    return jnp.concatenate([a, jnp.zeros(a.shape[:-1] + (LANES - a.shape[-1],), a.dtype)], axis=-1)


def kernel(x, mem, positions, w_in, nsa_k_pos, nsa_ck_w1, nsa_ck_b1, nsa_ck_w2, nsa_ck_b2,
           nsa_v_pos, nsa_cv_w1, nsa_cv_b1, nsa_cv_w2, nsa_cv_b2,
           mla_q_norm, mla_w_uq, mla_kv_norm, mla_w_ukv, w_o, ln1_g, ln1_b,
           mem_wq, mem_wk, mem_wv, mem_wo, ln2_g, ln2_b,
           ffn_w_up, ffn_conv_w, ffn_conv_b, ffn_w_down, ln3_g, ln3_b):
    B, S, D = x.shape
    T = B * S
    depth = w_in.shape[0]
    alpha = (2.0 * depth) ** 0.25
    d_ff = ffn_w_down.shape[1]
    tm = min(512, S)
    assert S % MLA_TQ == 0 and S >= WIN_SPAN and S % tm == 0
    assert (B * mem.shape[1]) % 256 == 0

    pos = positions.reshape(T, 1)
    pos_cmp = positions[:, CMP_LEN - 1::CMP_STRIDE]
    pos_cmp = jnp.concatenate([pos_cmp, pos_cmp[:, -1:]], axis=1)[:, :, None]
    inv_cmp = _inv_freq_row(NSA_DH, 0, NSA_DH, NSA_DH)
    inv_tok = (_inv_freq_row(NSA_DH, 0, NSA_DH // 2, NSA_DH)
               + _inv_freq_row(MLA_ROPE, NSA_DH // 2, NSA_DH // 2 + MLA_ROPE // 2, MLA_ROPE // 2))
    cov_t = _cover_t(S)
    eye = _eye_aug()
    memf = mem.reshape(B * mem.shape[1], D)

    xf = x.reshape(T, D)
    for l in range(depth):
        qn, kvn, kvc, misc, q_m, k_m, v_m = _inproj(
            pos, inv_tok, xf, _permute_w_in(w_in[l]), mla_q_norm[l][None, :],
            mla_kv_norm[l][None, :], _permute_w_uq(mla_w_uq[l]), _permute_w_ukv(mla_w_ukv[l]),
            B, S, tm)
        kvcmp = _compress(
            pos_cmp, inv_cmp, kvc,
            *_compress_weights(nsa_k_pos[l], nsa_ck_w1[l], nsa_ck_b1[l], nsa_ck_w2[l], nsa_ck_b2[l],
                               nsa_v_pos[l], nsa_cv_w1[l], nsa_cv_b1[l], nsa_cv_w2[l], nsa_cv_b2[l]),
            B, S)
        o_nsa = _nsa(qn, kvn, kvcmp, misc, cov_t, eye, B, S)
        o_mla = _mla(q_m, k_m, v_m, B, S)
        k_mem, v_mem = _mem_kv(memf, mem_wk[l].astype(BF16), mem_wv[l].astype(BF16), 256)
        xf = _mem_attn(xf, o_nsa.reshape(T, -1), o_mla.reshape(T, -1), w_o[l].astype(BF16),
                       ln1_g[l][None, :], ln1_b[l][None, :],
                       k_mem.reshape(B, -1, D), v_mem.reshape(B, -1, D),
                       mem_wq[l].astype(BF16), mem_wo[l].astype(BF16),
                       ln2_g[l][None, :], ln2_b[l][None, :], alpha, S, tm)
        xf = _ffn(xf, ffn_w_up[l][:, :d_ff].astype(BF16), ffn_w_up[l][:, d_ff:].astype(BF16),
                  ffn_conv_w[l], ffn_conv_b[l][None, :], ffn_w_down[l].astype(BF16),
                  ln3_g[l][None, :], ln3_b[l][None, :], alpha, S, FFN_TM)
    return xf.reshape(B, S, D)
```

```python
import functools
import math

import numpy as np
import jax
import jax.numpy as jnp
from jax import lax
from jax.experimental import pallas as pl
from jax.experimental.pallas import tpu as pltpu

F32 = jnp.float32
BF16 = jnp.bfloat16

NSA_HEADS = 8
NSA_GROUPS = 2
NSA_REP = NSA_HEADS // NSA_GROUPS
NSA_DH = 64
CMP_STRIDE = 16
CMP_LEN = 32
SLC_LEN = 64
TOPN = 16
WINDOW = 512
CMP_HIDDEN = 128
FORCE_BONUS = 1e4
MLA_HEADS = 8
MLA_Q_RANK = 384
MLA_KV_RANK = 256
MLA_NOPE = 64
MLA_ROPE = 32
MLA_V = 64
MEM_HEADS = 4
CONV_WIDTH = 3
ROPE_THETA = 10000.0
LN_EPS = 1e-5
RMS_EPS = 1e-6
NEG_INF = -1e30
LOG2E = math.log2(math.e)

LANES = 128
VMEM_LIMIT = 56 * 1024 * 1024

C_Q = 0
C_KVC = 512
C_KVN = 768
C_LAT = 1280
C_MISC = 1920
IN_COLS_PAD = 2048
GATE_LANE0 = MLA_ROPE
MLA_PE2 = 0
MLA_PE1 = 64
ONES_LANE = 64


def _dot(a, b):
    return jnp.dot(a, b, preferred_element_type=F32)


def _dot_nt(a, b):
    return lax.dot_general(a, b, (((1,), (1,)), ((), ())), preferred_element_type=F32)


def _layer_norm(y, g, b):
    mu = jnp.mean(y, axis=-1, keepdims=True)
    d = y - mu
    var = jnp.mean(d * d, axis=-1, keepdims=True)
    return d * lax.rsqrt(var + LN_EPS) * g + b


def _params(*sem):
    return pltpu.CompilerParams(dimension_semantics=sem, vmem_limit_bytes=VMEM_LIMIT)


def _rope_tables(pos_col, inv_row, half):
    ang = pos_col * inv_row
    cos = jnp.cos(ang)
    sin = jnp.sin(ang)
    lane = lax.broadcasted_iota(jnp.int32, (1, LANES), 1)
    upper = (lane & (2 * half - 1)) >= half
    rot = inv_row != 0.0
    sin_hi = jnp.where(upper & rot, sin, 0.0)
    sin_lo = jnp.where(upper | (~rot), 0.0, -sin)
    return cos, sin_hi, sin_lo


def _apply_rope(v, tabs, half):
    cos, sin_hi, sin_lo = tabs
    return v * cos + pltpu.roll(v, half, 1) * sin_hi + pltpu.roll(v, LANES - half, 1) * sin_lo


def _rms_norm(v, g):
    return v * lax.rsqrt(jnp.mean(v * v, axis=-1, keepdims=True) + RMS_EPS) * g


def _inproj_body(pos_ref, inv_ref, x_ref, w_ref, gq_ref, gkv_ref, wq_ref, wkv_ref,
                 qn_ref, kvn_ref, kvc_ref, misc_ref, qm_ref, km_ref, vm_ref):
    xb = x_ref[...].astype(BF16)
    lane = lax.broadcasted_iota(jnp.int32, (1, LANES), 1)
    low = lane < NSA_DH
    qscale = NSA_DH ** -0.5 * LOG2E

    ang = pos_ref[...].astype(F32) * inv_ref[...]
    cos, sin = jnp.cos(ang), jnp.sin(ang)
    nf, mf = NSA_DH // 2, MLA_ROPE // 2

    def tile_nsa(t):
        t = jnp.where(lane < nf, t, 0.0)
        t = t + pltpu.roll(t, nf, 1)
        return t + pltpu.roll(t, 2 * nf, 1)

    def place_mla(t):
        t = jnp.where((lane >= nf) & (lane < nf + mf), t, 0.0)
        return pltpu.roll(t, MLA_PE1 - nf, 1) + pltpu.roll(t, LANES + MLA_PE2 - nf, 1)

    upper = (lane & (NSA_DH - 1)) >= nf
    sin_n = tile_nsa(sin)
    tabs = (tile_nsa(cos), jnp.where(upper, sin_n, 0.0), jnp.where(upper, 0.0, -sin_n))
    pe1 = (lane >= MLA_PE1) & (lane < MLA_PE1 + mf)
    pe2 = (lane >= MLA_PE2) & (lane < MLA_PE2 + mf)
    pe_lanes = pe1 | pe2
    sin_m = place_mla(sin)
    cos_m = jnp.where(pe_lanes, place_mla(cos), 1.0)
    sin_m = jnp.where(pe1, -sin_m, jnp.where(pe2, sin_m, 0.0))

    def rope_mla(v):
        return v * cos_m + pltpu.roll(v, LANES // 2, 1) * sin_m

    def proj(c0, n):
        return _dot(xb, w_ref[:, c0:c0 + n])

    def split_store(v, ref, idx_lo, idx_hi, pad=0.0):
        ref[0, idx_lo] = jnp.where(low, v, pad).astype(BF16)
        ref[0, idx_hi] = jnp.where(low, pltpu.roll(v, NSA_DH, 1), pad).astype(BF16)

    ones_pad = jnp.where(lane == ONES_LANE, 1.0, 0.0)

    for slab in range(2):
        h = proj(C_Q + 256 * slab, 256)
        for j in range(2):
            r = _apply_rope(h[:, LANES * j:LANES * (j + 1)], tabs, NSA_DH // 2) * qscale
            split_store(r, qn_ref, 4 * slab + 2 * j, 4 * slab + 2 * j + 1)

    h = proj(C_KVC, 256)
    kvc_ref[0] = h[:, :LANES]
    kvc_ref[1] = h[:, LANES:]

    for slab in range(2):
        h = proj(C_KVN + 256 * slab, 256)
        k = _apply_rope(h[:, :LANES], tabs, NSA_DH // 2)
        split_store(k, kvn_ref, 4 * slab, 4 * slab + 1)
        split_store(h[:, LANES:], kvn_ref, 4 * slab + 2, 4 * slab + 3, ones_pad)

    lat = [proj(C_LAT + 256 * i, 256) for i in range((IN_COLS_PAD - C_LAT) // 256)]
    misc = lat[2][:, LANES:]
    misc_ref[...] = misc
    mq = jnp.concatenate([lat[0], lat[1][:, :LANES]], axis=1)
    mkv = jnp.concatenate([lat[1][:, LANES:], lat[2][:, :LANES]], axis=1)
    mscale = (MLA_NOPE + MLA_ROPE) ** -0.5 * LOG2E
    qn = _rms_norm(mq, gq_ref[...]).astype(BF16)
    kvn = _rms_norm(mkv, gkv_ref[...]).astype(BF16)
    kpe = jnp.where(pe_lanes, rope_mla(misc), 0.0)
    kcols = MLA_HEADS * LANES
    for slab in range(MLA_HEADS // 2):
        cs = slice(256 * slab, 256 * (slab + 1))
        hq = _dot(qn, wq_ref[:, cs])
        hk = _dot(kvn, wkv_ref[:, cs])
        hv = _dot(kvn, wkv_ref[:, kcols + 256 * slab:kcols + 256 * (slab + 1)])
        for j in range(2):
            ls = slice(LANES * j, LANES * (j + 1))
            qm_ref[0, 2 * slab + j] = (rope_mla(hq[:, ls]) * mscale).astype(BF16)
            km_ref[0, 2 * slab + j] = (hk[:, ls] + kpe).astype(BF16)
            vm_ref[0, 2 * slab + j] = (hv[:, ls] + ones_pad).astype(BF16)


def _inproj(pos, inv_row, xf, w_in_p, gq, gkv, wq_p, wkv_p, B, S, tm):
    T = B * S
    nst = S // tm
    tok = lambda i: (i, 0)
    const = lambda i: (0, 0)
    head_blk = lambda i: (i // nst, 0, i % nst, 0)
    heads = jax.ShapeDtypeStruct((B, 8, S, LANES), BF16)
    return pl.pallas_call(
        _inproj_body,
        grid=(T // tm,),
        in_specs=[
            pl.BlockSpec((tm, 1), tok),
            pl.BlockSpec((1, LANES), const),
            pl.BlockSpec((tm, xf.shape[1]), tok),
            pl.BlockSpec(w_in_p.shape, const),
            pl.BlockSpec(gq.shape, const),
            pl.BlockSpec(gkv.shape, const),
            pl.BlockSpec(wq_p.shape, const),
            pl.BlockSpec(wkv_p.shape, const),
        ],
        out_specs=[
            pl.BlockSpec((1, 8, tm, LANES), head_blk),
            pl.BlockSpec((1, 8, tm, LANES), head_blk),
            pl.BlockSpec((2, tm, LANES), lambda i: (0, i, 0)),
            pl.BlockSpec((tm, LANES), tok),
            pl.BlockSpec((1, 8, tm, LANES), head_blk),
            pl.BlockSpec((1, 8, tm, LANES), head_blk),
            pl.BlockSpec((1, 8, tm, LANES), head_blk),
        ],
        out_shape=[
            heads,
            heads,
            jax.ShapeDtypeStruct((2, T, LANES), F32),
            jax.ShapeDtypeStruct((T, LANES), F32),
            heads,
            heads,
            heads,
        ],
        compiler_params=_params("arbitrary"),
        name="inproj",
    )(pos, inv_row, xf, w_in_p, gq, gkv, wq_p, wkv_p)


def _compress_body(pos_ref, inv_ref, x_ref, pe_ref, w1_ref, b1_ref, w2_ref, b2_ref, o_ref):
    is_k = pl.program_id(0) == 0
    nch = o_ref.shape[2]
    a1 = jnp.zeros((nch, NSA_GROUPS * CMP_HIDDEN), F32)
    a2 = jnp.zeros((nch, NSA_GROUPS * CMP_HIDDEN), F32)
    for l in range(CMP_STRIDE):
        xl = x_ref.at[0, 0][pl.ds(l, nch, stride=CMP_STRIDE), :]
        a1 = a1 + _dot((xl + pe_ref[0, l:l + 1, :]).astype(BF16), w1_ref[0, l])
        a2 = a2 + _dot((xl + pe_ref[0, CMP_STRIDE + l:CMP_STRIDE + l + 1, :]).astype(BF16),
                       w1_ref[0, CMP_STRIDE + l])
    pre = a1 + pltpu.roll(a2, nch - 1, 0) + b1_ref[0]
    hid = jax.nn.gelu(pre, approximate=True)
    out = _dot(hid.astype(BF16), w2_ref[0]) + b2_ref[0]
    tabs = _rope_tables(pos_ref[0].astype(F32), inv_ref[...], NSA_DH // 2)
    row = lax.broadcasted_iota(jnp.int32, (nch, 1), 0)
    for g in range(NSA_GROUPS):
        og = out[:, LANES * g:LANES * (g + 1)]
        og = jnp.where(is_k, _apply_rope(og, tabs, NSA_DH // 2), og)
        o_ref[g, 0] = jnp.where(row < nch - 1, og, 0.0).astype(BF16)


def _compress(pos_cmp, inv_nsa, kvc, pe, w1, b1, w2, b2, B, S):
    nch = S // CMP_STRIDE
    x = kvc.reshape(2, B, S, LANES)
    kv = lambda j, b: (j, 0, 0)
    kv4 = lambda j, b: (j, 0, 0, 0)
    return pl.pallas_call(
        _compress_body,
        grid=(2, B),
        in_specs=[
            pl.BlockSpec((1, nch, 1), lambda j, b: (b, 0, 0)),
            pl.BlockSpec((1, LANES), lambda j, b: (0, 0)),
            pl.BlockSpec((1, 1, S, LANES), lambda j, b: (j, b, 0, 0)),
            pl.BlockSpec((1,) + pe.shape[1:], kv),
            pl.BlockSpec((1,) + w1.shape[1:], kv4),
            pl.BlockSpec((1,) + b1.shape[1:], kv),
            pl.BlockSpec((1,) + w2.shape[1:], kv),
            pl.BlockSpec((1,) + b2.shape[1:], kv),
        ],
        out_specs=pl.BlockSpec((NSA_GROUPS, 1, nch, LANES), lambda j, b: (j, b, 0, 0)),
        out_shape=jax.ShapeDtypeStruct((2 * NSA_GROUPS, B, nch, LANES), BF16),
        compiler_params=_params("arbitrary", "arbitrary"),
        name="compress",
    )(pos_cmp, inv_nsa, x, pe, w1, b1, w2, b2)


def _compress_weights(k_pos, k_w1, k_b1, k_w2, k_b2, v_pos, v_w1, v_b1, v_w2, v_b2):
    def one(pos, w1, b1, w2, b2):
        w1l = w1.reshape(CMP_LEN, NSA_DH, CMP_HIDDEN)
        z1 = jnp.zeros_like(w1l)
        w1bd = jnp.concatenate([jnp.concatenate([w1l, z1], axis=2),
                                jnp.concatenate([z1, w1l], axis=2)], axis=1)
        w2p = _pad_lanes(w2)
        z2 = jnp.zeros_like(w2p)
        w2bd = jnp.concatenate([jnp.concatenate([w2p, z2], axis=1),
                                jnp.concatenate([z2, w2p], axis=1)], axis=0)
        return (jnp.tile(pos, (1, NSA_GROUPS)), w1bd.astype(BF16), jnp.tile(b1, NSA_GROUPS)[None, :],
                w2bd.astype(BF16), jnp.tile(_pad_lanes(b2), NSA_GROUPS)[None, :])
    k = one(k_pos, k_w1, k_b1, k_w2, k_b2)
    v = one(v_pos, v_w1, v_b1, v_w2, v_b2)
    return tuple(jnp.stack([a, b]) for a, b in zip(k, v))


STRIP = 64


def _lane_tile(col, n):
    reps = [col] * (n // LANES)
    if n % LANES:
        reps.append(col[:, :n % LANES])
    return reps[0] if len(reps) == 1 else jnp.concatenate(reps, axis=1)


def _flash_reset(m_ref, acc_ref):
    m_ref[...] = jnp.full(m_ref.shape, NEG_INF, F32)
    acc_ref[...] = jnp.zeros(acc_ref.shape, F32)


def _flash_update(s_ref, v, m_ref, acc_ref, p_ref, mask=None):
    rows, n = s_ref.shape
    for r in range(rows // STRIP):
        rs = slice(STRIP * r, STRIP * (r + 1))
        s = s_ref[rs, :]
        if mask is not None:
            s = jnp.where(mask[rs], s, NEG_INF)
        if m_ref is None:
            p_ref[rs, :] = jnp.exp2(s - jnp.max(s, axis=1, keepdims=True)).astype(BF16)
            continue
        m_old = m_ref[rs, :]
        m_new = jnp.maximum(m_old, jnp.max(s, axis=1, keepdims=True))
        p_ref[rs, :] = jnp.exp2(s - _lane_tile(m_new, n)).astype(BF16)
        acc_ref[rs, :] = jnp.exp2(m_old - m_new) * acc_ref[rs, :]
        m_ref[rs, :] = m_new
    if m_ref is None:
        acc_ref[...] = _dot(p_ref[...], v)
    else:
        acc_ref[...] += _dot(p_ref[...], v)


def _flash_finish(acc):
    return acc * (1.0 / acc[:, ONES_LANE:ONES_LANE + 1])


NSA_TQ = 128
NSA_ROWS = NSA_REP * NSA_TQ
SLC_CHUNK = 512
WIN_SPAN = WINDOW + NSA_TQ
BIAS_LANE0 = LANES


def _nsa_body(q_ref, kvn_ref, kvc_ref, misc_ref, cov_ref, eye_ref, o_ref,
              kaug_ref, score_ref, bias_ref, sa_ref, sb_ref, pa_ref, pb_ref, ms_ref, accs_ref,
              sw_ref, pw_ref, accw_ref, sc_ref, pn_ref, pc_ref, oc_ref):
    c = pl.program_id(1)
    rows = NSA_ROWS
    t_row = c * NSA_TQ + (lax.broadcasted_iota(jnp.int32, (rows, 1), 0) & (NSA_TQ - 1))
    ncmp = kvc_ref.shape[2]
    nblk = kaug_ref.shape[1] // SLC_LEN

    @pl.when(c == 0)
    def _():
        for g in range(NSA_GROUPS):
            kaug_ref[g, :, :BIAS_LANE0] = kvn_ref[0, g]
            kaug_ref[g, :, BIAS_LANE0:] = jnp.zeros((kaug_ref.shape[1], NSA_TQ), BF16)

    qs = [q_ref[0, NSA_REP * g:NSA_REP * (g + 1)].reshape(rows, LANES) for g in range(NSA_GROUPS)]

    cmp_valid = (CMP_STRIDE * lax.broadcasted_iota(jnp.int32, (1, ncmp), 1) + CMP_LEN - 1) <= t_row
    for g in range(NSA_GROUPS):
        sc_ref[g] = _dot_nt(qs[g], kvc_ref[g, 0])
    for g in range(NSA_GROUPS):
        for r in range(rows // STRIP):
            rs = slice(STRIP * r, STRIP * (r + 1))
            s = jnp.where(cmp_valid[rs], sc_ref[g, rs, :], NEG_INF)
            p = jnp.where(cmp_valid[rs], jnp.exp2(s - jnp.max(s, axis=1, keepdims=True)), 0.0)
            l = jnp.sum(p, axis=1, keepdims=True)
            p = p * jnp.where(l > 0.0, 1.0 / l, 0.0)
            pn_ref[g, rs, :] = p
            pc_ref[g, rs, :] = p.astype(BF16)
        oc_ref[g] = _dot(pc_ref[g], kvc_ref[NSA_GROUPS + g, 0])

    @pl.when((c + 1) * (NSA_TQ // SLC_LEN) > TOPN)
    def _():
        width = NSA_GROUPS * NSA_TQ
        ps = jnp.concatenate(
            [sum(pn_ref[g, NSA_TQ * r:NSA_TQ * (r + 1), :] for r in range(NSA_REP))
             for g in range(NSA_GROUPS)], axis=0)
        hi = ps.astype(BF16)
        lo = (ps - hi.astype(F32)).astype(BF16)
        imp = _dot_nt(cov_ref[...], hi) + _dot_nt(cov_ref[...], lo)
        jidx = lax.broadcasted_iota(jnp.int32, (nblk, width), 0)
        lane_q = lax.broadcasted_iota(jnp.int32, (1, width), 1) & (NSA_TQ - 1)
        cur = c * (NSA_TQ // SLC_LEN) + lane_q // SLC_LEN
        forced = (jidx == 0) | (jidx == cur) | (jidx == cur - 1)
        score = jnp.where(jidx <= cur, jnp.where(forced, FORCE_BONUS, imp), NEG_INF)
        score_ref[...] = score
        sub = 8
        cnt = [jnp.zeros((sub, width), F32) for _ in range(nblk // sub)]
        tiles = [score[sub * v:sub * (v + 1)] for v in range(nblk // sub)]
        sidx = lax.broadcasted_iota(jnp.int32, (sub, width), 0)
        for jp in range(nblk):
            rowv = jnp.broadcast_to(score_ref[jp:jp + 1, :], (sub, width))
            for v in range(nblk // sub):
                if sub * v > jp:
                    cnt[v] = jnp.where(rowv >= tiles[v], cnt[v] + 1.0, cnt[v])
                elif sub * v + sub - 1 <= jp:
                    cnt[v] = jnp.where(rowv > tiles[v], cnt[v] + 1.0, cnt[v])
                else:
                    ge = jnp.where(rowv >= tiles[v], cnt[v] + 1.0, cnt[v])
                    gt = jnp.where(rowv > tiles[v], cnt[v] + 1.0, cnt[v])
                    cnt[v] = jnp.where(sidx + sub * v > jp, ge, gt)
        rank = jnp.concatenate(cnt, axis=0)
        bias = jnp.where(rank < float(TOPN), 0.0, NEG_INF)
        for g in range(NSA_GROUPS):
            bias_ref[g] = bias[:, NSA_TQ * g:NSA_TQ * (g + 1)]

        def write_bias(j, carry):
            r0 = pl.multiple_of(j * SLC_LEN, SLC_LEN)
            for g in range(NSA_GROUPS):
                blk = jnp.broadcast_to(bias_ref[g, pl.ds(j, 1), :], (SLC_LEN, NSA_TQ))
                kaug_ref[g, pl.ds(r0, SLC_LEN), BIAS_LANE0:] = blk.astype(BF16)
            return carry
        lax.fori_loop(0, (c + 1) * (NSA_TQ // SLC_LEN), write_bias, 0)

    _flash_reset(ms_ref, accs_ref)
    qas = [jnp.concatenate([q, eye_ref[...]], axis=1) for q in qs]
    last = c // (SLC_CHUNK // NSA_TQ)

    def slc_scores(kc, dst):
        k0 = pl.multiple_of(kc * SLC_CHUNK, SLC_CHUNK)
        for g in range(NSA_GROUPS):
            dst[g] = _dot_nt(qas[g], kaug_ref[g, pl.ds(k0, SLC_CHUNK), :])

    def slc_update(src, p_ref, kc, causal):
        k0 = pl.multiple_of(kc * SLC_CHUNK, SLC_CHUNK)
        mask = None
        if causal:
            mask = (k0 + lax.broadcasted_iota(jnp.int32, (1, SLC_CHUNK), 1)) <= t_row
        for g in range(NSA_GROUPS):
            _flash_update(src.at[g], kvn_ref[0, 2 + g, pl.ds(k0, SLC_CHUNK), :],
                          ms_ref.at[g], accs_ref.at[g], p_ref.at[g], mask)

    slc_scores(0, sa_ref)

    win_start = pl.multiple_of(jnp.maximum(c * NSA_TQ - WINDOW, 0), NSA_TQ)
    diff = t_row - (win_start + lax.broadcasted_iota(jnp.int32, (1, WIN_SPAN), 1))
    win_valid = (diff >= 0) & (diff < WINDOW)
    for g in range(NSA_GROUPS):
        sw_ref[g] = _dot_nt(qs[g], kvn_ref[0, 4 + g, pl.ds(win_start, WIN_SPAN), :])
    for g in range(NSA_GROUPS):
        _flash_update(sw_ref.at[g], kvn_ref[0, 6 + g, pl.ds(win_start, WIN_SPAN), :],
                      None, accw_ref.at[g], pw_ref.at[g], win_valid)

    def slc_pair(i, carry):
        slc_scores(2 * i + 1, sb_ref)
        slc_update(sa_ref, pa_ref, 2 * i, False)
        slc_scores(2 * i + 2, sa_ref)
        slc_update(sb_ref, pb_ref, 2 * i + 1, False)
        return carry
    lax.fori_loop(0, last // 2, slc_pair, 0)
    tail = 2 * (last // 2)

    @pl.when(last > tail)
    def _():
        slc_scores(tail + 1, sb_ref)
        slc_update(sa_ref, pa_ref, tail, False)
        slc_update(sb_ref, pb_ref, tail + 1, True)

    @pl.when(last == tail)
    def _():
        slc_update(sa_ref, pa_ref, tail, True)

    sig = 1.0 / (1.0 + jnp.exp(-misc_ref[...]))
    outs = []
    for g in range(NSA_GROUPS):
        for r in range(NSA_REP):
            lane0 = GATE_LANE0 + 3 * (NSA_REP * g + r)
            rs = slice(NSA_TQ * r, NSA_TQ * (r + 1))
            o = (sig[:, lane0:lane0 + 1] * oc_ref[g, rs, :]
                 + sig[:, lane0 + 1:lane0 + 2] * _flash_finish(accs_ref[g, rs, :])
                 + sig[:, lane0 + 2:lane0 + 3] * _flash_finish(accw_ref[g, rs, :]))
            outs.append(o[:, :NSA_DH])
    o_ref[0] = jnp.concatenate(outs, axis=1).astype(BF16)


def _nsa(qn, kvn, kvcmp, misc, cov_t, eye, B, S):
    nblk = S // SLC_LEN
    ncmp = S // CMP_STRIDE
    nq = S // NSA_TQ
    rows = NSA_ROWS
    return pl.pallas_call(
        _nsa_body,
        grid=(B, nq),
        in_specs=[
            pl.BlockSpec((1, NSA_HEADS, NSA_TQ, LANES), lambda b, c: (b, 0, c, 0)),
            pl.BlockSpec((1, 8, S, LANES), lambda b, c: (b, 0, 0, 0)),
            pl.BlockSpec((4, 1, ncmp, LANES), lambda b, c: (0, b, 0, 0)),
            pl.BlockSpec((NSA_TQ, LANES), lambda b, c: (b * nq + c, 0)),
            pl.BlockSpec(cov_t.shape, lambda b, c: (0, 0)),
            pl.BlockSpec(eye.shape, lambda b, c: (0, 0)),
        ],
        out_specs=pl.BlockSpec((1, NSA_TQ, NSA_HEADS * NSA_DH), lambda b, c: (b, c, 0)),
        out_shape=jax.ShapeDtypeStruct((B, S, NSA_HEADS * NSA_DH), BF16),
        scratch_shapes=[
            pltpu.VMEM((NSA_GROUPS, S, BIAS_LANE0 + NSA_TQ), BF16),
            pltpu.VMEM((nblk, NSA_GROUPS * NSA_TQ), F32),
            pltpu.VMEM((NSA_GROUPS, nblk, NSA_TQ), F32),
            pltpu.VMEM((NSA_GROUPS, rows, SLC_CHUNK), F32),
            pltpu.VMEM((NSA_GROUPS, rows, SLC_CHUNK), F32),
            pltpu.VMEM((NSA_GROUPS, rows, SLC_CHUNK), BF16),
            pltpu.VMEM((NSA_GROUPS, rows, SLC_CHUNK), BF16),
            pltpu.VMEM((NSA_GROUPS, rows, LANES), F32),
            pltpu.VMEM((NSA_GROUPS, rows, LANES), F32),
            pltpu.VMEM((NSA_GROUPS, rows, WIN_SPAN), F32),
            pltpu.VMEM((NSA_GROUPS, rows, WIN_SPAN), BF16),
            pltpu.VMEM((NSA_GROUPS, rows, LANES), F32),
            pltpu.VMEM((NSA_GROUPS, rows, ncmp), F32),
            pltpu.VMEM((NSA_GROUPS, rows, ncmp), F32),
            pltpu.VMEM((NSA_GROUPS, rows, ncmp), BF16),
            pltpu.VMEM((NSA_GROUPS, rows, LANES), F32),
        ],
        compiler_params=_params("arbitrary", "arbitrary"),
        name="nsa",
    )(qn, kvn, kvcmp, misc, cov_t, eye)


MLA_TQ = 512
MLA_CHUNK = 512
MLA_HPB = 4


def _mla_body(q_ref, k_ref, v_ref, o_ref, sa_ref, sb_ref, pa_ref, pb_ref, m_ref, acc_ref):
    qi = pl.program_id(2)
    t_row = qi * MLA_TQ + lax.broadcasted_iota(jnp.int32, (MLA_TQ, 1), 0)
    _flash_reset(m_ref, acc_ref)

    def scores(kc, dst):
        k0 = pl.multiple_of(kc * MLA_CHUNK, MLA_CHUNK)
        for j in range(MLA_HPB):
            dst[j] = _dot_nt(q_ref[0, j], k_ref[0, j, pl.ds(k0, MLA_CHUNK), :])

    def update(src, p_ref, kc, causal):
        k0 = pl.multiple_of(kc * MLA_CHUNK, MLA_CHUNK)
        mask = None
        if causal:
            mask = (k0 + lax.broadcasted_iota(jnp.int32, (1, MLA_CHUNK), 1)) <= t_row
        for j in range(MLA_HPB):
            _flash_update(src.at[j], v_ref[0, j, pl.ds(k0, MLA_CHUNK), :],
                          m_ref.at[j], acc_ref.at[j], p_ref.at[j], mask)

    scores(0, sa_ref)

    def pair(i, carry):
        scores(2 * i + 1, sb_ref)
        update(sa_ref, pa_ref, 2 * i, False)
        scores(2 * i + 2, sa_ref)
        update(sb_ref, pb_ref, 2 * i + 1, False)
        return carry
    lax.fori_loop(0, qi // 2, pair, 0)
    tail = 2 * (qi // 2)

    @pl.when(qi > tail)
    def _():
        scores(tail + 1, sb_ref)
        update(sa_ref, pa_ref, tail, False)
        update(sb_ref, pb_ref, tail + 1, True)

    @pl.when(qi == tail)
    def _():
        update(sa_ref, pa_ref, tail, True)
    o_ref[0] = jnp.concatenate([_flash_finish(acc_ref[j])[:, :MLA_V] for j in range(MLA_HPB)],
                               axis=1).astype(BF16)


def _mla(q, k, v, B, S):
    return pl.pallas_call(
        _mla_body,
        grid=(B, MLA_HEADS // MLA_HPB, S // MLA_TQ),
        in_specs=[
            pl.BlockSpec((1, MLA_HPB, MLA_TQ, LANES), lambda b, h, i: (b, h, i, 0)),
            pl.BlockSpec((1, MLA_HPB, S, LANES), lambda b, h, i: (b, h, 0, 0)),
            pl.BlockSpec((1, MLA_HPB, S, LANES), lambda b, h, i: (b, h, 0, 0)),
        ],
        out_specs=pl.BlockSpec((1, MLA_TQ, MLA_HPB * MLA_V), lambda b, h, i: (b, i, h)),
        out_shape=jax.ShapeDtypeStruct((B, S, MLA_HEADS * MLA_V), BF16),
        scratch_shapes=[
            pltpu.VMEM((MLA_HPB, MLA_TQ, MLA_CHUNK), F32),
            pltpu.VMEM((MLA_HPB, MLA_TQ, MLA_CHUNK), F32),
            pltpu.VMEM((MLA_HPB, MLA_TQ, MLA_CHUNK), BF16),
            pltpu.VMEM((MLA_HPB, MLA_TQ, MLA_CHUNK), BF16),
            pltpu.VMEM((MLA_HPB, MLA_TQ, LANES), F32),
            pltpu.VMEM((MLA_HPB, MLA_TQ, LANES), F32),
        ],
        compiler_params=_params("arbitrary", "arbitrary", "arbitrary"),
        name="mla",
    )(q, k, v)


def _mem_kv_body(m_ref, wk_ref, wv_ref, k_ref, v_ref):
    mb = m_ref[...].astype(BF16)
    k_ref[...] = _dot(mb, wk_ref[...]).astype(BF16)
    v_ref[...] = _dot(mb, wv_ref[...]).astype(BF16)


def _mem_kv(memf, wk, wv, tm):
    R, D = memf.shape
    tok = lambda i: (i, 0)
    const = lambda i: (0, 0)
    return pl.pallas_call(
        _mem_kv_body,
        grid=(R // tm,),
        in_specs=[pl.BlockSpec((tm, D), tok), pl.BlockSpec(wk.shape, const),
                  pl.BlockSpec(wv.shape, const)],
        out_specs=[pl.BlockSpec((tm, D), tok), pl.BlockSpec((tm, D), tok)],
        out_shape=[jax.ShapeDtypeStruct((R, D), BF16), jax.ShapeDtypeStruct((R, D), BF16)],
        compiler_params=_params("arbitrary"),
        name="mem_kv",
    )(memf, wk, wv)


def _mem_attn_body(x0_ref, on_ref, om_ref, wmix_ref, g1_ref, b1_ref,
                   k_ref, v_ref, wq_ref, wo_ref, g_ref, b_ref, o_ref, *, alpha):
    half = on_ref.shape[1]
    mix = _dot(on_ref[...], wmix_ref[:half, :]) + _dot(om_ref[...], wmix_ref[half:, :])
    x = _layer_norm(alpha * x0_ref[...] + mix, g1_ref[...], b1_ref[...])
    D = x.shape[1]
    dh = D // MEM_HEADS
    q = (_dot(x.astype(BF16), wq_ref[...]) * (dh ** -0.5 * LOG2E)).astype(BF16)
    outs = []
    for h in range(MEM_HEADS):
        cs = slice(dh * h, dh * (h + 1))
        s = _dot_nt(q[:, cs], k_ref[0, :, cs])
        p = jnp.exp2(s - jnp.max(s, axis=1, keepdims=True))
        l = jnp.sum(p, axis=1, keepdims=True)
        outs.append((_dot(p.astype(BF16), v_ref[0, :, cs]) * (1.0 / l)).astype(BF16))
    o = jnp.concatenate(outs, axis=1)
    y = _dot(o, wo_ref[...])
    o_ref[...] = _layer_norm(alpha * x + y, g_ref[...], b_ref[...])


def _mem_attn(xf, o_nsa, o_mla, w_o, g1, b1, k_mem, v_mem, wq, wo, g, b, alpha, S, tm):
    T, D = xf.shape
    nst = S // tm
    M = k_mem.shape[1]
    tok = lambda i: (i, 0)
    const = lambda i: (0, 0)
    memb = lambda i: (i // nst, 0, 0)
    return pl.pallas_call(
        functools.partial(_mem_attn_body, alpha=alpha),
        grid=(T // tm,),
        in_specs=[
            pl.BlockSpec((tm, D), tok),
            pl.BlockSpec((tm, o_nsa.shape[1]), tok),
            pl.BlockSpec((tm, o_mla.shape[1]), tok),
            pl.BlockSpec(w_o.shape, const),
            pl.BlockSpec((1, D), const),
            pl.BlockSpec((1, D), const),
            pl.BlockSpec((1, M, D), memb),
            pl.BlockSpec((1, M, D), memb),
            pl.BlockSpec(wq.shape, const),
            pl.BlockSpec(wo.shape, const),
            pl.BlockSpec((1, D), const),
            pl.BlockSpec((1, D), const),
        ],
        out_specs=pl.BlockSpec((tm, D), tok),
        out_shape=jax.ShapeDtypeStruct((T, D), F32),
        compiler_params=_params("arbitrary"),
        name="mem_attn",
    )(xf, o_nsa, o_mla, w_o, g1, b1, k_mem, v_mem, wq, wo, g, b)


HALO = 8


FFN_SLAB = 256
FFN_TM = 512


def _ffn_body(x_ref, xh_ref, wg_ref, wu_ref, cw_ref, cb_ref, wd_ref, g_ref, b_ref, o_ref,
              act_ref, *, alpha, seq_tiles):
    i = pl.program_id(0)
    x = x_ref[...]
    xb = x.astype(BF16)
    xhb = xh_ref[...].astype(BF16)
    tm = x.shape[0]
    row = lax.broadcasted_iota(jnp.int32, (tm, 1), 0)
    seq_start = i % seq_tiles == 0
    for c0 in range(0, wg_ref.shape[1], FFN_SLAB):
        cs = slice(c0, c0 + FFN_SLAB)
        gate = _dot(xb, wg_ref[:, cs])
        up = _dot(xb, wu_ref[:, cs])
        halo = jnp.where(seq_start, 0.0, _dot(xhb, wg_ref[:, cs]))
        g1 = jnp.where(row == 0, halo[HALO - 1:HALO], pltpu.roll(gate, 1, 0))
        g2 = jnp.where(row == 0, halo[HALO - 2:HALO - 1],
                       jnp.where(row == 1, halo[HALO - 1:HALO], pltpu.roll(gate, 2, 0)))
        conv = cw_ref[0:1, cs] * g2 + cw_ref[1:2, cs] * g1 + cw_ref[2:3, cs] * gate + cb_ref[:, cs]
        act_ref[:, cs] = (conv * (1.0 / (1.0 + jnp.exp(-conv))) * up).astype(BF16)
    y = _dot(act_ref[...], wd_ref[...])
    o_ref[...] = _layer_norm(alpha * x + y, g_ref[...], b_ref[...])


def _ffn(xf, wg, wu, cw, cb, wd, g, b, alpha, S, tm):
    T, D = xf.shape
    dff = wg.shape[1]
    assert dff % FFN_SLAB == 0
    tok = lambda i: (i, 0)
    const = lambda i: (0, 0)
    return pl.pallas_call(
        functools.partial(_ffn_body, alpha=alpha, seq_tiles=S // tm),
        grid=(T // tm,),
        in_specs=[
            pl.BlockSpec((tm, D), tok),
            pl.BlockSpec((HALO, D), lambda i: (jnp.maximum(i * (tm // HALO) - 1, 0), 0)),
            pl.BlockSpec((D, dff), const),
            pl.BlockSpec((D, dff), const),
            pl.BlockSpec((CONV_WIDTH, dff), const),
            pl.BlockSpec((1, dff), const),
            pl.BlockSpec((dff, D), const),
            pl.BlockSpec((1, D), const),
            pl.BlockSpec((1, D), const),
        ],
        out_specs=pl.BlockSpec((tm, D), tok),
        out_shape=jax.ShapeDtypeStruct((T, D), F32),
        scratch_shapes=[pltpu.VMEM((tm, dff), BF16)],
        compiler_params=_params("arbitrary"),
        name="ffn",
    )(xf, xf, wg, wu, cw, cb, wd, g, b)


def _inv_freq_row(dim, lane_lo, lane_hi, period):
    inv = ROPE_THETA ** (-np.arange(0, dim, 2, dtype=np.float64) / dim)
    row = np.zeros((1, LANES), np.float32)
    for lane in range(lane_lo, lane_hi):
        row[0, lane] = inv[(lane % period) % (dim // 2)]
    return jnp.asarray(row)


def _cover_t(S):
    nc = S // CMP_STRIDE
    ns = S // SLC_LEN
    cs = np.arange(nc)[:, None] * CMP_STRIDE
    ss = np.arange(ns)[None, :] * SLC_LEN
    cover = np.clip(np.minimum(cs + CMP_LEN, ss + SLC_LEN) - np.maximum(cs, ss), 0, None) / CMP_LEN
    cover[nc - 1:] = 0.0
    return jnp.asarray(cover.T, dtype=BF16)


def _eye_aug():
    eye = np.zeros((NSA_ROWS, NSA_TQ), np.float32)
    r = np.arange(NSA_ROWS)
    eye[r, r % NSA_TQ] = 1.0
    return jnp.asarray(eye, dtype=BF16)


def _permute_w_in(w):
    D = w.shape[0]
    c1 = NSA_HEADS * NSA_DH
    c2 = c1 + 3 * 2 * NSA_GROUPS * NSA_DH
    c3 = c2 + 3 * NSA_HEADS
    c4 = c3 + MLA_Q_RANK
    c5 = c4 + MLA_KV_RANK
    c6 = c5 + MLA_ROPE
    half = MLA_ROPE // 2
    z = lambda n: jnp.zeros((D, n), w.dtype)
    misc = jnp.concatenate(
        [w[:, c5 + half:c6], z(GATE_LANE0 - half), w[:, c2:c3], z(MLA_PE1 - GATE_LANE0 - (c3 - c2)),
         w[:, c5:c5 + half], z(LANES - MLA_PE1 - half)], axis=1)
    return jnp.concatenate([w[:, :c2], w[:, c3:c5], misc], axis=1).astype(BF16)


def _mla_head_lanes(nope, pe):
    r, H, _ = nope.shape
    half = MLA_ROPE // 2
    split = MLA_PE1 - half
    pad = jnp.zeros((r, H, LANES - MLA_NOPE - MLA_ROPE), nope.dtype)
    return jnp.concatenate([pe[..., half:], nope[..., :split], pe[..., :half], nope[..., split:], pad],
                           axis=2).reshape(r, H * LANES)


def _permute_w_uq(w):
    r = w.shape[0]
    w3 = w.reshape(r, MLA_HEADS, MLA_NOPE + MLA_ROPE)
    return _mla_head_lanes(w3[..., :MLA_NOPE], w3[..., MLA_NOPE:]).astype(BF16)


def _permute_w_ukv(w):
    r = w.shape[0]
    w3 = w.reshape(r, MLA_HEADS, MLA_NOPE + MLA_V)
    k = _mla_head_lanes(w3[..., :MLA_NOPE], jnp.zeros((r, MLA_HEADS, MLA_ROPE), w.dtype))
    v = _pad_lanes(w3[..., MLA_NOPE:]).reshape(r, MLA_HEADS * LANES)
    return jnp.concatenate([k, v], axis=1).astype(BF16)


def _pad_lanes(a):
    return jnp.concatenate([a, jnp.zeros(a.shape[:-1] + (LANES - a.shape[-1],), a.dtype)], axis=-1)


def kernel(x, mem, positions, w_in, nsa_k_pos, nsa_ck_w1, nsa_ck_b1, nsa_ck_w2, nsa_ck_b2,
           nsa_v_pos, nsa_cv_w1, nsa_cv_b1, nsa_cv_w2, nsa_cv_b2,
           mla_q_norm, mla_w_uq, mla_kv_norm, mla_w_ukv, w_o, ln1_g, ln1_b,
           mem_wq, mem_wk, mem_wv, mem_wo, ln2_g, ln2_b,
           ffn_w_up, ffn_conv_w, ffn_conv_b, ffn_w_down, ln3_g, ln3_b):
    B, S, D = x.shape
    T = B * S
    depth = w_in.shape[0]
    alpha = (2.0 * depth) ** 0.25
    d_ff = ffn_w_down.shape[1]
    tm = min(512, S)
    assert S % MLA_TQ == 0 and S >= WIN_SPAN and S % tm == 0
    assert (B * mem.shape[1]) % 256 == 0

    pos = positions.reshape(T, 1)
    pos_cmp = positions[:, CMP_LEN - 1::CMP_STRIDE]
    pos_cmp = jnp.concatenate([pos_cmp, pos_cmp[:, -1:]], axis=1)[:, :, None]
    inv_cmp = _inv_freq_row(NSA_DH, 0, NSA_DH, NSA_DH)
    inv_tok = (_inv_freq_row(NSA_DH, 0, NSA_DH // 2, NSA_DH)
               + _inv_freq_row(MLA_ROPE, NSA_DH // 2, NSA_DH // 2 + MLA_ROPE // 2, MLA_ROPE // 2))
    cov_t = _cover_t(S)
    eye = _eye_aug()
    memf = mem.reshape(B * mem.shape[1], D)

    xf = x.reshape(T, D)
    for l in range(depth):
        qn, kvn, kvc, misc, q_m, k_m, v_m = _inproj(
            pos, inv_tok, xf, _permute_w_in(w_in[l]), mla_q_norm[l][None, :],
            mla_kv_norm[l][None, :], _permute_w_uq(mla_w_uq[l]), _permute_w_ukv(mla_w_ukv[l]),
            B, S, tm)
        kvcmp = _compress(
            pos_cmp, inv_cmp, kvc,
            *_compress_weights(nsa_k_pos[l], nsa_ck_w1[l], nsa_ck_b1[l], nsa_ck_w2[l], nsa_ck_b2[l],
                               nsa_v_pos[l], nsa_cv_w1[l], nsa_cv_b1[l], nsa_cv_w2[l], nsa_cv_b2[l]),
            B, S)
        o_nsa = _nsa(qn, kvn, kvcmp, misc, cov_t, eye, B, S)
        o_mla = _mla(q_m, k_m, v_m, B, S)
        k_mem, v_mem = _mem_kv(memf, mem_wk[l].astype(BF16), mem_wv[l].astype(BF16), 256)
        xf = _mem_attn(xf, o_nsa.reshape(T, -1), o_mla.reshape(T, -1), w_o[l].astype(BF16),
                       ln1_g[l][None, :], ln1_b[l][None, :],
                       k_mem.reshape(B, -1, D), v_mem.reshape(B, -1, D),
                       mem_wq[l].astype(BF16), mem_wo[l].astype(BF16),
                       ln2_g[l][None, :], ln2_b[l][None, :], alpha, S, tm)
        xf = _ffn(xf, ffn_w_up[l][:, :d_ff].astype(BF16), ffn_w_up[l][:, d_ff:].astype(BF16),
                  ffn_conv_w[l], ffn_conv_b[l][None, :], ffn_w_down[l].astype(BF16),
                  ln3_g[l][None, :], ln3_b[l][None, :], alpha, S, FFN_TM)
    return xf.reshape(B, S, D)
```

```python
import functools
import math

import numpy as np
import jax
import jax.numpy as jnp
from jax import lax
from jax.experimental import pallas as pl
from jax.experimental.pallas import tpu as pltpu

F32 = jnp.float32
BF16 = jnp.bfloat16

NSA_HEADS = 8
NSA_GROUPS = 2
NSA_REP = NSA_HEADS // NSA_GROUPS
NSA_DH = 64
CMP_STRIDE = 16
CMP_LEN = 32
SLC_LEN = 64
TOPN = 16
WINDOW = 512
CMP_HIDDEN = 128
FORCE_BONUS = 1e4
MLA_HEADS = 8
MLA_Q_RANK = 384
MLA_KV_RANK = 256
MLA_NOPE = 64
MLA_ROPE = 32
MLA_V = 64
MEM_HEADS = 4
CONV_WIDTH = 3
ROPE_THETA = 10000.0
LN_EPS = 1e-5
RMS_EPS = 1e-6
NEG_INF = -1e30
LOG2E = math.log2(math.e)

LANES = 128
VMEM_LIMIT = 56 * 1024 * 1024

C_Q = 0
C_KVC = 512
C_KVN = 768
C_LAT = 1280
C_MISC = 1920
IN_COLS_PAD = 2048
GATE_LANE0 = MLA_ROPE
MLA_PE2 = 0
MLA_PE1 = 64
ONES_LANE = 64


def _dot(a, b):
    return jnp.dot(a, b, preferred_element_type=F32)


def _dot_nt(a, b):
    return lax.dot_general(a, b, (((1,), (1,)), ((), ())), preferred_element_type=F32)


def _layer_norm(y, g, b):
    mu = jnp.mean(y, axis=-1, keepdims=True)
    d = y - mu
    var = jnp.mean(d * d, axis=-1, keepdims=True)
    return d * lax.rsqrt(var + LN_EPS) * g + b


def _params(*sem):
    return pltpu.CompilerParams(dimension_semantics=sem, vmem_limit_bytes=VMEM_LIMIT)


def _rope_tables(pos_col, inv_row, half):
    ang = pos_col * inv_row
    cos = jnp.cos(ang)
    sin = jnp.sin(ang)
    lane = lax.broadcasted_iota(jnp.int32, (1, LANES), 1)
    upper = (lane & (2 * half - 1)) >= half
    rot = inv_row != 0.0
    sin_hi = jnp.where(upper & rot, sin, 0.0)
    sin_lo = jnp.where(upper | (~rot), 0.0, -sin)
    return cos, sin_hi, sin_lo


def _apply_rope(v, tabs, half):
    cos, sin_hi, sin_lo = tabs
    return v * cos + pltpu.roll(v, half, 1) * sin_hi + pltpu.roll(v, LANES - half, 1) * sin_lo


def _rms_norm(v, g):
    return v * lax.rsqrt(jnp.mean(v * v, axis=-1, keepdims=True) + RMS_EPS) * g


def _inproj_body(pos_ref, inv_ref, x_ref, w_ref, gq_ref, gkv_ref, wq_ref, wkv_ref,
                 qn_ref, kvn_ref, kvc_ref, misc_ref, qm_ref, km_ref, vm_ref):
    xb = x_ref[...].astype(BF16)
    lane = lax.broadcasted_iota(jnp.int32, (1, LANES), 1)
    low = lane < NSA_DH
    qscale = NSA_DH ** -0.5 * LOG2E

    ang = pos_ref[...].astype(F32) * inv_ref[...]
    cos, sin = jnp.cos(ang), jnp.sin(ang)
    nf, mf = NSA_DH // 2, MLA_ROPE // 2

    def tile_nsa(t):
        t = jnp.where(lane < nf, t, 0.0)
        t = t + pltpu.roll(t, nf, 1)
        return t + pltpu.roll(t, 2 * nf, 1)

    def place_mla(t):
        t = jnp.where((lane >= nf) & (lane < nf + mf), t, 0.0)
        return pltpu.roll(t, MLA_PE1 - nf, 1) + pltpu.roll(t, LANES + MLA_PE2 - nf, 1)

    upper = (lane & (NSA_DH - 1)) >= nf
    sin_n = tile_nsa(sin)
    tabs = (tile_nsa(cos), jnp.where(upper, sin_n, 0.0), jnp.where(upper, 0.0, -sin_n))
    pe1 = (lane >= MLA_PE1) & (lane < MLA_PE1 + mf)
    pe2 = (lane >= MLA_PE2) & (lane < MLA_PE2 + mf)
    pe_lanes = pe1 | pe2
    sin_m = place_mla(sin)
    cos_m = jnp.where(pe_lanes, place_mla(cos), 1.0)
    sin_m = jnp.where(pe1, -sin_m, jnp.where(pe2, sin_m, 0.0))

    def rope_mla(v):
        return v * cos_m + pltpu.roll(v, LANES // 2, 1) * sin_m

    def proj(c0, n):
        return _dot(xb, w_ref[:, c0:c0 + n])

    def split_store(v, ref, idx_lo, idx_hi, pad=0.0):
        ref[0, idx_lo] = jnp.where(low, v, pad).astype(BF16)
        ref[0, idx_hi] = jnp.where(low, pltpu.roll(v, NSA_DH, 1), pad).astype(BF16)

    ones_pad = jnp.where(lane == ONES_LANE, 1.0, 0.0)

    for slab in range(2):
        h = proj(C_Q + 256 * slab, 256)
        for j in range(2):
            r = _apply_rope(h[:, LANES * j:LANES * (j + 1)], tabs, NSA_DH // 2) * qscale
            split_store(r, qn_ref, 4 * slab + 2 * j, 4 * slab + 2 * j + 1)

    h = proj(C_KVC, 256)
    kvc_ref[0] = h[:, :LANES]
    kvc_ref[1] = h[:, LANES:]

    for slab in range(2):
        h = proj(C_KVN + 256 * slab, 256)
        k = _apply_rope(h[:, :LANES], tabs, NSA_DH // 2)
        split_store(k, kvn_ref, 4 * slab, 4 * slab + 1)
        split_store(h[:, LANES:], kvn_ref, 4 * slab + 2, 4 * slab + 3, ones_pad)

    lat = [proj(C_LAT + 256 * i, 256) for i in range((IN_COLS_PAD - C_LAT) // 256)]
    misc = lat[2][:, LANES:]
    misc_ref[...] = misc
    mq = jnp.concatenate([lat[0], lat[1][:, :LANES]], axis=1)
    mkv = jnp.concatenate([lat[1][:, LANES:], lat[2][:, :LANES]], axis=1)
    mscale = (MLA_NOPE + MLA_ROPE) ** -0.5 * LOG2E
    qn = _rms_norm(mq, gq_ref[...]).astype(BF16)
    kvn = _rms_norm(mkv, gkv_ref[...]).astype(BF16)
    kpe = jnp.where(pe_lanes, rope_mla(misc), 0.0)
    kcols = MLA_HEADS * LANES
    for slab in range(MLA_HEADS // 2):
        cs = slice(256 * slab, 256 * (slab + 1))
        hq = _dot(qn, wq_ref[:, cs])
        hk = _dot(kvn, wkv_ref[:, cs])
        hv = _dot(kvn, wkv_ref[:, kcols + 256 * slab:kcols + 256 * (slab + 1)])
        for j in range(2):
            ls = slice(LANES * j, LANES * (j + 1))
            qm_ref[0, 2 * slab + j] = (rope_mla(hq[:, ls]) * mscale).astype(BF16)
            km_ref[0, 2 * slab + j] = (hk[:, ls] + kpe).astype(BF16)
            vm_ref[0, 2 * slab + j] = (hv[:, ls] + ones_pad).astype(BF16)


def _inproj(pos, inv_row, xf, w_in_p, gq, gkv, wq_p, wkv_p, B, S, tm):
    T = B * S
    nst = S // tm
    tok = lambda i: (i, 0)
    const = lambda i: (0, 0)
    head_blk = lambda i: (i // nst, 0, i % nst, 0)
    heads = jax.ShapeDtypeStruct((B, 8, S, LANES), BF16)
    return pl.pallas_call(
        _inproj_body,
        grid=(T // tm,),
        in_specs=[
            pl.BlockSpec((tm, 1), tok),
            pl.BlockSpec((1, LANES), const),
            pl.BlockSpec((tm, xf.shape[1]), tok),
            pl.BlockSpec(w_in_p.shape, const),
            pl.BlockSpec(gq.shape, const),
            pl.BlockSpec(gkv.shape, const),
            pl.BlockSpec(wq_p.shape, const),
            pl.BlockSpec(wkv_p.shape, const),
        ],
        out_specs=[
            pl.BlockSpec((1, 8, tm, LANES), head_blk),
            pl.BlockSpec((1, 8, tm, LANES), head_blk),
            pl.BlockSpec((2, tm, LANES), lambda i: (0, i, 0)),
            pl.BlockSpec((tm, LANES), tok),
            pl.BlockSpec((1, 8, tm, LANES), head_blk),
            pl.BlockSpec((1, 8, tm, LANES), head_blk),
            pl.BlockSpec((1, 8, tm, LANES), head_blk),
        ],
        out_shape=[
            heads,
            heads,
            jax.ShapeDtypeStruct((2, T, LANES), F32),
            jax.ShapeDtypeStruct((T, LANES), F32),
            heads,
            heads,
            heads,
        ],
        compiler_params=_params("arbitrary"),
        name="inproj",
    )(pos, inv_row, xf, w_in_p, gq, gkv, wq_p, wkv_p)


def _compress_body(pos_ref, inv_ref, x_ref, pe_ref, w1_ref, b1_ref, w2_ref, b2_ref, o_ref):
    is_k = pl.program_id(0) == 0
    nch = o_ref.shape[2]
    a1 = jnp.zeros((nch, NSA_GROUPS * CMP_HIDDEN), F32)
    a2 = jnp.zeros((nch, NSA_GROUPS * CMP_HIDDEN), F32)
    for l in range(CMP_STRIDE):
        xl = x_ref.at[0, 0][pl.ds(l, nch, stride=CMP_STRIDE), :]
        a1 = a1 + _dot((xl + pe_ref[0, l:l + 1, :]).astype(BF16), w1_ref[0, l])
        a2 = a2 + _dot((xl + pe_ref[0, CMP_STRIDE + l:CMP_STRIDE + l + 1, :]).astype(BF16),
                       w1_ref[0, CMP_STRIDE + l])
    pre = a1 + pltpu.roll(a2, nch - 1, 0) + b1_ref[0]
    hid = jax.nn.gelu(pre, approximate=True)
    out = _dot(hid.astype(BF16), w2_ref[0]) + b2_ref[0]
    tabs = _rope_tables(pos_ref[0].astype(F32), inv_ref[...], NSA_DH // 2)
    row = lax.broadcasted_iota(jnp.int32, (nch, 1), 0)
    for g in range(NSA_GROUPS):
        og = out[:, LANES * g:LANES * (g + 1)]
        og = jnp.where(is_k, _apply_rope(og, tabs, NSA_DH // 2), og)
        o_ref[g, 0] = jnp.where(row < nch - 1, og, 0.0).astype(BF16)


def _compress(pos_cmp, inv_nsa, kvc, pe, w1, b1, w2, b2, B, S):
    nch = S // CMP_STRIDE
    x = kvc.reshape(2, B, S, LANES)
    kv = lambda j, b: (j, 0, 0)
    kv4 = lambda j, b: (j, 0, 0, 0)
    return pl.pallas_call(
        _compress_body,
        grid=(2, B),
        in_specs=[
            pl.BlockSpec((1, nch, 1), lambda j, b: (b, 0, 0)),
            pl.BlockSpec((1, LANES), lambda j, b: (0, 0)),
            pl.BlockSpec((1, 1, S, LANES), lambda j, b: (j, b, 0, 0)),
            pl.BlockSpec((1,) + pe.shape[1:], kv),
            pl.BlockSpec((1,) + w1.shape[1:], kv4),
            pl.BlockSpec((1,) + b1.shape[1:], kv),
            pl.BlockSpec((1,) + w2.shape[1:], kv),
            pl.BlockSpec((1,) + b2.shape[1:], kv),
        ],
        out_specs=pl.BlockSpec((NSA_GROUPS, 1, nch, LANES), lambda j, b: (j, b, 0, 0)),
        out_shape=jax.ShapeDtypeStruct((2 * NSA_GROUPS, B, nch, LANES), BF16),
        compiler_params=_params("arbitrary", "arbitrary"),
        name="compress",
    )(pos_cmp, inv_nsa, x, pe, w1, b1, w2, b2)


def _compress_weights(k_pos, k_w1, k_b1, k_w2, k_b2, v_pos, v_w1, v_b1, v_w2, v_b2):
    def one(pos, w1, b1, w2, b2):
        w1l = w1.reshape(CMP_LEN, NSA_DH, CMP_HIDDEN)
        z1 = jnp.zeros_like(w1l)
        w1bd = jnp.concatenate([jnp.concatenate([w1l, z1], axis=2),
                                jnp.concatenate([z1, w1l], axis=2)], axis=1)
        w2p = _pad_lanes(w2)
        z2 = jnp.zeros_like(w2p)
        w2bd = jnp.concatenate([jnp.concatenate([w2p, z2], axis=1),
                                jnp.concatenate([z2, w2p], axis=1)], axis=0)
        return (jnp.tile(pos, (1, NSA_GROUPS)), w1bd.astype(BF16), jnp.tile(b1, NSA_GROUPS)[None, :],
                w2bd.astype(BF16), jnp.tile(_pad_lanes(b2), NSA_GROUPS)[None, :])
    k = one(k_pos, k_w1, k_b1, k_w2, k_b2)
    v = one(v_pos, v_w1, v_b1, v_w2, v_b2)
    return tuple(jnp.stack([a, b]) for a, b in zip(k, v))


STRIP = 64


def _lane_tile(col, n):
    reps = [col] * (n // LANES)
    if n % LANES:
        reps.append(col[:, :n % LANES])
    return reps[0] if len(reps) == 1 else jnp.concatenate(reps, axis=1)


def _flash_reset(m_ref, acc_ref):
    m_ref[...] = jnp.full(m_ref.shape, NEG_INF, F32)
    acc_ref[...] = jnp.zeros(acc_ref.shape, F32)


def _flash_update(s_ref, v, m_ref, acc_ref, p_ref, mask=None):
    rows, n = s_ref.shape
    for r in range(rows // STRIP):
        rs = slice(STRIP * r, STRIP * (r + 1))
        s = s_ref[rs, :]
        if mask is not None:
            s = jnp.where(mask[rs], s, NEG_INF)
        if m_ref is None:
            p_ref[rs, :] = jnp.exp2(s - jnp.max(s, axis=1, keepdims=True)).astype(BF16)
            continue
        m_old = m_ref[rs, :]
        m_new = jnp.maximum(m_old, jnp.max(s, axis=1, keepdims=True))
        p_ref[rs, :] = jnp.exp2(s - _lane_tile(m_new, n)).astype(BF16)
        acc_ref[rs, :] = jnp.exp2(m_old - m_new) * acc_ref[rs, :]
        m_ref[rs, :] = m_new
    if m_ref is None:
        acc_ref[...] = _dot(p_ref[...], v)
    else:
        acc_ref[...] += _dot(p_ref[...], v)


def _flash_finish(acc):
    return acc * (1.0 / acc[:, ONES_LANE:ONES_LANE + 1])


NSA_TQ = 128
NSA_ROWS = NSA_REP * NSA_TQ
SLC_CHUNK = 512
WIN_SPAN = WINDOW + NSA_TQ
BIAS_LANE0 = LANES


def _nsa_body(q_ref, kvn_ref, kvc_ref, misc_ref, cov_ref, o_ref,
              kaug_ref, score_ref, qbias_ref, sa_ref, sb_ref, pa_ref, pb_ref, ms_ref, accs_ref,
              sw_ref, pw_ref, accw_ref, sc_ref, pn_ref, pc_ref, oc_ref):
    c = pl.program_id(1)
    rows = NSA_ROWS
    t_row = c * NSA_TQ + (lax.broadcasted_iota(jnp.int32, (rows, 1), 0) & (NSA_TQ - 1))
    ncmp = kvc_ref.shape[2]
    nblk = kaug_ref.shape[1] // SLC_LEN

    @pl.when(c == 0)
    def _():
        nkeys = kaug_ref.shape[1]
        key_blk = lax.broadcasted_iota(jnp.int32, (nkeys, LANES), 0) // SLC_LEN
        onehot = jnp.where(key_blk == lax.broadcasted_iota(jnp.int32, (nkeys, LANES), 1), 1.0, 0.0)
        for g in range(NSA_GROUPS):
            kaug_ref[g, :, :BIAS_LANE0] = kvn_ref[0, g]
            kaug_ref[g, :, BIAS_LANE0:] = onehot.astype(BF16)

    qs = [q_ref[0, NSA_REP * g:NSA_REP * (g + 1)].reshape(rows, LANES) for g in range(NSA_GROUPS)]

    cmp_valid = (CMP_STRIDE * lax.broadcasted_iota(jnp.int32, (1, ncmp), 1) + CMP_LEN - 1) <= t_row
    for g in range(NSA_GROUPS):
        sc_ref[g] = _dot_nt(qs[g], kvc_ref[g, 0])
    for g in range(NSA_GROUPS):
        for r in range(rows // STRIP):
            rs = slice(STRIP * r, STRIP * (r + 1))
            s = jnp.where(cmp_valid[rs], sc_ref[g, rs, :], NEG_INF)
            p = jnp.where(cmp_valid[rs], jnp.exp2(s - jnp.max(s, axis=1, keepdims=True)), 0.0)
            l = jnp.sum(p, axis=1, keepdims=True)
            p = p * jnp.where(l > 0.0, 1.0 / l, 0.0)
            pn_ref[g, rs, :] = p
            pc_ref[g, rs, :] = p.astype(BF16)
        oc_ref[g] = _dot(pc_ref[g], kvc_ref[NSA_GROUPS + g, 0])

    def select(nb):
        width = NSA_GROUPS * NSA_TQ
        ps = jnp.concatenate(
            [sum(pn_ref[g, NSA_TQ * r:NSA_TQ * (r + 1), :] for r in range(NSA_REP))
             for g in range(NSA_GROUPS)], axis=0)
        hi = ps.astype(BF16)
        lo = (ps - hi.astype(F32)).astype(BF16)
        imp = _dot_nt(cov_ref[:nb, :], hi) + _dot_nt(cov_ref[:nb, :], lo)
        jidx = lax.broadcasted_iota(jnp.int32, (nb, width), 0)
        lane_q = lax.broadcasted_iota(jnp.int32, (1, width), 1) & (NSA_TQ - 1)
        cur = c * (NSA_TQ // SLC_LEN) + lane_q // SLC_LEN
        forced = (jidx == 0) | (jidx == cur) | (jidx == cur - 1)
        score = jnp.where(jidx <= cur, jnp.where(forced, FORCE_BONUS, imp), NEG_INF)
        score_ref[:nb, :] = score
        sub = 8
        cnt = [jnp.zeros((sub, width), F32) for _ in range(nb // sub)]
        tiles = [score[sub * v:sub * (v + 1)] for v in range(nb // sub)]
        sidx = lax.broadcasted_iota(jnp.int32, (sub, width), 0)
        for jp in range(nb):
            rowv = jnp.broadcast_to(score_ref[jp:jp + 1, :], (sub, width))
            for v in range(nb // sub):
                if sub * v > jp:
                    cnt[v] = jnp.where(rowv >= tiles[v], cnt[v] + 1.0, cnt[v])
                elif sub * v + sub - 1 <= jp:
                    cnt[v] = jnp.where(rowv > tiles[v], cnt[v] + 1.0, cnt[v])
                else:
                    ge = jnp.where(rowv >= tiles[v], cnt[v] + 1.0, cnt[v])
                    gt = jnp.where(rowv > tiles[v], cnt[v] + 1.0, cnt[v])
                    cnt[v] = jnp.where(sidx + sub * v > jp, ge, gt)
        rank = jnp.concatenate(cnt, axis=0)
        bias = jnp.where(rank < float(TOPN), 0.0, NEG_INF)
        bias = jnp.concatenate([bias, jnp.zeros((LANES - nb, width), F32)], axis=0)
        qbias_ref[...] = bias.T.astype(BF16)

    visible = (c + 1) * (NSA_TQ // SLC_LEN)

    @pl.when(visible <= TOPN)
    def _():
        qbias_ref[...] = jnp.zeros(qbias_ref.shape, BF16)

    bounds = [TOPN] + [nb for nb in (nblk // 2, 3 * nblk // 4) if TOPN < nb < nblk] + [nblk]
    for lo_nb, nb in zip(bounds[:-1], bounds[1:]):
        pl.when((visible > lo_nb) & (visible <= nb))(functools.partial(select, nb))

    _flash_reset(ms_ref, accs_ref)
    qas = [jnp.concatenate(
        [qs[g], jnp.concatenate([qbias_ref[NSA_TQ * g:NSA_TQ * (g + 1), :]] * NSA_REP, axis=0)],
        axis=1) for g in range(NSA_GROUPS)]
    last = c // (SLC_CHUNK // NSA_TQ)

    def slc_scores(kc, dst):
        k0 = pl.multiple_of(kc * SLC_CHUNK, SLC_CHUNK)
        for g in range(NSA_GROUPS):
            dst[g] = _dot_nt(qas[g], kaug_ref[g, pl.ds(k0, SLC_CHUNK), :])

    def slc_update(src, p_ref, kc, causal):
        k0 = pl.multiple_of(kc * SLC_CHUNK, SLC_CHUNK)
        mask = None
        if causal:
            mask = (k0 + lax.broadcasted_iota(jnp.int32, (1, SLC_CHUNK), 1)) <= t_row
        for g in range(NSA_GROUPS):
            _flash_update(src.at[g], kvn_ref[0, 2 + g, pl.ds(k0, SLC_CHUNK), :],
                          ms_ref.at[g], accs_ref.at[g], p_ref.at[g], mask)

    slc_scores(0, sa_ref)

    win_start = pl.multiple_of(jnp.maximum(c * NSA_TQ - WINDOW, 0), NSA_TQ)
    diff = t_row - (win_start + lax.broadcasted_iota(jnp.int32, (1, WIN_SPAN), 1))
    win_valid = (diff >= 0) & (diff < WINDOW)
    for g in range(NSA_GROUPS):
        sw_ref[g] = _dot_nt(qs[g], kvn_ref[0, 4 + g, pl.ds(win_start, WIN_SPAN), :])
    for g in range(NSA_GROUPS):
        _flash_update(sw_ref.at[g], kvn_ref[0, 6 + g, pl.ds(win_start, WIN_SPAN), :],
                      None, accw_ref.at[g], pw_ref.at[g], win_valid)

    def slc_pair(i, carry):
        slc_scores(2 * i + 1, sb_ref)
        slc_update(sa_ref, pa_ref, 2 * i, False)
        slc_scores(2 * i + 2, sa_ref)
        slc_update(sb_ref, pb_ref, 2 * i + 1, False)
        return carry
    lax.fori_loop(0, last // 2, slc_pair, 0)
    tail = 2 * (last // 2)

    @pl.when(last > tail)
    def _():
        slc_scores(tail + 1, sb_ref)
        slc_update(sa_ref, pa_ref, tail, False)
        slc_update(sb_ref, pb_ref, tail + 1, True)

    @pl.when(last == tail)
    def _():
        slc_update(sa_ref, pa_ref, tail, True)

    sig = 1.0 / (1.0 + jnp.exp(-misc_ref[...]))
    outs = []
    for g in range(NSA_GROUPS):
        for r in range(NSA_REP):
            lane0 = GATE_LANE0 + 3 * (NSA_REP * g + r)
            rs = slice(NSA_TQ * r, NSA_TQ * (r + 1))
            o = (sig[:, lane0:lane0 + 1] * oc_ref[g, rs, :]
                 + sig[:, lane0 + 1:lane0 + 2] * _flash_finish(accs_ref[g, rs, :])
                 + sig[:, lane0 + 2:lane0 + 3] * _flash_finish(accw_ref[g, rs, :]))
            outs.append(o[:, :NSA_DH])
    o_ref[0] = jnp.concatenate(outs, axis=1).astype(BF16)


def _nsa(qn, kvn, kvcmp, misc, cov_t, B, S):
    nblk = S // SLC_LEN
    ncmp = S // CMP_STRIDE
    nq = S // NSA_TQ
    rows = NSA_ROWS
    return pl.pallas_call(
        _nsa_body,
        grid=(B, nq),
        in_specs=[
            pl.BlockSpec((1, NSA_HEADS, NSA_TQ, LANES), lambda b, c: (b, 0, c, 0)),
            pl.BlockSpec((1, 8, S, LANES), lambda b, c: (b, 0, 0, 0)),
            pl.BlockSpec((4, 1, ncmp, LANES), lambda b, c: (0, b, 0, 0)),
            pl.BlockSpec((NSA_TQ, LANES), lambda b, c: (b * nq + c, 0)),
            pl.BlockSpec(cov_t.shape, lambda b, c: (0, 0)),
        ],
        out_specs=pl.BlockSpec((1, NSA_TQ, NSA_HEADS * NSA_DH), lambda b, c: (b, c, 0)),
        out_shape=jax.ShapeDtypeStruct((B, S, NSA_HEADS * NSA_DH), BF16),
        scratch_shapes=[
            pltpu.VMEM((NSA_GROUPS, S, 2 * LANES), BF16),
            pltpu.VMEM((nblk, NSA_GROUPS * NSA_TQ), F32),
            pltpu.VMEM((NSA_GROUPS * NSA_TQ, LANES), BF16),
            pltpu.VMEM((NSA_GROUPS, rows, SLC_CHUNK), F32),
            pltpu.VMEM((NSA_GROUPS, rows, SLC_CHUNK), F32),
            pltpu.VMEM((NSA_GROUPS, rows, SLC_CHUNK), BF16),
            pltpu.VMEM((NSA_GROUPS, rows, SLC_CHUNK), BF16),
            pltpu.VMEM((NSA_GROUPS, rows, LANES), F32),
            pltpu.VMEM((NSA_GROUPS, rows, LANES), F32),
            pltpu.VMEM((NSA_GROUPS, rows, WIN_SPAN), F32),
            pltpu.VMEM((NSA_GROUPS, rows, WIN_SPAN), BF16),
            pltpu.VMEM((NSA_GROUPS, rows, LANES), F32),
            pltpu.VMEM((NSA_GROUPS, rows, ncmp), F32),
            pltpu.VMEM((NSA_GROUPS, rows, ncmp), F32),
            pltpu.VMEM((NSA_GROUPS, rows, ncmp), BF16),
            pltpu.VMEM((NSA_GROUPS, rows, LANES), F32),
        ],
        compiler_params=_params("arbitrary", "arbitrary"),
        name="nsa",
    )(qn, kvn, kvcmp, misc, cov_t)


MLA_TQ = 512
MLA_CHUNK = 512
MLA_HPB = 4


def _mla_body(q_ref, k_ref, v_ref, o_ref, sa_ref, sb_ref, pa_ref, pb_ref, m_ref, acc_ref):
    qi = pl.program_id(2)
    t_row = qi * MLA_TQ + lax.broadcasted_iota(jnp.int32, (MLA_TQ, 1), 0)
    _flash_reset(m_ref, acc_ref)

    def scores(kc, dst):
        k0 = pl.multiple_of(kc * MLA_CHUNK, MLA_CHUNK)
        for j in range(MLA_HPB):
            dst[j] = _dot_nt(q_ref[0, j], k_ref[0, j, pl.ds(k0, MLA_CHUNK), :])

    def update(src, p_ref, kc, causal):
        k0 = pl.multiple_of(kc * MLA_CHUNK, MLA_CHUNK)
        mask = None
        if causal:
            mask = (k0 + lax.broadcasted_iota(jnp.int32, (1, MLA_CHUNK), 1)) <= t_row
        for j in range(MLA_HPB):
            _flash_update(src.at[j], v_ref[0, j, pl.ds(k0, MLA_CHUNK), :],
                          m_ref.at[j], acc_ref.at[j], p_ref.at[j], mask)

    scores(0, sa_ref)

    def pair(i, carry):
        scores(2 * i + 1, sb_ref)
        update(sa_ref, pa_ref, 2 * i, False)
        scores(2 * i + 2, sa_ref)
        update(sb_ref, pb_ref, 2 * i + 1, False)
        return carry
    lax.fori_loop(0, qi // 2, pair, 0)
    tail = 2 * (qi // 2)

    @pl.when(qi > tail)
    def _():
        scores(tail + 1, sb_ref)
        update(sa_ref, pa_ref, tail, False)
        update(sb_ref, pb_ref, tail + 1, True)

    @pl.when(qi == tail)
    def _():
        update(sa_ref, pa_ref, tail, True)
    o_ref[0] = jnp.concatenate([_flash_finish(acc_ref[j])[:, :MLA_V] for j in range(MLA_HPB)],
                               axis=1).astype(BF16)


def _mla(q, k, v, B, S):
    return pl.pallas_call(
        _mla_body,
        grid=(B, MLA_HEADS // MLA_HPB, S // MLA_TQ),
        in_specs=[
            pl.BlockSpec((1, MLA_HPB, MLA_TQ, LANES), lambda b, h, i: (b, h, i, 0)),
            pl.BlockSpec((1, MLA_HPB, S, LANES), lambda b, h, i: (b, h, 0, 0)),
            pl.BlockSpec((1, MLA_HPB, S, LANES), lambda b, h, i: (b, h, 0, 0)),
        ],
        out_specs=pl.BlockSpec((1, MLA_TQ, MLA_HPB * MLA_V), lambda b, h, i: (b, i, h)),
        out_shape=jax.ShapeDtypeStruct((B, S, MLA_HEADS * MLA_V), BF16),
        scratch_shapes=[
            pltpu.VMEM((MLA_HPB, MLA_TQ, MLA_CHUNK), F32),
            pltpu.VMEM((MLA_HPB, MLA_TQ, MLA_CHUNK), F32),
            pltpu.VMEM((MLA_HPB, MLA_TQ, MLA_CHUNK), BF16),
            pltpu.VMEM((MLA_HPB, MLA_TQ, MLA_CHUNK), BF16),
            pltpu.VMEM((MLA_HPB, MLA_TQ, LANES), F32),
            pltpu.VMEM((MLA_HPB, MLA_TQ, LANES), F32),
        ],
        compiler_params=_params("arbitrary", "arbitrary", "arbitrary"),
        name="mla",
    )(q, k, v)


def _mem_kv_body(m_ref, wk_ref, wv_ref, k_ref, v_ref):
    mb = m_ref[...].astype(BF16)
    k_ref[...] = _dot(mb, wk_ref[...]).astype(BF16)
    v_ref[...] = _dot(mb, wv_ref[...]).astype(BF16)


def _mem_kv(memf, wk, wv, tm):
    R, D = memf.shape
    tok = lambda i: (i, 0)
    const = lambda i: (0, 0)
    return pl.pallas_call(
        _mem_kv_body,
        grid=(R // tm,),
        in_specs=[pl.BlockSpec((tm, D), tok), pl.BlockSpec(wk.shape, const),
                  pl.BlockSpec(wv.shape, const)],
        out_specs=[pl.BlockSpec((tm, D), tok), pl.BlockSpec((tm, D), tok)],
        out_shape=[jax.ShapeDtypeStruct((R, D), BF16), jax.ShapeDtypeStruct((R, D), BF16)],
        compiler_params=_params("arbitrary"),
        name="mem_kv",
    )(memf, wk, wv)


def _mem_attn_body(x0_ref, on_ref, om_ref, wmix_ref, g1_ref, b1_ref,
                   k_ref, v_ref, wq_ref, wo_ref, g_ref, b_ref, o_ref, *, alpha):
    half = on_ref.shape[1]
    mix = _dot(on_ref[...], wmix_ref[:half, :]) + _dot(om_ref[...], wmix_ref[half:, :])
    x = _layer_norm(alpha * x0_ref[...] + mix, g1_ref[...], b1_ref[...])
    D = x.shape[1]
    dh = D // MEM_HEADS
    q = (_dot(x.astype(BF16), wq_ref[...]) * (dh ** -0.5 * LOG2E)).astype(BF16)
    outs = []
    for h in range(MEM_HEADS):
        cs = slice(dh * h, dh * (h + 1))
        s = _dot_nt(q[:, cs], k_ref[0, :, cs])
        p = jnp.exp2(s - jnp.max(s, axis=1, keepdims=True))
        l = jnp.sum(p, axis=1, keepdims=True)
        outs.append((_dot(p.astype(BF16), v_ref[0, :, cs]) * (1.0 / l)).astype(BF16))
    o = jnp.concatenate(outs, axis=1)
    y = _dot(o, wo_ref[...])
    o_ref[...] = _layer_norm(alpha * x + y, g_ref[...], b_ref[...])


def _mem_attn(xf, o_nsa, o_mla, w_o, g1, b1, k_mem, v_mem, wq, wo, g, b, alpha, S, tm):
    T, D = xf.shape
    nst = S // tm
    M = k_mem.shape[1]
    tok = lambda i: (i, 0)
    const = lambda i: (0, 0)
    memb = lambda i: (i // nst, 0, 0)
    return pl.pallas_call(
        functools.partial(_mem_attn_body, alpha=alpha),
        grid=(T // tm,),
        in_specs=[
            pl.BlockSpec((tm, D), tok),
            pl.BlockSpec((tm, o_nsa.shape[1]), tok),
            pl.BlockSpec((tm, o_mla.shape[1]), tok),
            pl.BlockSpec(w_o.shape, const),
            pl.BlockSpec((1, D), const),
            pl.BlockSpec((1, D), const),
            pl.BlockSpec((1, M, D), memb),
            pl.BlockSpec((1, M, D), memb),
            pl.BlockSpec(wq.shape, const),
            pl.BlockSpec(wo.shape, const),
            pl.BlockSpec((1, D), const),
            pl.BlockSpec((1, D), const),
        ],
        out_specs=pl.BlockSpec((tm, D), tok),
        out_shape=jax.ShapeDtypeStruct((T, D), F32),
        compiler_params=_params("arbitrary"),
        name="mem_attn",
    )(xf, o_nsa, o_mla, w_o, g1, b1, k_mem, v_mem, wq, wo, g, b)


HALO = 8


FFN_SLAB = 256
FFN_TM = 512


def _ffn_body(x_ref, xh_ref, wg_ref, wu_ref, cw_ref, cb_ref, wd_ref, g_ref, b_ref, o_ref,
              act_ref, *, alpha, seq_tiles):
    i = pl.program_id(0)
    x = x_ref[...]
    xb = x.astype(BF16)
    xhb = xh_ref[...].astype(BF16)
    tm = x.shape[0]
    row = lax.broadcasted_iota(jnp.int32, (tm, 1), 0)
    seq_start = i % seq_tiles == 0
    for c0 in range(0, wg_ref.shape[1], FFN_SLAB):
        cs = slice(c0, c0 + FFN_SLAB)
        gate = _dot(xb, wg_ref[:, cs])
        up = _dot(xb, wu_ref[:, cs])
        halo = jnp.where(seq_start, 0.0, _dot(xhb, wg_ref[:, cs]))
        g1 = jnp.where(row == 0, halo[HALO - 1:HALO], pltpu.roll(gate, 1, 0))
        g2 = jnp.where(row == 0, halo[HALO - 2:HALO - 1],
                       jnp.where(row == 1, halo[HALO - 1:HALO], pltpu.roll(gate, 2, 0)))
        conv = cw_ref[0:1, cs] * g2 + cw_ref[1:2, cs] * g1 + cw_ref[2:3, cs] * gate + cb_ref[:, cs]
        act_ref[:, cs] = (conv * (1.0 / (1.0 + jnp.exp(-conv))) * up).astype(BF16)
    y = _dot(act_ref[...], wd_ref[...])
    o_ref[...] = _layer_norm(alpha * x + y, g_ref[...], b_ref[...])


def _ffn(xf, wg, wu, cw, cb, wd, g, b, alpha, S, tm):
    T, D = xf.shape
    dff = wg.shape[1]
    assert dff % FFN_SLAB == 0
    tok = lambda i: (i, 0)
    const = lambda i: (0, 0)
    return pl.pallas_call(
        functools.partial(_ffn_body, alpha=alpha, seq_tiles=S // tm),
        grid=(T // tm,),
        in_specs=[
            pl.BlockSpec((tm, D), tok),
            pl.BlockSpec((HALO, D), lambda i: (jnp.maximum(i * (tm // HALO) - 1, 0), 0)),
            pl.BlockSpec((D, dff), const),
            pl.BlockSpec((D, dff), const),
            pl.BlockSpec((CONV_WIDTH, dff), const),
            pl.BlockSpec((1, dff), const),
            pl.BlockSpec((dff, D), const),
            pl.BlockSpec((1, D), const),
            pl.BlockSpec((1, D), const),
        ],
        out_specs=pl.BlockSpec((tm, D), tok),
        out_shape=jax.ShapeDtypeStruct((T, D), F32),
        scratch_shapes=[pltpu.VMEM((tm, dff), BF16)],
        compiler_params=_params("arbitrary"),
        name="ffn",
    )(xf, xf, wg, wu, cw, cb, wd, g, b)


def _inv_freq_row(dim, lane_lo, lane_hi, period):
    inv = ROPE_THETA ** (-np.arange(0, dim, 2, dtype=np.float64) / dim)
    row = np.zeros((1, LANES), np.float32)
    for lane in range(lane_lo, lane_hi):
        row[0, lane] = inv[(lane % period) % (dim // 2)]
    return jnp.asarray(row)


def _cover_t(S):
    nc = S // CMP_STRIDE
    ns = S // SLC_LEN
    cs = np.arange(nc)[:, None] * CMP_STRIDE
    ss = np.arange(ns)[None, :] * SLC_LEN
    cover = np.clip(np.minimum(cs + CMP_LEN, ss + SLC_LEN) - np.maximum(cs, ss), 0, None) / CMP_LEN
    cover[nc - 1:] = 0.0
    return jnp.asarray(cover.T, dtype=BF16)


def _permute_w_in(w):
    D = w.shape[0]
    c1 = NSA_HEADS * NSA_DH
    c2 = c1 + 3 * 2 * NSA_GROUPS * NSA_DH
    c3 = c2 + 3 * NSA_HEADS
    c4 = c3 + MLA_Q_RANK
    c5 = c4 + MLA_KV_RANK
    c6 = c5 + MLA_ROPE
    half = MLA_ROPE // 2
    z = lambda n: jnp.zeros((D, n), w.dtype)
    misc = jnp.concatenate(
        [w[:, c5 + half:c6], z(GATE_LANE0 - half), w[:, c2:c3], z(MLA_PE1 - GATE_LANE0 - (c3 - c2)),
         w[:, c5:c5 + half], z(LANES - MLA_PE1 - half)], axis=1)
    return jnp.concatenate([w[:, :c2], w[:, c3:c5], misc], axis=1).astype(BF16)


def _mla_head_lanes(nope, pe):
    r, H, _ = nope.shape
    half = MLA_ROPE // 2
    split = MLA_PE1 - half
    pad = jnp.zeros((r, H, LANES - MLA_NOPE - MLA_ROPE), nope.dtype)
    return jnp.concatenate([pe[..., half:], nope[..., :split], pe[..., :half], nope[..., split:], pad],
                           axis=2).reshape(r, H * LANES)


def _permute_w_uq(w):
    r = w.shape[0]
    w3 = w.reshape(r, MLA_HEADS, MLA_NOPE + MLA_ROPE)
    return _mla_head_lanes(w3[..., :MLA_NOPE], w3[..., MLA_NOPE:]).astype(BF16)


def _permute_w_ukv(w):
    r = w.shape[0]
    w3 = w.reshape(r, MLA_HEADS, MLA_NOPE + MLA_V)
    k = _mla_head_lanes(w3[..., :MLA_NOPE], jnp.zeros((r, MLA_HEADS, MLA_ROPE), w.dtype))
    v = _pad_lanes(w3[..., MLA_NOPE:]).reshape(r, MLA_HEADS * LANES)
    return jnp.concatenate([k, v], axis=1).astype(BF16)


def _pad_lanes(a):
    return jnp.concatenate([a, jnp.zeros(a.shape[:-1] + (LANES - a.shape[-1],), a.dtype)], axis=-1)


def kernel(x, mem, positions, w_in, nsa_k_pos, nsa_ck_w1, nsa_ck_b1, nsa_ck_w2, nsa_ck_b2,
           nsa_v_pos, nsa_cv_w1, nsa_cv_b1, nsa_cv_w2, nsa_cv_b2,
           mla_q_norm, mla_w_uq, mla_kv_norm, mla_w_ukv, w_o, ln1_g, ln1_b,
           mem_wq, mem_wk, mem_wv, mem_wo, ln2_g, ln2_b,
           ffn_w_up, ffn_conv_w, ffn_conv_b, ffn_w_down, ln3_g, ln3_b):
    B, S, D = x.shape
    T = B * S
    depth = w_in.shape[0]
    alpha = (2.0 * depth) ** 0.25
    d_ff = ffn_w_down.shape[1]
    tm = min(512, S)
    assert S % MLA_TQ == 0 and S >= WIN_SPAN and S % tm == 0
    assert (B * mem.shape[1]) % 256 == 0

    pos = positions.reshape(T, 1)
    pos_cmp = positions[:, CMP_LEN - 1::CMP_STRIDE]
    pos_cmp = jnp.concatenate([pos_cmp, pos_cmp[:, -1:]], axis=1)[:, :, None]
    inv_cmp = _inv_freq_row(NSA_DH, 0, NSA_DH, NSA_DH)
    inv_tok = (_inv_freq_row(NSA_DH, 0, NSA_DH // 2, NSA_DH)
               + _inv_freq_row(MLA_ROPE, NSA_DH // 2, NSA_DH // 2 + MLA_ROPE // 2, MLA_ROPE // 2))
    cov_t = _cover_t(S)
    memf = mem.reshape(B * mem.shape[1], D)

    xf = x.reshape(T, D)
    for l in range(depth):
        qn, kvn, kvc, misc, q_m, k_m, v_m = _inproj(
            pos, inv_tok, xf, _permute_w_in(w_in[l]), mla_q_norm[l][None, :],
            mla_kv_norm[l][None, :], _permute_w_uq(mla_w_uq[l]), _permute_w_ukv(mla_w_ukv[l]),
            B, S, tm)
        kvcmp = _compress(
            pos_cmp, inv_cmp, kvc,
            *_compress_weights(nsa_k_pos[l], nsa_ck_w1[l], nsa_ck_b1[l], nsa_ck_w2[l], nsa_ck_b2[l],
                               nsa_v_pos[l], nsa_cv_w1[l], nsa_cv_b1[l], nsa_cv_w2[l], nsa_cv_b2[l]),
            B, S)
        o_nsa = _nsa(qn, kvn, kvcmp, misc, cov_t, B, S)
        o_mla = _mla(q_m, k_m, v_m, B, S)
        k_mem, v_mem = _mem_kv(memf, mem_wk[l].astype(BF16), mem_wv[l].astype(BF16), 256)
        xf = _mem_attn(xf, o_nsa.reshape(T, -1), o_mla.reshape(T, -1), w_o[l].astype(BF16),
                       ln1_g[l][None, :], ln1_b[l][None, :],
                       k_mem.reshape(B, -1, D), v_mem.reshape(B, -1, D),
                       mem_wq[l].astype(BF16), mem_wo[l].astype(BF16),
                       ln2_g[l][None, :], ln2_b[l][None, :], alpha, S, tm)
        xf = _ffn(xf, ffn_w_up[l][:, :d_ff].astype(BF16), ffn_w_up[l][:, d_ff:].astype(BF16),
                  ffn_conv_w[l], ffn_conv_b[l][None, :], ffn_w_down[l].astype(BF16),
                  ln3_g[l][None, :], ln3_b[l][None, :], alpha, S, FFN_TM)
    return xf.reshape(B, S, D)
```

```python
import functools
import math

import numpy as np
import jax
import jax.numpy as jnp
from jax import lax
from jax.experimental import pallas as pl
from jax.experimental.pallas import tpu as pltpu

F32 = jnp.float32
BF16 = jnp.bfloat16

NSA_HEADS = 8
NSA_GROUPS = 2
NSA_REP = NSA_HEADS // NSA_GROUPS
NSA_DH = 64
CMP_STRIDE = 16
CMP_LEN = 32
SLC_LEN = 64
TOPN = 16
WINDOW = 512
CMP_HIDDEN = 128
FORCE_BONUS = 1e4
MLA_HEADS = 8
MLA_Q_RANK = 384
MLA_KV_RANK = 256
MLA_NOPE = 64
MLA_ROPE = 32
MLA_V = 64
MEM_HEADS = 4
CONV_WIDTH = 3
ROPE_THETA = 10000.0
LN_EPS = 1e-5
RMS_EPS = 1e-6
NEG_INF = -1e30
LOG2E = math.log2(math.e)

LANES = 128
VMEM_LIMIT = 56 * 1024 * 1024

C_Q = 0
C_KVC = 512
C_KVN = 768
C_LAT = 1280
C_MISC = 1920
IN_COLS_PAD = 2048
GATE_LANE0 = MLA_ROPE
MLA_PE2 = 0
MLA_PE1 = 64
ONES_LANE = 64


def _dot(a, b):
    return jnp.dot(a, b, preferred_element_type=F32)


def _dot_nt(a, b):
    return lax.dot_general(a, b, (((1,), (1,)), ((), ())), preferred_element_type=F32)


def _layer_norm(y, g, b):
    mu = jnp.mean(y, axis=-1, keepdims=True)
    d = y - mu
    var = jnp.mean(d * d, axis=-1, keepdims=True)
    return d * lax.rsqrt(var + LN_EPS) * g + b


def _params(*sem):
    return pltpu.CompilerParams(dimension_semantics=sem, vmem_limit_bytes=VMEM_LIMIT)


def _rope_tables(pos_col, inv_row, half):
    ang = pos_col * inv_row
    cos = jnp.cos(ang)
    sin = jnp.sin(ang)
    lane = lax.broadcasted_iota(jnp.int32, (1, LANES), 1)
    upper = (lane & (2 * half - 1)) >= half
    rot = inv_row != 0.0
    sin_hi = jnp.where(upper & rot, sin, 0.0)
    sin_lo = jnp.where(upper | (~rot), 0.0, -sin)
    return cos, sin_hi, sin_lo


def _apply_rope(v, tabs, half):
    cos, sin_hi, sin_lo = tabs
    return v * cos + pltpu.roll(v, half, 1) * sin_hi + pltpu.roll(v, LANES - half, 1) * sin_lo


def _rms_norm(v, g):
    return v * lax.rsqrt(jnp.mean(v * v, axis=-1, keepdims=True) + RMS_EPS) * g


def _inproj_body(pos_ref, inv_ref, x_ref, w_ref, gq_ref, gkv_ref, wq_ref, wkv_ref,
                 qn_ref, kvn_ref, kvc_ref, misc_ref, qm_ref, km_ref, vm_ref):
    xb = x_ref[...].astype(BF16)
    lane = lax.broadcasted_iota(jnp.int32, (1, LANES), 1)
    low = lane < NSA_DH
    qscale = NSA_DH ** -0.5 * LOG2E

    ang = pos_ref[...].astype(F32) * inv_ref[...]
    cos, sin = jnp.cos(ang), jnp.sin(ang)
    nf, mf = NSA_DH // 2, MLA_ROPE // 2

    def tile_nsa(t):
        t = jnp.where(lane < nf, t, 0.0)
        t = t + pltpu.roll(t, nf, 1)
        return t + pltpu.roll(t, 2 * nf, 1)

    def place_mla(t):
        t = jnp.where((lane >= nf) & (lane < nf + mf), t, 0.0)
        return pltpu.roll(t, MLA_PE1 - nf, 1) + pltpu.roll(t, LANES + MLA_PE2 - nf, 1)

    upper = (lane & (NSA_DH - 1)) >= nf
    sin_n = tile_nsa(sin)
    tabs = (tile_nsa(cos), jnp.where(upper, sin_n, 0.0), jnp.where(upper, 0.0, -sin_n))
    pe1 = (lane >= MLA_PE1) & (lane < MLA_PE1 + mf)
    pe2 = (lane >= MLA_PE2) & (lane < MLA_PE2 + mf)
    pe_lanes = pe1 | pe2
    sin_m = place_mla(sin)
    cos_m = jnp.where(pe_lanes, place_mla(cos), 1.0)
    sin_m = jnp.where(pe1, -sin_m, jnp.where(pe2, sin_m, 0.0))

    def rope_mla(v):
        return v * cos_m + pltpu.roll(v, LANES // 2, 1) * sin_m

    def proj(c0, n):
        return _dot(xb, w_ref[:, c0:c0 + n])

    def split_store(v, ref, idx_lo, idx_hi, pad=0.0):
        ref[0, idx_lo] = jnp.where(low, v, pad).astype(BF16)
        ref[0, idx_hi] = jnp.where(low, pltpu.roll(v, NSA_DH, 1), pad).astype(BF16)

    ones_pad = jnp.where(lane == ONES_LANE, 1.0, 0.0)

    for slab in range(2):
        h = proj(C_Q + 256 * slab, 256)
        for j in range(2):
            r = _apply_rope(h[:, LANES * j:LANES * (j + 1)], tabs, NSA_DH // 2) * qscale
            split_store(r, qn_ref, 4 * slab + 2 * j, 4 * slab + 2 * j + 1)

    h = proj(C_KVC, 256)
    kvc_ref[0] = h[:, :LANES]
    kvc_ref[1] = h[:, LANES:]

    for slab in range(2):
        h = proj(C_KVN + 256 * slab, 256)
        k = _apply_rope(h[:, :LANES], tabs, NSA_DH // 2)
        split_store(k, kvn_ref, 4 * slab, 4 * slab + 1)
        split_store(h[:, LANES:], kvn_ref, 4 * slab + 2, 4 * slab + 3, ones_pad)

    lat = [proj(C_LAT + 256 * i, 256) for i in range((IN_COLS_PAD - C_LAT) // 256)]
    misc = lat[2][:, LANES:]
    misc_ref[...] = misc
    mq = jnp.concatenate([lat[0], lat[1][:, :LANES]], axis=1)
    mkv = jnp.concatenate([lat[1][:, LANES:], lat[2][:, :LANES]], axis=1)
    mscale = (MLA_NOPE + MLA_ROPE) ** -0.5 * LOG2E
    qn = _rms_norm(mq, gq_ref[...]).astype(BF16)
    kvn = _rms_norm(mkv, gkv_ref[...]).astype(BF16)
    kpe = jnp.where(pe_lanes, rope_mla(misc), 0.0)
    kcols = MLA_HEADS * LANES
    for slab in range(MLA_HEADS // 2):
        cs = slice(256 * slab, 256 * (slab + 1))
        hq = _dot(qn, wq_ref[:, cs])
        hk = _dot(kvn, wkv_ref[:, cs])
        hv = _dot(kvn, wkv_ref[:, kcols + 256 * slab:kcols + 256 * (slab + 1)])
        for j in range(2):
            ls = slice(LANES * j, LANES * (j + 1))
            qm_ref[0, 2 * slab + j] = (rope_mla(hq[:, ls]) * mscale).astype(BF16)
            km_ref[0, 2 * slab + j] = (hk[:, ls] + kpe).astype(BF16)
            vm_ref[0, 2 * slab + j] = (hv[:, ls] + ones_pad).astype(BF16)


def _inproj(pos, inv_row, xf, w_in_p, gq, gkv, wq_p, wkv_p, B, S, tm):
    T = B * S
    nst = S // tm
    tok = lambda i: (i, 0)
    const = lambda i: (0, 0)
    head_blk = lambda i: (i // nst, 0, i % nst, 0)
    heads = jax.ShapeDtypeStruct((B, 8, S, LANES), BF16)
    return pl.pallas_call(
        _inproj_body,
        grid=(T // tm,),
        in_specs=[
            pl.BlockSpec((tm, 1), tok),
            pl.BlockSpec((1, LANES), const),
            pl.BlockSpec((tm, xf.shape[1]), tok),
            pl.BlockSpec(w_in_p.shape, const),
            pl.BlockSpec(gq.shape, const),
            pl.BlockSpec(gkv.shape, const),
            pl.BlockSpec(wq_p.shape, const),
            pl.BlockSpec(wkv_p.shape, const),
        ],
        out_specs=[
            pl.BlockSpec((1, 8, tm, LANES), head_blk),
            pl.BlockSpec((1, 8, tm, LANES), head_blk),
            pl.BlockSpec((2, tm, LANES), lambda i: (0, i, 0)),
            pl.BlockSpec((tm, LANES), tok),
            pl.BlockSpec((1, 8, tm, LANES), head_blk),
            pl.BlockSpec((1, 8, tm, LANES), head_blk),
            pl.BlockSpec((1, 8, tm, LANES), head_blk),
        ],
        out_shape=[
            heads,
            heads,
            jax.ShapeDtypeStruct((2, T, LANES), F32),
            jax.ShapeDtypeStruct((T, LANES), F32),
            heads,
            heads,
            heads,
        ],
        compiler_params=_params("arbitrary"),
        name="inproj",
    )(pos, inv_row, xf, w_in_p, gq, gkv, wq_p, wkv_p)


def _compress_body(pos_ref, inv_ref, x_ref, pe_ref, w1_ref, b1_ref, w2_ref, b2_ref, o_ref):
    is_k = pl.program_id(0) == 0
    nch = o_ref.shape[2]
    a1 = jnp.zeros((nch, NSA_GROUPS * CMP_HIDDEN), F32)
    a2 = jnp.zeros((nch, NSA_GROUPS * CMP_HIDDEN), F32)
    for l in range(CMP_STRIDE):
        xl = x_ref.at[0, 0][pl.ds(l, nch, stride=CMP_STRIDE), :]
        a1 = a1 + _dot((xl + pe_ref[0, l:l + 1, :]).astype(BF16), w1_ref[0, l])
        a2 = a2 + _dot((xl + pe_ref[0, CMP_STRIDE + l:CMP_STRIDE + l + 1, :]).astype(BF16),
                       w1_ref[0, CMP_STRIDE + l])
    pre = a1 + pltpu.roll(a2, nch - 1, 0) + b1_ref[0]
    hid = jax.nn.gelu(pre, approximate=True)
    out = _dot(hid.astype(BF16), w2_ref[0]) + b2_ref[0]
    tabs = _rope_tables(pos_ref[0].astype(F32), inv_ref[...], NSA_DH // 2)
    row = lax.broadcasted_iota(jnp.int32, (nch, 1), 0)
    for g in range(NSA_GROUPS):
        og = out[:, LANES * g:LANES * (g + 1)]
        og = jnp.where(is_k, _apply_rope(og, tabs, NSA_DH // 2), og)
        o_ref[g, 0] = jnp.where(row < nch - 1, og, 0.0).astype(BF16)


def _compress(pos_cmp, inv_nsa, kvc, pe, w1, b1, w2, b2, B, S):
    nch = S // CMP_STRIDE
    x = kvc.reshape(2, B, S, LANES)
    kv = lambda j, b: (j, 0, 0)
    kv4 = lambda j, b: (j, 0, 0, 0)
    return pl.pallas_call(
        _compress_body,
        grid=(2, B),
        in_specs=[
            pl.BlockSpec((1, nch, 1), lambda j, b: (b, 0, 0)),
            pl.BlockSpec((1, LANES), lambda j, b: (0, 0)),
            pl.BlockSpec((1, 1, S, LANES), lambda j, b: (j, b, 0, 0)),
            pl.BlockSpec((1,) + pe.shape[1:], kv),
            pl.BlockSpec((1,) + w1.shape[1:], kv4),
            pl.BlockSpec((1,) + b1.shape[1:], kv),
            pl.BlockSpec((1,) + w2.shape[1:], kv),
            pl.BlockSpec((1,) + b2.shape[1:], kv),
        ],
        out_specs=pl.BlockSpec((NSA_GROUPS, 1, nch, LANES), lambda j, b: (j, b, 0, 0)),
        out_shape=jax.ShapeDtypeStruct((2 * NSA_GROUPS, B, nch, LANES), BF16),
        compiler_params=_params("arbitrary", "arbitrary"),
        name="compress",
    )(pos_cmp, inv_nsa, x, pe, w1, b1, w2, b2)


def _compress_weights(k_pos, k_w1, k_b1, k_w2, k_b2, v_pos, v_w1, v_b1, v_w2, v_b2):
    def one(pos, w1, b1, w2, b2):
        w1l = w1.reshape(CMP_LEN, NSA_DH, CMP_HIDDEN)
        z1 = jnp.zeros_like(w1l)
        w1bd = jnp.concatenate([jnp.concatenate([w1l, z1], axis=2),
                                jnp.concatenate([z1, w1l], axis=2)], axis=1)
        w2p = _pad_lanes(w2)
        z2 = jnp.zeros_like(w2p)
        w2bd = jnp.concatenate([jnp.concatenate([w2p, z2], axis=1),
                                jnp.concatenate([z2, w2p], axis=1)], axis=0)
        return (jnp.tile(pos, (1, NSA_GROUPS)), w1bd.astype(BF16), jnp.tile(b1, NSA_GROUPS)[None, :],
                w2bd.astype(BF16), jnp.tile(_pad_lanes(b2), NSA_GROUPS)[None, :])
    k = one(k_pos, k_w1, k_b1, k_w2, k_b2)
    v = one(v_pos, v_w1, v_b1, v_w2, v_b2)
    return tuple(jnp.stack([a, b]) for a, b in zip(k, v))


STRIP = 64


def _lane_tile(col, n):
    reps = [col] * (n // LANES)
    if n % LANES:
        reps.append(col[:, :n % LANES])
    return reps[0] if len(reps) == 1 else jnp.concatenate(reps, axis=1)


def _flash_reset(m_ref, acc_ref):
    m_ref[...] = jnp.full(m_ref.shape, NEG_INF, F32)
    acc_ref[...] = jnp.zeros(acc_ref.shape, F32)


def _flash_update(s_ref, v, m_ref, acc_ref, p_ref, mask=None):
    rows, n = s_ref.shape
    for r in range(rows // STRIP):
        rs = slice(STRIP * r, STRIP * (r + 1))
        s = s_ref[rs, :]
        if mask is not None:
            s = jnp.where(mask[rs], s, NEG_INF)
        if m_ref is None:
            p_ref[rs, :] = jnp.exp2(s - jnp.max(s, axis=1, keepdims=True)).astype(BF16)
            continue
        m_old = m_ref[rs, :]
        m_new = jnp.maximum(m_old, jnp.max(s, axis=1, keepdims=True))
        p_ref[rs, :] = jnp.exp2(s - _lane_tile(m_new, n)).astype(BF16)
        acc_ref[rs, :] = jnp.exp2(m_old - m_new) * acc_ref[rs, :]
        m_ref[rs, :] = m_new
    if m_ref is None:
        acc_ref[...] = _dot(p_ref[...], v)
    else:
        acc_ref[...] += _dot(p_ref[...], v)


def _flash_finish(acc):
    return acc * (1.0 / acc[:, ONES_LANE:ONES_LANE + 1])


NSA_TQ = 256
NSA_ROWS = NSA_REP * NSA_TQ
SLC_CHUNK = 512
WIN_SPAN = WINDOW + NSA_TQ
BIAS_LANE0 = LANES


def _nsa_body(q_ref, kvn_ref, kvc_ref, misc_ref, cov_ref, o_ref,
              kaug_ref, score_ref, qbias_ref, sa_ref, sb_ref, pa_ref, pb_ref, ms_ref, accs_ref,
              sw_ref, pw_ref, accw_ref, sc_ref, pn_ref, pc_ref, oc_ref):
    c = pl.program_id(1)
    rows = NSA_ROWS
    t_row = c * NSA_TQ + (lax.broadcasted_iota(jnp.int32, (rows, 1), 0) & (NSA_TQ - 1))
    ncmp = kvc_ref.shape[2]
    nblk = kaug_ref.shape[1] // SLC_LEN

    @pl.when(c == 0)
    def _():
        nkeys = kaug_ref.shape[1]
        key_blk = lax.broadcasted_iota(jnp.int32, (nkeys, LANES), 0) // SLC_LEN
        onehot = jnp.where(key_blk == lax.broadcasted_iota(jnp.int32, (nkeys, LANES), 1), 1.0, 0.0)
        for g in range(NSA_GROUPS):
            kaug_ref[g, :, :BIAS_LANE0] = kvn_ref[0, g]
            kaug_ref[g, :, BIAS_LANE0:] = onehot.astype(BF16)

    qs = [q_ref[0, NSA_REP * g:NSA_REP * (g + 1)].reshape(rows, LANES) for g in range(NSA_GROUPS)]

    cmp_valid = (CMP_STRIDE * lax.broadcasted_iota(jnp.int32, (1, ncmp), 1) + CMP_LEN - 1) <= t_row
    for g in range(NSA_GROUPS):
        sc_ref[g] = _dot_nt(qs[g], kvc_ref[g, 0])
    for g in range(NSA_GROUPS):
        for r in range(rows // STRIP):
            rs = slice(STRIP * r, STRIP * (r + 1))
            s = jnp.where(cmp_valid[rs], sc_ref[g, rs, :], NEG_INF)
            p = jnp.where(cmp_valid[rs], jnp.exp2(s - jnp.max(s, axis=1, keepdims=True)), 0.0)
            l = jnp.sum(p, axis=1, keepdims=True)
            p = p * jnp.where(l > 0.0, 1.0 / l, 0.0)
            pn_ref[g, rs, :] = p
            pc_ref[g, rs, :] = p.astype(BF16)
        oc_ref[g] = _dot(pc_ref[g], kvc_ref[NSA_GROUPS + g, 0])

    def select(nb):
        width = NSA_GROUPS * NSA_TQ
        ps = jnp.concatenate(
            [sum(pn_ref[g, NSA_TQ * r:NSA_TQ * (r + 1), :] for r in range(NSA_REP))
             for g in range(NSA_GROUPS)], axis=0)
        hi = ps.astype(BF16)
        lo = (ps - hi.astype(F32)).astype(BF16)
        imp = _dot_nt(cov_ref[:nb, :], hi) + _dot_nt(cov_ref[:nb, :], lo)
        jidx = lax.broadcasted_iota(jnp.int32, (nb, width), 0)
        lane_q = lax.broadcasted_iota(jnp.int32, (1, width), 1) & (NSA_TQ - 1)
        cur = c * (NSA_TQ // SLC_LEN) + lane_q // SLC_LEN
        forced = (jidx == 0) | (jidx == cur) | (jidx == cur - 1)
        score = jnp.where(jidx <= cur, jnp.where(forced, FORCE_BONUS, imp), NEG_INF)
        score_ref[:nb, :] = score
        sub = 8
        cnt = [jnp.zeros((sub, width), F32) for _ in range(nb // sub)]
        tiles = [score[sub * v:sub * (v + 1)] for v in range(nb // sub)]
        sidx = lax.broadcasted_iota(jnp.int32, (sub, width), 0)
        for jp in range(nb):
            rowv = jnp.broadcast_to(score_ref[jp:jp + 1, :], (sub, width))
            for v in range(nb // sub):
                if sub * v > jp:
                    cnt[v] = jnp.where(rowv >= tiles[v], cnt[v] + 1.0, cnt[v])
                elif sub * v + sub - 1 <= jp:
                    cnt[v] = jnp.where(rowv > tiles[v], cnt[v] + 1.0, cnt[v])
                else:
                    ge = jnp.where(rowv >= tiles[v], cnt[v] + 1.0, cnt[v])
                    gt = jnp.where(rowv > tiles[v], cnt[v] + 1.0, cnt[v])
                    cnt[v] = jnp.where(sidx + sub * v > jp, ge, gt)
        rank = jnp.concatenate(cnt, axis=0)
        bias = jnp.where(rank < float(TOPN), 0.0, NEG_INF)
        bias = jnp.concatenate([bias, jnp.zeros((LANES - nb, width), F32)], axis=0)
        qbias_ref[...] = bias.T.astype(BF16)

    visible = (c + 1) * (NSA_TQ // SLC_LEN)

    @pl.when(visible <= TOPN)
    def _():
        qbias_ref[...] = jnp.zeros(qbias_ref.shape, BF16)

    bounds = [TOPN] + [nb for nb in (nblk // 2, 3 * nblk // 4) if TOPN < nb < nblk] + [nblk]
    for lo_nb, nb in zip(bounds[:-1], bounds[1:]):
        pl.when((visible > lo_nb) & (visible <= nb))(functools.partial(select, nb))

    _flash_reset(ms_ref, accs_ref)
    qas = [jnp.concatenate(
        [qs[g], jnp.concatenate([qbias_ref[NSA_TQ * g:NSA_TQ * (g + 1), :]] * NSA_REP, axis=0)],
        axis=1) for g in range(NSA_GROUPS)]
    last = c // (SLC_CHUNK // NSA_TQ)

    def slc_scores(kc, dst):
        k0 = pl.multiple_of(kc * SLC_CHUNK, SLC_CHUNK)
        for g in range(NSA_GROUPS):
            dst[g] = _dot_nt(qas[g], kaug_ref[g, pl.ds(k0, SLC_CHUNK), :])

    def slc_update(src, p_ref, kc, causal):
        k0 = pl.multiple_of(kc * SLC_CHUNK, SLC_CHUNK)
        mask = None
        if causal:
            mask = (k0 + lax.broadcasted_iota(jnp.int32, (1, SLC_CHUNK), 1)) <= t_row
        for g in range(NSA_GROUPS):
            _flash_update(src.at[g], kvn_ref[0, 2 + g, pl.ds(k0, SLC_CHUNK), :],
                          ms_ref.at[g], accs_ref.at[g], p_ref.at[g], mask)

    slc_scores(0, sa_ref)

    win_start = pl.multiple_of(jnp.maximum(c * NSA_TQ - WINDOW, 0), NSA_TQ)
    diff = t_row - (win_start + lax.broadcasted_iota(jnp.int32, (1, WIN_SPAN), 1))
    win_valid = (diff >= 0) & (diff < WINDOW)
    for g in range(NSA_GROUPS):
        sw_ref[g] = _dot_nt(qs[g], kvn_ref[0, 4 + g, pl.ds(win_start, WIN_SPAN), :])
    for g in range(NSA_GROUPS):
        _flash_update(sw_ref.at[g], kvn_ref[0, 6 + g, pl.ds(win_start, WIN_SPAN), :],
                      None, accw_ref.at[g], pw_ref.at[g], win_valid)

    def slc_pair(i, carry):
        slc_scores(2 * i + 1, sb_ref)
        slc_update(sa_ref, pa_ref, 2 * i, False)
        slc_scores(2 * i + 2, sa_ref)
        slc_update(sb_ref, pb_ref, 2 * i + 1, False)
        return carry
    lax.fori_loop(0, last // 2, slc_pair, 0)
    tail = 2 * (last // 2)

    @pl.when(last > tail)
    def _():
        slc_scores(tail + 1, sb_ref)
        slc_update(sa_ref, pa_ref, tail, False)
        slc_update(sb_ref, pb_ref, tail + 1, True)

    @pl.when(last == tail)
    def _():
        slc_update(sa_ref, pa_ref, tail, True)

    sig = 1.0 / (1.0 + jnp.exp(-misc_ref[...]))
    outs = []
    for g in range(NSA_GROUPS):
        for r in range(NSA_REP):
            lane0 = GATE_LANE0 + 3 * (NSA_REP * g + r)
            rs = slice(NSA_TQ * r, NSA_TQ * (r + 1))
            o = (sig[:, lane0:lane0 + 1] * oc_ref[g, rs, :]
                 + sig[:, lane0 + 1:lane0 + 2] * _flash_finish(accs_ref[g, rs, :])
                 + sig[:, lane0 + 2:lane0 + 3] * _flash_finish(accw_ref[g, rs, :]))
            outs.append(o[:, :NSA_DH])
    o_ref[0] = jnp.concatenate(outs, axis=1).astype(BF16)


def _nsa(qn, kvn, kvcmp, misc, cov_t, B, S):
    nblk = S // SLC_LEN
    ncmp = S // CMP_STRIDE
    nq = S // NSA_TQ
    rows = NSA_ROWS
    return pl.pallas_call(
        _nsa_body,
        grid=(B, nq),
        in_specs=[
            pl.BlockSpec((1, NSA_HEADS, NSA_TQ, LANES), lambda b, c: (b, 0, c, 0)),
            pl.BlockSpec((1, 8, S, LANES), lambda b, c: (b, 0, 0, 0)),
            pl.BlockSpec((4, 1, ncmp, LANES), lambda b, c: (0, b, 0, 0)),
            pl.BlockSpec((NSA_TQ, LANES), lambda b, c: (b * nq + c, 0)),
            pl.BlockSpec(cov_t.shape, lambda b, c: (0, 0)),
        ],
        out_specs=pl.BlockSpec((1, NSA_TQ, NSA_HEADS * NSA_DH), lambda b, c: (b, c, 0)),
        out_shape=jax.ShapeDtypeStruct((B, S, NSA_HEADS * NSA_DH), BF16),
        scratch_shapes=[
            pltpu.VMEM((NSA_GROUPS, S, 2 * LANES), BF16),
            pltpu.VMEM((nblk, NSA_GROUPS * NSA_TQ), F32),
            pltpu.VMEM((NSA_GROUPS * NSA_TQ, LANES), BF16),
            pltpu.VMEM((NSA_GROUPS, rows, SLC_CHUNK), F32),
            pltpu.VMEM((NSA_GROUPS, rows, SLC_CHUNK), F32),
            pltpu.VMEM((NSA_GROUPS, rows, SLC_CHUNK), BF16),
            pltpu.VMEM((NSA_GROUPS, rows, SLC_CHUNK), BF16),
            pltpu.VMEM((NSA_GROUPS, rows, LANES), F32),
            pltpu.VMEM((NSA_GROUPS, rows, LANES), F32),
            pltpu.VMEM((NSA_GROUPS, rows, WIN_SPAN), F32),
            pltpu.VMEM((NSA_GROUPS, rows, WIN_SPAN), BF16),
            pltpu.VMEM((NSA_GROUPS, rows, LANES), F32),
            pltpu.VMEM((NSA_GROUPS, rows, ncmp), F32),
            pltpu.VMEM((NSA_GROUPS, rows, ncmp), F32),
            pltpu.VMEM((NSA_GROUPS, rows, ncmp), BF16),
            pltpu.VMEM((NSA_GROUPS, rows, LANES), F32),
        ],
        compiler_params=_params("arbitrary", "arbitrary"),
        name="nsa",
    )(qn, kvn, kvcmp, misc, cov_t)


MLA_TQ = 512
MLA_CHUNK = 512
MLA_HPB = 4


def _mla_body(q_ref, k_ref, v_ref, o_ref, sa_ref, sb_ref, pa_ref, pb_ref, m_ref, acc_ref):
    qi = pl.program_id(2)
    t_row = qi * MLA_TQ + lax.broadcasted_iota(jnp.int32, (MLA_TQ, 1), 0)
    _flash_reset(m_ref, acc_ref)

    def scores(kc, dst):
        k0 = pl.multiple_of(kc * MLA_CHUNK, MLA_CHUNK)
        for j in range(MLA_HPB):
            dst[j] = _dot_nt(q_ref[0, j], k_ref[0, j, pl.ds(k0, MLA_CHUNK), :])

    def update(src, p_ref, kc, causal):
        k0 = pl.multiple_of(kc * MLA_CHUNK, MLA_CHUNK)
        mask = None
        if causal:
            mask = (k0 + lax.broadcasted_iota(jnp.int32, (1, MLA_CHUNK), 1)) <= t_row
        for j in range(MLA_HPB):
            _flash_update(src.at[j], v_ref[0, j, pl.ds(k0, MLA_CHUNK), :],
                          m_ref.at[j], acc_ref.at[j], p_ref.at[j], mask)

    scores(0, sa_ref)

    def pair(i, carry):
        scores(2 * i + 1, sb_ref)
        update(sa_ref, pa_ref, 2 * i, False)
        scores(2 * i + 2, sa_ref)
        update(sb_ref, pb_ref, 2 * i + 1, False)
        return carry
    lax.fori_loop(0, qi // 2, pair, 0)
    tail = 2 * (qi // 2)

    @pl.when(qi > tail)
    def _():
        scores(tail + 1, sb_ref)
        update(sa_ref, pa_ref, tail, False)
        update(sb_ref, pb_ref, tail + 1, True)

    @pl.when(qi == tail)
    def _():
        update(sa_ref, pa_ref, tail, True)
    o_ref[0] = jnp.concatenate([_flash_finish(acc_ref[j])[:, :MLA_V] for j in range(MLA_HPB)],
                               axis=1).astype(BF16)


def _mla(q, k, v, B, S):
    return pl.pallas_call(
        _mla_body,
        grid=(B, MLA_HEADS // MLA_HPB, S // MLA_TQ),
        in_specs=[
            pl.BlockSpec((1, MLA_HPB, MLA_TQ, LANES), lambda b, h, i: (b, h, i, 0)),
            pl.BlockSpec((1, MLA_HPB, S, LANES), lambda b, h, i: (b, h, 0, 0)),
            pl.BlockSpec((1, MLA_HPB, S, LANES), lambda b, h, i: (b, h, 0, 0)),
        ],
        out_specs=pl.BlockSpec((1, MLA_TQ, MLA_HPB * MLA_V), lambda b, h, i: (b, i, h)),
        out_shape=jax.ShapeDtypeStruct((B, S, MLA_HEADS * MLA_V), BF16),
        scratch_shapes=[
            pltpu.VMEM((MLA_HPB, MLA_TQ, MLA_CHUNK), F32),
            pltpu.VMEM((MLA_HPB, MLA_TQ, MLA_CHUNK), F32),
            pltpu.VMEM((MLA_HPB, MLA_TQ, MLA_CHUNK), BF16),
            pltpu.VMEM((MLA_HPB, MLA_TQ, MLA_CHUNK), BF16),
            pltpu.VMEM((MLA_HPB, MLA_TQ, LANES), F32),
            pltpu.VMEM((MLA_HPB, MLA_TQ, LANES), F32),
        ],
        compiler_params=_params("arbitrary", "arbitrary", "arbitrary"),
        name="mla",
    )(q, k, v)


def _mem_kv_body(m_ref, wk_ref, wv_ref, k_ref, v_ref):
    mb = m_ref[...].astype(BF16)
    k_ref[...] = _dot(mb, wk_ref[...]).astype(BF16)
    v_ref[...] = _dot(mb, wv_ref[...]).astype(BF16)


def _mem_kv(memf, wk, wv, tm):
    R, D = memf.shape
    tok = lambda i: (i, 0)
    const = lambda i: (0, 0)
    return pl.pallas_call(
        _mem_kv_body,
        grid=(R // tm,),
        in_specs=[pl.BlockSpec((tm, D), tok), pl.BlockSpec(wk.shape, const),
                  pl.BlockSpec(wv.shape, const)],
        out_specs=[pl.BlockSpec((tm, D), tok), pl.BlockSpec((tm, D), tok)],
        out_shape=[jax.ShapeDtypeStruct((R, D), BF16), jax.ShapeDtypeStruct((R, D), BF16)],
        compiler_params=_params("arbitrary"),
        name="mem_kv",
    )(memf, wk, wv)


def _mem_attn_body(x0_ref, on_ref, om_ref, wmix_ref, g1_ref, b1_ref,
                   k_ref, v_ref, wq_ref, wo_ref, g_ref, b_ref, o_ref, *, alpha):
    half = on_ref.shape[1]
    mix = _dot(on_ref[...], wmix_ref[:half, :]) + _dot(om_ref[...], wmix_ref[half:, :])
    x = _layer_norm(alpha * x0_ref[...] + mix, g1_ref[...], b1_ref[...])
    D = x.shape[1]
    dh = D // MEM_HEADS
    q = (_dot(x.astype(BF16), wq_ref[...]) * (dh ** -0.5 * LOG2E)).astype(BF16)
    outs = []
    for h in range(MEM_HEADS):
        cs = slice(dh * h, dh * (h + 1))
        s = _dot_nt(q[:, cs], k_ref[0, :, cs])
        p = jnp.exp2(s - jnp.max(s, axis=1, keepdims=True))
        l = jnp.sum(p, axis=1, keepdims=True)
        outs.append((_dot(p.astype(BF16), v_ref[0, :, cs]) * (1.0 / l)).astype(BF16))
    o = jnp.concatenate(outs, axis=1)
    y = _dot(o, wo_ref[...])
    o_ref[...] = _layer_norm(alpha * x + y, g_ref[...], b_ref[...])


def _mem_attn(xf, o_nsa, o_mla, w_o, g1, b1, k_mem, v_mem, wq, wo, g, b, alpha, S, tm):
    T, D = xf.shape
    nst = S // tm
    M = k_mem.shape[1]
    tok = lambda i: (i, 0)
    const = lambda i: (0, 0)
    memb = lambda i: (i // nst, 0, 0)
    return pl.pallas_call(
        functools.partial(_mem_attn_body, alpha=alpha),
        grid=(T // tm,),
        in_specs=[
            pl.BlockSpec((tm, D), tok),
            pl.BlockSpec((tm, o_nsa.shape[1]), tok),
            pl.BlockSpec((tm, o_mla.shape[1]), tok),
            pl.BlockSpec(w_o.shape, const),
            pl.BlockSpec((1, D), const),
            pl.BlockSpec((1, D), const),
            pl.BlockSpec((1, M, D), memb),
            pl.BlockSpec((1, M, D), memb),
            pl.BlockSpec(wq.shape, const),
            pl.BlockSpec(wo.shape, const),
            pl.BlockSpec((1, D), const),
            pl.BlockSpec((1, D), const),
        ],
        out_specs=pl.BlockSpec((tm, D), tok),
        out_shape=jax.ShapeDtypeStruct((T, D), F32),
        compiler_params=_params("arbitrary"),
        name="mem_attn",
    )(xf, o_nsa, o_mla, w_o, g1, b1, k_mem, v_mem, wq, wo, g, b)


HALO = 8


FFN_SLAB = 256
FFN_TM = 512


def _ffn_body(x_ref, xh_ref, wg_ref, wu_ref, cw_ref, cb_ref, wd_ref, g_ref, b_ref, o_ref,
              act_ref, *, alpha, seq_tiles):
    i = pl.program_id(0)
    x = x_ref[...]
    xb = x.astype(BF16)
    xhb = xh_ref[...].astype(BF16)
    tm = x.shape[0]
    row = lax.broadcasted_iota(jnp.int32, (tm, 1), 0)
    seq_start = i % seq_tiles == 0
    for c0 in range(0, wg_ref.shape[1], FFN_SLAB):
        cs = slice(c0, c0 + FFN_SLAB)
        gate = _dot(xb, wg_ref[:, cs])
        up = _dot(xb, wu_ref[:, cs])
        halo = jnp.where(seq_start, 0.0, _dot(xhb, wg_ref[:, cs]))
        g1 = jnp.where(row == 0, halo[HALO - 1:HALO], pltpu.roll(gate, 1, 0))
        g2 = jnp.where(row == 0, halo[HALO - 2:HALO - 1],
                       jnp.where(row == 1, halo[HALO - 1:HALO], pltpu.roll(gate, 2, 0)))
        conv = cw_ref[0:1, cs] * g2 + cw_ref[1:2, cs] * g1 + cw_ref[2:3, cs] * gate + cb_ref[:, cs]
        act_ref[:, cs] = (conv * (1.0 / (1.0 + jnp.exp(-conv))) * up).astype(BF16)
    y = _dot(act_ref[...], wd_ref[...])
    o_ref[...] = _layer_norm(alpha * x + y, g_ref[...], b_ref[...])


def _ffn(xf, wg, wu, cw, cb, wd, g, b, alpha, S, tm):
    T, D = xf.shape
    dff = wg.shape[1]
    assert dff % FFN_SLAB == 0
    tok = lambda i: (i, 0)
    const = lambda i: (0, 0)
    return pl.pallas_call(
        functools.partial(_ffn_body, alpha=alpha, seq_tiles=S // tm),
        grid=(T // tm,),
        in_specs=[
            pl.BlockSpec((tm, D), tok),
            pl.BlockSpec((HALO, D), lambda i: (jnp.maximum(i * (tm // HALO) - 1, 0), 0)),
            pl.BlockSpec((D, dff), const),
            pl.BlockSpec((D, dff), const),
            pl.BlockSpec((CONV_WIDTH, dff), const),
            pl.BlockSpec((1, dff), const),
            pl.BlockSpec((dff, D), const),
            pl.BlockSpec((1, D), const),
            pl.BlockSpec((1, D), const),
        ],
        out_specs=pl.BlockSpec((tm, D), tok),
        out_shape=jax.ShapeDtypeStruct((T, D), F32),
        scratch_shapes=[pltpu.VMEM((tm, dff), BF16)],
        compiler_params=_params("arbitrary"),
        name="ffn",
    )(xf, xf, wg, wu, cw, cb, wd, g, b)


def _inv_freq_row(dim, lane_lo, lane_hi, period):
    inv = ROPE_THETA ** (-np.arange(0, dim, 2, dtype=np.float64) / dim)
    row = np.zeros((1, LANES), np.float32)
    for lane in range(lane_lo, lane_hi):
        row[0, lane] = inv[(lane % period) % (dim // 2)]
    return jnp.asarray(row)


def _cover_t(S):
    nc = S // CMP_STRIDE
    ns = S // SLC_LEN
    cs = np.arange(nc)[:, None] * CMP_STRIDE
    ss = np.arange(ns)[None, :] * SLC_LEN
    cover = np.clip(np.minimum(cs + CMP_LEN, ss + SLC_LEN) - np.maximum(cs, ss), 0, None) / CMP_LEN
    cover[nc - 1:] = 0.0
    return jnp.asarray(cover.T, dtype=BF16)


def _permute_w_in(w):
    D = w.shape[0]
    c1 = NSA_HEADS * NSA_DH
    c2 = c1 + 3 * 2 * NSA_GROUPS * NSA_DH
    c3 = c2 + 3 * NSA_HEADS
    c4 = c3 + MLA_Q_RANK
    c5 = c4 + MLA_KV_RANK
    c6 = c5 + MLA_ROPE
    half = MLA_ROPE // 2
    z = lambda n: jnp.zeros((D, n), w.dtype)
    misc = jnp.concatenate(
        [w[:, c5 + half:c6], z(GATE_LANE0 - half), w[:, c2:c3], z(MLA_PE1 - GATE_LANE0 - (c3 - c2)),
         w[:, c5:c5 + half], z(LANES - MLA_PE1 - half)], axis=1)
    return jnp.concatenate([w[:, :c2], w[:, c3:c5], misc], axis=1).astype(BF16)


def _mla_head_lanes(nope, pe):
    r, H, _ = nope.shape
    half = MLA_ROPE // 2
    split = MLA_PE1 - half
    pad = jnp.zeros((r, H, LANES - MLA_NOPE - MLA_ROPE), nope.dtype)
    return jnp.concatenate([pe[..., half:], nope[..., :split], pe[..., :half], nope[..., split:], pad],
                           axis=2).reshape(r, H * LANES)


def _permute_w_uq(w):
    r = w.shape[0]
    w3 = w.reshape(r, MLA_HEADS, MLA_NOPE + MLA_ROPE)
    return _mla_head_lanes(w3[..., :MLA_NOPE], w3[..., MLA_NOPE:]).astype(BF16)


def _permute_w_ukv(w):
    r = w.shape[0]
    w3 = w.reshape(r, MLA_HEADS, MLA_NOPE + MLA_V)
    k = _mla_head_lanes(w3[..., :MLA_NOPE], jnp.zeros((r, MLA_HEADS, MLA_ROPE), w.dtype))
    v = _pad_lanes(w3[..., MLA_NOPE:]).reshape(r, MLA_HEADS * LANES)
    return jnp.concatenate([k, v], axis=1).astype(BF16)


def _pad_lanes(a):
    return jnp.concatenate([a, jnp.zeros(a.shape[:-1] + (LANES - a.shape[-1],), a.dtype)], axis=-1)


def kernel(x, mem, positions, w_in, nsa_k_pos, nsa_ck_w1, nsa_ck_b1, nsa_ck_w2, nsa_ck_b2,
           nsa_v_pos, nsa_cv_w1, nsa_cv_b1, nsa_cv_w2, nsa_cv_b2,
           mla_q_norm, mla_w_uq, mla_kv_norm, mla_w_ukv, w_o, ln1_g, ln1_b,
           mem_wq, mem_wk, mem_wv, mem_wo, ln2_g, ln2_b,
           ffn_w_up, ffn_conv_w, ffn_conv_b, ffn_w_down, ln3_g, ln3_b):
    B, S, D = x.shape
    T = B * S
    depth = w_in.shape[0]
    alpha = (2.0 * depth) ** 0.25
    d_ff = ffn_w_down.shape[1]
    tm = min(512, S)
    assert S % MLA_TQ == 0 and S >= WIN_SPAN and S % tm == 0
    assert (B * mem.shape[1]) % 256 == 0

    pos = positions.reshape(T, 1)
    pos_cmp = positions[:, CMP_LEN - 1::CMP_STRIDE]
    pos_cmp = jnp.concatenate([pos_cmp, pos_cmp[:, -1:]], axis=1)[:, :, None]
    inv_cmp = _inv_freq_row(NSA_DH, 0, NSA_DH, NSA_DH)
    inv_tok = (_inv_freq_row(NSA_DH, 0, NSA_DH // 2, NSA_DH)
               + _inv_freq_row(MLA_ROPE, NSA_DH // 2, NSA_DH // 2 + MLA_ROPE // 2, MLA_ROPE // 2))
    cov_t = _cover_t(S)
    memf = mem.reshape(B * mem.shape[1], D)

    xf = x.reshape(T, D)
    for l in range(depth):
        qn, kvn, kvc, misc, q_m, k_m, v_m = _inproj(
            pos, inv_tok, xf, _permute_w_in(w_in[l]), mla_q_norm[l][None, :],
            mla_kv_norm[l][None, :], _permute_w_uq(mla_w_uq[l]), _permute_w_ukv(mla_w_ukv[l]),
            B, S, tm)
        kvcmp = _compress(
            pos_cmp, inv_cmp, kvc,
            *_compress_weights(nsa_k_pos[l], nsa_ck_w1[l], nsa_ck_b1[l], nsa_ck_w2[l], nsa_ck_b2[l],
                               nsa_v_pos[l], nsa_cv_w1[l], nsa_cv_b1[l], nsa_cv_w2[l], nsa_cv_b2[l]),
            B, S)
        o_nsa = _nsa(qn, kvn, kvcmp, misc, cov_t, B, S)
        o_mla = _mla(q_m, k_m, v_m, B, S)
        k_mem, v_mem = _mem_kv(memf, mem_wk[l].astype(BF16), mem_wv[l].astype(BF16), 256)
        xf = _mem_attn(xf, o_nsa.reshape(T, -1), o_mla.reshape(T, -1), w_o[l].astype(BF16),
                       ln1_g[l][None, :], ln1_b[l][None, :],
                       k_mem.reshape(B, -1, D), v_mem.reshape(B, -1, D),
                       mem_wq[l].astype(BF16), mem_wo[l].astype(BF16),
                       ln2_g[l][None, :], ln2_b[l][None, :], alpha, S, tm)
        xf = _ffn(xf, ffn_w_up[l][:, :d_ff].astype(BF16), ffn_w_up[l][:, d_ff:].astype(BF16),
                  ffn_conv_w[l], ffn_conv_b[l][None, :], ffn_w_down[l].astype(BF16),
                  ln3_g[l][None, :], ln3_b[l][None, :], alpha, S, FFN_TM)
    return xf.reshape(B, S, D)
```

```python
import functools
import math

import numpy as np
import jax
import jax.numpy as jnp
from jax import lax
from jax.experimental import pallas as pl
from jax.experimental.pallas import tpu as pltpu

F32 = jnp.float32
BF16 = jnp.bfloat16

NSA_HEADS = 8
NSA_GROUPS = 2
NSA_REP = NSA_HEADS // NSA_GROUPS
NSA_DH = 64
CMP_STRIDE = 16
CMP_LEN = 32
SLC_LEN = 64
TOPN = 16
WINDOW = 512
CMP_HIDDEN = 128
FORCE_BONUS = 1e4
MLA_HEADS = 8
MLA_Q_RANK = 384
MLA_KV_RANK = 256
MLA_NOPE = 64
MLA_ROPE = 32
MLA_V = 64
MEM_HEADS = 4
CONV_WIDTH = 3
ROPE_THETA = 10000.0
LN_EPS = 1e-5
RMS_EPS = 1e-6
NEG_INF = -1e30
LOG2E = math.log2(math.e)

LANES = 128
VMEM_LIMIT = 60 * 1024 * 1024

C_Q = 0
C_KVC = 512
C_KVN = 768
C_LAT = 1280
C_MISC = 1920
IN_COLS_PAD = 2048
GATE_LANE0 = MLA_ROPE
MLA_PE2 = 0
MLA_PE1 = 64
ONES_LANE = 64


def _dot(a, b):
    return jnp.dot(a, b, preferred_element_type=F32)


def _dot_nt(a, b):
    return lax.dot_general(a, b, (((1,), (1,)), ((), ())), preferred_element_type=F32)


def _layer_norm(y, g, b):
    mu = jnp.mean(y, axis=-1, keepdims=True)
    d = y - mu
    var = jnp.mean(d * d, axis=-1, keepdims=True)
    return d * lax.rsqrt(var + LN_EPS) * g + b


def _params(*sem):
    return pltpu.CompilerParams(dimension_semantics=sem, vmem_limit_bytes=VMEM_LIMIT)


def _rope_tables(pos_col, inv_row, half):
    ang = pos_col * inv_row
    cos = jnp.cos(ang)
    sin = jnp.sin(ang)
    lane = lax.broadcasted_iota(jnp.int32, (1, LANES), 1)
    upper = (lane & (2 * half - 1)) >= half
    rot = inv_row != 0.0
    sin_hi = jnp.where(upper & rot, sin, 0.0)
    sin_lo = jnp.where(upper | (~rot), 0.0, -sin)
    return cos, sin_hi, sin_lo


def _apply_rope(v, tabs, half):
    cos, sin_hi, sin_lo = tabs
    return v * cos + pltpu.roll(v, half, 1) * sin_hi + pltpu.roll(v, LANES - half, 1) * sin_lo


def _rms_norm(v, g):
    return v * lax.rsqrt(jnp.mean(v * v, axis=-1, keepdims=True) + RMS_EPS) * g


def _inproj_body(pos_ref, inv_ref, x_ref, w_ref, gq_ref, gkv_ref, wq_ref, wkv_ref,
                 qn_ref, kvn_ref, kvc_ref, misc_ref, qm_ref, km_ref, vm_ref):
    xb = x_ref[...].astype(BF16)
    lane = lax.broadcasted_iota(jnp.int32, (1, LANES), 1)
    low = lane < NSA_DH
    qscale = NSA_DH ** -0.5 * LOG2E

    ang = pos_ref[...].astype(F32) * inv_ref[...]
    cos, sin = jnp.cos(ang), jnp.sin(ang)
    nf, mf = NSA_DH // 2, MLA_ROPE // 2

    def tile_nsa(t):
        t = jnp.where(lane < nf, t, 0.0)
        t = t + pltpu.roll(t, nf, 1)
        return t + pltpu.roll(t, 2 * nf, 1)

    def place_mla(t):
        t = jnp.where((lane >= nf) & (lane < nf + mf), t, 0.0)
        return pltpu.roll(t, MLA_PE1 - nf, 1) + pltpu.roll(t, LANES + MLA_PE2 - nf, 1)

    upper = (lane & (NSA_DH - 1)) >= nf
    sin_n = tile_nsa(sin)
    tabs = (tile_nsa(cos), jnp.where(upper, sin_n, 0.0), jnp.where(upper, 0.0, -sin_n))
    pe1 = (lane >= MLA_PE1) & (lane < MLA_PE1 + mf)
    pe2 = (lane >= MLA_PE2) & (lane < MLA_PE2 + mf)
    pe_lanes = pe1 | pe2
    sin_m = place_mla(sin)
    cos_m = jnp.where(pe_lanes, place_mla(cos), 1.0)
    sin_m = jnp.where(pe1, -sin_m, jnp.where(pe2, sin_m, 0.0))

    def rope_mla(v):
        return v * cos_m + pltpu.roll(v, LANES // 2, 1) * sin_m

    def proj(c0, n):
        return _dot(xb, w_ref[:, c0:c0 + n])

    def split_store(v, ref, idx_lo, idx_hi, pad=0.0):
        ref[0, idx_lo] = jnp.where(low, v, pad).astype(BF16)
        ref[0, idx_hi] = jnp.where(low, pltpu.roll(v, NSA_DH, 1), pad).astype(BF16)

    ones_pad = jnp.where(lane == ONES_LANE, 1.0, 0.0)

    for slab in range(2):
        h = proj(C_Q + 256 * slab, 256)
        for j in range(2):
            r = _apply_rope(h[:, LANES * j:LANES * (j + 1)], tabs, NSA_DH // 2) * qscale
            split_store(r, qn_ref, 4 * slab + 2 * j, 4 * slab + 2 * j + 1)

    h = proj(C_KVC, 256)
    kvc_ref[0] = h[:, :LANES]
    kvc_ref[1] = h[:, LANES:]

    for slab in range(2):
        h = proj(C_KVN + 256 * slab, 256)
        k = _apply_rope(h[:, :LANES], tabs, NSA_DH // 2)
        split_store(k, kvn_ref, 4 * slab, 4 * slab + 1)
        split_store(h[:, LANES:], kvn_ref, 4 * slab + 2, 4 * slab + 3, ones_pad)

    lat = [proj(C_LAT + 256 * i, 256) for i in range((IN_COLS_PAD - C_LAT) // 256)]
    misc = lat[2][:, LANES:]
    misc_ref[...] = misc
    mq = jnp.concatenate([lat[0], lat[1][:, :LANES]], axis=1)
    mkv = jnp.concatenate([lat[1][:, LANES:], lat[2][:, :LANES]], axis=1)
    mscale = (MLA_NOPE + MLA_ROPE) ** -0.5 * LOG2E
    qn = _rms_norm(mq, gq_ref[...]).astype(BF16)
    kvn = _rms_norm(mkv, gkv_ref[...]).astype(BF16)
    kpe = jnp.where(pe_lanes, rope_mla(misc), 0.0)
    kcols = MLA_HEADS * LANES
    for slab in range(MLA_HEADS // 2):
        cs = slice(256 * slab, 256 * (slab + 1))
        hq = _dot(qn, wq_ref[:, cs])
        hk = _dot(kvn, wkv_ref[:, cs])
        hv = _dot(kvn, wkv_ref[:, kcols + 256 * slab:kcols + 256 * (slab + 1)])
        for j in range(2):
            ls = slice(LANES * j, LANES * (j + 1))
            qm_ref[0, 2 * slab + j] = (rope_mla(hq[:, ls]) * mscale).astype(BF16)
            km_ref[0, 2 * slab + j] = (hk[:, ls] + kpe).astype(BF16)
            vm_ref[0, 2 * slab + j] = (hv[:, ls] + ones_pad).astype(BF16)


def _inproj(pos, inv_row, xf, w_in_p, gq, gkv, wq_p, wkv_p, B, S, tm):
    T = B * S
    nst = S // tm
    tok = lambda i: (i, 0)
    const = lambda i: (0, 0)
    head_blk = lambda i: (i // nst, 0, i % nst, 0)
    heads = jax.ShapeDtypeStruct((B, 8, S, LANES), BF16)
    return pl.pallas_call(
        _inproj_body,
        grid=(T // tm,),
        in_specs=[
            pl.BlockSpec((tm, 1), tok),
            pl.BlockSpec((1, LANES), const),
            pl.BlockSpec((tm, xf.shape[1]), tok),
            pl.BlockSpec(w_in_p.shape, const),
            pl.BlockSpec(gq.shape, const),
            pl.BlockSpec(gkv.shape, const),
            pl.BlockSpec(wq_p.shape, const),
            pl.BlockSpec(wkv_p.shape, const),
        ],
        out_specs=[
            pl.BlockSpec((1, 8, tm, LANES), head_blk),
            pl.BlockSpec((1, 8, tm, LANES), head_blk),
            pl.BlockSpec((2, tm, LANES), lambda i: (0, i, 0)),
            pl.BlockSpec((tm, LANES), tok),
            pl.BlockSpec((1, 8, tm, LANES), head_blk),
            pl.BlockSpec((1, 8, tm, LANES), head_blk),
            pl.BlockSpec((1, 8, tm, LANES), head_blk),
        ],
        out_shape=[
            heads,
            heads,
            jax.ShapeDtypeStruct((2, T, LANES), F32),
            jax.ShapeDtypeStruct((T, LANES), F32),
            heads,
            heads,
            heads,
        ],
        compiler_params=_params("arbitrary"),
        name="inproj",
    )(pos, inv_row, xf, w_in_p, gq, gkv, wq_p, wkv_p)


def _compress_body(pos_ref, inv_ref, x_ref, pe_ref, w1_ref, b1_ref, w2_ref, b2_ref, o_ref):
    is_k = pl.program_id(0) == 0
    nch = o_ref.shape[2]
    a1 = jnp.zeros((nch, NSA_GROUPS * CMP_HIDDEN), F32)
    a2 = jnp.zeros((nch, NSA_GROUPS * CMP_HIDDEN), F32)
    for l in range(CMP_STRIDE):
        xl = x_ref.at[0, 0][pl.ds(l, nch, stride=CMP_STRIDE), :]
        a1 = a1 + _dot((xl + pe_ref[0, l:l + 1, :]).astype(BF16), w1_ref[0, l])
        a2 = a2 + _dot((xl + pe_ref[0, CMP_STRIDE + l:CMP_STRIDE + l + 1, :]).astype(BF16),
                       w1_ref[0, CMP_STRIDE + l])
    pre = a1 + pltpu.roll(a2, nch - 1, 0) + b1_ref[0]
    hid = jax.nn.gelu(pre, approximate=True)
    out = _dot(hid.astype(BF16), w2_ref[0]) + b2_ref[0]
    tabs = _rope_tables(pos_ref[0].astype(F32), inv_ref[...], NSA_DH // 2)
    row = lax.broadcasted_iota(jnp.int32, (nch, 1), 0)
    for g in range(NSA_GROUPS):
        og = out[:, LANES * g:LANES * (g + 1)]
        og = jnp.where(is_k, _apply_rope(og, tabs, NSA_DH // 2), og)
        o_ref[g, 0] = jnp.where(row < nch - 1, og, 0.0).astype(BF16)


def _compress(pos_cmp, inv_nsa, kvc, pe, w1, b1, w2, b2, B, S):
    nch = S // CMP_STRIDE
    x = kvc.reshape(2, B, S, LANES)
    kv = lambda j, b: (j, 0, 0)
    kv4 = lambda j, b: (j, 0, 0, 0)
    return pl.pallas_call(
        _compress_body,
        grid=(2, B),
        in_specs=[
            pl.BlockSpec((1, nch, 1), lambda j, b: (b, 0, 0)),
            pl.BlockSpec((1, LANES), lambda j, b: (0, 0)),
            pl.BlockSpec((1, 1, S, LANES), lambda j, b: (j, b, 0, 0)),
            pl.BlockSpec((1,) + pe.shape[1:], kv),
            pl.BlockSpec((1,) + w1.shape[1:], kv4),
            pl.BlockSpec((1,) + b1.shape[1:], kv),
            pl.BlockSpec((1,) + w2.shape[1:], kv),
            pl.BlockSpec((1,) + b2.shape[1:], kv),
        ],
        out_specs=pl.BlockSpec((NSA_GROUPS, 1, nch, LANES), lambda j, b: (j, b, 0, 0)),
        out_shape=jax.ShapeDtypeStruct((2 * NSA_GROUPS, B, nch, LANES), BF16),
        compiler_params=_params("arbitrary", "arbitrary"),
        name="compress",
    )(pos_cmp, inv_nsa, x, pe, w1, b1, w2, b2)


def _compress_weights(k_pos, k_w1, k_b1, k_w2, k_b2, v_pos, v_w1, v_b1, v_w2, v_b2):
    def one(pos, w1, b1, w2, b2):
        w1l = w1.reshape(CMP_LEN, NSA_DH, CMP_HIDDEN)
        z1 = jnp.zeros_like(w1l)
        w1bd = jnp.concatenate([jnp.concatenate([w1l, z1], axis=2),
                                jnp.concatenate([z1, w1l], axis=2)], axis=1)
        w2p = _pad_lanes(w2)
        z2 = jnp.zeros_like(w2p)
        w2bd = jnp.concatenate([jnp.concatenate([w2p, z2], axis=1),
                                jnp.concatenate([z2, w2p], axis=1)], axis=0)
        return (jnp.tile(pos, (1, NSA_GROUPS)), w1bd.astype(BF16), jnp.tile(b1, NSA_GROUPS)[None, :],
                w2bd.astype(BF16), jnp.tile(_pad_lanes(b2), NSA_GROUPS)[None, :])
    k = one(k_pos, k_w1, k_b1, k_w2, k_b2)
    v = one(v_pos, v_w1, v_b1, v_w2, v_b2)
    return tuple(jnp.stack([a, b]) for a, b in zip(k, v))


STRIP = 64


def _lane_tile(col, n):
    reps = [col] * (n // LANES)
    if n % LANES:
        reps.append(col[:, :n % LANES])
    return reps[0] if len(reps) == 1 else jnp.concatenate(reps, axis=1)


def _flash_reset(m_ref, acc_ref):
    m_ref[...] = jnp.full(m_ref.shape, NEG_INF, F32)
    acc_ref[...] = jnp.zeros(acc_ref.shape, F32)


def _flash_update(s_ref, v, m_ref, acc_ref, p_ref, mask=None):
    rows, n = s_ref.shape
    for r in range(rows // STRIP):
        rs = slice(STRIP * r, STRIP * (r + 1))
        s = s_ref[rs, :]
        if isinstance(mask, (list, tuple)):
            parts, k0 = [], 0
            for width, seg in mask:
                part = s[:, k0:k0 + width]
                parts.append(part if seg is None else jnp.where(seg[rs], part, NEG_INF))
                k0 += width
            s = jnp.concatenate(parts, axis=1)
        elif mask is not None:
            s = jnp.where(mask[rs], s, NEG_INF)
        if m_ref is None:
            p_ref[rs, :] = jnp.exp2(s - jnp.max(s, axis=1, keepdims=True)).astype(BF16)
            continue
        m_old = m_ref[rs, :]
        m_new = jnp.maximum(m_old, jnp.max(s, axis=1, keepdims=True))
        p_ref[rs, :] = jnp.exp2(s - _lane_tile(m_new, n)).astype(BF16)
        acc_ref[rs, :] = jnp.exp2(m_old - m_new) * acc_ref[rs, :]
        m_ref[rs, :] = m_new
    if m_ref is None:
        acc_ref[...] = _dot(p_ref[...], v)
    else:
        acc_ref[...] += _dot(p_ref[...], v)


def _flash_finish(acc):
    return acc * (1.0 / acc[:, ONES_LANE:ONES_LANE + 1])


NSA_TQ = 256
NSA_ROWS = NSA_REP * NSA_TQ
SLC_CHUNK = 512
WIN_SPAN = WINDOW + NSA_TQ
BIAS_LANE0 = LANES


def _nsa_body(q_ref, kvn_ref, kvc_ref, misc_ref, cov_ref, o_ref,
              kaug_ref, score_ref, qbias_ref, sa_ref, sb_ref, pa_ref, pb_ref, ms_ref, accs_ref,
              sw_ref, pw_ref, accw_ref, sc_ref, pn_ref, pc_ref, oc_ref):
    c = pl.program_id(1)
    rows = NSA_ROWS
    t_row = c * NSA_TQ + (lax.broadcasted_iota(jnp.int32, (rows, 1), 0) & (NSA_TQ - 1))
    ncmp = kvc_ref.shape[2]
    nblk = kaug_ref.shape[1] // SLC_LEN

    @pl.when(c == 0)
    def _():
        nkeys = kaug_ref.shape[1]
        key_blk = lax.broadcasted_iota(jnp.int32, (nkeys, LANES), 0) // SLC_LEN
        onehot = jnp.where(key_blk == lax.broadcasted_iota(jnp.int32, (nkeys, LANES), 1), 1.0, 0.0)
        for g in range(NSA_GROUPS):
            kaug_ref[g, :, :BIAS_LANE0] = kvn_ref[0, g]
            kaug_ref[g, :, BIAS_LANE0:] = onehot.astype(BF16)

    qs = [q_ref[0, NSA_REP * g:NSA_REP * (g + 1)].reshape(rows, LANES) for g in range(NSA_GROUPS)]

    cmp_valid = (CMP_STRIDE * lax.broadcasted_iota(jnp.int32, (1, ncmp), 1) + CMP_LEN - 1) <= t_row
    for g in range(NSA_GROUPS):
        sc_ref[g] = _dot_nt(qs[g], kvc_ref[g, 0])
    for g in range(NSA_GROUPS):
        for r in range(rows // STRIP):
            rs = slice(STRIP * r, STRIP * (r + 1))
            s = jnp.where(cmp_valid[rs], sc_ref[g, rs, :], NEG_INF)
            p = jnp.where(cmp_valid[rs], jnp.exp2(s - jnp.max(s, axis=1, keepdims=True)), 0.0)
            l = jnp.sum(p, axis=1, keepdims=True)
            p = p * jnp.where(l > 0.0, 1.0 / l, 0.0)
            pn_ref[g, rs, :] = p
            pc_ref[g, rs, :] = p.astype(BF16)
        oc_ref[g] = _dot(pc_ref[g], kvc_ref[NSA_GROUPS + g, 0])

    def select(nb):
        width = NSA_GROUPS * NSA_TQ
        ps = jnp.concatenate(
            [sum(pn_ref[g, NSA_TQ * r:NSA_TQ * (r + 1), :] for r in range(NSA_REP))
             for g in range(NSA_GROUPS)], axis=0)
        hi = ps.astype(BF16)
        lo = (ps - hi.astype(F32)).astype(BF16)
        imp = _dot_nt(cov_ref[:nb, :], hi) + _dot_nt(cov_ref[:nb, :], lo)
        jidx = lax.broadcasted_iota(jnp.int32, (nb, width), 0)
        lane_q = lax.broadcasted_iota(jnp.int32, (1, width), 1) & (NSA_TQ - 1)
        cur = c * (NSA_TQ // SLC_LEN) + lane_q // SLC_LEN
        forced = (jidx == 0) | (jidx == cur) | (jidx == cur - 1)
        score = jnp.where(jidx <= cur, jnp.where(forced, FORCE_BONUS, imp), NEG_INF)
        score_ref[:nb, :] = score
        sub = 8
        cnt = [jnp.zeros((sub, width), F32) for _ in range(nb // sub)]
        tiles = [score[sub * v:sub * (v + 1)] for v in range(nb // sub)]
        sidx = lax.broadcasted_iota(jnp.int32, (sub, width), 0)
        for jp in range(nb):
            rowv = jnp.broadcast_to(score_ref[jp:jp + 1, :], (sub, width))
            for v in range(nb // sub):
                if sub * v > jp:
                    cnt[v] = jnp.where(rowv >= tiles[v], cnt[v] + 1.0, cnt[v])
                elif sub * v + sub - 1 <= jp:
                    cnt[v] = jnp.where(rowv > tiles[v], cnt[v] + 1.0, cnt[v])
                else:
                    ge = jnp.where(rowv >= tiles[v], cnt[v] + 1.0, cnt[v])
                    gt = jnp.where(rowv > tiles[v], cnt[v] + 1.0, cnt[v])
                    cnt[v] = jnp.where(sidx + sub * v > jp, ge, gt)
        rank = jnp.concatenate(cnt, axis=0)
        bias = jnp.where(rank < float(TOPN), 0.0, NEG_INF)
        bias = jnp.concatenate([bias, jnp.zeros((LANES - nb, width), F32)], axis=0)
        qbias_ref[...] = bias.T.astype(BF16)

    visible = (c + 1) * (NSA_TQ // SLC_LEN)

    @pl.when(visible <= TOPN)
    def _():
        qbias_ref[...] = jnp.zeros(qbias_ref.shape, BF16)

    bounds = [TOPN] + [nb for nb in (nblk // 2, 3 * nblk // 4) if TOPN < nb < nblk] + [nblk]
    for lo_nb, nb in zip(bounds[:-1], bounds[1:]):
        pl.when((visible > lo_nb) & (visible <= nb))(functools.partial(select, nb))

    _flash_reset(ms_ref, accs_ref)
    qas = [jnp.concatenate(
        [qs[g], jnp.concatenate([qbias_ref[NSA_TQ * g:NSA_TQ * (g + 1), :]] * NSA_REP, axis=0)],
        axis=1) for g in range(NSA_GROUPS)]
    last = c // (SLC_CHUNK // NSA_TQ)

    def slc_scores(kc, dst):
        k0 = pl.multiple_of(kc * SLC_CHUNK, SLC_CHUNK)
        for g in range(NSA_GROUPS):
            dst[g] = _dot_nt(qas[g], kaug_ref[g, pl.ds(k0, SLC_CHUNK), :])

    def slc_update(src, p_ref, kc, causal):
        k0 = pl.multiple_of(kc * SLC_CHUNK, SLC_CHUNK)
        mask = None
        if causal:
            mask = (k0 + lax.broadcasted_iota(jnp.int32, (1, SLC_CHUNK), 1)) <= t_row
        for g in range(NSA_GROUPS):
            _flash_update(src.at[g], kvn_ref[0, 2 + g, pl.ds(k0, SLC_CHUNK), :],
                          ms_ref.at[g], accs_ref.at[g], p_ref.at[g], mask)

    slc_scores(0, sa_ref)

    win_start = pl.multiple_of(jnp.maximum(c * NSA_TQ - WINDOW, 0), NSA_TQ)
    for g in range(NSA_GROUPS):
        sw_ref[g] = _dot_nt(qs[g], kvn_ref[0, 4 + g, pl.ds(win_start, WIN_SPAN), :])

    def window(mask):
        for g in range(NSA_GROUPS):
            _flash_update(sw_ref.at[g], kvn_ref[0, 6 + g, pl.ds(win_start, WIN_SPAN), :],
                          None, accw_ref.at[g], pw_ref.at[g], mask)

    diff = t_row - (win_start + lax.broadcasted_iota(jnp.int32, (1, WIN_SPAN), 1))
    full_span = c * NSA_TQ >= WINDOW

    @pl.when(full_span)
    def _():
        edge = NSA_TQ
        window([(edge, diff[:, :edge] < WINDOW), (WIN_SPAN - 2 * edge, None),
                (edge, diff[:, WIN_SPAN - edge:] >= 0)])

    @pl.when(jnp.logical_not(full_span))
    def _():
        window((diff >= 0) & (diff < WINDOW))

    def slc_pair(i, carry):
        slc_scores(2 * i + 1, sb_ref)
        slc_update(sa_ref, pa_ref, 2 * i, False)
        slc_scores(2 * i + 2, sa_ref)
        slc_update(sb_ref, pb_ref, 2 * i + 1, False)
        return carry
    lax.fori_loop(0, last // 2, slc_pair, 0)
    tail = 2 * (last // 2)

    @pl.when(last > tail)
    def _():
        slc_scores(tail + 1, sb_ref)
        slc_update(sa_ref, pa_ref, tail, False)
        slc_update(sb_ref, pb_ref, tail + 1, True)

    @pl.when(last == tail)
    def _():
        slc_update(sa_ref, pa_ref, tail, True)

    sig = 1.0 / (1.0 + jnp.exp(-misc_ref[...]))
    outs = []
    for g in range(NSA_GROUPS):
        for r in range(NSA_REP):
            lane0 = GATE_LANE0 + 3 * (NSA_REP * g + r)
            rs = slice(NSA_TQ * r, NSA_TQ * (r + 1))
            o = (sig[:, lane0:lane0 + 1] * oc_ref[g, rs, :]
                 + sig[:, lane0 + 1:lane0 + 2] * _flash_finish(accs_ref[g, rs, :])
                 + sig[:, lane0 + 2:lane0 + 3] * _flash_finish(accw_ref[g, rs, :]))
            outs.append(o[:, :NSA_DH])
    o_ref[0] = jnp.concatenate(outs, axis=1).astype(BF16)


def _nsa(qn, kvn, kvcmp, misc, cov_t, B, S):
    nblk = S // SLC_LEN
    ncmp = S // CMP_STRIDE
    nq = S // NSA_TQ
    rows = NSA_ROWS
    return pl.pallas_call(
        _nsa_body,
        grid=(B, nq),
        in_specs=[
            pl.BlockSpec((1, NSA_HEADS, NSA_TQ, LANES), lambda b, c: (b, 0, c, 0)),
            pl.BlockSpec((1, 8, S, LANES), lambda b, c: (b, 0, 0, 0)),
            pl.BlockSpec((4, 1, ncmp, LANES), lambda b, c: (0, b, 0, 0)),
            pl.BlockSpec((NSA_TQ, LANES), lambda b, c: (b * nq + c, 0)),
            pl.BlockSpec(cov_t.shape, lambda b, c: (0, 0)),
        ],
        out_specs=pl.BlockSpec((1, NSA_TQ, NSA_HEADS * NSA_DH), lambda b, c: (b, c, 0)),
        out_shape=jax.ShapeDtypeStruct((B, S, NSA_HEADS * NSA_DH), BF16),
        scratch_shapes=[
            pltpu.VMEM((NSA_GROUPS, S, 2 * LANES), BF16),
            pltpu.VMEM((nblk, NSA_GROUPS * NSA_TQ), F32),
            pltpu.VMEM((NSA_GROUPS * NSA_TQ, LANES), BF16),
            pltpu.VMEM((NSA_GROUPS, rows, SLC_CHUNK), F32),
            pltpu.VMEM((NSA_GROUPS, rows, SLC_CHUNK), F32),
            pltpu.VMEM((NSA_GROUPS, rows, SLC_CHUNK), BF16),
            pltpu.VMEM((NSA_GROUPS, rows, SLC_CHUNK), BF16),
            pltpu.VMEM((NSA_GROUPS, rows, LANES), F32),
            pltpu.VMEM((NSA_GROUPS, rows, LANES), F32),
            pltpu.VMEM((NSA_GROUPS, rows, WIN_SPAN), F32),
            pltpu.VMEM((NSA_GROUPS, rows, WIN_SPAN), BF16),
            pltpu.VMEM((NSA_GROUPS, rows, LANES), F32),
            pltpu.VMEM((NSA_GROUPS, rows, ncmp), F32),
            pltpu.VMEM((NSA_GROUPS, rows, ncmp), F32),
            pltpu.VMEM((NSA_GROUPS, rows, ncmp), BF16),
            pltpu.VMEM((NSA_GROUPS, rows, LANES), F32),
        ],
        compiler_params=_params("arbitrary", "arbitrary"),
        name="nsa",
    )(qn, kvn, kvcmp, misc, cov_t)


MLA_TQ = 512
MLA_CHUNK = 512
MLA_HPB = 4


def _mla_body(q_ref, k_ref, v_ref, o_ref, sa_ref, sb_ref, pa_ref, pb_ref, m_ref, acc_ref):
    qi = pl.program_id(2)
    t_row = qi * MLA_TQ + lax.broadcasted_iota(jnp.int32, (MLA_TQ, 1), 0)
    _flash_reset(m_ref, acc_ref)

    def scores(kc, dst):
        k0 = pl.multiple_of(kc * MLA_CHUNK, MLA_CHUNK)
        for j in range(MLA_HPB):
            dst[j] = _dot_nt(q_ref[0, j], k_ref[0, j, pl.ds(k0, MLA_CHUNK), :])

    def update(src, p_ref, kc, causal):
        k0 = pl.multiple_of(kc * MLA_CHUNK, MLA_CHUNK)
        mask = None
        if causal:
            mask = (k0 + lax.broadcasted_iota(jnp.int32, (1, MLA_CHUNK), 1)) <= t_row
        for j in range(MLA_HPB):
            _flash_update(src.at[j], v_ref[0, j, pl.ds(k0, MLA_CHUNK), :],
                          m_ref.at[j], acc_ref.at[j], p_ref.at[j], mask)

    scores(0, sa_ref)

    def pair(i, carry):
        scores(2 * i + 1, sb_ref)
        update(sa_ref, pa_ref, 2 * i, False)
        scores(2 * i + 2, sa_ref)
        update(sb_ref, pb_ref, 2 * i + 1, False)
        return carry
    lax.fori_loop(0, qi // 2, pair, 0)
    tail = 2 * (qi // 2)

    @pl.when(qi > tail)
    def _():
        scores(tail + 1, sb_ref)
        update(sa_ref, pa_ref, tail, False)
        update(sb_ref, pb_ref, tail + 1, True)

    @pl.when(qi == tail)
    def _():
        update(sa_ref, pa_ref, tail, True)
    o_ref[0] = jnp.concatenate([_flash_finish(acc_ref[j])[:, :MLA_V] for j in range(MLA_HPB)],
                               axis=1).astype(BF16)


def _mla(q, k, v, B, S):
    return pl.pallas_call(
        _mla_body,
        grid=(B, MLA_HEADS // MLA_HPB, S // MLA_TQ),
        in_specs=[
            pl.BlockSpec((1, MLA_HPB, MLA_TQ, LANES), lambda b, h, i: (b, h, i, 0)),
            pl.BlockSpec((1, MLA_HPB, S, LANES), lambda b, h, i: (b, h, 0, 0)),
            pl.BlockSpec((1, MLA_HPB, S, LANES), lambda b, h, i: (b, h, 0, 0)),
        ],
        out_specs=pl.BlockSpec((1, MLA_TQ, MLA_HPB * MLA_V), lambda b, h, i: (b, i, h)),
        out_shape=jax.ShapeDtypeStruct((B, S, MLA_HEADS * MLA_V), BF16),
        scratch_shapes=[
            pltpu.VMEM((MLA_HPB, MLA_TQ, MLA_CHUNK), F32),
            pltpu.VMEM((MLA_HPB, MLA_TQ, MLA_CHUNK), F32),
            pltpu.VMEM((MLA_HPB, MLA_TQ, MLA_CHUNK), BF16),
            pltpu.VMEM((MLA_HPB, MLA_TQ, MLA_CHUNK), BF16),
            pltpu.VMEM((MLA_HPB, MLA_TQ, LANES), F32),
            pltpu.VMEM((MLA_HPB, MLA_TQ, LANES), F32),
        ],
        compiler_params=_params("arbitrary", "arbitrary", "arbitrary"),
        name="mla",
    )(q, k, v)


def _mem_kv_body(m_ref, wk_ref, wv_ref, k_ref, v_ref):
    mb = m_ref[...].astype(BF16)
    k_ref[...] = _dot(mb, wk_ref[...]).astype(BF16)
    v_ref[...] = _dot(mb, wv_ref[...]).astype(BF16)


def _mem_kv(memf, wk, wv, tm):
    R, D = memf.shape
    tok = lambda i: (i, 0)
    const = lambda i: (0, 0)
    return pl.pallas_call(
        _mem_kv_body,
        grid=(R // tm,),
        in_specs=[pl.BlockSpec((tm, D), tok), pl.BlockSpec(wk.shape, const),
                  pl.BlockSpec(wv.shape, const)],
        out_specs=[pl.BlockSpec((tm, D), tok), pl.BlockSpec((tm, D), tok)],
        out_shape=[jax.ShapeDtypeStruct((R, D), BF16), jax.ShapeDtypeStruct((R, D), BF16)],
        compiler_params=_params("arbitrary"),
        name="mem_kv",
    )(memf, wk, wv)


def _mem_attn_body(x0_ref, on_ref, om_ref, wmix_ref, g1_ref, b1_ref,
                   k_ref, v_ref, wq_ref, wo_ref, g_ref, b_ref, o_ref, *, alpha):
    half = on_ref.shape[1]
    mix = _dot(on_ref[...], wmix_ref[:half, :]) + _dot(om_ref[...], wmix_ref[half:, :])
    x = _layer_norm(alpha * x0_ref[...] + mix, g1_ref[...], b1_ref[...])
    D = x.shape[1]
    dh = D // MEM_HEADS
    q = (_dot(x.astype(BF16), wq_ref[...]) * (dh ** -0.5 * LOG2E)).astype(BF16)
    outs = []
    for h in range(MEM_HEADS):
        cs = slice(dh * h, dh * (h + 1))
        s = _dot_nt(q[:, cs], k_ref[0, :, cs])
        p = jnp.exp2(s - jnp.max(s, axis=1, keepdims=True))
        l = jnp.sum(p, axis=1, keepdims=True)
        outs.append((_dot(p.astype(BF16), v_ref[0, :, cs]) * (1.0 / l)).astype(BF16))
    o = jnp.concatenate(outs, axis=1)
    y = _dot(o, wo_ref[...])
    o_ref[...] = _layer_norm(alpha * x + y, g_ref[...], b_ref[...])


def _mem_attn(xf, o_nsa, o_mla, w_o, g1, b1, k_mem, v_mem, wq, wo, g, b, alpha, S, tm):
    T, D = xf.shape
    nst = S // tm
    M = k_mem.shape[1]
    tok = lambda i: (i, 0)
    const = lambda i: (0, 0)
    memb = lambda i: (i // nst, 0, 0)
    return pl.pallas_call(
        functools.partial(_mem_attn_body, alpha=alpha),
        grid=(T // tm,),
        in_specs=[
            pl.BlockSpec((tm, D), tok),
            pl.BlockSpec((tm, o_nsa.shape[1]), tok),
            pl.BlockSpec((tm, o_mla.shape[1]), tok),
            pl.BlockSpec(w_o.shape, const),
            pl.BlockSpec((1, D), const),
            pl.BlockSpec((1, D), const),
            pl.BlockSpec((1, M, D), memb),
            pl.BlockSpec((1, M, D), memb),
            pl.BlockSpec(wq.shape, const),
            pl.BlockSpec(wo.shape, const),
            pl.BlockSpec((1, D), const),
            pl.BlockSpec((1, D), const),
        ],
        out_specs=pl.BlockSpec((tm, D), tok),
        out_shape=jax.ShapeDtypeStruct((T, D), F32),
        compiler_params=_params("arbitrary"),
        name="mem_attn",
    )(xf, o_nsa, o_mla, w_o, g1, b1, k_mem, v_mem, wq, wo, g, b)


HALO = 8


FFN_SLAB = 256
FFN_TM = 512


def _ffn_body(x_ref, xh_ref, wg_ref, wu_ref, cw_ref, cb_ref, wd_ref, g_ref, b_ref, o_ref,
              act_ref, *, alpha, seq_tiles):
    i = pl.program_id(0)
    x = x_ref[...]
    xb = x.astype(BF16)
    xhb = xh_ref[...].astype(BF16)
    tm = x.shape[0]
    row = lax.broadcasted_iota(jnp.int32, (tm, 1), 0)
    seq_start = i % seq_tiles == 0
    for c0 in range(0, wg_ref.shape[1], FFN_SLAB):
        cs = slice(c0, c0 + FFN_SLAB)
        gate = _dot(xb, wg_ref[:, cs])
        up = _dot(xb, wu_ref[:, cs])
        halo = jnp.where(seq_start, 0.0, _dot(xhb, wg_ref[:, cs]))
        g1 = jnp.where(row == 0, halo[HALO - 1:HALO], pltpu.roll(gate, 1, 0))
        g2 = jnp.where(row == 0, halo[HALO - 2:HALO - 1],
                       jnp.where(row == 1, halo[HALO - 1:HALO], pltpu.roll(gate, 2, 0)))
        conv = cw_ref[0:1, cs] * g2 + cw_ref[1:2, cs] * g1 + cw_ref[2:3, cs] * gate + cb_ref[:, cs]
        act_ref[:, cs] = (conv * (1.0 / (1.0 + jnp.exp(-conv))) * up).astype(BF16)
    y = _dot(act_ref[...], wd_ref[...])
    o_ref[...] = _layer_norm(alpha * x + y, g_ref[...], b_ref[...])


def _ffn(xf, wg, wu, cw, cb, wd, g, b, alpha, S, tm):
    T, D = xf.shape
    dff = wg.shape[1]
    assert dff % FFN_SLAB == 0
    tok = lambda i: (i, 0)
    const = lambda i: (0, 0)
    return pl.pallas_call(
        functools.partial(_ffn_body, alpha=alpha, seq_tiles=S // tm),
        grid=(T // tm,),
        in_specs=[
            pl.BlockSpec((tm, D), tok),
            pl.BlockSpec((HALO, D), lambda i: (jnp.maximum(i * (tm // HALO) - 1, 0), 0)),
            pl.BlockSpec((D, dff), const),
            pl.BlockSpec((D, dff), const),
            pl.BlockSpec((CONV_WIDTH, dff), const),
            pl.BlockSpec((1, dff), const),
            pl.BlockSpec((dff, D), const),
            pl.BlockSpec((1, D), const),
            pl.BlockSpec((1, D), const),
        ],
        out_specs=pl.BlockSpec((tm, D), tok),
        out_shape=jax.ShapeDtypeStruct((T, D), F32),
        scratch_shapes=[pltpu.VMEM((tm, dff), BF16)],
        compiler_params=_params("arbitrary"),
        name="ffn",
    )(xf, xf, wg, wu, cw, cb, wd, g, b)


def _inv_freq_row(dim, lane_lo, lane_hi, period):
    inv = ROPE_THETA ** (-np.arange(0, dim, 2, dtype=np.float64) / dim)
    row = np.zeros((1, LANES), np.float32)
    for lane in range(lane_lo, lane_hi):
        row[0, lane] = inv[(lane % period) % (dim // 2)]
    return jnp.asarray(row)


def _cover_t(S):
    nc = S // CMP_STRIDE
    ns = S // SLC_LEN
    cs = np.arange(nc)[:, None] * CMP_STRIDE
    ss = np.arange(ns)[None, :] * SLC_LEN
    cover = np.clip(np.minimum(cs + CMP_LEN, ss + SLC_LEN) - np.maximum(cs, ss), 0, None) / CMP_LEN
    cover[nc - 1:] = 0.0
    return jnp.asarray(cover.T, dtype=BF16)


def _permute_w_in(w):
    D = w.shape[0]
    c1 = NSA_HEADS * NSA_DH
    c2 = c1 + 3 * 2 * NSA_GROUPS * NSA_DH
    c3 = c2 + 3 * NSA_HEADS
    c4 = c3 + MLA_Q_RANK
    c5 = c4 + MLA_KV_RANK
    c6 = c5 + MLA_ROPE
    half = MLA_ROPE // 2
    z = lambda n: jnp.zeros((D, n), w.dtype)
    misc = jnp.concatenate(
        [w[:, c5 + half:c6], z(GATE_LANE0 - half), w[:, c2:c3], z(MLA_PE1 - GATE_LANE0 - (c3 - c2)),
         w[:, c5:c5 + half], z(LANES - MLA_PE1 - half)], axis=1)
    return jnp.concatenate([w[:, :c2], w[:, c3:c5], misc], axis=1).astype(BF16)


def _mla_head_lanes(nope, pe):
    r, H, _ = nope.shape
    half = MLA_ROPE // 2
    split = MLA_PE1 - half
    pad = jnp.zeros((r, H, LANES - MLA_NOPE - MLA_ROPE), nope.dtype)
    return jnp.concatenate([pe[..., half:], nope[..., :split], pe[..., :half], nope[..., split:], pad],
                           axis=2).reshape(r, H * LANES)


def _permute_w_uq(w):
    r = w.shape[0]
    w3 = w.reshape(r, MLA_HEADS, MLA_NOPE + MLA_ROPE)
    return _mla_head_lanes(w3[..., :MLA_NOPE], w3[..., MLA_NOPE:]).astype(BF16)


def _permute_w_ukv(w):
    r = w.shape[0]
    w3 = w.reshape(r, MLA_HEADS, MLA_NOPE + MLA_V)
    k = _mla_head_lanes(w3[..., :MLA_NOPE], jnp.zeros((r, MLA_HEADS, MLA_ROPE), w.dtype))
    v = _pad_lanes(w3[..., MLA_NOPE:]).reshape(r, MLA_HEADS * LANES)
    return jnp.concatenate([k, v], axis=1).astype(BF16)


def _pad_lanes(a):
    return jnp.concatenate([a, jnp.zeros(a.shape[:-1] + (LANES - a.shape[-1],), a.dtype)], axis=-1)


def kernel(x, mem, positions, w_in, nsa_k_pos, nsa_ck_w1, nsa_ck_b1, nsa_ck_w2, nsa_ck_b2,
           nsa_v_pos, nsa_cv_w1, nsa_cv_b1, nsa_cv_w2, nsa_cv_b2,
           mla_q_norm, mla_w_uq, mla_kv_norm, mla_w_ukv, w_o, ln1_g, ln1_b,
           mem_wq, mem_wk, mem_wv, mem_wo, ln2_g, ln2_b,
           ffn_w_up, ffn_conv_w, ffn_conv_b, ffn_w_down, ln3_g, ln3_b):
    B, S, D = x.shape
    T = B * S
    depth = w_in.shape[0]
    alpha = (2.0 * depth) ** 0.25
    d_ff = ffn_w_down.shape[1]
    tm = min(512, S)
    assert S % MLA_TQ == 0 and S >= WIN_SPAN and S % tm == 0
    assert (B * mem.shape[1]) % 256 == 0

    pos = positions.reshape(T, 1)
    pos_cmp = positions[:, CMP_LEN - 1::CMP_STRIDE]
    pos_cmp = jnp.concatenate([pos_cmp, pos_cmp[:, -1:]], axis=1)[:, :, None]
    inv_cmp = _inv_freq_row(NSA_DH, 0, NSA_DH, NSA_DH)
    inv_tok = (_inv_freq_row(NSA_DH, 0, NSA_DH // 2, NSA_DH)
               + _inv_freq_row(MLA_ROPE, NSA_DH // 2, NSA_DH // 2 + MLA_ROPE // 2, MLA_ROPE // 2))
    cov_t = _cover_t(S)
    memf = mem.reshape(B * mem.shape[1], D)

    xf = x.reshape(T, D)
    for l in range(depth):
        qn, kvn, kvc, misc, q_m, k_m, v_m = _inproj(
            pos, inv_tok, xf, _permute_w_in(w_in[l]), mla_q_norm[l][None, :],
            mla_kv_norm[l][None, :], _permute_w_uq(mla_w_uq[l]), _permute_w_ukv(mla_w_ukv[l]),
            B, S, tm)
        kvcmp = _compress(
            pos_cmp, inv_cmp, kvc,
            *_compress_weights(nsa_k_pos[l], nsa_ck_w1[l], nsa_ck_b1[l], nsa_ck_w2[l], nsa_ck_b2[l],
                               nsa_v_pos[l], nsa_cv_w1[l], nsa_cv_b1[l], nsa_cv_w2[l], nsa_cv_b2[l]),
            B, S)
        o_nsa = _nsa(qn, kvn, kvcmp, misc, cov_t, B, S)
        o_mla = _mla(q_m, k_m, v_m, B, S)
        k_mem, v_mem = _mem_kv(memf, mem_wk[l].astype(BF16), mem_wv[l].astype(BF16), 256)
        xf = _mem_attn(xf, o_nsa.reshape(T, -1), o_mla.reshape(T, -1), w_o[l].astype(BF16),
                       ln1_g[l][None, :], ln1_b[l][None, :],
                       k_mem.reshape(B, -1, D), v_mem.reshape(B, -1, D),
                       mem_wq[l].astype(BF16), mem_wo[l].astype(BF16),
                       ln2_g[l][None, :], ln2_b[l][None, :], alpha, S, tm)
        xf = _ffn(xf, ffn_w_up[l][:, :d_ff].astype(BF16), ffn_w_up[l][:, d_ff:].astype(BF16),
                  ffn_conv_w[l], ffn_conv_b[l][None, :], ffn_w_down[l].astype(BF16),
                  ln3_g[l][None, :], ln3_b[l][None, :], alpha, S, FFN_TM)
    return xf.reshape(B, S, D)
```

```python
import functools
import math

import numpy as np
import jax
import jax.numpy as jnp
from jax import lax
from jax.experimental import pallas as pl
from jax.experimental.pallas import tpu as pltpu

F32 = jnp.float32
BF16 = jnp.bfloat16

NSA_HEADS = 8
NSA_GROUPS = 2
NSA_REP = NSA_HEADS // NSA_GROUPS
NSA_DH = 64
CMP_STRIDE = 16
CMP_LEN = 32
SLC_LEN = 64
TOPN = 16
WINDOW = 512
CMP_HIDDEN = 128
FORCE_BONUS = 1e4
MLA_HEADS = 8
MLA_Q_RANK = 384
MLA_KV_RANK = 256
MLA_NOPE = 64
MLA_ROPE = 32
MLA_V = 64
MEM_HEADS = 4
CONV_WIDTH = 3
ROPE_THETA = 10000.0
LN_EPS = 1e-5
RMS_EPS = 1e-6
NEG_INF = -1e30
LOG2E = math.log2(math.e)

LANES = 128
VMEM_LIMIT = 60 * 1024 * 1024

C_Q = 0
C_KVC = 512
C_KVN = 768
C_LAT = 1280
C_MISC = 1920
IN_COLS_PAD = 2048
GATE_LANE0 = MLA_ROPE
MLA_PE2 = 0
MLA_PE1 = 64
ONES_LANE = 64


def _dot(a, b):
    return jnp.dot(a, b, preferred_element_type=F32)


def _dot_nt(a, b):
    return lax.dot_general(a, b, (((1,), (1,)), ((), ())), preferred_element_type=F32)


def _layer_norm(y, g, b):
    mu = jnp.mean(y, axis=-1, keepdims=True)
    d = y - mu
    var = jnp.mean(d * d, axis=-1, keepdims=True)
    return d * lax.rsqrt(var + LN_EPS) * g + b


def _params(*sem):
    return pltpu.CompilerParams(dimension_semantics=sem, vmem_limit_bytes=VMEM_LIMIT)


def _rope_tables(pos_col, inv_row, half):
    ang = pos_col * inv_row
    cos = jnp.cos(ang)
    sin = jnp.sin(ang)
    lane = lax.broadcasted_iota(jnp.int32, (1, LANES), 1)
    upper = (lane & (2 * half - 1)) >= half
    rot = inv_row != 0.0
    sin_hi = jnp.where(upper & rot, sin, 0.0)
    sin_lo = jnp.where(upper | (~rot), 0.0, -sin)
    return cos, sin_hi, sin_lo


def _apply_rope(v, tabs, half):
    cos, sin_hi, sin_lo = tabs
    return v * cos + pltpu.roll(v, half, 1) * sin_hi + pltpu.roll(v, LANES - half, 1) * sin_lo


def _rms_norm(v, g):
    return v * lax.rsqrt(jnp.mean(v * v, axis=-1, keepdims=True) + RMS_EPS) * g


def _inproj_body(pos_ref, inv_ref, x_ref, w_ref, gq_ref, gkv_ref, wq_ref, wkv_ref,
                 qn_ref, kvn_ref, kvc_ref, misc_ref, qm_ref, km_ref, vm_ref):
    xb = x_ref[...].astype(BF16)
    lane = lax.broadcasted_iota(jnp.int32, (1, LANES), 1)
    low = lane < NSA_DH
    qscale = NSA_DH ** -0.5 * LOG2E

    ang = pos_ref[...].astype(F32) * inv_ref[...]
    cos, sin = jnp.cos(ang), jnp.sin(ang)
    nf, mf = NSA_DH // 2, MLA_ROPE // 2

    def tile_nsa(t):
        t = jnp.where(lane < nf, t, 0.0)
        t = t + pltpu.roll(t, nf, 1)
        return t + pltpu.roll(t, 2 * nf, 1)

    def place_mla(t):
        t = jnp.where((lane >= nf) & (lane < nf + mf), t, 0.0)
        return pltpu.roll(t, MLA_PE1 - nf, 1) + pltpu.roll(t, LANES + MLA_PE2 - nf, 1)

    upper = (lane & (NSA_DH - 1)) >= nf
    sin_n = tile_nsa(sin)
    tabs = (tile_nsa(cos), jnp.where(upper, sin_n, 0.0), jnp.where(upper, 0.0, -sin_n))
    pe1 = (lane >= MLA_PE1) & (lane < MLA_PE1 + mf)
    pe2 = (lane >= MLA_PE2) & (lane < MLA_PE2 + mf)
    pe_lanes = pe1 | pe2
    sin_m = place_mla(sin)
    cos_m = jnp.where(pe_lanes, place_mla(cos), 1.0)
    sin_m = jnp.where(pe1, -sin_m, jnp.where(pe2, sin_m, 0.0))

    def rope_mla(v):
        return v * cos_m + pltpu.roll(v, LANES // 2, 1) * sin_m

    def proj(c0, n):
        return _dot(xb, w_ref[:, c0:c0 + n])

    def split_store(v, ref, idx_lo, idx_hi, pad=0.0):
        ref[0, idx_lo] = jnp.where(low, v, pad).astype(BF16)
        ref[0, idx_hi] = jnp.where(low, pltpu.roll(v, NSA_DH, 1), pad).astype(BF16)

    ones_pad = jnp.where(lane == ONES_LANE, 1.0, 0.0)

    for slab in range(2):
        h = proj(C_Q + 256 * slab, 256)
        for j in range(2):
            r = _apply_rope(h[:, LANES * j:LANES * (j + 1)], tabs, NSA_DH // 2) * qscale
            split_store(r, qn_ref, 4 * slab + 2 * j, 4 * slab + 2 * j + 1)

    h = proj(C_KVC, 256)
    kvc_ref[0] = h[:, :LANES]
    kvc_ref[1] = h[:, LANES:]

    for slab in range(2):
        h = proj(C_KVN + 256 * slab, 256)
        k = _apply_rope(h[:, :LANES], tabs, NSA_DH // 2)
        split_store(k, kvn_ref, 4 * slab, 4 * slab + 1)
        split_store(h[:, LANES:], kvn_ref, 4 * slab + 2, 4 * slab + 3, ones_pad)

    lat = [proj(C_LAT + 256 * i, 256) for i in range((IN_COLS_PAD - C_LAT) // 256)]
    misc = lat[2][:, LANES:]
    misc_ref[...] = misc
    mq = jnp.concatenate([lat[0], lat[1][:, :LANES]], axis=1)
    mkv = jnp.concatenate([lat[1][:, LANES:], lat[2][:, :LANES]], axis=1)
    mscale = (MLA_NOPE + MLA_ROPE) ** -0.5 * LOG2E
    qn = _rms_norm(mq, gq_ref[...]).astype(BF16)
    kvn = _rms_norm(mkv, gkv_ref[...]).astype(BF16)
    kpe = jnp.where(pe_lanes, rope_mla(misc), 0.0)
    kcols = MLA_HEADS * LANES
    for slab in range(MLA_HEADS // 2):
        cs = slice(256 * slab, 256 * (slab + 1))
        hq = _dot(qn, wq_ref[:, cs])
        hk = _dot(kvn, wkv_ref[:, cs])
        hv = _dot(kvn, wkv_ref[:, kcols + 256 * slab:kcols + 256 * (slab + 1)])
        for j in range(2):
            ls = slice(LANES * j, LANES * (j + 1))
            qm_ref[0, 2 * slab + j] = (rope_mla(hq[:, ls]) * mscale).astype(BF16)
            km_ref[0, 2 * slab + j] = (hk[:, ls] + kpe).astype(BF16)
            vm_ref[0, 2 * slab + j] = (hv[:, ls] + ones_pad).astype(BF16)


def _inproj(pos, inv_row, xf, w_in_p, gq, gkv, wq_p, wkv_p, B, S, tm):
    T = B * S
    nst = S // tm
    tok = lambda i: (i, 0)
    const = lambda i: (0, 0)
    head_blk = lambda i: (i // nst, 0, i % nst, 0)
    heads = jax.ShapeDtypeStruct((B, 8, S, LANES), BF16)
    return pl.pallas_call(
        _inproj_body,
        grid=(T // tm,),
        in_specs=[
            pl.BlockSpec((tm, 1), tok),
            pl.BlockSpec((1, LANES), const),
            pl.BlockSpec((tm, xf.shape[1]), tok),
            pl.BlockSpec(w_in_p.shape, const),
            pl.BlockSpec(gq.shape, const),
            pl.BlockSpec(gkv.shape, const),
            pl.BlockSpec(wq_p.shape, const),
            pl.BlockSpec(wkv_p.shape, const),
        ],
        out_specs=[
            pl.BlockSpec((1, 8, tm, LANES), head_blk),
            pl.BlockSpec((1, 8, tm, LANES), head_blk),
            pl.BlockSpec((2, tm, LANES), lambda i: (0, i, 0)),
            pl.BlockSpec((tm, LANES), tok),
            pl.BlockSpec((1, 8, tm, LANES), head_blk),
            pl.BlockSpec((1, 8, tm, LANES), head_blk),
            pl.BlockSpec((1, 8, tm, LANES), head_blk),
        ],
        out_shape=[
            heads,
            heads,
            jax.ShapeDtypeStruct((2, T, LANES), F32),
            jax.ShapeDtypeStruct((T, LANES), F32),
            heads,
            heads,
            heads,
        ],
        compiler_params=_params("arbitrary"),
        name="inproj",
    )(pos, inv_row, xf, w_in_p, gq, gkv, wq_p, wkv_p)


def _compress_body(pos_ref, inv_ref, x_ref, pe_ref, w1_ref, b1_ref, w2_ref, b2_ref, o_ref):
    is_k = pl.program_id(0) == 0
    nch = o_ref.shape[2]
    a1 = jnp.zeros((nch, NSA_GROUPS * CMP_HIDDEN), F32)
    a2 = jnp.zeros((nch, NSA_GROUPS * CMP_HIDDEN), F32)
    for l in range(CMP_STRIDE):
        xl = x_ref.at[0, 0][pl.ds(l, nch, stride=CMP_STRIDE), :]
        a1 = a1 + _dot((xl + pe_ref[0, l:l + 1, :]).astype(BF16), w1_ref[0, l])
        a2 = a2 + _dot((xl + pe_ref[0, CMP_STRIDE + l:CMP_STRIDE + l + 1, :]).astype(BF16),
                       w1_ref[0, CMP_STRIDE + l])
    pre = a1 + pltpu.roll(a2, nch - 1, 0) + b1_ref[0]
    hid = jax.nn.gelu(pre, approximate=True)
    out = _dot(hid.astype(BF16), w2_ref[0]) + b2_ref[0]
    tabs = _rope_tables(pos_ref[0].astype(F32), inv_ref[...], NSA_DH // 2)
    row = lax.broadcasted_iota(jnp.int32, (nch, 1), 0)
    for g in range(NSA_GROUPS):
        og = out[:, LANES * g:LANES * (g + 1)]
        og = jnp.where(is_k, _apply_rope(og, tabs, NSA_DH // 2), og)
        o_ref[g, 0] = jnp.where(row < nch - 1, og, 0.0).astype(BF16)


def _compress(pos_cmp, inv_nsa, kvc, pe, w1, b1, w2, b2, B, S):
    nch = S // CMP_STRIDE
    x = kvc.reshape(2, B, S, LANES)
    kv = lambda j, b: (j, 0, 0)
    kv4 = lambda j, b: (j, 0, 0, 0)
    return pl.pallas_call(
        _compress_body,
        grid=(2, B),
        in_specs=[
            pl.BlockSpec((1, nch, 1), lambda j, b: (b, 0, 0)),
            pl.BlockSpec((1, LANES), lambda j, b: (0, 0)),
            pl.BlockSpec((1, 1, S, LANES), lambda j, b: (j, b, 0, 0)),
            pl.BlockSpec((1,) + pe.shape[1:], kv),
            pl.BlockSpec((1,) + w1.shape[1:], kv4),
            pl.BlockSpec((1,) + b1.shape[1:], kv),
            pl.BlockSpec((1,) + w2.shape[1:], kv),
            pl.BlockSpec((1,) + b2.shape[1:], kv),
        ],
        out_specs=pl.BlockSpec((NSA_GROUPS, 1, nch, LANES), lambda j, b: (j, b, 0, 0)),
        out_shape=jax.ShapeDtypeStruct((2 * NSA_GROUPS, B, nch, LANES), BF16),
        compiler_params=_params("arbitrary", "arbitrary"),
        name="compress",
    )(pos_cmp, inv_nsa, x, pe, w1, b1, w2, b2)


def _compress_weights(k_pos, k_w1, k_b1, k_w2, k_b2, v_pos, v_w1, v_b1, v_w2, v_b2):
    def one(pos, w1, b1, w2, b2):
        w1l = w1.reshape(CMP_LEN, NSA_DH, CMP_HIDDEN)
        z1 = jnp.zeros_like(w1l)
        w1bd = jnp.concatenate([jnp.concatenate([w1l, z1], axis=2),
                                jnp.concatenate([z1, w1l], axis=2)], axis=1)
        w2p = _pad_lanes(w2)
        z2 = jnp.zeros_like(w2p)
        w2bd = jnp.concatenate([jnp.concatenate([w2p, z2], axis=1),
                                jnp.concatenate([z2, w2p], axis=1)], axis=0)
        return (jnp.tile(pos, (1, NSA_GROUPS)), w1bd.astype(BF16), jnp.tile(b1, NSA_GROUPS)[None, :],
                w2bd.astype(BF16), jnp.tile(_pad_lanes(b2), NSA_GROUPS)[None, :])
    k = one(k_pos, k_w1, k_b1, k_w2, k_b2)
    v = one(v_pos, v_w1, v_b1, v_w2, v_b2)
    return tuple(jnp.stack([a, b]) for a, b in zip(k, v))


STRIP = 64


def _lane_tile(col, n):
    reps = [col] * (n // LANES)
    if n % LANES:
        reps.append(col[:, :n % LANES])
    return reps[0] if len(reps) == 1 else jnp.concatenate(reps, axis=1)


def _flash_reset(m_ref, acc_ref):
    m_ref[...] = jnp.full(m_ref.shape, NEG_INF, F32)
    acc_ref[...] = jnp.zeros(acc_ref.shape, F32)


def _flash_update(s_ref, v, m_ref, acc_ref, p_ref, mask=None):
    rows, n = s_ref.shape
    for r in range(rows // STRIP):
        rs = slice(STRIP * r, STRIP * (r + 1))
        s = s_ref[rs, :]
        if mask is not None:
            s = jnp.where(mask[rs], s, NEG_INF)
        if m_ref is None:
            p_ref[rs, :] = jnp.exp2(s - jnp.max(s, axis=1, keepdims=True)).astype(BF16)
            continue
        m_old = m_ref[rs, :]
        m_new = jnp.maximum(m_old, jnp.max(s, axis=1, keepdims=True))
        p_ref[rs, :] = jnp.exp2(s - _lane_tile(m_new, n)).astype(BF16)
        acc_ref[rs, :] = jnp.exp2(m_old - m_new) * acc_ref[rs, :]
        m_ref[rs, :] = m_new
    if m_ref is None:
        acc_ref[...] = _dot(p_ref[...], v)
    else:
        acc_ref[...] += _dot(p_ref[...], v)


def _flash_finish(acc):
    return acc * (1.0 / acc[:, ONES_LANE:ONES_LANE + 1])


NSA_TQ = 256
NSA_ROWS = NSA_REP * NSA_TQ
SLC_CHUNK = 512
WIN_SPAN = WINDOW + NSA_TQ
BIAS_LANE0 = LANES


def _nsa_body(q_ref, kvn_ref, kvc_ref, misc_ref, cov_ref, o_ref,
              kaug_ref, score_ref, qbias_ref, sa_ref, sb_ref, pa_ref, pb_ref, ms_ref, accs_ref,
              sw_ref, pw_ref, accw_ref, sc_ref, pn_ref, pc_ref, oc_ref):
    c = pl.program_id(1)
    rows = NSA_ROWS
    t_row = c * NSA_TQ + (lax.broadcasted_iota(jnp.int32, (rows, 1), 0) & (NSA_TQ - 1))
    ncmp = kvc_ref.shape[2]
    nblk = kaug_ref.shape[1] // SLC_LEN

    @pl.when(c == 0)
    def _():
        nkeys = kaug_ref.shape[1]
        key_blk = lax.broadcasted_iota(jnp.int32, (nkeys, LANES), 0) // SLC_LEN
        onehot = jnp.where(key_blk == lax.broadcasted_iota(jnp.int32, (nkeys, LANES), 1), 1.0, 0.0)
        for g in range(NSA_GROUPS):
            kaug_ref[g, :, :BIAS_LANE0] = kvn_ref[0, g]
            kaug_ref[g, :, BIAS_LANE0:] = onehot.astype(BF16)

    qs = [q_ref[0, NSA_REP * g:NSA_REP * (g + 1)].reshape(rows, LANES) for g in range(NSA_GROUPS)]

    cmp_valid = (CMP_STRIDE * lax.broadcasted_iota(jnp.int32, (1, ncmp), 1) + CMP_LEN - 1) <= t_row
    for g in range(NSA_GROUPS):
        sc_ref[g] = _dot_nt(qs[g], kvc_ref[g, 0])
    win_start = pl.multiple_of(jnp.maximum(c * NSA_TQ - WINDOW, 0), NSA_TQ)
    for g in range(NSA_GROUPS):
        sw_ref[g] = _dot_nt(qs[g], kvn_ref[0, 4 + g, pl.ds(win_start, WIN_SPAN), :])
    for g in range(NSA_GROUPS):
        for r in range(rows // STRIP):
            rs = slice(STRIP * r, STRIP * (r + 1))
            s = jnp.where(cmp_valid[rs], sc_ref[g, rs, :], NEG_INF)
            p = jnp.where(cmp_valid[rs], jnp.exp2(s - jnp.max(s, axis=1, keepdims=True)), 0.0)
            l = jnp.sum(p, axis=1, keepdims=True)
            p = p * jnp.where(l > 0.0, 1.0 / l, 0.0)
            pn_ref[g, rs, :] = p
            pc_ref[g, rs, :] = p.astype(BF16)
        oc_ref[g] = _dot(pc_ref[g], kvc_ref[NSA_GROUPS + g, 0])

    def select(nb):
        width = NSA_GROUPS * NSA_TQ
        ps = jnp.concatenate(
            [sum(pn_ref[g, NSA_TQ * r:NSA_TQ * (r + 1), :] for r in range(NSA_REP))
             for g in range(NSA_GROUPS)], axis=0)
        hi = ps.astype(BF16)
        lo = (ps - hi.astype(F32)).astype(BF16)
        imp = _dot_nt(cov_ref[:nb, :], hi) + _dot_nt(cov_ref[:nb, :], lo)
        jidx = lax.broadcasted_iota(jnp.int32, (nb, width), 0)
        lane_q = lax.broadcasted_iota(jnp.int32, (1, width), 1) & (NSA_TQ - 1)
        cur = c * (NSA_TQ // SLC_LEN) + lane_q // SLC_LEN
        forced = (jidx == 0) | (jidx == cur) | (jidx == cur - 1)
        score = jnp.where(jidx <= cur, jnp.where(forced, FORCE_BONUS, imp), NEG_INF)
        score_ref[:nb, :] = score
        sub = 8
        cnt = [jnp.zeros((sub, width), F32) for _ in range(nb // sub)]
        tiles = [score[sub * v:sub * (v + 1)] for v in range(nb // sub)]
        sidx = lax.broadcasted_iota(jnp.int32, (sub, width), 0)
        for jp in range(nb):
            rowv = jnp.broadcast_to(score_ref[jp:jp + 1, :], (sub, width))
            for v in range(nb // sub):
                if sub * v > jp:
                    cnt[v] = jnp.where(rowv >= tiles[v], cnt[v] + 1.0, cnt[v])
                elif sub * v + sub - 1 <= jp:
                    cnt[v] = jnp.where(rowv > tiles[v], cnt[v] + 1.0, cnt[v])
                else:
                    ge = jnp.where(rowv >= tiles[v], cnt[v] + 1.0, cnt[v])
                    gt = jnp.where(rowv > tiles[v], cnt[v] + 1.0, cnt[v])
                    cnt[v] = jnp.where(sidx + sub * v > jp, ge, gt)
        rank = jnp.concatenate(cnt, axis=0)
        bias = jnp.where(rank < float(TOPN), 0.0, NEG_INF)
        bias = jnp.concatenate([bias, jnp.zeros((LANES - nb, width), F32)], axis=0)
        qbias_ref[...] = bias.T.astype(BF16)

    visible = (c + 1) * (NSA_TQ // SLC_LEN)

    @pl.when(visible <= TOPN)
    def _():
        qbias_ref[...] = jnp.zeros(qbias_ref.shape, BF16)

    bounds = [TOPN] + [nb for nb in (nblk // 2, 3 * nblk // 4) if TOPN < nb < nblk] + [nblk]
    for lo_nb, nb in zip(bounds[:-1], bounds[1:]):
        pl.when((visible > lo_nb) & (visible <= nb))(functools.partial(select, nb))

    _flash_reset(ms_ref, accs_ref)
    qas = [jnp.concatenate(
        [qs[g], jnp.concatenate([qbias_ref[NSA_TQ * g:NSA_TQ * (g + 1), :]] * NSA_REP, axis=0)],
        axis=1) for g in range(NSA_GROUPS)]
    last = c // (SLC_CHUNK // NSA_TQ)

    def slc_scores(kc, dst):
        k0 = pl.multiple_of(kc * SLC_CHUNK, SLC_CHUNK)
        for g in range(NSA_GROUPS):
            dst[g] = _dot_nt(qas[g], kaug_ref[g, pl.ds(k0, SLC_CHUNK), :])

    def slc_update(src, p_ref, kc, causal):
        k0 = pl.multiple_of(kc * SLC_CHUNK, SLC_CHUNK)
        mask = None
        if causal:
            mask = (k0 + lax.broadcasted_iota(jnp.int32, (1, SLC_CHUNK), 1)) <= t_row
        for g in range(NSA_GROUPS):
            _flash_update(src.at[g], kvn_ref[0, 2 + g, pl.ds(k0, SLC_CHUNK), :],
                          ms_ref.at[g], accs_ref.at[g], p_ref.at[g], mask)

    slc_scores(0, sa_ref)

    diff = t_row - (win_start + lax.broadcasted_iota(jnp.int32, (1, WIN_SPAN), 1))
    win_valid = (diff >= 0) & (diff < WINDOW)
    for g in range(NSA_GROUPS):
        _flash_update(sw_ref.at[g], kvn_ref[0, 6 + g, pl.ds(win_start, WIN_SPAN), :],
                      None, accw_ref.at[g], pw_ref.at[g], win_valid)

    def slc_pair(i, carry):
        slc_scores(2 * i + 1, sb_ref)
        slc_update(sa_ref, pa_ref, 2 * i, False)
        slc_scores(2 * i + 2, sa_ref)
        slc_update(sb_ref, pb_ref, 2 * i + 1, False)
        return carry
    lax.fori_loop(0, last // 2, slc_pair, 0)
    tail = 2 * (last // 2)

    @pl.when(last > tail)
    def _():
        slc_scores(tail + 1, sb_ref)
        slc_update(sa_ref, pa_ref, tail, False)
        slc_update(sb_ref, pb_ref, tail + 1, True)

    @pl.when(last == tail)
    def _():
        slc_update(sa_ref, pa_ref, tail, True)

    sig = 1.0 / (1.0 + jnp.exp(-misc_ref[...]))
    outs = []
    for g in range(NSA_GROUPS):
        for r in range(NSA_REP):
            lane0 = GATE_LANE0 + 3 * (NSA_REP * g + r)
            rs = slice(NSA_TQ * r, NSA_TQ * (r + 1))
            o = (sig[:, lane0:lane0 + 1] * oc_ref[g, rs, :]
                 + sig[:, lane0 + 1:lane0 + 2] * _flash_finish(accs_ref[g, rs, :])
                 + sig[:, lane0 + 2:lane0 + 3] * _flash_finish(accw_ref[g, rs, :]))
            outs.append(o[:, :NSA_DH])
    o_ref[0] = jnp.concatenate(outs, axis=1).astype(BF16)


def _nsa(qn, kvn, kvcmp, misc, cov_t, B, S):
    nblk = S // SLC_LEN
    ncmp = S // CMP_STRIDE
    nq = S // NSA_TQ
    rows = NSA_ROWS
    return pl.pallas_call(
        _nsa_body,
        grid=(B, nq),
        in_specs=[
            pl.BlockSpec((1, NSA_HEADS, NSA_TQ, LANES), lambda b, c: (b, 0, c, 0)),
            pl.BlockSpec((1, 8, S, LANES), lambda b, c: (b, 0, 0, 0)),
            pl.BlockSpec((4, 1, ncmp, LANES), lambda b, c: (0, b, 0, 0)),
            pl.BlockSpec((NSA_TQ, LANES), lambda b, c: (b * nq + c, 0)),
            pl.BlockSpec(cov_t.shape, lambda b, c: (0, 0)),
        ],
        out_specs=pl.BlockSpec((1, NSA_TQ, NSA_HEADS * NSA_DH), lambda b, c: (b, c, 0)),
        out_shape=jax.ShapeDtypeStruct((B, S, NSA_HEADS * NSA_DH), BF16),
        scratch_shapes=[
            pltpu.VMEM((NSA_GROUPS, S, 2 * LANES), BF16),
            pltpu.VMEM((nblk, NSA_GROUPS * NSA_TQ), F32),
            pltpu.VMEM((NSA_GROUPS * NSA_TQ, LANES), BF16),
            pltpu.VMEM((NSA_GROUPS, rows, SLC_CHUNK), F32),
            pltpu.VMEM((NSA_GROUPS, rows, SLC_CHUNK), F32),
            pltpu.VMEM((NSA_GROUPS, rows, SLC_CHUNK), BF16),
            pltpu.VMEM((NSA_GROUPS, rows, SLC_CHUNK), BF16),
            pltpu.VMEM((NSA_GROUPS, rows, LANES), F32),
            pltpu.VMEM((NSA_GROUPS, rows, LANES), F32),
            pltpu.VMEM((NSA_GROUPS, rows, WIN_SPAN), F32),
            pltpu.VMEM((NSA_GROUPS, rows, WIN_SPAN), BF16),
            pltpu.VMEM((NSA_GROUPS, rows, LANES), F32),
            pltpu.VMEM((NSA_GROUPS, rows, ncmp), F32),
            pltpu.VMEM((NSA_GROUPS, rows, ncmp), F32),
            pltpu.VMEM((NSA_GROUPS, rows, ncmp), BF16),
            pltpu.VMEM((NSA_GROUPS, rows, LANES), F32),
        ],
        compiler_params=_params("arbitrary", "arbitrary"),
        name="nsa",
    )(qn, kvn, kvcmp, misc, cov_t)


MLA_TQ = 512
MLA_CHUNK = 512
MLA_HPB = 4


def _mla_body(q_ref, k_ref, v_ref, o_ref, sa_ref, sb_ref, pa_ref, pb_ref, m_ref, acc_ref):
    qi = pl.program_id(2)
    t_row = qi * MLA_TQ + lax.broadcasted_iota(jnp.int32, (MLA_TQ, 1), 0)
    _flash_reset(m_ref, acc_ref)

    def scores(kc, dst):
        k0 = pl.multiple_of(kc * MLA_CHUNK, MLA_CHUNK)
        for j in range(MLA_HPB):
            dst[j] = _dot_nt(q_ref[0, j], k_ref[0, j, pl.ds(k0, MLA_CHUNK), :])

    def update(src, p_ref, kc, causal):
        k0 = pl.multiple_of(kc * MLA_CHUNK, MLA_CHUNK)
        mask = None
        if causal:
            mask = (k0 + lax.broadcasted_iota(jnp.int32, (1, MLA_CHUNK), 1)) <= t_row
        for j in range(MLA_HPB):
            _flash_update(src.at[j], v_ref[0, j, pl.ds(k0, MLA_CHUNK), :],
                          m_ref.at[j], acc_ref.at[j], p_ref.at[j], mask)

    scores(0, sa_ref)

    def pair(i, carry):
        scores(2 * i + 1, sb_ref)
        update(sa_ref, pa_ref, 2 * i, False)
        scores(2 * i + 2, sa_ref)
        update(sb_ref, pb_ref, 2 * i + 1, False)
        return carry
    lax.fori_loop(0, qi // 2, pair, 0)
    tail = 2 * (qi // 2)

    @pl.when(qi > tail)
    def _():
        scores(tail + 1, sb_ref)
        update(sa_ref, pa_ref, tail, False)
        update(sb_ref, pb_ref, tail + 1, True)

    @pl.when(qi == tail)
    def _():
        update(sa_ref, pa_ref, tail, True)
    o_ref[0] = jnp.concatenate([_flash_finish(acc_ref[j])[:, :MLA_V] for j in range(MLA_HPB)],
                               axis=1).astype(BF16)


def _mla(q, k, v, B, S):
    return pl.pallas_call(
        _mla_body,
        grid=(B, MLA_HEADS // MLA_HPB, S // MLA_TQ),
        in_specs=[
            pl.BlockSpec((1, MLA_HPB, MLA_TQ, LANES), lambda b, h, i: (b, h, i, 0)),
            pl.BlockSpec((1, MLA_HPB, S, LANES), lambda b, h, i: (b, h, 0, 0)),
            pl.BlockSpec((1, MLA_HPB, S, LANES), lambda b, h, i: (b, h, 0, 0)),
        ],
        out_specs=pl.BlockSpec((1, MLA_TQ, MLA_HPB * MLA_V), lambda b, h, i: (b, i, h)),
        out_shape=jax.ShapeDtypeStruct((B, S, MLA_HEADS * MLA_V), BF16),
        scratch_shapes=[
            pltpu.VMEM((MLA_HPB, MLA_TQ, MLA_CHUNK), F32),
            pltpu.VMEM((MLA_HPB, MLA_TQ, MLA_CHUNK), F32),
            pltpu.VMEM((MLA_HPB, MLA_TQ, MLA_CHUNK), BF16),
            pltpu.VMEM((MLA_HPB, MLA_TQ, MLA_CHUNK), BF16),
            pltpu.VMEM((MLA_HPB, MLA_TQ, LANES), F32),
            pltpu.VMEM((MLA_HPB, MLA_TQ, LANES), F32),
        ],
        compiler_params=_params("arbitrary", "arbitrary", "arbitrary"),
        name="mla",
    )(q, k, v)


def _mem_kv_body(m_ref, wk_ref, wv_ref, k_ref, v_ref):
    mb = m_ref[...].astype(BF16)
    k_ref[...] = _dot(mb, wk_ref[...]).astype(BF16)
    v_ref[...] = _dot(mb, wv_ref[...]).astype(BF16)


def _mem_kv(memf, wk, wv, tm):
    R, D = memf.shape
    tok = lambda i: (i, 0)
    const = lambda i: (0, 0)
    return pl.pallas_call(
        _mem_kv_body,
        grid=(R // tm,),
        in_specs=[pl.BlockSpec((tm, D), tok), pl.BlockSpec(wk.shape, const),
                  pl.BlockSpec(wv.shape, const)],
        out_specs=[pl.BlockSpec((tm, D), tok), pl.BlockSpec((tm, D), tok)],
        out_shape=[jax.ShapeDtypeStruct((R, D), BF16), jax.ShapeDtypeStruct((R, D), BF16)],
        compiler_params=_params("arbitrary"),
        name="mem_kv",
    )(memf, wk, wv)


def _mem_attn_body(x0_ref, on_ref, om_ref, wmix_ref, g1_ref, b1_ref,
                   k_ref, v_ref, wq_ref, wo_ref, g_ref, b_ref, o_ref, *, alpha):
    half = on_ref.shape[1]
    mix = _dot(on_ref[...], wmix_ref[:half, :]) + _dot(om_ref[...], wmix_ref[half:, :])
    x = _layer_norm(alpha * x0_ref[...] + mix, g1_ref[...], b1_ref[...])
    D = x.shape[1]
    dh = D // MEM_HEADS
    q = (_dot(x.astype(BF16), wq_ref[...]) * (dh ** -0.5 * LOG2E)).astype(BF16)
    outs = []
    for h in range(MEM_HEADS):
        cs = slice(dh * h, dh * (h + 1))
        s = _dot_nt(q[:, cs], k_ref[0, :, cs])
        p = jnp.exp2(s - jnp.max(s, axis=1, keepdims=True))
        l = jnp.sum(p, axis=1, keepdims=True)
        outs.append((_dot(p.astype(BF16), v_ref[0, :, cs]) * (1.0 / l)).astype(BF16))
    o = jnp.concatenate(outs, axis=1)
    y = _dot(o, wo_ref[...])
    o_ref[...] = _layer_norm(alpha * x + y, g_ref[...], b_ref[...])


def _mem_attn(xf, o_nsa, o_mla, w_o, g1, b1, k_mem, v_mem, wq, wo, g, b, alpha, S, tm):
    T, D = xf.shape
    nst = S // tm
    M = k_mem.shape[1]
    tok = lambda i: (i, 0)
    const = lambda i: (0, 0)
    memb = lambda i: (i // nst, 0, 0)
    return pl.pallas_call(
        functools.partial(_mem_attn_body, alpha=alpha),
        grid=(T // tm,),
        in_specs=[
            pl.BlockSpec((tm, D), tok),
            pl.BlockSpec((tm, o_nsa.shape[1]), tok),
            pl.BlockSpec((tm, o_mla.shape[1]), tok),
            pl.BlockSpec(w_o.shape, const),
            pl.BlockSpec((1, D), const),
            pl.BlockSpec((1, D), const),
            pl.BlockSpec((1, M, D), memb),
            pl.BlockSpec((1, M, D), memb),
            pl.BlockSpec(wq.shape, const),
            pl.BlockSpec(wo.shape, const),
            pl.BlockSpec((1, D), const),
            pl.BlockSpec((1, D), const),
        ],
        out_specs=pl.BlockSpec((tm, D), tok),
        out_shape=jax.ShapeDtypeStruct((T, D), F32),
        compiler_params=_params("arbitrary"),
        name="mem_attn",
    )(xf, o_nsa, o_mla, w_o, g1, b1, k_mem, v_mem, wq, wo, g, b)


HALO = 8


FFN_SLAB = 256
FFN_TM = 512


def _ffn_body(x_ref, xh_ref, wg_ref, wu_ref, cw_ref, cb_ref, wd_ref, g_ref, b_ref, o_ref,
              act_ref, *, alpha, seq_tiles):
    i = pl.program_id(0)
    x = x_ref[...]
    xb = x.astype(BF16)
    xhb = xh_ref[...].astype(BF16)
    tm = x.shape[0]
    row = lax.broadcasted_iota(jnp.int32, (tm, 1), 0)
    seq_start = i % seq_tiles == 0
    for c0 in range(0, wg_ref.shape[1], FFN_SLAB):
        cs = slice(c0, c0 + FFN_SLAB)
        gate = _dot(xb, wg_ref[:, cs])
        up = _dot(xb, wu_ref[:, cs])
        halo = jnp.where(seq_start, 0.0, _dot(xhb, wg_ref[:, cs]))
        g1 = jnp.where(row == 0, halo[HALO - 1:HALO], pltpu.roll(gate, 1, 0))
        g2 = jnp.where(row == 0, halo[HALO - 2:HALO - 1],
                       jnp.where(row == 1, halo[HALO - 1:HALO], pltpu.roll(gate, 2, 0)))
        conv = cw_ref[0:1, cs] * g2 + cw_ref[1:2, cs] * g1 + cw_ref[2:3, cs] * gate + cb_ref[:, cs]
        act_ref[:, cs] = (conv * (1.0 / (1.0 + jnp.exp(-conv))) * up).astype(BF16)
    y = _dot(act_ref[...], wd_ref[...])
    o_ref[...] = _layer_norm(alpha * x + y, g_ref[...], b_ref[...])


def _ffn(xf, wg, wu, cw, cb, wd, g, b, alpha, S, tm):
    T, D = xf.shape
    dff = wg.shape[1]
    assert dff % FFN_SLAB == 0
    tok = lambda i: (i, 0)
    const = lambda i: (0, 0)
    return pl.pallas_call(
        functools.partial(_ffn_body, alpha=alpha, seq_tiles=S // tm),
        grid=(T // tm,),
        in_specs=[
            pl.BlockSpec((tm, D), tok),
            pl.BlockSpec((HALO, D), lambda i: (jnp.maximum(i * (tm // HALO) - 1, 0), 0)),
            pl.BlockSpec((D, dff), const),
            pl.BlockSpec((D, dff), const),
            pl.BlockSpec((CONV_WIDTH, dff), const),
            pl.BlockSpec((1, dff), const),
            pl.BlockSpec((dff, D), const),
            pl.BlockSpec((1, D), const),
            pl.BlockSpec((1, D), const),
        ],
        out_specs=pl.BlockSpec((tm, D), tok),
        out_shape=jax.ShapeDtypeStruct((T, D), F32),
        scratch_shapes=[pltpu.VMEM((tm, dff), BF16)],
        compiler_params=_params("arbitrary"),
        name="ffn",
    )(xf, xf, wg, wu, cw, cb, wd, g, b)


def _inv_freq_row(dim, lane_lo, lane_hi, period):
    inv = ROPE_THETA ** (-np.arange(0, dim, 2, dtype=np.float64) / dim)
    row = np.zeros((1, LANES), np.float32)
    for lane in range(lane_lo, lane_hi):
        row[0, lane] = inv[(lane % period) % (dim // 2)]
    return jnp.asarray(row)


def _cover_t(S):
    nc = S // CMP_STRIDE
    ns = S // SLC_LEN
    cs = np.arange(nc)[:, None] * CMP_STRIDE
    ss = np.arange(ns)[None, :] * SLC_LEN
    cover = np.clip(np.minimum(cs + CMP_LEN, ss + SLC_LEN) - np.maximum(cs, ss), 0, None) / CMP_LEN
    cover[nc - 1:] = 0.0
    return jnp.asarray(cover.T, dtype=BF16)


def _permute_w_in(w):
    D = w.shape[0]
    c1 = NSA_HEADS * NSA_DH
    c2 = c1 + 3 * 2 * NSA_GROUPS * NSA_DH
    c3 = c2 + 3 * NSA_HEADS
    c4 = c3 + MLA_Q_RANK
    c5 = c4 + MLA_KV_RANK
    c6 = c5 + MLA_ROPE
    half = MLA_ROPE // 2
    z = lambda n: jnp.zeros((D, n), w.dtype)
    misc = jnp.concatenate(
        [w[:, c5 + half:c6], z(GATE_LANE0 - half), w[:, c2:c3], z(MLA_PE1 - GATE_LANE0 - (c3 - c2)),
         w[:, c5:c5 + half], z(LANES - MLA_PE1 - half)], axis=1)
    return jnp.concatenate([w[:, :c2], w[:, c3:c5], misc], axis=1).astype(BF16)


def _mla_head_lanes(nope, pe):
    r, H, _ = nope.shape
    half = MLA_ROPE // 2
    split = MLA_PE1 - half
    pad = jnp.zeros((r, H, LANES - MLA_NOPE - MLA_ROPE), nope.dtype)
    return jnp.concatenate([pe[..., half:], nope[..., :split], pe[..., :half], nope[..., split:], pad],
                           axis=2).reshape(r, H * LANES)


def _permute_w_uq(w):
    r = w.shape[0]
    w3 = w.reshape(r, MLA_HEADS, MLA_NOPE + MLA_ROPE)
    return _mla_head_lanes(w3[..., :MLA_NOPE], w3[..., MLA_NOPE:]).astype(BF16)


def _permute_w_ukv(w):
    r = w.shape[0]
    w3 = w.reshape(r, MLA_HEADS, MLA_NOPE + MLA_V)
    k = _mla_head_lanes(w3[..., :MLA_NOPE], jnp.zeros((r, MLA_HEADS, MLA_ROPE), w.dtype))
    v = _pad_lanes(w3[..., MLA_NOPE:]).reshape(r, MLA_HEADS * LANES)
    return jnp.concatenate([k, v], axis=1).astype(BF16)


def _pad_lanes(a):
    return jnp.concatenate([a, jnp.zeros(a.shape[:-1] + (LANES - a.shape[-1],), a.dtype)], axis=-1)


def kernel(x, mem, positions, w_in, nsa_k_pos, nsa_ck_w1, nsa_ck_b1, nsa_ck_w2, nsa_ck_b2,
           nsa_v_pos, nsa_cv_w1, nsa_cv_b1, nsa_cv_w2, nsa_cv_b2,
           mla_q_norm, mla_w_uq, mla_kv_norm, mla_w_ukv, w_o, ln1_g, ln1_b,
           mem_wq, mem_wk, mem_wv, mem_wo, ln2_g, ln2_b,
           ffn_w_up, ffn_conv_w, ffn_conv_b, ffn_w_down, ln3_g, ln3_b):
    B, S, D = x.shape
    T = B * S
    depth = w_in.shape[0]
    alpha = (2.0 * depth) ** 0.25
    d_ff = ffn_w_down.shape[1]
    tm = min(512, S)
    assert S % MLA_TQ == 0 and S >= WIN_SPAN and S % tm == 0
    assert (B * mem.shape[1]) % 256 == 0

    pos = positions.reshape(T, 1)
    pos_cmp = positions[:, CMP_LEN - 1::CMP_STRIDE]
    pos_cmp = jnp.concatenate([pos_cmp, pos_cmp[:, -1:]], axis=1)[:, :, None]
    inv_cmp = _inv_freq_row(NSA_DH, 0, NSA_DH, NSA_DH)
    inv_tok = (_inv_freq_row(NSA_DH, 0, NSA_DH // 2, NSA_DH)
               + _inv_freq_row(MLA_ROPE, NSA_DH // 2, NSA_DH // 2 + MLA_ROPE // 2, MLA_ROPE // 2))
    cov_t = _cover_t(S)
    memf = mem.reshape(B * mem.shape[1], D)

    xf = x.reshape(T, D)
    for l in range(depth):
        qn, kvn, kvc, misc, q_m, k_m, v_m = _inproj(
            pos, inv_tok, xf, _permute_w_in(w_in[l]), mla_q_norm[l][None, :],
            mla_kv_norm[l][None, :], _permute_w_uq(mla_w_uq[l]), _permute_w_ukv(mla_w_ukv[l]),
            B, S, tm)
        kvcmp = _compress(
            pos_cmp, inv_cmp, kvc,
            *_compress_weights(nsa_k_pos[l], nsa_ck_w1[l], nsa_ck_b1[l], nsa_ck_w2[l], nsa_ck_b2[l],
                               nsa_v_pos[l], nsa_cv_w1[l], nsa_cv_b1[l], nsa_cv_w2[l], nsa_cv_b2[l]),
            B, S)
        o_nsa = _nsa(qn, kvn, kvcmp, misc, cov_t, B, S)
        o_mla = _mla(q_m, k_m, v_m, B, S)
        k_mem, v_mem = _mem_kv(memf, mem_wk[l].astype(BF16), mem_wv[l].astype(BF16), 256)
        xf = _mem_attn(xf, o_nsa.reshape(T, -1), o_mla.reshape(T, -1), w_o[l].astype(BF16),
                       ln1_g[l][None, :], ln1_b[l][None, :],
                       k_mem.reshape(B, -1, D), v_mem.reshape(B, -1, D),
                       mem_wq[l].astype(BF16), mem_wo[l].astype(BF16),
                       ln2_g[l][None, :], ln2_b[l][None, :], alpha, S, tm)
        xf = _ffn(xf, ffn_w_up[l][:, :d_ff].astype(BF16), ffn_w_up[l][:, d_ff:].astype(BF16),
                  ffn_conv_w[l], ffn_conv_b[l][None, :], ffn_w_down[l].astype(BF16),
                  ln3_g[l][None, :], ln3_b[l][None, :], alpha, S, FFN_TM)
    return xf.reshape(B, S, D)
```

```python
import functools
import math

import numpy as np
import jax
import jax.numpy as jnp
from jax import lax
from jax.experimental import pallas as pl
from jax.experimental.pallas import tpu as pltpu

F32 = jnp.float32
BF16 = jnp.bfloat16

NSA_HEADS = 8
NSA_GROUPS = 2
NSA_REP = NSA_HEADS // NSA_GROUPS
NSA_DH = 64
CMP_STRIDE = 16
CMP_LEN = 32
SLC_LEN = 64
TOPN = 16
WINDOW = 512
CMP_HIDDEN = 128
FORCE_BONUS = 1e4
MLA_HEADS = 8
MLA_Q_RANK = 384
MLA_KV_RANK = 256
MLA_NOPE = 64
MLA_ROPE = 32
MLA_V = 64
MEM_HEADS = 4
CONV_WIDTH = 3
ROPE_THETA = 10000.0
LN_EPS = 1e-5
RMS_EPS = 1e-6
NEG_INF = -1e30
LOG2E = math.log2(math.e)

LANES = 128
VMEM_LIMIT = 60 * 1024 * 1024

C_Q = 0
C_KVC = 512
C_KVN = 768
C_LAT = 1280
C_MISC = 1920
IN_COLS_PAD = 2048
GATE_LANE0 = MLA_ROPE
MLA_PE2 = 0
MLA_PE1 = 64
ONES_LANE = 64
ONES_LANE_G1 = 0


def _dot(a, b):
    return jnp.dot(a, b, preferred_element_type=F32)


def _dot_nt(a, b):
    return lax.dot_general(a, b, (((1,), (1,)), ((), ())), preferred_element_type=F32)


def _layer_norm(y, g, b):
    mu = jnp.mean(y, axis=-1, keepdims=True)
    d = y - mu
    var = jnp.mean(d * d, axis=-1, keepdims=True)
    return d * lax.rsqrt(var + LN_EPS) * g + b


def _params(*sem):
    return pltpu.CompilerParams(dimension_semantics=sem, vmem_limit_bytes=VMEM_LIMIT)


def _rope_tables(pos_col, inv_row, half):
    ang = pos_col * inv_row
    cos = jnp.cos(ang)
    sin = jnp.sin(ang)
    lane = lax.broadcasted_iota(jnp.int32, (1, LANES), 1)
    upper = (lane & (2 * half - 1)) >= half
    rot = inv_row != 0.0
    sin_hi = jnp.where(upper & rot, sin, 0.0)
    sin_lo = jnp.where(upper | (~rot), 0.0, -sin)
    return cos, sin_hi, sin_lo


def _apply_rope(v, tabs, half):
    cos, sin_hi, sin_lo = tabs
    return v * cos + pltpu.roll(v, half, 1) * sin_hi + pltpu.roll(v, LANES - half, 1) * sin_lo


def _rms_norm(v, g):
    return v * lax.rsqrt(jnp.mean(v * v, axis=-1, keepdims=True) + RMS_EPS) * g


def _inproj_body(pos_ref, inv_ref, x_ref, w_ref, gq_ref, gkv_ref, wq_ref, wkv_ref,
                 qn_ref, kvn_ref, kvc_ref, misc_ref, qm_ref, km_ref, vm_ref):
    xb = x_ref[...].astype(BF16)
    lane = lax.broadcasted_iota(jnp.int32, (1, LANES), 1)
    low = lane < NSA_DH
    qscale = NSA_DH ** -0.5 * LOG2E

    ang = pos_ref[...].astype(F32) * inv_ref[...]
    cos, sin = jnp.cos(ang), jnp.sin(ang)
    nf, mf = NSA_DH // 2, MLA_ROPE // 2

    def tile_nsa(t):
        t = jnp.where(lane < nf, t, 0.0)
        t = t + pltpu.roll(t, nf, 1)
        return t + pltpu.roll(t, 2 * nf, 1)

    def place_mla(t):
        t = jnp.where((lane >= nf) & (lane < nf + mf), t, 0.0)
        return pltpu.roll(t, MLA_PE1 - nf, 1) + pltpu.roll(t, LANES + MLA_PE2 - nf, 1)

    upper = (lane & (NSA_DH - 1)) >= nf
    sin_n = tile_nsa(sin)
    tabs = (tile_nsa(cos), jnp.where(upper, sin_n, 0.0), jnp.where(upper, 0.0, -sin_n))
    pe1 = (lane >= MLA_PE1) & (lane < MLA_PE1 + mf)
    pe2 = (lane >= MLA_PE2) & (lane < MLA_PE2 + mf)
    pe_lanes = pe1 | pe2
    sin_m = place_mla(sin)
    cos_m = jnp.where(pe_lanes, place_mla(cos), 1.0)
    sin_m = jnp.where(pe1, -sin_m, jnp.where(pe2, sin_m, 0.0))

    def rope_mla(v):
        return v * cos_m + pltpu.roll(v, LANES // 2, 1) * sin_m

    def proj(c0, n):
        return _dot(xb, w_ref[:, c0:c0 + n])

    def split_store(v, ref, idx_lo, idx_hi, pad=0.0):
        ref[0, idx_lo] = jnp.where(low, v, pad).astype(BF16)
        ref[0, idx_hi] = jnp.where(low, pltpu.roll(v, NSA_DH, 1), pad).astype(BF16)

    ones_pad = jnp.where(lane == ONES_LANE, 1.0, 0.0)
    ones_pad_g1 = jnp.where(lane == ONES_LANE_G1, 1.0, 0.0)

    for slab in range(2):
        h = proj(C_Q + 256 * slab, 256)
        for j in range(2):
            r = _apply_rope(h[:, LANES * j:LANES * (j + 1)], tabs, NSA_DH // 2) * qscale
            split_store(r, qn_ref, 4 * slab + 2 * j, 4 * slab + 2 * j + 1)

    h = proj(C_KVC, 256)
    kvc_ref[0] = h[:, :LANES]
    kvc_ref[1] = h[:, LANES:]

    for slab in range(2):
        h = proj(C_KVN + 256 * slab, 256)
        k = _apply_rope(h[:, :LANES], tabs, NSA_DH // 2)
        split_store(k, kvn_ref, 4 * slab, 4 * slab + 1)
        v = h[:, LANES:]
        kvn_ref[0, 4 * slab + 2] = jnp.where(low, v, ones_pad).astype(BF16)
        kvn_ref[0, 4 * slab + 3] = jnp.where(low, ones_pad_g1, v).astype(BF16)

    lat = [proj(C_LAT + 256 * i, 256) for i in range((IN_COLS_PAD - C_LAT) // 256)]
    misc = lat[2][:, LANES:]
    misc_ref[...] = misc
    mq = jnp.concatenate([lat[0], lat[1][:, :LANES]], axis=1)
    mkv = jnp.concatenate([lat[1][:, LANES:], lat[2][:, :LANES]], axis=1)
    mscale = (MLA_NOPE + MLA_ROPE) ** -0.5 * LOG2E
    qn = _rms_norm(mq, gq_ref[...]).astype(BF16)
    kvn = _rms_norm(mkv, gkv_ref[...]).astype(BF16)
    kpe = jnp.where(pe_lanes, rope_mla(misc), 0.0)
    kcols = MLA_HEADS * LANES
    for slab in range(MLA_HEADS // 2):
        cs = slice(256 * slab, 256 * (slab + 1))
        hq = _dot(qn, wq_ref[:, cs])
        hk = _dot(kvn, wkv_ref[:, cs])
        hv = _dot(kvn, wkv_ref[:, kcols + 256 * slab:kcols + 256 * (slab + 1)])
        for j in range(2):
            ls = slice(LANES * j, LANES * (j + 1))
            qm_ref[0, 2 * slab + j] = (rope_mla(hq[:, ls]) * mscale).astype(BF16)
            km_ref[0, 2 * slab + j] = (hk[:, ls] + kpe).astype(BF16)
            vm_ref[0, 2 * slab + j] = (hv[:, ls] + (ones_pad_g1 if j else ones_pad)).astype(BF16)


def _inproj(pos, inv_row, xf, w_in_p, gq, gkv, wq_p, wkv_p, B, S, tm):
    T = B * S
    nst = S // tm
    tok = lambda i: (i, 0)
    const = lambda i: (0, 0)
    head_blk = lambda i: (i // nst, 0, i % nst, 0)
    heads = jax.ShapeDtypeStruct((B, 8, S, LANES), BF16)
    return pl.pallas_call(
        _inproj_body,
        grid=(T // tm,),
        in_specs=[
            pl.BlockSpec((tm, 1), tok),
            pl.BlockSpec((1, LANES), const),
            pl.BlockSpec((tm, xf.shape[1]), tok),
            pl.BlockSpec(w_in_p.shape, const),
            pl.BlockSpec(gq.shape, const),
            pl.BlockSpec(gkv.shape, const),
            pl.BlockSpec(wq_p.shape, const),
            pl.BlockSpec(wkv_p.shape, const),
        ],
        out_specs=[
            pl.BlockSpec((1, 8, tm, LANES), head_blk),
            pl.BlockSpec((1, 8, tm, LANES), head_blk),
            pl.BlockSpec((2, tm, LANES), lambda i: (0, i, 0)),
            pl.BlockSpec((tm, LANES), tok),
            pl.BlockSpec((1, 8, tm, LANES), head_blk),
            pl.BlockSpec((1, 8, tm, LANES), head_blk),
            pl.BlockSpec((1, 8, tm, LANES), head_blk),
        ],
        out_shape=[
            heads,
            heads,
            jax.ShapeDtypeStruct((2, T, LANES), F32),
            jax.ShapeDtypeStruct((T, LANES), F32),
            heads,
            heads,
            heads,
        ],
        compiler_params=_params("arbitrary"),
        name="inproj",
    )(pos, inv_row, xf, w_in_p, gq, gkv, wq_p, wkv_p)


def _compress_body(pos_ref, inv_ref, x_ref, pe_ref, w1_ref, b1_ref, w2_ref, b2_ref, o_ref):
    is_k = pl.program_id(0) == 0
    nch = o_ref.shape[2]
    a1 = jnp.zeros((nch, NSA_GROUPS * CMP_HIDDEN), F32)
    a2 = jnp.zeros((nch, NSA_GROUPS * CMP_HIDDEN), F32)
    for l in range(CMP_STRIDE):
        xl = x_ref.at[0, 0][pl.ds(l, nch, stride=CMP_STRIDE), :]
        a1 = a1 + _dot((xl + pe_ref[0, l:l + 1, :]).astype(BF16), w1_ref[0, l])
        a2 = a2 + _dot((xl + pe_ref[0, CMP_STRIDE + l:CMP_STRIDE + l + 1, :]).astype(BF16),
                       w1_ref[0, CMP_STRIDE + l])
    pre = a1 + pltpu.roll(a2, nch - 1, 0) + b1_ref[0]
    hid = jax.nn.gelu(pre, approximate=True)
    out = _dot(hid.astype(BF16), w2_ref[0]) + b2_ref[0]
    tabs = _rope_tables(pos_ref[0].astype(F32), inv_ref[...], NSA_DH // 2)
    row = lax.broadcasted_iota(jnp.int32, (nch, 1), 0)
    for g in range(NSA_GROUPS):
        og = out[:, LANES * g:LANES * (g + 1)]
        og = jnp.where(is_k, _apply_rope(og, tabs, NSA_DH // 2), og)
        o_ref[g, 0] = jnp.where(row < nch - 1, og, 0.0).astype(BF16)


def _compress(pos_cmp, inv_nsa, kvc, pe, w1, b1, w2, b2, B, S):
    nch = S // CMP_STRIDE
    x = kvc.reshape(2, B, S, LANES)
    kv = lambda j, b: (j, 0, 0)
    kv4 = lambda j, b: (j, 0, 0, 0)
    return pl.pallas_call(
        _compress_body,
        grid=(2, B),
        in_specs=[
            pl.BlockSpec((1, nch, 1), lambda j, b: (b, 0, 0)),
            pl.BlockSpec((1, LANES), lambda j, b: (0, 0)),
            pl.BlockSpec((1, 1, S, LANES), lambda j, b: (j, b, 0, 0)),
            pl.BlockSpec((1,) + pe.shape[1:], kv),
            pl.BlockSpec((1,) + w1.shape[1:], kv4),
            pl.BlockSpec((1,) + b1.shape[1:], kv),
            pl.BlockSpec((1,) + w2.shape[1:], kv),
            pl.BlockSpec((1,) + b2.shape[1:], kv),
        ],
        out_specs=pl.BlockSpec((NSA_GROUPS, 1, nch, LANES), lambda j, b: (j, b, 0, 0)),
        out_shape=jax.ShapeDtypeStruct((2 * NSA_GROUPS, B, nch, LANES), BF16),
        compiler_params=_params("arbitrary", "arbitrary"),
        name="compress",
    )(pos_cmp, inv_nsa, x, pe, w1, b1, w2, b2)


def _compress_weights(k_pos, k_w1, k_b1, k_w2, k_b2, v_pos, v_w1, v_b1, v_w2, v_b2):
    def one(pos, w1, b1, w2, b2, g1_high):
        w1l = w1.reshape(CMP_LEN, NSA_DH, CMP_HIDDEN)
        z1 = jnp.zeros_like(w1l)
        w1bd = jnp.concatenate([jnp.concatenate([w1l, z1], axis=2),
                                jnp.concatenate([z1, w1l], axis=2)], axis=1)
        pad = lambda a: _pad_lanes(a)
        pad1 = (lambda a: jnp.roll(_pad_lanes(a), LANES - a.shape[-1], axis=-1)) if g1_high else pad
        z2 = jnp.zeros_like(pad(w2))
        w2bd = jnp.concatenate([jnp.concatenate([pad(w2), z2], axis=1),
                                jnp.concatenate([z2, pad1(w2)], axis=1)], axis=0)
        return (jnp.tile(pos, (1, NSA_GROUPS)), w1bd.astype(BF16), jnp.tile(b1, NSA_GROUPS)[None, :],
                w2bd.astype(BF16), jnp.concatenate([pad(b2), pad1(b2)])[None, :])
    k = one(k_pos, k_w1, k_b1, k_w2, k_b2, False)
    v = one(v_pos, v_w1, v_b1, v_w2, v_b2, True)
    return tuple(jnp.stack([a, b]) for a, b in zip(k, v))


STRIP = 64


def _lane_tile(col, n):
    reps = [col] * (n // LANES)
    if n % LANES:
        reps.append(col[:, :n % LANES])
    return reps[0] if len(reps) == 1 else jnp.concatenate(reps, axis=1)


def _flash_reset(m_ref, acc_ref):
    m_ref[...] = jnp.full(m_ref.shape, NEG_INF, F32)
    acc_ref[...] = jnp.zeros(acc_ref.shape, F32)


def _flash_update(s_ref, v, m_ref, acc_ref, p_ref, mask=None):
    rows, n = s_ref.shape
    for r in range(rows // STRIP):
        rs = slice(STRIP * r, STRIP * (r + 1))
        s = s_ref[rs, :]
        if mask is not None:
            s = jnp.where(mask[rs], s, NEG_INF)
        if m_ref is None:
            p_ref[rs, :] = jnp.exp2(s - jnp.max(s, axis=1, keepdims=True)).astype(BF16)
            continue
        m_old = m_ref[rs, :]
        m_new = jnp.maximum(m_old, jnp.max(s, axis=1, keepdims=True))
        p_ref[rs, :] = jnp.exp2(s - _lane_tile(m_new, n)).astype(BF16)
        acc_ref[rs, :] = jnp.exp2(m_old - m_new) * acc_ref[rs, :]
        m_ref[rs, :] = m_new
    if m_ref is None:
        acc_ref[...] = _dot(p_ref[...], v)
    else:
        acc_ref[...] += _dot(p_ref[...], v)


NSA_TQ = 256
NSA_ROWS = NSA_REP * NSA_TQ
SLC_CHUNK = 512
WIN_SPAN = WINDOW + NSA_TQ
BIAS_LANE0 = LANES


def _nsa_body(q_ref, kvn_ref, kvc_ref, misc_ref, cov_ref, o_ref,
              kaug_ref, score_ref, qbias_ref, sa_ref, sb_ref, pa_ref, pb_ref, ms_ref, accs_ref,
              sw_ref, pw_ref, accw_ref, sc_ref, pn_ref, pc_ref, oc_ref):
    c = pl.program_id(1)
    rows = NSA_ROWS
    t_row = c * NSA_TQ + (lax.broadcasted_iota(jnp.int32, (rows, 1), 0) & (NSA_TQ - 1))
    ncmp = kvc_ref.shape[2]
    nblk = kaug_ref.shape[1] // SLC_LEN

    @pl.when(c == 0)
    def _():
        nkeys = kaug_ref.shape[1]
        key_blk = lax.broadcasted_iota(jnp.int32, (nkeys, LANES), 0) // SLC_LEN
        onehot = jnp.where(key_blk == lax.broadcasted_iota(jnp.int32, (nkeys, LANES), 1), 1.0, 0.0)
        for g in range(NSA_GROUPS):
            kaug_ref[g, :, :BIAS_LANE0] = kvn_ref[0, g]
            kaug_ref[g, :, BIAS_LANE0:] = onehot.astype(BF16)

    qs = [q_ref[0, NSA_REP * g:NSA_REP * (g + 1)].reshape(rows, LANES) for g in range(NSA_GROUPS)]

    cmp_valid = (CMP_STRIDE * lax.broadcasted_iota(jnp.int32, (1, ncmp), 1) + CMP_LEN - 1) <= t_row
    for g in range(NSA_GROUPS):
        sc_ref[g] = _dot_nt(qs[g], kvc_ref[g, 0])
    win_start = pl.multiple_of(jnp.maximum(c * NSA_TQ - WINDOW, 0), NSA_TQ)
    for g in range(NSA_GROUPS):
        sw_ref[g] = _dot_nt(qs[g], kvn_ref[0, 4 + g, pl.ds(win_start, WIN_SPAN), :])
    for g in range(NSA_GROUPS):
        for r in range(rows // STRIP):
            rs = slice(STRIP * r, STRIP * (r + 1))
            s = jnp.where(cmp_valid[rs], sc_ref[g, rs, :], NEG_INF)
            p = jnp.where(cmp_valid[rs], jnp.exp2(s - jnp.max(s, axis=1, keepdims=True)), 0.0)
            l = jnp.sum(p, axis=1, keepdims=True)
            p = p * jnp.where(l > 0.0, 1.0 / l, 0.0)
            pn_ref[g, rs, :] = p
            pc_ref[g, rs, :] = p.astype(BF16)
        oc_ref[g] = _dot(pc_ref[g], kvc_ref[NSA_GROUPS + g, 0])

    def select(nb):
        width = NSA_GROUPS * NSA_TQ
        ps = jnp.concatenate(
            [sum(pn_ref[g, NSA_TQ * r:NSA_TQ * (r + 1), :] for r in range(NSA_REP))
             for g in range(NSA_GROUPS)], axis=0)
        hi = ps.astype(BF16)
        lo = (ps - hi.astype(F32)).astype(BF16)
        imp = _dot_nt(cov_ref[:nb, :], hi) + _dot_nt(cov_ref[:nb, :], lo)
        jidx = lax.broadcasted_iota(jnp.int32, (nb, width), 0)
        lane_q = lax.broadcasted_iota(jnp.int32, (1, width), 1) & (NSA_TQ - 1)
        cur = c * (NSA_TQ // SLC_LEN) + lane_q // SLC_LEN
        forced = (jidx == 0) | (jidx == cur) | (jidx == cur - 1)
        score = jnp.where(jidx <= cur, jnp.where(forced, FORCE_BONUS, imp), NEG_INF)
        score_ref[:nb, :] = score
        sub = 8
        cnt = [jnp.zeros((sub, width), F32) for _ in range(nb // sub)]
        tiles = [score[sub * v:sub * (v + 1)] for v in range(nb // sub)]
        sidx = lax.broadcasted_iota(jnp.int32, (sub, width), 0)
        for jp in range(nb):
            rowv = jnp.broadcast_to(score_ref[jp:jp + 1, :], (sub, width))
            for v in range(nb // sub):
                if sub * v > jp:
                    cnt[v] = jnp.where(rowv >= tiles[v], cnt[v] + 1.0, cnt[v])
                elif sub * v + sub - 1 <= jp:
                    cnt[v] = jnp.where(rowv > tiles[v], cnt[v] + 1.0, cnt[v])
                else:
                    ge = jnp.where(rowv >= tiles[v], cnt[v] + 1.0, cnt[v])
                    gt = jnp.where(rowv > tiles[v], cnt[v] + 1.0, cnt[v])
                    cnt[v] = jnp.where(sidx + sub * v > jp, ge, gt)
        rank = jnp.concatenate(cnt, axis=0)
        bias = jnp.where(rank < float(TOPN), 0.0, NEG_INF)
        bias = jnp.concatenate([bias, jnp.zeros((LANES - nb, width), F32)], axis=0)
        qbias_ref[...] = bias.T.astype(BF16)

    visible = (c + 1) * (NSA_TQ // SLC_LEN)

    @pl.when(visible <= TOPN)
    def _():
        qbias_ref[...] = jnp.zeros(qbias_ref.shape, BF16)

    bounds = [TOPN] + [nb for nb in (nblk // 2, 3 * nblk // 4) if TOPN < nb < nblk] + [nblk]
    for lo_nb, nb in zip(bounds[:-1], bounds[1:]):
        pl.when((visible > lo_nb) & (visible <= nb))(functools.partial(select, nb))

    _flash_reset(ms_ref, accs_ref)
    qas = [jnp.concatenate(
        [qs[g], jnp.concatenate([qbias_ref[NSA_TQ * g:NSA_TQ * (g + 1), :]] * NSA_REP, axis=0)],
        axis=1) for g in range(NSA_GROUPS)]
    last = c // (SLC_CHUNK // NSA_TQ)

    def slc_scores(kc, dst):
        k0 = pl.multiple_of(kc * SLC_CHUNK, SLC_CHUNK)
        for g in range(NSA_GROUPS):
            dst[g] = _dot_nt(qas[g], kaug_ref[g, pl.ds(k0, SLC_CHUNK), :])

    def slc_update(src, p_ref, kc, causal):
        k0 = pl.multiple_of(kc * SLC_CHUNK, SLC_CHUNK)
        mask = None
        if causal:
            mask = (k0 + lax.broadcasted_iota(jnp.int32, (1, SLC_CHUNK), 1)) <= t_row
        for g in range(NSA_GROUPS):
            _flash_update(src.at[g], kvn_ref[0, 2 + g, pl.ds(k0, SLC_CHUNK), :],
                          ms_ref.at[g], accs_ref.at[g], p_ref.at[g], mask)

    slc_scores(0, sa_ref)

    diff = t_row - (win_start + lax.broadcasted_iota(jnp.int32, (1, WIN_SPAN), 1))
    win_valid = (diff >= 0) & (diff < WINDOW)
    for g in range(NSA_GROUPS):
        _flash_update(sw_ref.at[g], kvn_ref[0, 6 + g, pl.ds(win_start, WIN_SPAN), :],
                      None, accw_ref.at[g], pw_ref.at[g], win_valid)

    def slc_pair(i, carry):
        slc_scores(2 * i + 1, sb_ref)
        slc_update(sa_ref, pa_ref, 2 * i, False)
        slc_scores(2 * i + 2, sa_ref)
        slc_update(sb_ref, pb_ref, 2 * i + 1, False)
        return carry
    lax.fori_loop(0, last // 2, slc_pair, 0)
    tail = 2 * (last // 2)

    @pl.when(last > tail)
    def _():
        slc_scores(tail + 1, sb_ref)
        slc_update(sa_ref, pa_ref, tail, False)
        slc_update(sb_ref, pb_ref, tail + 1, True)

    @pl.when(last == tail)
    def _():
        slc_update(sa_ref, pa_ref, tail, True)

    sig = 1.0 / (1.0 + jnp.exp(-misc_ref[...]))
    low = lax.broadcasted_iota(jnp.int32, (NSA_TQ, LANES), 1) < NSA_DH
    denom_lane = jnp.where(low, ONES_LANE, ONES_LANE_G1)

    def pair(ref, rs):
        a0, a1 = ref[0, rs, :], ref[1, rs, :]
        return jnp.where(low, a0, a1), jnp.where(low, a1, a0)

    outs = []
    for r in range(NSA_REP):
        rs = slice(NSA_TQ * r, NSA_TQ * (r + 1))
        gate = [jnp.take_along_axis(
            sig, jnp.where(low, GATE_LANE0 + 3 * r + br, GATE_LANE0 + 3 * (NSA_REP + r) + br), axis=1)
            for br in range(3)]
        o_cmp, _ = pair(oc_ref, rs)
        o_slc, l_slc = pair(accs_ref, rs)
        o_win, l_win = pair(accw_ref, rs)
        outs.append(gate[0] * o_cmp
                    + (gate[1] / jnp.take_along_axis(l_slc, denom_lane, axis=1)) * o_slc
                    + (gate[2] / jnp.take_along_axis(l_win, denom_lane, axis=1)) * o_win)
    o_ref[0] = jnp.concatenate(outs, axis=1).astype(BF16)


def _nsa(qn, kvn, kvcmp, misc, cov_t, B, S):
    nblk = S // SLC_LEN
    ncmp = S // CMP_STRIDE
    nq = S // NSA_TQ
    rows = NSA_ROWS
    return pl.pallas_call(
        _nsa_body,
        grid=(B, nq),
        in_specs=[
            pl.BlockSpec((1, NSA_HEADS, NSA_TQ, LANES), lambda b, c: (b, 0, c, 0)),
            pl.BlockSpec((1, 8, S, LANES), lambda b, c: (b, 0, 0, 0)),
            pl.BlockSpec((4, 1, ncmp, LANES), lambda b, c: (0, b, 0, 0)),
            pl.BlockSpec((NSA_TQ, LANES), lambda b, c: (b * nq + c, 0)),
            pl.BlockSpec(cov_t.shape, lambda b, c: (0, 0)),
        ],
        out_specs=pl.BlockSpec((1, NSA_TQ, NSA_HEADS * NSA_DH), lambda b, c: (b, c, 0)),
        out_shape=jax.ShapeDtypeStruct((B, S, NSA_HEADS * NSA_DH), BF16),
        scratch_shapes=[
            pltpu.VMEM((NSA_GROUPS, S, 2 * LANES), BF16),
            pltpu.VMEM((nblk, NSA_GROUPS * NSA_TQ), F32),
            pltpu.VMEM((NSA_GROUPS * NSA_TQ, LANES), BF16),
            pltpu.VMEM((NSA_GROUPS, rows, SLC_CHUNK), F32),
            pltpu.VMEM((NSA_GROUPS, rows, SLC_CHUNK), F32),
            pltpu.VMEM((NSA_GROUPS, rows, SLC_CHUNK), BF16),
            pltpu.VMEM((NSA_GROUPS, rows, SLC_CHUNK), BF16),
            pltpu.VMEM((NSA_GROUPS, rows, LANES), F32),
            pltpu.VMEM((NSA_GROUPS, rows, LANES), F32),
            pltpu.VMEM((NSA_GROUPS, rows, WIN_SPAN), F32),
            pltpu.VMEM((NSA_GROUPS, rows, WIN_SPAN), BF16),
            pltpu.VMEM((NSA_GROUPS, rows, LANES), F32),
            pltpu.VMEM((NSA_GROUPS, rows, ncmp), F32),
            pltpu.VMEM((NSA_GROUPS, rows, ncmp), F32),
            pltpu.VMEM((NSA_GROUPS, rows, ncmp), BF16),
            pltpu.VMEM((NSA_GROUPS, rows, LANES), F32),
        ],
        compiler_params=_params("arbitrary", "arbitrary"),
        name="nsa",
    )(qn, kvn, kvcmp, misc, cov_t)


MLA_TQ = 512
MLA_CHUNK = 512
MLA_HPB = 4


def _mla_body(q_ref, k_ref, v_ref, o_ref, sa_ref, sb_ref, pa_ref, pb_ref, m_ref, acc_ref):
    qi = pl.program_id(2)
    t_row = qi * MLA_TQ + lax.broadcasted_iota(jnp.int32, (MLA_TQ, 1), 0)
    _flash_reset(m_ref, acc_ref)

    def scores(kc, dst):
        k0 = pl.multiple_of(kc * MLA_CHUNK, MLA_CHUNK)
        for j in range(MLA_HPB):
            dst[j] = _dot_nt(q_ref[0, j], k_ref[0, j, pl.ds(k0, MLA_CHUNK), :])

    def update(src, p_ref, kc, causal):
        k0 = pl.multiple_of(kc * MLA_CHUNK, MLA_CHUNK)
        mask = None
        if causal:
            mask = (k0 + lax.broadcasted_iota(jnp.int32, (1, MLA_CHUNK), 1)) <= t_row
        for j in range(MLA_HPB):
            _flash_update(src.at[j], v_ref[0, j, pl.ds(k0, MLA_CHUNK), :],
                          m_ref.at[j], acc_ref.at[j], p_ref.at[j], mask)

    scores(0, sa_ref)

    def pair(i, carry):
        scores(2 * i + 1, sb_ref)
        update(sa_ref, pa_ref, 2 * i, False)
        scores(2 * i + 2, sa_ref)
        update(sb_ref, pb_ref, 2 * i + 1, False)
        return carry
    lax.fori_loop(0, qi // 2, pair, 0)
    tail = 2 * (qi // 2)

    @pl.when(qi > tail)
    def _():
        scores(tail + 1, sb_ref)
        update(sa_ref, pa_ref, tail, False)
        update(sb_ref, pb_ref, tail + 1, True)

    @pl.when(qi == tail)
    def _():
        update(sa_ref, pa_ref, tail, True)

    low = lax.broadcasted_iota(jnp.int32, (MLA_TQ, LANES), 1) < MLA_V
    denom_lane = jnp.where(low, ONES_LANE, ONES_LANE_G1)
    outs = []
    for j in range(0, MLA_HPB, 2):
        a0, a1 = acc_ref[j], acc_ref[j + 1]
        denom = jnp.take_along_axis(jnp.where(low, a1, a0), denom_lane, axis=1)
        outs.append(jnp.where(low, a0, a1) * (1.0 / denom))
    o_ref[0] = jnp.concatenate(outs, axis=1).astype(BF16)


def _mla(q, k, v, B, S):
    return pl.pallas_call(
        _mla_body,
        grid=(B, MLA_HEADS // MLA_HPB, S // MLA_TQ),
        in_specs=[
            pl.BlockSpec((1, MLA_HPB, MLA_TQ, LANES), lambda b, h, i: (b, h, i, 0)),
            pl.BlockSpec((1, MLA_HPB, S, LANES), lambda b, h, i: (b, h, 0, 0)),
            pl.BlockSpec((1, MLA_HPB, S, LANES), lambda b, h, i: (b, h, 0, 0)),
        ],
        out_specs=pl.BlockSpec((1, MLA_TQ, MLA_HPB * MLA_V), lambda b, h, i: (b, i, h)),
        out_shape=jax.ShapeDtypeStruct((B, S, MLA_HEADS * MLA_V), BF16),
        scratch_shapes=[
            pltpu.VMEM((MLA_HPB, MLA_TQ, MLA_CHUNK), F32),
            pltpu.VMEM((MLA_HPB, MLA_TQ, MLA_CHUNK), F32),
            pltpu.VMEM((MLA_HPB, MLA_TQ, MLA_CHUNK), BF16),
            pltpu.VMEM((MLA_HPB, MLA_TQ, MLA_CHUNK), BF16),
            pltpu.VMEM((MLA_HPB, MLA_TQ, LANES), F32),
            pltpu.VMEM((MLA_HPB, MLA_TQ, LANES), F32),
        ],
        compiler_params=_params("arbitrary", "arbitrary", "arbitrary"),
        name="mla",
    )(q, k, v)


def _mem_kv_body(m_ref, wk_ref, wv_ref, k_ref, v_ref):
    mb = m_ref[...].astype(BF16)
    k_ref[...] = _dot(mb, wk_ref[...]).astype(BF16)
    v_ref[...] = _dot(mb, wv_ref[...]).astype(BF16)


def _mem_kv(memf, wk, wv, tm):
    R, D = memf.shape
    tok = lambda i: (i, 0)
    const = lambda i: (0, 0)
    return pl.pallas_call(
        _mem_kv_body,
        grid=(R // tm,),
        in_specs=[pl.BlockSpec((tm, D), tok), pl.BlockSpec(wk.shape, const),
                  pl.BlockSpec(wv.shape, const)],
        out_specs=[pl.BlockSpec((tm, D), tok), pl.BlockSpec((tm, D), tok)],
        out_shape=[jax.ShapeDtypeStruct((R, D), BF16), jax.ShapeDtypeStruct((R, D), BF16)],
        compiler_params=_params("arbitrary"),
        name="mem_kv",
    )(memf, wk, wv)


def _mem_attn_body(x0_ref, on_ref, om_ref, wmix_ref, g1_ref, b1_ref,
                   k_ref, v_ref, wq_ref, wo_ref, g_ref, b_ref, o_ref, *, alpha):
    half = on_ref.shape[1]
    mix = _dot(on_ref[...], wmix_ref[:half, :]) + _dot(om_ref[...], wmix_ref[half:, :])
    x = _layer_norm(alpha * x0_ref[...] + mix, g1_ref[...], b1_ref[...])
    D = x.shape[1]
    dh = D // MEM_HEADS
    q = (_dot(x.astype(BF16), wq_ref[...]) * (dh ** -0.5 * LOG2E)).astype(BF16)
    outs = []
    for h in range(MEM_HEADS):
        cs = slice(dh * h, dh * (h + 1))
        s = _dot_nt(q[:, cs], k_ref[0, :, cs])
        p = jnp.exp2(s - jnp.max(s, axis=1, keepdims=True))
        l = jnp.sum(p, axis=1, keepdims=True)
        outs.append((_dot(p.astype(BF16), v_ref[0, :, cs]) * (1.0 / l)).astype(BF16))
    o = jnp.concatenate(outs, axis=1)
    y = _dot(o, wo_ref[...])
    o_ref[...] = _layer_norm(alpha * x + y, g_ref[...], b_ref[...])


def _mem_attn(xf, o_nsa, o_mla, w_o, g1, b1, k_mem, v_mem, wq, wo, g, b, alpha, S, tm):
    T, D = xf.shape
    nst = S // tm
    M = k_mem.shape[1]
    tok = lambda i: (i, 0)
    const = lambda i: (0, 0)
    memb = lambda i: (i // nst, 0, 0)
    return pl.pallas_call(
        functools.partial(_mem_attn_body, alpha=alpha),
        grid=(T // tm,),
        in_specs=[
            pl.BlockSpec((tm, D), tok),
            pl.BlockSpec((tm, o_nsa.shape[1]), tok),
            pl.BlockSpec((tm, o_mla.shape[1]), tok),
            pl.BlockSpec(w_o.shape, const),
            pl.BlockSpec((1, D), const),
            pl.BlockSpec((1, D), const),
            pl.BlockSpec((1, M, D), memb),
            pl.BlockSpec((1, M, D), memb),
            pl.BlockSpec(wq.shape, const),
            pl.BlockSpec(wo.shape, const),
            pl.BlockSpec((1, D), const),
            pl.BlockSpec((1, D), const),
        ],
        out_specs=pl.BlockSpec((tm, D), tok),
        out_shape=jax.ShapeDtypeStruct((T, D), F32),
        compiler_params=_params("arbitrary"),
        name="mem_attn",
    )(xf, o_nsa, o_mla, w_o, g1, b1, k_mem, v_mem, wq, wo, g, b)


HALO = 8


FFN_SLAB = 256
FFN_TM = 512


def _ffn_body(x_ref, xh_ref, wg_ref, wu_ref, cw_ref, cb_ref, wd_ref, g_ref, b_ref, o_ref,
              act_ref, *, alpha, seq_tiles):
    i = pl.program_id(0)
    x = x_ref[...]
    xb = x.astype(BF16)
    xhb = xh_ref[...].astype(BF16)
    tm = x.shape[0]
    row = lax.broadcasted_iota(jnp.int32, (tm, 1), 0)
    seq_start = i % seq_tiles == 0
    for c0 in range(0, wg_ref.shape[1], FFN_SLAB):
        cs = slice(c0, c0 + FFN_SLAB)
        gate = _dot(xb, wg_ref[:, cs])
        up = _dot(xb, wu_ref[:, cs])
        halo = jnp.where(seq_start, 0.0, _dot(xhb, wg_ref[:, cs]))
        g1 = jnp.where(row == 0, halo[HALO - 1:HALO], pltpu.roll(gate, 1, 0))
        g2 = jnp.where(row == 0, halo[HALO - 2:HALO - 1],
                       jnp.where(row == 1, halo[HALO - 1:HALO], pltpu.roll(gate, 2, 0)))
        conv = cw_ref[0:1, cs] * g2 + cw_ref[1:2, cs] * g1 + cw_ref[2:3, cs] * gate + cb_ref[:, cs]
        act_ref[:, cs] = (conv * (1.0 / (1.0 + jnp.exp(-conv))) * up).astype(BF16)
    y = _dot(act_ref[...], wd_ref[...])
    o_ref[...] = _layer_norm(alpha * x + y, g_ref[...], b_ref[...])


def _ffn(xf, wg, wu, cw, cb, wd, g, b, alpha, S, tm):
    T, D = xf.shape
    dff = wg.shape[1]
    assert dff % FFN_SLAB == 0
    tok = lambda i: (i, 0)
    const = lambda i: (0, 0)
    return pl.pallas_call(
        functools.partial(_ffn_body, alpha=alpha, seq_tiles=S // tm),
        grid=(T // tm,),
        in_specs=[
            pl.BlockSpec((tm, D), tok),
            pl.BlockSpec((HALO, D), lambda i: (jnp.maximum(i * (tm // HALO) - 1, 0), 0)),
            pl.BlockSpec((D, dff), const),
            pl.BlockSpec((D, dff), const),
            pl.BlockSpec((CONV_WIDTH, dff), const),
            pl.BlockSpec((1, dff), const),
            pl.BlockSpec((dff, D), const),
            pl.BlockSpec((1, D), const),
            pl.BlockSpec((1, D), const),
        ],
        out_specs=pl.BlockSpec((tm, D), tok),
        out_shape=jax.ShapeDtypeStruct((T, D), F32),
        scratch_shapes=[pltpu.VMEM((tm, dff), BF16)],
        compiler_params=_params("arbitrary"),
        name="ffn",
    )(xf, xf, wg, wu, cw, cb, wd, g, b)


def _inv_freq_row(dim, lane_lo, lane_hi, period):
    inv = ROPE_THETA ** (-np.arange(0, dim, 2, dtype=np.float64) / dim)
    row = np.zeros((1, LANES), np.float32)
    for lane in range(lane_lo, lane_hi):
        row[0, lane] = inv[(lane % period) % (dim // 2)]
    return jnp.asarray(row)


def _cover_t(S):
    nc = S // CMP_STRIDE
    ns = S // SLC_LEN
    cs = np.arange(nc)[:, None] * CMP_STRIDE
    ss = np.arange(ns)[None, :] * SLC_LEN
    cover = np.clip(np.minimum(cs + CMP_LEN, ss + SLC_LEN) - np.maximum(cs, ss), 0, None) / CMP_LEN
    cover[nc - 1:] = 0.0
    return jnp.asarray(cover.T, dtype=BF16)


def _permute_w_in(w):
    D = w.shape[0]
    c1 = NSA_HEADS * NSA_DH
    c2 = c1 + 3 * 2 * NSA_GROUPS * NSA_DH
    c3 = c2 + 3 * NSA_HEADS
    c4 = c3 + MLA_Q_RANK
    c5 = c4 + MLA_KV_RANK
    c6 = c5 + MLA_ROPE
    half = MLA_ROPE // 2
    z = lambda n: jnp.zeros((D, n), w.dtype)
    misc = jnp.concatenate(
        [w[:, c5 + half:c6], z(GATE_LANE0 - half), w[:, c2:c3], z(MLA_PE1 - GATE_LANE0 - (c3 - c2)),
         w[:, c5:c5 + half], z(LANES - MLA_PE1 - half)], axis=1)
    return jnp.concatenate([w[:, :c2], w[:, c3:c5], misc], axis=1).astype(BF16)


def _mla_head_lanes(nope, pe):
    r, H, _ = nope.shape
    half = MLA_ROPE // 2
    split = MLA_PE1 - half
    pad = jnp.zeros((r, H, LANES - MLA_NOPE - MLA_ROPE), nope.dtype)
    return jnp.concatenate([pe[..., half:], nope[..., :split], pe[..., :half], nope[..., split:], pad],
                           axis=2).reshape(r, H * LANES)


def _permute_w_uq(w):
    r = w.shape[0]
    w3 = w.reshape(r, MLA_HEADS, MLA_NOPE + MLA_ROPE)
    return _mla_head_lanes(w3[..., :MLA_NOPE], w3[..., MLA_NOPE:]).astype(BF16)


def _permute_w_ukv(w):
    r = w.shape[0]
    w3 = w.reshape(r, MLA_HEADS, MLA_NOPE + MLA_V)
    k = _mla_head_lanes(w3[..., :MLA_NOPE], jnp.zeros((r, MLA_HEADS, MLA_ROPE), w.dtype))
    v = _pad_lanes(w3[..., MLA_NOPE:]).reshape(r, MLA_HEADS // 2, 2, LANES)
    v = jnp.stack([v[:, :, 0], jnp.roll(v[:, :, 1], LANES - MLA_V, axis=-1)], axis=2)
    return jnp.concatenate([k, v.reshape(r, MLA_HEADS * LANES)], axis=1).astype(BF16)


def _permute_w_o(w):
    n = NSA_HEADS * NSA_DH
    wn = w[:n].reshape(NSA_GROUPS, NSA_REP, NSA_DH, -1).transpose(1, 0, 2, 3).reshape(n, -1)
    return jnp.concatenate([wn, w[n:]], axis=0).astype(BF16)


def _pad_lanes(a):
    return jnp.concatenate([a, jnp.zeros(a.shape[:-1] + (LANES - a.shape[-1],), a.dtype)], axis=-1)


def kernel(x, mem, positions, w_in, nsa_k_pos, nsa_ck_w1, nsa_ck_b1, nsa_ck_w2, nsa_ck_b2,
           nsa_v_pos, nsa_cv_w1, nsa_cv_b1, nsa_cv_w2, nsa_cv_b2,
           mla_q_norm, mla_w_uq, mla_kv_norm, mla_w_ukv, w_o, ln1_g, ln1_b,
           mem_wq, mem_wk, mem_wv, mem_wo, ln2_g, ln2_b,
           ffn_w_up, ffn_conv_w, ffn_conv_b, ffn_w_down, ln3_g, ln3_b):
    B, S, D = x.shape
    T = B * S
    depth = w_in.shape[0]
    alpha = (2.0 * depth) ** 0.25
    d_ff = ffn_w_down.shape[1]
    tm = min(512, S)
    assert S % MLA_TQ == 0 and S >= WIN_SPAN and S % tm == 0
    assert (B * mem.shape[1]) % 256 == 0

    pos = positions.reshape(T, 1)
    pos_cmp = positions[:, CMP_LEN - 1::CMP_STRIDE]
    pos_cmp = jnp.concatenate([pos_cmp, pos_cmp[:, -1:]], axis=1)[:, :, None]
    inv_cmp = _inv_freq_row(NSA_DH, 0, NSA_DH, NSA_DH)
    inv_tok = (_inv_freq_row(NSA_DH, 0, NSA_DH // 2, NSA_DH)
               + _inv_freq_row(MLA_ROPE, NSA_DH // 2, NSA_DH // 2 + MLA_ROPE // 2, MLA_ROPE // 2))
    cov_t = _cover_t(S)
    memf = mem.reshape(B * mem.shape[1], D)

    xf = x.reshape(T, D)
    for l in range(depth):
        qn, kvn, kvc, misc, q_m, k_m, v_m = _inproj(
            pos, inv_tok, xf, _permute_w_in(w_in[l]), mla_q_norm[l][None, :],
            mla_kv_norm[l][None, :], _permute_w_uq(mla_w_uq[l]), _permute_w_ukv(mla_w_ukv[l]),
            B, S, tm)
        kvcmp = _compress(
            pos_cmp, inv_cmp, kvc,
            *_compress_weights(nsa_k_pos[l], nsa_ck_w1[l], nsa_ck_b1[l], nsa_ck_w2[l], nsa_ck_b2[l],
                               nsa_v_pos[l], nsa_cv_w1[l], nsa_cv_b1[l], nsa_cv_w2[l], nsa_cv_b2[l]),
            B, S)
        o_nsa = _nsa(qn, kvn, kvcmp, misc, cov_t, B, S)
        o_mla = _mla(q_m, k_m, v_m, B, S)
        k_mem, v_mem = _mem_kv(memf, mem_wk[l].astype(BF16), mem_wv[l].astype(BF16), 256)
        xf = _mem_attn(xf, o_nsa.reshape(T, -1), o_mla.reshape(T, -1), _permute_w_o(w_o[l]),
                       ln1_g[l][None, :], ln1_b[l][None, :],
                       k_mem.reshape(B, -1, D), v_mem.reshape(B, -1, D),
                       mem_wq[l].astype(BF16), mem_wo[l].astype(BF16),
                       ln2_g[l][None, :], ln2_b[l][None, :], alpha, S, tm)
        xf = _ffn(xf, ffn_w_up[l][:, :d_ff].astype(BF16), ffn_w_up[l][:, d_ff:].astype(BF16),
                  ffn_conv_w[l], ffn_conv_b[l][None, :], ffn_w_down[l].astype(BF16),
                  ln3_g[l][None, :], ln3_b[l][None, :], alpha, S, FFN_TM)
    return xf.reshape(B, S, D)
```

```python
import functools
import math

import numpy as np
import jax
import jax.numpy as jnp
from jax import lax
from jax.experimental import pallas as pl
from jax.experimental.pallas import tpu as pltpu

F32 = jnp.float32
BF16 = jnp.bfloat16

NSA_HEADS = 8
NSA_GROUPS = 2
NSA_REP = NSA_HEADS // NSA_GROUPS
NSA_DH = 64
CMP_STRIDE = 16
CMP_LEN = 32
SLC_LEN = 64
TOPN = 16
WINDOW = 512
CMP_HIDDEN = 128
FORCE_BONUS = 1e4
MLA_HEADS = 8
MLA_Q_RANK = 384
MLA_KV_RANK = 256
MLA_NOPE = 64
MLA_ROPE = 32
MLA_V = 64
MEM_HEADS = 4
CONV_WIDTH = 3
ROPE_THETA = 10000.0
LN_EPS = 1e-5
RMS_EPS = 1e-6
NEG_INF = -1e30
LOG2E = math.log2(math.e)

LANES = 128
VMEM_LIMIT = 60 * 1024 * 1024

C_Q = 0
C_KVC = 512
C_KVN = 768
C_LAT = 1280
C_MISC = 1920
IN_COLS_PAD = 2048
GATE_LANE0 = MLA_ROPE
MLA_PE2 = 0
MLA_PE1 = 64
ONES_LANE = 64
ONES_LANE_G1 = 0


def _dot(a, b):
    return jnp.dot(a, b, preferred_element_type=F32)


def _dot_nt(a, b):
    return lax.dot_general(a, b, (((1,), (1,)), ((), ())), preferred_element_type=F32)


def _layer_norm(y, g, b):
    mu = jnp.mean(y, axis=-1, keepdims=True)
    d = y - mu
    var = jnp.mean(d * d, axis=-1, keepdims=True)
    return d * lax.rsqrt(var + LN_EPS) * g + b


def _params(*sem):
    return pltpu.CompilerParams(dimension_semantics=sem, vmem_limit_bytes=VMEM_LIMIT)


def _rope_tables(pos_col, inv_row, half):
    ang = pos_col * inv_row
    cos = jnp.cos(ang)
    sin = jnp.sin(ang)
    lane = lax.broadcasted_iota(jnp.int32, (1, LANES), 1)
    upper = (lane & (2 * half - 1)) >= half
    rot = inv_row != 0.0
    sin_hi = jnp.where(upper & rot, sin, 0.0)
    sin_lo = jnp.where(upper | (~rot), 0.0, -sin)
    return cos, sin_hi, sin_lo


def _apply_rope(v, tabs, half):
    cos, sin_hi, sin_lo = tabs
    return v * cos + pltpu.roll(v, half, 1) * sin_hi + pltpu.roll(v, LANES - half, 1) * sin_lo


def _rms_norm(v, g):
    return v * lax.rsqrt(jnp.mean(v * v, axis=-1, keepdims=True) + RMS_EPS) * g


def _inproj_body(pos_ref, inv_ref, x_ref, w_ref, gq_ref, gkv_ref, wq_ref, wkv_ref,
                 qn_ref, kvn_ref, kvc_ref, misc_ref, qm_ref, km_ref, vm_ref):
    xb = x_ref[...].astype(BF16)
    lane = lax.broadcasted_iota(jnp.int32, (1, LANES), 1)
    low = lane < NSA_DH
    qscale = NSA_DH ** -0.5 * LOG2E

    ang = pos_ref[...].astype(F32) * inv_ref[...]
    cos, sin = jnp.cos(ang), jnp.sin(ang)
    nf, mf = NSA_DH // 2, MLA_ROPE // 2

    def tile_nsa(t):
        t = jnp.where(lane < nf, t, 0.0)
        t = t + pltpu.roll(t, nf, 1)
        return t + pltpu.roll(t, 2 * nf, 1)

    def place_mla(t):
        t = jnp.where((lane >= nf) & (lane < nf + mf), t, 0.0)
        return pltpu.roll(t, MLA_PE1 - nf, 1) + pltpu.roll(t, LANES + MLA_PE2 - nf, 1)

    upper = (lane & (NSA_DH - 1)) >= nf
    sin_n = tile_nsa(sin)
    tabs = (tile_nsa(cos), jnp.where(upper, sin_n, 0.0), jnp.where(upper, 0.0, -sin_n))
    pe1 = (lane >= MLA_PE1) & (lane < MLA_PE1 + mf)
    pe2 = (lane >= MLA_PE2) & (lane < MLA_PE2 + mf)
    pe_lanes = pe1 | pe2
    sin_m = place_mla(sin)
    cos_m = jnp.where(pe_lanes, place_mla(cos), 1.0)
    sin_m = jnp.where(pe1, -sin_m, jnp.where(pe2, sin_m, 0.0))

    def rope_mla(v):
        return v * cos_m + pltpu.roll(v, LANES // 2, 1) * sin_m

    def proj(c0, n):
        return _dot(xb, w_ref[:, c0:c0 + n])

    def split_store(v, ref, idx_lo, idx_hi, pad=0.0):
        ref[0, idx_lo] = jnp.where(low, v, pad).astype(BF16)
        ref[0, idx_hi] = jnp.where(low, pltpu.roll(v, NSA_DH, 1), pad).astype(BF16)

    ones_pad = jnp.where(lane == ONES_LANE, 1.0, 0.0)
    ones_pad_g1 = jnp.where(lane == ONES_LANE_G1, 1.0, 0.0)

    for slab in range(2):
        h = proj(C_Q + 256 * slab, 256)
        for j in range(2):
            r = _apply_rope(h[:, LANES * j:LANES * (j + 1)], tabs, NSA_DH // 2) * qscale
            split_store(r, qn_ref, 4 * slab + 2 * j, 4 * slab + 2 * j + 1)

    h = proj(C_KVC, 256)
    kvc_ref[0] = h[:, :LANES]
    kvc_ref[1] = h[:, LANES:]

    for slab in range(2):
        h = proj(C_KVN + 256 * slab, 256)
        k = _apply_rope(h[:, :LANES], tabs, NSA_DH // 2)
        split_store(k, kvn_ref, 4 * slab, 4 * slab + 1)
        v = h[:, LANES:]
        kvn_ref[0, 4 * slab + 2] = jnp.where(low, v, ones_pad).astype(BF16)
        kvn_ref[0, 4 * slab + 3] = jnp.where(low, ones_pad_g1, v).astype(BF16)

    lat = [proj(C_LAT + 256 * i, 256) for i in range((IN_COLS_PAD - C_LAT) // 256)]
    misc = lat[2][:, LANES:]
    misc_ref[...] = misc
    mq = jnp.concatenate([lat[0], lat[1][:, :LANES]], axis=1)
    mkv = jnp.concatenate([lat[1][:, LANES:], lat[2][:, :LANES]], axis=1)
    mscale = (MLA_NOPE + MLA_ROPE) ** -0.5 * LOG2E
    qn = _rms_norm(mq, gq_ref[...]).astype(BF16)
    kvn = _rms_norm(mkv, gkv_ref[...]).astype(BF16)
    kpe = jnp.where(pe_lanes, rope_mla(misc), 0.0)
    kcols = MLA_HEADS * LANES
    for slab in range(MLA_HEADS // 2):
        cs = slice(256 * slab, 256 * (slab + 1))
        hq = _dot(qn, wq_ref[:, cs])
        hk = _dot(kvn, wkv_ref[:, cs])
        hv = _dot(kvn, wkv_ref[:, kcols + 256 * slab:kcols + 256 * (slab + 1)])
        for j in range(2):
            ls = slice(LANES * j, LANES * (j + 1))
            qm_ref[0, 2 * slab + j] = (rope_mla(hq[:, ls]) * mscale).astype(BF16)
            km_ref[0, 2 * slab + j] = (hk[:, ls] + kpe).astype(BF16)
            vm_ref[0, 2 * slab + j] = (hv[:, ls] + (ones_pad_g1 if j else ones_pad)).astype(BF16)


def _inproj(pos, inv_row, xf, w_in_p, gq, gkv, wq_p, wkv_p, B, S, tm):
    T = B * S
    nst = S // tm
    tok = lambda i: (i, 0)
    const = lambda i: (0, 0)
    head_blk = lambda i: (i // nst, 0, i % nst, 0)
    heads = jax.ShapeDtypeStruct((B, 8, S, LANES), BF16)
    return pl.pallas_call(
        _inproj_body,
        grid=(T // tm,),
        in_specs=[
            pl.BlockSpec((tm, 1), tok),
            pl.BlockSpec((1, LANES), const),
            pl.BlockSpec((tm, xf.shape[1]), tok),
            pl.BlockSpec(w_in_p.shape, const),
            pl.BlockSpec(gq.shape, const),
            pl.BlockSpec(gkv.shape, const),
            pl.BlockSpec(wq_p.shape, const),
            pl.BlockSpec(wkv_p.shape, const),
        ],
        out_specs=[
            pl.BlockSpec((1, 8, tm, LANES), head_blk),
            pl.BlockSpec((1, 8, tm, LANES), head_blk),
            pl.BlockSpec((2, tm, LANES), lambda i: (0, i, 0)),
            pl.BlockSpec((tm, LANES), tok),
            pl.BlockSpec((1, 8, tm, LANES), head_blk),
            pl.BlockSpec((1, 8, tm, LANES), head_blk),
            pl.BlockSpec((1, 8, tm, LANES), head_blk),
        ],
        out_shape=[
            heads,
            heads,
            jax.ShapeDtypeStruct((2, T, LANES), F32),
            jax.ShapeDtypeStruct((T, LANES), F32),
            heads,
            heads,
            heads,
        ],
        compiler_params=_params("arbitrary"),
        name="inproj",
    )(pos, inv_row, xf, w_in_p, gq, gkv, wq_p, wkv_p)


def _compress_body(pos_ref, inv_ref, x_ref, pe_ref, w1_ref, b1_ref, w2_ref, b2_ref, o_ref):
    is_k = pl.program_id(0) == 0
    nch = o_ref.shape[2]
    a1 = jnp.zeros((nch, NSA_GROUPS * CMP_HIDDEN), F32)
    a2 = jnp.zeros((nch, NSA_GROUPS * CMP_HIDDEN), F32)
    for l in range(CMP_STRIDE):
        xl = x_ref.at[0, 0][pl.ds(l, nch, stride=CMP_STRIDE), :]
        a1 = a1 + _dot((xl + pe_ref[0, l:l + 1, :]).astype(BF16), w1_ref[0, l])
        a2 = a2 + _dot((xl + pe_ref[0, CMP_STRIDE + l:CMP_STRIDE + l + 1, :]).astype(BF16),
                       w1_ref[0, CMP_STRIDE + l])
    pre = a1 + pltpu.roll(a2, nch - 1, 0) + b1_ref[0]
    hid = jax.nn.gelu(pre, approximate=True)
    out = _dot(hid.astype(BF16), w2_ref[0]) + b2_ref[0]
    tabs = _rope_tables(pos_ref[0].astype(F32), inv_ref[...], NSA_DH // 2)
    row = lax.broadcasted_iota(jnp.int32, (nch, 1), 0)
    for g in range(NSA_GROUPS):
        og = out[:, LANES * g:LANES * (g + 1)]
        og = jnp.where(is_k, _apply_rope(og, tabs, NSA_DH // 2), og)
        o_ref[g, 0] = jnp.where(row < nch - 1, og, 0.0).astype(BF16)


def _compress(pos_cmp, inv_nsa, kvc, pe, w1, b1, w2, b2, B, S):
    nch = S // CMP_STRIDE
    x = kvc.reshape(2, B, S, LANES)
    kv = lambda j, b: (j, 0, 0)
    kv4 = lambda j, b: (j, 0, 0, 0)
    return pl.pallas_call(
        _compress_body,
        grid=(2, B),
        in_specs=[
            pl.BlockSpec((1, nch, 1), lambda j, b: (b, 0, 0)),
            pl.BlockSpec((1, LANES), lambda j, b: (0, 0)),
            pl.BlockSpec((1, 1, S, LANES), lambda j, b: (j, b, 0, 0)),
            pl.BlockSpec((1,) + pe.shape[1:], kv),
            pl.BlockSpec((1,) + w1.shape[1:], kv4),
            pl.BlockSpec((1,) + b1.shape[1:], kv),
            pl.BlockSpec((1,) + w2.shape[1:], kv),
            pl.BlockSpec((1,) + b2.shape[1:], kv),
        ],
        out_specs=pl.BlockSpec((NSA_GROUPS, 1, nch, LANES), lambda j, b: (j, b, 0, 0)),
        out_shape=jax.ShapeDtypeStruct((2 * NSA_GROUPS, B, nch, LANES), BF16),
        compiler_params=_params("arbitrary", "arbitrary"),
        name="compress",
    )(pos_cmp, inv_nsa, x, pe, w1, b1, w2, b2)


def _compress_weights(k_pos, k_w1, k_b1, k_w2, k_b2, v_pos, v_w1, v_b1, v_w2, v_b2):
    def one(pos, w1, b1, w2, b2, g1_high):
        w1l = w1.reshape(CMP_LEN, NSA_DH, CMP_HIDDEN)
        z1 = jnp.zeros_like(w1l)
        w1bd = jnp.concatenate([jnp.concatenate([w1l, z1], axis=2),
                                jnp.concatenate([z1, w1l], axis=2)], axis=1)
        pad = lambda a: _pad_lanes(a)
        pad1 = (lambda a: jnp.roll(_pad_lanes(a), LANES - a.shape[-1], axis=-1)) if g1_high else pad
        z2 = jnp.zeros_like(pad(w2))
        w2bd = jnp.concatenate([jnp.concatenate([pad(w2), z2], axis=1),
                                jnp.concatenate([z2, pad1(w2)], axis=1)], axis=0)
        return (jnp.tile(pos, (1, NSA_GROUPS)), w1bd.astype(BF16), jnp.tile(b1, NSA_GROUPS)[None, :],
                w2bd.astype(BF16), jnp.concatenate([pad(b2), pad1(b2)])[None, :])
    k = one(k_pos, k_w1, k_b1, k_w2, k_b2, False)
    v = one(v_pos, v_w1, v_b1, v_w2, v_b2, True)
    return tuple(jnp.stack([a, b]) for a, b in zip(k, v))


STRIP = 64


def _lane_tile(col, n):
    reps = [col] * (n // LANES)
    if n % LANES:
        reps.append(col[:, :n % LANES])
    return reps[0] if len(reps) == 1 else jnp.concatenate(reps, axis=1)


def _flash_reset(m_ref, acc_ref):
    m_ref[...] = jnp.full(m_ref.shape, NEG_INF, F32)
    acc_ref[...] = jnp.zeros(acc_ref.shape, F32)


def _flash_update(s_ref, v, m_ref, acc_ref, p_ref, mask=None):
    rows, n = s_ref.shape
    for r in range(rows // STRIP):
        rs = slice(STRIP * r, STRIP * (r + 1))
        s = s_ref[rs, :]
        if mask is not None:
            s = jnp.where(mask[rs], s, NEG_INF)
        if m_ref is None:
            p_ref[rs, :] = jnp.exp2(s - jnp.max(s, axis=1, keepdims=True)).astype(BF16)
            continue
        m_old = m_ref[rs, :]
        m_new = jnp.maximum(m_old, jnp.max(s, axis=1, keepdims=True))
        p_ref[rs, :] = jnp.exp2(s - _lane_tile(m_new, n)).astype(BF16)
        acc_ref[rs, :] = jnp.exp2(m_old - m_new) * acc_ref[rs, :]
        m_ref[rs, :] = m_new
    if m_ref is None:
        acc_ref[...] = _dot(p_ref[...], v)
    else:
        acc_ref[...] += _dot(p_ref[...], v)


NSA_TQ = 256
NSA_ROWS = NSA_REP * NSA_TQ
SLC_CHUNK = 512
WIN_SPAN = WINDOW + NSA_TQ
BIAS_LANE0 = LANES


def _nsa_body(q_ref, kvn_ref, kvc_ref, misc_ref, cov_ref, o_ref,
              kaug_ref, score_ref, qbias_ref, sa_ref, sb_ref, pa_ref, pb_ref, ms_ref, accs_ref,
              sw_ref, pw_ref, accw_ref, sc_ref, pn_ref, pc_ref, oc_ref):
    c = pl.program_id(1)
    rows = NSA_ROWS
    t_row = c * NSA_TQ + (lax.broadcasted_iota(jnp.int32, (rows, 1), 0) & (NSA_TQ - 1))
    ncmp = kvc_ref.shape[2]
    nblk = kaug_ref.shape[1] // SLC_LEN

    @pl.when(c == 0)
    def _():
        nkeys = kaug_ref.shape[1]
        key_blk = lax.broadcasted_iota(jnp.int32, (nkeys, LANES), 0) // SLC_LEN
        onehot = jnp.where(key_blk == lax.broadcasted_iota(jnp.int32, (nkeys, LANES), 1), 1.0, 0.0)
        for g in range(NSA_GROUPS):
            kaug_ref[g, :, :BIAS_LANE0] = kvn_ref[0, g]
            kaug_ref[g, :, BIAS_LANE0:] = onehot.astype(BF16)

    qs = [q_ref[0, NSA_REP * g:NSA_REP * (g + 1)].reshape(rows, LANES) for g in range(NSA_GROUPS)]

    cmp_valid = (CMP_STRIDE * lax.broadcasted_iota(jnp.int32, (1, ncmp), 1) + CMP_LEN - 1) <= t_row
    for g in range(NSA_GROUPS):
        sc_ref[g] = _dot_nt(qs[g], kvc_ref[g, 0])
    win_start = pl.multiple_of(jnp.maximum(c * NSA_TQ - WINDOW, 0), NSA_TQ)

    def window_scores(g):
        sw_ref[g] = _dot_nt(qs[g], kvn_ref[0, 4 + g, pl.ds(win_start, WIN_SPAN), :])

    window_scores(0)
    for g in range(NSA_GROUPS):
        for r in range(rows // STRIP):
            rs = slice(STRIP * r, STRIP * (r + 1))
            s = jnp.where(cmp_valid[rs], sc_ref[g, rs, :], NEG_INF)
            p = jnp.where(cmp_valid[rs], jnp.exp2(s - jnp.max(s, axis=1, keepdims=True)), 0.0)
            l = jnp.sum(p, axis=1, keepdims=True)
            p = p * jnp.where(l > 0.0, 1.0 / l, 0.0)
            pn_ref[g, rs, :] = p
            pc_ref[g, rs, :] = p.astype(BF16)
        oc_ref[g] = _dot(pc_ref[g], kvc_ref[NSA_GROUPS + g, 0])

    def select(nb):
        window_scores(1)
        width = NSA_GROUPS * NSA_TQ
        ps = jnp.concatenate(
            [sum(pn_ref[g, NSA_TQ * r:NSA_TQ * (r + 1), :] for r in range(NSA_REP))
             for g in range(NSA_GROUPS)], axis=0)
        hi = ps.astype(BF16)
        lo = (ps - hi.astype(F32)).astype(BF16)
        imp = _dot_nt(cov_ref[:nb, :], hi) + _dot_nt(cov_ref[:nb, :], lo)
        jidx = lax.broadcasted_iota(jnp.int32, (nb, width), 0)
        lane_q = lax.broadcasted_iota(jnp.int32, (1, width), 1) & (NSA_TQ - 1)
        cur = c * (NSA_TQ // SLC_LEN) + lane_q // SLC_LEN
        forced = (jidx == 0) | (jidx == cur) | (jidx == cur - 1)
        score = jnp.where(jidx <= cur, jnp.where(forced, FORCE_BONUS, imp), NEG_INF)
        score_ref[:nb, :] = score
        sub = 8
        cnt = [jnp.zeros((sub, width), F32) for _ in range(nb // sub)]
        tiles = [score[sub * v:sub * (v + 1)] for v in range(nb // sub)]
        sidx = lax.broadcasted_iota(jnp.int32, (sub, width), 0)
        for jp in range(nb):
            rowv = jnp.broadcast_to(score_ref[jp:jp + 1, :], (sub, width))
            for v in range(nb // sub):
                if sub * v > jp:
                    cnt[v] = jnp.where(rowv >= tiles[v], cnt[v] + 1.0, cnt[v])
                elif sub * v + sub - 1 <= jp:
                    cnt[v] = jnp.where(rowv > tiles[v], cnt[v] + 1.0, cnt[v])
                else:
                    ge = jnp.where(rowv >= tiles[v], cnt[v] + 1.0, cnt[v])
                    gt = jnp.where(rowv > tiles[v], cnt[v] + 1.0, cnt[v])
                    cnt[v] = jnp.where(sidx + sub * v > jp, ge, gt)
        rank = jnp.concatenate(cnt, axis=0)
        bias = jnp.where(rank < float(TOPN), 0.0, NEG_INF)
        bias = jnp.concatenate([bias, jnp.zeros((LANES - nb, width), F32)], axis=0)
        qbias_ref[...] = bias.T.astype(BF16)

    visible = (c + 1) * (NSA_TQ // SLC_LEN)

    @pl.when(visible <= TOPN)
    def _():
        window_scores(1)
        qbias_ref[...] = jnp.zeros(qbias_ref.shape, BF16)

    bounds = [TOPN] + [nb for nb in (nblk // 2, 3 * nblk // 4) if TOPN < nb < nblk] + [nblk]
    for lo_nb, nb in zip(bounds[:-1], bounds[1:]):
        pl.when((visible > lo_nb) & (visible <= nb))(functools.partial(select, nb))

    _flash_reset(ms_ref, accs_ref)
    qas = [jnp.concatenate(
        [qs[g], jnp.concatenate([qbias_ref[NSA_TQ * g:NSA_TQ * (g + 1), :]] * NSA_REP, axis=0)],
        axis=1) for g in range(NSA_GROUPS)]
    last = c // (SLC_CHUNK // NSA_TQ)

    def slc_scores(kc, dst):
        k0 = pl.multiple_of(kc * SLC_CHUNK, SLC_CHUNK)
        for g in range(NSA_GROUPS):
            dst[g] = _dot_nt(qas[g], kaug_ref[g, pl.ds(k0, SLC_CHUNK), :])

    def slc_update(src, p_ref, kc, causal):
        k0 = pl.multiple_of(kc * SLC_CHUNK, SLC_CHUNK)
        mask = None
        if causal:
            mask = (k0 + lax.broadcasted_iota(jnp.int32, (1, SLC_CHUNK), 1)) <= t_row
        for g in range(NSA_GROUPS):
            _flash_update(src.at[g], kvn_ref[0, 2 + g, pl.ds(k0, SLC_CHUNK), :],
                          ms_ref.at[g], accs_ref.at[g], p_ref.at[g], mask)

    slc_scores(0, sa_ref)

    diff = t_row - (win_start + lax.broadcasted_iota(jnp.int32, (1, WIN_SPAN), 1))
    win_valid = (diff >= 0) & (diff < WINDOW)
    for g in range(NSA_GROUPS):
        _flash_update(sw_ref.at[g], kvn_ref[0, 6 + g, pl.ds(win_start, WIN_SPAN), :],
                      None, accw_ref.at[g], pw_ref.at[g], win_valid)

    def slc_pair(i, carry):
        slc_scores(2 * i + 1, sb_ref)
        slc_update(sa_ref, pa_ref, 2 * i, False)
        slc_scores(2 * i + 2, sa_ref)
        slc_update(sb_ref, pb_ref, 2 * i + 1, False)
        return carry
    lax.fori_loop(0, last // 2, slc_pair, 0)
    tail = 2 * (last // 2)

    @pl.when(last > tail)
    def _():
        slc_scores(tail + 1, sb_ref)
        slc_update(sa_ref, pa_ref, tail, False)
        slc_update(sb_ref, pb_ref, tail + 1, True)

    @pl.when(last == tail)
    def _():
        slc_update(sa_ref, pa_ref, tail, True)

    sig = 1.0 / (1.0 + jnp.exp(-misc_ref[...]))
    low = lax.broadcasted_iota(jnp.int32, (NSA_TQ, LANES), 1) < NSA_DH
    denom_lane = jnp.where(low, ONES_LANE, ONES_LANE_G1)

    def pair(ref, rs):
        a0, a1 = ref[0, rs, :], ref[1, rs, :]
        return jnp.where(low, a0, a1), jnp.where(low, a1, a0)

    outs = []
    for r in range(NSA_REP):
        rs = slice(NSA_TQ * r, NSA_TQ * (r + 1))
        gate = [jnp.take_along_axis(
            sig, jnp.where(low, GATE_LANE0 + 3 * r + br, GATE_LANE0 + 3 * (NSA_REP + r) + br), axis=1)
            for br in range(3)]
        o_cmp, _ = pair(oc_ref, rs)
        o_slc, l_slc = pair(accs_ref, rs)
        o_win, l_win = pair(accw_ref, rs)
        outs.append(gate[0] * o_cmp
                    + (gate[1] / jnp.take_along_axis(l_slc, denom_lane, axis=1)) * o_slc
                    + (gate[2] / jnp.take_along_axis(l_win, denom_lane, axis=1)) * o_win)
    o_ref[0] = jnp.concatenate(outs, axis=1).astype(BF16)


def _nsa(qn, kvn, kvcmp, misc, cov_t, B, S):
    nblk = S // SLC_LEN
    ncmp = S // CMP_STRIDE
    nq = S // NSA_TQ
    rows = NSA_ROWS
    return pl.pallas_call(
        _nsa_body,
        grid=(B, nq),
        in_specs=[
            pl.BlockSpec((1, NSA_HEADS, NSA_TQ, LANES), lambda b, c: (b, 0, c, 0)),
            pl.BlockSpec((1, 8, S, LANES), lambda b, c: (b, 0, 0, 0)),
            pl.BlockSpec((4, 1, ncmp, LANES), lambda b, c: (0, b, 0, 0)),
            pl.BlockSpec((NSA_TQ, LANES), lambda b, c: (b * nq + c, 0)),
            pl.BlockSpec(cov_t.shape, lambda b, c: (0, 0)),
        ],
        out_specs=pl.BlockSpec((1, NSA_TQ, NSA_HEADS * NSA_DH), lambda b, c: (b, c, 0)),
        out_shape=jax.ShapeDtypeStruct((B, S, NSA_HEADS * NSA_DH), BF16),
        scratch_shapes=[
            pltpu.VMEM((NSA_GROUPS, S, 2 * LANES), BF16),
            pltpu.VMEM((nblk, NSA_GROUPS * NSA_TQ), F32),
            pltpu.VMEM((NSA_GROUPS * NSA_TQ, LANES), BF16),
            pltpu.VMEM((NSA_GROUPS, rows, SLC_CHUNK), F32),
            pltpu.VMEM((NSA_GROUPS, rows, SLC_CHUNK), F32),
            pltpu.VMEM((NSA_GROUPS, rows, SLC_CHUNK), BF16),
            pltpu.VMEM((NSA_GROUPS, rows, SLC_CHUNK), BF16),
            pltpu.VMEM((NSA_GROUPS, rows, LANES), F32),
            pltpu.VMEM((NSA_GROUPS, rows, LANES), F32),
            pltpu.VMEM((NSA_GROUPS, rows, WIN_SPAN), F32),
            pltpu.VMEM((NSA_GROUPS, rows, WIN_SPAN), BF16),
            pltpu.VMEM((NSA_GROUPS, rows, LANES), F32),
            pltpu.VMEM((NSA_GROUPS, rows, ncmp), F32),
            pltpu.VMEM((NSA_GROUPS, rows, ncmp), F32),
            pltpu.VMEM((NSA_GROUPS, rows, ncmp), BF16),
            pltpu.VMEM((NSA_GROUPS, rows, LANES), F32),
        ],
        compiler_params=_params("arbitrary", "arbitrary"),
        name="nsa",
    )(qn, kvn, kvcmp, misc, cov_t)


MLA_TQ = 512
MLA_CHUNK = 512
MLA_HPB = 4


def _mla_body(q_ref, k_ref, v_ref, o_ref, sa_ref, sb_ref, pa_ref, pb_ref, m_ref, acc_ref):
    qi = pl.program_id(2)
    t_row = qi * MLA_TQ + lax.broadcasted_iota(jnp.int32, (MLA_TQ, 1), 0)
    _flash_reset(m_ref, acc_ref)

    def scores(kc, dst):
        k0 = pl.multiple_of(kc * MLA_CHUNK, MLA_CHUNK)
        for j in range(MLA_HPB):
            dst[j] = _dot_nt(q_ref[0, j], k_ref[0, j, pl.ds(k0, MLA_CHUNK), :])

    def update(src, p_ref, kc, causal):
        k0 = pl.multiple_of(kc * MLA_CHUNK, MLA_CHUNK)
        mask = None
        if causal:
            mask = (k0 + lax.broadcasted_iota(jnp.int32, (1, MLA_CHUNK), 1)) <= t_row
        for j in range(MLA_HPB):
            _flash_update(src.at[j], v_ref[0, j, pl.ds(k0, MLA_CHUNK), :],
                          m_ref.at[j], acc_ref.at[j], p_ref.at[j], mask)

    scores(0, sa_ref)

    def pair(i, carry):
        scores(2 * i + 1, sb_ref)
        update(sa_ref, pa_ref, 2 * i, False)
        scores(2 * i + 2, sa_ref)
        update(sb_ref, pb_ref, 2 * i + 1, False)
        return carry
    lax.fori_loop(0, qi // 2, pair, 0)
    tail = 2 * (qi // 2)

    @pl.when(qi > tail)
    def _():
        scores(tail + 1, sb_ref)
        update(sa_ref, pa_ref, tail, False)
        update(sb_ref, pb_ref, tail + 1, True)

    @pl.when(qi == tail)
    def _():
        update(sa_ref, pa_ref, tail, True)

    low = lax.broadcasted_iota(jnp.int32, (MLA_TQ, LANES), 1) < MLA_V
    denom_lane = jnp.where(low, ONES_LANE, ONES_LANE_G1)
    outs = []
    for j in range(0, MLA_HPB, 2):
        a0, a1 = acc_ref[j], acc_ref[j + 1]
        denom = jnp.take_along_axis(jnp.where(low, a1, a0), denom_lane, axis=1)
        outs.append(jnp.where(low, a0, a1) * (1.0 / denom))
    o_ref[0] = jnp.concatenate(outs, axis=1).astype(BF16)


def _mla(q, k, v, B, S):
    return pl.pallas_call(
        _mla_body,
        grid=(B, MLA_HEADS // MLA_HPB, S // MLA_TQ),
        in_specs=[
            pl.BlockSpec((1, MLA_HPB, MLA_TQ, LANES), lambda b, h, i: (b, h, i, 0)),
            pl.BlockSpec((1, MLA_HPB, S, LANES), lambda b, h, i: (b, h, 0, 0)),
            pl.BlockSpec((1, MLA_HPB, S, LANES), lambda b, h, i: (b, h, 0, 0)),
        ],
        out_specs=pl.BlockSpec((1, MLA_TQ, MLA_HPB * MLA_V), lambda b, h, i: (b, i, h)),
        out_shape=jax.ShapeDtypeStruct((B, S, MLA_HEADS * MLA_V), BF16),
        scratch_shapes=[
            pltpu.VMEM((MLA_HPB, MLA_TQ, MLA_CHUNK), F32),
            pltpu.VMEM((MLA_HPB, MLA_TQ, MLA_CHUNK), F32),
            pltpu.VMEM((MLA_HPB, MLA_TQ, MLA_CHUNK), BF16),
            pltpu.VMEM((MLA_HPB, MLA_TQ, MLA_CHUNK), BF16),
            pltpu.VMEM((MLA_HPB, MLA_TQ, LANES), F32),
            pltpu.VMEM((MLA_HPB, MLA_TQ, LANES), F32),
        ],
        compiler_params=_params("arbitrary", "arbitrary", "arbitrary"),
        name="mla",
    )(q, k, v)


def _mem_kv_body(m_ref, wk_ref, wv_ref, k_ref, v_ref):
    mb = m_ref[...].astype(BF16)
    k_ref[...] = _dot(mb, wk_ref[...]).astype(BF16)
    v_ref[...] = _dot(mb, wv_ref[...]).astype(BF16)


def _mem_kv(memf, wk, wv, tm):
    R, D = memf.shape
    tok = lambda i: (i, 0)
    const = lambda i: (0, 0)
    return pl.pallas_call(
        _mem_kv_body,
        grid=(R // tm,),
        in_specs=[pl.BlockSpec((tm, D), tok), pl.BlockSpec(wk.shape, const),
                  pl.BlockSpec(wv.shape, const)],
        out_specs=[pl.BlockSpec((tm, D), tok), pl.BlockSpec((tm, D), tok)],
        out_shape=[jax.ShapeDtypeStruct((R, D), BF16), jax.ShapeDtypeStruct((R, D), BF16)],
        compiler_params=_params("arbitrary"),
        name="mem_kv",
    )(memf, wk, wv)


def _mem_attn_body(x0_ref, on_ref, om_ref, wmix_ref, g1_ref, b1_ref,
                   k_ref, v_ref, wq_ref, wo_ref, g_ref, b_ref, o_ref, *, alpha):
    half = on_ref.shape[1]
    mix = _dot(on_ref[...], wmix_ref[:half, :]) + _dot(om_ref[...], wmix_ref[half:, :])
    x = _layer_norm(alpha * x0_ref[...] + mix, g1_ref[...], b1_ref[...])
    D = x.shape[1]
    dh = D // MEM_HEADS
    q = (_dot(x.astype(BF16), wq_ref[...]) * (dh ** -0.5 * LOG2E)).astype(BF16)
    outs = []
    for h in range(MEM_HEADS):
        cs = slice(dh * h, dh * (h + 1))
        s = _dot_nt(q[:, cs], k_ref[0, :, cs])
        p = jnp.exp2(s - jnp.max(s, axis=1, keepdims=True))
        l = jnp.sum(p, axis=1, keepdims=True)
        outs.append((_dot(p.astype(BF16), v_ref[0, :, cs]) * (1.0 / l)).astype(BF16))
    o = jnp.concatenate(outs, axis=1)
    y = _dot(o, wo_ref[...])
    o_ref[...] = _layer_norm(alpha * x + y, g_ref[...], b_ref[...])


def _mem_attn(xf, o_nsa, o_mla, w_o, g1, b1, k_mem, v_mem, wq, wo, g, b, alpha, S, tm):
    T, D = xf.shape
    nst = S // tm
    M = k_mem.shape[1]
    tok = lambda i: (i, 0)
    const = lambda i: (0, 0)
    memb = lambda i: (i // nst, 0, 0)
    return pl.pallas_call(
        functools.partial(_mem_attn_body, alpha=alpha),
        grid=(T // tm,),
        in_specs=[
            pl.BlockSpec((tm, D), tok),
            pl.BlockSpec((tm, o_nsa.shape[1]), tok),
            pl.BlockSpec((tm, o_mla.shape[1]), tok),
            pl.BlockSpec(w_o.shape, const),
            pl.BlockSpec((1, D), const),
            pl.BlockSpec((1, D), const),
            pl.BlockSpec((1, M, D), memb),
            pl.BlockSpec((1, M, D), memb),
            pl.BlockSpec(wq.shape, const),
            pl.BlockSpec(wo.shape, const),
            pl.BlockSpec((1, D), const),
            pl.BlockSpec((1, D), const),
        ],
        out_specs=pl.BlockSpec((tm, D), tok),
        out_shape=jax.ShapeDtypeStruct((T, D), F32),
        compiler_params=_params("arbitrary"),
        name="mem_attn",
    )(xf, o_nsa, o_mla, w_o, g1, b1, k_mem, v_mem, wq, wo, g, b)


HALO = 8


FFN_SLAB = 256
FFN_TM = 512


def _ffn_body(x_ref, xh_ref, wg_ref, wu_ref, cw_ref, cb_ref, wd_ref, g_ref, b_ref, o_ref,
              act_ref, *, alpha, seq_tiles):
    i = pl.program_id(0)
    x = x_ref[...]
    xb = x.astype(BF16)
    xhb = xh_ref[...].astype(BF16)
    tm = x.shape[0]
    row = lax.broadcasted_iota(jnp.int32, (tm, 1), 0)
    seq_start = i % seq_tiles == 0
    for c0 in range(0, wg_ref.shape[1], FFN_SLAB):
        cs = slice(c0, c0 + FFN_SLAB)
        gate = _dot(xb, wg_ref[:, cs])
        up = _dot(xb, wu_ref[:, cs])
        halo = jnp.where(seq_start, 0.0, _dot(xhb, wg_ref[:, cs]))
        g1 = jnp.where(row == 0, halo[HALO - 1:HALO], pltpu.roll(gate, 1, 0))
        g2 = jnp.where(row == 0, halo[HALO - 2:HALO - 1],
                       jnp.where(row == 1, halo[HALO - 1:HALO], pltpu.roll(gate, 2, 0)))
        conv = cw_ref[0:1, cs] * g2 + cw_ref[1:2, cs] * g1 + cw_ref[2:3, cs] * gate + cb_ref[:, cs]
        act_ref[:, cs] = (conv * (1.0 / (1.0 + jnp.exp(-conv))) * up).astype(BF16)
    y = _dot(act_ref[...], wd_ref[...])
    o_ref[...] = _layer_norm(alpha * x + y, g_ref[...], b_ref[...])


def _ffn(xf, wg, wu, cw, cb, wd, g, b, alpha, S, tm):
    T, D = xf.shape
    dff = wg.shape[1]
    assert dff % FFN_SLAB == 0
    tok = lambda i: (i, 0)
    const = lambda i: (0, 0)
    return pl.pallas_call(
        functools.partial(_ffn_body, alpha=alpha, seq_tiles=S // tm),
        grid=(T // tm,),
        in_specs=[
            pl.BlockSpec((tm, D), tok),
            pl.BlockSpec((HALO, D), lambda i: (jnp.maximum(i * (tm // HALO) - 1, 0), 0)),
            pl.BlockSpec((D, dff), const),
            pl.BlockSpec((D, dff), const),
            pl.BlockSpec((CONV_WIDTH, dff), const),
            pl.BlockSpec((1, dff), const),
            pl.BlockSpec((dff, D), const),
            pl.BlockSpec((1, D), const),
            pl.BlockSpec((1, D), const),
        ],
        out_specs=pl.BlockSpec((tm, D), tok),
        out_shape=jax.ShapeDtypeStruct((T, D), F32),
        scratch_shapes=[pltpu.VMEM((tm, dff), BF16)],
        compiler_params=_params("arbitrary"),
        name="ffn",
    )(xf, xf, wg, wu, cw, cb, wd, g, b)


def _inv_freq_row(dim, lane_lo, lane_hi, period):
    inv = ROPE_THETA ** (-np.arange(0, dim, 2, dtype=np.float64) / dim)
    row = np.zeros((1, LANES), np.float32)
    for lane in range(lane_lo, lane_hi):
        row[0, lane] = inv[(lane % period) % (dim // 2)]
    return jnp.asarray(row)


def _cover_t(S):
    nc = S // CMP_STRIDE
    ns = S // SLC_LEN
    cs = np.arange(nc)[:, None] * CMP_STRIDE
    ss = np.arange(ns)[None, :] * SLC_LEN
    cover = np.clip(np.minimum(cs + CMP_LEN, ss + SLC_LEN) - np.maximum(cs, ss), 0, None) / CMP_LEN
    cover[nc - 1:] = 0.0
    return jnp.asarray(cover.T, dtype=BF16)


def _permute_w_in(w):
    D = w.shape[0]
    c1 = NSA_HEADS * NSA_DH
    c2 = c1 + 3 * 2 * NSA_GROUPS * NSA_DH
    c3 = c2 + 3 * NSA_HEADS
    c4 = c3 + MLA_Q_RANK
    c5 = c4 + MLA_KV_RANK
    c6 = c5 + MLA_ROPE
    half = MLA_ROPE // 2
    z = lambda n: jnp.zeros((D, n), w.dtype)
    misc = jnp.concatenate(
        [w[:, c5 + half:c6], z(GATE_LANE0 - half), w[:, c2:c3], z(MLA_PE1 - GATE_LANE0 - (c3 - c2)),
         w[:, c5:c5 + half], z(LANES - MLA_PE1 - half)], axis=1)
    return jnp.concatenate([w[:, :c2], w[:, c3:c5], misc], axis=1).astype(BF16)


def _mla_head_lanes(nope, pe):
    r, H, _ = nope.shape
    half = MLA_ROPE // 2
    split = MLA_PE1 - half
    pad = jnp.zeros((r, H, LANES - MLA_NOPE - MLA_ROPE), nope.dtype)
    return jnp.concatenate([pe[..., half:], nope[..., :split], pe[..., :half], nope[..., split:], pad],
                           axis=2).reshape(r, H * LANES)


def _permute_w_uq(w):
    r = w.shape[0]
    w3 = w.reshape(r, MLA_HEADS, MLA_NOPE + MLA_ROPE)
    return _mla_head_lanes(w3[..., :MLA_NOPE], w3[..., MLA_NOPE:]).astype(BF16)


def _permute_w_ukv(w):
    r = w.shape[0]
    w3 = w.reshape(r, MLA_HEADS, MLA_NOPE + MLA_V)
    k = _mla_head_lanes(w3[..., :MLA_NOPE], jnp.zeros((r, MLA_HEADS, MLA_ROPE), w.dtype))
    v = _pad_lanes(w3[..., MLA_NOPE:]).reshape(r, MLA_HEADS // 2, 2, LANES)
    v = jnp.stack([v[:, :, 0], jnp.roll(v[:, :, 1], LANES - MLA_V, axis=-1)], axis=2)
    return jnp.concatenate([k, v.reshape(r, MLA_HEADS * LANES)], axis=1).astype(BF16)


def _permute_w_o(w):
    n = NSA_HEADS * NSA_DH
    wn = w[:n].reshape(NSA_GROUPS, NSA_REP, NSA_DH, -1).transpose(1, 0, 2, 3).reshape(n, -1)
    return jnp.concatenate([wn, w[n:]], axis=0).astype(BF16)


def _pad_lanes(a):
    return jnp.concatenate([a, jnp.zeros(a.shape[:-1] + (LANES - a.shape[-1],), a.dtype)], axis=-1)


def kernel(x, mem, positions, w_in, nsa_k_pos, nsa_ck_w1, nsa_ck_b1, nsa_ck_w2, nsa_ck_b2,
           nsa_v_pos, nsa_cv_w1, nsa_cv_b1, nsa_cv_w2, nsa_cv_b2,
           mla_q_norm, mla_w_uq, mla_kv_norm, mla_w_ukv, w_o, ln1_g, ln1_b,
           mem_wq, mem_wk, mem_wv, mem_wo, ln2_g, ln2_b,
           ffn_w_up, ffn_conv_w, ffn_conv_b, ffn_w_down, ln3_g, ln3_b):
    B, S, D = x.shape
    T = B * S
    depth = w_in.shape[0]
    alpha = (2.0 * depth) ** 0.25
    d_ff = ffn_w_down.shape[1]
    tm = min(512, S)
    assert S % MLA_TQ == 0 and S >= WIN_SPAN and S % tm == 0
    assert (B * mem.shape[1]) % 256 == 0

    pos = positions.reshape(T, 1)
    pos_cmp = positions[:, CMP_LEN - 1::CMP_STRIDE]
    pos_cmp = jnp.concatenate([pos_cmp, pos_cmp[:, -1:]], axis=1)[:, :, None]
    inv_cmp = _inv_freq_row(NSA_DH, 0, NSA_DH, NSA_DH)
    inv_tok = (_inv_freq_row(NSA_DH, 0, NSA_DH // 2, NSA_DH)
               + _inv_freq_row(MLA_ROPE, NSA_DH // 2, NSA_DH // 2 + MLA_ROPE // 2, MLA_ROPE // 2))
    cov_t = _cover_t(S)
    memf = mem.reshape(B * mem.shape[1], D)

    xf = x.reshape(T, D)
    for l in range(depth):
        qn, kvn, kvc, misc, q_m, k_m, v_m = _inproj(
            pos, inv_tok, xf, _permute_w_in(w_in[l]), mla_q_norm[l][None, :],
            mla_kv_norm[l][None, :], _permute_w_uq(mla_w_uq[l]), _permute_w_ukv(mla_w_ukv[l]),
            B, S, tm)
        kvcmp = _compress(
            pos_cmp, inv_cmp, kvc,
            *_compress_weights(nsa_k_pos[l], nsa_ck_w1[l], nsa_ck_b1[l], nsa_ck_w2[l], nsa_ck_b2[l],
                               nsa_v_pos[l], nsa_cv_w1[l], nsa_cv_b1[l], nsa_cv_w2[l], nsa_cv_b2[l]),
            B, S)
        o_nsa = _nsa(qn, kvn, kvcmp, misc, cov_t, B, S)
        o_mla = _mla(q_m, k_m, v_m, B, S)
        k_mem, v_mem = _mem_kv(memf, mem_wk[l].astype(BF16), mem_wv[l].astype(BF16), 256)
        xf = _mem_attn(xf, o_nsa.reshape(T, -1), o_mla.reshape(T, -1), _permute_w_o(w_o[l]),
                       ln1_g[l][None, :], ln1_b[l][None, :],
                       k_mem.reshape(B, -1, D), v_mem.reshape(B, -1, D),
                       mem_wq[l].astype(BF16), mem_wo[l].astype(BF16),
                       ln2_g[l][None, :], ln2_b[l][None, :], alpha, S, tm)
        xf = _ffn(xf, ffn_w_up[l][:, :d_ff].astype(BF16), ffn_w_up[l][:, d_ff:].astype(BF16),
                  ffn_conv_w[l], ffn_conv_b[l][None, :], ffn_w_down[l].astype(BF16),
                  ln3_g[l][None, :], ln3_b[l][None, :], alpha, S, FFN_TM)
    return xf.reshape(B, S, D)
```

```python
import functools
import math

import numpy as np
import jax
import jax.numpy as jnp
from jax import lax
from jax.experimental import pallas as pl
from jax.experimental.pallas import tpu as pltpu

F32 = jnp.float32
BF16 = jnp.bfloat16

NSA_HEADS = 8
NSA_GROUPS = 2
NSA_REP = NSA_HEADS // NSA_GROUPS
NSA_DH = 64
CMP_STRIDE = 16
CMP_LEN = 32
SLC_LEN = 64
TOPN = 16
WINDOW = 512
CMP_HIDDEN = 128
FORCE_BONUS = 1e4
MLA_HEADS = 8
MLA_Q_RANK = 384
MLA_KV_RANK = 256
MLA_NOPE = 64
MLA_ROPE = 32
MLA_V = 64
MEM_HEADS = 4
CONV_WIDTH = 3
ROPE_THETA = 10000.0
LN_EPS = 1e-5
RMS_EPS = 1e-6
NEG_INF = -1e30
LOG2E = math.log2(math.e)

LANES = 128
VMEM_LIMIT = 60 * 1024 * 1024

C_Q = 0
C_KVC = 512
C_KVN = 768
C_LAT = 1280
C_MISC = 1920
IN_COLS_PAD = 2048
GATE_LANE0 = MLA_ROPE
MLA_PE2 = 0
MLA_PE1 = 64
ONES_LANE = 64
ONES_LANE_G1 = 0


def _dot(a, b):
    return jnp.dot(a, b, preferred_element_type=F32)


def _dot_nt(a, b):
    return lax.dot_general(a, b, (((1,), (1,)), ((), ())), preferred_element_type=F32)


def _layer_norm(y, g, b):
    mu = jnp.mean(y, axis=-1, keepdims=True)
    d = y - mu
    var = jnp.mean(d * d, axis=-1, keepdims=True)
    return d * lax.rsqrt(var + LN_EPS) * g + b


def _params(*sem):
    return pltpu.CompilerParams(dimension_semantics=sem, vmem_limit_bytes=VMEM_LIMIT)


def _rope_tables(pos_col, inv_row, half):
    ang = pos_col * inv_row
    cos = jnp.cos(ang)
    sin = jnp.sin(ang)
    lane = lax.broadcasted_iota(jnp.int32, (1, LANES), 1)
    upper = (lane & (2 * half - 1)) >= half
    rot = inv_row != 0.0
    sin_hi = jnp.where(upper & rot, sin, 0.0)
    sin_lo = jnp.where(upper | (~rot), 0.0, -sin)
    return cos, sin_hi, sin_lo


def _apply_rope(v, tabs, half):
    cos, sin_hi, sin_lo = tabs
    return v * cos + pltpu.roll(v, half, 1) * sin_hi + pltpu.roll(v, LANES - half, 1) * sin_lo


def _rms_norm(v, g):
    return v * lax.rsqrt(jnp.mean(v * v, axis=-1, keepdims=True) + RMS_EPS) * g


def _inproj_body(pos_ref, inv_ref, x_ref, w_ref, gq_ref, gkv_ref, wq_ref, wkv_ref,
                 qn_ref, kvn_ref, kvc_ref, misc_ref, qm_ref, km_ref, vm_ref):
    xb = x_ref[...].astype(BF16)
    lane = lax.broadcasted_iota(jnp.int32, (1, LANES), 1)
    low = lane < NSA_DH
    qscale = NSA_DH ** -0.5 * LOG2E

    ang = pos_ref[...].astype(F32) * inv_ref[...]
    cos, sin = jnp.cos(ang), jnp.sin(ang)
    nf, mf = NSA_DH // 2, MLA_ROPE // 2

    def tile_nsa(t):
        t = jnp.where(lane < nf, t, 0.0)
        t = t + pltpu.roll(t, nf, 1)
        return t + pltpu.roll(t, 2 * nf, 1)

    def place_mla(t):
        t = jnp.where((lane >= nf) & (lane < nf + mf), t, 0.0)
        return pltpu.roll(t, MLA_PE1 - nf, 1) + pltpu.roll(t, LANES + MLA_PE2 - nf, 1)

    upper = (lane & (NSA_DH - 1)) >= nf
    sin_n = tile_nsa(sin)
    tabs = (tile_nsa(cos), jnp.where(upper, sin_n, 0.0), jnp.where(upper, 0.0, -sin_n))
    pe1 = (lane >= MLA_PE1) & (lane < MLA_PE1 + mf)
    pe2 = (lane >= MLA_PE2) & (lane < MLA_PE2 + mf)
    pe_lanes = pe1 | pe2
    sin_m = place_mla(sin)
    cos_m = jnp.where(pe_lanes, place_mla(cos), 1.0)
    sin_m = jnp.where(pe1, -sin_m, jnp.where(pe2, sin_m, 0.0))

    def rope_mla(v):
        return v * cos_m + pltpu.roll(v, LANES // 2, 1) * sin_m

    def proj(c0, n):
        return _dot(xb, w_ref[:, c0:c0 + n])

    def split_store(v, ref, idx_lo, idx_hi, pad=0.0):
        ref[0, idx_lo] = jnp.where(low, v, pad).astype(BF16)
        ref[0, idx_hi] = jnp.where(low, pltpu.roll(v, NSA_DH, 1), pad).astype(BF16)

    ones_pad = jnp.where(lane == ONES_LANE, 1.0, 0.0)
    ones_pad_g1 = jnp.where(lane == ONES_LANE_G1, 1.0, 0.0)

    for slab in range(2):
        h = proj(C_Q + 256 * slab, 256)
        for j in range(2):
            r = _apply_rope(h[:, LANES * j:LANES * (j + 1)], tabs, NSA_DH // 2) * qscale
            split_store(r, qn_ref, 4 * slab + 2 * j, 4 * slab + 2 * j + 1)

    h = proj(C_KVC, 256)
    kvc_ref[0] = h[:, :LANES]
    kvc_ref[1] = h[:, LANES:]

    for slab in range(2):
        h = proj(C_KVN + 256 * slab, 256)
        k = _apply_rope(h[:, :LANES], tabs, NSA_DH // 2)
        split_store(k, kvn_ref, 4 * slab, 4 * slab + 1)
        v = h[:, LANES:]
        kvn_ref[0, 4 * slab + 2] = jnp.where(low, v, ones_pad).astype(BF16)
        kvn_ref[0, 4 * slab + 3] = jnp.where(low, ones_pad_g1, v).astype(BF16)

    lat = [proj(C_LAT + 256 * i, 256) for i in range((IN_COLS_PAD - C_LAT) // 256)]
    misc = lat[2][:, LANES:]
    misc_ref[...] = misc
    mq = jnp.concatenate([lat[0], lat[1][:, :LANES]], axis=1)
    mkv = jnp.concatenate([lat[1][:, LANES:], lat[2][:, :LANES]], axis=1)
    mscale = (MLA_NOPE + MLA_ROPE) ** -0.5 * LOG2E
    qn = _rms_norm(mq, gq_ref[...]).astype(BF16)
    kvn = _rms_norm(mkv, gkv_ref[...]).astype(BF16)
    kpe = jnp.where(pe_lanes, rope_mla(misc), 0.0)
    kcols = MLA_HEADS * LANES
    for slab in range(MLA_HEADS // 2):
        cs = slice(256 * slab, 256 * (slab + 1))
        hq = _dot(qn, wq_ref[:, cs])
        hk = _dot(kvn, wkv_ref[:, cs])
        hv = _dot(kvn, wkv_ref[:, kcols + 256 * slab:kcols + 256 * (slab + 1)])
        for j in range(2):
            ls = slice(LANES * j, LANES * (j + 1))
            qm_ref[0, 2 * slab + j] = (rope_mla(hq[:, ls]) * mscale).astype(BF16)
            km_ref[0, 2 * slab + j] = (hk[:, ls] + kpe).astype(BF16)
            vm_ref[0, 2 * slab + j] = (hv[:, ls] + (ones_pad_g1 if j else ones_pad)).astype(BF16)


def _inproj(pos, inv_row, xf, w_in_p, gq, gkv, wq_p, wkv_p, B, S, tm):
    T = B * S
    nst = S // tm
    tok = lambda i: (i, 0)
    const = lambda i: (0, 0)
    head_blk = lambda i: (i // nst, 0, i % nst, 0)
    heads = jax.ShapeDtypeStruct((B, 8, S, LANES), BF16)
    return pl.pallas_call(
        _inproj_body,
        grid=(T // tm,),
        in_specs=[
            pl.BlockSpec((tm, 1), tok),
            pl.BlockSpec((1, LANES), const),
            pl.BlockSpec((tm, xf.shape[1]), tok),
            pl.BlockSpec(w_in_p.shape, const),
            pl.BlockSpec(gq.shape, const),
            pl.BlockSpec(gkv.shape, const),
            pl.BlockSpec(wq_p.shape, const),
            pl.BlockSpec(wkv_p.shape, const),
        ],
        out_specs=[
            pl.BlockSpec((1, 8, tm, LANES), head_blk),
            pl.BlockSpec((1, 8, tm, LANES), head_blk),
            pl.BlockSpec((2, tm, LANES), lambda i: (0, i, 0)),
            pl.BlockSpec((tm, LANES), tok),
            pl.BlockSpec((1, 8, tm, LANES), head_blk),
            pl.BlockSpec((1, 8, tm, LANES), head_blk),
            pl.BlockSpec((1, 8, tm, LANES), head_blk),
        ],
        out_shape=[
            heads,
            heads,
            jax.ShapeDtypeStruct((2, T, LANES), F32),
            jax.ShapeDtypeStruct((T, LANES), F32),
            heads,
            heads,
            heads,
        ],
        compiler_params=_params("arbitrary"),
        name="inproj",
    )(pos, inv_row, xf, w_in_p, gq, gkv, wq_p, wkv_p)


def _compress_body(pos_ref, inv_ref, x_ref, pe_ref, w1_ref, b1_ref, w2_ref, b2_ref, o_ref):
    is_k = pl.program_id(0) == 0
    nch = o_ref.shape[2]
    a1 = jnp.zeros((nch, NSA_GROUPS * CMP_HIDDEN), F32)
    a2 = jnp.zeros((nch, NSA_GROUPS * CMP_HIDDEN), F32)
    for l in range(CMP_STRIDE):
        xl = x_ref.at[0, 0][pl.ds(l, nch, stride=CMP_STRIDE), :]
        a1 = a1 + _dot((xl + pe_ref[0, l:l + 1, :]).astype(BF16), w1_ref[0, l])
        a2 = a2 + _dot((xl + pe_ref[0, CMP_STRIDE + l:CMP_STRIDE + l + 1, :]).astype(BF16),
                       w1_ref[0, CMP_STRIDE + l])
    pre = a1 + pltpu.roll(a2, nch - 1, 0) + b1_ref[0]
    hid = jax.nn.gelu(pre, approximate=True)
    out = _dot(hid.astype(BF16), w2_ref[0]) + b2_ref[0]
    tabs = _rope_tables(pos_ref[0].astype(F32), inv_ref[...], NSA_DH // 2)
    row = lax.broadcasted_iota(jnp.int32, (nch, 1), 0)
    for g in range(NSA_GROUPS):
        og = out[:, LANES * g:LANES * (g + 1)]
        og = jnp.where(is_k, _apply_rope(og, tabs, NSA_DH // 2), og)
        o_ref[g, 0] = jnp.where(row < nch - 1, og, 0.0).astype(BF16)


def _compress(pos_cmp, inv_nsa, kvc, pe, w1, b1, w2, b2, B, S):
    nch = S // CMP_STRIDE
    x = kvc.reshape(2, B, S, LANES)
    kv = lambda j, b: (j, 0, 0)
    kv4 = lambda j, b: (j, 0, 0, 0)
    return pl.pallas_call(
        _compress_body,
        grid=(2, B),
        in_specs=[
            pl.BlockSpec((1, nch, 1), lambda j, b: (b, 0, 0)),
            pl.BlockSpec((1, LANES), lambda j, b: (0, 0)),
            pl.BlockSpec((1, 1, S, LANES), lambda j, b: (j, b, 0, 0)),
            pl.BlockSpec((1,) + pe.shape[1:], kv),
            pl.BlockSpec((1,) + w1.shape[1:], kv4),
            pl.BlockSpec((1,) + b1.shape[1:], kv),
            pl.BlockSpec((1,) + w2.shape[1:], kv),
            pl.BlockSpec((1,) + b2.shape[1:], kv),
        ],
        out_specs=pl.BlockSpec((NSA_GROUPS, 1, nch, LANES), lambda j, b: (j, b, 0, 0)),
        out_shape=jax.ShapeDtypeStruct((2 * NSA_GROUPS, B, nch, LANES), BF16),
        compiler_params=_params("arbitrary", "arbitrary"),
        name="compress",
    )(pos_cmp, inv_nsa, x, pe, w1, b1, w2, b2)


def _compress_weights(k_pos, k_w1, k_b1, k_w2, k_b2, v_pos, v_w1, v_b1, v_w2, v_b2):
    def one(pos, w1, b1, w2, b2, g1_high):
        w1l = w1.reshape(CMP_LEN, NSA_DH, CMP_HIDDEN)
        z1 = jnp.zeros_like(w1l)
        w1bd = jnp.concatenate([jnp.concatenate([w1l, z1], axis=2),
                                jnp.concatenate([z1, w1l], axis=2)], axis=1)
        pad = lambda a: _pad_lanes(a)
        pad1 = (lambda a: jnp.roll(_pad_lanes(a), LANES - a.shape[-1], axis=-1)) if g1_high else pad
        z2 = jnp.zeros_like(pad(w2))
        w2bd = jnp.concatenate([jnp.concatenate([pad(w2), z2], axis=1),
                                jnp.concatenate([z2, pad1(w2)], axis=1)], axis=0)
        return (jnp.tile(pos, (1, NSA_GROUPS)), w1bd.astype(BF16), jnp.tile(b1, NSA_GROUPS)[None, :],
                w2bd.astype(BF16), jnp.concatenate([pad(b2), pad1(b2)])[None, :])
    k = one(k_pos, k_w1, k_b1, k_w2, k_b2, False)
    v = one(v_pos, v_w1, v_b1, v_w2, v_b2, True)
    return tuple(jnp.stack([a, b]) for a, b in zip(k, v))


STRIP = 64


def _lane_tile(col, n):
    reps = [col] * (n // LANES)
    if n % LANES:
        reps.append(col[:, :n % LANES])
    return reps[0] if len(reps) == 1 else jnp.concatenate(reps, axis=1)


def _flash_reset(m_ref, acc_ref):
    m_ref[...] = jnp.full(m_ref.shape, NEG_INF, F32)
    acc_ref[...] = jnp.zeros(acc_ref.shape, F32)


def _flash_update(s_ref, v, m_ref, acc_ref, p_ref, mask=None):
    rows, n = s_ref.shape
    for r in range(rows // STRIP):
        rs = slice(STRIP * r, STRIP * (r + 1))
        s = s_ref[rs, :]
        if mask is not None:
            s = jnp.where(mask[rs], s, NEG_INF)
        if m_ref is None:
            p_ref[rs, :] = jnp.exp2(s - jnp.max(s, axis=1, keepdims=True)).astype(BF16)
            continue
        m_old = m_ref[rs, :]
        m_new = jnp.maximum(m_old, jnp.max(s, axis=1, keepdims=True))
        p_ref[rs, :] = jnp.exp2(s - _lane_tile(m_new, n)).astype(BF16)
        acc_ref[rs, :] = jnp.exp2(m_old - m_new) * acc_ref[rs, :]
        m_ref[rs, :] = m_new
    if m_ref is None:
        acc_ref[...] = _dot(p_ref[...], v)
    else:
        acc_ref[...] += _dot(p_ref[...], v)


NSA_TQ = 256
NSA_ROWS = NSA_REP * NSA_TQ
SLC_CHUNK = 512
WIN_SPAN = WINDOW + NSA_TQ
BIAS_LANE0 = LANES


def _nsa_body(q_ref, kvn_ref, kvc_ref, misc_ref, cov_ref, o_ref,
              kaug_ref, score_ref, qbias_ref, sa_ref, sb_ref, pa_ref, pb_ref, ms_ref, accs_ref,
              sw_ref, pw_ref, accw_ref, sc_ref, pn_ref, pc_ref, oc_ref):
    c = pl.program_id(1)
    rows = NSA_ROWS
    t_row = c * NSA_TQ + (lax.broadcasted_iota(jnp.int32, (rows, 1), 0) & (NSA_TQ - 1))
    ncmp = kvc_ref.shape[2]
    nblk = kaug_ref.shape[1] // SLC_LEN

    @pl.when(c == 0)
    def _():
        nkeys = kaug_ref.shape[1]
        key_blk = lax.broadcasted_iota(jnp.int32, (nkeys, LANES), 0) // SLC_LEN
        onehot = jnp.where(key_blk == lax.broadcasted_iota(jnp.int32, (nkeys, LANES), 1), 1.0, 0.0)
        for g in range(NSA_GROUPS):
            kaug_ref[g, :, :BIAS_LANE0] = kvn_ref[0, g]
            kaug_ref[g, :, BIAS_LANE0:] = onehot.astype(BF16)

    qs = [q_ref[0, NSA_REP * g:NSA_REP * (g + 1)].reshape(rows, LANES) for g in range(NSA_GROUPS)]

    cmp_valid = (CMP_STRIDE * lax.broadcasted_iota(jnp.int32, (1, ncmp), 1) + CMP_LEN - 1) <= t_row
    for g in range(NSA_GROUPS):
        sc_ref[g] = _dot_nt(qs[g], kvc_ref[g, 0])
    win_start = pl.multiple_of(jnp.maximum(c * NSA_TQ - WINDOW, 0), NSA_TQ)

    def window_scores(g):
        sw_ref[g] = _dot_nt(qs[g], kvn_ref[0, 4 + g, pl.ds(win_start, WIN_SPAN), :])

    window_scores(0)
    for g in range(NSA_GROUPS):
        for r in range(rows // STRIP):
            rs = slice(STRIP * r, STRIP * (r + 1))
            s = jnp.where(cmp_valid[rs], sc_ref[g, rs, :], NEG_INF)
            p = jnp.where(cmp_valid[rs], jnp.exp2(s - jnp.max(s, axis=1, keepdims=True)), 0.0)
            l = jnp.sum(p, axis=1, keepdims=True)
            p = p * jnp.where(l > 0.0, 1.0 / l, 0.0)
            pn_ref[g, rs, :] = p
            pc_ref[g, rs, :] = p.astype(BF16)
        oc_ref[g] = _dot(pc_ref[g], kvc_ref[NSA_GROUPS + g, 0])

    def select(nb):
        window_scores(1)
        width = NSA_GROUPS * NSA_TQ
        ps = jnp.concatenate(
            [sum(pn_ref[g, NSA_TQ * r:NSA_TQ * (r + 1), :] for r in range(NSA_REP))
             for g in range(NSA_GROUPS)], axis=0)
        hi = ps.astype(BF16)
        lo = (ps - hi.astype(F32)).astype(BF16)
        imp = _dot_nt(cov_ref[:nb, :], hi) + _dot_nt(cov_ref[:nb, :], lo)
        jidx = lax.broadcasted_iota(jnp.int32, (nb, width), 0)
        lane_q = lax.broadcasted_iota(jnp.int32, (1, width), 1) & (NSA_TQ - 1)
        cur = c * (NSA_TQ // SLC_LEN) + lane_q // SLC_LEN
        forced = (jidx == 0) | (jidx == cur) | (jidx == cur - 1)
        score = jnp.where(jidx <= cur, jnp.where(forced, FORCE_BONUS, imp), NEG_INF)
        score_ref[:nb, :] = score
        sub = 8
        cnt = [jnp.zeros((sub, width), F32) for _ in range(nb // sub)]
        tiles = [score[sub * v:sub * (v + 1)] for v in range(nb // sub)]
        sidx = lax.broadcasted_iota(jnp.int32, (sub, width), 0)
        for jp in range(nb):
            rowv = jnp.broadcast_to(score_ref[jp:jp + 1, :], (sub, width))
            for v in range(nb // sub):
                if sub * v > jp:
                    cnt[v] = jnp.where(rowv >= tiles[v], cnt[v] + 1.0, cnt[v])
                elif sub * v + sub - 1 <= jp:
                    cnt[v] = jnp.where(rowv > tiles[v], cnt[v] + 1.0, cnt[v])
                else:
                    ge = jnp.where(rowv >= tiles[v], cnt[v] + 1.0, cnt[v])
                    gt = jnp.where(rowv > tiles[v], cnt[v] + 1.0, cnt[v])
                    cnt[v] = jnp.where(sidx + sub * v > jp, ge, gt)
        rank = jnp.concatenate(cnt, axis=0)
        bias = jnp.where(rank < float(TOPN), 0.0, NEG_INF)
        bias = jnp.concatenate([bias, jnp.zeros((LANES - nb, width), F32)], axis=0)
        qbias_ref[...] = bias.T.astype(BF16)

    visible = (c + 1) * (NSA_TQ // SLC_LEN)

    @pl.when(visible <= TOPN)
    def _():
        window_scores(1)
        qbias_ref[...] = jnp.zeros(qbias_ref.shape, BF16)

    bounds = [TOPN] + [nb for nb in (nblk // 2, 3 * nblk // 4) if TOPN < nb < nblk] + [nblk]
    for lo_nb, nb in zip(bounds[:-1], bounds[1:]):
        pl.when((visible > lo_nb) & (visible <= nb))(functools.partial(select, nb))

    _flash_reset(ms_ref, accs_ref)
    qas = [jnp.concatenate(
        [qs[g], jnp.concatenate([qbias_ref[NSA_TQ * g:NSA_TQ * (g + 1), :]] * NSA_REP, axis=0)],
        axis=1) for g in range(NSA_GROUPS)]
    last = c // (SLC_CHUNK // NSA_TQ)

    def slc_scores(kc, dst):
        k0 = pl.multiple_of(kc * SLC_CHUNK, SLC_CHUNK)
        for g in range(NSA_GROUPS):
            dst[g] = _dot_nt(qas[g], kaug_ref[g, pl.ds(k0, SLC_CHUNK), :])

    def slc_update(src, p_ref, kc, causal):
        k0 = pl.multiple_of(kc * SLC_CHUNK, SLC_CHUNK)
        mask = None
        if causal:
            mask = (k0 + lax.broadcasted_iota(jnp.int32, (1, SLC_CHUNK), 1)) <= t_row
        for g in range(NSA_GROUPS):
            _flash_update(src.at[g], kvn_ref[0, 2 + g, pl.ds(k0, SLC_CHUNK), :],
                          ms_ref.at[g], accs_ref.at[g], p_ref.at[g], mask)

    slc_scores(0, sa_ref)

    diff = t_row - (win_start + lax.broadcasted_iota(jnp.int32, (1, WIN_SPAN), 1))
    win_valid = (diff >= 0) & (diff < WINDOW)
    for g in range(NSA_GROUPS):
        _flash_update(sw_ref.at[g], kvn_ref[0, 6 + g, pl.ds(win_start, WIN_SPAN), :],
                      None, accw_ref.at[g], pw_ref.at[g], win_valid)

    def slc_pair(i, carry):
        slc_scores(2 * i + 1, sb_ref)
        slc_update(sa_ref, pa_ref, 2 * i, False)
        slc_scores(2 * i + 2, sa_ref)
        slc_update(sb_ref, pb_ref, 2 * i + 1, False)
        return carry
    lax.fori_loop(0, last // 2, slc_pair, 0)
    tail = 2 * (last // 2)

    @pl.when(last > tail)
    def _():
        slc_scores(tail + 1, sb_ref)
        slc_update(sa_ref, pa_ref, tail, False)
        slc_update(sb_ref, pb_ref, tail + 1, True)

    @pl.when(last == tail)
    def _():
        slc_update(sa_ref, pa_ref, tail, True)

    sig = 1.0 / (1.0 + jnp.exp(-misc_ref[...]))
    low = lax.broadcasted_iota(jnp.int32, (NSA_TQ, LANES), 1) < NSA_DH
    denom_lane = jnp.where(low, ONES_LANE, ONES_LANE_G1)

    def pair(ref, rs):
        a0, a1 = ref[0, rs, :], ref[1, rs, :]
        return jnp.where(low, a0, a1), jnp.where(low, a1, a0)

    outs = []
    for r in range(NSA_REP):
        rs = slice(NSA_TQ * r, NSA_TQ * (r + 1))
        gate = [jnp.take_along_axis(
            sig, jnp.where(low, GATE_LANE0 + 3 * r + br, GATE_LANE0 + 3 * (NSA_REP + r) + br), axis=1)
            for br in range(3)]
        o_cmp, _ = pair(oc_ref, rs)
        o_slc, l_slc = pair(accs_ref, rs)
        o_win, l_win = pair(accw_ref, rs)
        outs.append(gate[0] * o_cmp
                    + (gate[1] / jnp.take_along_axis(l_slc, denom_lane, axis=1)) * o_slc
                    + (gate[2] / jnp.take_along_axis(l_win, denom_lane, axis=1)) * o_win)
    o_ref[0] = jnp.concatenate(outs, axis=1).astype(BF16)


def _nsa(qn, kvn, kvcmp, misc, cov_t, B, S):
    nblk = S // SLC_LEN
    ncmp = S // CMP_STRIDE
    nq = S // NSA_TQ
    rows = NSA_ROWS
    return pl.pallas_call(
        _nsa_body,
        grid=(B, nq),
        in_specs=[
            pl.BlockSpec((1, NSA_HEADS, NSA_TQ, LANES), lambda b, c: (b, 0, c, 0)),
            pl.BlockSpec((1, 8, S, LANES), lambda b, c: (b, 0, 0, 0)),
            pl.BlockSpec((4, 1, ncmp, LANES), lambda b, c: (0, b, 0, 0)),
            pl.BlockSpec((NSA_TQ, LANES), lambda b, c: (b * nq + c, 0)),
            pl.BlockSpec(cov_t.shape, lambda b, c: (0, 0)),
        ],
        out_specs=pl.BlockSpec((1, NSA_TQ, NSA_HEADS * NSA_DH), lambda b, c: (b, c, 0)),
        out_shape=jax.ShapeDtypeStruct((B, S, NSA_HEADS * NSA_DH), BF16),
        scratch_shapes=[
            pltpu.VMEM((NSA_GROUPS, S, 2 * LANES), BF16),
            pltpu.VMEM((nblk, NSA_GROUPS * NSA_TQ), F32),
            pltpu.VMEM((NSA_GROUPS * NSA_TQ, LANES), BF16),
            pltpu.VMEM((NSA_GROUPS, rows, SLC_CHUNK), F32),
            pltpu.VMEM((NSA_GROUPS, rows, SLC_CHUNK), F32),
            pltpu.VMEM((NSA_GROUPS, rows, SLC_CHUNK), BF16),
            pltpu.VMEM((NSA_GROUPS, rows, SLC_CHUNK), BF16),
            pltpu.VMEM((NSA_GROUPS, rows, LANES), F32),
            pltpu.VMEM((NSA_GROUPS, rows, LANES), F32),
            pltpu.VMEM((NSA_GROUPS, rows, WIN_SPAN), F32),
            pltpu.VMEM((NSA_GROUPS, rows, WIN_SPAN), BF16),
            pltpu.VMEM((NSA_GROUPS, rows, LANES), F32),
            pltpu.VMEM((NSA_GROUPS, rows, ncmp), F32),
            pltpu.VMEM((NSA_GROUPS, rows, ncmp), F32),
            pltpu.VMEM((NSA_GROUPS, rows, ncmp), BF16),
            pltpu.VMEM((NSA_GROUPS, rows, LANES), F32),
        ],
        compiler_params=_params("arbitrary", "arbitrary"),
        name="nsa",
    )(qn, kvn, kvcmp, misc, cov_t)


MLA_TQ = 512
MLA_CHUNK = 512
MLA_HPB = 4


def _mla_body(q_ref, k_ref, v_ref, o_ref, sa_ref, sb_ref, pa_ref, pb_ref, m_ref, acc_ref):
    qi = pl.program_id(2)
    t_row = qi * MLA_TQ + lax.broadcasted_iota(jnp.int32, (MLA_TQ, 1), 0)
    _flash_reset(m_ref, acc_ref)

    def scores(kc, dst):
        k0 = pl.multiple_of(kc * MLA_CHUNK, MLA_CHUNK)
        for j in range(MLA_HPB):
            dst[j] = _dot_nt(q_ref[0, j], k_ref[0, j, pl.ds(k0, MLA_CHUNK), :])

    def update(src, p_ref, kc, causal):
        k0 = pl.multiple_of(kc * MLA_CHUNK, MLA_CHUNK)
        mask = None
        if causal:
            mask = (k0 + lax.broadcasted_iota(jnp.int32, (1, MLA_CHUNK), 1)) <= t_row
        for j in range(MLA_HPB):
            _flash_update(src.at[j], v_ref[0, j, pl.ds(k0, MLA_CHUNK), :],
                          m_ref.at[j], acc_ref.at[j], p_ref.at[j], mask)

    scores(0, sa_ref)

    def pair(i, carry):
        scores(2 * i + 1, sb_ref)
        update(sa_ref, pa_ref, 2 * i, False)
        scores(2 * i + 2, sa_ref)
        update(sb_ref, pb_ref, 2 * i + 1, False)
        return carry
    lax.fori_loop(0, qi // 2, pair, 0)
    tail = 2 * (qi // 2)

    @pl.when(qi > tail)
    def _():
        scores(tail + 1, sb_ref)
        update(sa_ref, pa_ref, tail, False)
        update(sb_ref, pb_ref, tail + 1, True)

    @pl.when(qi == tail)
    def _():
        update(sa_ref, pa_ref, tail, True)

    low = lax.broadcasted_iota(jnp.int32, (MLA_TQ, LANES), 1) < MLA_V
    denom_lane = jnp.where(low, ONES_LANE, ONES_LANE_G1)
    outs = []
    for j in range(0, MLA_HPB, 2):
        a0, a1 = acc_ref[j], acc_ref[j + 1]
        denom = jnp.take_along_axis(jnp.where(low, a1, a0), denom_lane, axis=1)
        outs.append(jnp.where(low, a0, a1) * (1.0 / denom))
    o_ref[0] = jnp.concatenate(outs, axis=1).astype(BF16)


def _mla(q, k, v, B, S):
    return pl.pallas_call(
        _mla_body,
        grid=(B, MLA_HEADS // MLA_HPB, S // MLA_TQ),
        in_specs=[
            pl.BlockSpec((1, MLA_HPB, MLA_TQ, LANES), lambda b, h, i: (b, h, i, 0)),
            pl.BlockSpec((1, MLA_HPB, S, LANES), lambda b, h, i: (b, h, 0, 0)),
            pl.BlockSpec((1, MLA_HPB, S, LANES), lambda b, h, i: (b, h, 0, 0)),
        ],
        out_specs=pl.BlockSpec((1, MLA_TQ, MLA_HPB * MLA_V), lambda b, h, i: (b, i, h)),
        out_shape=jax.ShapeDtypeStruct((B, S, MLA_HEADS * MLA_V), BF16),
        scratch_shapes=[
            pltpu.VMEM((MLA_HPB, MLA_TQ, MLA_CHUNK), F32),
            pltpu.VMEM((MLA_HPB, MLA_TQ, MLA_CHUNK), F32),
            pltpu.VMEM((MLA_HPB, MLA_TQ, MLA_CHUNK), BF16),
            pltpu.VMEM((MLA_HPB, MLA_TQ, MLA_CHUNK), BF16),
            pltpu.VMEM((MLA_HPB, MLA_TQ, LANES), F32),
            pltpu.VMEM((MLA_HPB, MLA_TQ, LANES), F32),
        ],
        compiler_params=_params("arbitrary", "arbitrary", "arbitrary"),
        name="mla",
    )(q, k, v)


def _mem_kv_body(m_ref, wk_ref, wv_ref, k_ref, v_ref):
    mb = m_ref[...].astype(BF16)
    k_ref[...] = _dot(mb, wk_ref[...]).astype(BF16)
    v_ref[...] = _dot(mb, wv_ref[...]).astype(BF16)


def _mem_kv(memf, wk, wv, tm):
    R, D = memf.shape
    tok = lambda i: (i, 0)
    const = lambda i: (0, 0)
    return pl.pallas_call(
        _mem_kv_body,
        grid=(R // tm,),
        in_specs=[pl.BlockSpec((tm, D), tok), pl.BlockSpec(wk.shape, const),
                  pl.BlockSpec(wv.shape, const)],
        out_specs=[pl.BlockSpec((tm, D), tok), pl.BlockSpec((tm, D), tok)],
        out_shape=[jax.ShapeDtypeStruct((R, D), BF16), jax.ShapeDtypeStruct((R, D), BF16)],
        compiler_params=_params("arbitrary"),
        name="mem_kv",
    )(memf, wk, wv)


def _mem_attn_body(x0_ref, on_ref, om_ref, wmix_ref, g1_ref, b1_ref,
                   k_ref, v_ref, wq_ref, wo_ref, g_ref, b_ref, o_ref, *, alpha):
    half = on_ref.shape[1]
    mix = _dot(on_ref[...], wmix_ref[:half, :]) + _dot(om_ref[...], wmix_ref[half:, :])
    x = _layer_norm(alpha * x0_ref[...] + mix, g1_ref[...], b1_ref[...])
    D = x.shape[1]
    dh = D // MEM_HEADS
    q = (_dot(x.astype(BF16), wq_ref[...]) * (dh ** -0.5 * LOG2E)).astype(BF16)
    outs = []
    for h in range(MEM_HEADS):
        cs = slice(dh * h, dh * (h + 1))
        s = _dot_nt(q[:, cs], k_ref[0, :, cs])
        p = jnp.exp2(s - jnp.max(s, axis=1, keepdims=True))
        l = jnp.sum(p, axis=1, keepdims=True)
        outs.append((_dot(p.astype(BF16), v_ref[0, :, cs]) * (1.0 / l)).astype(BF16))
    o = jnp.concatenate(outs, axis=1)
    y = _dot(o, wo_ref[...])
    o_ref[...] = _layer_norm(alpha * x + y, g_ref[...], b_ref[...])


def _mem_attn(xf, o_nsa, o_mla, w_o, g1, b1, k_mem, v_mem, wq, wo, g, b, alpha, S, tm):
    T, D = xf.shape
    nst = S // tm
    M = k_mem.shape[1]
    tok = lambda i: (i, 0)
    const = lambda i: (0, 0)
    memb = lambda i: (i // nst, 0, 0)
    return pl.pallas_call(
        functools.partial(_mem_attn_body, alpha=alpha),
        grid=(T // tm,),
        in_specs=[
            pl.BlockSpec((tm, D), tok),
            pl.BlockSpec((tm, o_nsa.shape[1]), tok),
            pl.BlockSpec((tm, o_mla.shape[1]), tok),
            pl.BlockSpec(w_o.shape, const),
            pl.BlockSpec((1, D), const),
            pl.BlockSpec((1, D), const),
            pl.BlockSpec((1, M, D), memb),
            pl.BlockSpec((1, M, D), memb),
            pl.BlockSpec(wq.shape, const),
            pl.BlockSpec(wo.shape, const),
            pl.BlockSpec((1, D), const),
            pl.BlockSpec((1, D), const),
        ],
        out_specs=pl.BlockSpec((tm, D), tok),
        out_shape=jax.ShapeDtypeStruct((T, D), F32),
        compiler_params=_params("arbitrary"),
        name="mem_attn",
    )(xf, o_nsa, o_mla, w_o, g1, b1, k_mem, v_mem, wq, wo, g, b)


HALO = 16


FFN_SLAB = 256
FFN_TM = 512


def _ffn_body(x_ref, xh_ref, wg_ref, wu_ref, cw_ref, cb_ref, wd_ref, g_ref, b_ref, o_ref,
              act_ref, *, alpha, seq_tiles):
    i = pl.program_id(0)
    x = x_ref[...]
    xb = x.astype(BF16)
    xe = jnp.concatenate([xh_ref[...].astype(BF16), xb], axis=0)
    tm = x.shape[0]
    row = lax.broadcasted_iota(jnp.int32, (tm, 1), 0)
    seq_start = i % seq_tiles == 0
    for c0 in range(0, wg_ref.shape[1], FFN_SLAB):
        cs = slice(c0, c0 + FFN_SLAB)
        gate_e = _dot(xe, wg_ref[:, cs])
        gate = gate_e[HALO:]
        up = _dot(xb, wu_ref[:, cs])
        halo = jnp.where(seq_start, 0.0, gate_e[:HALO])
        g1 = jnp.where(row == 0, halo[HALO - 1:HALO], pltpu.roll(gate, 1, 0))
        g2 = jnp.where(row == 0, halo[HALO - 2:HALO - 1],
                       jnp.where(row == 1, halo[HALO - 1:HALO], pltpu.roll(gate, 2, 0)))
        conv = cw_ref[0:1, cs] * g2 + cw_ref[1:2, cs] * g1 + cw_ref[2:3, cs] * gate + cb_ref[:, cs]
        act_ref[:, cs] = (conv * (1.0 / (1.0 + jnp.exp(-conv))) * up).astype(BF16)
    y = _dot(act_ref[...], wd_ref[...])
    o_ref[...] = _layer_norm(alpha * x + y, g_ref[...], b_ref[...])


def _ffn(xf, wg, wu, cw, cb, wd, g, b, alpha, S, tm):
    T, D = xf.shape
    dff = wg.shape[1]
    assert dff % FFN_SLAB == 0
    tok = lambda i: (i, 0)
    const = lambda i: (0, 0)
    return pl.pallas_call(
        functools.partial(_ffn_body, alpha=alpha, seq_tiles=S // tm),
        grid=(T // tm,),
        in_specs=[
            pl.BlockSpec((tm, D), tok),
            pl.BlockSpec((HALO, D), lambda i: (jnp.maximum(i * (tm // HALO) - 1, 0), 0)),
            pl.BlockSpec((D, dff), const),
            pl.BlockSpec((D, dff), const),
            pl.BlockSpec((CONV_WIDTH, dff), const),
            pl.BlockSpec((1, dff), const),
            pl.BlockSpec((dff, D), const),
            pl.BlockSpec((1, D), const),
            pl.BlockSpec((1, D), const),
        ],
        out_specs=pl.BlockSpec((tm, D), tok),
        out_shape=jax.ShapeDtypeStruct((T, D), F32),
        scratch_shapes=[pltpu.VMEM((tm, dff), BF16)],
        compiler_params=_params("arbitrary"),
        name="ffn",
    )(xf, xf, wg, wu, cw, cb, wd, g, b)


def _inv_freq_row(dim, lane_lo, lane_hi, period):
    inv = ROPE_THETA ** (-np.arange(0, dim, 2, dtype=np.float64) / dim)
    row = np.zeros((1, LANES), np.float32)
    for lane in range(lane_lo, lane_hi):
        row[0, lane] = inv[(lane % period) % (dim // 2)]
    return jnp.asarray(row)


def _cover_t(S):
    nc = S // CMP_STRIDE
    ns = S // SLC_LEN
    cs = np.arange(nc)[:, None] * CMP_STRIDE
    ss = np.arange(ns)[None, :] * SLC_LEN
    cover = np.clip(np.minimum(cs + CMP_LEN, ss + SLC_LEN) - np.maximum(cs, ss), 0, None) / CMP_LEN
    cover[nc - 1:] = 0.0
    return jnp.asarray(cover.T, dtype=BF16)


def _permute_w_in(w):
    D = w.shape[0]
    c1 = NSA_HEADS * NSA_DH
    c2 = c1 + 3 * 2 * NSA_GROUPS * NSA_DH
    c3 = c2 + 3 * NSA_HEADS
    c4 = c3 + MLA_Q_RANK
    c5 = c4 + MLA_KV_RANK
    c6 = c5 + MLA_ROPE
    half = MLA_ROPE // 2
    z = lambda n: jnp.zeros((D, n), w.dtype)
    misc = jnp.concatenate(
        [w[:, c5 + half:c6], z(GATE_LANE0 - half), w[:, c2:c3], z(MLA_PE1 - GATE_LANE0 - (c3 - c2)),
         w[:, c5:c5 + half], z(LANES - MLA_PE1 - half)], axis=1)
    return jnp.concatenate([w[:, :c2], w[:, c3:c5], misc], axis=1).astype(BF16)


def _mla_head_lanes(nope, pe):
    r, H, _ = nope.shape
    half = MLA_ROPE // 2
    split = MLA_PE1 - half
    pad = jnp.zeros((r, H, LANES - MLA_NOPE - MLA_ROPE), nope.dtype)
    return jnp.concatenate([pe[..., half:], nope[..., :split], pe[..., :half], nope[..., split:], pad],
                           axis=2).reshape(r, H * LANES)


def _permute_w_uq(w):
    r = w.shape[0]
    w3 = w.reshape(r, MLA_HEADS, MLA_NOPE + MLA_ROPE)
    return _mla_head_lanes(w3[..., :MLA_NOPE], w3[..., MLA_NOPE:]).astype(BF16)


def _permute_w_ukv(w):
    r = w.shape[0]
    w3 = w.reshape(r, MLA_HEADS, MLA_NOPE + MLA_V)
    k = _mla_head_lanes(w3[..., :MLA_NOPE], jnp.zeros((r, MLA_HEADS, MLA_ROPE), w.dtype))
    v = _pad_lanes(w3[..., MLA_NOPE:]).reshape(r, MLA_HEADS // 2, 2, LANES)
    v = jnp.stack([v[:, :, 0], jnp.roll(v[:, :, 1], LANES - MLA_V, axis=-1)], axis=2)
    return jnp.concatenate([k, v.reshape(r, MLA_HEADS * LANES)], axis=1).astype(BF16)


def _permute_w_o(w):
    n = NSA_HEADS * NSA_DH
    wn = w[:n].reshape(NSA_GROUPS, NSA_REP, NSA_DH, -1).transpose(1, 0, 2, 3).reshape(n, -1)
    return jnp.concatenate([wn, w[n:]], axis=0).astype(BF16)


def _pad_lanes(a):
    return jnp.concatenate([a, jnp.zeros(a.shape[:-1] + (LANES - a.shape[-1],), a.dtype)], axis=-1)


def kernel(x, mem, positions, w_in, nsa_k_pos, nsa_ck_w1, nsa_ck_b1, nsa_ck_w2, nsa_ck_b2,
           nsa_v_pos, nsa_cv_w1, nsa_cv_b1, nsa_cv_w2, nsa_cv_b2,
           mla_q_norm, mla_w_uq, mla_kv_norm, mla_w_ukv, w_o, ln1_g, ln1_b,
           mem_wq, mem_wk, mem_wv, mem_wo, ln2_g, ln2_b,
           ffn_w_up, ffn_conv_w, ffn_conv_b, ffn_w_down, ln3_g, ln3_b):
    B, S, D = x.shape
    T = B * S
    depth = w_in.shape[0]
    alpha = (2.0 * depth) ** 0.25
    d_ff = ffn_w_down.shape[1]
    tm = min(512, S)
    assert S % MLA_TQ == 0 and S >= WIN_SPAN and S % tm == 0
    assert (B * mem.shape[1]) % 256 == 0

    pos = positions.reshape(T, 1)
    pos_cmp = positions[:, CMP_LEN - 1::CMP_STRIDE]
    pos_cmp = jnp.concatenate([pos_cmp, pos_cmp[:, -1:]], axis=1)[:, :, None]
    inv_cmp = _inv_freq_row(NSA_DH, 0, NSA_DH, NSA_DH)
    inv_tok = (_inv_freq_row(NSA_DH, 0, NSA_DH // 2, NSA_DH)
               + _inv_freq_row(MLA_ROPE, NSA_DH // 2, NSA_DH // 2 + MLA_ROPE // 2, MLA_ROPE // 2))
    cov_t = _cover_t(S)
    memf = mem.reshape(B * mem.shape[1], D)

    xf = x.reshape(T, D)
    for l in range(depth):
        qn, kvn, kvc, misc, q_m, k_m, v_m = _inproj(
            pos, inv_tok, xf, _permute_w_in(w_in[l]), mla_q_norm[l][None, :],
            mla_kv_norm[l][None, :], _permute_w_uq(mla_w_uq[l]), _permute_w_ukv(mla_w_ukv[l]),
            B, S, tm)
        kvcmp = _compress(
            pos_cmp, inv_cmp, kvc,
            *_compress_weights(nsa_k_pos[l], nsa_ck_w1[l], nsa_ck_b1[l], nsa_ck_w2[l], nsa_ck_b2[l],
                               nsa_v_pos[l], nsa_cv_w1[l], nsa_cv_b1[l], nsa_cv_w2[l], nsa_cv_b2[l]),
            B, S)
        o_nsa = _nsa(qn, kvn, kvcmp, misc, cov_t, B, S)
        o_mla = _mla(q_m, k_m, v_m, B, S)
        k_mem, v_mem = _mem_kv(memf, mem_wk[l].astype(BF16), mem_wv[l].astype(BF16), 256)
        xf = _mem_attn(xf, o_nsa.reshape(T, -1), o_mla.reshape(T, -1), _permute_w_o(w_o[l]),
                       ln1_g[l][None, :], ln1_b[l][None, :],
                       k_mem.reshape(B, -1, D), v_mem.reshape(B, -1, D),
                       mem_wq[l].astype(BF16), mem_wo[l].astype(BF16),
                       ln2_g[l][None, :], ln2_b[l][None, :], alpha, S, tm)
        xf = _ffn(xf, ffn_w_up[l][:, :d_ff].astype(BF16), ffn_w_up[l][:, d_ff:].astype(BF16),
                  ffn_conv_w[l], ffn_conv_b[l][None, :], ffn_w_down[l].astype(BF16),
                  ln3_g[l][None, :], ln3_b[l][None, :], alpha, S, FFN_TM)
    return xf.reshape(B, S, D)
```

```python
import functools
import math

import numpy as np
import jax
import jax.numpy as jnp
from jax import lax
from jax.experimental import pallas as pl
from jax.experimental.pallas import tpu as pltpu

F32 = jnp.float32
BF16 = jnp.bfloat16

NSA_HEADS = 8
NSA_GROUPS = 2
NSA_REP = NSA_HEADS // NSA_GROUPS
NSA_DH = 64
CMP_STRIDE = 16
CMP_LEN = 32
SLC_LEN = 64
TOPN = 16
WINDOW = 512
CMP_HIDDEN = 128
FORCE_BONUS = 1e4
MLA_HEADS = 8
MLA_Q_RANK = 384
MLA_KV_RANK = 256
MLA_NOPE = 64
MLA_ROPE = 32
MLA_V = 64
MEM_HEADS = 4
CONV_WIDTH = 3
ROPE_THETA = 10000.0
LN_EPS = 1e-5
RMS_EPS = 1e-6
NEG_INF = -1e30
LOG2E = math.log2(math.e)

LANES = 128
VMEM_LIMIT = 60 * 1024 * 1024

C_Q = 0
C_KVC = 512
C_KVN = 768
C_LAT = 1280
C_MISC = 1920
IN_COLS_PAD = 2048
GATE_LANE0 = MLA_ROPE
MLA_PE2 = 0
MLA_PE1 = 64
ONES_LANE = 64
ONES_LANE_G1 = 0


def _dot(a, b):
    return jnp.dot(a, b, preferred_element_type=F32)


def _dot_nt(a, b):
    return lax.dot_general(a, b, (((1,), (1,)), ((), ())), preferred_element_type=F32)


def _layer_norm(y, g, b):
    mu = jnp.mean(y, axis=-1, keepdims=True)
    d = y - mu
    var = jnp.mean(d * d, axis=-1, keepdims=True)
    return d * lax.rsqrt(var + LN_EPS) * g + b


def _params(*sem):
    return pltpu.CompilerParams(dimension_semantics=sem, vmem_limit_bytes=VMEM_LIMIT)


def _rope_tables(pos_col, inv_row, half):
    ang = pos_col * inv_row
    cos = jnp.cos(ang)
    sin = jnp.sin(ang)
    lane = lax.broadcasted_iota(jnp.int32, (1, LANES), 1)
    upper = (lane & (2 * half - 1)) >= half
    rot = inv_row != 0.0
    sin_hi = jnp.where(upper & rot, sin, 0.0)
    sin_lo = jnp.where(upper | (~rot), 0.0, -sin)
    return cos, sin_hi, sin_lo


def _apply_rope(v, tabs, half):
    cos, sin_hi, sin_lo = tabs
    return v * cos + pltpu.roll(v, half, 1) * sin_hi + pltpu.roll(v, LANES - half, 1) * sin_lo


def _rms_norm(v, g):
    return v * lax.rsqrt(jnp.mean(v * v, axis=-1, keepdims=True) + RMS_EPS) * g


def _inproj_body(pos_ref, inv_ref, x_ref, w_ref, gq_ref, gkv_ref, wq_ref, wkv_ref,
                 qn_ref, kvn_ref, kvc_ref, misc_ref, qm_ref, km_ref, vm_ref):
    xb = x_ref[...].astype(BF16)
    lane = lax.broadcasted_iota(jnp.int32, (1, LANES), 1)
    low = lane < NSA_DH
    qscale = NSA_DH ** -0.5 * LOG2E

    ang = pos_ref[...].astype(F32) * inv_ref[...]
    cos, sin = jnp.cos(ang), jnp.sin(ang)
    nf, mf = NSA_DH // 2, MLA_ROPE // 2

    def tile_nsa(t):
        t = jnp.where(lane < nf, t, 0.0)
        t = t + pltpu.roll(t, nf, 1)
        return t + pltpu.roll(t, 2 * nf, 1)

    def place_mla(t):
        t = jnp.where((lane >= nf) & (lane < nf + mf), t, 0.0)
        return pltpu.roll(t, MLA_PE1 - nf, 1) + pltpu.roll(t, LANES + MLA_PE2 - nf, 1)

    upper = (lane & (NSA_DH - 1)) >= nf
    sin_n = tile_nsa(sin)
    tabs = (tile_nsa(cos), jnp.where(upper, sin_n, 0.0), jnp.where(upper, 0.0, -sin_n))
    pe1 = (lane >= MLA_PE1) & (lane < MLA_PE1 + mf)
    pe2 = (lane >= MLA_PE2) & (lane < MLA_PE2 + mf)
    pe_lanes = pe1 | pe2
    sin_m = place_mla(sin)
    cos_m = jnp.where(pe_lanes, place_mla(cos), 1.0)
    sin_m = jnp.where(pe1, -sin_m, jnp.where(pe2, sin_m, 0.0))

    def rope_mla(v):
        return v * cos_m + pltpu.roll(v, LANES // 2, 1) * sin_m

    def proj(c0, n):
        return _dot(xb, w_ref[:, c0:c0 + n])

    def split_store(v, ref, idx_lo, idx_hi, pad=0.0):
        ref[0, idx_lo] = jnp.where(low, v, pad).astype(BF16)
        ref[0, idx_hi] = jnp.where(low, pltpu.roll(v, NSA_DH, 1), pad).astype(BF16)

    ones_pad = jnp.where(lane == ONES_LANE, 1.0, 0.0)
    ones_pad_g1 = jnp.where(lane == ONES_LANE_G1, 1.0, 0.0)

    for slab in range(2):
        h = proj(C_Q + 256 * slab, 256)
        for j in range(2):
            r = _apply_rope(h[:, LANES * j:LANES * (j + 1)], tabs, NSA_DH // 2) * qscale
            split_store(r, qn_ref, 4 * slab + 2 * j, 4 * slab + 2 * j + 1)

    h = proj(C_KVC, 256)
    kvc_ref[0] = h[:, :LANES]
    kvc_ref[1] = h[:, LANES:]

    for slab in range(2):
        h = proj(C_KVN + 256 * slab, 256)
        k = _apply_rope(h[:, :LANES], tabs, NSA_DH // 2)
        split_store(k, kvn_ref, 4 * slab, 4 * slab + 1)
        v = h[:, LANES:]
        kvn_ref[0, 4 * slab + 2] = jnp.where(low, v, ones_pad).astype(BF16)
        kvn_ref[0, 4 * slab + 3] = jnp.where(low, ones_pad_g1, v).astype(BF16)

    lat = [proj(C_LAT + 256 * i, 256) for i in range((IN_COLS_PAD - C_LAT) // 256)]
    misc = lat[2][:, LANES:]
    misc_ref[...] = misc
    mq = jnp.concatenate([lat[0], lat[1][:, :LANES]], axis=1)
    mkv = jnp.concatenate([lat[1][:, LANES:], lat[2][:, :LANES]], axis=1)
    mscale = (MLA_NOPE + MLA_ROPE) ** -0.5 * LOG2E
    qn = _rms_norm(mq, gq_ref[...]).astype(BF16)
    kvn = _rms_norm(mkv, gkv_ref[...]).astype(BF16)
    kpe = jnp.where(pe_lanes, rope_mla(misc), 0.0)
    kcols = MLA_HEADS * LANES
    for slab in range(MLA_HEADS // 2):
        cs = slice(256 * slab, 256 * (slab + 1))
        hq = _dot(qn, wq_ref[:, cs])
        hk = _dot(kvn, wkv_ref[:, cs])
        hv = _dot(kvn, wkv_ref[:, kcols + 256 * slab:kcols + 256 * (slab + 1)])
        for j in range(2):
            ls = slice(LANES * j, LANES * (j + 1))
            qm_ref[0, 2 * slab + j] = (rope_mla(hq[:, ls]) * mscale).astype(BF16)
            km_ref[0, 2 * slab + j] = (hk[:, ls] + kpe).astype(BF16)
            vm_ref[0, 2 * slab + j] = (hv[:, ls] + (ones_pad_g1 if j else ones_pad)).astype(BF16)


def _inproj(pos, inv_row, xf, w_in_p, gq, gkv, wq_p, wkv_p, B, S, tm):
    T = B * S
    nst = S // tm
    tok = lambda i: (i, 0)
    const = lambda i: (0, 0)
    head_blk = lambda i: (i // nst, 0, i % nst, 0)
    heads = jax.ShapeDtypeStruct((B, 8, S, LANES), BF16)
    return pl.pallas_call(
        _inproj_body,
        grid=(T // tm,),
        in_specs=[
            pl.BlockSpec((tm, 1), tok),
            pl.BlockSpec((1, LANES), const),
            pl.BlockSpec((tm, xf.shape[1]), tok),
            pl.BlockSpec(w_in_p.shape, const),
            pl.BlockSpec(gq.shape, const),
            pl.BlockSpec(gkv.shape, const),
            pl.BlockSpec(wq_p.shape, const),
            pl.BlockSpec(wkv_p.shape, const),
        ],
        out_specs=[
            pl.BlockSpec((1, 8, tm, LANES), head_blk),
            pl.BlockSpec((1, 8, tm, LANES), head_blk),
            pl.BlockSpec((2, tm, LANES), lambda i: (0, i, 0)),
            pl.BlockSpec((tm, LANES), tok),
            pl.BlockSpec((1, 8, tm, LANES), head_blk),
            pl.BlockSpec((1, 8, tm, LANES), head_blk),
            pl.BlockSpec((1, 8, tm, LANES), head_blk),
        ],
        out_shape=[
            heads,
            heads,
            jax.ShapeDtypeStruct((2, T, LANES), F32),
            jax.ShapeDtypeStruct((T, LANES), F32),
            heads,
            heads,
            heads,
        ],
        compiler_params=_params("arbitrary"),
        name="inproj",
    )(pos, inv_row, xf, w_in_p, gq, gkv, wq_p, wkv_p)


def _compress_body(pos_ref, inv_ref, x_ref, pe_ref, w1_ref, b1_ref, w2_ref, b2_ref, o_ref):
    is_k = pl.program_id(0) == 0
    nch = o_ref.shape[2]
    a1 = jnp.zeros((nch, NSA_GROUPS * CMP_HIDDEN), F32)
    a2 = jnp.zeros((nch, NSA_GROUPS * CMP_HIDDEN), F32)
    for l in range(CMP_STRIDE):
        xl = x_ref.at[0, 0][pl.ds(l, nch, stride=CMP_STRIDE), :]
        a1 = a1 + _dot((xl + pe_ref[0, l:l + 1, :]).astype(BF16), w1_ref[0, l])
        a2 = a2 + _dot((xl + pe_ref[0, CMP_STRIDE + l:CMP_STRIDE + l + 1, :]).astype(BF16),
                       w1_ref[0, CMP_STRIDE + l])
    pre = a1 + pltpu.roll(a2, nch - 1, 0) + b1_ref[0]
    hid = jax.nn.gelu(pre, approximate=True)
    out = _dot(hid.astype(BF16), w2_ref[0]) + b2_ref[0]
    tabs = _rope_tables(pos_ref[0].astype(F32), inv_ref[...], NSA_DH // 2)
    row = lax.broadcasted_iota(jnp.int32, (nch, 1), 0)
    for g in range(NSA_GROUPS):
        og = out[:, LANES * g:LANES * (g + 1)]
        og = jnp.where(is_k, _apply_rope(og, tabs, NSA_DH // 2), og)
        o_ref[g, 0] = jnp.where(row < nch - 1, og, 0.0).astype(BF16)


def _compress(pos_cmp, inv_nsa, kvc, pe, w1, b1, w2, b2, B, S):
    nch = S // CMP_STRIDE
    x = kvc.reshape(2, B, S, LANES)
    kv = lambda j, b: (j, 0, 0)
    kv4 = lambda j, b: (j, 0, 0, 0)
    return pl.pallas_call(
        _compress_body,
        grid=(2, B),
        in_specs=[
            pl.BlockSpec((1, nch, 1), lambda j, b: (b, 0, 0)),
            pl.BlockSpec((1, LANES), lambda j, b: (0, 0)),
            pl.BlockSpec((1, 1, S, LANES), lambda j, b: (j, b, 0, 0)),
            pl.BlockSpec((1,) + pe.shape[1:], kv),
            pl.BlockSpec((1,) + w1.shape[1:], kv4),
            pl.BlockSpec((1,) + b1.shape[1:], kv),
            pl.BlockSpec((1,) + w2.shape[1:], kv),
            pl.BlockSpec((1,) + b2.shape[1:], kv),
        ],
        out_specs=pl.BlockSpec((NSA_GROUPS, 1, nch, LANES), lambda j, b: (j, b, 0, 0)),
        out_shape=jax.ShapeDtypeStruct((2 * NSA_GROUPS, B, nch, LANES), BF16),
        compiler_params=_params("arbitrary", "arbitrary"),
        name="compress",
    )(pos_cmp, inv_nsa, x, pe, w1, b1, w2, b2)


def _compress_weights(k_pos, k_w1, k_b1, k_w2, k_b2, v_pos, v_w1, v_b1, v_w2, v_b2):
    def one(pos, w1, b1, w2, b2, g1_high):
        w1l = w1.reshape(CMP_LEN, NSA_DH, CMP_HIDDEN)
        z1 = jnp.zeros_like(w1l)
        w1bd = jnp.concatenate([jnp.concatenate([w1l, z1], axis=2),
                                jnp.concatenate([z1, w1l], axis=2)], axis=1)
        pad = lambda a: _pad_lanes(a)
        pad1 = (lambda a: jnp.roll(_pad_lanes(a), LANES - a.shape[-1], axis=-1)) if g1_high else pad
        z2 = jnp.zeros_like(pad(w2))
        w2bd = jnp.concatenate([jnp.concatenate([pad(w2), z2], axis=1),
                                jnp.concatenate([z2, pad1(w2)], axis=1)], axis=0)
        return (jnp.tile(pos, (1, NSA_GROUPS)), w1bd.astype(BF16), jnp.tile(b1, NSA_GROUPS)[None, :],
                w2bd.astype(BF16), jnp.concatenate([pad(b2), pad1(b2)])[None, :])
    k = one(k_pos, k_w1, k_b1, k_w2, k_b2, False)
    v = one(v_pos, v_w1, v_b1, v_w2, v_b2, True)
    return tuple(jnp.stack([a, b]) for a, b in zip(k, v))


STRIP = 64


def _lane_tile(col, n):
    reps = [col] * (n // LANES)
    if n % LANES:
        reps.append(col[:, :n % LANES])
    return reps[0] if len(reps) == 1 else jnp.concatenate(reps, axis=1)


def _flash_reset(m_ref, acc_ref):
    m_ref[...] = jnp.full(m_ref.shape, NEG_INF, F32)
    acc_ref[...] = jnp.zeros(acc_ref.shape, F32)


def _flash_update(s_ref, v, m_ref, acc_ref, p_ref, mask=None):
    rows, n = s_ref.shape
    for r in range(rows // STRIP):
        rs = slice(STRIP * r, STRIP * (r + 1))
        s = s_ref[rs, :]
        if mask is not None:
            s = jnp.where(mask[rs], s, NEG_INF)
        if m_ref is None:
            p_ref[rs, :] = jnp.exp2(s - jnp.max(s, axis=1, keepdims=True)).astype(BF16)
            continue
        m_old = m_ref[rs, :]
        m_new = jnp.maximum(m_old, jnp.max(s, axis=1, keepdims=True))
        p_ref[rs, :] = jnp.exp2(s - _lane_tile(m_new, n)).astype(BF16)
        acc_ref[rs, :] = jnp.exp2(m_old - m_new) * acc_ref[rs, :]
        m_ref[rs, :] = m_new
    if m_ref is None:
        acc_ref[...] = _dot(p_ref[...], v)
    else:
        acc_ref[...] += _dot(p_ref[...], v)


NSA_TQ = 256
NSA_ROWS = NSA_REP * NSA_TQ
SLC_CHUNK = 512
WIN_SPAN = WINDOW + NSA_TQ
BIAS_LANE0 = LANES


def _nsa_body(q_ref, kvn_ref, kvc_ref, misc_ref, cov_ref, o_ref,
              kaug_ref, score_ref, qbias_ref, sa_ref, sb_ref, pa_ref, pb_ref, ms_ref, accs_ref,
              sw_ref, pw_ref, accw_ref, sc_ref, pn_ref, pc_ref, oc_ref):
    c = pl.program_id(1)
    rows = NSA_ROWS
    t_row = c * NSA_TQ + (lax.broadcasted_iota(jnp.int32, (rows, 1), 0) & (NSA_TQ - 1))
    ncmp = kvc_ref.shape[2]
    nblk = kaug_ref.shape[1] // SLC_LEN

    @pl.when(c == 0)
    def _():
        nkeys = kaug_ref.shape[1]
        key_blk = lax.broadcasted_iota(jnp.int32, (nkeys, LANES), 0) // SLC_LEN
        onehot = jnp.where(key_blk == lax.broadcasted_iota(jnp.int32, (nkeys, LANES), 1), 1.0, 0.0)
        for g in range(NSA_GROUPS):
            kaug_ref[g, :, :BIAS_LANE0] = kvn_ref[0, g]
            kaug_ref[g, :, BIAS_LANE0:] = onehot.astype(BF16)

    qs = [q_ref[0, NSA_REP * g:NSA_REP * (g + 1)].reshape(rows, LANES) for g in range(NSA_GROUPS)]

    cmp_valid = (CMP_STRIDE * lax.broadcasted_iota(jnp.int32, (1, ncmp), 1) + CMP_LEN - 1) <= t_row
    for g in range(NSA_GROUPS):
        sc_ref[g] = _dot_nt(qs[g], kvc_ref[g, 0])
    win_start = pl.multiple_of(jnp.maximum(c * NSA_TQ - WINDOW, 0), NSA_TQ)

    def window_scores(g):
        sw_ref[g] = _dot_nt(qs[g], kvn_ref[0, 4 + g, pl.ds(win_start, WIN_SPAN), :])

    window_scores(0)
    for g in range(NSA_GROUPS):
        for r in range(rows // STRIP):
            rs = slice(STRIP * r, STRIP * (r + 1))
            s = jnp.where(cmp_valid[rs], sc_ref[g, rs, :], NEG_INF)
            p = jnp.where(cmp_valid[rs], jnp.exp2(s - jnp.max(s, axis=1, keepdims=True)), 0.0)
            l = jnp.sum(p, axis=1, keepdims=True)
            p = p * jnp.where(l > 0.0, 1.0 / l, 0.0)
            pn_ref[g, rs, :] = p
            pc_ref[g, rs, :] = p.astype(BF16)
        oc_ref[g] = _dot(pc_ref[g], kvc_ref[NSA_GROUPS + g, 0])

    def select(nb):
        window_scores(1)
        width = NSA_GROUPS * NSA_TQ
        ps = jnp.concatenate(
            [sum(pn_ref[g, NSA_TQ * r:NSA_TQ * (r + 1), :] for r in range(NSA_REP))
             for g in range(NSA_GROUPS)], axis=0)
        hi = ps.astype(BF16)
        lo = (ps - hi.astype(F32)).astype(BF16)
        imp = _dot_nt(cov_ref[:nb, :], hi) + _dot_nt(cov_ref[:nb, :], lo)
        jidx = lax.broadcasted_iota(jnp.int32, (nb, width), 0)
        lane_q = lax.broadcasted_iota(jnp.int32, (1, width), 1) & (NSA_TQ - 1)
        cur = c * (NSA_TQ // SLC_LEN) + lane_q // SLC_LEN
        forced = (jidx == 0) | (jidx == cur) | (jidx == cur - 1)
        score = jnp.where(jidx <= cur, jnp.where(forced, FORCE_BONUS, imp), NEG_INF)
        score_ref[:nb, :] = score
        sub = 8
        cnt = [jnp.zeros((sub, width), F32) for _ in range(nb // sub)]
        tiles = [score[sub * v:sub * (v + 1)] for v in range(nb // sub)]
        sidx = lax.broadcasted_iota(jnp.int32, (sub, width), 0)
        for jp in range(nb):
            rowv = jnp.broadcast_to(score_ref[jp:jp + 1, :], (sub, width))
            for v in range(nb // sub):
                if sub * v > jp:
                    cnt[v] = jnp.where(rowv >= tiles[v], cnt[v] + 1.0, cnt[v])
                elif sub * v + sub - 1 <= jp:
                    cnt[v] = jnp.where(rowv > tiles[v], cnt[v] + 1.0, cnt[v])
                else:
                    ge = jnp.where(rowv >= tiles[v], cnt[v] + 1.0, cnt[v])
                    gt = jnp.where(rowv > tiles[v], cnt[v] + 1.0, cnt[v])
                    cnt[v] = jnp.where(sidx + sub * v > jp, ge, gt)
        rank = jnp.concatenate(cnt, axis=0)
        bias = jnp.where(rank < float(TOPN), 0.0, NEG_INF)
        bias = jnp.concatenate([bias, jnp.zeros((LANES - nb, width), F32)], axis=0)
        qbias_ref[...] = bias.T.astype(BF16)

    visible = (c + 1) * (NSA_TQ // SLC_LEN)

    @pl.when(visible <= TOPN)
    def _():
        window_scores(1)
        qbias_ref[...] = jnp.zeros(qbias_ref.shape, BF16)

    bounds = [TOPN] + [nb for nb in (nblk // 2, 3 * nblk // 4) if TOPN < nb < nblk] + [nblk]
    for lo_nb, nb in zip(bounds[:-1], bounds[1:]):
        pl.when((visible > lo_nb) & (visible <= nb))(functools.partial(select, nb))

    _flash_reset(ms_ref, accs_ref)
    qas = [jnp.concatenate(
        [qs[g], jnp.concatenate([qbias_ref[NSA_TQ * g:NSA_TQ * (g + 1), :]] * NSA_REP, axis=0)],
        axis=1) for g in range(NSA_GROUPS)]
    last = c // (SLC_CHUNK // NSA_TQ)

    def slc_scores(kc, dst):
        k0 = pl.multiple_of(kc * SLC_CHUNK, SLC_CHUNK)
        for g in range(NSA_GROUPS):
            dst[g] = _dot_nt(qas[g], kaug_ref[g, pl.ds(k0, SLC_CHUNK), :])

    def slc_update(src, p_ref, kc, causal):
        k0 = pl.multiple_of(kc * SLC_CHUNK, SLC_CHUNK)
        mask = None
        if causal:
            mask = (k0 + lax.broadcasted_iota(jnp.int32, (1, SLC_CHUNK), 1)) <= t_row
        for g in range(NSA_GROUPS):
            _flash_update(src.at[g], kvn_ref[0, 2 + g, pl.ds(k0, SLC_CHUNK), :],
                          ms_ref.at[g], accs_ref.at[g], p_ref.at[g], mask)

    slc_scores(0, sa_ref)

    diff = t_row - (win_start + lax.broadcasted_iota(jnp.int32, (1, WIN_SPAN), 1))
    win_valid = (diff >= 0) & (diff < WINDOW)
    for g in range(NSA_GROUPS):
        _flash_update(sw_ref.at[g], kvn_ref[0, 6 + g, pl.ds(win_start, WIN_SPAN), :],
                      None, accw_ref.at[g], pw_ref.at[g], win_valid)

    def slc_pair(i, carry):
        slc_scores(2 * i + 1, sb_ref)
        slc_update(sa_ref, pa_ref, 2 * i, False)
        slc_scores(2 * i + 2, sa_ref)
        slc_update(sb_ref, pb_ref, 2 * i + 1, False)
        return carry
    lax.fori_loop(0, last // 2, slc_pair, 0)
    tail = 2 * (last // 2)

    @pl.when(last > tail)
    def _():
        slc_scores(tail + 1, sb_ref)
        slc_update(sa_ref, pa_ref, tail, False)
        slc_update(sb_ref, pb_ref, tail + 1, True)

    @pl.when(last == tail)
    def _():
        slc_update(sa_ref, pa_ref, tail, True)

    sig = 1.0 / (1.0 + jnp.exp(-misc_ref[...]))
    low = lax.broadcasted_iota(jnp.int32, (NSA_TQ, LANES), 1) < NSA_DH
    denom_lane = jnp.where(low, ONES_LANE, ONES_LANE_G1)

    def pair(ref, rs):
        a0, a1 = ref[0, rs, :], ref[1, rs, :]
        return jnp.where(low, a0, a1), jnp.where(low, a1, a0)

    outs = []
    for r in range(NSA_REP):
        rs = slice(NSA_TQ * r, NSA_TQ * (r + 1))
        gate = [jnp.take_along_axis(
            sig, jnp.where(low, GATE_LANE0 + 3 * r + br, GATE_LANE0 + 3 * (NSA_REP + r) + br), axis=1)
            for br in range(3)]
        o_cmp, _ = pair(oc_ref, rs)
        o_slc, l_slc = pair(accs_ref, rs)
        o_win, l_win = pair(accw_ref, rs)
        outs.append(gate[0] * o_cmp
                    + (gate[1] / jnp.take_along_axis(l_slc, denom_lane, axis=1)) * o_slc
                    + (gate[2] / jnp.take_along_axis(l_win, denom_lane, axis=1)) * o_win)
    o_ref[0] = jnp.concatenate(outs, axis=1).astype(BF16)


def _nsa(qn, kvn, kvcmp, misc, cov_t, B, S):
    nblk = S // SLC_LEN
    ncmp = S // CMP_STRIDE
    nq = S // NSA_TQ
    rows = NSA_ROWS
    return pl.pallas_call(
        _nsa_body,
        grid=(B, nq),
        in_specs=[
            pl.BlockSpec((1, NSA_HEADS, NSA_TQ, LANES), lambda b, c: (b, 0, c, 0)),
            pl.BlockSpec((1, 8, S, LANES), lambda b, c: (b, 0, 0, 0)),
            pl.BlockSpec((4, 1, ncmp, LANES), lambda b, c: (0, b, 0, 0)),
            pl.BlockSpec((NSA_TQ, LANES), lambda b, c: (b * nq + c, 0)),
            pl.BlockSpec(cov_t.shape, lambda b, c: (0, 0)),
        ],
        out_specs=pl.BlockSpec((1, NSA_TQ, NSA_HEADS * NSA_DH), lambda b, c: (b, c, 0)),
        out_shape=jax.ShapeDtypeStruct((B, S, NSA_HEADS * NSA_DH), BF16),
        scratch_shapes=[
            pltpu.VMEM((NSA_GROUPS, S, 2 * LANES), BF16),
            pltpu.VMEM((nblk, NSA_GROUPS * NSA_TQ), F32),
            pltpu.VMEM((NSA_GROUPS * NSA_TQ, LANES), BF16),
            pltpu.VMEM((NSA_GROUPS, rows, SLC_CHUNK), F32),
            pltpu.VMEM((NSA_GROUPS, rows, SLC_CHUNK), F32),
            pltpu.VMEM((NSA_GROUPS, rows, SLC_CHUNK), BF16),
            pltpu.VMEM((NSA_GROUPS, rows, SLC_CHUNK), BF16),
            pltpu.VMEM((NSA_GROUPS, rows, LANES), F32),
            pltpu.VMEM((NSA_GROUPS, rows, LANES), F32),
            pltpu.VMEM((NSA_GROUPS, rows, WIN_SPAN), F32),
            pltpu.VMEM((NSA_GROUPS, rows, WIN_SPAN), BF16),
            pltpu.VMEM((NSA_GROUPS, rows, LANES), F32),
            pltpu.VMEM((NSA_GROUPS, rows, ncmp), F32),
            pltpu.VMEM((NSA_GROUPS, rows, ncmp), F32),
            pltpu.VMEM((NSA_GROUPS, rows, ncmp), BF16),
            pltpu.VMEM((NSA_GROUPS, rows, LANES), F32),
        ],
        compiler_params=_params("arbitrary", "arbitrary"),
        name="nsa",
    )(qn, kvn, kvcmp, misc, cov_t)


MLA_TQ = 512
MLA_CHUNK = 512
MLA_HPB = 4


def _mla_body(q_ref, k_ref, v_ref, o_ref, sa_ref, sb_ref, pa_ref, pb_ref, m_ref, acc_ref):
    qi = pl.program_id(2)
    t_row = qi * MLA_TQ + lax.broadcasted_iota(jnp.int32, (MLA_TQ, 1), 0)
    _flash_reset(m_ref, acc_ref)

    def scores(kc, dst):
        k0 = pl.multiple_of(kc * MLA_CHUNK, MLA_CHUNK)
        for j in range(MLA_HPB):
            dst[j] = _dot_nt(q_ref[0, j], k_ref[0, j, pl.ds(k0, MLA_CHUNK), :])

    def update(src, p_ref, kc, causal):
        k0 = pl.multiple_of(kc * MLA_CHUNK, MLA_CHUNK)
        mask = None
        if causal:
            mask = (k0 + lax.broadcasted_iota(jnp.int32, (1, MLA_CHUNK), 1)) <= t_row
        for j in range(MLA_HPB):
            _flash_update(src.at[j], v_ref[0, j, pl.ds(k0, MLA_CHUNK), :],
                          m_ref.at[j], acc_ref.at[j], p_ref.at[j], mask)

    scores(0, sa_ref)

    def pair(i, carry):
        scores(2 * i + 1, sb_ref)
        update(sa_ref, pa_ref, 2 * i, False)
        scores(2 * i + 2, sa_ref)
        update(sb_ref, pb_ref, 2 * i + 1, False)
        return carry
    lax.fori_loop(0, qi // 2, pair, 0)
    tail = 2 * (qi // 2)

    @pl.when(qi > tail)
    def _():
        scores(tail + 1, sb_ref)
        update(sa_ref, pa_ref, tail, False)
        update(sb_ref, pb_ref, tail + 1, True)

    @pl.when(qi == tail)
    def _():
        update(sa_ref, pa_ref, tail, True)

    low = lax.broadcasted_iota(jnp.int32, (MLA_TQ, LANES), 1) < MLA_V
    denom_lane = jnp.where(low, ONES_LANE, ONES_LANE_G1)
    outs = []
    for j in range(0, MLA_HPB, 2):
        a0, a1 = acc_ref[j], acc_ref[j + 1]
        denom = jnp.take_along_axis(jnp.where(low, a1, a0), denom_lane, axis=1)
        outs.append(jnp.where(low, a0, a1) * (1.0 / denom))
    o_ref[0] = jnp.concatenate(outs, axis=1).astype(BF16)


def _mla(q, k, v, B, S):
    return pl.pallas_call(
        _mla_body,
        grid=(B, MLA_HEADS // MLA_HPB, S // MLA_TQ),
        in_specs=[
            pl.BlockSpec((1, MLA_HPB, MLA_TQ, LANES), lambda b, h, i: (b, h, i, 0)),
            pl.BlockSpec((1, MLA_HPB, S, LANES), lambda b, h, i: (b, h, 0, 0)),
            pl.BlockSpec((1, MLA_HPB, S, LANES), lambda b, h, i: (b, h, 0, 0)),
        ],
        out_specs=pl.BlockSpec((1, MLA_TQ, MLA_HPB * MLA_V), lambda b, h, i: (b, i, h)),
        out_shape=jax.ShapeDtypeStruct((B, S, MLA_HEADS * MLA_V), BF16),
        scratch_shapes=[
            pltpu.VMEM((MLA_HPB, MLA_TQ, MLA_CHUNK), F32),
            pltpu.VMEM((MLA_HPB, MLA_TQ, MLA_CHUNK), F32),
            pltpu.VMEM((MLA_HPB, MLA_TQ, MLA_CHUNK), BF16),
            pltpu.VMEM((MLA_HPB, MLA_TQ, MLA_CHUNK), BF16),
            pltpu.VMEM((MLA_HPB, MLA_TQ, LANES), F32),
            pltpu.VMEM((MLA_HPB, MLA_TQ, LANES), F32),
        ],
        compiler_params=_params("arbitrary", "arbitrary", "arbitrary"),
        name="mla",
    )(q, k, v)


MEM_TM = 1024


def _mem_kv_body(m_ref, wk_ref, wv_ref, k_ref, v_ref):
    mb = m_ref[...].astype(BF16)
    k_ref[...] = _dot(mb, wk_ref[...]).astype(BF16)
    v_ref[...] = _dot(mb, wv_ref[...]).astype(BF16)


def _mem_kv(memf, wk, wv, tm):
    R, D = memf.shape
    tok = lambda i: (i, 0)
    const = lambda i: (0, 0)
    return pl.pallas_call(
        _mem_kv_body,
        grid=(R // tm,),
        in_specs=[pl.BlockSpec((tm, D), tok), pl.BlockSpec(wk.shape, const),
                  pl.BlockSpec(wv.shape, const)],
        out_specs=[pl.BlockSpec((tm, D), tok), pl.BlockSpec((tm, D), tok)],
        out_shape=[jax.ShapeDtypeStruct((R, D), BF16), jax.ShapeDtypeStruct((R, D), BF16)],
        compiler_params=_params("arbitrary"),
        name="mem_kv",
    )(memf, wk, wv)


def _mem_attn_body(x0_ref, on_ref, om_ref, wmix_ref, g1_ref, b1_ref,
                   k_ref, v_ref, wq_ref, wo_ref, g_ref, b_ref, o_ref, *, alpha):
    half = on_ref.shape[1]
    mix = _dot(on_ref[...], wmix_ref[:half, :]) + _dot(om_ref[...], wmix_ref[half:, :])
    x = _layer_norm(alpha * x0_ref[...] + mix, g1_ref[...], b1_ref[...])
    D = x.shape[1]
    dh = D // MEM_HEADS
    q = (_dot(x.astype(BF16), wq_ref[...]) * (dh ** -0.5 * LOG2E)).astype(BF16)
    outs = []
    for h in range(MEM_HEADS):
        cs = slice(dh * h, dh * (h + 1))
        s = _dot_nt(q[:, cs], k_ref[0, :, cs])
        p = jnp.exp2(s - jnp.max(s, axis=1, keepdims=True))
        l = jnp.sum(p, axis=1, keepdims=True)
        outs.append((_dot(p.astype(BF16), v_ref[0, :, cs]) * (1.0 / l)).astype(BF16))
    o = jnp.concatenate(outs, axis=1)
    y = _dot(o, wo_ref[...])
    o_ref[...] = _layer_norm(alpha * x + y, g_ref[...], b_ref[...])


def _mem_attn(xf, o_nsa, o_mla, w_o, g1, b1, k_mem, v_mem, wq, wo, g, b, alpha, S, tm):
    T, D = xf.shape
    nst = S // tm
    M = k_mem.shape[1]
    tok = lambda i: (i, 0)
    const = lambda i: (0, 0)
    memb = lambda i: (i // nst, 0, 0)
    return pl.pallas_call(
        functools.partial(_mem_attn_body, alpha=alpha),
        grid=(T // tm,),
        in_specs=[
            pl.BlockSpec((tm, D), tok),
            pl.BlockSpec((tm, o_nsa.shape[1]), tok),
            pl.BlockSpec((tm, o_mla.shape[1]), tok),
            pl.BlockSpec(w_o.shape, const),
            pl.BlockSpec((1, D), const),
            pl.BlockSpec((1, D), const),
            pl.BlockSpec((1, M, D), memb),
            pl.BlockSpec((1, M, D), memb),
            pl.BlockSpec(wq.shape, const),
            pl.BlockSpec(wo.shape, const),
            pl.BlockSpec((1, D), const),
            pl.BlockSpec((1, D), const),
        ],
        out_specs=pl.BlockSpec((tm, D), tok),
        out_shape=jax.ShapeDtypeStruct((T, D), F32),
        compiler_params=_params("arbitrary"),
        name="mem_attn",
    )(xf, o_nsa, o_mla, w_o, g1, b1, k_mem, v_mem, wq, wo, g, b)


HALO = 16


FFN_SLAB = 256
FFN_TM = 512


def _ffn_body(x_ref, xh_ref, wg_ref, wu_ref, cw_ref, cb_ref, wd_ref, g_ref, b_ref, o_ref,
              act_ref, *, alpha, seq_tiles):
    i = pl.program_id(0)
    x = x_ref[...]
    xb = x.astype(BF16)
    xe = jnp.concatenate([xh_ref[...].astype(BF16), xb], axis=0)
    tm = x.shape[0]
    row = lax.broadcasted_iota(jnp.int32, (tm, 1), 0)
    seq_start = i % seq_tiles == 0
    for c0 in range(0, wg_ref.shape[1], FFN_SLAB):
        cs = slice(c0, c0 + FFN_SLAB)
        gate_e = _dot(xe, wg_ref[:, cs])
        gate = gate_e[HALO:]
        up = _dot(xb, wu_ref[:, cs])
        halo = jnp.where(seq_start, 0.0, gate_e[:HALO])
        g1 = jnp.where(row == 0, halo[HALO - 1:HALO], pltpu.roll(gate, 1, 0))
        g2 = jnp.where(row == 0, halo[HALO - 2:HALO - 1],
                       jnp.where(row == 1, halo[HALO - 1:HALO], pltpu.roll(gate, 2, 0)))
        conv = cw_ref[0:1, cs] * g2 + cw_ref[1:2, cs] * g1 + cw_ref[2:3, cs] * gate + cb_ref[:, cs]
        act_ref[:, cs] = (conv * (1.0 / (1.0 + jnp.exp(-conv))) * up).astype(BF16)
    y = _dot(act_ref[...], wd_ref[...])
    o_ref[...] = _layer_norm(alpha * x + y, g_ref[...], b_ref[...])


def _ffn(xf, wg, wu, cw, cb, wd, g, b, alpha, S, tm):
    T, D = xf.shape
    dff = wg.shape[1]
    assert dff % FFN_SLAB == 0
    tok = lambda i: (i, 0)
    const = lambda i: (0, 0)
    return pl.pallas_call(
        functools.partial(_ffn_body, alpha=alpha, seq_tiles=S // tm),
        grid=(T // tm,),
        in_specs=[
            pl.BlockSpec((tm, D), tok),
            pl.BlockSpec((HALO, D), lambda i: (jnp.maximum(i * (tm // HALO) - 1, 0), 0)),
            pl.BlockSpec((D, dff), const),
            pl.BlockSpec((D, dff), const),
            pl.BlockSpec((CONV_WIDTH, dff), const),
            pl.BlockSpec((1, dff), const),
            pl.BlockSpec((dff, D), const),
            pl.BlockSpec((1, D), const),
            pl.BlockSpec((1, D), const),
        ],
        out_specs=pl.BlockSpec((tm, D), tok),
        out_shape=jax.ShapeDtypeStruct((T, D), F32),
        scratch_shapes=[pltpu.VMEM((tm, dff), BF16)],
        compiler_params=_params("arbitrary"),
        name="ffn",
    )(xf, xf, wg, wu, cw, cb, wd, g, b)


def _inv_freq_row(dim, lane_lo, lane_hi, period):
    inv = ROPE_THETA ** (-np.arange(0, dim, 2, dtype=np.float64) / dim)
    row = np.zeros((1, LANES), np.float32)
    for lane in range(lane_lo, lane_hi):
        row[0, lane] = inv[(lane % period) % (dim // 2)]
    return jnp.asarray(row)


def _cover_t(S):
    nc = S // CMP_STRIDE
    ns = S // SLC_LEN
    cs = np.arange(nc)[:, None] * CMP_STRIDE
    ss = np.arange(ns)[None, :] * SLC_LEN
    cover = np.clip(np.minimum(cs + CMP_LEN, ss + SLC_LEN) - np.maximum(cs, ss), 0, None) / CMP_LEN
    cover[nc - 1:] = 0.0
    return jnp.asarray(cover.T, dtype=BF16)


def _permute_w_in(w):
    D = w.shape[0]
    c1 = NSA_HEADS * NSA_DH
    c2 = c1 + 3 * 2 * NSA_GROUPS * NSA_DH
    c3 = c2 + 3 * NSA_HEADS
    c4 = c3 + MLA_Q_RANK
    c5 = c4 + MLA_KV_RANK
    c6 = c5 + MLA_ROPE
    half = MLA_ROPE // 2
    z = lambda n: jnp.zeros((D, n), w.dtype)
    misc = jnp.concatenate(
        [w[:, c5 + half:c6], z(GATE_LANE0 - half), w[:, c2:c3], z(MLA_PE1 - GATE_LANE0 - (c3 - c2)),
         w[:, c5:c5 + half], z(LANES - MLA_PE1 - half)], axis=1)
    return jnp.concatenate([w[:, :c2], w[:, c3:c5], misc], axis=1).astype(BF16)


def _mla_head_lanes(nope, pe):
    r, H, _ = nope.shape
    half = MLA_ROPE // 2
    split = MLA_PE1 - half
    pad = jnp.zeros((r, H, LANES - MLA_NOPE - MLA_ROPE), nope.dtype)
    return jnp.concatenate([pe[..., half:], nope[..., :split], pe[..., :half], nope[..., split:], pad],
                           axis=2).reshape(r, H * LANES)


def _permute_w_uq(w):
    r = w.shape[0]
    w3 = w.reshape(r, MLA_HEADS, MLA_NOPE + MLA_ROPE)
    return _mla_head_lanes(w3[..., :MLA_NOPE], w3[..., MLA_NOPE:]).astype(BF16)


def _permute_w_ukv(w):
    r = w.shape[0]
    w3 = w.reshape(r, MLA_HEADS, MLA_NOPE + MLA_V)
    k = _mla_head_lanes(w3[..., :MLA_NOPE], jnp.zeros((r, MLA_HEADS, MLA_ROPE), w.dtype))
    v = _pad_lanes(w3[..., MLA_NOPE:]).reshape(r, MLA_HEADS // 2, 2, LANES)
    v = jnp.stack([v[:, :, 0], jnp.roll(v[:, :, 1], LANES - MLA_V, axis=-1)], axis=2)
    return jnp.concatenate([k, v.reshape(r, MLA_HEADS * LANES)], axis=1).astype(BF16)


def _permute_w_o(w):
    n = NSA_HEADS * NSA_DH
    wn = w[:n].reshape(NSA_GROUPS, NSA_REP, NSA_DH, -1).transpose(1, 0, 2, 3).reshape(n, -1)
    return jnp.concatenate([wn, w[n:]], axis=0).astype(BF16)


def _pad_lanes(a):
    return jnp.concatenate([a, jnp.zeros(a.shape[:-1] + (LANES - a.shape[-1],), a.dtype)], axis=-1)


def kernel(x, mem, positions, w_in, nsa_k_pos, nsa_ck_w1, nsa_ck_b1, nsa_ck_w2, nsa_ck_b2,
           nsa_v_pos, nsa_cv_w1, nsa_cv_b1, nsa_cv_w2, nsa_cv_b2,
           mla_q_norm, mla_w_uq, mla_kv_norm, mla_w_ukv, w_o, ln1_g, ln1_b,
           mem_wq, mem_wk, mem_wv, mem_wo, ln2_g, ln2_b,
           ffn_w_up, ffn_conv_w, ffn_conv_b, ffn_w_down, ln3_g, ln3_b):
    B, S, D = x.shape
    T = B * S
    depth = w_in.shape[0]
    alpha = (2.0 * depth) ** 0.25
    d_ff = ffn_w_down.shape[1]
    tm = min(512, S)
    assert S % MLA_TQ == 0 and S >= WIN_SPAN and S % tm == 0
    assert (B * mem.shape[1]) % 256 == 0

    pos = positions.reshape(T, 1)
    pos_cmp = positions[:, CMP_LEN - 1::CMP_STRIDE]
    pos_cmp = jnp.concatenate([pos_cmp, pos_cmp[:, -1:]], axis=1)[:, :, None]
    inv_cmp = _inv_freq_row(NSA_DH, 0, NSA_DH, NSA_DH)
    inv_tok = (_inv_freq_row(NSA_DH, 0, NSA_DH // 2, NSA_DH)
               + _inv_freq_row(MLA_ROPE, NSA_DH // 2, NSA_DH // 2 + MLA_ROPE // 2, MLA_ROPE // 2))
    cov_t = _cover_t(S)
    memf = mem.reshape(B * mem.shape[1], D)

    xf = x.reshape(T, D)
    for l in range(depth):
        qn, kvn, kvc, misc, q_m, k_m, v_m = _inproj(
            pos, inv_tok, xf, _permute_w_in(w_in[l]), mla_q_norm[l][None, :],
            mla_kv_norm[l][None, :], _permute_w_uq(mla_w_uq[l]), _permute_w_ukv(mla_w_ukv[l]),
            B, S, tm)
        kvcmp = _compress(
            pos_cmp, inv_cmp, kvc,
            *_compress_weights(nsa_k_pos[l], nsa_ck_w1[l], nsa_ck_b1[l], nsa_ck_w2[l], nsa_ck_b2[l],
                               nsa_v_pos[l], nsa_cv_w1[l], nsa_cv_b1[l], nsa_cv_w2[l], nsa_cv_b2[l]),
            B, S)
        o_nsa = _nsa(qn, kvn, kvcmp, misc, cov_t, B, S)
        o_mla = _mla(q_m, k_m, v_m, B, S)
        k_mem, v_mem = _mem_kv(memf, mem_wk[l].astype(BF16), mem_wv[l].astype(BF16), 256)
        xf = _mem_attn(xf, o_nsa.reshape(T, -1), o_mla.reshape(T, -1), _permute_w_o(w_o[l]),
                       ln1_g[l][None, :], ln1_b[l][None, :],
                       k_mem.reshape(B, -1, D), v_mem.reshape(B, -1, D),
                       mem_wq[l].astype(BF16), mem_wo[l].astype(BF16),
                       ln2_g[l][None, :], ln2_b[l][None, :], alpha, S, min(MEM_TM, S))
        xf = _ffn(xf, ffn_w_up[l][:, :d_ff].astype(BF16), ffn_w_up[l][:, d_ff:].astype(BF16),
                  ffn_conv_w[l], ffn_conv_b[l][None, :], ffn_w_down[l].astype(BF16),
                  ln3_g[l][None, :], ln3_b[l][None, :], alpha, S, FFN_TM)
    return xf.reshape(B, S, D)
```

```python
import functools
import math

import numpy as np
import jax
import jax.numpy as jnp
from jax import lax
from jax.experimental import pallas as pl
from jax.experimental.pallas import tpu as pltpu

F32 = jnp.float32
BF16 = jnp.bfloat16

NSA_HEADS = 8
NSA_GROUPS = 2
NSA_REP = NSA_HEADS // NSA_GROUPS
NSA_DH = 64
CMP_STRIDE = 16
CMP_LEN = 32
SLC_LEN = 64
TOPN = 16
WINDOW = 512
CMP_HIDDEN = 128
FORCE_BONUS = 1e4
MLA_HEADS = 8
MLA_Q_RANK = 384
MLA_KV_RANK = 256
MLA_NOPE = 64
MLA_ROPE = 32
MLA_V = 64
MEM_HEADS = 4
CONV_WIDTH = 3
ROPE_THETA = 10000.0
LN_EPS = 1e-5
RMS_EPS = 1e-6
NEG_INF = -1e30
LOG2E = math.log2(math.e)

LANES = 128
VMEM_LIMIT = 60 * 1024 * 1024

C_Q = 0
C_KVC = 512
C_KVN = 768
C_LAT = 1280
C_MISC = 1920
IN_COLS_PAD = 2048
GATE_LANE0 = MLA_ROPE
MLA_PE2 = 0
MLA_PE1 = 64
ONES_LANE = 64
ONES_LANE_G1 = 0


def _dot(a, b):
    return jnp.dot(a, b, preferred_element_type=F32)


def _dot_nt(a, b):
    return lax.dot_general(a, b, (((1,), (1,)), ((), ())), preferred_element_type=F32)


def _layer_norm(y, g, b):
    mu = jnp.mean(y, axis=-1, keepdims=True)
    d = y - mu
    var = jnp.mean(d * d, axis=-1, keepdims=True)
    return d * lax.rsqrt(var + LN_EPS) * g + b


def _params(*sem):
    return pltpu.CompilerParams(dimension_semantics=sem, vmem_limit_bytes=VMEM_LIMIT)


def _rope_tables(pos_col, inv_row, half):
    ang = pos_col * inv_row
    cos = jnp.cos(ang)
    sin = jnp.sin(ang)
    lane = lax.broadcasted_iota(jnp.int32, (1, LANES), 1)
    upper = (lane & (2 * half - 1)) >= half
    rot = inv_row != 0.0
    sin_hi = jnp.where(upper & rot, sin, 0.0)
    sin_lo = jnp.where(upper | (~rot), 0.0, -sin)
    return cos, sin_hi, sin_lo


def _apply_rope(v, tabs, half):
    cos, sin_hi, sin_lo = tabs
    return v * cos + pltpu.roll(v, half, 1) * sin_hi + pltpu.roll(v, LANES - half, 1) * sin_lo


def _rms_norm(v, g):
    return v * lax.rsqrt(jnp.mean(v * v, axis=-1, keepdims=True) + RMS_EPS) * g


def _inproj_body(pos_ref, inv_ref, x_ref, w_ref, gq_ref, gkv_ref, wq_ref, wkv_ref,
                 qn_ref, kvn_ref, kvc_ref, misc_ref, qm_ref, km_ref, vm_ref):
    xb = x_ref[...].astype(BF16)
    lane = lax.broadcasted_iota(jnp.int32, (1, LANES), 1)
    low = lane < NSA_DH
    qscale = NSA_DH ** -0.5 * LOG2E

    ang = pos_ref[...].astype(F32) * inv_ref[...]
    cos, sin = jnp.cos(ang), jnp.sin(ang)
    nf, mf = NSA_DH // 2, MLA_ROPE // 2

    def tile_nsa(t):
        t = jnp.where(lane < nf, t, 0.0)
        t = t + pltpu.roll(t, nf, 1)
        return t + pltpu.roll(t, 2 * nf, 1)

    def place_mla(t):
        t = jnp.where((lane >= nf) & (lane < nf + mf), t, 0.0)
        return pltpu.roll(t, MLA_PE1 - nf, 1) + pltpu.roll(t, LANES + MLA_PE2 - nf, 1)

    upper = (lane & (NSA_DH - 1)) >= nf
    sin_n = tile_nsa(sin)
    tabs = (tile_nsa(cos), jnp.where(upper, sin_n, 0.0), jnp.where(upper, 0.0, -sin_n))
    pe1 = (lane >= MLA_PE1) & (lane < MLA_PE1 + mf)
    pe2 = (lane >= MLA_PE2) & (lane < MLA_PE2 + mf)
    pe_lanes = pe1 | pe2
    sin_m = place_mla(sin)
    cos_m = jnp.where(pe_lanes, place_mla(cos), 1.0)
    sin_m = jnp.where(pe1, -sin_m, jnp.where(pe2, sin_m, 0.0))

    def rope_mla(v):
        return v * cos_m + pltpu.roll(v, LANES // 2, 1) * sin_m

    def proj(c0, n):
        return _dot(xb, w_ref[:, c0:c0 + n])

    def split_store(v, ref, idx_lo, idx_hi, pad=0.0):
        ref[0, idx_lo] = jnp.where(low, v, pad).astype(BF16)
        ref[0, idx_hi] = jnp.where(low, pltpu.roll(v, NSA_DH, 1), pad).astype(BF16)

    ones_pad = jnp.where(lane == ONES_LANE, 1.0, 0.0)
    ones_pad_g1 = jnp.where(lane == ONES_LANE_G1, 1.0, 0.0)

    for slab in range(2):
        h = proj(C_Q + 256 * slab, 256)
        for j in range(2):
            r = _apply_rope(h[:, LANES * j:LANES * (j + 1)], tabs, NSA_DH // 2) * qscale
            split_store(r, qn_ref, 4 * slab + 2 * j, 4 * slab + 2 * j + 1)

    h = proj(C_KVC, 256)
    kvc_ref[0] = h[:, :LANES]
    kvc_ref[1] = h[:, LANES:]

    for slab in range(2):
        h = proj(C_KVN + 256 * slab, 256)
        k = _apply_rope(h[:, :LANES], tabs, NSA_DH // 2)
        split_store(k, kvn_ref, 4 * slab, 4 * slab + 1)
        v = h[:, LANES:]
        kvn_ref[0, 4 * slab + 2] = jnp.where(low, v, ones_pad).astype(BF16)
        kvn_ref[0, 4 * slab + 3] = jnp.where(low, ones_pad_g1, v).astype(BF16)

    lat = [proj(C_LAT + 256 * i, 256) for i in range((IN_COLS_PAD - C_LAT) // 256)]
    misc = lat[2][:, LANES:]
    misc_ref[...] = misc
    mq = jnp.concatenate([lat[0], lat[1][:, :LANES]], axis=1)
    mkv = jnp.concatenate([lat[1][:, LANES:], lat[2][:, :LANES]], axis=1)
    mscale = (MLA_NOPE + MLA_ROPE) ** -0.5 * LOG2E
    qn = _rms_norm(mq, gq_ref[...]).astype(BF16)
    kvn = _rms_norm(mkv, gkv_ref[...]).astype(BF16)
    kpe = jnp.where(pe_lanes, rope_mla(misc), 0.0)
    kcols = MLA_HEADS * LANES
    for slab in range(MLA_HEADS // 2):
        cs = slice(256 * slab, 256 * (slab + 1))
        hq = _dot(qn, wq_ref[:, cs])
        hk = _dot(kvn, wkv_ref[:, cs])
        hv = _dot(kvn, wkv_ref[:, kcols + 256 * slab:kcols + 256 * (slab + 1)])
        for j in range(2):
            ls = slice(LANES * j, LANES * (j + 1))
            qm_ref[0, 2 * slab + j] = (rope_mla(hq[:, ls]) * mscale).astype(BF16)
            km_ref[0, 2 * slab + j] = (hk[:, ls] + kpe).astype(BF16)
            vm_ref[0, 2 * slab + j] = (hv[:, ls] + (ones_pad_g1 if j else ones_pad)).astype(BF16)


def _inproj(pos, inv_row, xf, w_in_p, gq, gkv, wq_p, wkv_p, B, S, tm):
    T = B * S
    nst = S // tm
    tok = lambda i: (i, 0)
    const = lambda i: (0, 0)
    head_blk = lambda i: (i // nst, 0, i % nst, 0)
    heads = jax.ShapeDtypeStruct((B, 8, S, LANES), BF16)
    return pl.pallas_call(
        _inproj_body,
        grid=(T // tm,),
        in_specs=[
            pl.BlockSpec((tm, 1), tok),
            pl.BlockSpec((1, LANES), const),
            pl.BlockSpec((tm, xf.shape[1]), tok),
            pl.BlockSpec(w_in_p.shape, const),
            pl.BlockSpec(gq.shape, const),
            pl.BlockSpec(gkv.shape, const),
            pl.BlockSpec(wq_p.shape, const),
            pl.BlockSpec(wkv_p.shape, const),
        ],
        out_specs=[
            pl.BlockSpec((1, 8, tm, LANES), head_blk),
            pl.BlockSpec((1, 8, tm, LANES), head_blk),
            pl.BlockSpec((2, tm, LANES), lambda i: (0, i, 0)),
            pl.BlockSpec((tm, LANES), tok),
            pl.BlockSpec((1, 8, tm, LANES), head_blk),
            pl.BlockSpec((1, 8, tm, LANES), head_blk),
            pl.BlockSpec((1, 8, tm, LANES), head_blk),
        ],
        out_shape=[
            heads,
            heads,
            jax.ShapeDtypeStruct((2, T, LANES), F32),
            jax.ShapeDtypeStruct((T, LANES), F32),
            heads,
            heads,
            heads,
        ],
        compiler_params=_params("arbitrary"),
        name="inproj",
    )(pos, inv_row, xf, w_in_p, gq, gkv, wq_p, wkv_p)


def _compress_body(pos_ref, inv_ref, x_ref, pe_ref, w1_ref, b1_ref, w2_ref, b2_ref, o_ref):
    is_k = pl.program_id(0) == 0
    nch = o_ref.shape[2]
    a1 = jnp.zeros((nch, NSA_GROUPS * CMP_HIDDEN), F32)
    a2 = jnp.zeros((nch, NSA_GROUPS * CMP_HIDDEN), F32)
    for l in range(CMP_STRIDE):
        xl = x_ref.at[0, 0][pl.ds(l, nch, stride=CMP_STRIDE), :]
        a1 = a1 + _dot((xl + pe_ref[0, l:l + 1, :]).astype(BF16), w1_ref[0, l])
        a2 = a2 + _dot((xl + pe_ref[0, CMP_STRIDE + l:CMP_STRIDE + l + 1, :]).astype(BF16),
                       w1_ref[0, CMP_STRIDE + l])
    pre = a1 + pltpu.roll(a2, nch - 1, 0) + b1_ref[0]
    hid = jax.nn.gelu(pre, approximate=True)
    out = _dot(hid.astype(BF16), w2_ref[0]) + b2_ref[0]
    tabs = _rope_tables(pos_ref[0].astype(F32), inv_ref[...], NSA_DH // 2)
    row = lax.broadcasted_iota(jnp.int32, (nch, 1), 0)
    for g in range(NSA_GROUPS):
        og = out[:, LANES * g:LANES * (g + 1)]
        og = jnp.where(is_k, _apply_rope(og, tabs, NSA_DH // 2), og)
        o_ref[g, 0] = jnp.where(row < nch - 1, og, 0.0).astype(BF16)


def _compress(pos_cmp, inv_nsa, kvc, pe, w1, b1, w2, b2, B, S):
    nch = S // CMP_STRIDE
    x = kvc.reshape(2, B, S, LANES)
    kv = lambda j, b: (j, 0, 0)
    kv4 = lambda j, b: (j, 0, 0, 0)
    return pl.pallas_call(
        _compress_body,
        grid=(2, B),
        in_specs=[
            pl.BlockSpec((1, nch, 1), lambda j, b: (b, 0, 0)),
            pl.BlockSpec((1, LANES), lambda j, b: (0, 0)),
            pl.BlockSpec((1, 1, S, LANES), lambda j, b: (j, b, 0, 0)),
            pl.BlockSpec((1,) + pe.shape[1:], kv),
            pl.BlockSpec((1,) + w1.shape[1:], kv4),
            pl.BlockSpec((1,) + b1.shape[1:], kv),
            pl.BlockSpec((1,) + w2.shape[1:], kv),
            pl.BlockSpec((1,) + b2.shape[1:], kv),
        ],
        out_specs=pl.BlockSpec((NSA_GROUPS, 1, nch, LANES), lambda j, b: (j, b, 0, 0)),
        out_shape=jax.ShapeDtypeStruct((2 * NSA_GROUPS, B, nch, LANES), BF16),
        compiler_params=_params("arbitrary", "arbitrary"),
        name="compress",
    )(pos_cmp, inv_nsa, x, pe, w1, b1, w2, b2)


def _compress_weights(k_pos, k_w1, k_b1, k_w2, k_b2, v_pos, v_w1, v_b1, v_w2, v_b2):
    def one(pos, w1, b1, w2, b2, g1_high):
        w1l = w1.reshape(CMP_LEN, NSA_DH, CMP_HIDDEN)
        z1 = jnp.zeros_like(w1l)
        w1bd = jnp.concatenate([jnp.concatenate([w1l, z1], axis=2),
                                jnp.concatenate([z1, w1l], axis=2)], axis=1)
        pad = lambda a: _pad_lanes(a)
        pad1 = (lambda a: jnp.roll(_pad_lanes(a), LANES - a.shape[-1], axis=-1)) if g1_high else pad
        z2 = jnp.zeros_like(pad(w2))
        w2bd = jnp.concatenate([jnp.concatenate([pad(w2), z2], axis=1),
                                jnp.concatenate([z2, pad1(w2)], axis=1)], axis=0)
        return (jnp.tile(pos, (1, NSA_GROUPS)), w1bd.astype(BF16), jnp.tile(b1, NSA_GROUPS)[None, :],
                w2bd.astype(BF16), jnp.concatenate([pad(b2), pad1(b2)])[None, :])
    k = one(k_pos, k_w1, k_b1, k_w2, k_b2, False)
    v = one(v_pos, v_w1, v_b1, v_w2, v_b2, True)
    return tuple(jnp.stack([a, b]) for a, b in zip(k, v))


STRIP = 64


def _lane_tile(col, n):
    reps = [col] * (n // LANES)
    if n % LANES:
        reps.append(col[:, :n % LANES])
    return reps[0] if len(reps) == 1 else jnp.concatenate(reps, axis=1)


def _flash_reset(m_ref, acc_ref):
    m_ref[...] = jnp.full(m_ref.shape, NEG_INF, F32)
    acc_ref[...] = jnp.zeros(acc_ref.shape, F32)


def _flash_update(s_ref, v, m_ref, acc_ref, p_ref, mask=None):
    rows, n = s_ref.shape
    for r in range(rows // STRIP):
        rs = slice(STRIP * r, STRIP * (r + 1))
        s = s_ref[rs, :]
        if mask is not None:
            s = jnp.where(mask[rs], s, NEG_INF)
        if m_ref is None:
            p_ref[rs, :] = jnp.exp2(s - jnp.max(s, axis=1, keepdims=True)).astype(BF16)
            continue
        m_old = m_ref[rs, :]
        m_new = jnp.maximum(m_old, jnp.max(s, axis=1, keepdims=True))
        p_ref[rs, :] = jnp.exp2(s - _lane_tile(m_new, n)).astype(BF16)
        acc_ref[rs, :] = jnp.exp2(m_old - m_new) * acc_ref[rs, :]
        m_ref[rs, :] = m_new
    if m_ref is None:
        acc_ref[...] = _dot(p_ref[...], v)
    else:
        acc_ref[...] += _dot(p_ref[...], v)


NSA_TQ = 256
NSA_ROWS = NSA_REP * NSA_TQ
SLC_CHUNK = 512
WIN_SPAN = WINDOW + NSA_TQ
BIAS_LANE0 = LANES


def _nsa_body(q_ref, kvn_ref, kvc_ref, misc_ref, cov_ref, o_ref,
              kaug_ref, score_ref, qbias_ref, sa_ref, sb_ref, pa_ref, pb_ref, ms_ref, accs_ref,
              sw_ref, pw_ref, accw_ref, sc_ref, pn_ref, pc_ref, oc_ref):
    c = pl.program_id(1)
    rows = NSA_ROWS
    t_row = c * NSA_TQ + (lax.broadcasted_iota(jnp.int32, (rows, 1), 0) & (NSA_TQ - 1))
    ncmp = kvc_ref.shape[2]
    nblk = kaug_ref.shape[1] // SLC_LEN

    @pl.when(c == 0)
    def _():
        nkeys = kaug_ref.shape[1]
        key_blk = lax.broadcasted_iota(jnp.int32, (nkeys, LANES), 0) // SLC_LEN
        onehot = jnp.where(key_blk == lax.broadcasted_iota(jnp.int32, (nkeys, LANES), 1), 1.0, 0.0)
        for g in range(NSA_GROUPS):
            kaug_ref[g, :, :BIAS_LANE0] = kvn_ref[0, g]
            kaug_ref[g, :, BIAS_LANE0:] = onehot.astype(BF16)

    qs = [q_ref[0, NSA_REP * g:NSA_REP * (g + 1)].reshape(rows, LANES) for g in range(NSA_GROUPS)]

    cmp_valid = (CMP_STRIDE * lax.broadcasted_iota(jnp.int32, (1, ncmp), 1) + CMP_LEN - 1) <= t_row
    for g in range(NSA_GROUPS):
        sc_ref[g] = _dot_nt(qs[g], kvc_ref[g, 0])
    win_start = pl.multiple_of(jnp.maximum(c * NSA_TQ - WINDOW, 0), NSA_TQ)

    def window_scores(g):
        sw_ref[g] = _dot_nt(qs[g], kvn_ref[0, 4 + g, pl.ds(win_start, WIN_SPAN), :])

    window_scores(0)
    for g in range(NSA_GROUPS):
        for r in range(rows // STRIP):
            rs = slice(STRIP * r, STRIP * (r + 1))
            s = jnp.where(cmp_valid[rs], sc_ref[g, rs, :], NEG_INF)
            p = jnp.where(cmp_valid[rs], jnp.exp2(s - jnp.max(s, axis=1, keepdims=True)), 0.0)
            l = jnp.sum(p, axis=1, keepdims=True)
            p = p * jnp.where(l > 0.0, 1.0 / l, 0.0)
            pn_ref[g, rs, :] = p
            pc_ref[g, rs, :] = p.astype(BF16)
        oc_ref[g] = _dot(pc_ref[g], kvc_ref[NSA_GROUPS + g, 0])

    def select(nb):
        window_scores(1)
        width = NSA_GROUPS * NSA_TQ
        ps = jnp.concatenate(
            [sum(pn_ref[g, NSA_TQ * r:NSA_TQ * (r + 1), :] for r in range(NSA_REP))
             for g in range(NSA_GROUPS)], axis=0)
        hi = ps.astype(BF16)
        lo = (ps - hi.astype(F32)).astype(BF16)
        imp = _dot_nt(cov_ref[:nb, :], hi) + _dot_nt(cov_ref[:nb, :], lo)
        jidx = lax.broadcasted_iota(jnp.int32, (nb, width), 0)
        lane_q = lax.broadcasted_iota(jnp.int32, (1, width), 1) & (NSA_TQ - 1)
        cur = c * (NSA_TQ // SLC_LEN) + lane_q // SLC_LEN
        forced = (jidx == 0) | (jidx == cur) | (jidx == cur - 1)
        score = jnp.where(jidx <= cur, jnp.where(forced, FORCE_BONUS, imp), NEG_INF)
        score_ref[:nb, :] = score
        sub = 8
        cnt = [jnp.zeros((sub, width), F32) for _ in range(nb // sub)]
        tiles = [score[sub * v:sub * (v + 1)] for v in range(nb // sub)]
        sidx = lax.broadcasted_iota(jnp.int32, (sub, width), 0)
        for jp in range(nb):
            rowv = jnp.broadcast_to(score_ref[jp:jp + 1, :], (sub, width))
            for v in range(nb // sub):
                if sub * v > jp:
                    cnt[v] = jnp.where(rowv >= tiles[v], cnt[v] + 1.0, cnt[v])
                elif sub * v + sub - 1 <= jp:
                    cnt[v] = jnp.where(rowv > tiles[v], cnt[v] + 1.0, cnt[v])
                else:
                    ge = jnp.where(rowv >= tiles[v], cnt[v] + 1.0, cnt[v])
                    gt = jnp.where(rowv > tiles[v], cnt[v] + 1.0, cnt[v])
                    cnt[v] = jnp.where(sidx + sub * v > jp, ge, gt)
        rank = jnp.concatenate(cnt, axis=0)
        bias = jnp.where(rank < float(TOPN), 0.0, NEG_INF)
        bias = jnp.concatenate([bias, jnp.zeros((LANES - nb, width), F32)], axis=0)
        qbias_ref[...] = bias.T.astype(BF16)

    visible = (c + 1) * (NSA_TQ // SLC_LEN)

    @pl.when(visible <= TOPN)
    def _():
        window_scores(1)
        qbias_ref[...] = jnp.zeros(qbias_ref.shape, BF16)

    bounds = [TOPN] + [nb for nb in (nblk // 2, 3 * nblk // 4) if TOPN < nb < nblk] + [nblk]
    for lo_nb, nb in zip(bounds[:-1], bounds[1:]):
        pl.when((visible > lo_nb) & (visible <= nb))(functools.partial(select, nb))

    _flash_reset(ms_ref, accs_ref)
    qas = [jnp.concatenate(
        [qs[g], jnp.concatenate([qbias_ref[NSA_TQ * g:NSA_TQ * (g + 1), :]] * NSA_REP, axis=0)],
        axis=1) for g in range(NSA_GROUPS)]
    last = c // (SLC_CHUNK // NSA_TQ)

    def slc_scores(kc, dst):
        k0 = pl.multiple_of(kc * SLC_CHUNK, SLC_CHUNK)
        for g in range(NSA_GROUPS):
            dst[g] = _dot_nt(qas[g], kaug_ref[g, pl.ds(k0, SLC_CHUNK), :])

    def slc_update(src, p_ref, kc, causal):
        k0 = pl.multiple_of(kc * SLC_CHUNK, SLC_CHUNK)
        mask = None
        if causal:
            mask = (k0 + lax.broadcasted_iota(jnp.int32, (1, SLC_CHUNK), 1)) <= t_row
        for g in range(NSA_GROUPS):
            _flash_update(src.at[g], kvn_ref[0, 2 + g, pl.ds(k0, SLC_CHUNK), :],
                          ms_ref.at[g], accs_ref.at[g], p_ref.at[g], mask)

    slc_scores(0, sa_ref)

    diff = t_row - (win_start + lax.broadcasted_iota(jnp.int32, (1, WIN_SPAN), 1))
    win_valid = lax.bitcast_convert_type(diff, jnp.uint32) < jnp.uint32(WINDOW)
    for g in range(NSA_GROUPS):
        _flash_update(sw_ref.at[g], kvn_ref[0, 6 + g, pl.ds(win_start, WIN_SPAN), :],
                      None, accw_ref.at[g], pw_ref.at[g], win_valid)

    def slc_pair(i, carry):
        slc_scores(2 * i + 1, sb_ref)
        slc_update(sa_ref, pa_ref, 2 * i, False)
        slc_scores(2 * i + 2, sa_ref)
        slc_update(sb_ref, pb_ref, 2 * i + 1, False)
        return carry
    lax.fori_loop(0, last // 2, slc_pair, 0)
    tail = 2 * (last // 2)

    @pl.when(last > tail)
    def _():
        slc_scores(tail + 1, sb_ref)
        slc_update(sa_ref, pa_ref, tail, False)
        slc_update(sb_ref, pb_ref, tail + 1, True)

    @pl.when(last == tail)
    def _():
        slc_update(sa_ref, pa_ref, tail, True)

    sig = 1.0 / (1.0 + jnp.exp(-misc_ref[...]))
    low = lax.broadcasted_iota(jnp.int32, (NSA_TQ, LANES), 1) < NSA_DH
    denom_lane = jnp.where(low, ONES_LANE, ONES_LANE_G1)

    def pair(ref, rs):
        a0, a1 = ref[0, rs, :], ref[1, rs, :]
        return jnp.where(low, a0, a1), jnp.where(low, a1, a0)

    outs = []
    for r in range(NSA_REP):
        rs = slice(NSA_TQ * r, NSA_TQ * (r + 1))
        gate = [jnp.take_along_axis(
            sig, jnp.where(low, GATE_LANE0 + 3 * r + br, GATE_LANE0 + 3 * (NSA_REP + r) + br), axis=1)
            for br in range(3)]
        o_cmp, _ = pair(oc_ref, rs)
        o_slc, l_slc = pair(accs_ref, rs)
        o_win, l_win = pair(accw_ref, rs)
        outs.append(gate[0] * o_cmp
                    + (gate[1] / jnp.take_along_axis(l_slc, denom_lane, axis=1)) * o_slc
                    + (gate[2] / jnp.take_along_axis(l_win, denom_lane, axis=1)) * o_win)
    o_ref[0] = jnp.concatenate(outs, axis=1).astype(BF16)


def _nsa(qn, kvn, kvcmp, misc, cov_t, B, S):
    nblk = S // SLC_LEN
    ncmp = S // CMP_STRIDE
    nq = S // NSA_TQ
    rows = NSA_ROWS
    return pl.pallas_call(
        _nsa_body,
        grid=(B, nq),
        in_specs=[
            pl.BlockSpec((1, NSA_HEADS, NSA_TQ, LANES), lambda b, c: (b, 0, c, 0)),
            pl.BlockSpec((1, 8, S, LANES), lambda b, c: (b, 0, 0, 0)),
            pl.BlockSpec((4, 1, ncmp, LANES), lambda b, c: (0, b, 0, 0)),
            pl.BlockSpec((NSA_TQ, LANES), lambda b, c: (b * nq + c, 0)),
            pl.BlockSpec(cov_t.shape, lambda b, c: (0, 0)),
        ],
        out_specs=pl.BlockSpec((1, NSA_TQ, NSA_HEADS * NSA_DH), lambda b, c: (b, c, 0)),
        out_shape=jax.ShapeDtypeStruct((B, S, NSA_HEADS * NSA_DH), BF16),
        scratch_shapes=[
            pltpu.VMEM((NSA_GROUPS, S, 2 * LANES), BF16),
            pltpu.VMEM((nblk, NSA_GROUPS * NSA_TQ), F32),
            pltpu.VMEM((NSA_GROUPS * NSA_TQ, LANES), BF16),
            pltpu.VMEM((NSA_GROUPS, rows, SLC_CHUNK), F32),
            pltpu.VMEM((NSA_GROUPS, rows, SLC_CHUNK), F32),
            pltpu.VMEM((NSA_GROUPS, rows, SLC_CHUNK), BF16),
            pltpu.VMEM((NSA_GROUPS, rows, SLC_CHUNK), BF16),
            pltpu.VMEM((NSA_GROUPS, rows, LANES), F32),
            pltpu.VMEM((NSA_GROUPS, rows, LANES), F32),
            pltpu.VMEM((NSA_GROUPS, rows, WIN_SPAN), F32),
            pltpu.VMEM((NSA_GROUPS, rows, WIN_SPAN), BF16),
            pltpu.VMEM((NSA_GROUPS, rows, LANES), F32),
            pltpu.VMEM((NSA_GROUPS, rows, ncmp), F32),
            pltpu.VMEM((NSA_GROUPS, rows, ncmp), F32),
            pltpu.VMEM((NSA_GROUPS, rows, ncmp), BF16),
            pltpu.VMEM((NSA_GROUPS, rows, LANES), F32),
        ],
        compiler_params=_params("arbitrary", "arbitrary"),
        name="nsa",
    )(qn, kvn, kvcmp, misc, cov_t)


MLA_TQ = 512
MLA_CHUNK = 512
MLA_HPB = 4


def _mla_body(q_ref, k_ref, v_ref, o_ref, sa_ref, sb_ref, pa_ref, pb_ref, m_ref, acc_ref):
    qi = pl.program_id(2)
    t_row = qi * MLA_TQ + lax.broadcasted_iota(jnp.int32, (MLA_TQ, 1), 0)
    _flash_reset(m_ref, acc_ref)

    def scores(kc, dst):
        k0 = pl.multiple_of(kc * MLA_CHUNK, MLA_CHUNK)
        for j in range(MLA_HPB):
            dst[j] = _dot_nt(q_ref[0, j], k_ref[0, j, pl.ds(k0, MLA_CHUNK), :])

    def update(src, p_ref, kc, causal):
        k0 = pl.multiple_of(kc * MLA_CHUNK, MLA_CHUNK)
        mask = None
        if causal:
            mask = (k0 + lax.broadcasted_iota(jnp.int32, (1, MLA_CHUNK), 1)) <= t_row
        for j in range(MLA_HPB):
            _flash_update(src.at[j], v_ref[0, j, pl.ds(k0, MLA_CHUNK), :],
                          m_ref.at[j], acc_ref.at[j], p_ref.at[j], mask)

    scores(0, sa_ref)

    def pair(i, carry):
        scores(2 * i + 1, sb_ref)
        update(sa_ref, pa_ref, 2 * i, False)
        scores(2 * i + 2, sa_ref)
        update(sb_ref, pb_ref, 2 * i + 1, False)
        return carry
    lax.fori_loop(0, qi // 2, pair, 0)
    tail = 2 * (qi // 2)

    @pl.when(qi > tail)
    def _():
        scores(tail + 1, sb_ref)
        update(sa_ref, pa_ref, tail, False)
        update(sb_ref, pb_ref, tail + 1, True)

    @pl.when(qi == tail)
    def _():
        update(sa_ref, pa_ref, tail, True)

    low = lax.broadcasted_iota(jnp.int32, (MLA_TQ, LANES), 1) < MLA_V
    denom_lane = jnp.where(low, ONES_LANE, ONES_LANE_G1)
    outs = []
    for j in range(0, MLA_HPB, 2):
        a0, a1 = acc_ref[j], acc_ref[j + 1]
        denom = jnp.take_along_axis(jnp.where(low, a1, a0), denom_lane, axis=1)
        outs.append(jnp.where(low, a0, a1) * (1.0 / denom))
    o_ref[0] = jnp.concatenate(outs, axis=1).astype(BF16)


def _mla(q, k, v, B, S):
    return pl.pallas_call(
        _mla_body,
        grid=(B, MLA_HEADS // MLA_HPB, S // MLA_TQ),
        in_specs=[
            pl.BlockSpec((1, MLA_HPB, MLA_TQ, LANES), lambda b, h, i: (b, h, i, 0)),
            pl.BlockSpec((1, MLA_HPB, S, LANES), lambda b, h, i: (b, h, 0, 0)),
            pl.BlockSpec((1, MLA_HPB, S, LANES), lambda b, h, i: (b, h, 0, 0)),
        ],
        out_specs=pl.BlockSpec((1, MLA_TQ, MLA_HPB * MLA_V), lambda b, h, i: (b, i, h)),
        out_shape=jax.ShapeDtypeStruct((B, S, MLA_HEADS * MLA_V), BF16),
        scratch_shapes=[
            pltpu.VMEM((MLA_HPB, MLA_TQ, MLA_CHUNK), F32),
            pltpu.VMEM((MLA_HPB, MLA_TQ, MLA_CHUNK), F32),
            pltpu.VMEM((MLA_HPB, MLA_TQ, MLA_CHUNK), BF16),
            pltpu.VMEM((MLA_HPB, MLA_TQ, MLA_CHUNK), BF16),
            pltpu.VMEM((MLA_HPB, MLA_TQ, LANES), F32),
            pltpu.VMEM((MLA_HPB, MLA_TQ, LANES), F32),
        ],
        compiler_params=_params("arbitrary", "arbitrary", "arbitrary"),
        name="mla",
    )(q, k, v)


MEM_TM = 1024


def _mem_kv_body(m_ref, wk_ref, wv_ref, k_ref, v_ref):
    mb = m_ref[...].astype(BF16)
    k_ref[...] = _dot(mb, wk_ref[...]).astype(BF16)
    v_ref[...] = _dot(mb, wv_ref[...]).astype(BF16)


def _mem_kv(memf, wk, wv, tm):
    R, D = memf.shape
    tok = lambda i: (i, 0)
    const = lambda i: (0, 0)
    return pl.pallas_call(
        _mem_kv_body,
        grid=(R // tm,),
        in_specs=[pl.BlockSpec((tm, D), tok), pl.BlockSpec(wk.shape, const),
                  pl.BlockSpec(wv.shape, const)],
        out_specs=[pl.BlockSpec((tm, D), tok), pl.BlockSpec((tm, D), tok)],
        out_shape=[jax.ShapeDtypeStruct((R, D), BF16), jax.ShapeDtypeStruct((R, D), BF16)],
        compiler_params=_params("arbitrary"),
        name="mem_kv",
    )(memf, wk, wv)


def _mem_attn_body(x0_ref, on_ref, om_ref, wmix_ref, g1_ref, b1_ref,
                   k_ref, v_ref, wq_ref, wo_ref, g_ref, b_ref, o_ref, *, alpha):
    half = on_ref.shape[1]
    mix = _dot(on_ref[...], wmix_ref[:half, :]) + _dot(om_ref[...], wmix_ref[half:, :])
    x = _layer_norm(alpha * x0_ref[...] + mix, g1_ref[...], b1_ref[...])
    D = x.shape[1]
    dh = D // MEM_HEADS
    q = (_dot(x.astype(BF16), wq_ref[...]) * (dh ** -0.5 * LOG2E)).astype(BF16)
    outs = []
    for h in range(MEM_HEADS):
        cs = slice(dh * h, dh * (h + 1))
        s = _dot_nt(q[:, cs], k_ref[0, :, cs])
        p = jnp.exp2(s - jnp.max(s, axis=1, keepdims=True))
        l = jnp.sum(p, axis=1, keepdims=True)
        outs.append((_dot(p.astype(BF16), v_ref[0, :, cs]) * (1.0 / l)).astype(BF16))
    o = jnp.concatenate(outs, axis=1)
    y = _dot(o, wo_ref[...])
    o_ref[...] = _layer_norm(alpha * x + y, g_ref[...], b_ref[...])


def _mem_attn(xf, o_nsa, o_mla, w_o, g1, b1, k_mem, v_mem, wq, wo, g, b, alpha, S, tm):
    T, D = xf.shape
    nst = S // tm
    M = k_mem.shape[1]
    tok = lambda i: (i, 0)
    const = lambda i: (0, 0)
    memb = lambda i: (i // nst, 0, 0)
    return pl.pallas_call(
        functools.partial(_mem_attn_body, alpha=alpha),
        grid=(T // tm,),
        in_specs=[
            pl.BlockSpec((tm, D), tok),
            pl.BlockSpec((tm, o_nsa.shape[1]), tok),
            pl.BlockSpec((tm, o_mla.shape[1]), tok),
            pl.BlockSpec(w_o.shape, const),
            pl.BlockSpec((1, D), const),
            pl.BlockSpec((1, D), const),
            pl.BlockSpec((1, M, D), memb),
            pl.BlockSpec((1, M, D), memb),
            pl.BlockSpec(wq.shape, const),
            pl.BlockSpec(wo.shape, const),
            pl.BlockSpec((1, D), const),
            pl.BlockSpec((1, D), const),
        ],
        out_specs=pl.BlockSpec((tm, D), tok),
        out_shape=jax.ShapeDtypeStruct((T, D), F32),
        compiler_params=_params("arbitrary"),
        name="mem_attn",
    )(xf, o_nsa, o_mla, w_o, g1, b1, k_mem, v_mem, wq, wo, g, b)


HALO = 16


FFN_SLAB = 256
FFN_TM = 512


def _ffn_body(x_ref, xh_ref, wg_ref, wu_ref, cw_ref, cb_ref, wd_ref, g_ref, b_ref, o_ref,
              act_ref, *, alpha, seq_tiles):
    i = pl.program_id(0)
    x = x_ref[...]
    xb = x.astype(BF16)
    xe = jnp.concatenate([xh_ref[...].astype(BF16), xb], axis=0)
    tm = x.shape[0]
    row = lax.broadcasted_iota(jnp.int32, (tm, 1), 0)
    seq_start = i % seq_tiles == 0
    for c0 in range(0, wg_ref.shape[1], FFN_SLAB):
        cs = slice(c0, c0 + FFN_SLAB)
        gate_e = _dot(xe, wg_ref[:, cs])
        gate = gate_e[HALO:]
        up = _dot(xb, wu_ref[:, cs])
        halo = jnp.where(seq_start, 0.0, gate_e[:HALO])
        g1 = jnp.where(row == 0, halo[HALO - 1:HALO], pltpu.roll(gate, 1, 0))
        g2 = jnp.where(row == 0, halo[HALO - 2:HALO - 1],
                       jnp.where(row == 1, halo[HALO - 1:HALO], pltpu.roll(gate, 2, 0)))
        conv = cw_ref[0:1, cs] * g2 + cw_ref[1:2, cs] * g1 + cw_ref[2:3, cs] * gate + cb_ref[:, cs]
        act_ref[:, cs] = (conv * (1.0 / (1.0 + jnp.exp(-conv))) * up).astype(BF16)
    y = _dot(act_ref[...], wd_ref[...])
    o_ref[...] = _layer_norm(alpha * x + y, g_ref[...], b_ref[...])


def _ffn(xf, wg, wu, cw, cb, wd, g, b, alpha, S, tm):
    T, D = xf.shape
    dff = wg.shape[1]
    assert dff % FFN_SLAB == 0
    tok = lambda i: (i, 0)
    const = lambda i: (0, 0)
    return pl.pallas_call(
        functools.partial(_ffn_body, alpha=alpha, seq_tiles=S // tm),
        grid=(T // tm,),
        in_specs=[
            pl.BlockSpec((tm, D), tok),
            pl.BlockSpec((HALO, D), lambda i: (jnp.maximum(i * (tm // HALO) - 1, 0), 0)),
            pl.BlockSpec((D, dff), const),
            pl.BlockSpec((D, dff), const),
            pl.BlockSpec((CONV_WIDTH, dff), const),
            pl.BlockSpec((1, dff), const),
            pl.BlockSpec((dff, D), const),
            pl.BlockSpec((1, D), const),
            pl.BlockSpec((1, D), const),
        ],
        out_specs=pl.BlockSpec((tm, D), tok),
        out_shape=jax.ShapeDtypeStruct((T, D), F32),
        scratch_shapes=[pltpu.VMEM((tm, dff), BF16)],
        compiler_params=_params("arbitrary"),
        name="ffn",
    )(xf, xf, wg, wu, cw, cb, wd, g, b)


def _inv_freq_row(dim, lane_lo, lane_hi, period):
    inv = ROPE_THETA ** (-np.arange(0, dim, 2, dtype=np.float64) / dim)
    row = np.zeros((1, LANES), np.float32)
    for lane in range(lane_lo, lane_hi):
        row[0, lane] = inv[(lane % period) % (dim // 2)]
    return jnp.asarray(row)


def _cover_t(S):
    nc = S // CMP_STRIDE
    ns = S // SLC_LEN
    cs = np.arange(nc)[:, None] * CMP_STRIDE
    ss = np.arange(ns)[None, :] * SLC_LEN
    cover = np.clip(np.minimum(cs + CMP_LEN, ss + SLC_LEN) - np.maximum(cs, ss), 0, None) / CMP_LEN
    cover[nc - 1:] = 0.0
    return jnp.asarray(cover.T, dtype=BF16)


def _permute_w_in(w):
    D = w.shape[0]
    c1 = NSA_HEADS * NSA_DH
    c2 = c1 + 3 * 2 * NSA_GROUPS * NSA_DH
    c3 = c2 + 3 * NSA_HEADS
    c4 = c3 + MLA_Q_RANK
    c5 = c4 + MLA_KV_RANK
    c6 = c5 + MLA_ROPE
    half = MLA_ROPE // 2
    z = lambda n: jnp.zeros((D, n), w.dtype)
    misc = jnp.concatenate(
        [w[:, c5 + half:c6], z(GATE_LANE0 - half), w[:, c2:c3], z(MLA_PE1 - GATE_LANE0 - (c3 - c2)),
         w[:, c5:c5 + half], z(LANES - MLA_PE1 - half)], axis=1)
    return jnp.concatenate([w[:, :c2], w[:, c3:c5], misc], axis=1).astype(BF16)


def _mla_head_lanes(nope, pe):
    r, H, _ = nope.shape
    half = MLA_ROPE // 2
    split = MLA_PE1 - half
    pad = jnp.zeros((r, H, LANES - MLA_NOPE - MLA_ROPE), nope.dtype)
    return jnp.concatenate([pe[..., half:], nope[..., :split], pe[..., :half], nope[..., split:], pad],
                           axis=2).reshape(r, H * LANES)


def _permute_w_uq(w):
    r = w.shape[0]
    w3 = w.reshape(r, MLA_HEADS, MLA_NOPE + MLA_ROPE)
    return _mla_head_lanes(w3[..., :MLA_NOPE], w3[..., MLA_NOPE:]).astype(BF16)


def _permute_w_ukv(w):
    r = w.shape[0]
    w3 = w.reshape(r, MLA_HEADS, MLA_NOPE + MLA_V)
    k = _mla_head_lanes(w3[..., :MLA_NOPE], jnp.zeros((r, MLA_HEADS, MLA_ROPE), w.dtype))
    v = _pad_lanes(w3[..., MLA_NOPE:]).reshape(r, MLA_HEADS // 2, 2, LANES)
    v = jnp.stack([v[:, :, 0], jnp.roll(v[:, :, 1], LANES - MLA_V, axis=-1)], axis=2)
    return jnp.concatenate([k, v.reshape(r, MLA_HEADS * LANES)], axis=1).astype(BF16)


def _permute_w_o(w):
    n = NSA_HEADS * NSA_DH
    wn = w[:n].reshape(NSA_GROUPS, NSA_REP, NSA_DH, -1).transpose(1, 0, 2, 3).reshape(n, -1)
    return jnp.concatenate([wn, w[n:]], axis=0).astype(BF16)


def _pad_lanes(a):
    return jnp.concatenate([a, jnp.zeros(a.shape[:-1] + (LANES - a.shape[-1],), a.dtype)], axis=-1)


def kernel(x, mem, positions, w_in, nsa_k_pos, nsa_ck_w1, nsa_ck_b1, nsa_ck_w2, nsa_ck_b2,
           nsa_v_pos, nsa_cv_w1, nsa_cv_b1, nsa_cv_w2, nsa_cv_b2,
           mla_q_norm, mla_w_uq, mla_kv_norm, mla_w_ukv, w_o, ln1_g, ln1_b,
           mem_wq, mem_wk, mem_wv, mem_wo, ln2_g, ln2_b,
           ffn_w_up, ffn_conv_w, ffn_conv_b, ffn_w_down, ln3_g, ln3_b):
    B, S, D = x.shape
    T = B * S
    depth = w_in.shape[0]
    alpha = (2.0 * depth) ** 0.25
    d_ff = ffn_w_down.shape[1]
    tm = min(512, S)
    assert S % MLA_TQ == 0 and S >= WIN_SPAN and S % tm == 0
    assert (B * mem.shape[1]) % 256 == 0

    pos = positions.reshape(T, 1)
    pos_cmp = positions[:, CMP_LEN - 1::CMP_STRIDE]
    pos_cmp = jnp.concatenate([pos_cmp, pos_cmp[:, -1:]], axis=1)[:, :, None]
    inv_cmp = _inv_freq_row(NSA_DH, 0, NSA_DH, NSA_DH)
    inv_tok = (_inv_freq_row(NSA_DH, 0, NSA_DH // 2, NSA_DH)
               + _inv_freq_row(MLA_ROPE, NSA_DH // 2, NSA_DH // 2 + MLA_ROPE // 2, MLA_ROPE // 2))
    cov_t = _cover_t(S)
    memf = mem.reshape(B * mem.shape[1], D)

    xf = x.reshape(T, D)
    for l in range(depth):
        qn, kvn, kvc, misc, q_m, k_m, v_m = _inproj(
            pos, inv_tok, xf, _permute_w_in(w_in[l]), mla_q_norm[l][None, :],
            mla_kv_norm[l][None, :], _permute_w_uq(mla_w_uq[l]), _permute_w_ukv(mla_w_ukv[l]),
            B, S, tm)
        kvcmp = _compress(
            pos_cmp, inv_cmp, kvc,
            *_compress_weights(nsa_k_pos[l], nsa_ck_w1[l], nsa_ck_b1[l], nsa_ck_w2[l], nsa_ck_b2[l],
                               nsa_v_pos[l], nsa_cv_w1[l], nsa_cv_b1[l], nsa_cv_w2[l], nsa_cv_b2[l]),
            B, S)
        o_nsa = _nsa(qn, kvn, kvcmp, misc, cov_t, B, S)
        o_mla = _mla(q_m, k_m, v_m, B, S)
        k_mem, v_mem = _mem_kv(memf, mem_wk[l].astype(BF16), mem_wv[l].astype(BF16), 256)
        xf = _mem_attn(xf, o_nsa.reshape(T, -1), o_mla.reshape(T, -1), _permute_w_o(w_o[l]),
                       ln1_g[l][None, :], ln1_b[l][None, :],
                       k_mem.reshape(B, -1, D), v_mem.reshape(B, -1, D),
                       mem_wq[l].astype(BF16), mem_wo[l].astype(BF16),
                       ln2_g[l][None, :], ln2_b[l][None, :], alpha, S, min(MEM_TM, S))
        xf = _ffn(xf, ffn_w_up[l][:, :d_ff].astype(BF16), ffn_w_up[l][:, d_ff:].astype(BF16),
                  ffn_conv_w[l], ffn_conv_b[l][None, :], ffn_w_down[l].astype(BF16),
                  ln3_g[l][None, :], ln3_b[l][None, :], alpha, S, FFN_TM)
    return xf.reshape(B, S, D)
```

```python
import functools
import math

import numpy as np
import jax
import jax.numpy as jnp
from jax import lax
from jax.experimental import pallas as pl
from jax.experimental.pallas import tpu as pltpu

F32 = jnp.float32
BF16 = jnp.bfloat16

NSA_HEADS = 8
NSA_GROUPS = 2
NSA_REP = NSA_HEADS // NSA_GROUPS
NSA_DH = 64
CMP_STRIDE = 16
CMP_LEN = 32
SLC_LEN = 64
TOPN = 16
WINDOW = 512
CMP_HIDDEN = 128
FORCE_BONUS = 1e4
MLA_HEADS = 8
MLA_Q_RANK = 384
MLA_KV_RANK = 256
MLA_NOPE = 64
MLA_ROPE = 32
MLA_V = 64
MEM_HEADS = 4
CONV_WIDTH = 3
ROPE_THETA = 10000.0
LN_EPS = 1e-5
RMS_EPS = 1e-6
NEG_INF = -1e30
LOG2E = math.log2(math.e)

LANES = 128
VMEM_LIMIT = 60 * 1024 * 1024

C_Q = 0
C_KVC = 512
C_KVN = 768
C_LAT = 1280
C_MISC = 1920
IN_COLS_PAD = 2048
GATE_LANE0 = MLA_ROPE
MLA_PE2 = 0
MLA_PE1 = 64
ONES_LANE = 64
ONES_LANE_G1 = 0


def _dot(a, b):
    return jnp.dot(a, b, preferred_element_type=F32)


def _dot_nt(a, b):
    return lax.dot_general(a, b, (((1,), (1,)), ((), ())), preferred_element_type=F32)


def _layer_norm(y, g, b):
    mu = jnp.mean(y, axis=-1, keepdims=True)
    d = y - mu
    var = jnp.mean(d * d, axis=-1, keepdims=True)
    return d * lax.rsqrt(var + LN_EPS) * g + b


def _params(*sem):
    return pltpu.CompilerParams(dimension_semantics=sem, vmem_limit_bytes=VMEM_LIMIT)


def _rope_tables(pos_col, inv_row, half):
    ang = pos_col * inv_row
    cos = jnp.cos(ang)
    sin = jnp.sin(ang)
    lane = lax.broadcasted_iota(jnp.int32, (1, LANES), 1)
    upper = (lane & (2 * half - 1)) >= half
    rot = inv_row != 0.0
    sin_hi = jnp.where(upper & rot, sin, 0.0)
    sin_lo = jnp.where(upper | (~rot), 0.0, -sin)
    return cos, sin_hi, sin_lo


def _apply_rope(v, tabs, half):
    cos, sin_hi, sin_lo = tabs
    return v * cos + pltpu.roll(v, half, 1) * sin_hi + pltpu.roll(v, LANES - half, 1) * sin_lo


def _rms_norm(v, g):
    return v * lax.rsqrt(jnp.mean(v * v, axis=-1, keepdims=True) + RMS_EPS) * g


def _inproj_body(pos_ref, inv_ref, x_ref, w_ref, gq_ref, gkv_ref, wq_ref, wkv_ref,
                 qn_ref, kvn_ref, kvc_ref, misc_ref, qm_ref, km_ref, vm_ref):
    xb = x_ref[...].astype(BF16)
    lane = lax.broadcasted_iota(jnp.int32, (1, LANES), 1)
    low = lane < NSA_DH
    qscale = NSA_DH ** -0.5 * LOG2E

    ang = pos_ref[...].astype(F32) * inv_ref[...]
    cos, sin = jnp.cos(ang), jnp.sin(ang)
    nf, mf = NSA_DH // 2, MLA_ROPE // 2

    def tile_nsa(t):
        t = jnp.where(lane < nf, t, 0.0)
        t = t + pltpu.roll(t, nf, 1)
        return t + pltpu.roll(t, 2 * nf, 1)

    def place_mla(t):
        t = jnp.where((lane >= nf) & (lane < nf + mf), t, 0.0)
        return pltpu.roll(t, MLA_PE1 - nf, 1) + pltpu.roll(t, LANES + MLA_PE2 - nf, 1)

    upper = (lane & (NSA_DH - 1)) >= nf
    sin_n = tile_nsa(sin)
    tabs = (tile_nsa(cos), jnp.where(upper, sin_n, 0.0), jnp.where(upper, 0.0, -sin_n))
    pe1 = (lane >= MLA_PE1) & (lane < MLA_PE1 + mf)
    pe2 = (lane >= MLA_PE2) & (lane < MLA_PE2 + mf)
    pe_lanes = pe1 | pe2
    sin_m = place_mla(sin)
    cos_m = jnp.where(pe_lanes, place_mla(cos), 1.0)
    sin_m = jnp.where(pe1, -sin_m, jnp.where(pe2, sin_m, 0.0))

    def rope_mla(v):
        return v * cos_m + pltpu.roll(v, LANES // 2, 1) * sin_m

    def proj(c0, n):
        return _dot(xb, w_ref[:, c0:c0 + n])

    def split_store(v, ref, idx_lo, idx_hi, pad=0.0):
        ref[0, idx_lo] = jnp.where(low, v, pad).astype(BF16)
        ref[0, idx_hi] = jnp.where(low, pltpu.roll(v, NSA_DH, 1), pad).astype(BF16)

    ones_pad = jnp.where(lane == ONES_LANE, 1.0, 0.0)
    ones_pad_g1 = jnp.where(lane == ONES_LANE_G1, 1.0, 0.0)

    for slab in range(2):
        h = proj(C_Q + 256 * slab, 256)
        for j in range(2):
            r = _apply_rope(h[:, LANES * j:LANES * (j + 1)], tabs, NSA_DH // 2) * qscale
            split_store(r, qn_ref, 4 * slab + 2 * j, 4 * slab + 2 * j + 1)

    h = proj(C_KVC, 256)
    kvc_ref[0] = h[:, :LANES]
    kvc_ref[1] = h[:, LANES:]

    for slab in range(2):
        h = proj(C_KVN + 256 * slab, 256)
        k = _apply_rope(h[:, :LANES], tabs, NSA_DH // 2)
        split_store(k, kvn_ref, 4 * slab, 4 * slab + 1)
        v = h[:, LANES:]
        kvn_ref[0, 4 * slab + 2] = jnp.where(low, v, ones_pad).astype(BF16)
        kvn_ref[0, 4 * slab + 3] = jnp.where(low, ones_pad_g1, v).astype(BF16)

    lat = [proj(C_LAT + 256 * i, 256) for i in range((IN_COLS_PAD - C_LAT) // 256)]
    misc = lat[2][:, LANES:]
    misc_ref[...] = misc
    mq = jnp.concatenate([lat[0], lat[1][:, :LANES]], axis=1)
    mkv = jnp.concatenate([lat[1][:, LANES:], lat[2][:, :LANES]], axis=1)
    mscale = (MLA_NOPE + MLA_ROPE) ** -0.5 * LOG2E
    qn = _rms_norm(mq, gq_ref[...]).astype(BF16)
    kvn = _rms_norm(mkv, gkv_ref[...]).astype(BF16)
    kpe = jnp.where(pe_lanes, rope_mla(misc), 0.0)
    kcols = MLA_HEADS * LANES
    for slab in range(MLA_HEADS // 2):
        cs = slice(256 * slab, 256 * (slab + 1))
        hq = _dot(qn, wq_ref[:, cs])
        hk = _dot(kvn, wkv_ref[:, cs])
        hv = _dot(kvn, wkv_ref[:, kcols + 256 * slab:kcols + 256 * (slab + 1)])
        for j in range(2):
            ls = slice(LANES * j, LANES * (j + 1))
            qm_ref[0, 2 * slab + j] = (rope_mla(hq[:, ls]) * mscale).astype(BF16)
            km_ref[0, 2 * slab + j] = (hk[:, ls] + kpe).astype(BF16)
            vm_ref[0, 2 * slab + j] = (hv[:, ls] + (ones_pad_g1 if j else ones_pad)).astype(BF16)


def _inproj(pos, inv_row, xf, w_in_p, gq, gkv, wq_p, wkv_p, B, S, tm):
    T = B * S
    nst = S // tm
    tok = lambda i: (i, 0)
    const = lambda i: (0, 0)
    head_blk = lambda i: (i // nst, 0, i % nst, 0)
    heads = jax.ShapeDtypeStruct((B, 8, S, LANES), BF16)
    return pl.pallas_call(
        _inproj_body,
        grid=(T // tm,),
        in_specs=[
            pl.BlockSpec((tm, 1), tok),
            pl.BlockSpec((1, LANES), const),
            pl.BlockSpec((tm, xf.shape[1]), tok),
            pl.BlockSpec(w_in_p.shape, const),
            pl.BlockSpec(gq.shape, const),
            pl.BlockSpec(gkv.shape, const),
            pl.BlockSpec(wq_p.shape, const),
            pl.BlockSpec(wkv_p.shape, const),
        ],
        out_specs=[
            pl.BlockSpec((1, 8, tm, LANES), head_blk),
            pl.BlockSpec((1, 8, tm, LANES), head_blk),
            pl.BlockSpec((2, tm, LANES), lambda i: (0, i, 0)),
            pl.BlockSpec((tm, LANES), tok),
            pl.BlockSpec((1, 8, tm, LANES), head_blk),
            pl.BlockSpec((1, 8, tm, LANES), head_blk),
            pl.BlockSpec((1, 8, tm, LANES), head_blk),
        ],
        out_shape=[
            heads,
            heads,
            jax.ShapeDtypeStruct((2, T, LANES), F32),
            jax.ShapeDtypeStruct((T, LANES), F32),
            heads,
            heads,
            heads,
        ],
        compiler_params=_params("arbitrary"),
        name="inproj",
    )(pos, inv_row, xf, w_in_p, gq, gkv, wq_p, wkv_p)


def _compress_body(pos_ref, inv_ref, x_ref, pe_ref, w1_ref, b1_ref, w2_ref, b2_ref, o_ref):
    is_k = pl.program_id(0) == 0
    nch = o_ref.shape[2]
    a1 = jnp.zeros((nch, NSA_GROUPS * CMP_HIDDEN), F32)
    a2 = jnp.zeros((nch, NSA_GROUPS * CMP_HIDDEN), F32)
    for l in range(CMP_STRIDE):
        xl = x_ref.at[0, 0][pl.ds(l, nch, stride=CMP_STRIDE), :]
        a1 = a1 + _dot((xl + pe_ref[0, l:l + 1, :]).astype(BF16), w1_ref[0, l])
        a2 = a2 + _dot((xl + pe_ref[0, CMP_STRIDE + l:CMP_STRIDE + l + 1, :]).astype(BF16),
                       w1_ref[0, CMP_STRIDE + l])
    pre = a1 + pltpu.roll(a2, nch - 1, 0) + b1_ref[0]
    hid = jax.nn.gelu(pre, approximate=True)
    out = _dot(hid.astype(BF16), w2_ref[0]) + b2_ref[0]
    tabs = _rope_tables(pos_ref[0].astype(F32), inv_ref[...], NSA_DH // 2)
    row = lax.broadcasted_iota(jnp.int32, (nch, 1), 0)
    for g in range(NSA_GROUPS):
        og = out[:, LANES * g:LANES * (g + 1)]
        og = jnp.where(is_k, _apply_rope(og, tabs, NSA_DH // 2), og)
        o_ref[g, 0] = jnp.where(row < nch - 1, og, 0.0).astype(BF16)


def _compress(pos_cmp, inv_nsa, kvc, pe, w1, b1, w2, b2, B, S):
    nch = S // CMP_STRIDE
    x = kvc.reshape(2, B, S, LANES)
    kv = lambda j, b: (j, 0, 0)
    kv4 = lambda j, b: (j, 0, 0, 0)
    return pl.pallas_call(
        _compress_body,
        grid=(2, B),
        in_specs=[
            pl.BlockSpec((1, nch, 1), lambda j, b: (b, 0, 0)),
            pl.BlockSpec((1, LANES), lambda j, b: (0, 0)),
            pl.BlockSpec((1, 1, S, LANES), lambda j, b: (j, b, 0, 0)),
            pl.BlockSpec((1,) + pe.shape[1:], kv),
            pl.BlockSpec((1,) + w1.shape[1:], kv4),
            pl.BlockSpec((1,) + b1.shape[1:], kv),
            pl.BlockSpec((1,) + w2.shape[1:], kv),
            pl.BlockSpec((1,) + b2.shape[1:], kv),
        ],
        out_specs=pl.BlockSpec((NSA_GROUPS, 1, nch, LANES), lambda j, b: (j, b, 0, 0)),
        out_shape=jax.ShapeDtypeStruct((2 * NSA_GROUPS, B, nch, LANES), BF16),
        compiler_params=_params("arbitrary", "arbitrary"),
        name="compress",
    )(pos_cmp, inv_nsa, x, pe, w1, b1, w2, b2)


def _compress_weights(k_pos, k_w1, k_b1, k_w2, k_b2, v_pos, v_w1, v_b1, v_w2, v_b2):
    def one(pos, w1, b1, w2, b2, g1_high):
        w1l = w1.reshape(CMP_LEN, NSA_DH, CMP_HIDDEN)
        z1 = jnp.zeros_like(w1l)
        w1bd = jnp.concatenate([jnp.concatenate([w1l, z1], axis=2),
                                jnp.concatenate([z1, w1l], axis=2)], axis=1)
        pad = lambda a: _pad_lanes(a)
        pad1 = (lambda a: jnp.roll(_pad_lanes(a), LANES - a.shape[-1], axis=-1)) if g1_high else pad
        z2 = jnp.zeros_like(pad(w2))
        w2bd = jnp.concatenate([jnp.concatenate([pad(w2), z2], axis=1),
                                jnp.concatenate([z2, pad1(w2)], axis=1)], axis=0)
        return (jnp.tile(pos, (1, NSA_GROUPS)), w1bd.astype(BF16), jnp.tile(b1, NSA_GROUPS)[None, :],
                w2bd.astype(BF16), jnp.concatenate([pad(b2), pad1(b2)])[None, :])
    k = one(k_pos, k_w1, k_b1, k_w2, k_b2, False)
    v = one(v_pos, v_w1, v_b1, v_w2, v_b2, True)
    return tuple(jnp.stack([a, b]) for a, b in zip(k, v))


STRIP = 64


def _lane_tile(col, n):
    reps = [col] * (n // LANES)
    if n % LANES:
        reps.append(col[:, :n % LANES])
    return reps[0] if len(reps) == 1 else jnp.concatenate(reps, axis=1)


def _flash_reset(m_ref, acc_ref):
    m_ref[...] = jnp.full(m_ref.shape, NEG_INF, F32)
    acc_ref[...] = jnp.zeros(acc_ref.shape, F32)


def _flash_update(s_ref, v, m_ref, acc_ref, p_ref, mask=None):
    rows, n = s_ref.shape
    for r in range(rows // STRIP):
        rs = slice(STRIP * r, STRIP * (r + 1))
        s = s_ref[rs, :]
        if mask is not None:
            s = jnp.where(mask[rs], s, NEG_INF)
        if m_ref is None:
            p_ref[rs, :] = jnp.exp2(s - jnp.max(s, axis=1, keepdims=True)).astype(BF16)
            continue
        m_old = m_ref[rs, :]
        m_new = jnp.maximum(m_old, jnp.max(s, axis=1, keepdims=True))
        p_ref[rs, :] = jnp.exp2(s - _lane_tile(m_new, n)).astype(BF16)
        acc_ref[rs, :] = jnp.exp2(m_old - m_new) * acc_ref[rs, :]
        m_ref[rs, :] = m_new
    if m_ref is None:
        acc_ref[...] = _dot(p_ref[...], v)
    else:
        acc_ref[...] += _dot(p_ref[...], v)


NSA_TQ = 256
NSA_ROWS = NSA_REP * NSA_TQ
SLC_CHUNK = 512
WIN_SPAN = WINDOW + NSA_TQ
BIAS_LANE0 = LANES


def _nsa_body(q_ref, kvn_ref, kvc_ref, misc_ref, cov_ref, o_ref,
              kaug_ref, score_ref, qbias_ref, sa_ref, sb_ref, pa_ref, pb_ref, ms_ref, accs_ref,
              sw_ref, pw_ref, accw_ref, sc_ref, pn_ref, pc_ref, oc_ref):
    c = pl.program_id(1)
    rows = NSA_ROWS
    t_row = c * NSA_TQ + (lax.broadcasted_iota(jnp.int32, (rows, 1), 0) & (NSA_TQ - 1))
    ncmp = kvc_ref.shape[2]
    nblk = kaug_ref.shape[1] // SLC_LEN

    @pl.when(c == 0)
    def _():
        nkeys = kaug_ref.shape[1]
        key_blk = lax.broadcasted_iota(jnp.int32, (nkeys, LANES), 0) // SLC_LEN
        onehot = jnp.where(key_blk == lax.broadcasted_iota(jnp.int32, (nkeys, LANES), 1), 1.0, 0.0)
        for g in range(NSA_GROUPS):
            kaug_ref[g, :, :BIAS_LANE0] = kvn_ref[0, g]
            kaug_ref[g, :, BIAS_LANE0:] = onehot.astype(BF16)

    qs = [q_ref[0, NSA_REP * g:NSA_REP * (g + 1)].reshape(rows, LANES) for g in range(NSA_GROUPS)]

    cmp_valid = (CMP_STRIDE * lax.broadcasted_iota(jnp.int32, (1, ncmp), 1) + CMP_LEN - 1) <= t_row
    for g in range(NSA_GROUPS):
        sc_ref[g] = _dot_nt(qs[g], kvc_ref[g, 0])
    win_start = pl.multiple_of(jnp.maximum(c * NSA_TQ - WINDOW, 0), NSA_TQ)

    def window_scores(g):
        sw_ref[g] = _dot_nt(qs[g], kvn_ref[0, 4 + g, pl.ds(win_start, WIN_SPAN), :])

    window_scores(0)
    for g in range(NSA_GROUPS):
        for r in range(rows // STRIP):
            rs = slice(STRIP * r, STRIP * (r + 1))
            s = jnp.where(cmp_valid[rs], sc_ref[g, rs, :], NEG_INF)
            p = jnp.where(cmp_valid[rs], jnp.exp2(s - jnp.max(s, axis=1, keepdims=True)), 0.0)
            l = jnp.sum(p, axis=1, keepdims=True)
            p = p * jnp.where(l > 0.0, 1.0 / l, 0.0)
            pn_ref[g, rs, :] = p
            pc_ref[g, rs, :] = p.astype(BF16)
        oc_ref[g] = _dot(pc_ref[g], kvc_ref[NSA_GROUPS + g, 0])

    def select(nb):
        window_scores(1)
        width = NSA_GROUPS * NSA_TQ
        ps = jnp.concatenate(
            [sum(pn_ref[g, NSA_TQ * r:NSA_TQ * (r + 1), :] for r in range(NSA_REP))
             for g in range(NSA_GROUPS)], axis=0)
        hi = ps.astype(BF16)
        lo = (ps - hi.astype(F32)).astype(BF16)
        imp = _dot_nt(cov_ref[:nb, :], hi) + _dot_nt(cov_ref[:nb, :], lo)
        jidx = lax.broadcasted_iota(jnp.int32, (nb, width), 0)
        lane_q = lax.broadcasted_iota(jnp.int32, (1, width), 1) & (NSA_TQ - 1)
        cur = c * (NSA_TQ // SLC_LEN) + lane_q // SLC_LEN
        forced = (jidx == 0) | (jidx == cur) | (jidx == cur - 1)
        score = jnp.where(jidx <= cur, jnp.where(forced, FORCE_BONUS, imp), NEG_INF)
        score_ref[:nb, :] = score
        sub = 8
        cnt = [jnp.zeros((sub, width), F32) for _ in range(nb // sub)]
        tiles = [score[sub * v:sub * (v + 1)] for v in range(nb // sub)]
        sidx = lax.broadcasted_iota(jnp.int32, (sub, width), 0)
        for jp in range(nb):
            rowv = jnp.broadcast_to(score_ref[jp:jp + 1, :], (sub, width))
            for v in range(nb // sub):
                if sub * v > jp:
                    cnt[v] = jnp.where(rowv >= tiles[v], cnt[v] + 1.0, cnt[v])
                elif sub * v + sub - 1 <= jp:
                    cnt[v] = jnp.where(rowv > tiles[v], cnt[v] + 1.0, cnt[v])
                else:
                    ge = jnp.where(rowv >= tiles[v], cnt[v] + 1.0, cnt[v])
                    gt = jnp.where(rowv > tiles[v], cnt[v] + 1.0, cnt[v])
                    cnt[v] = jnp.where(sidx + sub * v > jp, ge, gt)
        rank = jnp.concatenate(cnt, axis=0)
        bias = jnp.where(rank < float(TOPN), 0.0, NEG_INF)
        bias = jnp.concatenate([bias, jnp.zeros((LANES - nb, width), F32)], axis=0)
        qbias_ref[...] = bias.T.astype(BF16)

    visible = (c + 1) * (NSA_TQ // SLC_LEN)

    @pl.when(visible <= TOPN)
    def _():
        window_scores(1)
        qbias_ref[...] = jnp.zeros(qbias_ref.shape, BF16)

    bounds = [TOPN] + list(range(TOPN + 8, nblk, 8)) + [nblk]
    for lo_nb, nb in zip(bounds[:-1], bounds[1:]):
        pl.when((visible > lo_nb) & (visible <= nb))(functools.partial(select, nb))

    _flash_reset(ms_ref, accs_ref)
    qas = [jnp.concatenate(
        [qs[g], jnp.concatenate([qbias_ref[NSA_TQ * g:NSA_TQ * (g + 1), :]] * NSA_REP, axis=0)],
        axis=1) for g in range(NSA_GROUPS)]
    last = c // (SLC_CHUNK // NSA_TQ)

    def slc_scores(kc, dst):
        k0 = pl.multiple_of(kc * SLC_CHUNK, SLC_CHUNK)
        for g in range(NSA_GROUPS):
            dst[g] = _dot_nt(qas[g], kaug_ref[g, pl.ds(k0, SLC_CHUNK), :])

    def slc_update(src, p_ref, kc, causal):
        k0 = pl.multiple_of(kc * SLC_CHUNK, SLC_CHUNK)
        mask = None
        if causal:
            mask = (k0 + lax.broadcasted_iota(jnp.int32, (1, SLC_CHUNK), 1)) <= t_row
        for g in range(NSA_GROUPS):
            _flash_update(src.at[g], kvn_ref[0, 2 + g, pl.ds(k0, SLC_CHUNK), :],
                          ms_ref.at[g], accs_ref.at[g], p_ref.at[g], mask)

    slc_scores(0, sa_ref)

    diff = t_row - (win_start + lax.broadcasted_iota(jnp.int32, (1, WIN_SPAN), 1))
    win_valid = lax.bitcast_convert_type(diff, jnp.uint32) < jnp.uint32(WINDOW)
    for g in range(NSA_GROUPS):
        _flash_update(sw_ref.at[g], kvn_ref[0, 6 + g, pl.ds(win_start, WIN_SPAN), :],
                      None, accw_ref.at[g], pw_ref.at[g], win_valid)

    def slc_pair(i, carry):
        slc_scores(2 * i + 1, sb_ref)
        slc_update(sa_ref, pa_ref, 2 * i, False)
        slc_scores(2 * i + 2, sa_ref)
        slc_update(sb_ref, pb_ref, 2 * i + 1, False)
        return carry
    lax.fori_loop(0, last // 2, slc_pair, 0)
    tail = 2 * (last // 2)

    @pl.when(last > tail)
    def _():
        slc_scores(tail + 1, sb_ref)
        slc_update(sa_ref, pa_ref, tail, False)
        slc_update(sb_ref, pb_ref, tail + 1, True)

    @pl.when(last == tail)
    def _():
        slc_update(sa_ref, pa_ref, tail, True)

    sig = 1.0 / (1.0 + jnp.exp(-misc_ref[...]))
    low = lax.broadcasted_iota(jnp.int32, (NSA_TQ, LANES), 1) < NSA_DH
    denom_lane = jnp.where(low, ONES_LANE, ONES_LANE_G1)

    def pair(ref, rs):
        a0, a1 = ref[0, rs, :], ref[1, rs, :]
        return jnp.where(low, a0, a1), jnp.where(low, a1, a0)

    outs = []
    for r in range(NSA_REP):
        rs = slice(NSA_TQ * r, NSA_TQ * (r + 1))
        gate = [jnp.take_along_axis(
            sig, jnp.where(low, GATE_LANE0 + 3 * r + br, GATE_LANE0 + 3 * (NSA_REP + r) + br), axis=1)
            for br in range(3)]
        o_cmp, _ = pair(oc_ref, rs)
        o_slc, l_slc = pair(accs_ref, rs)
        o_win, l_win = pair(accw_ref, rs)
        outs.append(gate[0] * o_cmp
                    + (gate[1] / jnp.take_along_axis(l_slc, denom_lane, axis=1)) * o_slc
                    + (gate[2] / jnp.take_along_axis(l_win, denom_lane, axis=1)) * o_win)
    o_ref[0] = jnp.concatenate(outs, axis=1).astype(BF16)


def _nsa(qn, kvn, kvcmp, misc, cov_t, B, S):
    nblk = S // SLC_LEN
    ncmp = S // CMP_STRIDE
    nq = S // NSA_TQ
    rows = NSA_ROWS
    return pl.pallas_call(
        _nsa_body,
        grid=(B, nq),
        in_specs=[
            pl.BlockSpec((1, NSA_HEADS, NSA_TQ, LANES), lambda b, c: (b, 0, c, 0)),
            pl.BlockSpec((1, 8, S, LANES), lambda b, c: (b, 0, 0, 0)),
            pl.BlockSpec((4, 1, ncmp, LANES), lambda b, c: (0, b, 0, 0)),
            pl.BlockSpec((NSA_TQ, LANES), lambda b, c: (b * nq + c, 0)),
            pl.BlockSpec(cov_t.shape, lambda b, c: (0, 0)),
        ],
        out_specs=pl.BlockSpec((1, NSA_TQ, NSA_HEADS * NSA_DH), lambda b, c: (b, c, 0)),
        out_shape=jax.ShapeDtypeStruct((B, S, NSA_HEADS * NSA_DH), BF16),
        scratch_shapes=[
            pltpu.VMEM((NSA_GROUPS, S, 2 * LANES), BF16),
            pltpu.VMEM((nblk, NSA_GROUPS * NSA_TQ), F32),
            pltpu.VMEM((NSA_GROUPS * NSA_TQ, LANES), BF16),
            pltpu.VMEM((NSA_GROUPS, rows, SLC_CHUNK), F32),
            pltpu.VMEM((NSA_GROUPS, rows, SLC_CHUNK), F32),
            pltpu.VMEM((NSA_GROUPS, rows, SLC_CHUNK), BF16),
            pltpu.VMEM((NSA_GROUPS, rows, SLC_CHUNK), BF16),
            pltpu.VMEM((NSA_GROUPS, rows, LANES), F32),
            pltpu.VMEM((NSA_GROUPS, rows, LANES), F32),
            pltpu.VMEM((NSA_GROUPS, rows, WIN_SPAN), F32),
            pltpu.VMEM((NSA_GROUPS, rows, WIN_SPAN), BF16),
            pltpu.VMEM((NSA_GROUPS, rows, LANES), F32),
            pltpu.VMEM((NSA_GROUPS, rows, ncmp), F32),
            pltpu.VMEM((NSA_GROUPS, rows, ncmp), F32),
            pltpu.VMEM((NSA_GROUPS, rows, ncmp), BF16),
            pltpu.VMEM((NSA_GROUPS, rows, LANES), F32),
        ],
        compiler_params=_params("arbitrary", "arbitrary"),
        name="nsa",
    )(qn, kvn, kvcmp, misc, cov_t)


MLA_TQ = 512
MLA_CHUNK = 512
MLA_HPB = 4


def _mla_body(q_ref, k_ref, v_ref, o_ref, sa_ref, sb_ref, pa_ref, pb_ref, m_ref, acc_ref):
    qi = pl.program_id(2)
    t_row = qi * MLA_TQ + lax.broadcasted_iota(jnp.int32, (MLA_TQ, 1), 0)
    _flash_reset(m_ref, acc_ref)

    def scores(kc, dst):
        k0 = pl.multiple_of(kc * MLA_CHUNK, MLA_CHUNK)
        for j in range(MLA_HPB):
            dst[j] = _dot_nt(q_ref[0, j], k_ref[0, j, pl.ds(k0, MLA_CHUNK), :])

    def update(src, p_ref, kc, causal):
        k0 = pl.multiple_of(kc * MLA_CHUNK, MLA_CHUNK)
        mask = None
        if causal:
            mask = (k0 + lax.broadcasted_iota(jnp.int32, (1, MLA_CHUNK), 1)) <= t_row
        for j in range(MLA_HPB):
            _flash_update(src.at[j], v_ref[0, j, pl.ds(k0, MLA_CHUNK), :],
                          m_ref.at[j], acc_ref.at[j], p_ref.at[j], mask)

    scores(0, sa_ref)

    def pair(i, carry):
        scores(2 * i + 1, sb_ref)
        update(sa_ref, pa_ref, 2 * i, False)
        scores(2 * i + 2, sa_ref)
        update(sb_ref, pb_ref, 2 * i + 1, False)
        return carry
    lax.fori_loop(0, qi // 2, pair, 0)
    tail = 2 * (qi // 2)

    @pl.when(qi > tail)
    def _():
        scores(tail + 1, sb_ref)
        update(sa_ref, pa_ref, tail, False)
        update(sb_ref, pb_ref, tail + 1, True)

    @pl.when(qi == tail)
    def _():
        update(sa_ref, pa_ref, tail, True)

    low = lax.broadcasted_iota(jnp.int32, (MLA_TQ, LANES), 1) < MLA_V
    denom_lane = jnp.where(low, ONES_LANE, ONES_LANE_G1)
    outs = []
    for j in range(0, MLA_HPB, 2):
        a0, a1 = acc_ref[j], acc_ref[j + 1]
        denom = jnp.take_along_axis(jnp.where(low, a1, a0), denom_lane, axis=1)
        outs.append(jnp.where(low, a0, a1) * (1.0 / denom))
    o_ref[0] = jnp.concatenate(outs, axis=1).astype(BF16)


def _mla(q, k, v, B, S):
    return pl.pallas_call(
        _mla_body,
        grid=(B, MLA_HEADS // MLA_HPB, S // MLA_TQ),
        in_specs=[
            pl.BlockSpec((1, MLA_HPB, MLA_TQ, LANES), lambda b, h, i: (b, h, i, 0)),
            pl.BlockSpec((1, MLA_HPB, S, LANES), lambda b, h, i: (b, h, 0, 0)),
            pl.BlockSpec((1, MLA_HPB, S, LANES), lambda b, h, i: (b, h, 0, 0)),
        ],
        out_specs=pl.BlockSpec((1, MLA_TQ, MLA_HPB * MLA_V), lambda b, h, i: (b, i, h)),
        out_shape=jax.ShapeDtypeStruct((B, S, MLA_HEADS * MLA_V), BF16),
        scratch_shapes=[
            pltpu.VMEM((MLA_HPB, MLA_TQ, MLA_CHUNK), F32),
            pltpu.VMEM((MLA_HPB, MLA_TQ, MLA_CHUNK), F32),
            pltpu.VMEM((MLA_HPB, MLA_TQ, MLA_CHUNK), BF16),
            pltpu.VMEM((MLA_HPB, MLA_TQ, MLA_CHUNK), BF16),
            pltpu.VMEM((MLA_HPB, MLA_TQ, LANES), F32),
            pltpu.VMEM((MLA_HPB, MLA_TQ, LANES), F32),
        ],
        compiler_params=_params("arbitrary", "arbitrary", "arbitrary"),
        name="mla",
    )(q, k, v)


MEM_TM = 1024


def _mem_kv_body(m_ref, wk_ref, wv_ref, k_ref, v_ref):
    mb = m_ref[...].astype(BF16)
    k_ref[...] = _dot(mb, wk_ref[...]).astype(BF16)
    v_ref[...] = _dot(mb, wv_ref[...]).astype(BF16)


def _mem_kv(memf, wk, wv, tm):
    R, D = memf.shape
    tok = lambda i: (i, 0)
    const = lambda i: (0, 0)
    return pl.pallas_call(
        _mem_kv_body,
        grid=(R // tm,),
        in_specs=[pl.BlockSpec((tm, D), tok), pl.BlockSpec(wk.shape, const),
                  pl.BlockSpec(wv.shape, const)],
        out_specs=[pl.BlockSpec((tm, D), tok), pl.BlockSpec((tm, D), tok)],
        out_shape=[jax.ShapeDtypeStruct((R, D), BF16), jax.ShapeDtypeStruct((R, D), BF16)],
        compiler_params=_params("arbitrary"),
        name="mem_kv",
    )(memf, wk, wv)


def _mem_attn_body(x0_ref, on_ref, om_ref, wmix_ref, g1_ref, b1_ref,
                   k_ref, v_ref, wq_ref, wo_ref, g_ref, b_ref, o_ref, *, alpha):
    half = on_ref.shape[1]
    mix = _dot(on_ref[...], wmix_ref[:half, :]) + _dot(om_ref[...], wmix_ref[half:, :])
    x = _layer_norm(alpha * x0_ref[...] + mix, g1_ref[...], b1_ref[...])
    D = x.shape[1]
    dh = D // MEM_HEADS
    q = (_dot(x.astype(BF16), wq_ref[...]) * (dh ** -0.5 * LOG2E)).astype(BF16)
    outs = []
    for h in range(MEM_HEADS):
        cs = slice(dh * h, dh * (h + 1))
        s = _dot_nt(q[:, cs], k_ref[0, :, cs])
        p = jnp.exp2(s - jnp.max(s, axis=1, keepdims=True))
        l = jnp.sum(p, axis=1, keepdims=True)
        outs.append((_dot(p.astype(BF16), v_ref[0, :, cs]) * (1.0 / l)).astype(BF16))
    o = jnp.concatenate(outs, axis=1)
    y = _dot(o, wo_ref[...])
    o_ref[...] = _layer_norm(alpha * x + y, g_ref[...], b_ref[...])


def _mem_attn(xf, o_nsa, o_mla, w_o, g1, b1, k_mem, v_mem, wq, wo, g, b, alpha, S, tm):
    T, D = xf.shape
    nst = S // tm
    M = k_mem.shape[1]
    tok = lambda i: (i, 0)
    const = lambda i: (0, 0)
    memb = lambda i: (i // nst, 0, 0)
    return pl.pallas_call(
        functools.partial(_mem_attn_body, alpha=alpha),
        grid=(T // tm,),
        in_specs=[
            pl.BlockSpec((tm, D), tok),
            pl.BlockSpec((tm, o_nsa.shape[1]), tok),
            pl.BlockSpec((tm, o_mla.shape[1]), tok),
            pl.BlockSpec(w_o.shape, const),
            pl.BlockSpec((1, D), const),
            pl.BlockSpec((1, D), const),
            pl.BlockSpec((1, M, D), memb),
            pl.BlockSpec((1, M, D), memb),
            pl.BlockSpec(wq.shape, const),
            pl.BlockSpec(wo.shape, const),
            pl.BlockSpec((1, D), const),
            pl.BlockSpec((1, D), const),
        ],
        out_specs=pl.BlockSpec((tm, D), tok),
        out_shape=jax.ShapeDtypeStruct((T, D), F32),
        compiler_params=_params("arbitrary"),
        name="mem_attn",
    )(xf, o_nsa, o_mla, w_o, g1, b1, k_mem, v_mem, wq, wo, g, b)


HALO = 16


FFN_SLAB = 256
FFN_TM = 512


def _ffn_body(x_ref, xh_ref, wg_ref, wu_ref, cw_ref, cb_ref, wd_ref, g_ref, b_ref, o_ref,
              act_ref, *, alpha, seq_tiles):
    i = pl.program_id(0)
    x = x_ref[...]
    xb = x.astype(BF16)
    xe = jnp.concatenate([xh_ref[...].astype(BF16), xb], axis=0)
    tm = x.shape[0]
    row = lax.broadcasted_iota(jnp.int32, (tm, 1), 0)
    seq_start = i % seq_tiles == 0
    for c0 in range(0, wg_ref.shape[1], FFN_SLAB):
        cs = slice(c0, c0 + FFN_SLAB)
        gate_e = _dot(xe, wg_ref[:, cs])
        gate = gate_e[HALO:]
        up = _dot(xb, wu_ref[:, cs])
        halo = jnp.where(seq_start, 0.0, gate_e[:HALO])
        g1 = jnp.where(row == 0, halo[HALO - 1:HALO], pltpu.roll(gate, 1, 0))
        g2 = jnp.where(row == 0, halo[HALO - 2:HALO - 1],
                       jnp.where(row == 1, halo[HALO - 1:HALO], pltpu.roll(gate, 2, 0)))
        conv = cw_ref[0:1, cs] * g2 + cw_ref[1:2, cs] * g1 + cw_ref[2:3, cs] * gate + cb_ref[:, cs]
        act_ref[:, cs] = (conv * (1.0 / (1.0 + jnp.exp(-conv))) * up).astype(BF16)
    y = _dot(act_ref[...], wd_ref[...])
    o_ref[...] = _layer_norm(alpha * x + y, g_ref[...], b_ref[...])


def _ffn(xf, wg, wu, cw, cb, wd, g, b, alpha, S, tm):
    T, D = xf.shape
    dff = wg.shape[1]
    assert dff % FFN_SLAB == 0
    tok = lambda i: (i, 0)
    const = lambda i: (0, 0)
    return pl.pallas_call(
        functools.partial(_ffn_body, alpha=alpha, seq_tiles=S // tm),
        grid=(T // tm,),
        in_specs=[
            pl.BlockSpec((tm, D), tok),
            pl.BlockSpec((HALO, D), lambda i: (jnp.maximum(i * (tm // HALO) - 1, 0), 0)),
            pl.BlockSpec((D, dff), const),
            pl.BlockSpec((D, dff), const),
            pl.BlockSpec((CONV_WIDTH, dff), const),
            pl.BlockSpec((1, dff), const),
            pl.BlockSpec((dff, D), const),
            pl.BlockSpec((1, D), const),
            pl.BlockSpec((1, D), const),
        ],
        out_specs=pl.BlockSpec((tm, D), tok),
        out_shape=jax.ShapeDtypeStruct((T, D), F32),
        scratch_shapes=[pltpu.VMEM((tm, dff), BF16)],
        compiler_params=_params("arbitrary"),
        name="ffn",
    )(xf, xf, wg, wu, cw, cb, wd, g, b)


def _inv_freq_row(dim, lane_lo, lane_hi, period):
    inv = ROPE_THETA ** (-np.arange(0, dim, 2, dtype=np.float64) / dim)
    row = np.zeros((1, LANES), np.float32)
    for lane in range(lane_lo, lane_hi):
        row[0, lane] = inv[(lane % period) % (dim // 2)]
    return jnp.asarray(row)


def _cover_t(S):
    nc = S // CMP_STRIDE
    ns = S // SLC_LEN
    cs = np.arange(nc)[:, None] * CMP_STRIDE
    ss = np.arange(ns)[None, :] * SLC_LEN
    cover = np.clip(np.minimum(cs + CMP_LEN, ss + SLC_LEN) - np.maximum(cs, ss), 0, None) / CMP_LEN
    cover[nc - 1:] = 0.0
    return jnp.asarray(cover.T, dtype=BF16)


def _permute_w_in(w):
    D = w.shape[0]
    c1 = NSA_HEADS * NSA_DH
    c2 = c1 + 3 * 2 * NSA_GROUPS * NSA_DH
    c3 = c2 + 3 * NSA_HEADS
    c4 = c3 + MLA_Q_RANK
    c5 = c4 + MLA_KV_RANK
    c6 = c5 + MLA_ROPE
    half = MLA_ROPE // 2
    z = lambda n: jnp.zeros((D, n), w.dtype)
    misc = jnp.concatenate(
        [w[:, c5 + half:c6], z(GATE_LANE0 - half), w[:, c2:c3], z(MLA_PE1 - GATE_LANE0 - (c3 - c2)),
         w[:, c5:c5 + half], z(LANES - MLA_PE1 - half)], axis=1)
    return jnp.concatenate([w[:, :c2], w[:, c3:c5], misc], axis=1).astype(BF16)


def _mla_head_lanes(nope, pe):
    r, H, _ = nope.shape
    half = MLA_ROPE // 2
    split = MLA_PE1 - half
    pad = jnp.zeros((r, H, LANES - MLA_NOPE - MLA_ROPE), nope.dtype)
    return jnp.concatenate([pe[..., half:], nope[..., :split], pe[..., :half], nope[..., split:], pad],
                           axis=2).reshape(r, H * LANES)


def _permute_w_uq(w):
    r = w.shape[0]
    w3 = w.reshape(r, MLA_HEADS, MLA_NOPE + MLA_ROPE)
    return _mla_head_lanes(w3[..., :MLA_NOPE], w3[..., MLA_NOPE:]).astype(BF16)


def _permute_w_ukv(w):
    r = w.shape[0]
    w3 = w.reshape(r, MLA_HEADS, MLA_NOPE + MLA_V)
    k = _mla_head_lanes(w3[..., :MLA_NOPE], jnp.zeros((r, MLA_HEADS, MLA_ROPE), w.dtype))
    v = _pad_lanes(w3[..., MLA_NOPE:]).reshape(r, MLA_HEADS // 2, 2, LANES)
    v = jnp.stack([v[:, :, 0], jnp.roll(v[:, :, 1], LANES - MLA_V, axis=-1)], axis=2)
    return jnp.concatenate([k, v.reshape(r, MLA_HEADS * LANES)], axis=1).astype(BF16)


def _permute_w_o(w):
    n = NSA_HEADS * NSA_DH
    wn = w[:n].reshape(NSA_GROUPS, NSA_REP, NSA_DH, -1).transpose(1, 0, 2, 3).reshape(n, -1)
    return jnp.concatenate([wn, w[n:]], axis=0).astype(BF16)


def _pad_lanes(a):
    return jnp.concatenate([a, jnp.zeros(a.shape[:-1] + (LANES - a.shape[-1],), a.dtype)], axis=-1)


def kernel(x, mem, positions, w_in, nsa_k_pos, nsa_ck_w1, nsa_ck_b1, nsa_ck_w2, nsa_ck_b2,
           nsa_v_pos, nsa_cv_w1, nsa_cv_b1, nsa_cv_w2, nsa_cv_b2,
           mla_q_norm, mla_w_uq, mla_kv_norm, mla_w_ukv, w_o, ln1_g, ln1_b,
           mem_wq, mem_wk, mem_wv, mem_wo, ln2_g, ln2_b,
           ffn_w_up, ffn_conv_w, ffn_conv_b, ffn_w_down, ln3_g, ln3_b):
    B, S, D = x.shape
    T = B * S
    depth = w_in.shape[0]
    alpha = (2.0 * depth) ** 0.25
    d_ff = ffn_w_down.shape[1]
    tm = min(512, S)
    assert S % MLA_TQ == 0 and S >= WIN_SPAN and S % tm == 0
    assert (B * mem.shape[1]) % 256 == 0

    pos = positions.reshape(T, 1)
    pos_cmp = positions[:, CMP_LEN - 1::CMP_STRIDE]
    pos_cmp = jnp.concatenate([pos_cmp, pos_cmp[:, -1:]], axis=1)[:, :, None]
    inv_cmp = _inv_freq_row(NSA_DH, 0, NSA_DH, NSA_DH)
    inv_tok = (_inv_freq_row(NSA_DH, 0, NSA_DH // 2, NSA_DH)
               + _inv_freq_row(MLA_ROPE, NSA_DH // 2, NSA_DH // 2 + MLA_ROPE // 2, MLA_ROPE // 2))
    cov_t = _cover_t(S)
    memf = mem.reshape(B * mem.shape[1], D)

    xf = x.reshape(T, D)
    for l in range(depth):
        qn, kvn, kvc, misc, q_m, k_m, v_m = _inproj(
            pos, inv_tok, xf, _permute_w_in(w_in[l]), mla_q_norm[l][None, :],
            mla_kv_norm[l][None, :], _permute_w_uq(mla_w_uq[l]), _permute_w_ukv(mla_w_ukv[l]),
            B, S, tm)
        kvcmp = _compress(
            pos_cmp, inv_cmp, kvc,
            *_compress_weights(nsa_k_pos[l], nsa_ck_w1[l], nsa_ck_b1[l], nsa_ck_w2[l], nsa_ck_b2[l],
                               nsa_v_pos[l], nsa_cv_w1[l], nsa_cv_b1[l], nsa_cv_w2[l], nsa_cv_b2[l]),
            B, S)
        o_nsa = _nsa(qn, kvn, kvcmp, misc, cov_t, B, S)
        o_mla = _mla(q_m, k_m, v_m, B, S)
        k_mem, v_mem = _mem_kv(memf, mem_wk[l].astype(BF16), mem_wv[l].astype(BF16), 256)
        xf = _mem_attn(xf, o_nsa.reshape(T, -1), o_mla.reshape(T, -1), _permute_w_o(w_o[l]),
                       ln1_g[l][None, :], ln1_b[l][None, :],
                       k_mem.reshape(B, -1, D), v_mem.reshape(B, -1, D),
                       mem_wq[l].astype(BF16), mem_wo[l].astype(BF16),
                       ln2_g[l][None, :], ln2_b[l][None, :], alpha, S, min(MEM_TM, S))
        xf = _ffn(xf, ffn_w_up[l][:, :d_ff].astype(BF16), ffn_w_up[l][:, d_ff:].astype(BF16),
                  ffn_conv_w[l], ffn_conv_b[l][None, :], ffn_w_down[l].astype(BF16),
                  ln3_g[l][None, :], ln3_b[l][None, :], alpha, S, FFN_TM)
    return xf.reshape(B, S, D)
```

```python
import functools
import math

import numpy as np
import jax
import jax.numpy as jnp
from jax import lax
from jax.experimental import pallas as pl
from jax.experimental.pallas import tpu as pltpu

F32 = jnp.float32
BF16 = jnp.bfloat16

NSA_HEADS = 8
NSA_GROUPS = 2
NSA_REP = NSA_HEADS // NSA_GROUPS
NSA_DH = 64
CMP_STRIDE = 16
CMP_LEN = 32
SLC_LEN = 64
TOPN = 16
WINDOW = 512
CMP_HIDDEN = 128
FORCE_BONUS = 1e4
MLA_HEADS = 8
MLA_Q_RANK = 384
MLA_KV_RANK = 256
MLA_NOPE = 64
MLA_ROPE = 32
MLA_V = 64
MEM_HEADS = 4
CONV_WIDTH = 3
ROPE_THETA = 10000.0
LN_EPS = 1e-5
RMS_EPS = 1e-6
NEG_INF = -1e30
LOG2E = math.log2(math.e)

LANES = 128
VMEM_LIMIT = 60 * 1024 * 1024

C_Q = 0
C_KVC = 512
C_KVN = 768
C_LAT = 1280
C_MISC = 1920
IN_COLS_PAD = 2048
GATE_LANE0 = MLA_ROPE
MLA_PE2 = 0
MLA_PE1 = 64
ONES_LANE = 64
ONES_LANE_G1 = 0


def _dot(a, b):
    return jnp.dot(a, b, preferred_element_type=F32)


def _dot_nt(a, b):
    return lax.dot_general(a, b, (((1,), (1,)), ((), ())), preferred_element_type=F32)


def _layer_norm(y, g, b):
    mu = jnp.mean(y, axis=-1, keepdims=True)
    d = y - mu
    var = jnp.mean(d * d, axis=-1, keepdims=True)
    return d * lax.rsqrt(var + LN_EPS) * g + b


def _params(*sem):
    return pltpu.CompilerParams(dimension_semantics=sem, vmem_limit_bytes=VMEM_LIMIT)


def _rope_tables(pos_col, inv_row, half):
    ang = pos_col * inv_row
    cos = jnp.cos(ang)
    sin = jnp.sin(ang)
    lane = lax.broadcasted_iota(jnp.int32, (1, LANES), 1)
    upper = (lane & (2 * half - 1)) >= half
    rot = inv_row != 0.0
    sin_hi = jnp.where(upper & rot, sin, 0.0)
    sin_lo = jnp.where(upper | (~rot), 0.0, -sin)
    return cos, sin_hi, sin_lo


def _apply_rope(v, tabs, half):
    cos, sin_hi, sin_lo = tabs
    return v * cos + pltpu.roll(v, half, 1) * sin_hi + pltpu.roll(v, LANES - half, 1) * sin_lo


def _rms_norm(v, g):
    return v * lax.rsqrt(jnp.mean(v * v, axis=-1, keepdims=True) + RMS_EPS) * g


def _inproj_body(pos_ref, inv_ref, x_ref, w_ref, gq_ref, gkv_ref, wq_ref, wkv_ref,
                 qn_ref, kvn_ref, kvc_ref, misc_ref, qm_ref, km_ref, vm_ref):
    xb = x_ref[...].astype(BF16)
    lane = lax.broadcasted_iota(jnp.int32, (1, LANES), 1)
    low = lane < NSA_DH
    qscale = NSA_DH ** -0.5 * LOG2E

    ang = pos_ref[...].astype(F32) * inv_ref[...]
    cos, sin = jnp.cos(ang), jnp.sin(ang)
    nf, mf = NSA_DH // 2, MLA_ROPE // 2

    def tile_nsa(t):
        t = jnp.where(lane < nf, t, 0.0)
        t = t + pltpu.roll(t, nf, 1)
        return t + pltpu.roll(t, 2 * nf, 1)

    def place_mla(t):
        t = jnp.where((lane >= nf) & (lane < nf + mf), t, 0.0)
        return pltpu.roll(t, MLA_PE1 - nf, 1) + pltpu.roll(t, LANES + MLA_PE2 - nf, 1)

    upper = (lane & (NSA_DH - 1)) >= nf
    sin_n = tile_nsa(sin)
    tabs = (tile_nsa(cos), jnp.where(upper, sin_n, 0.0), jnp.where(upper, 0.0, -sin_n))
    pe1 = (lane >= MLA_PE1) & (lane < MLA_PE1 + mf)
    pe2 = (lane >= MLA_PE2) & (lane < MLA_PE2 + mf)
    pe_lanes = pe1 | pe2
    sin_m = place_mla(sin)
    cos_m = jnp.where(pe_lanes, place_mla(cos), 1.0)
    sin_m = jnp.where(pe1, -sin_m, jnp.where(pe2, sin_m, 0.0))

    def rope_mla(v):
        return v * cos_m + pltpu.roll(v, LANES // 2, 1) * sin_m

    def proj(c0, n):
        return _dot(xb, w_ref[:, c0:c0 + n])

    def split_store(v, ref, idx_lo, idx_hi, pad=0.0):
        ref[0, idx_lo] = jnp.where(low, v, pad).astype(BF16)
        ref[0, idx_hi] = jnp.where(low, pltpu.roll(v, NSA_DH, 1), pad).astype(BF16)

    ones_pad = jnp.where(lane == ONES_LANE, 1.0, 0.0)
    ones_pad_g1 = jnp.where(lane == ONES_LANE_G1, 1.0, 0.0)

    for slab in range(2):
        h = proj(C_Q + 256 * slab, 256)
        for j in range(2):
            r = _apply_rope(h[:, LANES * j:LANES * (j + 1)], tabs, NSA_DH // 2) * qscale
            split_store(r, qn_ref, 4 * slab + 2 * j, 4 * slab + 2 * j + 1)

    h = proj(C_KVC, 256)
    kvc_ref[0] = h[:, :LANES]
    kvc_ref[1] = h[:, LANES:]

    for slab in range(2):
        h = proj(C_KVN + 256 * slab, 256)
        k = _apply_rope(h[:, :LANES], tabs, NSA_DH // 2)
        split_store(k, kvn_ref, 4 * slab, 4 * slab + 1)
        v = h[:, LANES:]
        kvn_ref[0, 4 * slab + 2] = jnp.where(low, v, ones_pad).astype(BF16)
        kvn_ref[0, 4 * slab + 3] = jnp.where(low, ones_pad_g1, v).astype(BF16)

    lat = [proj(C_LAT + 256 * i, 256) for i in range((IN_COLS_PAD - C_LAT) // 256)]
    misc = lat[2][:, LANES:]
    misc_ref[...] = misc
    mq = jnp.concatenate([lat[0], lat[1][:, :LANES]], axis=1)
    mkv = jnp.concatenate([lat[1][:, LANES:], lat[2][:, :LANES]], axis=1)
    mscale = (MLA_NOPE + MLA_ROPE) ** -0.5 * LOG2E
    qn = _rms_norm(mq, gq_ref[...]).astype(BF16)
    kvn = _rms_norm(mkv, gkv_ref[...]).astype(BF16)
    kpe = jnp.where(pe_lanes, rope_mla(misc), 0.0)
    kcols = MLA_HEADS * LANES
    for slab in range(MLA_HEADS // 2):
        cs = slice(256 * slab, 256 * (slab + 1))
        hq = _dot(qn, wq_ref[:, cs])
        hk = _dot(kvn, wkv_ref[:, cs])
        hv = _dot(kvn, wkv_ref[:, kcols + 256 * slab:kcols + 256 * (slab + 1)])
        for j in range(2):
            ls = slice(LANES * j, LANES * (j + 1))
            qm_ref[0, 2 * slab + j] = (rope_mla(hq[:, ls]) * mscale).astype(BF16)
            km_ref[0, 2 * slab + j] = (hk[:, ls] + kpe).astype(BF16)
            vm_ref[0, 2 * slab + j] = (hv[:, ls] + (ones_pad_g1 if j else ones_pad)).astype(BF16)


def _inproj(pos, inv_row, xf, w_in_p, gq, gkv, wq_p, wkv_p, B, S, tm):
    T = B * S
    nst = S // tm
    tok = lambda i: (i, 0)
    const = lambda i: (0, 0)
    head_blk = lambda i: (i // nst, 0, i % nst, 0)
    heads = jax.ShapeDtypeStruct((B, 8, S, LANES), BF16)
    return pl.pallas_call(
        _inproj_body,
        grid=(T // tm,),
        in_specs=[
            pl.BlockSpec((tm, 1), tok),
            pl.BlockSpec((1, LANES), const),
            pl.BlockSpec((tm, xf.shape[1]), tok),
            pl.BlockSpec(w_in_p.shape, const),
            pl.BlockSpec(gq.shape, const),
            pl.BlockSpec(gkv.shape, const),
            pl.BlockSpec(wq_p.shape, const),
            pl.BlockSpec(wkv_p.shape, const),
        ],
        out_specs=[
            pl.BlockSpec((1, 8, tm, LANES), head_blk),
            pl.BlockSpec((1, 8, tm, LANES), head_blk),
            pl.BlockSpec((2, tm, LANES), lambda i: (0, i, 0)),
            pl.BlockSpec((tm, LANES), tok),
            pl.BlockSpec((1, 8, tm, LANES), head_blk),
            pl.BlockSpec((1, 8, tm, LANES), head_blk),
            pl.BlockSpec((1, 8, tm, LANES), head_blk),
        ],
        out_shape=[
            heads,
            heads,
            jax.ShapeDtypeStruct((2, T, LANES), F32),
            jax.ShapeDtypeStruct((T, LANES), F32),
            heads,
            heads,
            heads,
        ],
        compiler_params=_params("arbitrary"),
        name="inproj",
    )(pos, inv_row, xf, w_in_p, gq, gkv, wq_p, wkv_p)


def _compress_body(pos_ref, inv_ref, x_ref, pe_ref, w1_ref, b1_ref, w2_ref, b2_ref, o_ref):
    is_k = pl.program_id(0) == 0
    nch = o_ref.shape[2]
    a1 = jnp.zeros((nch, NSA_GROUPS * CMP_HIDDEN), F32)
    a2 = jnp.zeros((nch, NSA_GROUPS * CMP_HIDDEN), F32)
    for l in range(CMP_STRIDE):
        xl = x_ref.at[0, 0][pl.ds(l, nch, stride=CMP_STRIDE), :]
        a1 = a1 + _dot((xl + pe_ref[0, l:l + 1, :]).astype(BF16), w1_ref[0, l])
        a2 = a2 + _dot((xl + pe_ref[0, CMP_STRIDE + l:CMP_STRIDE + l + 1, :]).astype(BF16),
                       w1_ref[0, CMP_STRIDE + l])
    pre = a1 + pltpu.roll(a2, nch - 1, 0) + b1_ref[0]
    hid = jax.nn.gelu(pre, approximate=True)
    out = _dot(hid.astype(BF16), w2_ref[0]) + b2_ref[0]
    tabs = _rope_tables(pos_ref[0].astype(F32), inv_ref[...], NSA_DH // 2)
    row = lax.broadcasted_iota(jnp.int32, (nch, 1), 0)
    for g in range(NSA_GROUPS):
        og = out[:, LANES * g:LANES * (g + 1)]
        og = jnp.where(is_k, _apply_rope(og, tabs, NSA_DH // 2), og)
        o_ref[g, 0] = jnp.where(row < nch - 1, og, 0.0).astype(BF16)


def _compress(pos_cmp, inv_nsa, kvc, pe, w1, b1, w2, b2, B, S):
    nch = S // CMP_STRIDE
    x = kvc.reshape(2, B, S, LANES)
    kv = lambda j, b: (j, 0, 0)
    kv4 = lambda j, b: (j, 0, 0, 0)
    return pl.pallas_call(
        _compress_body,
        grid=(2, B),
        in_specs=[
            pl.BlockSpec((1, nch, 1), lambda j, b: (b, 0, 0)),
            pl.BlockSpec((1, LANES), lambda j, b: (0, 0)),
            pl.BlockSpec((1, 1, S, LANES), lambda j, b: (j, b, 0, 0)),
            pl.BlockSpec((1,) + pe.shape[1:], kv),
            pl.BlockSpec((1,) + w1.shape[1:], kv4),
            pl.BlockSpec((1,) + b1.shape[1:], kv),
            pl.BlockSpec((1,) + w2.shape[1:], kv),
            pl.BlockSpec((1,) + b2.shape[1:], kv),
        ],
        out_specs=pl.BlockSpec((NSA_GROUPS, 1, nch, LANES), lambda j, b: (j, b, 0, 0)),
        out_shape=jax.ShapeDtypeStruct((2 * NSA_GROUPS, B, nch, LANES), BF16),
        compiler_params=_params("arbitrary", "arbitrary"),
        name="compress",
    )(pos_cmp, inv_nsa, x, pe, w1, b1, w2, b2)


def _compress_weights(k_pos, k_w1, k_b1, k_w2, k_b2, v_pos, v_w1, v_b1, v_w2, v_b2):
    def one(pos, w1, b1, w2, b2, g1_high):
        w1l = w1.reshape(CMP_LEN, NSA_DH, CMP_HIDDEN)
        z1 = jnp.zeros_like(w1l)
        w1bd = jnp.concatenate([jnp.concatenate([w1l, z1], axis=2),
                                jnp.concatenate([z1, w1l], axis=2)], axis=1)
        pad = lambda a: _pad_lanes(a)
        pad1 = (lambda a: jnp.roll(_pad_lanes(a), LANES - a.shape[-1], axis=-1)) if g1_high else pad
        z2 = jnp.zeros_like(pad(w2))
        w2bd = jnp.concatenate([jnp.concatenate([pad(w2), z2], axis=1),
                                jnp.concatenate([z2, pad1(w2)], axis=1)], axis=0)
        return (jnp.tile(pos, (1, NSA_GROUPS)), w1bd.astype(BF16), jnp.tile(b1, NSA_GROUPS)[None, :],
                w2bd.astype(BF16), jnp.concatenate([pad(b2), pad1(b2)])[None, :])
    k = one(k_pos, k_w1, k_b1, k_w2, k_b2, False)
    v = one(v_pos, v_w1, v_b1, v_w2, v_b2, True)
    return tuple(jnp.stack([a, b]) for a, b in zip(k, v))


STRIP = 64


def _lane_tile(col, n):
    reps = [col] * (n // LANES)
    if n % LANES:
        reps.append(col[:, :n % LANES])
    return reps[0] if len(reps) == 1 else jnp.concatenate(reps, axis=1)


def _flash_reset(m_ref, acc_ref):
    m_ref[...] = jnp.full(m_ref.shape, NEG_INF, F32)
    acc_ref[...] = jnp.zeros(acc_ref.shape, F32)


def _flash_update(s_ref, v, m_ref, acc_ref, p_ref, mask=None):
    rows, n = s_ref.shape
    for r in range(rows // STRIP):
        rs = slice(STRIP * r, STRIP * (r + 1))
        s = s_ref[rs, :]
        if mask is not None:
            s = jnp.where(mask[rs], s, NEG_INF)
        if m_ref is None:
            p_ref[rs, :] = jnp.exp2(s - jnp.max(s, axis=1, keepdims=True)).astype(BF16)
            continue
        m_old = m_ref[rs, :]
        m_new = jnp.maximum(m_old, jnp.max(s, axis=1, keepdims=True))
        p_ref[rs, :] = jnp.exp2(s - _lane_tile(m_new, n)).astype(BF16)
        acc_ref[rs, :] = jnp.exp2(m_old - m_new) * acc_ref[rs, :]
        m_ref[rs, :] = m_new
    if m_ref is None:
        acc_ref[...] = _dot(p_ref[...], v)
    else:
        acc_ref[...] += _dot(p_ref[...], v)


NSA_TQ = 256
NSA_ROWS = NSA_REP * NSA_TQ
SLC_CHUNK = 512
WIN_SPAN = WINDOW + NSA_TQ
BIAS_LANE0 = LANES


def _nsa_body(q_ref, kvn_ref, kvc_ref, misc_ref, cov_ref, o_ref,
              kaug_ref, score_ref, qbias_ref, sa_ref, sb_ref, pa_ref, pb_ref, ms_ref, accs_ref,
              sw_ref, pw_ref, accw_ref, sc_ref, pn_ref, pc_ref, oc_ref):
    c = pl.program_id(1)
    rows = NSA_ROWS
    t_row = c * NSA_TQ + (lax.broadcasted_iota(jnp.int32, (rows, 1), 0) & (NSA_TQ - 1))
    ncmp = kvc_ref.shape[2]
    nblk = kaug_ref.shape[1] // SLC_LEN

    @pl.when(c == 0)
    def _():
        nkeys = kaug_ref.shape[1]
        key_blk = lax.broadcasted_iota(jnp.int32, (nkeys, LANES), 0) // SLC_LEN
        onehot = jnp.where(key_blk == lax.broadcasted_iota(jnp.int32, (nkeys, LANES), 1), 1.0, 0.0)
        for g in range(NSA_GROUPS):
            kaug_ref[g, :, :BIAS_LANE0] = kvn_ref[0, g]
            kaug_ref[g, :, BIAS_LANE0:] = onehot.astype(BF16)

    qs = [q_ref[0, NSA_REP * g:NSA_REP * (g + 1)].reshape(rows, LANES) for g in range(NSA_GROUPS)]

    cmp_valid = (CMP_STRIDE * lax.broadcasted_iota(jnp.int32, (1, ncmp), 1) + CMP_LEN - 1) <= t_row
    for g in range(NSA_GROUPS):
        sc_ref[g] = _dot_nt(qs[g], kvc_ref[g, 0])
    win_start = pl.multiple_of(jnp.maximum(c * NSA_TQ - WINDOW, 0), NSA_TQ)

    def window_scores(g):
        sw_ref[g] = _dot_nt(qs[g], kvn_ref[0, 4 + g, pl.ds(win_start, WIN_SPAN), :])

    window_scores(0)
    for g in range(NSA_GROUPS):
        for r in range(rows // STRIP):
            rs = slice(STRIP * r, STRIP * (r + 1))
            s = jnp.where(cmp_valid[rs], sc_ref[g, rs, :], NEG_INF)
            p = jnp.exp2(s - jnp.max(s, axis=1, keepdims=True))
            l = jnp.sum(p, axis=1, keepdims=True)
            p = p * jnp.where(t_row[rs] >= CMP_LEN - 1, 1.0 / l, 0.0)
            pn_ref[g, rs, :] = p
            pc_ref[g, rs, :] = p.astype(BF16)
        oc_ref[g] = _dot(pc_ref[g], kvc_ref[NSA_GROUPS + g, 0])

    def select(nb):
        window_scores(1)
        width = NSA_GROUPS * NSA_TQ
        ps = jnp.concatenate(
            [sum(pn_ref[g, NSA_TQ * r:NSA_TQ * (r + 1), :] for r in range(NSA_REP))
             for g in range(NSA_GROUPS)], axis=0)
        hi = ps.astype(BF16)
        lo = (ps - hi.astype(F32)).astype(BF16)
        imp = _dot_nt(cov_ref[:nb, :], hi) + _dot_nt(cov_ref[:nb, :], lo)
        jidx = lax.broadcasted_iota(jnp.int32, (nb, width), 0)
        lane_q = lax.broadcasted_iota(jnp.int32, (1, width), 1) & (NSA_TQ - 1)
        cur = c * (NSA_TQ // SLC_LEN) + lane_q // SLC_LEN
        forced = (jidx == 0) | (jidx == cur) | (jidx == cur - 1)
        score = jnp.where(jidx <= cur, jnp.where(forced, FORCE_BONUS, imp), NEG_INF)
        score_ref[:nb, :] = score
        sub = 8
        cnt = [jnp.zeros((sub, width), F32) for _ in range(nb // sub)]
        tiles = [score[sub * v:sub * (v + 1)] for v in range(nb // sub)]
        sidx = lax.broadcasted_iota(jnp.int32, (sub, width), 0)
        for jp in range(nb):
            rowv = jnp.broadcast_to(score_ref[jp:jp + 1, :], (sub, width))
            for v in range(nb // sub):
                if sub * v > jp:
                    cnt[v] = jnp.where(rowv >= tiles[v], cnt[v] + 1.0, cnt[v])
                elif sub * v + sub - 1 <= jp:
                    cnt[v] = jnp.where(rowv > tiles[v], cnt[v] + 1.0, cnt[v])
                else:
                    ge = jnp.where(rowv >= tiles[v], cnt[v] + 1.0, cnt[v])
                    gt = jnp.where(rowv > tiles[v], cnt[v] + 1.0, cnt[v])
                    cnt[v] = jnp.where(sidx + sub * v > jp, ge, gt)
        rank = jnp.concatenate(cnt, axis=0)
        bias = jnp.where(rank < float(TOPN), 0.0, NEG_INF)
        bias = jnp.concatenate([bias, jnp.zeros((LANES - nb, width), F32)], axis=0)
        qbias_ref[...] = bias.T.astype(BF16)

    visible = (c + 1) * (NSA_TQ // SLC_LEN)

    @pl.when(visible <= TOPN)
    def _():
        window_scores(1)
        qbias_ref[...] = jnp.zeros(qbias_ref.shape, BF16)

    bounds = [TOPN] + list(range(TOPN + 8, nblk, 8)) + [nblk]
    for lo_nb, nb in zip(bounds[:-1], bounds[1:]):
        pl.when((visible > lo_nb) & (visible <= nb))(functools.partial(select, nb))

    _flash_reset(ms_ref, accs_ref)
    qas = [jnp.concatenate(
        [qs[g], jnp.concatenate([qbias_ref[NSA_TQ * g:NSA_TQ * (g + 1), :]] * NSA_REP, axis=0)],
        axis=1) for g in range(NSA_GROUPS)]
    last = c // (SLC_CHUNK // NSA_TQ)

    def slc_scores(kc, dst):
        k0 = pl.multiple_of(kc * SLC_CHUNK, SLC_CHUNK)
        for g in range(NSA_GROUPS):
            dst[g] = _dot_nt(qas[g], kaug_ref[g, pl.ds(k0, SLC_CHUNK), :])

    def slc_update(src, p_ref, kc, causal):
        k0 = pl.multiple_of(kc * SLC_CHUNK, SLC_CHUNK)
        mask = None
        if causal:
            mask = (k0 + lax.broadcasted_iota(jnp.int32, (1, SLC_CHUNK), 1)) <= t_row
        for g in range(NSA_GROUPS):
            _flash_update(src.at[g], kvn_ref[0, 2 + g, pl.ds(k0, SLC_CHUNK), :],
                          ms_ref.at[g], accs_ref.at[g], p_ref.at[g], mask)

    slc_scores(0, sa_ref)

    diff = t_row - (win_start + lax.broadcasted_iota(jnp.int32, (1, WIN_SPAN), 1))
    win_valid = lax.bitcast_convert_type(diff, jnp.uint32) < jnp.uint32(WINDOW)
    for g in range(NSA_GROUPS):
        _flash_update(sw_ref.at[g], kvn_ref[0, 6 + g, pl.ds(win_start, WIN_SPAN), :],
                      None, accw_ref.at[g], pw_ref.at[g], win_valid)

    def slc_pair(i, carry):
        slc_scores(2 * i + 1, sb_ref)
        slc_update(sa_ref, pa_ref, 2 * i, False)
        slc_scores(2 * i + 2, sa_ref)
        slc_update(sb_ref, pb_ref, 2 * i + 1, False)
        return carry
    lax.fori_loop(0, last // 2, slc_pair, 0)
    tail = 2 * (last // 2)

    @pl.when(last > tail)
    def _():
        slc_scores(tail + 1, sb_ref)
        slc_update(sa_ref, pa_ref, tail, False)
        slc_update(sb_ref, pb_ref, tail + 1, True)

    @pl.when(last == tail)
    def _():
        slc_update(sa_ref, pa_ref, tail, True)

    sig = 1.0 / (1.0 + jnp.exp(-misc_ref[...]))
    low = lax.broadcasted_iota(jnp.int32, (NSA_TQ, LANES), 1) < NSA_DH
    denom_lane = jnp.where(low, ONES_LANE, ONES_LANE_G1)

    def pair(ref, rs):
        a0, a1 = ref[0, rs, :], ref[1, rs, :]
        return jnp.where(low, a0, a1), jnp.where(low, a1, a0)

    outs = []
    for r in range(NSA_REP):
        rs = slice(NSA_TQ * r, NSA_TQ * (r + 1))
        gate = [jnp.take_along_axis(
            sig, jnp.where(low, GATE_LANE0 + 3 * r + br, GATE_LANE0 + 3 * (NSA_REP + r) + br), axis=1)
            for br in range(3)]
        o_cmp, _ = pair(oc_ref, rs)
        o_slc, l_slc = pair(accs_ref, rs)
        o_win, l_win = pair(accw_ref, rs)
        outs.append(gate[0] * o_cmp
                    + (gate[1] / jnp.take_along_axis(l_slc, denom_lane, axis=1)) * o_slc
                    + (gate[2] / jnp.take_along_axis(l_win, denom_lane, axis=1)) * o_win)
    o_ref[0] = jnp.concatenate(outs, axis=1).astype(BF16)


def _nsa(qn, kvn, kvcmp, misc, cov_t, B, S):
    nblk = S // SLC_LEN
    ncmp = S // CMP_STRIDE
    nq = S // NSA_TQ
    rows = NSA_ROWS
    return pl.pallas_call(
        _nsa_body,
        grid=(B, nq),
        in_specs=[
            pl.BlockSpec((1, NSA_HEADS, NSA_TQ, LANES), lambda b, c: (b, 0, c, 0)),
            pl.BlockSpec((1, 8, S, LANES), lambda b, c: (b, 0, 0, 0)),
            pl.BlockSpec((4, 1, ncmp, LANES), lambda b, c: (0, b, 0, 0)),
            pl.BlockSpec((NSA_TQ, LANES), lambda b, c: (b * nq + c, 0)),
            pl.BlockSpec(cov_t.shape, lambda b, c: (0, 0)),
        ],
        out_specs=pl.BlockSpec((1, NSA_TQ, NSA_HEADS * NSA_DH), lambda b, c: (b, c, 0)),
        out_shape=jax.ShapeDtypeStruct((B, S, NSA_HEADS * NSA_DH), BF16),
        scratch_shapes=[
            pltpu.VMEM((NSA_GROUPS, S, 2 * LANES), BF16),
            pltpu.VMEM((nblk, NSA_GROUPS * NSA_TQ), F32),
            pltpu.VMEM((NSA_GROUPS * NSA_TQ, LANES), BF16),
            pltpu.VMEM((NSA_GROUPS, rows, SLC_CHUNK), F32),
            pltpu.VMEM((NSA_GROUPS, rows, SLC_CHUNK), F32),
            pltpu.VMEM((NSA_GROUPS, rows, SLC_CHUNK), BF16),
            pltpu.VMEM((NSA_GROUPS, rows, SLC_CHUNK), BF16),
            pltpu.VMEM((NSA_GROUPS, rows, LANES), F32),
            pltpu.VMEM((NSA_GROUPS, rows, LANES), F32),
            pltpu.VMEM((NSA_GROUPS, rows, WIN_SPAN), F32),
            pltpu.VMEM((NSA_GROUPS, rows, WIN_SPAN), BF16),
            pltpu.VMEM((NSA_GROUPS, rows, LANES), F32),
            pltpu.VMEM((NSA_GROUPS, rows, ncmp), F32),
            pltpu.VMEM((NSA_GROUPS, rows, ncmp), F32),
            pltpu.VMEM((NSA_GROUPS, rows, ncmp), BF16),
            pltpu.VMEM((NSA_GROUPS, rows, LANES), F32),
        ],
        compiler_params=_params("arbitrary", "arbitrary"),
        name="nsa",
    )(qn, kvn, kvcmp, misc, cov_t)


MLA_TQ = 512
MLA_CHUNK = 512
MLA_HPB = 4


def _mla_body(q_ref, k_ref, v_ref, o_ref, sa_ref, sb_ref, pa_ref, pb_ref, m_ref, acc_ref):
    qi = pl.program_id(2)
    t_row = qi * MLA_TQ + lax.broadcasted_iota(jnp.int32, (MLA_TQ, 1), 0)
    _flash_reset(m_ref, acc_ref)

    def scores(kc, dst):
        k0 = pl.multiple_of(kc * MLA_CHUNK, MLA_CHUNK)
        for j in range(MLA_HPB):
            dst[j] = _dot_nt(q_ref[0, j], k_ref[0, j, pl.ds(k0, MLA_CHUNK), :])

    def update(src, p_ref, kc, causal):
        k0 = pl.multiple_of(kc * MLA_CHUNK, MLA_CHUNK)
        mask = None
        if causal:
            mask = (k0 + lax.broadcasted_iota(jnp.int32, (1, MLA_CHUNK), 1)) <= t_row
        for j in range(MLA_HPB):
            _flash_update(src.at[j], v_ref[0, j, pl.ds(k0, MLA_CHUNK), :],
                          m_ref.at[j], acc_ref.at[j], p_ref.at[j], mask)

    scores(0, sa_ref)

    def pair(i, carry):
        scores(2 * i + 1, sb_ref)
        update(sa_ref, pa_ref, 2 * i, False)
        scores(2 * i + 2, sa_ref)
        update(sb_ref, pb_ref, 2 * i + 1, False)
        return carry
    lax.fori_loop(0, qi // 2, pair, 0)
    tail = 2 * (qi // 2)

    @pl.when(qi > tail)
    def _():
        scores(tail + 1, sb_ref)
        update(sa_ref, pa_ref, tail, False)
        update(sb_ref, pb_ref, tail + 1, True)

    @pl.when(qi == tail)
    def _():
        update(sa_ref, pa_ref, tail, True)

    low = lax.broadcasted_iota(jnp.int32, (MLA_TQ, LANES), 1) < MLA_V
    denom_lane = jnp.where(low, ONES_LANE, ONES_LANE_G1)
    outs = []
    for j in range(0, MLA_HPB, 2):
        a0, a1 = acc_ref[j], acc_ref[j + 1]
        denom = jnp.take_along_axis(jnp.where(low, a1, a0), denom_lane, axis=1)
        outs.append(jnp.where(low, a0, a1) * (1.0 / denom))
    o_ref[0] = jnp.concatenate(outs, axis=1).astype(BF16)


def _mla(q, k, v, B, S):
    return pl.pallas_call(
        _mla_body,
        grid=(B, MLA_HEADS // MLA_HPB, S // MLA_TQ),
        in_specs=[
            pl.BlockSpec((1, MLA_HPB, MLA_TQ, LANES), lambda b, h, i: (b, h, i, 0)),
            pl.BlockSpec((1, MLA_HPB, S, LANES), lambda b, h, i: (b, h, 0, 0)),
            pl.BlockSpec((1, MLA_HPB, S, LANES), lambda b, h, i: (b, h, 0, 0)),
        ],
        out_specs=pl.BlockSpec((1, MLA_TQ, MLA_HPB * MLA_V), lambda b, h, i: (b, i, h)),
        out_shape=jax.ShapeDtypeStruct((B, S, MLA_HEADS * MLA_V), BF16),
        scratch_shapes=[
            pltpu.VMEM((MLA_HPB, MLA_TQ, MLA_CHUNK), F32),
            pltpu.VMEM((MLA_HPB, MLA_TQ, MLA_CHUNK), F32),
            pltpu.VMEM((MLA_HPB, MLA_TQ, MLA_CHUNK), BF16),
            pltpu.VMEM((MLA_HPB, MLA_TQ, MLA_CHUNK), BF16),
            pltpu.VMEM((MLA_HPB, MLA_TQ, LANES), F32),
            pltpu.VMEM((MLA_HPB, MLA_TQ, LANES), F32),
        ],
        compiler_params=_params("arbitrary", "arbitrary", "arbitrary"),
        name="mla",
    )(q, k, v)


MEM_TM = 1024


def _mem_kv_body(m_ref, wk_ref, wv_ref, k_ref, v_ref):
    mb = m_ref[...].astype(BF16)
    k_ref[...] = _dot(mb, wk_ref[...]).astype(BF16)
    v_ref[...] = _dot(mb, wv_ref[...]).astype(BF16)


def _mem_kv(memf, wk, wv, tm):
    R, D = memf.shape
    tok = lambda i: (i, 0)
    const = lambda i: (0, 0)
    return pl.pallas_call(
        _mem_kv_body,
        grid=(R // tm,),
        in_specs=[pl.BlockSpec((tm, D), tok), pl.BlockSpec(wk.shape, const),
                  pl.BlockSpec(wv.shape, const)],
        out_specs=[pl.BlockSpec((tm, D), tok), pl.BlockSpec((tm, D), tok)],
        out_shape=[jax.ShapeDtypeStruct((R, D), BF16), jax.ShapeDtypeStruct((R, D), BF16)],
        compiler_params=_params("arbitrary"),
        name="mem_kv",
    )(memf, wk, wv)


def _mem_attn_body(x0_ref, on_ref, om_ref, wmix_ref, g1_ref, b1_ref,
                   k_ref, v_ref, wq_ref, wo_ref, g_ref, b_ref, o_ref, *, alpha):
    half = on_ref.shape[1]
    mix = _dot(on_ref[...], wmix_ref[:half, :]) + _dot(om_ref[...], wmix_ref[half:, :])
    x = _layer_norm(alpha * x0_ref[...] + mix, g1_ref[...], b1_ref[...])
    D = x.shape[1]
    dh = D // MEM_HEADS
    q = (_dot(x.astype(BF16), wq_ref[...]) * (dh ** -0.5 * LOG2E)).astype(BF16)
    outs = []
    for h in range(MEM_HEADS):
        cs = slice(dh * h, dh * (h + 1))
        s = _dot_nt(q[:, cs], k_ref[0, :, cs])
        p = jnp.exp2(s - jnp.max(s, axis=1, keepdims=True))
        l = jnp.sum(p, axis=1, keepdims=True)
        outs.append((_dot(p.astype(BF16), v_ref[0, :, cs]) * (1.0 / l)).astype(BF16))
    o = jnp.concatenate(outs, axis=1)
    y = _dot(o, wo_ref[...])
    o_ref[...] = _layer_norm(alpha * x + y, g_ref[...], b_ref[...])


def _mem_attn(xf, o_nsa, o_mla, w_o, g1, b1, k_mem, v_mem, wq, wo, g, b, alpha, S, tm):
    T, D = xf.shape
    nst = S // tm
    M = k_mem.shape[1]
    tok = lambda i: (i, 0)
    const = lambda i: (0, 0)
    memb = lambda i: (i // nst, 0, 0)
    return pl.pallas_call(
        functools.partial(_mem_attn_body, alpha=alpha),
        grid=(T // tm,),
        in_specs=[
            pl.BlockSpec((tm, D), tok),
            pl.BlockSpec((tm, o_nsa.shape[1]), tok),
            pl.BlockSpec((tm, o_mla.shape[1]), tok),
            pl.BlockSpec(w_o.shape, const),
            pl.BlockSpec((1, D), const),
            pl.BlockSpec((1, D), const),
            pl.BlockSpec((1, M, D), memb),
            pl.BlockSpec((1, M, D), memb),
            pl.BlockSpec(wq.shape, const),
            pl.BlockSpec(wo.shape, const),
            pl.BlockSpec((1, D), const),
            pl.BlockSpec((1, D), const),
        ],
        out_specs=pl.BlockSpec((tm, D), tok),
        out_shape=jax.ShapeDtypeStruct((T, D), F32),
        compiler_params=_params("arbitrary"),
        name="mem_attn",
    )(xf, o_nsa, o_mla, w_o, g1, b1, k_mem, v_mem, wq, wo, g, b)


HALO = 16


FFN_SLAB = 256
FFN_TM = 512


def _ffn_body(x_ref, xh_ref, wg_ref, wu_ref, cw_ref, cb_ref, wd_ref, g_ref, b_ref, o_ref,
              act_ref, *, alpha, seq_tiles):
    i = pl.program_id(0)
    x = x_ref[...]
    xb = x.astype(BF16)
    xe = jnp.concatenate([xh_ref[...].astype(BF16), xb], axis=0)
    tm = x.shape[0]
    row = lax.broadcasted_iota(jnp.int32, (tm, 1), 0)
    seq_start = i % seq_tiles == 0
    for c0 in range(0, wg_ref.shape[1], FFN_SLAB):
        cs = slice(c0, c0 + FFN_SLAB)
        gate_e = _dot(xe, wg_ref[:, cs])
        gate = gate_e[HALO:]
        up = _dot(xb, wu_ref[:, cs])
        halo = jnp.where(seq_start, 0.0, gate_e[:HALO])
        g1 = jnp.where(row == 0, halo[HALO - 1:HALO], pltpu.roll(gate, 1, 0))
        g2 = jnp.where(row == 0, halo[HALO - 2:HALO - 1],
                       jnp.where(row == 1, halo[HALO - 1:HALO], pltpu.roll(gate, 2, 0)))
        conv = cw_ref[0:1, cs] * g2 + cw_ref[1:2, cs] * g1 + cw_ref[2:3, cs] * gate + cb_ref[:, cs]
        act_ref[:, cs] = (conv * (1.0 / (1.0 + jnp.exp(-conv))) * up).astype(BF16)
    y = _dot(act_ref[...], wd_ref[...])
    o_ref[...] = _layer_norm(alpha * x + y, g_ref[...], b_ref[...])


def _ffn(xf, wg, wu, cw, cb, wd, g, b, alpha, S, tm):
    T, D = xf.shape
    dff = wg.shape[1]
    assert dff % FFN_SLAB == 0
    tok = lambda i: (i, 0)
    const = lambda i: (0, 0)
    return pl.pallas_call(
        functools.partial(_ffn_body, alpha=alpha, seq_tiles=S // tm),
        grid=(T // tm,),
        in_specs=[
            pl.BlockSpec((tm, D), tok),
            pl.BlockSpec((HALO, D), lambda i: (jnp.maximum(i * (tm // HALO) - 1, 0), 0)),
            pl.BlockSpec((D, dff), const),
            pl.BlockSpec((D, dff), const),
            pl.BlockSpec((CONV_WIDTH, dff), const),
            pl.BlockSpec((1, dff), const),
            pl.BlockSpec((dff, D), const),
            pl.BlockSpec((1, D), const),
            pl.BlockSpec((1, D), const),
        ],
        out_specs=pl.BlockSpec((tm, D), tok),
        out_shape=jax.ShapeDtypeStruct((T, D), F32),
        scratch_shapes=[pltpu.VMEM((tm, dff), BF16)],
        compiler_params=_params("arbitrary"),
        name="ffn",
    )(xf, xf, wg, wu, cw, cb, wd, g, b)


def _inv_freq_row(dim, lane_lo, lane_hi, period):
    inv = ROPE_THETA ** (-np.arange(0, dim, 2, dtype=np.float64) / dim)
    row = np.zeros((1, LANES), np.float32)
    for lane in range(lane_lo, lane_hi):
        row[0, lane] = inv[(lane % period) % (dim // 2)]
    return jnp.asarray(row)


def _cover_t(S):
    nc = S // CMP_STRIDE
    ns = S // SLC_LEN
    cs = np.arange(nc)[:, None] * CMP_STRIDE
    ss = np.arange(ns)[None, :] * SLC_LEN
    cover = np.clip(np.minimum(cs + CMP_LEN, ss + SLC_LEN) - np.maximum(cs, ss), 0, None) / CMP_LEN
    cover[nc - 1:] = 0.0
    return jnp.asarray(cover.T, dtype=BF16)


def _permute_w_in(w):
    D = w.shape[0]
    c1 = NSA_HEADS * NSA_DH
    c2 = c1 + 3 * 2 * NSA_GROUPS * NSA_DH
    c3 = c2 + 3 * NSA_HEADS
    c4 = c3 + MLA_Q_RANK
    c5 = c4 + MLA_KV_RANK
    c6 = c5 + MLA_ROPE
    half = MLA_ROPE // 2
    z = lambda n: jnp.zeros((D, n), w.dtype)
    misc = jnp.concatenate(
        [w[:, c5 + half:c6], z(GATE_LANE0 - half), w[:, c2:c3], z(MLA_PE1 - GATE_LANE0 - (c3 - c2)),
         w[:, c5:c5 + half], z(LANES - MLA_PE1 - half)], axis=1)
    return jnp.concatenate([w[:, :c2], w[:, c3:c5], misc], axis=1).astype(BF16)


def _mla_head_lanes(nope, pe):
    r, H, _ = nope.shape
    half = MLA_ROPE // 2
    split = MLA_PE1 - half
    pad = jnp.zeros((r, H, LANES - MLA_NOPE - MLA_ROPE), nope.dtype)
    return jnp.concatenate([pe[..., half:], nope[..., :split], pe[..., :half], nope[..., split:], pad],
                           axis=2).reshape(r, H * LANES)


def _permute_w_uq(w):
    r = w.shape[0]
    w3 = w.reshape(r, MLA_HEADS, MLA_NOPE + MLA_ROPE)
    return _mla_head_lanes(w3[..., :MLA_NOPE], w3[..., MLA_NOPE:]).astype(BF16)


def _permute_w_ukv(w):
    r = w.shape[0]
    w3 = w.reshape(r, MLA_HEADS, MLA_NOPE + MLA_V)
    k = _mla_head_lanes(w3[..., :MLA_NOPE], jnp.zeros((r, MLA_HEADS, MLA_ROPE), w.dtype))
    v = _pad_lanes(w3[..., MLA_NOPE:]).reshape(r, MLA_HEADS // 2, 2, LANES)
    v = jnp.stack([v[:, :, 0], jnp.roll(v[:, :, 1], LANES - MLA_V, axis=-1)], axis=2)
    return jnp.concatenate([k, v.reshape(r, MLA_HEADS * LANES)], axis=1).astype(BF16)


def _permute_w_o(w):
    n = NSA_HEADS * NSA_DH
    wn = w[:n].reshape(NSA_GROUPS, NSA_REP, NSA_DH, -1).transpose(1, 0, 2, 3).reshape(n, -1)
    return jnp.concatenate([wn, w[n:]], axis=0).astype(BF16)


def _pad_lanes(a):
    return jnp.concatenate([a, jnp.zeros(a.shape[:-1] + (LANES - a.shape[-1],), a.dtype)], axis=-1)


def kernel(x, mem, positions, w_in, nsa_k_pos, nsa_ck_w1, nsa_ck_b1, nsa_ck_w2, nsa_ck_b2,
           nsa_v_pos, nsa_cv_w1, nsa_cv_b1, nsa_cv_w2, nsa_cv_b2,
           mla_q_norm, mla_w_uq, mla_kv_norm, mla_w_ukv, w_o, ln1_g, ln1_b,
           mem_wq, mem_wk, mem_wv, mem_wo, ln2_g, ln2_b,
           ffn_w_up, ffn_conv_w, ffn_conv_b, ffn_w_down, ln3_g, ln3_b):
    B, S, D = x.shape
    T = B * S
    depth = w_in.shape[0]
    alpha = (2.0 * depth) ** 0.25
    d_ff = ffn_w_down.shape[1]
    tm = min(512, S)
    assert S % MLA_TQ == 0 and S >= WIN_SPAN and S % tm == 0
    assert (B * mem.shape[1]) % 256 == 0

    pos = positions.reshape(T, 1)
    pos_cmp = positions[:, CMP_LEN - 1::CMP_STRIDE]
    pos_cmp = jnp.concatenate([pos_cmp, pos_cmp[:, -1:]], axis=1)[:, :, None]
    inv_cmp = _inv_freq_row(NSA_DH, 0, NSA_DH, NSA_DH)
    inv_tok = (_inv_freq_row(NSA_DH, 0, NSA_DH // 2, NSA_DH)
               + _inv_freq_row(MLA_ROPE, NSA_DH // 2, NSA_DH // 2 + MLA_ROPE // 2, MLA_ROPE // 2))
    cov_t = _cover_t(S)
    memf = mem.reshape(B * mem.shape[1], D)

    xf = x.reshape(T, D)
    for l in range(depth):
        qn, kvn, kvc, misc, q_m, k_m, v_m = _inproj(
            pos, inv_tok, xf, _permute_w_in(w_in[l]), mla_q_norm[l][None, :],
            mla_kv_norm[l][None, :], _permute_w_uq(mla_w_uq[l]), _permute_w_ukv(mla_w_ukv[l]),
            B, S, tm)
        kvcmp = _compress(
            pos_cmp, inv_cmp, kvc,
            *_compress_weights(nsa_k_pos[l], nsa_ck_w1[l], nsa_ck_b1[l], nsa_ck_w2[l], nsa_ck_b2[l],
                               nsa_v_pos[l], nsa_cv_w1[l], nsa_cv_b1[l], nsa_cv_w2[l], nsa_cv_b2[l]),
            B, S)
        o_nsa = _nsa(qn, kvn, kvcmp, misc, cov_t, B, S)
        o_mla = _mla(q_m, k_m, v_m, B, S)
        k_mem, v_mem = _mem_kv(memf, mem_wk[l].astype(BF16), mem_wv[l].astype(BF16), 256)
        xf = _mem_attn(xf, o_nsa.reshape(T, -1), o_mla.reshape(T, -1), _permute_w_o(w_o[l]),
                       ln1_g[l][None, :], ln1_b[l][None, :],
                       k_mem.reshape(B, -1, D), v_mem.reshape(B, -1, D),
                       mem_wq[l].astype(BF16), mem_wo[l].astype(BF16),
                       ln2_g[l][None, :], ln2_b[l][None, :], alpha, S, min(MEM_TM, S))
        xf = _ffn(xf, ffn_w_up[l][:, :d_ff].astype(BF16), ffn_w_up[l][:, d_ff:].astype(BF16),
                  ffn_conv_w[l], ffn_conv_b[l][None, :], ffn_w_down[l].astype(BF16),
                  ln3_g[l][None, :], ln3_b[l][None, :], alpha, S, FFN_TM)
    return xf.reshape(B, S, D)
```

```python
import functools
import math

import numpy as np
import jax
import jax.numpy as jnp
from jax import lax
from jax.experimental import pallas as pl
from jax.experimental.pallas import tpu as pltpu

F32 = jnp.float32
BF16 = jnp.bfloat16

NSA_HEADS = 8
NSA_GROUPS = 2
NSA_REP = NSA_HEADS // NSA_GROUPS
NSA_DH = 64
CMP_STRIDE = 16
CMP_LEN = 32
SLC_LEN = 64
TOPN = 16
WINDOW = 512
CMP_HIDDEN = 128
FORCE_BONUS = 1e4
MLA_HEADS = 8
MLA_Q_RANK = 384
MLA_KV_RANK = 256
MLA_NOPE = 64
MLA_ROPE = 32
MLA_V = 64
MEM_HEADS = 4
CONV_WIDTH = 3
ROPE_THETA = 10000.0
LN_EPS = 1e-5
RMS_EPS = 1e-6
NEG_INF = -1e30
LOG2E = math.log2(math.e)

LANES = 128
VMEM_LIMIT = 60 * 1024 * 1024

C_Q = 0
C_KVC = 512
C_KVN = 768
C_LAT = 1280
C_MISC = 1920
IN_COLS_PAD = 2048
GATE_LANE0 = MLA_ROPE
MLA_PE2 = 0
MLA_PE1 = 64
ONES_LANE = 64
ONES_LANE_G1 = 0


def _dot(a, b):
    return jnp.dot(a, b, preferred_element_type=F32)


def _dot_nt(a, b):
    return lax.dot_general(a, b, (((1,), (1,)), ((), ())), preferred_element_type=F32)


def _layer_norm(y, g, b):
    mu = jnp.mean(y, axis=-1, keepdims=True)
    d = y - mu
    var = jnp.mean(d * d, axis=-1, keepdims=True)
    return d * lax.rsqrt(var + LN_EPS) * g + b


def _params(*sem):
    return pltpu.CompilerParams(dimension_semantics=sem, vmem_limit_bytes=VMEM_LIMIT)


def _rope_tables(pos_col, inv_row, half):
    ang = pos_col * inv_row
    cos = jnp.cos(ang)
    sin = jnp.sin(ang)
    lane = lax.broadcasted_iota(jnp.int32, (1, LANES), 1)
    upper = (lane & (2 * half - 1)) >= half
    rot = inv_row != 0.0
    sin_hi = jnp.where(upper & rot, sin, 0.0)
    sin_lo = jnp.where(upper | (~rot), 0.0, -sin)
    return cos, sin_hi, sin_lo


def _apply_rope(v, tabs, half):
    cos, sin_hi, sin_lo = tabs
    return v * cos + pltpu.roll(v, half, 1) * sin_hi + pltpu.roll(v, LANES - half, 1) * sin_lo


def _rms_norm(v, g):
    return v * lax.rsqrt(jnp.mean(v * v, axis=-1, keepdims=True) + RMS_EPS) * g


def _inproj_body(pos_ref, inv_ref, x_ref, w_ref, gq_ref, gkv_ref, wq_ref, wkv_ref,
                 qn_ref, kvn_ref, kvc_ref, misc_ref, qm_ref, km_ref, vm_ref):
    xb = x_ref[...].astype(BF16)
    lane = lax.broadcasted_iota(jnp.int32, (1, LANES), 1)
    low = lane < NSA_DH
    qscale = NSA_DH ** -0.5 * LOG2E

    ang = pos_ref[...].astype(F32) * inv_ref[...]
    cos, sin = jnp.cos(ang), jnp.sin(ang)
    nf, mf = NSA_DH // 2, MLA_ROPE // 2

    def tile_nsa(t):
        t = jnp.where(lane < nf, t, 0.0)
        t = t + pltpu.roll(t, nf, 1)
        return t + pltpu.roll(t, 2 * nf, 1)

    def place_mla(t):
        t = jnp.where((lane >= nf) & (lane < nf + mf), t, 0.0)
        return pltpu.roll(t, MLA_PE1 - nf, 1) + pltpu.roll(t, LANES + MLA_PE2 - nf, 1)

    upper = (lane & (NSA_DH - 1)) >= nf
    sin_n = tile_nsa(sin)
    tabs = (tile_nsa(cos), jnp.where(upper, sin_n, 0.0), jnp.where(upper, 0.0, -sin_n))
    pe1 = (lane >= MLA_PE1) & (lane < MLA_PE1 + mf)
    pe2 = (lane >= MLA_PE2) & (lane < MLA_PE2 + mf)
    pe_lanes = pe1 | pe2
    sin_m = place_mla(sin)
    cos_m = jnp.where(pe_lanes, place_mla(cos), 1.0)
    sin_m = jnp.where(pe1, -sin_m, jnp.where(pe2, sin_m, 0.0))

    def rope_mla(v):
        return v * cos_m + pltpu.roll(v, LANES // 2, 1) * sin_m

    def proj(c0, n):
        return _dot(xb, w_ref[:, c0:c0 + n])

    def split_store(v, ref, idx_lo, idx_hi, pad=0.0):
        ref[0, idx_lo] = jnp.where(low, v, pad).astype(BF16)
        ref[0, idx_hi] = jnp.where(low, pltpu.roll(v, NSA_DH, 1), pad).astype(BF16)

    ones_pad = jnp.where(lane == ONES_LANE, 1.0, 0.0)
    ones_pad_g1 = jnp.where(lane == ONES_LANE_G1, 1.0, 0.0)

    for slab in range(2):
        h = proj(C_Q + 256 * slab, 256)
        for j in range(2):
            r = _apply_rope(h[:, LANES * j:LANES * (j + 1)], tabs, NSA_DH // 2) * qscale
            split_store(r, qn_ref, 4 * slab + 2 * j, 4 * slab + 2 * j + 1)

    h = proj(C_KVC, 256)
    kvc_ref[0] = h[:, :LANES]
    kvc_ref[1] = h[:, LANES:]

    for slab in range(2):
        h = proj(C_KVN + 256 * slab, 256)
        k = _apply_rope(h[:, :LANES], tabs, NSA_DH // 2)
        split_store(k, kvn_ref, 4 * slab, 4 * slab + 1)
        v = h[:, LANES:]
        kvn_ref[0, 4 * slab + 2] = jnp.where(low, v, ones_pad).astype(BF16)
        kvn_ref[0, 4 * slab + 3] = jnp.where(low, ones_pad_g1, v).astype(BF16)

    lat = [proj(C_LAT + 256 * i, 256) for i in range((IN_COLS_PAD - C_LAT) // 256)]
    misc = lat[2][:, LANES:]
    misc_ref[...] = misc
    mq = jnp.concatenate([lat[0], lat[1][:, :LANES]], axis=1)
    mkv = jnp.concatenate([lat[1][:, LANES:], lat[2][:, :LANES]], axis=1)
    mscale = (MLA_NOPE + MLA_ROPE) ** -0.5 * LOG2E
    qn = _rms_norm(mq, gq_ref[...]).astype(BF16)
    kvn = _rms_norm(mkv, gkv_ref[...]).astype(BF16)
    kpe = jnp.where(pe_lanes, rope_mla(misc), 0.0)
    kcols = MLA_HEADS * LANES
    for slab in range(MLA_HEADS // 2):
        cs = slice(256 * slab, 256 * (slab + 1))
        hq = _dot(qn, wq_ref[:, cs])
        hk = _dot(kvn, wkv_ref[:, cs])
        hv = _dot(kvn, wkv_ref[:, kcols + 256 * slab:kcols + 256 * (slab + 1)])
        for j in range(2):
            ls = slice(LANES * j, LANES * (j + 1))
            qm_ref[0, 2 * slab + j] = (rope_mla(hq[:, ls]) * mscale).astype(BF16)
            km_ref[0, 2 * slab + j] = (hk[:, ls] + kpe).astype(BF16)
            vm_ref[0, 2 * slab + j] = (hv[:, ls] + (ones_pad_g1 if j else ones_pad)).astype(BF16)


def _inproj(pos, inv_row, xf, w_in_p, gq, gkv, wq_p, wkv_p, B, S, tm):
    T = B * S
    nst = S // tm
    tok = lambda i: (i, 0)
    const = lambda i: (0, 0)
    head_blk = lambda i: (i // nst, 0, i % nst, 0)
    heads = jax.ShapeDtypeStruct((B, 8, S, LANES), BF16)
    return pl.pallas_call(
        _inproj_body,
        grid=(T // tm,),
        in_specs=[
            pl.BlockSpec((tm, 1), tok),
            pl.BlockSpec((1, LANES), const),
            pl.BlockSpec((tm, xf.shape[1]), tok),
            pl.BlockSpec(w_in_p.shape, const),
            pl.BlockSpec(gq.shape, const),
            pl.BlockSpec(gkv.shape, const),
            pl.BlockSpec(wq_p.shape, const),
            pl.BlockSpec(wkv_p.shape, const),
        ],
        out_specs=[
            pl.BlockSpec((1, 8, tm, LANES), head_blk),
            pl.BlockSpec((1, 8, tm, LANES), head_blk),
            pl.BlockSpec((2, tm, LANES), lambda i: (0, i, 0)),
            pl.BlockSpec((tm, LANES), tok),
            pl.BlockSpec((1, 8, tm, LANES), head_blk),
            pl.BlockSpec((1, 8, tm, LANES), head_blk),
            pl.BlockSpec((1, 8, tm, LANES), head_blk),
        ],
        out_shape=[
            heads,
            heads,
            jax.ShapeDtypeStruct((2, T, LANES), F32),
            jax.ShapeDtypeStruct((T, LANES), F32),
            heads,
            heads,
            heads,
        ],
        compiler_params=_params("arbitrary"),
        name="inproj",
    )(pos, inv_row, xf, w_in_p, gq, gkv, wq_p, wkv_p)


def _compress_body(pos_ref, inv_ref, x_ref, pe_ref, w1_ref, b1_ref, w2_ref, b2_ref, o_ref):
    is_k = pl.program_id(0) == 0
    nch = o_ref.shape[2]
    a1 = jnp.zeros((nch, NSA_GROUPS * CMP_HIDDEN), F32)
    a2 = jnp.zeros((nch, NSA_GROUPS * CMP_HIDDEN), F32)
    for l in range(CMP_STRIDE):
        xl = x_ref.at[0, 0][pl.ds(l, nch, stride=CMP_STRIDE), :]
        a1 = a1 + _dot((xl + pe_ref[0, l:l + 1, :]).astype(BF16), w1_ref[0, l])
        a2 = a2 + _dot((xl + pe_ref[0, CMP_STRIDE + l:CMP_STRIDE + l + 1, :]).astype(BF16),
                       w1_ref[0, CMP_STRIDE + l])
    pre = a1 + pltpu.roll(a2, nch - 1, 0) + b1_ref[0]
    hid = jax.nn.gelu(pre, approximate=True)
    out = _dot(hid.astype(BF16), w2_ref[0]) + b2_ref[0]
    tabs = _rope_tables(pos_ref[0].astype(F32), inv_ref[...], NSA_DH // 2)
    row = lax.broadcasted_iota(jnp.int32, (nch, 1), 0)
    for g in range(NSA_GROUPS):
        og = out[:, LANES * g:LANES * (g + 1)]
        og = jnp.where(is_k, _apply_rope(og, tabs, NSA_DH // 2), og)
        o_ref[g, 0] = jnp.where(row < nch - 1, og, 0.0).astype(BF16)


def _compress(pos_cmp, inv_nsa, kvc, pe, w1, b1, w2, b2, B, S):
    nch = S // CMP_STRIDE
    x = kvc.reshape(2, B, S, LANES)
    kv = lambda j, b: (j, 0, 0)
    kv4 = lambda j, b: (j, 0, 0, 0)
    return pl.pallas_call(
        _compress_body,
        grid=(2, B),
        in_specs=[
            pl.BlockSpec((1, nch, 1), lambda j, b: (b, 0, 0)),
            pl.BlockSpec((1, LANES), lambda j, b: (0, 0)),
            pl.BlockSpec((1, 1, S, LANES), lambda j, b: (j, b, 0, 0)),
            pl.BlockSpec((1,) + pe.shape[1:], kv),
            pl.BlockSpec((1,) + w1.shape[1:], kv4),
            pl.BlockSpec((1,) + b1.shape[1:], kv),
            pl.BlockSpec((1,) + w2.shape[1:], kv),
            pl.BlockSpec((1,) + b2.shape[1:], kv),
        ],
        out_specs=pl.BlockSpec((NSA_GROUPS, 1, nch, LANES), lambda j, b: (j, b, 0, 0)),
        out_shape=jax.ShapeDtypeStruct((2 * NSA_GROUPS, B, nch, LANES), BF16),
        compiler_params=_params("arbitrary", "arbitrary"),
        name="compress",
    )(pos_cmp, inv_nsa, x, pe, w1, b1, w2, b2)


def _compress_weights(k_pos, k_w1, k_b1, k_w2, k_b2, v_pos, v_w1, v_b1, v_w2, v_b2):
    def one(pos, w1, b1, w2, b2, g1_high):
        w1l = w1.reshape(CMP_LEN, NSA_DH, CMP_HIDDEN)
        z1 = jnp.zeros_like(w1l)
        w1bd = jnp.concatenate([jnp.concatenate([w1l, z1], axis=2),
                                jnp.concatenate([z1, w1l], axis=2)], axis=1)
        pad = lambda a: _pad_lanes(a)
        pad1 = (lambda a: jnp.roll(_pad_lanes(a), LANES - a.shape[-1], axis=-1)) if g1_high else pad
        z2 = jnp.zeros_like(pad(w2))
        w2bd = jnp.concatenate([jnp.concatenate([pad(w2), z2], axis=1),
                                jnp.concatenate([z2, pad1(w2)], axis=1)], axis=0)
        return (jnp.tile(pos, (1, NSA_GROUPS)), w1bd.astype(BF16), jnp.tile(b1, NSA_GROUPS)[None, :],
                w2bd.astype(BF16), jnp.concatenate([pad(b2), pad1(b2)])[None, :])
    k = one(k_pos, k_w1, k_b1, k_w2, k_b2, False)
    v = one(v_pos, v_w1, v_b1, v_w2, v_b2, True)
    return tuple(jnp.stack([a, b]) for a, b in zip(k, v))


STRIP = 64


def _lane_tile(col, n):
    reps = [col] * (n // LANES)
    if n % LANES:
        reps.append(col[:, :n % LANES])
    return reps[0] if len(reps) == 1 else jnp.concatenate(reps, axis=1)


def _flash_reset(m_ref, acc_ref):
    m_ref[...] = jnp.full(m_ref.shape, NEG_INF, F32)
    acc_ref[...] = jnp.zeros(acc_ref.shape, F32)


def _flash_update(s_ref, v, m_ref, acc_ref, p_ref, mask=None):
    rows, n = s_ref.shape
    for r in range(rows // STRIP):
        rs = slice(STRIP * r, STRIP * (r + 1))
        s = s_ref[rs, :]
        if mask is not None:
            s = jnp.where(mask[rs], s, NEG_INF)
        if m_ref is None:
            p_ref[rs, :] = jnp.exp2(s - jnp.max(s, axis=1, keepdims=True)).astype(BF16)
            continue
        m_old = m_ref[rs, :]
        m_new = jnp.maximum(m_old, jnp.max(s, axis=1, keepdims=True))
        p_ref[rs, :] = jnp.exp2(s - _lane_tile(m_new, n)).astype(BF16)
        acc_ref[rs, :] = jnp.exp2(m_old - m_new) * acc_ref[rs, :]
        m_ref[rs, :] = m_new
    if m_ref is None:
        acc_ref[...] = _dot(p_ref[...], v)
    else:
        acc_ref[...] += _dot(p_ref[...], v)


NSA_TQ = 256
NSA_ROWS = NSA_REP * NSA_TQ
SLC_CHUNK = 512
WIN_SPAN = WINDOW + NSA_TQ
BIAS_LANE0 = LANES


def _nsa_body(q_ref, kvn_ref, kvc_ref, misc_ref, cov_ref, o_ref,
              kaug_ref, score_ref, qbias_ref, sa_ref, sb_ref, pa_ref, pb_ref, ms_ref, accs_ref,
              sw_ref, pw_ref, accw_ref, sc_ref, pn_ref, pc_ref, oc_ref):
    c = pl.program_id(1)
    rows = NSA_ROWS
    t_row = c * NSA_TQ + (lax.broadcasted_iota(jnp.int32, (rows, 1), 0) & (NSA_TQ - 1))
    ncmp = kvc_ref.shape[2]
    nblk = kaug_ref.shape[1] // SLC_LEN

    @pl.when(c == 0)
    def _():
        nkeys = kaug_ref.shape[1]
        key_blk = lax.broadcasted_iota(jnp.int32, (nkeys, LANES), 0) // SLC_LEN
        onehot = jnp.where(key_blk == lax.broadcasted_iota(jnp.int32, (nkeys, LANES), 1), 1.0, 0.0)
        for g in range(NSA_GROUPS):
            kaug_ref[g, :, :BIAS_LANE0] = kvn_ref[0, g]
            kaug_ref[g, :, BIAS_LANE0:] = onehot.astype(BF16)

    qs = [q_ref[0, NSA_REP * g:NSA_REP * (g + 1)].reshape(rows, LANES) for g in range(NSA_GROUPS)]

    cmp_valid = (CMP_STRIDE * lax.broadcasted_iota(jnp.int32, (1, ncmp), 1) + CMP_LEN - 1) <= t_row
    for g in range(NSA_GROUPS):
        sc_ref[g] = _dot_nt(qs[g], kvc_ref[g, 0])
    win_start = pl.multiple_of(jnp.maximum(c * NSA_TQ - WINDOW, 0), NSA_TQ)

    def window_scores(g):
        sw_ref[g] = _dot_nt(qs[g], kvn_ref[0, 4 + g, pl.ds(win_start, WIN_SPAN), :])

    window_scores(0)
    for g in range(NSA_GROUPS):
        for r in range(rows // STRIP):
            rs = slice(STRIP * r, STRIP * (r + 1))
            s = jnp.where(cmp_valid[rs], sc_ref[g, rs, :], NEG_INF)
            p = jnp.exp2(s - jnp.max(s, axis=1, keepdims=True))
            l = jnp.sum(p, axis=1, keepdims=True)
            p = p * jnp.where(t_row[rs] >= CMP_LEN - 1, 1.0 / l, 0.0)
            pn_ref[g, rs, :] = p
            pc_ref[g, rs, :] = p.astype(BF16)
        oc_ref[g] = _dot(pc_ref[g], kvc_ref[NSA_GROUPS + g, 0])

    def select(nb):
        window_scores(1)
        width = NSA_GROUPS * NSA_TQ
        ps = jnp.concatenate(
            [sum(pn_ref[g, NSA_TQ * r:NSA_TQ * (r + 1), :] for r in range(NSA_REP))
             for g in range(NSA_GROUPS)], axis=0)
        hi = ps.astype(BF16)
        lo = (ps - hi.astype(F32)).astype(BF16)
        imp = _dot_nt(cov_ref[:nb, :], hi) + _dot_nt(cov_ref[:nb, :], lo)
        jidx = lax.broadcasted_iota(jnp.int32, (nb, width), 0)
        lane_q = lax.broadcasted_iota(jnp.int32, (1, width), 1) & (NSA_TQ - 1)
        cur = c * (NSA_TQ // SLC_LEN) + lane_q // SLC_LEN
        forced = (jidx == 0) | (jidx == cur) | (jidx == cur - 1)
        score = jnp.where(jidx <= cur, jnp.where(forced, FORCE_BONUS, imp), NEG_INF)
        score_ref[:nb, :] = score
        sub = 8
        cnt = [jnp.zeros((sub, width), F32) for _ in range(nb // sub)]
        tiles = [score[sub * v:sub * (v + 1)] for v in range(nb // sub)]
        sidx = lax.broadcasted_iota(jnp.int32, (sub, width), 0)
        for jp in range(nb):
            rowv = jnp.broadcast_to(score_ref[jp:jp + 1, :], (sub, width))
            for v in range(nb // sub):
                if sub * v > jp:
                    cnt[v] = jnp.where(rowv >= tiles[v], cnt[v] + 1.0, cnt[v])
                elif sub * v + sub - 1 <= jp:
                    cnt[v] = jnp.where(rowv > tiles[v], cnt[v] + 1.0, cnt[v])
                else:
                    ge = jnp.where(rowv >= tiles[v], cnt[v] + 1.0, cnt[v])
                    gt = jnp.where(rowv > tiles[v], cnt[v] + 1.0, cnt[v])
                    cnt[v] = jnp.where(sidx + sub * v > jp, ge, gt)
        rank = jnp.concatenate(cnt, axis=0)
        bias = jnp.where(rank < float(TOPN), 0.0, NEG_INF)
        bias = jnp.concatenate([bias, jnp.zeros((LANES - nb, width), F32)], axis=0)
        qbias_ref[...] = bias.T.astype(BF16)

    visible = (c + 1) * (NSA_TQ // SLC_LEN)

    @pl.when(visible <= TOPN)
    def _():
        window_scores(1)
        qbias_ref[...] = jnp.zeros(qbias_ref.shape, BF16)

    bounds = [TOPN] + list(range(TOPN + 8, nblk, 8)) + [nblk]
    for lo_nb, nb in zip(bounds[:-1], bounds[1:]):
        pl.when((visible > lo_nb) & (visible <= nb))(functools.partial(select, nb))

    _flash_reset(ms_ref, accs_ref)
    qas = [jnp.concatenate(
        [qs[g], jnp.concatenate([qbias_ref[NSA_TQ * g:NSA_TQ * (g + 1), :]] * NSA_REP, axis=0)],
        axis=1) for g in range(NSA_GROUPS)]
    last = c // (SLC_CHUNK // NSA_TQ)

    def slc_scores(kc, dst):
        k0 = pl.multiple_of(kc * SLC_CHUNK, SLC_CHUNK)
        for g in range(NSA_GROUPS):
            dst[g] = _dot_nt(qas[g], kaug_ref[g, pl.ds(k0, SLC_CHUNK), :])

    def slc_update(src, p_ref, kc, causal):
        k0 = pl.multiple_of(kc * SLC_CHUNK, SLC_CHUNK)
        mask = None
        if causal:
            mask = (k0 + lax.broadcasted_iota(jnp.int32, (1, SLC_CHUNK), 1)) <= t_row
        for g in range(NSA_GROUPS):
            _flash_update(src.at[g], kvn_ref[0, 2 + g, pl.ds(k0, SLC_CHUNK), :],
                          ms_ref.at[g], accs_ref.at[g], p_ref.at[g], mask)

    slc_scores(0, sa_ref)

    diff = t_row - (win_start + lax.broadcasted_iota(jnp.int32, (1, WIN_SPAN), 1))
    win_valid = lax.bitcast_convert_type(diff, jnp.uint32) < jnp.uint32(WINDOW)
    for g in range(NSA_GROUPS):
        _flash_update(sw_ref.at[g], kvn_ref[0, 6 + g, pl.ds(win_start, WIN_SPAN), :],
                      None, accw_ref.at[g], pw_ref.at[g], win_valid)

    def slc_pair(i, carry):
        slc_scores(2 * i + 1, sb_ref)
        slc_update(sa_ref, pa_ref, 2 * i, False)
        slc_scores(2 * i + 2, sa_ref)
        slc_update(sb_ref, pb_ref, 2 * i + 1, False)
        return carry
    lax.fori_loop(0, last // 2, slc_pair, 0)
    tail = 2 * (last // 2)

    @pl.when(last > tail)
    def _():
        slc_scores(tail + 1, sb_ref)
        slc_update(sa_ref, pa_ref, tail, False)
        slc_update(sb_ref, pb_ref, tail + 1, True)

    @pl.when(last == tail)
    def _():
        slc_update(sa_ref, pa_ref, tail, True)

    sig = 1.0 / (1.0 + jnp.exp(-misc_ref[...]))
    low = lax.broadcasted_iota(jnp.int32, (NSA_TQ, LANES), 1) < NSA_DH
    denom_lane = jnp.where(low, ONES_LANE, ONES_LANE_G1)

    def pair(ref, rs):
        a0, a1 = ref[0, rs, :], ref[1, rs, :]
        return jnp.where(low, a0, a1), jnp.where(low, a1, a0)

    outs = []
    for r in range(NSA_REP):
        rs = slice(NSA_TQ * r, NSA_TQ * (r + 1))
        gate = [jnp.take_along_axis(
            sig, jnp.where(low, GATE_LANE0 + 3 * r + br, GATE_LANE0 + 3 * (NSA_REP + r) + br), axis=1)
            for br in range(3)]
        o_cmp, _ = pair(oc_ref, rs)
        o_slc, l_slc = pair(accs_ref, rs)
        o_win, l_win = pair(accw_ref, rs)
        outs.append(gate[0] * o_cmp
                    + (gate[1] / jnp.take_along_axis(l_slc, denom_lane, axis=1)) * o_slc
                    + (gate[2] / jnp.take_along_axis(l_win, denom_lane, axis=1)) * o_win)
    o_ref[0] = jnp.concatenate(outs, axis=1).astype(BF16)


def _nsa(qn, kvn, kvcmp, misc, cov_t, B, S):
    nblk = S // SLC_LEN
    ncmp = S // CMP_STRIDE
    nq = S // NSA_TQ
    rows = NSA_ROWS
    return pl.pallas_call(
        _nsa_body,
        grid=(B, nq),
        in_specs=[
            pl.BlockSpec((1, NSA_HEADS, NSA_TQ, LANES), lambda b, c: (b, 0, c, 0)),
            pl.BlockSpec((1, 8, S, LANES), lambda b, c: (b, 0, 0, 0)),
            pl.BlockSpec((4, 1, ncmp, LANES), lambda b, c: (0, b, 0, 0)),
            pl.BlockSpec((NSA_TQ, LANES), lambda b, c: (b * nq + c, 0)),
            pl.BlockSpec(cov_t.shape, lambda b, c: (0, 0)),
        ],
        out_specs=pl.BlockSpec((1, NSA_TQ, NSA_HEADS * NSA_DH), lambda b, c: (b, c, 0)),
        out_shape=jax.ShapeDtypeStruct((B, S, NSA_HEADS * NSA_DH), BF16),
        scratch_shapes=[
            pltpu.VMEM((NSA_GROUPS, S, 2 * LANES), BF16),
            pltpu.VMEM((nblk, NSA_GROUPS * NSA_TQ), F32),
            pltpu.VMEM((NSA_GROUPS * NSA_TQ, LANES), BF16),
            pltpu.VMEM((NSA_GROUPS, rows, SLC_CHUNK), F32),
            pltpu.VMEM((NSA_GROUPS, rows, SLC_CHUNK), F32),
            pltpu.VMEM((NSA_GROUPS, rows, SLC_CHUNK), BF16),
            pltpu.VMEM((NSA_GROUPS, rows, SLC_CHUNK), BF16),
            pltpu.VMEM((NSA_GROUPS, rows, LANES), F32),
            pltpu.VMEM((NSA_GROUPS, rows, LANES), F32),
            pltpu.VMEM((NSA_GROUPS, rows, WIN_SPAN), F32),
            pltpu.VMEM((NSA_GROUPS, rows, WIN_SPAN), BF16),
            pltpu.VMEM((NSA_GROUPS, rows, LANES), F32),
            pltpu.VMEM((NSA_GROUPS, rows, ncmp), F32),
            pltpu.VMEM((NSA_GROUPS, rows, ncmp), F32),
            pltpu.VMEM((NSA_GROUPS, rows, ncmp), BF16),
            pltpu.VMEM((NSA_GROUPS, rows, LANES), F32),
        ],
        compiler_params=_params("arbitrary", "arbitrary"),
        name="nsa",
    )(qn, kvn, kvcmp, misc, cov_t)


MLA_TQ = 512
MLA_CHUNK = 512
MLA_HPB = 4


def _mla_body(q_ref, k_ref, v_ref, o_ref, sa_ref, sb_ref, pa_ref, pb_ref, m_ref, acc_ref):
    qi = pl.program_id(2)
    t_row = qi * MLA_TQ + lax.broadcasted_iota(jnp.int32, (MLA_TQ, 1), 0)
    _flash_reset(m_ref, acc_ref)

    def scores(kc, dst):
        k0 = pl.multiple_of(kc * MLA_CHUNK, MLA_CHUNK)
        for j in range(MLA_HPB):
            dst[j] = _dot_nt(q_ref[0, j], k_ref[0, j, pl.ds(k0, MLA_CHUNK), :])

    def update(src, p_ref, kc, causal):
        k0 = pl.multiple_of(kc * MLA_CHUNK, MLA_CHUNK)
        mask = None
        if causal:
            mask = (k0 + lax.broadcasted_iota(jnp.int32, (1, MLA_CHUNK), 1)) <= t_row
        for j in range(MLA_HPB):
            _flash_update(src.at[j], v_ref[0, j, pl.ds(k0, MLA_CHUNK), :],
                          m_ref.at[j], acc_ref.at[j], p_ref.at[j], mask)

    scores(0, sa_ref)

    def pair(i, carry):
        scores(2 * i + 1, sb_ref)
        update(sa_ref, pa_ref, 2 * i, False)
        scores(2 * i + 2, sa_ref)
        update(sb_ref, pb_ref, 2 * i + 1, False)
        return carry
    lax.fori_loop(0, qi // 2, pair, 0)
    tail = 2 * (qi // 2)

    @pl.when(qi > tail)
    def _():
        scores(tail + 1, sb_ref)
        update(sa_ref, pa_ref, tail, False)
        update(sb_ref, pb_ref, tail + 1, True)

    @pl.when(qi == tail)
    def _():
        update(sa_ref, pa_ref, tail, True)

    low = lax.broadcasted_iota(jnp.int32, (MLA_TQ, LANES), 1) < MLA_V
    denom_lane = jnp.where(low, ONES_LANE, ONES_LANE_G1)
    outs = []
    for j in range(0, MLA_HPB, 2):
        a0, a1 = acc_ref[j], acc_ref[j + 1]
        denom = jnp.take_along_axis(jnp.where(low, a1, a0), denom_lane, axis=1)
        outs.append(jnp.where(low, a0, a1) * (1.0 / denom))
    o_ref[0] = jnp.concatenate(outs, axis=1).astype(BF16)


def _mla(q, k, v, B, S):
    return pl.pallas_call(
        _mla_body,
        grid=(B, MLA_HEADS // MLA_HPB, S // MLA_TQ),
        in_specs=[
            pl.BlockSpec((1, MLA_HPB, MLA_TQ, LANES), lambda b, h, i: (b, h, i, 0)),
            pl.BlockSpec((1, MLA_HPB, S, LANES), lambda b, h, i: (b, h, 0, 0)),
            pl.BlockSpec((1, MLA_HPB, S, LANES), lambda b, h, i: (b, h, 0, 0)),
        ],
        out_specs=pl.BlockSpec((1, MLA_TQ, MLA_HPB * MLA_V), lambda b, h, i: (b, i, h)),
        out_shape=jax.ShapeDtypeStruct((B, S, MLA_HEADS * MLA_V), BF16),
        scratch_shapes=[
            pltpu.VMEM((MLA_HPB, MLA_TQ, MLA_CHUNK), F32),
            pltpu.VMEM((MLA_HPB, MLA_TQ, MLA_CHUNK), F32),
            pltpu.VMEM((MLA_HPB, MLA_TQ, MLA_CHUNK), BF16),
            pltpu.VMEM((MLA_HPB, MLA_TQ, MLA_CHUNK), BF16),
            pltpu.VMEM((MLA_HPB, MLA_TQ, LANES), F32),
            pltpu.VMEM((MLA_HPB, MLA_TQ, LANES), F32),
        ],
        compiler_params=_params("arbitrary", "arbitrary", "arbitrary"),
        name="mla",
    )(q, k, v)


MEM_TM = 1024


def _mem_kv_body(m_ref, wk_ref, wv_ref, k_ref, v_ref):
    mb = m_ref[...].astype(BF16)
    k_ref[...] = _dot(mb, wk_ref[...]).astype(BF16)
    v_ref[...] = _dot(mb, wv_ref[...]).astype(BF16)


def _mem_kv(memf, wk, wv, tm):
    R, D = memf.shape
    tok = lambda i: (i, 0)
    const = lambda i: (0, 0)
    return pl.pallas_call(
        _mem_kv_body,
        grid=(R // tm,),
        in_specs=[pl.BlockSpec((tm, D), tok), pl.BlockSpec(wk.shape, const),
                  pl.BlockSpec(wv.shape, const)],
        out_specs=[pl.BlockSpec((tm, D), tok), pl.BlockSpec((tm, D), tok)],
        out_shape=[jax.ShapeDtypeStruct((R, D), BF16), jax.ShapeDtypeStruct((R, D), BF16)],
        compiler_params=_params("arbitrary"),
        name="mem_kv",
    )(memf, wk, wv)


def _mem_attn_body(x0_ref, on_ref, om_ref, wmix_ref, g1_ref, b1_ref,
                   k_ref, v_ref, wq_ref, wo_ref, g_ref, b_ref, o_ref, *, alpha):
    half = on_ref.shape[1]
    mix = _dot(on_ref[...], wmix_ref[:half, :]) + _dot(om_ref[...], wmix_ref[half:, :])
    x = _layer_norm(alpha * x0_ref[...] + mix, g1_ref[...], b1_ref[...])
    D = x.shape[1]
    dh = D // MEM_HEADS
    q = (_dot(x.astype(BF16), wq_ref[...]) * (dh ** -0.5 * LOG2E)).astype(BF16)
    outs = []
    for h in range(MEM_HEADS):
        cs = slice(dh * h, dh * (h + 1))
        s = _dot_nt(q[:, cs], k_ref[0, :, cs])
        p = jnp.exp2(s - jnp.max(s, axis=1, keepdims=True))
        l = jnp.sum(p, axis=1, keepdims=True)
        outs.append((_dot(p.astype(BF16), v_ref[0, :, cs]) * (1.0 / l)).astype(BF16))
    o = jnp.concatenate(outs, axis=1)
    y = _dot(o, wo_ref[...])
    o_ref[...] = _layer_norm(alpha * x + y, g_ref[...], b_ref[...])


def _mem_attn(xf, o_nsa, o_mla, w_o, g1, b1, k_mem, v_mem, wq, wo, g, b, alpha, S, tm):
    T, D = xf.shape
    nst = S // tm
    M = k_mem.shape[1]
    tok = lambda i: (i, 0)
    const = lambda i: (0, 0)
    memb = lambda i: (i // nst, 0, 0)
    return pl.pallas_call(
        functools.partial(_mem_attn_body, alpha=alpha),
        grid=(T // tm,),
        in_specs=[
            pl.BlockSpec((tm, D), tok),
            pl.BlockSpec((tm, o_nsa.shape[1]), tok),
            pl.BlockSpec((tm, o_mla.shape[1]), tok),
            pl.BlockSpec(w_o.shape, const),
            pl.BlockSpec((1, D), const),
            pl.BlockSpec((1, D), const),
            pl.BlockSpec((1, M, D), memb),
            pl.BlockSpec((1, M, D), memb),
            pl.BlockSpec(wq.shape, const),
            pl.BlockSpec(wo.shape, const),
            pl.BlockSpec((1, D), const),
            pl.BlockSpec((1, D), const),
        ],
        out_specs=pl.BlockSpec((tm, D), tok),
        out_shape=jax.ShapeDtypeStruct((T, D), F32),
        compiler_params=_params("arbitrary"),
        name="mem_attn",
    )(xf, o_nsa, o_mla, w_o, g1, b1, k_mem, v_mem, wq, wo, g, b)


HALO = 16


FFN_SLAB = 256
FFN_TM = 1024


def _ffn_body(x_ref, xh_ref, wg_ref, wu_ref, cw_ref, cb_ref, wd_ref, g_ref, b_ref, o_ref,
              act_ref, *, alpha, seq_tiles):
    i = pl.program_id(0)
    x = x_ref[...]
    xb = x.astype(BF16)
    xe = jnp.concatenate([xh_ref[...].astype(BF16), xb], axis=0)
    tm = x.shape[0]
    row = lax.broadcasted_iota(jnp.int32, (tm, 1), 0)
    seq_start = i % seq_tiles == 0
    for c0 in range(0, wg_ref.shape[1], FFN_SLAB):
        cs = slice(c0, c0 + FFN_SLAB)
        gate_e = _dot(xe, wg_ref[:, cs])
        gate = gate_e[HALO:]
        up = _dot(xb, wu_ref[:, cs])
        halo = jnp.where(seq_start, 0.0, gate_e[:HALO])
        g1 = jnp.where(row == 0, halo[HALO - 1:HALO], pltpu.roll(gate, 1, 0))
        g2 = jnp.where(row == 0, halo[HALO - 2:HALO - 1],
                       jnp.where(row == 1, halo[HALO - 1:HALO], pltpu.roll(gate, 2, 0)))
        conv = cw_ref[0:1, cs] * g2 + cw_ref[1:2, cs] * g1 + cw_ref[2:3, cs] * gate + cb_ref[:, cs]
        act_ref[:, cs] = (conv * (1.0 / (1.0 + jnp.exp(-conv))) * up).astype(BF16)
    y = _dot(act_ref[...], wd_ref[...])
    o_ref[...] = _layer_norm(alpha * x + y, g_ref[...], b_ref[...])


def _ffn(xf, wg, wu, cw, cb, wd, g, b, alpha, S, tm):
    T, D = xf.shape
    dff = wg.shape[1]
    assert dff % FFN_SLAB == 0
    tok = lambda i: (i, 0)
    const = lambda i: (0, 0)
    return pl.pallas_call(
        functools.partial(_ffn_body, alpha=alpha, seq_tiles=S // tm),
        grid=(T // tm,),
        in_specs=[
            pl.BlockSpec((tm, D), tok),
            pl.BlockSpec((HALO, D), lambda i: (jnp.maximum(i * (tm // HALO) - 1, 0), 0)),
            pl.BlockSpec((D, dff), const, pipeline_mode=pl.Buffered(1)),
            pl.BlockSpec((D, dff), const, pipeline_mode=pl.Buffered(1)),
            pl.BlockSpec((CONV_WIDTH, dff), const),
            pl.BlockSpec((1, dff), const),
            pl.BlockSpec((dff, D), const, pipeline_mode=pl.Buffered(1)),
            pl.BlockSpec((1, D), const),
            pl.BlockSpec((1, D), const),
        ],
        out_specs=pl.BlockSpec((tm, D), tok),
        out_shape=jax.ShapeDtypeStruct((T, D), F32),
        scratch_shapes=[pltpu.VMEM((tm, dff), BF16)],
        compiler_params=_params("arbitrary"),
        name="ffn",
    )(xf, xf, wg, wu, cw, cb, wd, g, b)


def _inv_freq_row(dim, lane_lo, lane_hi, period):
    inv = ROPE_THETA ** (-np.arange(0, dim, 2, dtype=np.float64) / dim)
    row = np.zeros((1, LANES), np.float32)
    for lane in range(lane_lo, lane_hi):
        row[0, lane] = inv[(lane % period) % (dim // 2)]
    return jnp.asarray(row)


def _cover_t(S):
    nc = S // CMP_STRIDE
    ns = S // SLC_LEN
    cs = np.arange(nc)[:, None] * CMP_STRIDE
    ss = np.arange(ns)[None, :] * SLC_LEN
    cover = np.clip(np.minimum(cs + CMP_LEN, ss + SLC_LEN) - np.maximum(cs, ss), 0, None) / CMP_LEN
    cover[nc - 1:] = 0.0
    return jnp.asarray(cover.T, dtype=BF16)


def _permute_w_in(w):
    D = w.shape[0]
    c1 = NSA_HEADS * NSA_DH
    c2 = c1 + 3 * 2 * NSA_GROUPS * NSA_DH
    c3 = c2 + 3 * NSA_HEADS
    c4 = c3 + MLA_Q_RANK
    c5 = c4 + MLA_KV_RANK
    c6 = c5 + MLA_ROPE
    half = MLA_ROPE // 2
    z = lambda n: jnp.zeros((D, n), w.dtype)
    misc = jnp.concatenate(
        [w[:, c5 + half:c6], z(GATE_LANE0 - half), w[:, c2:c3], z(MLA_PE1 - GATE_LANE0 - (c3 - c2)),
         w[:, c5:c5 + half], z(LANES - MLA_PE1 - half)], axis=1)
    return jnp.concatenate([w[:, :c2], w[:, c3:c5], misc], axis=1).astype(BF16)


def _mla_head_lanes(nope, pe):
    r, H, _ = nope.shape
    half = MLA_ROPE // 2
    split = MLA_PE1 - half
    pad = jnp.zeros((r, H, LANES - MLA_NOPE - MLA_ROPE), nope.dtype)
    return jnp.concatenate([pe[..., half:], nope[..., :split], pe[..., :half], nope[..., split:], pad],
                           axis=2).reshape(r, H * LANES)


def _permute_w_uq(w):
    r = w.shape[0]
    w3 = w.reshape(r, MLA_HEADS, MLA_NOPE + MLA_ROPE)
    return _mla_head_lanes(w3[..., :MLA_NOPE], w3[..., MLA_NOPE:]).astype(BF16)


def _permute_w_ukv(w):
    r = w.shape[0]
    w3 = w.reshape(r, MLA_HEADS, MLA_NOPE + MLA_V)
    k = _mla_head_lanes(w3[..., :MLA_NOPE], jnp.zeros((r, MLA_HEADS, MLA_ROPE), w.dtype))
    v = _pad_lanes(w3[..., MLA_NOPE:]).reshape(r, MLA_HEADS // 2, 2, LANES)
    v = jnp.stack([v[:, :, 0], jnp.roll(v[:, :, 1], LANES - MLA_V, axis=-1)], axis=2)
    return jnp.concatenate([k, v.reshape(r, MLA_HEADS * LANES)], axis=1).astype(BF16)


def _permute_w_o(w):
    n = NSA_HEADS * NSA_DH
    wn = w[:n].reshape(NSA_GROUPS, NSA_REP, NSA_DH, -1).transpose(1, 0, 2, 3).reshape(n, -1)
    return jnp.concatenate([wn, w[n:]], axis=0).astype(BF16)


def _pad_lanes(a):
    return jnp.concatenate([a, jnp.zeros(a.shape[:-1] + (LANES - a.shape[-1],), a.dtype)], axis=-1)


def kernel(x, mem, positions, w_in, nsa_k_pos, nsa_ck_w1, nsa_ck_b1, nsa_ck_w2, nsa_ck_b2,
           nsa_v_pos, nsa_cv_w1, nsa_cv_b1, nsa_cv_w2, nsa_cv_b2,
           mla_q_norm, mla_w_uq, mla_kv_norm, mla_w_ukv, w_o, ln1_g, ln1_b,
           mem_wq, mem_wk, mem_wv, mem_wo, ln2_g, ln2_b,
           ffn_w_up, ffn_conv_w, ffn_conv_b, ffn_w_down, ln3_g, ln3_b):
    B, S, D = x.shape
    T = B * S
    depth = w_in.shape[0]
    alpha = (2.0 * depth) ** 0.25
    d_ff = ffn_w_down.shape[1]
    tm = min(512, S)
    assert S % MLA_TQ == 0 and S >= WIN_SPAN and S % tm == 0
    assert (B * mem.shape[1]) % 256 == 0

    pos = positions.reshape(T, 1)
    pos_cmp = positions[:, CMP_LEN - 1::CMP_STRIDE]
    pos_cmp = jnp.concatenate([pos_cmp, pos_cmp[:, -1:]], axis=1)[:, :, None]
    inv_cmp = _inv_freq_row(NSA_DH, 0, NSA_DH, NSA_DH)
    inv_tok = (_inv_freq_row(NSA_DH, 0, NSA_DH // 2, NSA_DH)
               + _inv_freq_row(MLA_ROPE, NSA_DH // 2, NSA_DH // 2 + MLA_ROPE // 2, MLA_ROPE // 2))
    cov_t = _cover_t(S)
    memf = mem.reshape(B * mem.shape[1], D)

    xf = x.reshape(T, D)
    for l in range(depth):
        qn, kvn, kvc, misc, q_m, k_m, v_m = _inproj(
            pos, inv_tok, xf, _permute_w_in(w_in[l]), mla_q_norm[l][None, :],
            mla_kv_norm[l][None, :], _permute_w_uq(mla_w_uq[l]), _permute_w_ukv(mla_w_ukv[l]),
            B, S, tm)
        kvcmp = _compress(
            pos_cmp, inv_cmp, kvc,
            *_compress_weights(nsa_k_pos[l], nsa_ck_w1[l], nsa_ck_b1[l], nsa_ck_w2[l], nsa_ck_b2[l],
                               nsa_v_pos[l], nsa_cv_w1[l], nsa_cv_b1[l], nsa_cv_w2[l], nsa_cv_b2[l]),
            B, S)
        o_nsa = _nsa(qn, kvn, kvcmp, misc, cov_t, B, S)
        o_mla = _mla(q_m, k_m, v_m, B, S)
        k_mem, v_mem = _mem_kv(memf, mem_wk[l].astype(BF16), mem_wv[l].astype(BF16), 256)
        xf = _mem_attn(xf, o_nsa.reshape(T, -1), o_mla.reshape(T, -1), _permute_w_o(w_o[l]),
                       ln1_g[l][None, :], ln1_b[l][None, :],
                       k_mem.reshape(B, -1, D), v_mem.reshape(B, -1, D),
                       mem_wq[l].astype(BF16), mem_wo[l].astype(BF16),
                       ln2_g[l][None, :], ln2_b[l][None, :], alpha, S, min(MEM_TM, S))
        xf = _ffn(xf, ffn_w_up[l][:, :d_ff].astype(BF16), ffn_w_up[l][:, d_ff:].astype(BF16),
                  ffn_conv_w[l], ffn_conv_b[l][None, :], ffn_w_down[l].astype(BF16),
                  ln3_g[l][None, :], ln3_b[l][None, :], alpha, S, FFN_TM)
    return xf.reshape(B, S, D)
```
